```python
import math
import jax
import jax.numpy as jnp
from jax import lax
import numpy as np

D_MODEL = 1024
BATCH = 16
SEQ = 4096
DEPTH = 2

EPS = 1e-6
N_BRANCH = 3
GDN_HEADS = 4
GDN_DK = 128
GDN_DV = 128
GDN_CONV = 4
GDN_CHUNK = 64
GDN_KEY_W = GDN_HEADS * GDN_DK
GDN_VAL_W = GDN_HEADS * GDN_DV
HGRN_HEADS = 4
HGRN_DK = 128
HGRN_DV = 128
HGRN_CHUNK = 16
HGRN_KEY_W = HGRN_HEADS * HGRN_DK
HGRN_VAL_W = HGRN_HEADS * HGRN_DV
SSD_HEADS = 8
SSD_HEAD_DIM = 64
SSD_GROUPS = 2
SSD_HEADS_PER_GROUP = SSD_HEADS // SSD_GROUPS
SSD_STATE = 128
SSD_CONV = 4
SSD_CHUNK = 64
SSD_INNER = SSD_HEADS * SSD_HEAD_DIM
SSD_XBC_W = SSD_INNER + 2 * SSD_GROUPS * SSD_STATE
FFN_HIDDEN = 2816
FFN_CONV = 3

SPLIT_SIZES = (
    2 * GDN_KEY_W + GDN_VAL_W,
    GDN_HEADS,
    GDN_HEADS,
    GDN_VAL_W,
    HGRN_KEY_W,
    HGRN_KEY_W,
    HGRN_VAL_W,
    HGRN_VAL_W,
    SSD_INNER,
    SSD_XBC_W,
    SSD_HEADS,
    N_BRANCH * D_MODEL,
)
N_IN = sum(SPLIT_SIZES)

kernel_name = "hybrid_gdn_hgrn2_ssd_adaln_block"


def _f32(t):
    return t.astype(jnp.float32)


def rms_norm(x, w):
    xf = x.astype(jnp.float32)
    y = xf * lax.rsqrt(jnp.mean(xf * xf, axis=-1, keepdims=True) + EPS)
    return (y * w.astype(jnp.float32)).astype(x.dtype)


def l2_normalize(x):
    return x * lax.rsqrt(jnp.sum(x * x, axis=-1, keepdims=True) + EPS)


def modulate(h, shift, scale):
    return h * (1.0 + scale[:, None, :]) + shift[:, None, :]


def causal_dwconv(x, w, b=None):
    k_width = w.shape[0]
    s = x.shape[1]
    xp = jnp.pad(x, ((0, 0), (k_width - 1, 0), (0, 0)))
    y = w[k_width - 1] * x
    for k in range(k_width - 1):
        y = y + w[k] * xp[:, k:k + s]
    if b is not None:
        y = y + b
    return y


def to_chunks(t, c):
    b, s = t.shape[:2]
    t = t.reshape((b, s // c, c) + t.shape[2:])
    t = jnp.swapaxes(t, 0, 1)
    return jnp.swapaxes(t, 2, 3)


def from_chunks(t):
    t = jnp.swapaxes(jnp.swapaxes(t, 2, 3), 0, 1)
    return t.reshape((t.shape[0], t.shape[1] * t.shape[2]) + t.shape[3:])


def gated_delta_chunked(q, k, v, g, beta):
    b, s, h, dk = q.shape
    dv = v.shape[-1]
    c = GDN_CHUNK
    qc = to_chunks(q * (dk ** -0.5), c)
    kc = to_chunks(k, c)
    vc = to_chunks(v, c)
    bc = to_chunks(beta, c)
    big_g = jnp.cumsum(to_chunks(g, c), axis=-1)
    incl = jnp.tril(jnp.ones((c, c), bool))
    strict = jnp.tril(jnp.ones((c, c), bool), -1)
    diff = big_g[..., :, None] - big_g[..., None, :]
    decay = jnp.where(incl, jnp.exp(jnp.where(incl, diff, 0.0)), 0.0)
    kb = kc * bc[..., None]
    m = jnp.where(strict, jnp.einsum('nbhlk,nbhsk->nbhls', kb, kc) * decay, 0.0)
    a = m + jnp.eye(c, dtype=m.dtype)
    rhs = jnp.concatenate([vc * bc[..., None], kb * jnp.exp(big_g)[..., None]], axis=-1)
    sol = lax.linalg.triangular_solve(a, rhs, left_side=True, lower=True, unit_diagonal=True)
    u, w = sol[..., :dv], sol[..., dv:]
    attn = jnp.einsum('nbhlk,nbhsk->nbhls', qc, kc) * decay
    qg = qc * jnp.exp(big_g)[..., None]
    k_end = kc * jnp.exp(big_g[..., -1:] - big_g)[..., None]
    g_end = jnp.exp(big_g[..., -1])

    def step(state, inp):
        qg_i, k_end_i, u_i, w_i, attn_i, g_end_i = inp
        v_new = u_i - jnp.einsum('bhlk,bhkv->bhlv', w_i, state)
        o = (jnp.einsum('bhlk,bhkv->bhlv', qg_i, state)
             + jnp.einsum('bhls,bhsv->bhlv', attn_i, v_new))
        state = state * g_end_i[..., None, None] + jnp.einsum('bhsk,bhsv->bhkv', k_end_i, v_new)
        return state, o

    state0 = jnp.zeros((b, h, dk, dv), q.dtype)
    _, o = lax.scan(step, state0, (qg, k_end, u, w, attn, g_end))
    return from_chunks(o)


def hgrn2_chunked(q, k, v, logf):
    b, s, h, dk = q.shape
    dv = v.shape[-1]
    c = HGRN_CHUNK
    qc, kc, vc = to_chunks(q, c), to_chunks(k, c), to_chunks(v, c)
    big_g = jnp.cumsum(to_chunks(logf, c), axis=-2)
    g_ref = big_g[..., c // 2 - 1:c // 2, :]
    incl = jnp.tril(jnp.ones((c, c), bool))
    scores = jnp.einsum('nbhlk,nbhsk->nbhls', qc * jnp.exp(big_g - g_ref), kc * jnp.exp(g_ref - big_g))
    attn = jnp.where(incl, scores, 0.0)
    o_intra = jnp.einsum('nbhls,nbhsv->nbhlv', attn, vc)
    qg = qc * jnp.exp(big_g)
    k_end = kc * jnp.exp(big_g[..., -1:, :] - big_g)
    g_end = jnp.exp(big_g[..., -1, :])

    def step(state, inp):
        qg_i, k_end_i, v_i, g_end_i = inp
        o = jnp.einsum('bhlk,bhkv->bhlv', qg_i, state)
        state = state * g_end_i[..., None] + jnp.einsum('bhsk,bhsv->bhkv', k_end_i, v_i)
        return state, o

    state0 = jnp.zeros((b, h, dk, dv), q.dtype)
    _, o_inter = lax.scan(step, state0, (qg, k_end, vc, g_end))
    return from_chunks(o_intra + o_inter)


def ssd_chunked(xs, da, bm, cm):
    b, s, g, hg, p = xs.shape
    n_state = bm.shape[-1]
    c = SSD_CHUNK
    n = s // c
    xc = jnp.swapaxes(xs.reshape(b, n, c, g, hg, p), 0, 1)
    bc = jnp.swapaxes(bm.reshape(b, n, c, g, n_state), 0, 1)
    cc = jnp.swapaxes(cm.reshape(b, n, c, g, n_state), 0, 1)
    acs = jnp.cumsum(jnp.swapaxes(da.reshape(b, n, c, g, hg), 0, 1), axis=2)
    incl = jnp.tril(jnp.ones((c, c), bool))[:, :, None, None]
    diff = acs[:, :, :, None] - acs[:, :, None, :]
    seg = jnp.where(incl, jnp.exp(jnp.where(incl, diff, 0.0)), 0.0)
    cb = jnp.einsum('nblgd,nbsgd->nblsg', cc, bc)
    y_diag = jnp.einsum('nblsg,nblsgh,nbsghp->nblghp', cb, seg, xc)

    def step(state, inp):
        x_i, b_i, c_i, acs_i = inp
        y_off = jnp.einsum('blgd,bghpd,blgh->blghp', c_i, state, jnp.exp(acs_i))
        last = acs_i[:, -1]
        state = (state * jnp.exp(last)[..., None, None]
                 + jnp.einsum('bsgd,bsgh,bsghp->bghpd', b_i, jnp.exp(last[:, None] - acs_i), x_i))
        return state, y_off

    state0 = jnp.zeros((b, g, hg, p, n_state), xs.dtype)
    _, y_off = lax.scan(step, state0, (xc, bc, cc, acs))
    return jnp.swapaxes(y_diag + y_off, 0, 1).reshape(b, s, g, hg, p)


def gdn_branch(qkv_raw, a_raw, b_raw, z_raw, conv_w, a_log, dt_bias, norm_w):
    bsz, s, _ = qkv_raw.shape
    qkv = jax.nn.silu(causal_dwconv(qkv_raw, conv_w))
    q, k, v = jnp.split(qkv, [GDN_KEY_W, 2 * GDN_KEY_W], axis=-1)
    q = l2_normalize(q.reshape(bsz, s, GDN_HEADS, GDN_DK))
    k = l2_normalize(k.reshape(bsz, s, GDN_HEADS, GDN_DK))
    v = v.reshape(bsz, s, GDN_HEADS, GDN_DV)
    beta = jax.nn.sigmoid(b_raw)
    g = -jnp.exp(a_log) * jax.nn.softplus(a_raw + dt_bias)
    o = gated_delta_chunked(q, k, v, g, beta)
    o = rms_norm(o, norm_w) * jax.nn.silu(z_raw.reshape(bsz, s, GDN_HEADS, GDN_DV))
    return o.reshape(bsz, s, GDN_VAL_W)


def hgrn2_branch(q_raw, f_raw, i_raw, g_raw, lb, norm_w):
    bsz, s, _ = q_raw.shape
    shp = (bsz, s, HGRN_HEADS, HGRN_DK)
    q = jax.nn.silu(q_raw).reshape(shp)
    logf = jnp.log(lb + (1.0 - lb) * jax.nn.sigmoid(f_raw)).reshape(shp)
    k = ((1.0 - lb) * jax.nn.sigmoid(-f_raw)).reshape(shp)
    v = i_raw.reshape(bsz, s, HGRN_HEADS, HGRN_DV)
    o = hgrn2_chunked(q, k, v, logf)
    o = rms_norm(o, norm_w) * jax.nn.silu(g_raw.reshape(bsz, s, HGRN_HEADS, HGRN_DV))
    return o.reshape(bsz, s, HGRN_VAL_W)


def ssd_branch(z_raw, xbc_raw, dt_raw, conv_w, conv_b, a_log, dt_bias, d_skip, norm_w):
    bsz, s, _ = xbc_raw.shape
    xbc = jax.nn.silu(causal_dwconv(xbc_raw, conv_w, conv_b))
    xs, bm, cm = jnp.split(xbc, [SSD_INNER, SSD_INNER + SSD_GROUPS * SSD_STATE], axis=-1)
    xs = xs.reshape(bsz, s, SSD_GROUPS, SSD_HEADS_PER_GROUP, SSD_HEAD_DIM)
    bm = bm.reshape(bsz, s, SSD_GROUPS, SSD_STATE)
    cm = cm.reshape(bsz, s, SSD_GROUPS, SSD_STATE)
    dt = jax.nn.softplus(dt_raw + dt_bias).reshape(bsz, s, SSD_GROUPS, SSD_HEADS_PER_GROUP)
    a = -jnp.exp(a_log).reshape(SSD_GROUPS, SSD_HEADS_PER_GROUP)
    y = ssd_chunked(xs * dt[..., None], dt * a, bm, cm)
    y = y + d_skip.reshape(SSD_GROUPS, SSD_HEADS_PER_GROUP)[..., None] * xs
    group_w = SSD_HEADS_PER_GROUP * SSD_HEAD_DIM
    y = y.reshape(bsz, s, SSD_GROUPS, group_w)
    z = z_raw.reshape(bsz, s, SSD_GROUPS, group_w)
    y = rms_norm(y * jax.nn.silu(z), norm_w.reshape(SSD_GROUPS, group_w))
    return y.reshape(bsz, s, SSD_INNER)


def token_mixing(h, lb, w_in, gdn_conv_w, gdn_a_log, gdn_dt_bias, gdn_norm_w, hgrn_norm_w,
                 ssd_conv_w, ssd_conv_b, ssd_a_log, ssd_dt_bias, ssd_d, ssd_norm_w,
                 w_br_a, w_br_b, w_br_c, w_out):
    bsz, s, _ = h.shape
    dtype = h.dtype
    split_at = [int(i) for i in np.cumsum(SPLIT_SIZES)[:-1]]
    parts = jnp.split(h @ w_in, split_at, axis=-1)
    (gdn_qkv, gdn_a, gdn_b, gdn_z, hg_q, hg_f, hg_i, hg_g,
     ssd_z, ssd_xbc, ssd_dt, gate_raw) = parts
    o_a = gdn_branch(_f32(gdn_qkv), _f32(gdn_a), _f32(gdn_b), _f32(gdn_z),
                     _f32(gdn_conv_w), _f32(gdn_a_log), _f32(gdn_dt_bias), _f32(gdn_norm_w))
    o_b = hgrn2_branch(_f32(hg_q), _f32(hg_f), _f32(hg_i), _f32(hg_g), lb, _f32(hgrn_norm_w))
    o_c = ssd_branch(_f32(ssd_z), _f32(ssd_xbc), _f32(ssd_dt), _f32(ssd_conv_w), _f32(ssd_conv_b),
                     _f32(ssd_a_log), _f32(ssd_dt_bias), _f32(ssd_d), _f32(ssd_norm_w))
    gates = jax.nn.sigmoid(gate_raw).reshape(bsz, s, N_BRANCH, D_MODEL)
    merged = (gates[:, :, 0] * (o_a.astype(dtype) @ w_br_a)
              + gates[:, :, 1] * (o_b.astype(dtype) @ w_br_b)
              + gates[:, :, 2] * (o_c.astype(dtype) @ w_br_c))
    return merged @ w_out


def conv_glu_ffn(h, w_up, conv_w, conv_b, w_down):
    u = causal_dwconv(h @ w_up, conv_w, conv_b)
    gate, val = jnp.split(u, 2, axis=-1)
    return (jax.nn.silu(gate) * val) @ w_down


def _log_uniform_dt_bias(key, shape):
    lo, hi = math.log(1e-3), math.log(1e-1)
    dt = jnp.exp(jax.random.uniform(key, shape) * (hi - lo) + lo)
    return dt + jnp.log(-jnp.expm1(-dt))


def _fwd_setup_inputs(seed: int = 0) -> dict:
    key = jax.random.key(seed)
    ks = jax.random.split(key, 32)
    nrm = jax.random.normal
    d = D_MODEL
    f2 = 2 * FFN_HIDDEN
    gdn_qkv_w = 2 * GDN_KEY_W + GDN_VAL_W
    return {
        "x": nrm(ks[0], (BATCH, SEQ, d), jnp.float32),
        "c": nrm(ks[1], (BATCH, d), jnp.float32),
        "w_ada": nrm(ks[2], (DEPTH, d, 6 * d)) * (0.5 * d ** -0.5),
        "b_ada": 0.02 * nrm(ks[3], (DEPTH, 6 * d)),
        "norm1_w": 1.0 + 0.05 * nrm(ks[4], (DEPTH, d)),
        "w_in": nrm(ks[5], (DEPTH, d, N_IN)) * d ** -0.5,
        "gdn_conv_w": nrm(ks[6], (DEPTH, GDN_CONV, gdn_qkv_w)) * GDN_CONV ** -0.5,
        "gdn_a_log": jnp.log(jax.random.uniform(ks[7], (DEPTH, GDN_HEADS), minval=1.0, maxval=16.0)),
        "gdn_dt_bias": _log_uniform_dt_bias(ks[8], (DEPTH, GDN_HEADS)),
        "gdn_norm_w": 1.0 + 0.05 * nrm(ks[9], (DEPTH, GDN_DV)),
        "hgrn_lb_param": nrm(ks[10], (DEPTH, HGRN_KEY_W)),
        "hgrn_norm_w": 1.0 + 0.05 * nrm(ks[11], (DEPTH, HGRN_DV)),
        "ssd_conv_w": nrm(ks[12], (DEPTH, SSD_CONV, SSD_XBC_W)) * SSD_CONV ** -0.5,
        "ssd_conv_b": 0.02 * nrm(ks[13], (DEPTH, SSD_XBC_W)),
        "ssd_a_log": jnp.log(jax.random.uniform(ks[14], (DEPTH, SSD_HEADS), minval=1.0, maxval=16.0)),
        "ssd_dt_bias": _log_uniform_dt_bias(ks[15], (DEPTH, SSD_HEADS)),
        "ssd_d": 1.0 + 0.05 * nrm(ks[16], (DEPTH, SSD_HEADS)),
        "ssd_norm_w": 1.0 + 0.05 * nrm(ks[17], (DEPTH, SSD_INNER)),
        "w_br_a": nrm(ks[18], (DEPTH, GDN_VAL_W, d)) * GDN_VAL_W ** -0.5,
        "w_br_b": nrm(ks[19], (DEPTH, HGRN_VAL_W, d)) * HGRN_VAL_W ** -0.5,
        "w_br_c": nrm(ks[20], (DEPTH, SSD_INNER, d)) * SSD_INNER ** -0.5,
        "w_out": nrm(ks[21], (DEPTH, d, d)) * d ** -0.5,
        "norm2_w": 1.0 + 0.05 * nrm(ks[22], (DEPTH, d)),
        "ffn_w_up": nrm(ks[23], (DEPTH, d, f2)) * d ** -0.5,
        "ffn_conv_w": nrm(ks[24], (DEPTH, FFN_CONV, f2)) * FFN_CONV ** -0.5,
        "ffn_conv_b": 0.02 * nrm(ks[25], (DEPTH, f2)),
        "ffn_w_down": nrm(ks[26], (DEPTH, FFN_HIDDEN, d)) * FFN_HIDDEN ** -0.5,
        "final_norm_w": 1.0 + 0.05 * nrm(ks[27], (d,)),
    }


def _fwd_reference(x, c, w_ada, b_ada, norm1_w, w_in, gdn_conv_w, gdn_a_log, gdn_dt_bias, gdn_norm_w,
              hgrn_lb_param, hgrn_norm_w, ssd_conv_w, ssd_conv_b, ssd_a_log, ssd_dt_bias, ssd_d,
              ssd_norm_w, w_br_a, w_br_b, w_br_c, w_out, norm2_w, ffn_w_up, ffn_conv_w, ffn_conv_b,
              ffn_w_down, final_norm_w):
    c_act = jax.nn.silu(c)
    lb_soft = jax.nn.softmax(hgrn_lb_param.astype(jnp.float32), axis=0)
    lower_bounds = jnp.cumsum(lb_soft, axis=0) - lb_soft[0]
    for l in range(DEPTH):
        mod = c_act @ w_ada[l] + b_ada[l]
        shift1, scale1, gate1, shift2, scale2, gate2 = jnp.split(mod, 6, axis=-1)
        h = modulate(rms_norm(x, norm1_w[l]), shift1, scale1)
        mix = token_mixing(h, lower_bounds[l], w_in[l], gdn_conv_w[l], gdn_a_log[l], gdn_dt_bias[l],
                           gdn_norm_w[l], hgrn_norm_w[l], ssd_conv_w[l], ssd_conv_b[l], ssd_a_log[l],
                           ssd_dt_bias[l], ssd_d[l], ssd_norm_w[l], w_br_a[l], w_br_b[l], w_br_c[l],
                           w_out[l])
        x = x + gate1[:, None, :] * mix
        h = modulate(rms_norm(x, norm2_w[l]), shift2, scale2)
        x = x + gate2[:, None, :] * conv_glu_ffn(h, ffn_w_up[l], ffn_conv_w[l], ffn_conv_b[l], ffn_w_down[l])
    return rms_norm(x, final_norm_w)


import jax as _jax
import jax.numpy as _jnp

TWIN_FORMAT = 'train_step'
FWD_PARAMS = ['x', 'c', 'w_ada', 'b_ada', 'norm1_w', 'w_in', 'gdn_conv_w', 'gdn_a_log', 'gdn_dt_bias', 'gdn_norm_w', 'hgrn_lb_param', 'hgrn_norm_w', 'ssd_conv_w', 'ssd_conv_b', 'ssd_a_log', 'ssd_dt_bias', 'ssd_d', 'ssd_norm_w', 'w_br_a', 'w_br_b', 'w_br_c', 'w_out', 'norm2_w', 'ffn_w_up', 'ffn_conv_w', 'ffn_conv_b', 'ffn_w_down', 'final_norm_w']
TWIN_WEIGHTS = ['w_ada', 'b_ada', 'norm1_w', 'w_in', 'gdn_conv_w', 'gdn_a_log', 'gdn_dt_bias', 'gdn_norm_w', 'hgrn_lb_param', 'hgrn_norm_w', 'ssd_conv_w', 'ssd_conv_b', 'ssd_a_log', 'ssd_dt_bias', 'ssd_d', 'ssd_norm_w', 'w_br_a', 'w_br_b', 'w_br_c', 'w_out', 'norm2_w', 'ffn_w_up', 'ffn_conv_w', 'ffn_conv_b', 'ffn_w_down', 'final_norm_w']
TWIN_DIFF_INPUT = 'x'
TWIN_INPUTS = ['x', 'c', 'w_ada', 'b_ada', 'norm1_w', 'w_in', 'gdn_conv_w', 'gdn_a_log', 'gdn_dt_bias', 'gdn_norm_w', 'hgrn_lb_param', 'hgrn_norm_w', 'ssd_conv_w', 'ssd_conv_b', 'ssd_a_log', 'ssd_dt_bias', 'ssd_d', 'ssd_norm_w', 'w_br_a', 'w_br_b', 'w_br_c', 'w_out', 'norm2_w', 'ffn_w_up', 'ffn_conv_w', 'ffn_conv_b', 'ffn_w_down', 'final_norm_w', 'loss_target', 'm_w_ada', 'm_b_ada', 'm_norm1_w', 'm_w_in', 'm_gdn_conv_w', 'm_gdn_a_log', 'm_gdn_dt_bias', 'm_gdn_norm_w', 'm_hgrn_lb_param', 'm_hgrn_norm_w', 'm_ssd_conv_w', 'm_ssd_conv_b', 'm_ssd_a_log', 'm_ssd_dt_bias', 'm_ssd_d', 'm_ssd_norm_w', 'm_w_br_a', 'm_w_br_b', 'm_w_br_c', 'm_w_out', 'm_norm2_w', 'm_ffn_w_up', 'm_ffn_conv_w', 'm_ffn_conv_b', 'm_ffn_w_down', 'm_final_norm_w', 'v_w_ada', 'v_b_ada', 'v_norm1_w', 'v_w_in', 'v_gdn_conv_w', 'v_gdn_a_log', 'v_gdn_dt_bias', 'v_gdn_norm_w', 'v_hgrn_lb_param', 'v_hgrn_norm_w', 'v_ssd_conv_w', 'v_ssd_conv_b', 'v_ssd_a_log', 'v_ssd_dt_bias', 'v_ssd_d', 'v_ssd_norm_w', 'v_w_br_a', 'v_w_br_b', 'v_w_br_c', 'v_w_out', 'v_norm2_w', 'v_ffn_w_up', 'v_ffn_conv_w', 'v_ffn_conv_b', 'v_ffn_w_down', 'v_final_norm_w']
TWIN_OUTPUTS = ['loss', 'grad_x', 'grad_w_ada', 'grad_b_ada', 'grad_norm1_w', 'grad_w_in', 'grad_gdn_conv_w', 'grad_gdn_a_log', 'grad_gdn_dt_bias', 'grad_gdn_norm_w', 'grad_hgrn_lb_param', 'grad_hgrn_norm_w', 'grad_ssd_conv_w', 'grad_ssd_conv_b', 'grad_ssd_a_log', 'grad_ssd_dt_bias', 'grad_ssd_d', 'grad_ssd_norm_w', 'grad_w_br_a', 'grad_w_br_b', 'grad_w_br_c', 'grad_w_out', 'grad_norm2_w', 'grad_ffn_w_up', 'grad_ffn_conv_w', 'grad_ffn_conv_b', 'grad_ffn_w_down', 'grad_final_norm_w', 'delta_w_ada', 'delta_b_ada', 'delta_norm1_w', 'delta_w_in', 'delta_gdn_conv_w', 'delta_gdn_a_log', 'delta_gdn_dt_bias', 'delta_gdn_norm_w', 'delta_hgrn_lb_param', 'delta_hgrn_norm_w', 'delta_ssd_conv_w', 'delta_ssd_conv_b', 'delta_ssd_a_log', 'delta_ssd_dt_bias', 'delta_ssd_d', 'delta_ssd_norm_w', 'delta_w_br_a', 'delta_w_br_b', 'delta_w_br_c', 'delta_w_out', 'delta_norm2_w', 'delta_ffn_w_up', 'delta_ffn_conv_w', 'delta_ffn_conv_b', 'delta_ffn_w_down', 'delta_final_norm_w', 'new_m_w_ada', 'new_m_b_ada', 'new_m_norm1_w', 'new_m_w_in', 'new_m_gdn_conv_w', 'new_m_gdn_a_log', 'new_m_gdn_dt_bias', 'new_m_gdn_norm_w', 'new_m_hgrn_lb_param', 'new_m_hgrn_norm_w', 'new_m_ssd_conv_w', 'new_m_ssd_conv_b', 'new_m_ssd_a_log', 'new_m_ssd_dt_bias', 'new_m_ssd_d', 'new_m_ssd_norm_w', 'new_m_w_br_a', 'new_m_w_br_b', 'new_m_w_br_c', 'new_m_w_out', 'new_m_norm2_w', 'new_m_ffn_w_up', 'new_m_ffn_conv_w', 'new_m_ffn_conv_b', 'new_m_ffn_w_down', 'new_m_final_norm_w', 'new_v_w_ada', 'new_v_b_ada', 'new_v_norm1_w', 'new_v_w_in', 'new_v_gdn_conv_w', 'new_v_gdn_a_log', 'new_v_gdn_dt_bias', 'new_v_gdn_norm_w', 'new_v_hgrn_lb_param', 'new_v_hgrn_norm_w', 'new_v_ssd_conv_w', 'new_v_ssd_conv_b', 'new_v_ssd_a_log', 'new_v_ssd_dt_bias', 'new_v_ssd_d', 'new_v_ssd_norm_w', 'new_v_w_br_a', 'new_v_w_br_b', 'new_v_w_br_c', 'new_v_w_out', 'new_v_norm2_w', 'new_v_ffn_w_up', 'new_v_ffn_conv_w', 'new_v_ffn_conv_b', 'new_v_ffn_w_down', 'new_v_final_norm_w']
TWIN_LEAF_KINDS = {'loss': 'loss', 'grad_x': 'grad_x', 'grad_w_ada': 'grad_w', 'grad_b_ada': 'grad_w', 'grad_norm1_w': 'grad_w', 'grad_w_in': 'grad_w', 'grad_gdn_conv_w': 'grad_w', 'grad_gdn_a_log': 'grad_w', 'grad_gdn_dt_bias': 'grad_w', 'grad_gdn_norm_w': 'grad_w', 'grad_hgrn_lb_param': 'grad_w', 'grad_hgrn_norm_w': 'grad_w', 'grad_ssd_conv_w': 'grad_w', 'grad_ssd_conv_b': 'grad_w', 'grad_ssd_a_log': 'grad_w', 'grad_ssd_dt_bias': 'grad_w', 'grad_ssd_d': 'grad_w', 'grad_ssd_norm_w': 'grad_w', 'grad_w_br_a': 'grad_w', 'grad_w_br_b': 'grad_w', 'grad_w_br_c': 'grad_w', 'grad_w_out': 'grad_w', 'grad_norm2_w': 'grad_w', 'grad_ffn_w_up': 'grad_w', 'grad_ffn_conv_w': 'grad_w', 'grad_ffn_conv_b': 'grad_w', 'grad_ffn_w_down': 'grad_w', 'grad_final_norm_w': 'grad_w', 'delta_w_ada': 'delta_w', 'delta_b_ada': 'delta_w', 'delta_norm1_w': 'delta_w', 'delta_w_in': 'delta_w', 'delta_gdn_conv_w': 'delta_w', 'delta_gdn_a_log': 'delta_w', 'delta_gdn_dt_bias': 'delta_w', 'delta_gdn_norm_w': 'delta_w', 'delta_hgrn_lb_param': 'delta_w', 'delta_hgrn_norm_w': 'delta_w', 'delta_ssd_conv_w': 'delta_w', 'delta_ssd_conv_b': 'delta_w', 'delta_ssd_a_log': 'delta_w', 'delta_ssd_dt_bias': 'delta_w', 'delta_ssd_d': 'delta_w', 'delta_ssd_norm_w': 'delta_w', 'delta_w_br_a': 'delta_w', 'delta_w_br_b': 'delta_w', 'delta_w_br_c': 'delta_w', 'delta_w_out': 'delta_w', 'delta_norm2_w': 'delta_w', 'delta_ffn_w_up': 'delta_w', 'delta_ffn_conv_w': 'delta_w', 'delta_ffn_conv_b': 'delta_w', 'delta_ffn_w_down': 'delta_w', 'delta_final_norm_w': 'delta_w', 'new_m_w_ada': 'new_m', 'new_m_b_ada': 'new_m', 'new_m_norm1_w': 'new_m', 'new_m_w_in': 'new_m', 'new_m_gdn_conv_w': 'new_m', 'new_m_gdn_a_log': 'new_m', 'new_m_gdn_dt_bias': 'new_m', 'new_m_gdn_norm_w': 'new_m', 'new_m_hgrn_lb_param': 'new_m', 'new_m_hgrn_norm_w': 'new_m', 'new_m_ssd_conv_w': 'new_m', 'new_m_ssd_conv_b': 'new_m', 'new_m_ssd_a_log': 'new_m', 'new_m_ssd_dt_bias': 'new_m', 'new_m_ssd_d': 'new_m', 'new_m_ssd_norm_w': 'new_m', 'new_m_w_br_a': 'new_m', 'new_m_w_br_b': 'new_m', 'new_m_w_br_c': 'new_m', 'new_m_w_out': 'new_m', 'new_m_norm2_w': 'new_m', 'new_m_ffn_w_up': 'new_m', 'new_m_ffn_conv_w': 'new_m', 'new_m_ffn_conv_b': 'new_m', 'new_m_ffn_w_down': 'new_m', 'new_m_final_norm_w': 'new_m', 'new_v_w_ada': 'new_v', 'new_v_b_ada': 'new_v', 'new_v_norm1_w': 'new_v', 'new_v_w_in': 'new_v', 'new_v_gdn_conv_w': 'new_v', 'new_v_gdn_a_log': 'new_v', 'new_v_gdn_dt_bias': 'new_v', 'new_v_gdn_norm_w': 'new_v', 'new_v_hgrn_lb_param': 'new_v', 'new_v_hgrn_norm_w': 'new_v', 'new_v_ssd_conv_w': 'new_v', 'new_v_ssd_conv_b': 'new_v', 'new_v_ssd_a_log': 'new_v', 'new_v_ssd_dt_bias': 'new_v', 'new_v_ssd_d': 'new_v', 'new_v_ssd_norm_w': 'new_v', 'new_v_w_br_a': 'new_v', 'new_v_w_br_b': 'new_v', 'new_v_w_br_c': 'new_v', 'new_v_w_out': 'new_v', 'new_v_norm2_w': 'new_v', 'new_v_ffn_w_up': 'new_v', 'new_v_ffn_conv_w': 'new_v', 'new_v_ffn_conv_b': 'new_v', 'new_v_ffn_w_down': 'new_v', 'new_v_final_norm_w': 'new_v'}


def _forward(args):
    return _fwd_reference(*[args[k] for k in FWD_PARAMS])


def _output_shape():
    out = _jax.eval_shape(lambda: _forward(_fwd_setup_inputs(0)))
    return out.shape, out.dtype

N_MICROBATCH = 1
ADAM_LR = 0.001
ADAM_B1 = 0.9
ADAM_B2 = 0.999
ADAM_EPS = 1e-08
ADAM_WD = 0.01
ADAM_STEP = 10
PER_EXAMPLE_BATCH_AXIS = {'x': 0, 'c': 0, 'loss_target': 0}
SHARED_INPUTS = []
_WEIGHT_DTYPES = {'w_ada': _jnp.float32, 'b_ada': _jnp.float32, 'norm1_w': _jnp.float32, 'w_in': _jnp.float32, 'gdn_conv_w': _jnp.float32, 'gdn_a_log': _jnp.float32, 'gdn_dt_bias': _jnp.float32, 'gdn_norm_w': _jnp.float32, 'hgrn_lb_param': _jnp.float32, 'hgrn_norm_w': _jnp.float32, 'ssd_conv_w': _jnp.float32, 'ssd_conv_b': _jnp.float32, 'ssd_a_log': _jnp.float32, 'ssd_dt_bias': _jnp.float32, 'ssd_d': _jnp.float32, 'ssd_norm_w': _jnp.float32, 'w_br_a': _jnp.float32, 'w_br_b': _jnp.float32, 'w_br_c': _jnp.float32, 'w_out': _jnp.float32, 'norm2_w': _jnp.float32, 'ffn_w_up': _jnp.float32, 'ffn_conv_w': _jnp.float32, 'ffn_conv_b': _jnp.float32, 'ffn_w_down': _jnp.float32, 'final_norm_w': _jnp.float32}
MOMENT_SCALE = {'w_ada': 1.045307e-01, 'b_ada': 1.833872e-01, 'norm1_w': 8.600961e-02, 'w_in': 3.175894e-02, 'gdn_conv_w': 3.078100e-02, 'gdn_a_log': 2.538571e-01, 'gdn_dt_bias': 2.612085e-01, 'gdn_norm_w': 1.116074e-01, 'hgrn_lb_param': 2.712372e-03, 'hgrn_norm_w': 7.889728e-02, 'ssd_conv_w': 4.434964e-02, 'ssd_conv_b': 6.261457e-02, 'ssd_a_log': 1.703488e-01, 'ssd_dt_bias': 1.496816e-01, 'ssd_d': 3.166379e-01, 'ssd_norm_w': 6.385626e-02, 'w_br_a': 2.863717e-02, 'w_br_b': 2.846999e-02, 'w_br_c': 4.329609e-02, 'w_out': 5.903486e-02, 'norm2_w': 8.178922e-02, 'ffn_w_up': 3.368059e-02, 'ffn_conv_w': 3.374820e-02, 'ffn_conv_b': 3.121380e-02, 'ffn_w_down': 5.499465e-02, 'final_norm_w': 6.402499e+01}


def _to_microbatches(a, axis):
    t = _jnp.moveaxis(a, axis, 0)
    t = t.reshape((N_MICROBATCH, t.shape[0] // N_MICROBATCH) + t.shape[1:])
    return _jnp.moveaxis(t, 1, axis + 1)


def setup_inputs(seed: int = 0) -> dict:
    inp = _fwd_setup_inputs(seed)
    key = _jax.random.fold_in(_jax.random.key(seed), 7919)
    shape, _ = _output_shape()
    out = dict(inp)
    out["loss_target"] = _jax.random.normal(_jax.random.fold_in(key, 0), shape, _jnp.float32)
    for i, name in enumerate(TWIN_WEIGHTS):
        w = inp[name].astype(_jnp.float32)
        if MOMENT_SCALE is None:
            s = _jnp.sqrt(_jnp.mean(_jnp.square(w)) + 1e-30)
        else:
            s = MOMENT_SCALE[name]
        km, kv = _jax.random.split(_jax.random.fold_in(key, i + 1))
        out[name] = w
        out["m_" + name] = s * _jax.random.normal(km, w.shape, _jnp.float32)
        out["v_" + name] = (s * s) * _jax.random.uniform(kv, w.shape, _jnp.float32, 0.5, 1.5)
    if N_MICROBATCH > 1:
        for name, axis in PER_EXAMPLE_BATCH_AXIS.items():
            out[name] = _to_microbatches(out[name], axis)
    return {'x': out['x'], 'c': out['c'], 'w_ada': out['w_ada'], 'b_ada': out['b_ada'], 'norm1_w': out['norm1_w'], 'w_in': out['w_in'], 'gdn_conv_w': out['gdn_conv_w'], 'gdn_a_log': out['gdn_a_log'], 'gdn_dt_bias': out['gdn_dt_bias'], 'gdn_norm_w': out['gdn_norm_w'], 'hgrn_lb_param': out['hgrn_lb_param'], 'hgrn_norm_w': out['hgrn_norm_w'], 'ssd_conv_w': out['ssd_conv_w'], 'ssd_conv_b': out['ssd_conv_b'], 'ssd_a_log': out['ssd_a_log'], 'ssd_dt_bias': out['ssd_dt_bias'], 'ssd_d': out['ssd_d'], 'ssd_norm_w': out['ssd_norm_w'], 'w_br_a': out['w_br_a'], 'w_br_b': out['w_br_b'], 'w_br_c': out['w_br_c'], 'w_out': out['w_out'], 'norm2_w': out['norm2_w'], 'ffn_w_up': out['ffn_w_up'], 'ffn_conv_w': out['ffn_conv_w'], 'ffn_conv_b': out['ffn_conv_b'], 'ffn_w_down': out['ffn_w_down'], 'final_norm_w': out['final_norm_w'], 'loss_target': out['loss_target'], 'm_w_ada': out['m_w_ada'], 'm_b_ada': out['m_b_ada'], 'm_norm1_w': out['m_norm1_w'], 'm_w_in': out['m_w_in'], 'm_gdn_conv_w': out['m_gdn_conv_w'], 'm_gdn_a_log': out['m_gdn_a_log'], 'm_gdn_dt_bias': out['m_gdn_dt_bias'], 'm_gdn_norm_w': out['m_gdn_norm_w'], 'm_hgrn_lb_param': out['m_hgrn_lb_param'], 'm_hgrn_norm_w': out['m_hgrn_norm_w'], 'm_ssd_conv_w': out['m_ssd_conv_w'], 'm_ssd_conv_b': out['m_ssd_conv_b'], 'm_ssd_a_log': out['m_ssd_a_log'], 'm_ssd_dt_bias': out['m_ssd_dt_bias'], 'm_ssd_d': out['m_ssd_d'], 'm_ssd_norm_w': out['m_ssd_norm_w'], 'm_w_br_a': out['m_w_br_a'], 'm_w_br_b': out['m_w_br_b'], 'm_w_br_c': out['m_w_br_c'], 'm_w_out': out['m_w_out'], 'm_norm2_w': out['m_norm2_w'], 'm_ffn_w_up': out['m_ffn_w_up'], 'm_ffn_conv_w': out['m_ffn_conv_w'], 'm_ffn_conv_b': out['m_ffn_conv_b'], 'm_ffn_w_down': out['m_ffn_w_down'], 'm_final_norm_w': out['m_final_norm_w'], 'v_w_ada': out['v_w_ada'], 'v_b_ada': out['v_b_ada'], 'v_norm1_w': out['v_norm1_w'], 'v_w_in': out['v_w_in'], 'v_gdn_conv_w': out['v_gdn_conv_w'], 'v_gdn_a_log': out['v_gdn_a_log'], 'v_gdn_dt_bias': out['v_gdn_dt_bias'], 'v_gdn_norm_w': out['v_gdn_norm_w'], 'v_hgrn_lb_param': out['v_hgrn_lb_param'], 'v_hgrn_norm_w': out['v_hgrn_norm_w'], 'v_ssd_conv_w': out['v_ssd_conv_w'], 'v_ssd_conv_b': out['v_ssd_conv_b'], 'v_ssd_a_log': out['v_ssd_a_log'], 'v_ssd_dt_bias': out['v_ssd_dt_bias'], 'v_ssd_d': out['v_ssd_d'], 'v_ssd_norm_w': out['v_ssd_norm_w'], 'v_w_br_a': out['v_w_br_a'], 'v_w_br_b': out['v_w_br_b'], 'v_w_br_c': out['v_w_br_c'], 'v_w_out': out['v_w_out'], 'v_norm2_w': out['v_norm2_w'], 'v_ffn_w_up': out['v_ffn_w_up'], 'v_ffn_conv_w': out['v_ffn_conv_w'], 'v_ffn_conv_b': out['v_ffn_conv_b'], 'v_ffn_w_down': out['v_ffn_w_down'], 'v_final_norm_w': out['v_final_norm_w']}


def _loss(weights, diff, rest, loss_target):
    with _jax.named_scope("forward"):
        args = {**rest, TWIN_DIFF_INPUT: diff, **{k: w.astype(_WEIGHT_DTYPES[k]) for k, w in weights.items()}}
        y = _forward(args)
    with _jax.named_scope("loss_head"):
        err = _jnp.square(y.astype(_jnp.float32) - loss_target)
        return 0.5 * _jnp.sum(_jnp.mean(err, axis=-1)) if err.ndim else 0.5 * err


def _adamw(w, g, m, v):
    m = ADAM_B1 * m + (1.0 - ADAM_B1) * g
    v = ADAM_B2 * v + (1.0 - ADAM_B2) * _jnp.square(g)
    m_hat = m / (1.0 - ADAM_B1 ** ADAM_STEP)
    v_hat = v / (1.0 - ADAM_B2 ** ADAM_STEP)
    delta = -ADAM_LR * (m_hat / (_jnp.sqrt(v_hat) + ADAM_EPS) + ADAM_WD * w)
    return delta, m, v


def reference(x, c, w_ada, b_ada, norm1_w, w_in, gdn_conv_w, gdn_a_log, gdn_dt_bias, gdn_norm_w, hgrn_lb_param, hgrn_norm_w, ssd_conv_w, ssd_conv_b, ssd_a_log, ssd_dt_bias, ssd_d, ssd_norm_w, w_br_a, w_br_b, w_br_c, w_out, norm2_w, ffn_w_up, ffn_conv_w, ffn_conv_b, ffn_w_down, final_norm_w, loss_target, m_w_ada, m_b_ada, m_norm1_w, m_w_in, m_gdn_conv_w, m_gdn_a_log, m_gdn_dt_bias, m_gdn_norm_w, m_hgrn_lb_param, m_hgrn_norm_w, m_ssd_conv_w, m_ssd_conv_b, m_ssd_a_log, m_ssd_dt_bias, m_ssd_d, m_ssd_norm_w, m_w_br_a, m_w_br_b, m_w_br_c, m_w_out, m_norm2_w, m_ffn_w_up, m_ffn_conv_w, m_ffn_conv_b, m_ffn_w_down, m_final_norm_w, v_w_ada, v_b_ada, v_norm1_w, v_w_in, v_gdn_conv_w, v_gdn_a_log, v_gdn_dt_bias, v_gdn_norm_w, v_hgrn_lb_param, v_hgrn_norm_w, v_ssd_conv_w, v_ssd_conv_b, v_ssd_a_log, v_ssd_dt_bias, v_ssd_d, v_ssd_norm_w, v_w_br_a, v_w_br_b, v_w_br_c, v_w_out, v_norm2_w, v_ffn_w_up, v_ffn_conv_w, v_ffn_conv_b, v_ffn_w_down, v_final_norm_w):
    given = dict(x=x, c=c, w_ada=w_ada, b_ada=b_ada, norm1_w=norm1_w, w_in=w_in, gdn_conv_w=gdn_conv_w, gdn_a_log=gdn_a_log, gdn_dt_bias=gdn_dt_bias, gdn_norm_w=gdn_norm_w, hgrn_lb_param=hgrn_lb_param, hgrn_norm_w=hgrn_norm_w, ssd_conv_w=ssd_conv_w, ssd_conv_b=ssd_conv_b, ssd_a_log=ssd_a_log, ssd_dt_bias=ssd_dt_bias, ssd_d=ssd_d, ssd_norm_w=ssd_norm_w, w_br_a=w_br_a, w_br_b=w_br_b, w_br_c=w_br_c, w_out=w_out, norm2_w=norm2_w, ffn_w_up=ffn_w_up, ffn_conv_w=ffn_conv_w, ffn_conv_b=ffn_conv_b, ffn_w_down=ffn_w_down, final_norm_w=final_norm_w, loss_target=loss_target, m_w_ada=m_w_ada, m_b_ada=m_b_ada, m_norm1_w=m_norm1_w, m_w_in=m_w_in, m_gdn_conv_w=m_gdn_conv_w, m_gdn_a_log=m_gdn_a_log, m_gdn_dt_bias=m_gdn_dt_bias, m_gdn_norm_w=m_gdn_norm_w, m_hgrn_lb_param=m_hgrn_lb_param, m_hgrn_norm_w=m_hgrn_norm_w, m_ssd_conv_w=m_ssd_conv_w, m_ssd_conv_b=m_ssd_conv_b, m_ssd_a_log=m_ssd_a_log, m_ssd_dt_bias=m_ssd_dt_bias, m_ssd_d=m_ssd_d, m_ssd_norm_w=m_ssd_norm_w, m_w_br_a=m_w_br_a, m_w_br_b=m_w_br_b, m_w_br_c=m_w_br_c, m_w_out=m_w_out, m_norm2_w=m_norm2_w, m_ffn_w_up=m_ffn_w_up, m_ffn_conv_w=m_ffn_conv_w, m_ffn_conv_b=m_ffn_conv_b, m_ffn_w_down=m_ffn_w_down, m_final_norm_w=m_final_norm_w, v_w_ada=v_w_ada, v_b_ada=v_b_ada, v_norm1_w=v_norm1_w, v_w_in=v_w_in, v_gdn_conv_w=v_gdn_conv_w, v_gdn_a_log=v_gdn_a_log, v_gdn_dt_bias=v_gdn_dt_bias, v_gdn_norm_w=v_gdn_norm_w, v_hgrn_lb_param=v_hgrn_lb_param, v_hgrn_norm_w=v_hgrn_norm_w, v_ssd_conv_w=v_ssd_conv_w, v_ssd_conv_b=v_ssd_conv_b, v_ssd_a_log=v_ssd_a_log, v_ssd_dt_bias=v_ssd_dt_bias, v_ssd_d=v_ssd_d, v_ssd_norm_w=v_ssd_norm_w, v_w_br_a=v_w_br_a, v_w_br_b=v_w_br_b, v_w_br_c=v_w_br_c, v_w_out=v_w_out, v_norm2_w=v_norm2_w, v_ffn_w_up=v_ffn_w_up, v_ffn_conv_w=v_ffn_conv_w, v_ffn_conv_b=v_ffn_conv_b, v_ffn_w_down=v_ffn_w_down, v_final_norm_w=v_final_norm_w)
    weights = {n: given[n] for n in TWIN_WEIGHTS}
    shared = {n: given[n] for n in SHARED_INPUTS}
    per_example = {n: given[n] for n in ['x', 'c']}
    grad_fn = _jax.value_and_grad(_loss, argnums=(0, 1))

    def one_microbatch(ex, loss_target):
        ex = dict(ex)
        diff = ex.pop(TWIN_DIFF_INPUT)
        return grad_fn(weights, diff, {**shared, **ex}, loss_target)

    if N_MICROBATCH == 1:
        loss, (grad_w, grad_x) = one_microbatch(per_example, given["loss_target"])
    else:
        def body(carry, xs):
            loss_sum, grad_sum = carry
            l_k, (gw_k, gx_k) = one_microbatch(xs[0], xs[1])
            with _jax.named_scope("update"):
                return (loss_sum + l_k, _jax.tree.map(_jnp.add, grad_sum, gw_k)), gx_k

        init = (_jnp.zeros((), _jnp.float32), _jax.tree.map(_jnp.zeros_like, weights))
        (loss, grad_w), grad_x = _jax.lax.scan(body, init, (per_example, given["loss_target"]))
    with _jax.named_scope("update"):
        delta_w, new_m, new_v = {}, {}, {}
        for n in TWIN_WEIGHTS:
            delta_w[n], new_m[n], new_v[n] = _adamw(weights[n], grad_w[n], given["m_" + n], given["v_" + n])
    return (loss, grad_x, *[grad_w[n] for n in TWIN_WEIGHTS], *[delta_w[n] for n in TWIN_WEIGHTS],
            *[new_m[n] for n in TWIN_WEIGHTS], *[new_v[n] for n in TWIN_WEIGHTS])
```

```python
import functools

import jax
import jax.numpy as jnp
from jax import lax
from jax.experimental import pallas as pl
from jax.experimental.pallas import tpu as pltpu

F32 = jnp.float32
BF16 = jnp.bfloat16
HI = lax.Precision.HIGHEST
MESH = pl.DeviceIdType.MESH

EPS = 1e-6
D_MODEL = 1024
GDN_HEADS, GDN_DK, GDN_CHUNK = 4, 128, 64
HGRN_HEADS, HGRN_DK, HGRN_CHUNK = 4, 128, 16
SSD_HEADS, SSD_P, SSD_GROUPS, SSD_STATE, SSD_CHUNK = 8, 64, 2, 128, 64
FFN_HIDDEN = 2816
N_CHIPS = 4
N_DEV = 8

ADAM_LR, ADAM_B1, ADAM_B2, ADAM_EPS, ADAM_WD, ADAM_STEP = 0.001, 0.9, 0.999, 1e-08, 0.01, 10

W_G, W_A, W_B, W_C = 3072, 2176, 2048, 1664
VMEM_LIMIT = 56 * 1024 * 1024


def _cparams(sem):
    return pltpu.CompilerParams(dimension_semantics=sem, vmem_limit_bytes=VMEM_LIMIT)


def _dg(a, b, ca, cb):
    return lax.dot_general(a.astype(BF16), b.astype(BF16), (((ca,), (cb,)), ((), ())),
                           preferred_element_type=F32)


@jax.custom_vjp
def bdot(a, b):
    return _dg(a, b, 1, 0)


bdot.defvjp(lambda a, b: (_dg(a, b, 1, 0), (a, b)),
            lambda r, g: (_dg(g, r[1], 1, 1), _dg(r[0], g, 0, 0)))


@jax.custom_vjp
def bdot_nt(a, b):
    return _dg(a, b, 1, 1)


bdot_nt.defvjp(lambda a, b: (_dg(a, b, 1, 1), (a, b)),
               lambda r, g: (_dg(g, r[1], 1, 0), _dg(g, r[0], 0, 0)))


@jax.custom_vjp
def bdot_tn(a, b):
    return _dg(a, b, 0, 0)


bdot_tn.defvjp(lambda a, b: (_dg(a, b, 0, 0), (a, b)),
               lambda r, g: (_dg(r[1], g, 1, 1), _dg(r[0], g, 1, 0)))


def hdot(a, b):
    return jnp.dot(a, b, precision=HI, preferred_element_type=F32)


def _sigmoid(x):
    return 1.0 / (1.0 + jnp.exp(-x))


def _silu(x):
    return x * _sigmoid(x)


def _softplus(x):
    return jnp.maximum(x, 0.0) + jnp.log(1.0 + jnp.exp(-jnp.abs(x)))


def _rms(x, w):
    return x * lax.rsqrt(jnp.mean(x * x, axis=-1, keepdims=True) + EPS) * w


def _iota(shape, dim):
    return lax.broadcasted_iota(jnp.int32, shape, dim)


def _tri_ones(n, chunk, kind):
    i, j = _iota((n, n), 0), _iota((n, n), 1)
    same = lax.div(i, chunk) == lax.div(j, chunk)
    if kind == "incl":
        m = same & (j <= i)
    elif kind == "strict":
        m = same & (j < i)
    elif kind == "all":
        m = same
    else:
        m = same & (lax.rem(j, chunk) < (chunk // 2))
    return m


def _causal_conv(w, halo, x, width):
    r = x.shape[0]
    xin = jnp.concatenate([halo, x], axis=0)
    y = w[width - 1:width, :] * x
    for k in range(width - 1):
        off = 8 - (width - 1) + k
        y = y + w[k:k + 1, :] * xin[off:off + r, :]
    return y


def _neumann_inverse(m):
    n = m.shape[0]
    eye = (_iota((n, n), 0) == _iota((n, n), 1)).astype(F32)
    acc = eye - m
    p = m
    steps = 1
    while steps * 2 < n:
        p = hdot(p, p)
        acc = acc + hdot(acc, p)
        steps *= 2
    return acc


def gdn_tile(params, state, ins, halos):
    conv_w, pk = params
    (pa,), (ha,) = ins, halos
    r = pa.shape[0]
    c = GDN_CHUNK
    kw = GDN_HEADS * GDN_DK
    qkv = _silu(_causal_conv(conv_w, ha[:, :3 * kw], pa[:, :3 * kw], 4))
    z = pa[:, 3 * kw:4 * kw]
    gsm = pa[:, 4 * kw:]
    a_log, dtb, nw = pk[0:1, :], pk[1:2, :], pk[2:3, :]
    g_all = -jnp.exp(a_log) * _softplus(gsm + dtb)
    beta_all = _sigmoid(gsm)
    incl = _tri_ones(c, c, "incl")
    strict = _tri_ones(c, c, "strict")
    lmat = incl.astype(F32)
    scale = GDN_DK ** -0.5
    new_state = []
    outs = [[] for _ in range(GDN_HEADS)]
    st = [state[h * GDN_DK:(h + 1) * GDN_DK, :] for h in range(GDN_HEADS)]
    for ci in range(r // c):
        r0 = ci * c
        gc = hdot(lmat, g_all[r0:r0 + c, :])
        gct = gc.T
        for h in range(GDN_HEADS):
            sl = slice(h * GDN_DK, (h + 1) * GDN_DK)
            g_col, g_row, g_last = gc[:, h:h + 1], gct[h:h + 1, :], gc[c - 1:c, h:h + 1]
            beta = beta_all[r0:r0 + c, GDN_HEADS + h:GDN_HEADS + h + 1]
            qh = qkv[r0:r0 + c, sl]
            kh = qkv[r0:r0 + c, kw + h * GDN_DK:kw + (h + 1) * GDN_DK]
            vh = qkv[r0:r0 + c, 2 * kw + h * GDN_DK:2 * kw + (h + 1) * GDN_DK]
            qh = qh * lax.rsqrt(jnp.sum(qh * qh, axis=-1, keepdims=True) + EPS)
            kh = kh * lax.rsqrt(jnp.sum(kh * kh, axis=-1, keepdims=True) + EPS)
            diff = g_col - g_row
            decay = jnp.where(incl, jnp.exp(jnp.where(incl, diff, 0.0)), 0.0)
            kb = kh * beta
            m = jnp.where(strict, bdot_nt(kb, kh) * decay, 0.0)
            ainv = _neumann_inverse(m)
            eg = jnp.exp(g_col)
            sol = hdot(ainv, jnp.concatenate([vh * beta, kb * eg], axis=1))
            u, w = sol[:, :GDN_DK], sol[:, GDN_DK:]
            qs = qh * scale
            attn = bdot_nt(qs, kh) * decay
            qg = qs * eg
            k_end = kh * jnp.exp(g_last - g_col)
            s_h = st[h]
            v_new = u - bdot(w, s_h)
            o = bdot(qg, s_h) + bdot(attn, v_new)
            st[h] = s_h * jnp.exp(g_last) + bdot_tn(k_end, v_new)
            outs[h].append(_rms(o, nw) * _silu(z[r0:r0 + c, sl]))
    out = jnp.concatenate([jnp.concatenate(o, axis=0) for o in outs], axis=1)
    return jnp.concatenate(st, axis=0), [out]


def make_hgrn_tile(layer, depth):
    def hgrn_tile(params, state, ins, halos):
        lbp, nwp = params
        (pb,) = ins
        r = pb.shape[0]
        c = HGRN_CHUNK
        kw = HGRN_HEADS * HGRN_DK
        rows = [lbp[i:i + 1, :] for i in range(depth)]
        mx = functools.reduce(jnp.maximum, rows)
        ex = [jnp.exp(x - mx) for x in rows]
        den = functools.reduce(lambda a, b: a + b, ex)
        soft = [e / den for e in ex]
        lb = functools.reduce(lambda a, b: a + b, soft[:layer + 1]) - soft[0]
        nw = nwp[0:1, :]
        q = _silu(pb[:, :kw])
        fr = pb[:, kw:2 * kw]
        logf = jnp.log(lb + (1.0 - lb) * _sigmoid(fr))
        k = (1.0 - lb) * _sigmoid(-fr)
        v = pb[:, 2 * kw:3 * kw]
        gate = pb[:, 3 * kw:]
        incl = _tri_ones(r, c, "incl")
        g_cum = hdot(incl.astype(F32), logf)
        g_ref = hdot(_tri_ones(r, c, "upto").astype(F32), logf)
        g_end = hdot(_tri_ones(r, c, "all").astype(F32), logf)
        qs = q * jnp.exp(g_cum - g_ref)
        ks = k * jnp.exp(g_ref - g_cum)
        qg = q * jnp.exp(g_cum)
        k_end = k * jnp.exp(g_end - g_cum)
        e_end = jnp.exp(g_end)
        new_state, outs = [], []
        for h in range(HGRN_HEADS):
            sl = slice(h * HGRN_DK, (h + 1) * HGRN_DK)
            attn = jnp.where(incl, bdot_nt(qs[:, sl], ks[:, sl]), 0.0)
            o_intra = bdot(attn, v[:, sl])
            s_t = state[h * HGRN_DK:(h + 1) * HGRN_DK, :]
            o_inter = []
            for j in range(r // c):
                rs = slice(j * c, (j + 1) * c)
                o_inter.append(bdot_nt(qg[rs, sl], s_t))
                s_t = s_t * e_end[j * c:j * c + 1, sl] + bdot_tn(v[rs, sl], k_end[rs, sl])
            o = o_intra + jnp.concatenate(o_inter, axis=0)
            outs.append(_rms(o, nw) * _silu(gate[:, sl]))
            new_state.append(s_t)
        return jnp.concatenate(new_state, axis=0), [jnp.concatenate(outs, axis=1)]
    return hgrn_tile


def ssd_tile(params, state, ins, halos):
    conv_w, pv, ps = params
    (pc,), (hc,) = ins, halos
    r = pc.shape[0]
    c = SSD_CHUNK
    inner = SSD_HEADS * SSD_P
    gw = inner // SSD_GROUPS
    z = pc[:, :inner]
    xbc = _silu(_causal_conv(conv_w, hc[:, inner:inner + 1024], pc[:, inner:inner + 1024], 4) + pv[0:1, :])
    ssm = pc[:, inner + 1024:]
    xs = xbc[:, :inner]
    bm = xbc[:, inner:inner + SSD_GROUPS * SSD_STATE]
    cm = xbc[:, inner + SSD_GROUPS * SSD_STATE:]
    a_log, dtb, dsk = ps[0:1, :], ps[1:2, :], ps[2:3, :]
    nw = pv[1:2, :inner]
    dt = _softplus(ssm + dtb)
    da = dt * (-jnp.exp(a_log))
    expand = (lax.div(_iota((128, inner), 1), SSD_P) == _iota((128, inner), 0)).astype(F32)
    xdt = xs * hdot(dt, expand)
    d_e = hdot(jnp.concatenate([dsk] * 8, axis=0), expand)[0:1, :]
    incl = _tri_ones(c, c, "incl")
    lmat = incl.astype(F32)
    st = [state[g * SSD_STATE:(g + 1) * SSD_STATE, :] for g in range(SSD_GROUPS)]
    ys = []
    hpg = SSD_HEADS // SSD_GROUPS
    for ci in range(r // c):
        rs = slice(ci * c, (ci + 1) * c)
        acs = hdot(lmat, da[rs, :])
        acs_t = acs.T
        acs_e = hdot(acs, expand)
        last_e = acs_e[c - 1:c, :]
        yg = []
        for g in range(SSD_GROUPS):
            gl = slice(g * gw, (g + 1) * gw)
            bm_g = bm[rs, g * SSD_STATE:(g + 1) * SSD_STATE]
            cm_g = cm[rs, g * SSD_STATE:(g + 1) * SSD_STATE]
            cb = bdot_nt(cm_g, bm_g)
            yd = []
            for hg in range(hpg):
                hh = g * hpg + hg
                diff = acs[:, hh:hh + 1] - acs_t[hh:hh + 1, :]
                seg = jnp.where(incl, jnp.exp(jnp.where(incl, diff, 0.0)), 0.0)
                yd.append(bdot(cb * seg, xdt[rs, hh * SSD_P:(hh + 1) * SSD_P]))
            y_off = bdot(cm_g, st[g]) * jnp.exp(acs_e[:, gl])
            xw = xdt[rs, gl] * jnp.exp(last_e[:, gl] - acs_e[:, gl])
            st[g] = st[g] * jnp.exp(last_e[:, gl]) + bdot_tn(bm_g, xw)
            yg.append(jnp.concatenate(yd, axis=1) + y_off)
        ys.append(jnp.concatenate(yg, axis=1))
    y = jnp.concatenate(ys, axis=0) + d_e * xs
    yz = y * _silu(z)
    out = jnp.concatenate([_rms(yz[:, g * gw:(g + 1) * gw], nw[:, g * gw:(g + 1) * gw])
                           for g in range(SSD_GROUPS)], axis=1)
    return jnp.concatenate(st, axis=0), [out]


def convglu_tile(params, state, ins, halos):
    (cw,) = params
    (u,), (hu,) = ins, halos
    y = _causal_conv(cw, hu, u, 3) + cw[3:4, :]
    return None, [_silu(y[:, :FFN_HIDDEN]) * y[:, FFN_HIDDEN:]]


def _halo_map(nt, r):
    return lambda b, n: (jnp.maximum((b * nt + n) * (r // 8) - 1, 0), 0)


def seq_fwd(name, tile_fn, params, ins, use_halo, out_specs, state_shape, nb, s, r):
    nt = s // r
    n_p, n_i, n_o = len(params), len(ins), len(out_specs)
    has_state = state_shape is not None

    def body(*refs):
        p_refs, i_refs = refs[:n_p], refs[n_p:n_p + n_i]
        h_refs = refs[n_p + n_i:n_p + 2 * n_i] if use_halo else ()
        k = n_p + n_i + len(h_refs)
        o_refs = refs[k:k + n_o]
        n = pl.program_id(1)
        state = None
        if has_state:
            sv_ref, st_ref = refs[k + n_o], refs[k + n_o + 1]

            @pl.when(n == 0)
            def _():
                st_ref[...] = jnp.zeros(state_shape, F32)

            state = st_ref[...]
            sv_ref[0, 0] = state
        pv = [p[...] for p in p_refs]
        iv = [i[...].astype(F32) for i in i_refs]
        hv = [jnp.where(n > 0, h[...].astype(F32), 0.0) for h in h_refs]
        new_state, ov = tile_fn(pv, state, iv, hv)
        for o_ref, o in zip(o_refs, ov):
            o_ref[...] = o.astype(o_ref.dtype)
        if has_state:
            st_ref[...] = new_state

    row = lambda b, n: (b * nt + n, 0)
    in_specs = [pl.BlockSpec(p.shape, lambda b, n: (0, 0)) for p in params]
    in_specs += [pl.BlockSpec((r, a.shape[1]), row) for a in ins]
    if use_halo:
        in_specs += [pl.BlockSpec((8, a.shape[1]), _halo_map(nt, r)) for a in ins]
    out_shape = [jax.ShapeDtypeStruct((nb * s, w), dt) for w, dt in out_specs]
    o_specs = [pl.BlockSpec((r, w), row) for w, _ in out_specs]
    scratch = []
    if has_state:
        out_shape.append(jax.ShapeDtypeStruct((nb, nt) + tuple(state_shape), F32))
        o_specs.append(pl.BlockSpec((1, 1) + tuple(state_shape), lambda b, n: (b, n, 0, 0)))
        scratch.append(pltpu.VMEM(tuple(state_shape), F32))
    args = list(params) + list(ins) + (list(ins) if use_halo else [])
    return pl.pallas_call(body, grid=(nb, nt), in_specs=in_specs, out_specs=o_specs, out_shape=out_shape,
                          scratch_shapes=scratch, compiler_params=_cparams(("arbitrary", "arbitrary")),
                          name=name)(*args)


def seq_bwd(name, tile_fn, params, ins, use_halo, states, douts, din_dtypes, state_shape, nb, s, r):
    nt = s // r
    n_p, n_i, n_o = len(params), len(ins), len(douts)
    has_state = state_shape is not None

    def body(*refs):
        p_refs, i_refs = refs[:n_p], refs[n_p:n_p + n_i]
        h_refs = refs[n_p + n_i:n_p + 2 * n_i] if use_halo else ()
        k = n_p + n_i + len(h_refs)
        sv_ref = None
        if has_state:
            sv_ref = refs[k]
            k += 1
        do_refs = refs[k:k + n_o]
        k += n_o
        di_refs, dp_refs = refs[k:k + n_i], refs[k + n_i:k + n_i + n_p]
        k += n_i + n_p
        dst_ref = None
        if has_state:
            dst_ref = refs[k]
            k += 1
        dh_refs = refs[k:k + len(h_refs)]
        b, nn = pl.program_id(0), pl.program_id(1)
        n = nt - 1 - nn

        @pl.when((b == 0) & (nn == 0))
        def _():
            for dp in dp_refs:
                dp[...] = jnp.zeros(dp.shape, F32)

        @pl.when(nn == 0)
        def _():
            if has_state:
                dst_ref[...] = jnp.zeros(state_shape, F32)
            for dh in dh_refs:
                dh[...] = jnp.zeros(dh.shape, F32)

        pv = [p[...] for p in p_refs]
        iv = [i[...].astype(F32) for i in i_refs]
        hv = [jnp.where(n > 0, h[...].astype(F32), 0.0) for h in h_refs]
        if has_state:
            f = lambda pv_, st_, iv_, hv_: tile_fn(pv_, st_, iv_, hv_)
            _, vjp = jax.vjp(f, pv, sv_ref[0, 0], iv, hv)
            dpv, dst, div, dhv = vjp((dst_ref[...], [d[...].astype(F32) for d in do_refs]))
            dst_ref[...] = dst
        else:
            f = lambda pv_, iv_, hv_: tile_fn(pv_, None, iv_, hv_)[1]
            _, vjp = jax.vjp(f, pv, iv, hv)
            dpv, div, dhv = vjp([d[...].astype(F32) for d in do_refs])
        for j, (di_ref, d) in enumerate(zip(di_refs, div)):
            if use_halo:
                d = jnp.concatenate([d[:r - 8], d[r - 8:] + dh_refs[j][...]], axis=0)
            di_ref[...] = d.astype(di_ref.dtype)
        for dh_ref, d in zip(dh_refs, dhv):
            dh_ref[...] = d
        for dp_ref, d in zip(dp_refs, dpv):
            dp_ref[...] += d

    row = lambda b, nn: (b * nt + nt - 1 - nn, 0)
    hmap = _halo_map(nt, r)
    in_specs = [pl.BlockSpec(p.shape, lambda b, nn: (0, 0)) for p in params]
    in_specs += [pl.BlockSpec((r, a.shape[1]), row) for a in ins]
    if use_halo:
        in_specs += [pl.BlockSpec((8, a.shape[1]), lambda b, nn: hmap(b, nt - 1 - nn)) for a in ins]
    args = list(params) + list(ins) + (list(ins) if use_halo else [])
    scratch = []
    if has_state:
        in_specs.append(pl.BlockSpec((1, 1) + tuple(state_shape), lambda b, nn: (b, nt - 1 - nn, 0, 0)))
        args.append(states)
        scratch.append(pltpu.VMEM(tuple(state_shape), F32))
    in_specs += [pl.BlockSpec((r, d.shape[1]), row) for d in douts]
    args += list(douts)
    if use_halo:
        scratch += [pltpu.VMEM((8, a.shape[1]), F32) for a in ins]
    out_shape = [jax.ShapeDtypeStruct(a.shape, dt) for a, dt in zip(ins, din_dtypes)]
    out_shape += [jax.ShapeDtypeStruct(p.shape, F32) for p in params]
    o_specs = [pl.BlockSpec((r, a.shape[1]), row) for a in ins]
    o_specs += [pl.BlockSpec(p.shape, lambda b, nn: (0, 0)) for p in params]
    res = pl.pallas_call(body, grid=(nb, nt), in_specs=in_specs, out_specs=o_specs, out_shape=out_shape,
                         scratch_shapes=scratch, compiler_params=_cparams(("arbitrary", "arbitrary")),
                         name=name)(*args)
    return res[:n_i], res[n_i:]


def matmul(name, a, b, mode, out_dtype=F32, addend=None, tm=512, tn=None, tk=None):
    if mode == "nn":
        (m, kd), (_, n) = a.shape, b.shape
    elif mode == "nt":
        (m, kd), (n, _) = a.shape, b.shape
    else:
        (kd, m), (_, n) = a.shape, b.shape
    tm, tn, tk = min(tm, m), tn or n, tk or kd
    nk = kd // tk
    assert m % tm == 0 and n % tn == 0 and kd % tk == 0
    dims = {"nn": ((1,), (0,)), "nt": ((1,), (1,)), "tn": ((0,), (0,))}[mode]
    has_add = addend is not None

    def body(*refs):
        a_ref, b_ref = refs[0], refs[1]
        add_ref = refs[2] if has_add else None
        o_ref = refs[2 + has_add]
        part = lax.dot_general(a_ref[...].astype(BF16), b_ref[...].astype(BF16), (dims, ((), ())),
                               preferred_element_type=F32)

        def finish(acc):
            if has_add:
                acc = acc + add_ref[...]
            o_ref[...] = acc.astype(o_ref.dtype)

        if nk == 1:
            finish(part)
        else:
            acc_ref = refs[3 + has_add]
            k = pl.program_id(2)

            @pl.when(k == 0)
            def _():
                acc_ref[...] = part

            @pl.when(k > 0)
            def _():
                acc_ref[...] += part

            @pl.when(k == nk - 1)
            def _():
                finish(acc_ref[...])

    if mode == "tn":
        a_spec = pl.BlockSpec((tk, tm), lambda j, i, k: (k, i))
    else:
        a_spec = pl.BlockSpec((tm, tk), lambda j, i, k: (i, k))
    if mode == "nt":
        b_spec = pl.BlockSpec((tn, tk), lambda j, i, k: (j, k))
    else:
        b_spec = pl.BlockSpec((tk, tn), lambda j, i, k: (k, j))
    o_spec = pl.BlockSpec((tm, tn), lambda j, i, k: (i, j))
    in_specs, args = [a_spec, b_spec], [a, b]
    if has_add:
        in_specs.append(o_spec)
        args.append(addend)
    scratch = [pltpu.VMEM((tm, tn), F32)] if nk > 1 else []
    return pl.pallas_call(body, grid=(n // tn, m // tm, nk), in_specs=in_specs, out_specs=o_spec,
                          out_shape=jax.ShapeDtypeStruct((m, n), out_dtype), scratch_shapes=scratch,
                          compiler_params=_cparams(("parallel", "parallel", "arbitrary")), name=name)(*args)


def _normmod(x, nw, shift, scale):
    return _rms(x, nw) * (1.0 + scale) + shift


def _row_specs(nb, s, tr, d):
    nt = s // tr
    row = pl.BlockSpec((tr, d), lambda b, i: (b * nt + i, 0))
    per_seq = pl.BlockSpec((1, 1, d), lambda b, i: (b, 0, 0))
    full = pl.BlockSpec((1, d), lambda b, i: (0, 0))
    return nt, row, per_seq, full


def normmod_fwd(name, x, nw, shift, scale, nb, s, tr=512):
    d, tr = x.shape[1], min(tr, s)
    nt, row, per_seq, full = _row_specs(nb, s, tr, d)

    def body(x_ref, nw_ref, sh_ref, sc_ref, h_ref):
        h_ref[...] = _normmod(x_ref[...], nw_ref[...], sh_ref[0], sc_ref[0]).astype(h_ref.dtype)

    return pl.pallas_call(body, grid=(nb, nt), in_specs=[row, full, per_seq, per_seq], out_specs=row,
                          out_shape=jax.ShapeDtypeStruct(x.shape, BF16),
                          compiler_params=_cparams(("parallel", "parallel")), name=name)(x, nw, shift, scale)


def normmod_bwd(name, x, nw, shift, scale, dh, dres, nb, s, tr=512):
    d, tr = x.shape[1], min(tr, s)
    nt, row, per_seq, full = _row_specs(nb, s, tr, d)

    def body(x_ref, nw_ref, sh_ref, sc_ref, dh_ref, dres_ref, dx_ref, dnw_ref, dsh_ref, dsc_ref):
        b, i = pl.program_id(0), pl.program_id(1)

        @pl.when((b == 0) & (i == 0))
        def _():
            dnw_ref[...] = jnp.zeros(dnw_ref.shape, F32)

        @pl.when(i == 0)
        def _():
            dsh_ref[...] = jnp.zeros(dsh_ref.shape, F32)
            dsc_ref[...] = jnp.zeros(dsc_ref.shape, F32)

        _, vjp = jax.vjp(_normmod, x_ref[...], nw_ref[...], sh_ref[0], sc_ref[0])
        dx, dnw, dsh, dsc = vjp(dh_ref[...])
        dx_ref[...] = dres_ref[...] + dx
        dnw_ref[...] += dnw
        dsh_ref[0] += dsh
        dsc_ref[0] += dsc

    out_shape = [jax.ShapeDtypeStruct(x.shape, F32), jax.ShapeDtypeStruct((1, d), F32),
                 jax.ShapeDtypeStruct((nb, 1, d), F32), jax.ShapeDtypeStruct((nb, 1, d), F32)]
    return pl.pallas_call(body, grid=(nb, nt), in_specs=[row, full, per_seq, per_seq, row, row],
                          out_specs=[row, full, per_seq, per_seq], out_shape=out_shape,
                          compiler_params=_cparams(("arbitrary", "arbitrary")),
                          name=name)(x, nw, shift, scale, dh, dres)


def _merge(oa, ob, oc, graw, gate1, wa, wb, wc, wo):
    d = wo.shape[0]
    g = _sigmoid(graw)
    merged = g[:, :d] * bdot(oa, wa) + g[:, d:2 * d] * bdot(ob, wb) + g[:, 2 * d:] * bdot(oc, wc)
    return gate1 * bdot(merged, wo)


def _merge_specs(nb, s, tr, d, wbr):
    nt, row, per_seq, _ = _row_specs(nb, s, tr, d)
    o_spec = pl.BlockSpec((tr, wbr), lambda b, i: (b * nt + i, 0))
    g_spec = pl.BlockSpec((tr, 3 * d), lambda b, i: (b * nt + i, 0))
    wbr_spec = pl.BlockSpec((wbr, d), lambda b, i: (0, 0))
    wo_spec = pl.BlockSpec((d, d), lambda b, i: (0, 0))
    return nt, row, per_seq, o_spec, g_spec, wbr_spec, wo_spec


def merge_fwd(name, x, oa, ob, oc, pg, gate1, wa, wb, wc, wo, nb, s, tr=512):
    d, tr = x.shape[1], min(tr, s)
    nt, row, per_seq, o_spec, g_spec, wbr_spec, wo_spec = _merge_specs(nb, s, tr, d, oa.shape[1])

    def body(x_ref, oa_ref, ob_ref, oc_ref, pg_ref, g1_ref, wa_ref, wb_ref, wc_ref, wo_ref, x1_ref):
        x1_ref[...] = x_ref[...] + _merge(oa_ref[...], ob_ref[...], oc_ref[...], pg_ref[...], g1_ref[0],
                                          wa_ref[...], wb_ref[...], wc_ref[...], wo_ref[...])

    return pl.pallas_call(body, grid=(nb, nt),
                          in_specs=[row, o_spec, o_spec, o_spec, g_spec, per_seq, wbr_spec, wbr_spec, wbr_spec, wo_spec],
                          out_specs=row, out_shape=jax.ShapeDtypeStruct(x.shape, F32),
                          compiler_params=_cparams(("parallel", "parallel")),
                          name=name)(x, oa, ob, oc, pg, gate1, wa, wb, wc, wo)


def merge_bwd(name, oa, ob, oc, pg, gate1, wa, wb, wc, wo, dx1, nb, s, tr=256):
    d, tr = dx1.shape[1], min(tr, s)
    wbr = oa.shape[1]
    nt, row, per_seq, o_spec, g_spec, wbr_spec, wo_spec = _merge_specs(nb, s, tr, d, wbr)

    def body(oa_ref, ob_ref, oc_ref, pg_ref, g1_ref, wa_ref, wb_ref, wc_ref, wo_ref, dx_ref,
             doa_ref, dob_ref, doc_ref, dpg_ref, dg1_ref, dwa_ref, dwb_ref, dwc_ref, dwo_ref):
        b, i = pl.program_id(0), pl.program_id(1)

        @pl.when((b == 0) & (i == 0))
        def _():
            for r in (dwa_ref, dwb_ref, dwc_ref, dwo_ref):
                r[...] = jnp.zeros(r.shape, F32)

        @pl.when(i == 0)
        def _():
            dg1_ref[...] = jnp.zeros(dg1_ref.shape, F32)

        args = [oa_ref[...].astype(F32), ob_ref[...].astype(F32), oc_ref[...].astype(F32), pg_ref[...], g1_ref[0],
                wa_ref[...].astype(F32), wb_ref[...].astype(F32), wc_ref[...].astype(F32), wo_ref[...].astype(F32)]
        _, vjp = jax.vjp(_merge, *args)
        doa, dob, doc, dpg, dg1, dwa, dwb, dwc, dwo = vjp(dx_ref[...])
        doa_ref[...] = doa
        dob_ref[...] = dob
        doc_ref[...] = doc
        dpg_ref[...] = dpg
        dg1_ref[0] += dg1
        dwa_ref[...] += dwa
        dwb_ref[...] += dwb
        dwc_ref[...] += dwc
        dwo_ref[...] += dwo

    t = nb * s
    out_shape = ([jax.ShapeDtypeStruct((t, wbr), F32)] * 3
                 + [jax.ShapeDtypeStruct((t, 3 * d), F32), jax.ShapeDtypeStruct((nb, 1, d), F32)]
                 + [jax.ShapeDtypeStruct((wbr, d), F32)] * 3 + [jax.ShapeDtypeStruct((d, d), F32)])
    return pl.pallas_call(body, grid=(nb, nt),
                          in_specs=[o_spec, o_spec, o_spec, g_spec, per_seq, wbr_spec, wbr_spec, wbr_spec, wo_spec, row],
                          out_specs=[o_spec, o_spec, o_spec, g_spec, per_seq, wbr_spec, wbr_spec, wbr_spec, wo_spec],
                          out_shape=out_shape, compiler_params=_cparams(("arbitrary", "arbitrary")),
                          name=name)(oa, ob, oc, pg, gate1, wa, wb, wc, wo, dx1)


def resid_fwd(name, x, f, gate, nb, s, tr=512):
    d, tr = x.shape[1], min(tr, s)
    nt, row, per_seq, _ = _row_specs(nb, s, tr, d)

    def body(x_ref, f_ref, g_ref, o_ref):
        o_ref[...] = x_ref[...] + g_ref[0] * f_ref[...]

    return pl.pallas_call(body, grid=(nb, nt), in_specs=[row, row, per_seq], out_specs=row,
                          out_shape=jax.ShapeDtypeStruct(x.shape, F32),
                          compiler_params=_cparams(("parallel", "parallel")), name=name)(x, f, gate)


def resid_bwd(name, dx, f, gate, nb, s, tr=512):
    d, tr = dx.shape[1], min(tr, s)
    nt, row, per_seq, _ = _row_specs(nb, s, tr, d)

    def body(dx_ref, f_ref, g_ref, df_ref, dg_ref):
        @pl.when(pl.program_id(1) == 0)
        def _():
            dg_ref[...] = jnp.zeros(dg_ref.shape, F32)

        df_ref[...] = (g_ref[0] * dx_ref[...]).astype(df_ref.dtype)
        dg_ref[0] += jnp.sum(dx_ref[...] * f_ref[...], axis=0, keepdims=True)

    return pl.pallas_call(body, grid=(nb, nt), in_specs=[row, row, per_seq], out_specs=[row, per_seq],
                          out_shape=[jax.ShapeDtypeStruct(dx.shape, BF16), jax.ShapeDtypeStruct((nb, 1, d), F32)],
                          compiler_params=_cparams(("arbitrary", "arbitrary")), name=name)(dx, f, gate)


def loss_head(name, x, fw, target, tr=512):
    t, d = x.shape
    row = pl.BlockSpec((tr, d), lambda i: (i, 0))
    full = pl.BlockSpec((1, d), lambda i: (0, 0))

    def loss_fn(xv, fwv, tv):
        err = _rms(xv, fwv) - tv
        return 0.5 * jnp.sum(jnp.mean(err * err, axis=-1))

    def body(x_ref, fw_ref, t_ref, dx_ref, l_ref, dfw_ref):
        @pl.when(pl.program_id(0) == 0)
        def _():
            l_ref[...] = jnp.zeros(l_ref.shape, F32)
            dfw_ref[...] = jnp.zeros(dfw_ref.shape, F32)

        val, (dx, dfw) = jax.value_and_grad(loss_fn, argnums=(0, 1))(x_ref[...], fw_ref[...], t_ref[...])
        dx_ref[...] = dx
        l_ref[...] += val
        dfw_ref[...] += dfw

    return pl.pallas_call(body, grid=(t // tr,), in_specs=[row, full, row],
                          out_specs=[row, pl.BlockSpec((1, 128), lambda i: (0, 0)), full],
                          out_shape=[jax.ShapeDtypeStruct((t, d), F32), jax.ShapeDtypeStruct((1, 128), F32),
                                     jax.ShapeDtypeStruct((1, d), F32)],
                          compiler_params=_cparams(("arbitrary",)), name=name)(x, fw, target)


def _row_tile(rows, cols, n_arrays):
    budget = 24 * 1024 * 1024 // (8 * cols * max(n_arrays, 1))
    tr = rows
    while tr > max(budget, 16) and tr % 2 == 0 and (tr // 2) % 16 == 0:
        tr //= 2
    return tr


def elementwise(name, fn, ins, out_dtypes):
    rows, cols = ins[0].shape
    tr = _row_tile(rows, cols, len(ins) + len(out_dtypes))
    spec = pl.BlockSpec((tr, cols), lambda i: (i, 0))
    n_in = len(ins)

    def body(*refs):
        outs = fn(*[r[...] for r in refs[:n_in]])
        for o_ref, o in zip(refs[n_in:], outs):
            o_ref[...] = o.astype(o_ref.dtype)

    return pl.pallas_call(body, grid=(rows // tr,), in_specs=[spec] * n_in, out_specs=[spec] * len(out_dtypes),
                          out_shape=[jax.ShapeDtypeStruct((rows, cols), dt) for dt in out_dtypes],
                          compiler_params=_cparams(("parallel",)), name=name)(*ins)


def _adamw(w, g, m, v):
    m = ADAM_B1 * m + (1.0 - ADAM_B1) * g
    v = ADAM_B2 * v + (1.0 - ADAM_B2) * (g * g)
    m_hat = m / (1.0 - ADAM_B1 ** ADAM_STEP)
    v_hat = v / (1.0 - ADAM_B2 ** ADAM_STEP)
    delta = -ADAM_LR * (m_hat / (jnp.sqrt(v_hat) + ADAM_EPS) + ADAM_WD * w)
    return delta, m, v


def adamw(name, w, g, m, v):
    return elementwise(name, _adamw, [w, g, m, v], [F32, F32, F32])


_ANY = pl.BlockSpec(memory_space=pl.ANY)


def _coords():
    return lax.axis_index("x"), lax.axis_index("y"), lax.axis_index("c")


def allgather8(name, arrays, halves):
    n = len(arrays)

    def body(*refs):
        in_refs, out_refs = refs[:n], refs[n:2 * n]
        send_sems, recv_sems, local_sems = refs[2 * n:]
        x, y, c = _coords()
        me, sibling = (x, y, c), (x, y, 1 - c)
        chips = [(1 - x, y), (x, 1 - y), (1 - x, 1 - y)]

        def blk(i, px, py, pc):
            return out_refs[i].at[4 * px + 2 * py + pc]

        def piece(i):
            return in_refs[i].at[c] if halves[i] else in_refs[i]

        def copy(i, k, block, to, src=None):
            return pltpu.make_async_remote_copy(
                src_ref=blk(i, *block) if src is None else src, dst_ref=blk(i, *block),
                send_sem=send_sems.at[7 * i + k], recv_sem=recv_sems.at[7 * i + k],
                device_id=to, device_id_type=MESH)

        mine = [pltpu.make_async_copy(piece(i), blk(i, *me), local_sems.at[i]) for i in range(n)]
        for cp in mine:
            cp.start()
        first = []
        for i in range(n):
            first.append(copy(i, 0, me, sibling, src=piece(i)))
            first += [copy(i, 1 + j, me, (*chip, c), src=piece(i)) for j, chip in enumerate(chips)]
        for cp in first:
            cp.start()
        passed = []
        for j, chip in enumerate(chips):
            for i in range(n):
                copy(i, 1 + j, (*chip, c), me).wait_recv()
                fwd = copy(i, 4 + j, (*chip, c), sibling)
                fwd.start()
                passed.append(fwd)
        for i in range(n):
            copy(i, 0, sibling, me).wait_recv()
            for j, chip in enumerate(chips):
                copy(i, 4 + j, (*chip, 1 - c), me).wait_recv()
        for cp in first + passed:
            cp.wait_send()
        for cp in mine:
            cp.wait()

    out_shape = []
    for a, hv in zip(arrays, halves):
        out_shape.append(jax.ShapeDtypeStruct((N_DEV,) + tuple(a.shape[1:] if hv else a.shape), a.dtype))
    return pl.pallas_call(
        body, in_specs=[_ANY] * n, out_specs=[_ANY] * n, out_shape=out_shape,
        scratch_shapes=[pltpu.SemaphoreType.DMA((7 * n,)), pltpu.SemaphoreType.DMA((7 * n,)),
                        pltpu.SemaphoreType.DMA((n,))],
        name=name)(*arrays)


def exchange(name, ins, out_shapes, plan, local_plan=()):
    n_in, n_out = len(ins), len(out_shapes)

    def body(*refs):
        in_refs, out_refs = refs[:n_in], refs[n_in:n_in + n_out]
        send_sems, recv_sems, local_sems = refs[n_in + n_out:]
        x, y, c = _coords()
        copies = []
        for k, fn in enumerate(plan):
            src, dst, peer = fn(in_refs, out_refs, x, y, c)
            copies.append(pltpu.make_async_remote_copy(src_ref=src, dst_ref=dst, send_sem=send_sems.at[k],
                                                       recv_sem=recv_sems.at[k], device_id=peer,
                                                       device_id_type=MESH))
        local = []
        for k, fn in enumerate(local_plan):
            src, dst = fn(in_refs, out_refs, x, y, c)
            local.append(pltpu.make_async_copy(src, dst, local_sems.at[k]))
        for cp in copies + local:
            cp.start()
        for cp in copies + local:
            cp.wait()

    return pl.pallas_call(
        body, in_specs=[_ANY] * n_in, out_specs=[_ANY] * n_out, out_shape=out_shapes,
        scratch_shapes=[pltpu.SemaphoreType.DMA((len(plan),)), pltpu.SemaphoreType.DMA((len(plan),)),
                        pltpu.SemaphoreType.DMA((max(len(local_plan), 1),))],
        name=name)(*ins)


def sum_halves(name, gs, recv, c_arr):
    _, _, hr, cs = gs.shape
    tr = _row_tile(hr, cs, 4)

    def body(c_ref, g_ref, r_ref, qf_ref, qb_ref):
        q = g_ref[0, 0] + r_ref[0, 0]
        qf_ref[0] = q
        qb_ref[0] = q.astype(BF16)

    grid_spec = pltpu.PrefetchScalarGridSpec(
        num_scalar_prefetch=1, grid=(N_CHIPS, hr // tr),
        in_specs=[pl.BlockSpec((1, 1, tr, cs), lambda j, i, c_ref: (j, c_ref[0], i, 0)),
                  pl.BlockSpec((1, 1, tr, cs), lambda j, i, c_ref: (j, 0, i, 0))],
        out_specs=[pl.BlockSpec((1, tr, cs), lambda j, i, c_ref: (j, i, 0))] * 2)
    return pl.pallas_call(body, grid_spec=grid_spec,
                          out_shape=[jax.ShapeDtypeStruct((N_CHIPS, hr, cs), F32),
                                     jax.ShapeDtypeStruct((N_CHIPS, hr, cs), BF16)],
                          compiler_params=_cparams(("parallel", "parallel")), name=name)(c_arr, gs, recv)


def sum8(name, g):
    _, rows, cols = g.shape
    tr = _row_tile(rows, cols, 9)

    def body(*refs):
        acc = refs[0][0]
        for r in refs[1:N_DEV]:
            acc = acc + r[0]
        refs[N_DEV][...] = acc

    in_specs = [pl.BlockSpec((1, tr, cols), functools.partial(lambda k, i: (k, i, 0), k)) for k in range(N_DEV)]
    return pl.pallas_call(body, grid=(rows // tr,), in_specs=in_specs,
                          out_specs=pl.BlockSpec((tr, cols), lambda i: (i, 0)),
                          out_shape=jax.ShapeDtypeStruct((rows, cols), F32),
                          compiler_params=_cparams(("parallel",)), name=name)(*([g] * N_DEV))


_QKV, _AB, _GZ = (0, 1536), (1536, 1544), (1544, 2056)
_HG = (2056, 4104)
_SZ, _XBC, _DT = (4104, 4616), (4616, 5640), (5640, 5648)
_GATES = (5648, 8720)


def _cols(w, rng):
    return w[..., rng[0]:rng[1]]


def _split_w_in(w):
    pad = jnp.zeros(w.shape[:-1] + (120,), w.dtype)
    return (_cols(w, _GATES),
            jnp.concatenate([_cols(w, _QKV), _cols(w, _GZ), _cols(w, _AB), pad], axis=-1),
            _cols(w, _HG),
            jnp.concatenate([_cols(w, _SZ), _cols(w, _XBC), _cols(w, _DT), pad], axis=-1))


def _join_w_in(g, a, b, c):
    return jnp.concatenate([a[..., 0:1536], a[..., 2048:2056], a[..., 1536:2048], b,
                            c[..., 0:512], c[..., 512:1536], c[..., 1536:1544], g], axis=-1)


def _rows8(rows, width):
    out = [jnp.pad(r.astype(F32), (0, width - r.shape[0])) for r in rows]
    out += [jnp.zeros((width,), F32)] * (8 - len(out))
    return jnp.stack(out)


class _Packer:
    def __init__(self):
        self.items, self.size = [], 0

    def add(self, name, shape):
        n = 1
        for d in shape:
            n *= d
        self.items.append((name, tuple(shape), self.size, n))
        self.size += n

    def rows(self):
        return -(-self.size // 8192) * 8

    def pack(self, values):
        flat = [values[name].astype(F32).reshape(-1) for name, _, _, _ in self.items]
        flat.append(jnp.zeros((self.rows() * 1024 - self.size,), F32))
        return jnp.concatenate(flat).reshape(self.rows(), 1024)

    def unpack(self, buf):
        flat = buf.reshape(-1)
        return {name: flat[off:off + n].reshape(shape) for name, shape, off, n in self.items}


def _stack_by_chip(g, axis):
    l, r, c = g.shape
    if axis == 2:
        cs = c // N_CHIPS
        g = g.reshape(l, r, N_CHIPS, cs).transpose(2, 0, 1, 3).reshape(N_CHIPS, 2, l * r // 2, cs)
    else:
        rs = r // N_CHIPS
        g = g.reshape(l, N_CHIPS, rs, c).transpose(1, 0, 2, 3).reshape(N_CHIPS, 2, l * rs // 2, c)
    return g


def _unstack_gathered(w8, l, axis):
    _, hr, cs = w8.shape
    w = w8.reshape(N_CHIPS, l, 2 * hr // l, cs)
    if axis == 2:
        return w.transpose(1, 2, 0, 3).reshape(l, 2 * hr // l, N_CHIPS * cs)
    return w.transpose(1, 0, 2, 3).reshape(l, N_CHIPS * 2 * hr // l, cs)


_BIG = (("w_in", 2), ("w_br_a", 2), ("w_br_b", 2), ("w_br_c", 2), ("w_out", 1), ("ffn_w_up", 2), ("ffn_w_down", 1))
_SMALL = ("b_ada", "norm1_w", "gdn_conv_w", "gdn_a_log", "gdn_dt_bias", "gdn_norm_w", "hgrn_lb_param",
          "hgrn_norm_w", "ssd_conv_w", "ssd_conv_b", "ssd_a_log", "ssd_dt_bias", "ssd_d", "ssd_norm_w",
          "norm2_w", "ffn_conv_w", "ffn_conv_b", "final_norm_w")
_WEIGHTS = ("w_ada", "b_ada", "norm1_w", "w_in", "gdn_conv_w", "gdn_a_log", "gdn_dt_bias", "gdn_norm_w",
            "hgrn_lb_param", "hgrn_norm_w", "ssd_conv_w", "ssd_conv_b", "ssd_a_log", "ssd_dt_bias", "ssd_d",
            "ssd_norm_w", "w_br_a", "w_br_b", "w_br_c", "w_out", "norm2_w", "ffn_w_up", "ffn_conv_w",
            "ffn_conv_b", "ffn_w_down", "final_norm_w")
_R_GDN, _R_HGRN, _R_SSD, _R_FFN = 128, 128, 128, 256


def _reduce_scatter(grads):
    n = len(grads)
    c_arr = lax.axis_index("c").astype(jnp.int32).reshape(1)

    plan = [functools.partial(lambda i, ins, outs, x, y, c: (ins[i].at[:, pl.ds(1 - c, 1)], outs[i], (x, y, 1 - c)), i)
            for i in range(n)]
    recv = exchange("rs_d2d", grads, [jax.ShapeDtypeStruct((N_CHIPS, 1) + g.shape[2:], F32) for g in grads], plan)
    q = [sum_halves("rs_sum_d2d%d" % i, g, r, c_arr) for i, (g, r) in enumerate(zip(grads, recv))]
    qf, qb = [a for a, _ in q], [b for _, b in q]

    masks = ((1, 0), (0, 1), (1, 1))

    def ici(i, k, ins, outs, x, y, c):
        px = 1 - x if masks[k][0] else x
        py = 1 - y if masks[k][1] else y
        return ins[n + i].at[2 * px + py], outs[n + 3 * i + k], (px, py, c)

    def own(i, ins, outs, x, y, c):
        return ins[i].at[2 * x + y], outs[i]

    plan = [functools.partial(ici, i, k) for i in range(n) for k in range(3)]
    local = [functools.partial(own, i) for i in range(n)]
    shapes = [jax.ShapeDtypeStruct(g.shape[2:], F32) for g in grads]
    shapes += [jax.ShapeDtypeStruct(g.shape[2:], BF16) for g in grads for _ in range(3)]
    res = exchange("rs_ici", qf + qb, shapes, plan, local)
    red = []
    for i in range(n):
        (r,) = elementwise("rs_sum_ici%d" % i,
                           lambda o, a, b, c: (o + a.astype(F32) + b.astype(F32) + c.astype(F32),),
                           [res[i], res[n + 3 * i], res[n + 3 * i + 1], res[n + 3 * i + 2]], [F32])
        red.append(r)

    plan = [functools.partial(lambda i, ins, outs, x, y, c: (ins[i], outs[i].at[c], (x, y, 1 - c)), i) for i in range(n)]
    local = [functools.partial(lambda i, ins, outs, x, y, c: (ins[i], outs[i].at[c]), i) for i in range(n)]
    full = exchange("rs_swap", red, [jax.ShapeDtypeStruct((2,) + r.shape, F32) for r in red], plan, local)
    return [f.reshape(2 * f.shape[1], f.shape[2]) for f in full]


def kernel(x, c, w_ada, b_ada, norm1_w, w_in, gdn_conv_w, gdn_a_log, gdn_dt_bias, gdn_norm_w, hgrn_lb_param, hgrn_norm_w, ssd_conv_w, ssd_conv_b, ssd_a_log, ssd_dt_bias, ssd_d, ssd_norm_w, w_br_a, w_br_b, w_br_c, w_out, norm2_w, ffn_w_up, ffn_conv_w, ffn_conv_b, ffn_w_down, final_norm_w, loss_target, m_w_ada, m_b_ada, m_norm1_w, m_w_in, m_gdn_conv_w, m_gdn_a_log, m_gdn_dt_bias, m_gdn_norm_w, m_hgrn_lb_param, m_hgrn_norm_w, m_ssd_conv_w, m_ssd_conv_b, m_ssd_a_log, m_ssd_dt_bias, m_ssd_d, m_ssd_norm_w, m_w_br_a, m_w_br_b, m_w_br_c, m_w_out, m_norm2_w, m_ffn_w_up, m_ffn_conv_w, m_ffn_conv_b, m_ffn_w_down, m_final_norm_w, v_w_ada, v_b_ada, v_norm1_w, v_w_in, v_gdn_conv_w, v_gdn_a_log, v_gdn_dt_bias, v_gdn_norm_w, v_hgrn_lb_param, v_hgrn_norm_w, v_ssd_conv_w, v_ssd_conv_b, v_ssd_a_log, v_ssd_dt_bias, v_ssd_d, v_ssd_norm_w, v_w_br_a, v_w_br_b, v_w_br_c, v_w_out, v_norm2_w, v_ffn_w_up, v_ffn_conv_w, v_ffn_conv_b, v_ffn_w_down, v_final_norm_w):
    loc = dict(locals())
    w = {k: loc[k] for k in _WEIGHTS}
    mom = {k: loc["m_" + k] for k in _WEIGHTS}
    var = {k: loc["v_" + k] for k in _WEIGHTS}
    nb, s, d = x.shape
    t = nb * s
    depth = w_ada.shape[0]
    chip = 2 * lax.axis_index("x") + lax.axis_index("y")
    dev = 2 * chip + lax.axis_index("c")
    x0 = x.reshape(t, d)
    target = loss_target.reshape(t, d)

    small_in = [c, gdn_conv_w.reshape(depth * 4, -1), ssd_conv_w.reshape(depth * 4, -1),
                ffn_conv_w.reshape(depth * 3, -1)]
    c_all, gcw, scw, fcw = allgather8("ag_small", small_in, [False] * 4)
    c_all = c_all.reshape(N_DEV * nb, d)

    def conv_full(g, taps):
        g = g[::2].reshape(N_CHIPS, depth, taps, -1)
        return g.transpose(1, 2, 0, 3).reshape(depth, taps, -1)

    gdn_cw, ssd_cw, ffn_cw = conv_full(gcw, 4), conv_full(scw, 4), conv_full(fcw, 3)

    big_in = []
    for name, axis in _BIG:
        a = w[name].astype(BF16)
        big_in.append(a.reshape(2, a.shape[0] * a.shape[1] // 2, a.shape[2]))
    big = allgather8("ag_weights", big_in, [True] * len(_BIG))
    wf = {name: _unstack_gathered(g, depth, axis) for (name, axis), g in zip(_BIG, big)}
    w_g, w_a, w_b, w_c = _split_w_in(wf["w_in"])

    (c_act,) = elementwise("silu_c", lambda v: (_silu(v),), [c_all], [F32])
    mod_cols = jnp.concatenate([matmul("ada_fwd%d" % l, c_act, w_ada[l], "nn") for l in range(depth)], axis=0)
    (mod8,) = allgather8("ag_mod", [mod_cols], [False])
    mod = mod8[::2].reshape(N_CHIPS, depth, N_DEV * nb, -1).transpose(1, 2, 0, 3).reshape(depth, N_DEV * nb, 6 * d)
    mod = lax.dynamic_slice_in_dim(mod, dev * nb, nb, axis=1) + b_ada[:, None, :]

    def mod_part(l, k):
        return mod[l, :, k * d:(k + 1) * d].reshape(nb, 1, d)

    saved = []
    xl = x0
    for l in range(depth):
        sfx = str(l)
        sv = {"x0": xl}
        shift1, scale1, gate1, shift2, scale2, gate2 = [mod_part(l, k) for k in range(6)]
        sv["mods"] = (shift1, scale1, gate1, shift2, scale2, gate2)
        h = normmod_fwd("norm1_fwd" + sfx, xl, norm1_w[l][None], shift1, scale1, nb, s)
        pg = matmul("proj_g" + sfx, h, w_g[l], "nn")
        pa = matmul("proj_a" + sfx, h, w_a[l], "nn")
        pb = matmul("proj_b" + sfx, h, w_b[l], "nn")
        pc = matmul("proj_c" + sfx, h, w_c[l], "nn")
        gdn_p = [_rows8(list(gdn_cw[l]), 1536), _rows8([gdn_a_log[l], gdn_dt_bias[l], gdn_norm_w[l]], 128)]
        hgrn_p = [_rows8(list(hgrn_lb_param), 512), _rows8([hgrn_norm_w[l]], 128)]
        ssd_p = [_rows8(list(ssd_cw[l]), 1024), _rows8([ssd_conv_b[l], ssd_norm_w[l]], 1024),
                 _rows8([ssd_a_log[l], ssd_dt_bias[l], ssd_d[l]], 128)]
        ffn_p = [_rows8(list(ffn_cw[l]) + [ffn_conv_b[l]], 2 * FFN_HIDDEN)]
        hgrn_fn = make_hgrn_tile(l, depth)
        oa, st_a = seq_fwd("gdn_fwd" + sfx, gdn_tile, gdn_p, [pa], True, [(512, BF16)], (512, 128), nb, s, _R_GDN)
        ob, st_b = seq_fwd("hgrn_fwd" + sfx, hgrn_fn, hgrn_p, [pb], False, [(512, BF16)], (512, 128), nb, s, _R_HGRN)
        oc, st_c = seq_fwd("ssd_fwd" + sfx, ssd_tile, ssd_p, [pc], True, [(512, BF16)], (256, 256), nb, s, _R_SSD)
        x1 = merge_fwd("merge_fwd" + sfx, xl, oa, ob, oc, pg, gate1, wf["w_br_a"][l], wf["w_br_b"][l],
                       wf["w_br_c"][l], wf["w_out"][l], nb, s)
        h2 = normmod_fwd("norm2_fwd" + sfx, x1, norm2_w[l][None], shift2, scale2, nb, s)
        u = matmul("ffn_up" + sfx, h2, wf["ffn_w_up"][l], "nn", tn=FFN_HIDDEN)
        (act,) = seq_fwd("convglu_fwd" + sfx, convglu_tile, ffn_p, [u], True, [(FFN_HIDDEN, BF16)], None, nb, s, _R_FFN)
        f = matmul("ffn_down" + sfx, act, wf["ffn_w_down"][l], "nn")
        xl = resid_fwd("resid_fwd" + sfx, x1, f, gate2, nb, s)
        sv.update(h=h, pg=pg, pa=pa, pb=pb, pc=pc, oa=oa, ob=ob, oc=oc, st_a=st_a, st_b=st_b, st_c=st_c, x1=x1, h2=h2,
                  u=u, act=act, f=f, gdn_p=gdn_p, hgrn_p=hgrn_p, ssd_p=ssd_p, ffn_p=ffn_p, hgrn_fn=hgrn_fn)
        saved.append(sv)

    dx, loss_part, d_final = loss_head("loss_head", xl, final_norm_w[None], target)

    gfull = {name: [None] * depth for name, _ in _BIG}
    sg = {}
    dmod = [None] * depth
    d_lb = None
    for l in reversed(range(depth)):
        sfx = str(l)
        sv = saved[l]
        shift1, scale1, gate1, shift2, scale2, gate2 = sv["mods"]
        df, dgate2 = resid_bwd("resid_bwd" + sfx, dx, sv["f"], gate2, nb, s)
        dact = matmul("ffn_down_dx" + sfx, df, wf["ffn_w_down"][l], "nt")
        gfull["ffn_w_down"][l] = matmul("ffn_down_dw" + sfx, sv["act"], df, "tn", tm=1408, tk=512)
        (du,), (dcw,) = seq_bwd("convglu_bwd" + sfx, convglu_tile, sv["ffn_p"], [sv["u"]], True, None, [dact], [BF16],
                                None, nb, s, _R_FFN)
        dh2 = matmul("ffn_up_dx" + sfx, du, wf["ffn_w_up"][l], "nt", tk=FFN_HIDDEN)
        gfull["ffn_w_up"][l] = matmul("ffn_up_dw" + sfx, sv["h2"], du, "tn", tm=1024, tn=1408, tk=512)
        dx1, dnw2, dshift2, dscale2 = normmod_bwd("norm2_bwd" + sfx, sv["x1"], norm2_w[l][None], shift2, scale2, dh2, dx,
                                                  nb, s)
        doa, dob, doc, dpg, dgate1, dwa, dwb, dwc, dwo = merge_bwd(
            "merge_bwd" + sfx, sv["oa"], sv["ob"], sv["oc"], sv["pg"], gate1, wf["w_br_a"][l], wf["w_br_b"][l],
            wf["w_br_c"][l], wf["w_out"][l], dx1, nb, s)
        gfull["w_br_a"][l], gfull["w_br_b"][l], gfull["w_br_c"][l], gfull["w_out"][l] = dwa, dwb, dwc, dwo
        (dpa,), (dgcw, dgpk) = seq_bwd("gdn_bwd" + sfx, gdn_tile, sv["gdn_p"], [sv["pa"]], True, sv["st_a"], [doa],
                                       [F32], (512, 128), nb, s, _R_GDN)
        (dpb,), (dlbp, dhnw) = seq_bwd("hgrn_bwd" + sfx, sv["hgrn_fn"], sv["hgrn_p"], [sv["pb"]], False, sv["st_b"],
                                       [dob], [F32], (512, 128), nb, s, _R_HGRN)
        (dpc,), (dscw, dspv, dsps) = seq_bwd("ssd_bwd" + sfx, ssd_tile, sv["ssd_p"], [sv["pc"]], True, sv["st_c"],
                                             [doc], [F32], (256, 256), nb, s, _R_SSD)
        dh = matmul("proj_g_dx" + sfx, dpg, w_g[l], "nt")
        dh = matmul("proj_a_dx" + sfx, dpa, w_a[l], "nt", addend=dh)
        dh = matmul("proj_b_dx" + sfx, dpb, w_b[l], "nt", addend=dh)
        dh = matmul("proj_c_dx" + sfx, dpc, w_c[l], "nt", addend=dh)
        gfull["w_in"][l] = _join_w_in(
            matmul("proj_g_dw" + sfx, sv["h"], dpg, "tn", tm=1024, tn=1536, tk=512),
            matmul("proj_a_dw" + sfx, sv["h"], dpa, "tn", tm=1024, tk=512),
            matmul("proj_b_dw" + sfx, sv["h"], dpb, "tn", tm=1024, tn=1024, tk=512),
            matmul("proj_c_dw" + sfx, sv["h"], dpc, "tn", tm=1024, tk=512))
        dx, dnw1, dshift1, dscale1 = normmod_bwd("norm1_bwd" + sfx, sv["x0"], norm1_w[l][None], shift1, scale1, dh, dx1,
                                                 nb, s)
        dmod[l] = jnp.concatenate([dshift1, dscale1, dgate1, dshift2, dscale2, dgate2], axis=-1).reshape(nb, 6 * d)
        d_lb = dlbp[:depth] if d_lb is None else d_lb + dlbp[:depth]
        sg[l] = dict(norm1_w=dnw1[0], norm2_w=dnw2[0], gdn_conv_w=dgcw[:4], gdn_a_log=dgpk[0, :4],
                     gdn_dt_bias=dgpk[1, :4], gdn_norm_w=dgpk[2], hgrn_norm_w=dhnw[0], ssd_conv_w=dscw[:4],
                     ssd_conv_b=dspv[0], ssd_norm_w=dspv[1, :512], ssd_a_log=dsps[0, :8], ssd_dt_bias=dsps[1, :8],
                     ssd_d=dsps[2, :8], ffn_conv_w=dcw[:3], ffn_conv_b=dcw[3])
    grad_x = dx.reshape(nb, s, d)

    dmod = jnp.stack(dmod)
    (b_sum,) = elementwise("bias_rows", lambda *r: (functools.reduce(lambda p, q: p + q, r),),
                           [dmod[:, b].reshape(depth * 6, d) for b in range(nb)], [F32])
    per_layer = ("norm1_w", "norm2_w", "gdn_conv_w", "gdn_a_log", "gdn_dt_bias", "gdn_norm_w", "hgrn_norm_w",
                 "ssd_conv_w", "ssd_conv_b", "ssd_norm_w", "ssd_a_log", "ssd_dt_bias", "ssd_d", "ffn_conv_w", "ffn_conv_b")
    vals = {k: jnp.stack([sg[l][k] for l in range(depth)]) for k in per_layer}
    vals.update(loss=loss_part[0, :1], b_ada=b_sum.reshape(depth, 6 * d), hgrn_lb_param=d_lb, final_norm_w=d_final[0])
    gp = _Packer()
    for k, v in vals.items():
        gp.add(k, v.shape)
    packed8, dmod8 = allgather8("ag_grads", [gp.pack(vals), dmod.reshape(depth * nb, 6 * d)], [False, False])
    gs = gp.unpack(sum8("sum_small", packed8))
    loss = gs["loss"].reshape(())

    def my_cols(g):
        cs = g.shape[-1] // N_CHIPS
        return lax.dynamic_slice_in_dim(g, chip * cs, cs, axis=g.ndim - 1)

    for k in ("gdn_conv_w", "ssd_conv_w", "ffn_conv_w"):
        gs[k] = my_cols(gs[k])

    dmod_all = dmod8.reshape(N_DEV, depth, nb, 6 * d).transpose(1, 0, 2, 3).reshape(depth, N_DEV * nb, 6 * d)
    dmod_mine = lax.dynamic_slice_in_dim(dmod_all, chip * (6 * d // N_CHIPS), 6 * d // N_CHIPS, axis=2)
    g_w_ada = jnp.stack([matmul("ada_dw%d" % l, c_act, dmod_mine[l], "tn", tm=1024) for l in range(depth)])

    stacked = [_stack_by_chip(jnp.stack(gfull[name]), axis) for name, axis in _BIG]
    reduced = _reduce_scatter(stacked)
    grads = {name: r.reshape(w[name].shape) for (name, _), r in zip(_BIG, reduced)}
    grads["w_ada"] = g_w_ada
    for k in _SMALL:
        grads[k] = gs[k].reshape(w[k].shape)

    delta, new_m, new_v = {}, {}, {}
    for name in [n for n, _ in _BIG] + ["w_ada"]:
        shp = w[name].shape
        flat = lambda a: a.reshape(shp[0] * shp[1], shp[2])
        dl, nm, nv = adamw("adamw_" + name, flat(w[name]), flat(grads[name]), flat(mom[name]), flat(var[name]))
        delta[name], new_m[name], new_v[name] = dl.reshape(shp), nm.reshape(shp), nv.reshape(shp)
    sp = _Packer()
    for k in _SMALL:
        sp.add(k, w[k].shape)
    dl, nm, nv = adamw("adamw_small", sp.pack(w), sp.pack(grads), sp.pack(mom), sp.pack(var))
    delta.update(sp.unpack(dl))
    new_m.update(sp.unpack(nm))
    new_v.update(sp.unpack(nv))

    return (loss, grad_x, *[grads[k] for k in _WEIGHTS], *[delta[k] for k in _WEIGHTS],
            *[new_m[k] for k in _WEIGHTS], *[new_v[k] for k in _WEIGHTS])
```

```python
import functools

import jax
import jax.numpy as jnp
from jax import lax
from jax.experimental import pallas as pl
from jax.experimental.pallas import tpu as pltpu

F32 = jnp.float32
BF16 = jnp.bfloat16
HI = lax.Precision.HIGHEST
MESH = pl.DeviceIdType.MESH

EPS = 1e-6
D_MODEL = 1024
GDN_HEADS, GDN_DK, GDN_CHUNK = 4, 128, 64
HGRN_HEADS, HGRN_DK, HGRN_CHUNK = 4, 128, 16
SSD_HEADS, SSD_P, SSD_GROUPS, SSD_STATE, SSD_CHUNK = 8, 64, 2, 128, 64
FFN_HIDDEN = 2816
N_CHIPS = 4
N_DEV = 8

ADAM_LR, ADAM_B1, ADAM_B2, ADAM_EPS, ADAM_WD, ADAM_STEP = 0.001, 0.9, 0.999, 1e-08, 0.01, 10

W_G, W_A, W_B, W_C = 3072, 2176, 2048, 1664
VMEM_LIMIT = 56 * 1024 * 1024


def _cparams(sem):
    return pltpu.CompilerParams(dimension_semantics=sem, vmem_limit_bytes=VMEM_LIMIT)


def _dg(a, b, ca, cb):
    return lax.dot_general(a.astype(BF16), b.astype(BF16), (((ca,), (cb,)), ((), ())),
                           preferred_element_type=F32)


@jax.custom_vjp
def bdot(a, b):
    return _dg(a, b, 1, 0)


bdot.defvjp(lambda a, b: (_dg(a, b, 1, 0), (a, b)),
            lambda r, g: (_dg(g, r[1], 1, 1), _dg(r[0], g, 0, 0)))


@jax.custom_vjp
def bdot_nt(a, b):
    return _dg(a, b, 1, 1)


bdot_nt.defvjp(lambda a, b: (_dg(a, b, 1, 1), (a, b)),
               lambda r, g: (_dg(g, r[1], 1, 0), _dg(g, r[0], 0, 0)))


@jax.custom_vjp
def bdot_tn(a, b):
    return _dg(a, b, 0, 0)


bdot_tn.defvjp(lambda a, b: (_dg(a, b, 0, 0), (a, b)),
               lambda r, g: (_dg(r[1], g, 1, 1), _dg(r[0], g, 1, 0)))


def _split(x, n):
    parts, rest = [], x
    for _ in range(n):
        p = rest.astype(BF16)
        parts.append(p)
        rest = rest - p.astype(F32)
    return parts


def _dgb(a, b, ca, cb):
    return lax.dot_general(a, b, (((ca,), (cb,)), ((), ())), preferred_element_type=F32)


def _dg3(a, b, ca, cb):
    (ah, al), (bh, bl) = _split(a, 2), _split(b, 2)
    return _dgb(jnp.concatenate([ah, ah, al], axis=ca), jnp.concatenate([bh, bl, bh], axis=cb), ca, cb)


@jax.custom_vjp
def hdot(a, b):
    return _dg3(a, b, 1, 0)


hdot.defvjp(lambda a, b: (_dg3(a, b, 1, 0), (a, b)),
            lambda r, g: (_dg3(g, r[1], 1, 1), _dg3(r[0], g, 0, 0)))


def _dge(e, x, ce, cx, e_first):
    eb = e.astype(BF16)
    es = jnp.concatenate([eb, eb, eb], axis=ce)
    xs = jnp.concatenate(_split(x, 3), axis=cx)
    return _dgb(es, xs, ce, cx) if e_first else _dgb(xs, es, cx, ce)


@jax.custom_vjp
def ldot(e, x):
    return _dge(e, x, 1, 0, True)


ldot.defvjp(lambda e, x: (_dge(e, x, 1, 0, True), e),
            lambda e, g: (jnp.zeros_like(e), _dge(e, g, 0, 0, True)))


@jax.custom_vjp
def rdot(x, e):
    return _dge(e, x, 0, 1, False)


rdot.defvjp(lambda x, e: (_dge(e, x, 0, 1, False), e),
            lambda e, g: (_dge(e, g, 1, 1, False), jnp.zeros_like(e)))


def _sigmoid(x):
    return 1.0 / (1.0 + jnp.exp(-x))


def _silu(x):
    return x * _sigmoid(x)


def _softplus(x):
    return jnp.maximum(x, 0.0) + jnp.log(1.0 + jnp.exp(-jnp.abs(x)))


def _rms(x, w):
    return x * lax.rsqrt(jnp.mean(x * x, axis=-1, keepdims=True) + EPS) * w


def _iota(shape, dim):
    return lax.broadcasted_iota(jnp.int32, shape, dim)


def _tri_ones(n, chunk, kind):
    i, j = _iota((n, n), 0), _iota((n, n), 1)
    same = lax.div(i, chunk) == lax.div(j, chunk)
    if kind == "incl":
        m = same & (j <= i)
    elif kind == "strict":
        m = same & (j < i)
    elif kind == "all":
        m = same
    else:
        m = same & (lax.rem(j, chunk) < (chunk // 2))
    return m


def _causal_conv(w, halo, x, width):
    r = x.shape[0]
    xin = jnp.concatenate([halo, x], axis=0)
    y = w[width - 1:width, :] * x
    for k in range(width - 1):
        off = 8 - (width - 1) + k
        y = y + w[k:k + 1, :] * xin[off:off + r, :]
    return y


def _each(fn, *lists):
    return [fn(*a) for a in zip(*lists)]


def _neumann(ms):
    n = ms[0].shape[0]
    eye = (_iota((n, n), 0) == _iota((n, n), 1)).astype(F32)
    accs = [eye - m for m in ms]
    ps = ms
    steps = 1
    while steps * 2 < n:
        ps = _each(hdot, ps, ps)
        accs = [acc + ap for acc, ap in zip(accs, _each(hdot, accs, ps))]
        steps *= 2
    return accs


@jax.custom_vjp
def tri_inverse(ms):
    return _neumann(ms)


def _tri_inverse_fwd(ms):
    ainvs = _neumann(ms)
    return ainvs, ainvs


def _tri_inverse_bwd(ainvs, gs):
    t = _each(lambda g, a: _dg3(g, a, 1, 1), gs, ainvs)
    return ([-x for x in _each(lambda a, y: _dg3(a, y, 0, 0), ainvs, t)],)


tri_inverse.defvjp(_tri_inverse_fwd, _tri_inverse_bwd)


def gdn_tile(params, state, ins, halos):
    conv_w, pk = params
    (pa,), (ha,) = ins, halos
    r = pa.shape[0]
    c, nh, dk = GDN_CHUNK, GDN_HEADS, GDN_DK
    kw = nh * dk
    qkv = _silu(_causal_conv(conv_w, ha[:, :3 * kw], pa[:, :3 * kw], 4))
    z = pa[:, 3 * kw:4 * kw]
    gsm = pa[:, 4 * kw:]
    a_log, dtb, nw = pk[0:1, :], pk[1:2, :], pk[2:3, :]
    g_all = -jnp.exp(a_log) * _softplus(gsm + dtb)
    beta_all = _sigmoid(gsm)
    incl = _tri_ones(c, c, "incl")
    strict = _tri_ones(c, c, "strict")
    lmat = incl.astype(F32)
    scale = dk ** -0.5
    nck = r // c
    inst = [(ci, h) for ci in range(nck) for h in range(nh)]

    def l2n(v):
        return v * lax.rsqrt(jnp.sum(v * v, axis=-1, keepdims=True) + EPS)

    gcs = [ldot(lmat, g_all[ci * c:(ci + 1) * c, :]) for ci in range(nck)]
    gcts = [g.T for g in gcs]
    g_col = [gcs[ci][:, h:h + 1] for ci, h in inst]
    g_row = [gcts[ci][h:h + 1, :] for ci, h in inst]
    g_last = [gcs[ci][c - 1:c, h:h + 1] for ci, h in inst]
    beta = [beta_all[ci * c:(ci + 1) * c, nh + h:nh + h + 1] for ci, h in inst]
    qh = [l2n(qkv[ci * c:(ci + 1) * c, h * dk:(h + 1) * dk]) for ci, h in inst]
    kh = [l2n(qkv[ci * c:(ci + 1) * c, kw + h * dk:kw + (h + 1) * dk]) for ci, h in inst]
    vh = [qkv[ci * c:(ci + 1) * c, 2 * kw + h * dk:2 * kw + (h + 1) * dk] for ci, h in inst]
    decay = [jnp.where(incl, jnp.exp(jnp.where(incl, gc_ - gr_, 0.0)), 0.0) for gc_, gr_ in zip(g_col, g_row)]
    kb = [k * b for k, b in zip(kh, beta)]
    qs = [q * scale for q in qh]
    kk = _each(lambda a, b, k: bdot_nt(jnp.concatenate([a, b], axis=0), k), kb, qs, kh)
    ms = [jnp.where(strict, x[:c] * d, 0.0) for x, d in zip(kk, decay)]
    attn = [x[c:] * d for x, d in zip(kk, decay)]
    ainv = tri_inverse(ms)
    eg = [jnp.exp(g) for g in g_col]
    rhs = [jnp.concatenate([v * b, k_ * e], axis=1) for v, b, k_, e in zip(vh, beta, kb, eg)]
    sol = _each(hdot, ainv, rhs)
    qg = [q * e for q, e in zip(qs, eg)]
    k_end = [k * jnp.exp(gl - g) for k, gl, g in zip(kh, g_last, g_col)]
    e_last = [jnp.exp(gl) for gl in g_last]

    st = [state[h * dk:(h + 1) * dk, :] for h in range(nh)]
    outs = [[] for _ in range(nh)]
    for ci in range(nck):
        idx = [ci * nh + h for h in range(nh)]
        ws = [bdot(jnp.concatenate([sol[i][:, dk:], qg[i]], axis=0), st[h]) for h, i in enumerate(idx)]
        v_new = [sol[i][:, :dk] - w_[:c] for i, w_ in zip(idx, ws)]
        av = [bdot(attn[i], v) for i, v in zip(idx, v_new)]
        kv = [bdot_tn(k_end[i], v) for i, v in zip(idx, v_new)]
        for h, i in enumerate(idx):
            o = ws[h][c:] + av[h]
            st[h] = st[h] * e_last[i] + kv[h]
            outs[h].append(_rms(o, nw) * _silu(z[ci * c:(ci + 1) * c, h * dk:(h + 1) * dk]))
    out = jnp.concatenate([jnp.concatenate(o, axis=0) for o in outs], axis=1)
    return jnp.concatenate(st, axis=0), [out]


def make_hgrn_tile(layer, depth):
    def hgrn_tile(params, state, ins, halos):
        lbp, nwp = params
        (pb,) = ins
        r = pb.shape[0]
        c = HGRN_CHUNK
        kw = HGRN_HEADS * HGRN_DK
        rows = [lbp[i:i + 1, :] for i in range(depth)]
        mx = functools.reduce(jnp.maximum, rows)
        ex = [jnp.exp(x - mx) for x in rows]
        den = functools.reduce(lambda a, b: a + b, ex)
        soft = [e / den for e in ex]
        lb = functools.reduce(lambda a, b: a + b, soft[:layer + 1]) - soft[0]
        nw = nwp[0:1, :]
        q = _silu(pb[:, :kw])
        fr = pb[:, kw:2 * kw]
        logf = jnp.log(lb + (1.0 - lb) * _sigmoid(fr))
        k = (1.0 - lb) * _sigmoid(-fr)
        v = pb[:, 2 * kw:3 * kw]
        gate = pb[:, 3 * kw:]
        incl = _tri_ones(r, c, "incl")
        masks = jnp.concatenate([incl.astype(F32), _tri_ones(r, c, "upto").astype(F32),
                                 _tri_ones(r, c, "all").astype(F32)], axis=0)
        sums = ldot(masks, logf)
        g_cum, g_ref, g_end = sums[:r], sums[r:2 * r], sums[2 * r:]
        qs = q * jnp.exp(g_cum - g_ref)
        ks = k * jnp.exp(g_ref - g_cum)
        qg = q * jnp.exp(g_cum)
        k_end = k * jnp.exp(g_end - g_cum)
        e_end = jnp.exp(g_end)
        sls = [slice(h * HGRN_DK, (h + 1) * HGRN_DK) for h in range(HGRN_HEADS)]
        attn = [jnp.where(incl, bdot_nt(qs[:, sl], ks[:, sl]), 0.0) for sl in sls]
        o_intra = [bdot(a, v[:, sl]) for a, sl in zip(attn, sls)]
        s_t = [state[sl, :] for sl in sls]
        o_inter = [[] for _ in sls]
        for j in range(r // c):
            rs = slice(j * c, (j + 1) * c)
            oi = [bdot_nt(qg[rs, sl], s) for sl, s in zip(sls, s_t)]
            kv = [bdot_tn(v[rs, sl], k_end[rs, sl]) for sl in sls]
            s_t = [s * e_end[j * c:j * c + 1, sl] + x for s, sl, x in zip(s_t, sls, kv)]
            for lst, x in zip(o_inter, oi):
                lst.append(x)
        outs = [_rms(oa + jnp.concatenate(ob, axis=0), nw) * _silu(gate[:, sl])
                for oa, ob, sl in zip(o_intra, o_inter, sls)]
        return jnp.concatenate(s_t, axis=0), [jnp.concatenate(outs, axis=1)]
    return hgrn_tile


def ssd_tile(params, state, ins, halos):
    conv_w, pv, ps = params
    (pc,), (hc,) = ins, halos
    r = pc.shape[0]
    c = SSD_CHUNK
    inner = SSD_HEADS * SSD_P
    gw = inner // SSD_GROUPS
    z = pc[:, :inner]
    xbc = _silu(_causal_conv(conv_w, hc[:, inner:inner + 1024], pc[:, inner:inner + 1024], 4) + pv[0:1, :])
    ssm = pc[:, inner + 1024:]
    xs = xbc[:, :inner]
    bm = xbc[:, inner:inner + SSD_GROUPS * SSD_STATE]
    cm = xbc[:, inner + SSD_GROUPS * SSD_STATE:]
    a_log, dtb, dsk = ps[0:1, :], ps[1:2, :], ps[2:3, :]
    nw = pv[1:2, :inner]
    dt = _softplus(ssm + dtb)
    da = dt * (-jnp.exp(a_log))
    expand = (lax.div(_iota((128, inner), 1), SSD_P) == _iota((128, inner), 0)).astype(F32)
    xdt = xs * rdot(dt, expand)
    d_e = rdot(jnp.concatenate([dsk] * 8, axis=0), expand)[0:1, :]
    incl = _tri_ones(c, c, "incl")
    lmat = incl.astype(F32)
    st = [state[g * SSD_STATE:(g + 1) * SSD_STATE, :] for g in range(SSD_GROUPS)]
    hpg = SSD_HEADS // SSD_GROUPS
    nck = r // c
    groups = range(SSD_GROUPS)
    cg = [(ci, g) for ci in range(nck) for g in groups]
    rows = [slice(ci * c, (ci + 1) * c) for ci in range(nck)]
    gls = [slice(g * gw, (g + 1) * gw) for g in groups]
    acs = [ldot(lmat, da[rs, :]) for rs in rows]
    acs_t = [a.T for a in acs]
    acs_e = [rdot(a, expand) for a in acs]
    last_e = [a[c - 1:c, :] for a in acs_e]
    bm_g = [bm[rows[ci], g * SSD_STATE:(g + 1) * SSD_STATE] for ci, g in cg]
    cm_g = [cm[rows[ci], g * SSD_STATE:(g + 1) * SSD_STATE] for ci, g in cg]
    cb = _each(bdot_nt, cm_g, bm_g)
    heads = [(i, ci, g * hpg + hg) for i, (ci, g) in enumerate(cg) for hg in range(hpg)]
    seg = [jnp.where(incl, jnp.exp(jnp.where(incl, acs[ci][:, hh:hh + 1] - acs_t[ci][hh:hh + 1, :], 0.0)), 0.0)
           for _, ci, hh in heads]
    yd = [bdot(cb[i] * sg, xdt[rows[ci], hh * SSD_P:(hh + 1) * SSD_P]) for (i, ci, hh), sg in zip(heads, seg)]
    y_diag = [jnp.concatenate(yd[i * hpg:(i + 1) * hpg], axis=1) for i in range(len(cg))]
    xw = [xdt[rows[ci], gls[g]] * jnp.exp(last_e[ci][:, gls[g]] - acs_e[ci][:, gls[g]]) for ci, g in cg]
    e_acs = [jnp.exp(acs_e[ci][:, gls[g]]) for ci, g in cg]
    e_last = [jnp.exp(last_e[ci][:, gls[g]]) for ci, g in cg]
    kv = _each(bdot_tn, bm_g, xw)
    ys = []
    for ci in range(nck):
        idx = [ci * SSD_GROUPS + g for g in groups]
        y_off = [bdot(cm_g[i], st[g]) * e_acs[i] for g, i in zip(groups, idx)]
        st = [st[g] * e_last[i] + kv[i] for g, i in zip(groups, idx)]
        ys.append(jnp.concatenate([y_diag[i] + yo for i, yo in zip(idx, y_off)], axis=1))
    y = jnp.concatenate(ys, axis=0) + d_e * xs
    yz = y * _silu(z)
    out = jnp.concatenate([_rms(yz[:, g * gw:(g + 1) * gw], nw[:, g * gw:(g + 1) * gw])
                           for g in range(SSD_GROUPS)], axis=1)
    return jnp.concatenate(st, axis=0), [out]


def convglu_tile(params, state, ins, halos):
    (cw,) = params
    (u,), (hu,) = ins, halos
    y = _causal_conv(cw, hu, u, 3) + cw[3:4, :]
    return None, [_silu(y[:, :FFN_HIDDEN]) * y[:, FFN_HIDDEN:]]


def _halo_map(nt, r):
    return lambda b, n: (jnp.maximum((b * nt + n) * (r // 8) - 1, 0), 0)


def seq_fwd(name, tile_fn, params, ins, use_halo, out_specs, state_shape, nb, s, r):
    nt = s // r
    n_p, n_i, n_o = len(params), len(ins), len(out_specs)
    has_state = state_shape is not None

    def body(*refs):
        p_refs, i_refs = refs[:n_p], refs[n_p:n_p + n_i]
        h_refs = refs[n_p + n_i:n_p + 2 * n_i] if use_halo else ()
        k = n_p + n_i + len(h_refs)
        o_refs = refs[k:k + n_o]
        n = pl.program_id(1)
        state = None
        if has_state:
            sv_ref, st_ref = refs[k + n_o], refs[k + n_o + 1]

            @pl.when(n == 0)
            def _():
                st_ref[...] = jnp.zeros(state_shape, F32)

            state = st_ref[...]
            sv_ref[0, 0] = state
        pv = [p[...] for p in p_refs]
        iv = [i[...].astype(F32) for i in i_refs]
        hv = [jnp.where(n > 0, h[...].astype(F32), 0.0) for h in h_refs]
        new_state, ov = tile_fn(pv, state, iv, hv)
        for o_ref, o in zip(o_refs, ov):
            o_ref[...] = o.astype(o_ref.dtype)
        if has_state:
            st_ref[...] = new_state

    row = lambda b, n: (b * nt + n, 0)
    in_specs = [pl.BlockSpec(p.shape, lambda b, n: (0, 0)) for p in params]
    in_specs += [pl.BlockSpec((r, a.shape[1]), row) for a in ins]
    if use_halo:
        in_specs += [pl.BlockSpec((8, a.shape[1]), _halo_map(nt, r)) for a in ins]
    out_shape = [jax.ShapeDtypeStruct((nb * s, w), dt) for w, dt in out_specs]
    o_specs = [pl.BlockSpec((r, w), row) for w, _ in out_specs]
    scratch = []
    if has_state:
        out_shape.append(jax.ShapeDtypeStruct((nb, nt) + tuple(state_shape), F32))
        o_specs.append(pl.BlockSpec((1, 1) + tuple(state_shape), lambda b, n: (b, n, 0, 0)))
        scratch.append(pltpu.VMEM(tuple(state_shape), F32))
    args = list(params) + list(ins) + (list(ins) if use_halo else [])
    return pl.pallas_call(body, grid=(nb, nt), in_specs=in_specs, out_specs=o_specs, out_shape=out_shape,
                          scratch_shapes=scratch, compiler_params=_cparams(("arbitrary", "arbitrary")),
                          name=name)(*args)


def seq_bwd(name, tile_fn, params, ins, use_halo, states, douts, din_dtypes, state_shape, nb, s, r):
    nt = s // r
    n_p, n_i, n_o = len(params), len(ins), len(douts)
    has_state = state_shape is not None

    def body(*refs):
        p_refs, i_refs = refs[:n_p], refs[n_p:n_p + n_i]
        h_refs = refs[n_p + n_i:n_p + 2 * n_i] if use_halo else ()
        k = n_p + n_i + len(h_refs)
        sv_ref = None
        if has_state:
            sv_ref = refs[k]
            k += 1
        do_refs = refs[k:k + n_o]
        k += n_o
        di_refs, dp_refs = refs[k:k + n_i], refs[k + n_i:k + n_i + n_p]
        k += n_i + n_p
        dst_ref = None
        if has_state:
            dst_ref = refs[k]
            k += 1
        dh_refs = refs[k:k + len(h_refs)]
        b, nn = pl.program_id(0), pl.program_id(1)
        n = nt - 1 - nn

        @pl.when((b == 0) & (nn == 0))
        def _():
            for dp in dp_refs:
                dp[...] = jnp.zeros(dp.shape, F32)

        @pl.when(nn == 0)
        def _():
            if has_state:
                dst_ref[...] = jnp.zeros(state_shape, F32)
            for dh in dh_refs:
                dh[...] = jnp.zeros(dh.shape, F32)

        pv = [p[...] for p in p_refs]
        iv = [i[...].astype(F32) for i in i_refs]
        hv = [jnp.where(n > 0, h[...].astype(F32), 0.0) for h in h_refs]
        if has_state:
            f = lambda pv_, st_, iv_, hv_: tile_fn(pv_, st_, iv_, hv_)
            _, vjp = jax.vjp(f, pv, sv_ref[0, 0], iv, hv)
            dpv, dst, div, dhv = vjp((dst_ref[...], [d[...].astype(F32) for d in do_refs]))
            dst_ref[...] = dst
        else:
            f = lambda pv_, iv_, hv_: tile_fn(pv_, None, iv_, hv_)[1]
            _, vjp = jax.vjp(f, pv, iv, hv)
            dpv, div, dhv = vjp([d[...].astype(F32) for d in do_refs])
        for j, (di_ref, d) in enumerate(zip(di_refs, div)):
            if use_halo:
                d = jnp.concatenate([d[:r - 8], d[r - 8:] + dh_refs[j][...]], axis=0)
            di_ref[...] = d.astype(di_ref.dtype)
        for dh_ref, d in zip(dh_refs, dhv):
            dh_ref[...] = d
        for dp_ref, d in zip(dp_refs, dpv):
            dp_ref[...] += d

    row = lambda b, nn: (b * nt + nt - 1 - nn, 0)
    hmap = _halo_map(nt, r)
    in_specs = [pl.BlockSpec(p.shape, lambda b, nn: (0, 0)) for p in params]
    in_specs += [pl.BlockSpec((r, a.shape[1]), row) for a in ins]
    if use_halo:
        in_specs += [pl.BlockSpec((8, a.shape[1]), lambda b, nn: hmap(b, nt - 1 - nn)) for a in ins]
    args = list(params) + list(ins) + (list(ins) if use_halo else [])
    scratch = []
    if has_state:
        in_specs.append(pl.BlockSpec((1, 1) + tuple(state_shape), lambda b, nn: (b, nt - 1 - nn, 0, 0)))
        args.append(states)
        scratch.append(pltpu.VMEM(tuple(state_shape), F32))
    in_specs += [pl.BlockSpec((r, d.shape[1]), row) for d in douts]
    args += list(douts)
    if use_halo:
        scratch += [pltpu.VMEM((8, a.shape[1]), F32) for a in ins]
    out_shape = [jax.ShapeDtypeStruct(a.shape, dt) for a, dt in zip(ins, din_dtypes)]
    out_shape += [jax.ShapeDtypeStruct(p.shape, F32) for p in params]
    o_specs = [pl.BlockSpec((r, a.shape[1]), row) for a in ins]
    o_specs += [pl.BlockSpec(p.shape, lambda b, nn: (0, 0)) for p in params]
    res = pl.pallas_call(body, grid=(nb, nt), in_specs=in_specs, out_specs=o_specs, out_shape=out_shape,
                         scratch_shapes=scratch, compiler_params=_cparams(("arbitrary", "arbitrary")),
                         name=name)(*args)
    return res[:n_i], res[n_i:]


def matmul(name, a, b, mode, out_dtype=F32, addend=None, tm=512, tn=None, tk=None):
    if mode == "nn":
        (m, kd), (_, n) = a.shape, b.shape
    elif mode == "nt":
        (m, kd), (n, _) = a.shape, b.shape
    else:
        (kd, m), (_, n) = a.shape, b.shape
    tm, tn, tk = min(tm, m), tn or n, tk or kd
    nk = kd // tk
    assert m % tm == 0 and n % tn == 0 and kd % tk == 0
    dims = {"nn": ((1,), (0,)), "nt": ((1,), (1,)), "tn": ((0,), (0,))}[mode]
    has_add = addend is not None

    def body(*refs):
        a_ref, b_ref = refs[0], refs[1]
        add_ref = refs[2] if has_add else None
        o_ref = refs[2 + has_add]
        part = lax.dot_general(a_ref[...].astype(BF16), b_ref[...].astype(BF16), (dims, ((), ())),
                               preferred_element_type=F32)

        def finish(acc):
            if has_add:
                acc = acc + add_ref[...]
            o_ref[...] = acc.astype(o_ref.dtype)

        if nk == 1:
            finish(part)
        else:
            acc_ref = refs[3 + has_add]
            k = pl.program_id(2)

            @pl.when(k == 0)
            def _():
                acc_ref[...] = part

            @pl.when(k > 0)
            def _():
                acc_ref[...] += part

            @pl.when(k == nk - 1)
            def _():
                finish(acc_ref[...])

    if mode == "tn":
        a_spec = pl.BlockSpec((tk, tm), lambda j, i, k: (k, i))
    else:
        a_spec = pl.BlockSpec((tm, tk), lambda j, i, k: (i, k))
    if mode == "nt":
        b_spec = pl.BlockSpec((tn, tk), lambda j, i, k: (j, k))
    else:
        b_spec = pl.BlockSpec((tk, tn), lambda j, i, k: (k, j))
    o_spec = pl.BlockSpec((tm, tn), lambda j, i, k: (i, j))
    in_specs, args = [a_spec, b_spec], [a, b]
    if has_add:
        in_specs.append(o_spec)
        args.append(addend)
    scratch = [pltpu.VMEM((tm, tn), F32)] if nk > 1 else []
    return pl.pallas_call(body, grid=(n // tn, m // tm, nk), in_specs=in_specs, out_specs=o_spec,
                          out_shape=jax.ShapeDtypeStruct((m, n), out_dtype), scratch_shapes=scratch,
                          compiler_params=_cparams(("parallel", "parallel", "arbitrary")), name=name)(*args)


def _normmod(x, nw, shift, scale):
    return _rms(x, nw) * (1.0 + scale) + shift


def _row_specs(nb, s, tr, d):
    nt = s // tr
    row = pl.BlockSpec((tr, d), lambda b, i: (b * nt + i, 0))
    per_seq = pl.BlockSpec((1, 1, d), lambda b, i: (b, 0, 0))
    full = pl.BlockSpec((1, d), lambda b, i: (0, 0))
    return nt, row, per_seq, full


def normmod_fwd(name, x, nw, shift, scale, nb, s, tr=512):
    d, tr = x.shape[1], min(tr, s)
    nt, row, per_seq, full = _row_specs(nb, s, tr, d)

    def body(x_ref, nw_ref, sh_ref, sc_ref, h_ref):
        h_ref[...] = _normmod(x_ref[...], nw_ref[...], sh_ref[0], sc_ref[0]).astype(h_ref.dtype)

    return pl.pallas_call(body, grid=(nb, nt), in_specs=[row, full, per_seq, per_seq], out_specs=row,
                          out_shape=jax.ShapeDtypeStruct(x.shape, BF16),
                          compiler_params=_cparams(("parallel", "parallel")), name=name)(x, nw, shift, scale)


def normmod_bwd(name, x, nw, shift, scale, dh, dres, nb, s, tr=512):
    d, tr = x.shape[1], min(tr, s)
    nt, row, per_seq, full = _row_specs(nb, s, tr, d)

    def body(x_ref, nw_ref, sh_ref, sc_ref, dh_ref, dres_ref, dx_ref, dnw_ref, dsh_ref, dsc_ref):
        b, i = pl.program_id(0), pl.program_id(1)

        @pl.when((b == 0) & (i == 0))
        def _():
            dnw_ref[...] = jnp.zeros(dnw_ref.shape, F32)

        @pl.when(i == 0)
        def _():
            dsh_ref[...] = jnp.zeros(dsh_ref.shape, F32)
            dsc_ref[...] = jnp.zeros(dsc_ref.shape, F32)

        _, vjp = jax.vjp(_normmod, x_ref[...], nw_ref[...], sh_ref[0], sc_ref[0])
        dx, dnw, dsh, dsc = vjp(dh_ref[...])
        dx_ref[...] = dres_ref[...] + dx
        dnw_ref[...] += dnw
        dsh_ref[0] += dsh
        dsc_ref[0] += dsc

    out_shape = [jax.ShapeDtypeStruct(x.shape, F32), jax.ShapeDtypeStruct((1, d), F32),
                 jax.ShapeDtypeStruct((nb, 1, d), F32), jax.ShapeDtypeStruct((nb, 1, d), F32)]
    return pl.pallas_call(body, grid=(nb, nt), in_specs=[row, full, per_seq, per_seq, row, row],
                          out_specs=[row, full, per_seq, per_seq], out_shape=out_shape,
                          compiler_params=_cparams(("arbitrary", "arbitrary")),
                          name=name)(x, nw, shift, scale, dh, dres)


def _merge(oa, ob, oc, graw, gate1, wa, wb, wc, wo):
    d = wo.shape[0]
    g = _sigmoid(graw)
    merged = g[:, :d] * bdot(oa, wa) + g[:, d:2 * d] * bdot(ob, wb) + g[:, 2 * d:] * bdot(oc, wc)
    return gate1 * bdot(merged, wo)


def _merge_specs(nb, s, tr, d, wbr):
    nt, row, per_seq, _ = _row_specs(nb, s, tr, d)
    o_spec = pl.BlockSpec((tr, wbr), lambda b, i: (b * nt + i, 0))
    g_spec = pl.BlockSpec((tr, 3 * d), lambda b, i: (b * nt + i, 0))
    wbr_spec = pl.BlockSpec((wbr, d), lambda b, i: (0, 0))
    wo_spec = pl.BlockSpec((d, d), lambda b, i: (0, 0))
    return nt, row, per_seq, o_spec, g_spec, wbr_spec, wo_spec


def merge_fwd(name, x, oa, ob, oc, pg, gate1, wa, wb, wc, wo, nb, s, tr=512):
    d, tr = x.shape[1], min(tr, s)
    nt, row, per_seq, o_spec, g_spec, wbr_spec, wo_spec = _merge_specs(nb, s, tr, d, oa.shape[1])

    def body(x_ref, oa_ref, ob_ref, oc_ref, pg_ref, g1_ref, wa_ref, wb_ref, wc_ref, wo_ref, x1_ref):
        x1_ref[...] = x_ref[...] + _merge(oa_ref[...], ob_ref[...], oc_ref[...], pg_ref[...], g1_ref[0],
                                          wa_ref[...], wb_ref[...], wc_ref[...], wo_ref[...])

    return pl.pallas_call(body, grid=(nb, nt),
                          in_specs=[row, o_spec, o_spec, o_spec, g_spec, per_seq, wbr_spec, wbr_spec, wbr_spec, wo_spec],
                          out_specs=row, out_shape=jax.ShapeDtypeStruct(x.shape, F32),
                          compiler_params=_cparams(("parallel", "parallel")),
                          name=name)(x, oa, ob, oc, pg, gate1, wa, wb, wc, wo)


def merge_bwd(name, oa, ob, oc, pg, gate1, wa, wb, wc, wo, dx1, nb, s, tr=256):
    d, tr = dx1.shape[1], min(tr, s)
    wbr = oa.shape[1]
    nt, row, per_seq, o_spec, g_spec, wbr_spec, wo_spec = _merge_specs(nb, s, tr, d, wbr)

    def body(oa_ref, ob_ref, oc_ref, pg_ref, g1_ref, wa_ref, wb_ref, wc_ref, wo_ref, dx_ref,
             doa_ref, dob_ref, doc_ref, dpg_ref, dg1_ref, dwa_ref, dwb_ref, dwc_ref, dwo_ref):
        b, i = pl.program_id(0), pl.program_id(1)

        @pl.when((b == 0) & (i == 0))
        def _():
            for r in (dwa_ref, dwb_ref, dwc_ref, dwo_ref):
                r[...] = jnp.zeros(r.shape, F32)

        @pl.when(i == 0)
        def _():
            dg1_ref[...] = jnp.zeros(dg1_ref.shape, F32)

        args = [oa_ref[...].astype(F32), ob_ref[...].astype(F32), oc_ref[...].astype(F32), pg_ref[...], g1_ref[0],
                wa_ref[...].astype(F32), wb_ref[...].astype(F32), wc_ref[...].astype(F32), wo_ref[...].astype(F32)]
        _, vjp = jax.vjp(_merge, *args)
        doa, dob, doc, dpg, dg1, dwa, dwb, dwc, dwo = vjp(dx_ref[...])
        doa_ref[...] = doa
        dob_ref[...] = dob
        doc_ref[...] = doc
        dpg_ref[...] = dpg
        dg1_ref[0] += dg1
        dwa_ref[...] += dwa
        dwb_ref[...] += dwb
        dwc_ref[...] += dwc
        dwo_ref[...] += dwo

    t = nb * s
    out_shape = ([jax.ShapeDtypeStruct((t, wbr), F32)] * 3
                 + [jax.ShapeDtypeStruct((t, 3 * d), F32), jax.ShapeDtypeStruct((nb, 1, d), F32)]
                 + [jax.ShapeDtypeStruct((wbr, d), F32)] * 3 + [jax.ShapeDtypeStruct((d, d), F32)])
    return pl.pallas_call(body, grid=(nb, nt),
                          in_specs=[o_spec, o_spec, o_spec, g_spec, per_seq, wbr_spec, wbr_spec, wbr_spec, wo_spec, row],
                          out_specs=[o_spec, o_spec, o_spec, g_spec, per_seq, wbr_spec, wbr_spec, wbr_spec, wo_spec],
                          out_shape=out_shape, compiler_params=_cparams(("arbitrary", "arbitrary")),
                          name=name)(oa, ob, oc, pg, gate1, wa, wb, wc, wo, dx1)


def resid_fwd(name, x, f, gate, nb, s, tr=512):
    d, tr = x.shape[1], min(tr, s)
    nt, row, per_seq, _ = _row_specs(nb, s, tr, d)

    def body(x_ref, f_ref, g_ref, o_ref):
        o_ref[...] = x_ref[...] + g_ref[0] * f_ref[...]

    return pl.pallas_call(body, grid=(nb, nt), in_specs=[row, row, per_seq], out_specs=row,
                          out_shape=jax.ShapeDtypeStruct(x.shape, F32),
                          compiler_params=_cparams(("parallel", "parallel")), name=name)(x, f, gate)


def resid_bwd(name, dx, f, gate, nb, s, tr=512):
    d, tr = dx.shape[1], min(tr, s)
    nt, row, per_seq, _ = _row_specs(nb, s, tr, d)

    def body(dx_ref, f_ref, g_ref, df_ref, dg_ref):
        @pl.when(pl.program_id(1) == 0)
        def _():
            dg_ref[...] = jnp.zeros(dg_ref.shape, F32)

        df_ref[...] = (g_ref[0] * dx_ref[...]).astype(df_ref.dtype)
        dg_ref[0] += jnp.sum(dx_ref[...] * f_ref[...], axis=0, keepdims=True)

    return pl.pallas_call(body, grid=(nb, nt), in_specs=[row, row, per_seq], out_specs=[row, per_seq],
                          out_shape=[jax.ShapeDtypeStruct(dx.shape, BF16), jax.ShapeDtypeStruct((nb, 1, d), F32)],
                          compiler_params=_cparams(("arbitrary", "arbitrary")), name=name)(dx, f, gate)


def loss_head(name, x, fw, target, tr=512):
    t, d = x.shape
    row = pl.BlockSpec((tr, d), lambda i: (i, 0))
    full = pl.BlockSpec((1, d), lambda i: (0, 0))

    def loss_fn(xv, fwv, tv):
        err = _rms(xv, fwv) - tv
        return 0.5 * jnp.sum(jnp.mean(err * err, axis=-1))

    def body(x_ref, fw_ref, t_ref, dx_ref, l_ref, dfw_ref):
        @pl.when(pl.program_id(0) == 0)
        def _():
            l_ref[...] = jnp.zeros(l_ref.shape, F32)
            dfw_ref[...] = jnp.zeros(dfw_ref.shape, F32)

        val, (dx, dfw) = jax.value_and_grad(loss_fn, argnums=(0, 1))(x_ref[...], fw_ref[...], t_ref[...])
        dx_ref[...] = dx
        l_ref[...] += val
        dfw_ref[...] += dfw

    return pl.pallas_call(body, grid=(t // tr,), in_specs=[row, full, row],
                          out_specs=[row, pl.BlockSpec((1, 128), lambda i: (0, 0)), full],
                          out_shape=[jax.ShapeDtypeStruct((t, d), F32), jax.ShapeDtypeStruct((1, 128), F32),
                                     jax.ShapeDtypeStruct((1, d), F32)],
                          compiler_params=_cparams(("arbitrary",)), name=name)(x, fw, target)


def _row_tile(rows, cols, n_arrays):
    budget = 24 * 1024 * 1024 // (8 * cols * max(n_arrays, 1))
    tr = rows
    while tr > max(budget, 16) and tr % 2 == 0 and (tr // 2) % 16 == 0:
        tr //= 2
    return tr


def elementwise(name, fn, ins, out_dtypes):
    rows, cols = ins[0].shape
    tr = _row_tile(rows, cols, len(ins) + len(out_dtypes))
    spec = pl.BlockSpec((tr, cols), lambda i: (i, 0))
    n_in = len(ins)

    def body(*refs):
        outs = fn(*[r[...] for r in refs[:n_in]])
        for o_ref, o in zip(refs[n_in:], outs):
            o_ref[...] = o.astype(o_ref.dtype)

    return pl.pallas_call(body, grid=(rows // tr,), in_specs=[spec] * n_in, out_specs=[spec] * len(out_dtypes),
                          out_shape=[jax.ShapeDtypeStruct((rows, cols), dt) for dt in out_dtypes],
                          compiler_params=_cparams(("parallel",)), name=name)(*ins)


def _adamw(w, g, m, v):
    m = ADAM_B1 * m + (1.0 - ADAM_B1) * g
    v = ADAM_B2 * v + (1.0 - ADAM_B2) * (g * g)
    m_hat = m / (1.0 - ADAM_B1 ** ADAM_STEP)
    v_hat = v / (1.0 - ADAM_B2 ** ADAM_STEP)
    delta = -ADAM_LR * (m_hat / (jnp.sqrt(v_hat) + ADAM_EPS) + ADAM_WD * w)
    return delta, m, v


def adamw(name, w, g, m, v):
    return elementwise(name, _adamw, [w, g, m, v], [F32, F32, F32])


_ANY = pl.BlockSpec(memory_space=pl.ANY)


def _coords():
    return lax.axis_index("x"), lax.axis_index("y"), lax.axis_index("c")


def allgather8(name, arrays, halves):
    n = len(arrays)

    def body(*refs):
        in_refs, out_refs = refs[:n], refs[n:2 * n]
        send_sems, recv_sems, local_sems = refs[2 * n:]
        x, y, c = _coords()
        me, sibling = (x, y, c), (x, y, 1 - c)
        chips = [(1 - x, y), (x, 1 - y), (1 - x, 1 - y)]

        def blk(i, px, py, pc):
            return out_refs[i].at[4 * px + 2 * py + pc]

        def piece(i):
            return in_refs[i].at[c] if halves[i] else in_refs[i]

        def copy(i, k, block, to, src=None):
            return pltpu.make_async_remote_copy(
                src_ref=blk(i, *block) if src is None else src, dst_ref=blk(i, *block),
                send_sem=send_sems.at[7 * i + k], recv_sem=recv_sems.at[7 * i + k],
                device_id=to, device_id_type=MESH)

        mine = [pltpu.make_async_copy(piece(i), blk(i, *me), local_sems.at[i]) for i in range(n)]
        for cp in mine:
            cp.start()
        first = []
        for i in range(n):
            first.append(copy(i, 0, me, sibling, src=piece(i)))
            first += [copy(i, 1 + j, me, (*chip, c), src=piece(i)) for j, chip in enumerate(chips)]
        for cp in first:
            cp.start()
        passed = []
        for j, chip in enumerate(chips):
            for i in range(n):
                copy(i, 1 + j, (*chip, c), me).wait_recv()
                fwd = copy(i, 4 + j, (*chip, c), sibling)
                fwd.start()
                passed.append(fwd)
        for i in range(n):
            copy(i, 0, sibling, me).wait_recv()
            for j, chip in enumerate(chips):
                copy(i, 4 + j, (*chip, 1 - c), me).wait_recv()
        for cp in first + passed:
            cp.wait_send()
        for cp in mine:
            cp.wait()

    out_shape = []
    for a, hv in zip(arrays, halves):
        out_shape.append(jax.ShapeDtypeStruct((N_DEV,) + tuple(a.shape[1:] if hv else a.shape), a.dtype))
    return pl.pallas_call(
        body, in_specs=[_ANY] * n, out_specs=[_ANY] * n, out_shape=out_shape,
        scratch_shapes=[pltpu.SemaphoreType.DMA((7 * n,)), pltpu.SemaphoreType.DMA((7 * n,)),
                        pltpu.SemaphoreType.DMA((n,))],
        name=name)(*arrays)


def exchange(name, ins, out_shapes, plan, local_plan=()):
    n_in, n_out = len(ins), len(out_shapes)

    def body(*refs):
        in_refs, out_refs = refs[:n_in], refs[n_in:n_in + n_out]
        send_sems, recv_sems, local_sems = refs[n_in + n_out:]
        x, y, c = _coords()
        copies = []
        for k, fn in enumerate(plan):
            src, dst, peer = fn(in_refs, out_refs, x, y, c)
            copies.append(pltpu.make_async_remote_copy(src_ref=src, dst_ref=dst, send_sem=send_sems.at[k],
                                                       recv_sem=recv_sems.at[k], device_id=peer,
                                                       device_id_type=MESH))
        local = []
        for k, fn in enumerate(local_plan):
            src, dst = fn(in_refs, out_refs, x, y, c)
            local.append(pltpu.make_async_copy(src, dst, local_sems.at[k]))
        for cp in copies + local:
            cp.start()
        for cp in copies + local:
            cp.wait()

    return pl.pallas_call(
        body, in_specs=[_ANY] * n_in, out_specs=[_ANY] * n_out, out_shape=out_shapes,
        scratch_shapes=[pltpu.SemaphoreType.DMA((len(plan),)), pltpu.SemaphoreType.DMA((len(plan),)),
                        pltpu.SemaphoreType.DMA((max(len(local_plan), 1),))],
        name=name)(*ins)


def sum_halves(name, gs, recv, c_arr):
    _, _, hr, cs = gs.shape
    tr = _row_tile(hr, cs, 4)

    def body(c_ref, g_ref, r_ref, qf_ref, qb_ref):
        q = g_ref[0, 0] + r_ref[0, 0]
        qf_ref[0] = q
        qb_ref[0] = q.astype(BF16)

    grid_spec = pltpu.PrefetchScalarGridSpec(
        num_scalar_prefetch=1, grid=(N_CHIPS, hr // tr),
        in_specs=[pl.BlockSpec((1, 1, tr, cs), lambda j, i, c_ref: (j, c_ref[0], i, 0)),
                  pl.BlockSpec((1, 1, tr, cs), lambda j, i, c_ref: (j, 0, i, 0))],
        out_specs=[pl.BlockSpec((1, tr, cs), lambda j, i, c_ref: (j, i, 0))] * 2)
    return pl.pallas_call(body, grid_spec=grid_spec,
                          out_shape=[jax.ShapeDtypeStruct((N_CHIPS, hr, cs), F32),
                                     jax.ShapeDtypeStruct((N_CHIPS, hr, cs), BF16)],
                          compiler_params=_cparams(("parallel", "parallel")), name=name)(c_arr, gs, recv)


def sum8(name, g):
    _, rows, cols = g.shape
    tr = _row_tile(rows, cols, 9)

    def body(*refs):
        acc = refs[0][0]
        for r in refs[1:N_DEV]:
            acc = acc + r[0]
        refs[N_DEV][...] = acc

    in_specs = [pl.BlockSpec((1, tr, cols), functools.partial(lambda k, i: (k, i, 0), k)) for k in range(N_DEV)]
    return pl.pallas_call(body, grid=(rows // tr,), in_specs=in_specs,
                          out_specs=pl.BlockSpec((tr, cols), lambda i: (i, 0)),
                          out_shape=jax.ShapeDtypeStruct((rows, cols), F32),
                          compiler_params=_cparams(("parallel",)), name=name)(*([g] * N_DEV))


_QKV, _AB, _GZ = (0, 1536), (1536, 1544), (1544, 2056)
_HG = (2056, 4104)
_SZ, _XBC, _DT = (4104, 4616), (4616, 5640), (5640, 5648)
_GATES = (5648, 8720)


def _cols(w, rng):
    return w[..., rng[0]:rng[1]]


def _split_w_in(w):
    pad = jnp.zeros(w.shape[:-1] + (120,), w.dtype)
    return (_cols(w, _GATES),
            jnp.concatenate([_cols(w, _QKV), _cols(w, _GZ), _cols(w, _AB), pad], axis=-1),
            _cols(w, _HG),
            jnp.concatenate([_cols(w, _SZ), _cols(w, _XBC), _cols(w, _DT), pad], axis=-1))


def _join_w_in(g, a, b, c):
    return jnp.concatenate([a[..., 0:1536], a[..., 2048:2056], a[..., 1536:2048], b,
                            c[..., 0:512], c[..., 512:1536], c[..., 1536:1544], g], axis=-1)


def _rows8(rows, width):
    out = [jnp.pad(r.astype(F32), (0, width - r.shape[0])) for r in rows]
    out += [jnp.zeros((width,), F32)] * (8 - len(out))
    return jnp.stack(out)


class _Packer:
    def __init__(self):
        self.items, self.size = [], 0

    def add(self, name, shape):
        n = 1
        for d in shape:
            n *= d
        self.items.append((name, tuple(shape), self.size, n))
        self.size += n

    def rows(self):
        return -(-self.size // 8192) * 8

    def pack(self, values):
        flat = [values[name].astype(F32).reshape(-1) for name, _, _, _ in self.items]
        flat.append(jnp.zeros((self.rows() * 1024 - self.size,), F32))
        return jnp.concatenate(flat).reshape(self.rows(), 1024)

    def unpack(self, buf):
        flat = buf.reshape(-1)
        return {name: flat[off:off + n].reshape(shape) for name, shape, off, n in self.items}


def _stack_by_chip(g, axis):
    l, r, c = g.shape
    if axis == 2:
        cs = c // N_CHIPS
        g = g.reshape(l, r, N_CHIPS, cs).transpose(2, 0, 1, 3).reshape(N_CHIPS, 2, l * r // 2, cs)
    else:
        rs = r // N_CHIPS
        g = g.reshape(l, N_CHIPS, rs, c).transpose(1, 0, 2, 3).reshape(N_CHIPS, 2, l * rs // 2, c)
    return g


def _unstack_gathered(w8, l, axis):
    _, hr, cs = w8.shape
    w = w8.reshape(N_CHIPS, l, 2 * hr // l, cs)
    if axis == 2:
        return w.transpose(1, 2, 0, 3).reshape(l, 2 * hr // l, N_CHIPS * cs)
    return w.transpose(1, 0, 2, 3).reshape(l, N_CHIPS * 2 * hr // l, cs)


_BIG = (("w_in", 2), ("w_br_a", 2), ("w_br_b", 2), ("w_br_c", 2), ("w_out", 1), ("ffn_w_up", 2), ("ffn_w_down", 1))
_SMALL = ("b_ada", "norm1_w", "gdn_conv_w", "gdn_a_log", "gdn_dt_bias", "gdn_norm_w", "hgrn_lb_param",
          "hgrn_norm_w", "ssd_conv_w", "ssd_conv_b", "ssd_a_log", "ssd_dt_bias", "ssd_d", "ssd_norm_w",
          "norm2_w", "ffn_conv_w", "ffn_conv_b", "final_norm_w")
_WEIGHTS = ("w_ada", "b_ada", "norm1_w", "w_in", "gdn_conv_w", "gdn_a_log", "gdn_dt_bias", "gdn_norm_w",
            "hgrn_lb_param", "hgrn_norm_w", "ssd_conv_w", "ssd_conv_b", "ssd_a_log", "ssd_dt_bias", "ssd_d",
            "ssd_norm_w", "w_br_a", "w_br_b", "w_br_c", "w_out", "norm2_w", "ffn_w_up", "ffn_conv_w",
            "ffn_conv_b", "ffn_w_down", "final_norm_w")
_R_GDN, _R_HGRN, _R_SSD, _R_FFN = 128, 128, 128, 256


def _reduce_scatter(grads):
    n = len(grads)
    c_arr = lax.axis_index("c").astype(jnp.int32).reshape(1)

    plan = [functools.partial(lambda i, ins, outs, x, y, c: (ins[i].at[:, pl.ds(1 - c, 1)], outs[i], (x, y, 1 - c)), i)
            for i in range(n)]
    recv = exchange("rs_d2d", grads, [jax.ShapeDtypeStruct((N_CHIPS, 1) + g.shape[2:], F32) for g in grads], plan)
    q = [sum_halves("rs_sum_d2d%d" % i, g, r, c_arr) for i, (g, r) in enumerate(zip(grads, recv))]
    qf, qb = [a for a, _ in q], [b for _, b in q]

    masks = ((1, 0), (0, 1), (1, 1))

    def ici(i, k, ins, outs, x, y, c):
        px = 1 - x if masks[k][0] else x
        py = 1 - y if masks[k][1] else y
        return ins[n + i].at[2 * px + py], outs[n + 3 * i + k], (px, py, c)

    def own(i, ins, outs, x, y, c):
        return ins[i].at[2 * x + y], outs[i]

    plan = [functools.partial(ici, i, k) for i in range(n) for k in range(3)]
    local = [functools.partial(own, i) for i in range(n)]
    shapes = [jax.ShapeDtypeStruct(g.shape[2:], F32) for g in grads]
    shapes += [jax.ShapeDtypeStruct(g.shape[2:], BF16) for g in grads for _ in range(3)]
    res = exchange("rs_ici", qf + qb, shapes, plan, local)
    red = []
    for i in range(n):
        (r,) = elementwise("rs_sum_ici%d" % i,
                           lambda o, a, b, c: (o + a.astype(F32) + b.astype(F32) + c.astype(F32),),
                           [res[i], res[n + 3 * i], res[n + 3 * i + 1], res[n + 3 * i + 2]], [F32])
        red.append(r)

    plan = [functools.partial(lambda i, ins, outs, x, y, c: (ins[i], outs[i].at[c], (x, y, 1 - c)), i) for i in range(n)]
    local = [functools.partial(lambda i, ins, outs, x, y, c: (ins[i], outs[i].at[c]), i) for i in range(n)]
    full = exchange("rs_swap", red, [jax.ShapeDtypeStruct((2,) + r.shape, F32) for r in red], plan, local)
    return [f.reshape(2 * f.shape[1], f.shape[2]) for f in full]


def kernel(x, c, w_ada, b_ada, norm1_w, w_in, gdn_conv_w, gdn_a_log, gdn_dt_bias, gdn_norm_w, hgrn_lb_param, hgrn_norm_w, ssd_conv_w, ssd_conv_b, ssd_a_log, ssd_dt_bias, ssd_d, ssd_norm_w, w_br_a, w_br_b, w_br_c, w_out, norm2_w, ffn_w_up, ffn_conv_w, ffn_conv_b, ffn_w_down, final_norm_w, loss_target, m_w_ada, m_b_ada, m_norm1_w, m_w_in, m_gdn_conv_w, m_gdn_a_log, m_gdn_dt_bias, m_gdn_norm_w, m_hgrn_lb_param, m_hgrn_norm_w, m_ssd_conv_w, m_ssd_conv_b, m_ssd_a_log, m_ssd_dt_bias, m_ssd_d, m_ssd_norm_w, m_w_br_a, m_w_br_b, m_w_br_c, m_w_out, m_norm2_w, m_ffn_w_up, m_ffn_conv_w, m_ffn_conv_b, m_ffn_w_down, m_final_norm_w, v_w_ada, v_b_ada, v_norm1_w, v_w_in, v_gdn_conv_w, v_gdn_a_log, v_gdn_dt_bias, v_gdn_norm_w, v_hgrn_lb_param, v_hgrn_norm_w, v_ssd_conv_w, v_ssd_conv_b, v_ssd_a_log, v_ssd_dt_bias, v_ssd_d, v_ssd_norm_w, v_w_br_a, v_w_br_b, v_w_br_c, v_w_out, v_norm2_w, v_ffn_w_up, v_ffn_conv_w, v_ffn_conv_b, v_ffn_w_down, v_final_norm_w):
    loc = dict(locals())
    w = {k: loc[k] for k in _WEIGHTS}
    mom = {k: loc["m_" + k] for k in _WEIGHTS}
    var = {k: loc["v_" + k] for k in _WEIGHTS}
    nb, s, d = x.shape
    t = nb * s
    depth = w_ada.shape[0]
    chip = 2 * lax.axis_index("x") + lax.axis_index("y")
    dev = 2 * chip + lax.axis_index("c")
    x0 = x.reshape(t, d)
    target = loss_target.reshape(t, d)

    small_in = [c, gdn_conv_w.reshape(depth * 4, -1), ssd_conv_w.reshape(depth * 4, -1),
                ffn_conv_w.reshape(depth * 3, -1)]
    c_all, gcw, scw, fcw = allgather8("ag_small", small_in, [False] * 4)
    c_all = c_all.reshape(N_DEV * nb, d)

    def conv_full(g, taps):
        g = g[::2].reshape(N_CHIPS, depth, taps, -1)
        return g.transpose(1, 2, 0, 3).reshape(depth, taps, -1)

    gdn_cw, ssd_cw, ffn_cw = conv_full(gcw, 4), conv_full(scw, 4), conv_full(fcw, 3)

    big_in = []
    for name, axis in _BIG:
        a = w[name].astype(BF16)
        big_in.append(a.reshape(2, a.shape[0] * a.shape[1] // 2, a.shape[2]))
    big = allgather8("ag_weights", big_in, [True] * len(_BIG))
    wf = {name: _unstack_gathered(g, depth, axis) for (name, axis), g in zip(_BIG, big)}
    w_g, w_a, w_b, w_c = _split_w_in(wf["w_in"])

    (c_act,) = elementwise("silu_c", lambda v: (_silu(v),), [c_all], [F32])
    mod_cols = jnp.concatenate([matmul("ada_fwd%d" % l, c_act, w_ada[l], "nn") for l in range(depth)], axis=0)
    (mod8,) = allgather8("ag_mod", [mod_cols], [False])
    mod = mod8[::2].reshape(N_CHIPS, depth, N_DEV * nb, -1).transpose(1, 2, 0, 3).reshape(depth, N_DEV * nb, 6 * d)
    mod = lax.dynamic_slice_in_dim(mod, dev * nb, nb, axis=1) + b_ada[:, None, :]

    def mod_part(l, k):
        return mod[l, :, k * d:(k + 1) * d].reshape(nb, 1, d)

    saved = []
    xl = x0
    for l in range(depth):
        sfx = str(l)
        sv = {"x0": xl}
        shift1, scale1, gate1, shift2, scale2, gate2 = [mod_part(l, k) for k in range(6)]
        sv["mods"] = (shift1, scale1, gate1, shift2, scale2, gate2)
        h = normmod_fwd("norm1_fwd" + sfx, xl, norm1_w[l][None], shift1, scale1, nb, s)
        pg = matmul("proj_g" + sfx, h, w_g[l], "nn")
        pa = matmul("proj_a" + sfx, h, w_a[l], "nn")
        pb = matmul("proj_b" + sfx, h, w_b[l], "nn")
        pc = matmul("proj_c" + sfx, h, w_c[l], "nn")
        gdn_p = [_rows8(list(gdn_cw[l]), 1536), _rows8([gdn_a_log[l], gdn_dt_bias[l], gdn_norm_w[l]], 128)]
        hgrn_p = [_rows8(list(hgrn_lb_param), 512), _rows8([hgrn_norm_w[l]], 128)]
        ssd_p = [_rows8(list(ssd_cw[l]), 1024), _rows8([ssd_conv_b[l], ssd_norm_w[l]], 1024),
                 _rows8([ssd_a_log[l], ssd_dt_bias[l], ssd_d[l]], 128)]
        ffn_p = [_rows8(list(ffn_cw[l]) + [ffn_conv_b[l]], 2 * FFN_HIDDEN)]
        hgrn_fn = make_hgrn_tile(l, depth)
        oa, st_a = seq_fwd("gdn_fwd" + sfx, gdn_tile, gdn_p, [pa], True, [(512, BF16)], (512, 128), nb, s, _R_GDN)
        ob, st_b = seq_fwd("hgrn_fwd" + sfx, hgrn_fn, hgrn_p, [pb], False, [(512, BF16)], (512, 128), nb, s, _R_HGRN)
        oc, st_c = seq_fwd("ssd_fwd" + sfx, ssd_tile, ssd_p, [pc], True, [(512, BF16)], (256, 256), nb, s, _R_SSD)
        x1 = merge_fwd("merge_fwd" + sfx, xl, oa, ob, oc, pg, gate1, wf["w_br_a"][l], wf["w_br_b"][l],
                       wf["w_br_c"][l], wf["w_out"][l], nb, s)
        h2 = normmod_fwd("norm2_fwd" + sfx, x1, norm2_w[l][None], shift2, scale2, nb, s)
        u = matmul("ffn_up" + sfx, h2, wf["ffn_w_up"][l], "nn", tn=FFN_HIDDEN)
        (act,) = seq_fwd("convglu_fwd" + sfx, convglu_tile, ffn_p, [u], True, [(FFN_HIDDEN, BF16)], None, nb, s, _R_FFN)
        f = matmul("ffn_down" + sfx, act, wf["ffn_w_down"][l], "nn")
        xl = resid_fwd("resid_fwd" + sfx, x1, f, gate2, nb, s)
        sv.update(h=h, pg=pg, pa=pa, pb=pb, pc=pc, oa=oa, ob=ob, oc=oc, st_a=st_a, st_b=st_b, st_c=st_c, x1=x1, h2=h2,
                  u=u, act=act, f=f, gdn_p=gdn_p, hgrn_p=hgrn_p, ssd_p=ssd_p, ffn_p=ffn_p, hgrn_fn=hgrn_fn)
        saved.append(sv)

    dx, loss_part, d_final = loss_head("loss_head", xl, final_norm_w[None], target)

    gfull = {name: [None] * depth for name, _ in _BIG}
    sg = {}
    dmod = [None] * depth
    d_lb = None
    for l in reversed(range(depth)):
        sfx = str(l)
        sv = saved[l]
        shift1, scale1, gate1, shift2, scale2, gate2 = sv["mods"]
        df, dgate2 = resid_bwd("resid_bwd" + sfx, dx, sv["f"], gate2, nb, s)
        dact = matmul("ffn_down_dx" + sfx, df, wf["ffn_w_down"][l], "nt")
        gfull["ffn_w_down"][l] = matmul("ffn_down_dw" + sfx, sv["act"], df, "tn", tm=1408, tk=512)
        (du,), (dcw,) = seq_bwd("convglu_bwd" + sfx, convglu_tile, sv["ffn_p"], [sv["u"]], True, None, [dact], [BF16],
                                None, nb, s, _R_FFN)
        dh2 = matmul("ffn_up_dx" + sfx, du, wf["ffn_w_up"][l], "nt", tk=FFN_HIDDEN)
        gfull["ffn_w_up"][l] = matmul("ffn_up_dw" + sfx, sv["h2"], du, "tn", tm=1024, tn=1408, tk=512)
        dx1, dnw2, dshift2, dscale2 = normmod_bwd("norm2_bwd" + sfx, sv["x1"], norm2_w[l][None], shift2, scale2, dh2, dx,
                                                  nb, s)
        doa, dob, doc, dpg, dgate1, dwa, dwb, dwc, dwo = merge_bwd(
            "merge_bwd" + sfx, sv["oa"], sv["ob"], sv["oc"], sv["pg"], gate1, wf["w_br_a"][l], wf["w_br_b"][l],
            wf["w_br_c"][l], wf["w_out"][l], dx1, nb, s)
        gfull["w_br_a"][l], gfull["w_br_b"][l], gfull["w_br_c"][l], gfull["w_out"][l] = dwa, dwb, dwc, dwo
        (dpa,), (dgcw, dgpk) = seq_bwd("gdn_bwd" + sfx, gdn_tile, sv["gdn_p"], [sv["pa"]], True, sv["st_a"], [doa],
                                       [F32], (512, 128), nb, s, _R_GDN)
        (dpb,), (dlbp, dhnw) = seq_bwd("hgrn_bwd" + sfx, sv["hgrn_fn"], sv["hgrn_p"], [sv["pb"]], False, sv["st_b"],
                                       [dob], [F32], (512, 128), nb, s, _R_HGRN)
        (dpc,), (dscw, dspv, dsps) = seq_bwd("ssd_bwd" + sfx, ssd_tile, sv["ssd_p"], [sv["pc"]], True, sv["st_c"],
                                             [doc], [F32], (256, 256), nb, s, _R_SSD)
        dh = matmul("proj_g_dx" + sfx, dpg, w_g[l], "nt")
        dh = matmul("proj_a_dx" + sfx, dpa, w_a[l], "nt", addend=dh)
        dh = matmul("proj_b_dx" + sfx, dpb, w_b[l], "nt", addend=dh)
        dh = matmul("proj_c_dx" + sfx, dpc, w_c[l], "nt", addend=dh)
        gfull["w_in"][l] = _join_w_in(
            matmul("proj_g_dw" + sfx, sv["h"], dpg, "tn", tm=1024, tn=1536, tk=512),
            matmul("proj_a_dw" + sfx, sv["h"], dpa, "tn", tm=1024, tk=512),
            matmul("proj_b_dw" + sfx, sv["h"], dpb, "tn", tm=1024, tn=1024, tk=512),
            matmul("proj_c_dw" + sfx, sv["h"], dpc, "tn", tm=1024, tk=512))
        dx, dnw1, dshift1, dscale1 = normmod_bwd("norm1_bwd" + sfx, sv["x0"], norm1_w[l][None], shift1, scale1, dh, dx1,
                                                 nb, s)
        dmod[l] = jnp.concatenate([dshift1, dscale1, dgate1, dshift2, dscale2, dgate2], axis=-1).reshape(nb, 6 * d)
        d_lb = dlbp[:depth] if d_lb is None else d_lb + dlbp[:depth]
        sg[l] = dict(norm1_w=dnw1[0], norm2_w=dnw2[0], gdn_conv_w=dgcw[:4], gdn_a_log=dgpk[0, :4],
                     gdn_dt_bias=dgpk[1, :4], gdn_norm_w=dgpk[2], hgrn_norm_w=dhnw[0], ssd_conv_w=dscw[:4],
                     ssd_conv_b=dspv[0], ssd_norm_w=dspv[1, :512], ssd_a_log=dsps[0, :8], ssd_dt_bias=dsps[1, :8],
                     ssd_d=dsps[2, :8], ffn_conv_w=dcw[:3], ffn_conv_b=dcw[3])
    grad_x = dx.reshape(nb, s, d)

    dmod = jnp.stack(dmod)
    (b_sum,) = elementwise("bias_rows", lambda *r: (functools.reduce(lambda p, q: p + q, r),),
                           [dmod[:, b].reshape(depth * 6, d) for b in range(nb)], [F32])
    per_layer = ("norm1_w", "norm2_w", "gdn_conv_w", "gdn_a_log", "gdn_dt_bias", "gdn_norm_w", "hgrn_norm_w",
                 "ssd_conv_w", "ssd_conv_b", "ssd_norm_w", "ssd_a_log", "ssd_dt_bias", "ssd_d", "ffn_conv_w", "ffn_conv_b")
    vals = {k: jnp.stack([sg[l][k] for l in range(depth)]) for k in per_layer}
    vals.update(loss=loss_part[0, :1], b_ada=b_sum.reshape(depth, 6 * d), hgrn_lb_param=d_lb, final_norm_w=d_final[0])
    gp = _Packer()
    for k, v in vals.items():
        gp.add(k, v.shape)
    packed8, dmod8 = allgather8("ag_grads", [gp.pack(vals), dmod.reshape(depth * nb, 6 * d)], [False, False])
    gs = gp.unpack(sum8("sum_small", packed8))
    loss = gs["loss"].reshape(())

    def my_cols(g):
        cs = g.shape[-1] // N_CHIPS
        return lax.dynamic_slice_in_dim(g, chip * cs, cs, axis=g.ndim - 1)

    for k in ("gdn_conv_w", "ssd_conv_w", "ffn_conv_w"):
        gs[k] = my_cols(gs[k])

    dmod_all = dmod8.reshape(N_DEV, depth, nb, 6 * d).transpose(1, 0, 2, 3).reshape(depth, N_DEV * nb, 6 * d)
    dmod_mine = lax.dynamic_slice_in_dim(dmod_all, chip * (6 * d // N_CHIPS), 6 * d // N_CHIPS, axis=2)
    g_w_ada = jnp.stack([matmul("ada_dw%d" % l, c_act, dmod_mine[l], "tn", tm=1024) for l in range(depth)])

    stacked = [_stack_by_chip(jnp.stack(gfull[name]), axis) for name, axis in _BIG]
    reduced = _reduce_scatter(stacked)
    grads = {name: r.reshape(w[name].shape) for (name, _), r in zip(_BIG, reduced)}
    grads["w_ada"] = g_w_ada
    for k in _SMALL:
        grads[k] = gs[k].reshape(w[k].shape)

    delta, new_m, new_v = {}, {}, {}
    for name in [n for n, _ in _BIG] + ["w_ada"]:
        shp = w[name].shape
        flat = lambda a: a.reshape(shp[0] * shp[1], shp[2])
        dl, nm, nv = adamw("adamw_" + name, flat(w[name]), flat(grads[name]), flat(mom[name]), flat(var[name]))
        delta[name], new_m[name], new_v[name] = dl.reshape(shp), nm.reshape(shp), nv.reshape(shp)
    sp = _Packer()
    for k in _SMALL:
        sp.add(k, w[k].shape)
    dl, nm, nv = adamw("adamw_small", sp.pack(w), sp.pack(grads), sp.pack(mom), sp.pack(var))
    delta.update(sp.unpack(dl))
    new_m.update(sp.unpack(nm))
    new_v.update(sp.unpack(nv))

    return (loss, grad_x, *[grads[k] for k in _WEIGHTS], *[delta[k] for k in _WEIGHTS],
            *[new_m[k] for k in _WEIGHTS], *[new_v[k] for k in _WEIGHTS])
```

```python
import functools

import jax
import jax.numpy as jnp
from jax import lax
from jax.experimental import pallas as pl
from jax.experimental.pallas import tpu as pltpu

F32 = jnp.float32
BF16 = jnp.bfloat16
HI = lax.Precision.HIGHEST
MESH = pl.DeviceIdType.MESH

EPS = 1e-6
D_MODEL = 1024
GDN_HEADS, GDN_DK, GDN_CHUNK = 4, 128, 64
HGRN_HEADS, HGRN_DK, HGRN_CHUNK = 4, 128, 16
SSD_HEADS, SSD_P, SSD_GROUPS, SSD_STATE, SSD_CHUNK = 8, 64, 2, 128, 64
FFN_HIDDEN = 2816
N_CHIPS = 4
N_DEV = 8

ADAM_LR, ADAM_B1, ADAM_B2, ADAM_EPS, ADAM_WD, ADAM_STEP = 0.001, 0.9, 0.999, 1e-08, 0.01, 10

W_G, W_A, W_B, W_C = 3072, 2176, 2048, 1664
VMEM_LIMIT = 56 * 1024 * 1024


def _cparams(sem):
    return pltpu.CompilerParams(dimension_semantics=sem, vmem_limit_bytes=VMEM_LIMIT)


def _dg(a, b, ca, cb):
    return lax.dot_general(a.astype(BF16), b.astype(BF16), (((ca,), (cb,)), ((), ())),
                           preferred_element_type=F32)


@jax.custom_vjp
def bdot(a, b):
    return _dg(a, b, 1, 0)


bdot.defvjp(lambda a, b: (_dg(a, b, 1, 0), (a, b)),
            lambda r, g: (_dg(g, r[1], 1, 1), _dg(r[0], g, 0, 0)))


@jax.custom_vjp
def bdot_nt(a, b):
    return _dg(a, b, 1, 1)


bdot_nt.defvjp(lambda a, b: (_dg(a, b, 1, 1), (a, b)),
               lambda r, g: (_dg(g, r[1], 1, 0), _dg(g, r[0], 0, 0)))


@jax.custom_vjp
def bdot_tn(a, b):
    return _dg(a, b, 0, 0)


bdot_tn.defvjp(lambda a, b: (_dg(a, b, 0, 0), (a, b)),
               lambda r, g: (_dg(r[1], g, 1, 1), _dg(r[0], g, 1, 0)))


def _split(x, n):
    parts, rest = [], x
    for _ in range(n):
        p = rest.astype(BF16)
        parts.append(p)
        rest = rest - p.astype(F32)
    return parts


def _dgb(a, b, ca, cb):
    return lax.dot_general(a, b, (((ca,), (cb,)), ((), ())), preferred_element_type=F32)


def _dg3(a, b, ca, cb):
    (ah, al), (bh, bl) = _split(a, 2), _split(b, 2)
    return _dgb(jnp.concatenate([ah, ah, al], axis=ca), jnp.concatenate([bh, bl, bh], axis=cb), ca, cb)


@jax.custom_vjp
def hdot(a, b):
    return _dg3(a, b, 1, 0)


hdot.defvjp(lambda a, b: (_dg3(a, b, 1, 0), (a, b)),
            lambda r, g: (_dg3(g, r[1], 1, 1), _dg3(r[0], g, 0, 0)))


def _dge(e, x, ce, cx, e_first):
    eb = e.astype(BF16)
    es = jnp.concatenate([eb, eb, eb], axis=ce)
    xs = jnp.concatenate(_split(x, 3), axis=cx)
    return _dgb(es, xs, ce, cx) if e_first else _dgb(xs, es, cx, ce)


@jax.custom_vjp
def ldot(e, x):
    return _dge(e, x, 1, 0, True)


ldot.defvjp(lambda e, x: (_dge(e, x, 1, 0, True), e),
            lambda e, g: (jnp.zeros_like(e), _dge(e, g, 0, 0, True)))


@jax.custom_vjp
def rdot(x, e):
    return _dge(e, x, 0, 1, False)


rdot.defvjp(lambda x, e: (_dge(e, x, 0, 1, False), e),
            lambda e, g: (_dge(e, g, 1, 1, False), jnp.zeros_like(e)))


def _sigmoid(x):
    return 1.0 / (1.0 + jnp.exp(-x))


def _silu(x):
    return x * _sigmoid(x)


def _softplus(x):
    return jnp.maximum(x, 0.0) + jnp.log(1.0 + jnp.exp(-jnp.abs(x)))


def _rms(x, w):
    return x * lax.rsqrt(jnp.mean(x * x, axis=-1, keepdims=True) + EPS) * w


def _iota(shape, dim):
    return lax.broadcasted_iota(jnp.int32, shape, dim)


def _tri_ones(n, chunk, kind):
    i, j = _iota((n, n), 0), _iota((n, n), 1)
    same = lax.div(i, chunk) == lax.div(j, chunk)
    if kind == "incl":
        m = same & (j <= i)
    elif kind == "strict":
        m = same & (j < i)
    elif kind == "all":
        m = same
    else:
        m = same & (lax.rem(j, chunk) < (chunk // 2))
    return m


def _causal_conv(w, halo, x, width):
    r = x.shape[0]
    xin = jnp.concatenate([halo, x], axis=0)
    y = w[width - 1:width, :] * x
    for k in range(width - 1):
        off = 8 - (width - 1) + k
        y = y + w[k:k + 1, :] * xin[off:off + r, :]
    return y


def _each(fn, *lists):
    return [fn(*a) for a in zip(*lists)]


def _neumann(ms):
    n = ms[0].shape[0]
    eye = (_iota((n, n), 0) == _iota((n, n), 1)).astype(F32)
    accs = [eye - m for m in ms]
    ps = ms
    steps = 1
    while steps * 2 < n:
        ps = _each(hdot, ps, ps)
        accs = [acc + ap for acc, ap in zip(accs, _each(hdot, accs, ps))]
        steps *= 2
    return accs


@jax.custom_vjp
def tri_inverse(ms):
    return _neumann(ms)


def _tri_inverse_fwd(ms):
    ainvs = _neumann(ms)
    return ainvs, ainvs


def _tri_inverse_bwd(ainvs, gs):
    t = _each(lambda g, a: _dg3(g, a, 1, 1), gs, ainvs)
    return ([-x for x in _each(lambda a, y: _dg3(a, y, 0, 0), ainvs, t)],)


tri_inverse.defvjp(_tri_inverse_fwd, _tri_inverse_bwd)


def gdn_tile(params, state, ins, halos):
    conv_w, pk = params
    (pa,), (ha,) = ins, halos
    r = pa.shape[0]
    c, nh, dk = GDN_CHUNK, GDN_HEADS, GDN_DK
    kw = nh * dk
    qkv = _silu(_causal_conv(conv_w, ha[:, :3 * kw], pa[:, :3 * kw], 4))
    z = pa[:, 3 * kw:4 * kw]
    gsm = pa[:, 4 * kw:]
    a_log, dtb, nw = pk[0:1, :], pk[1:2, :], pk[2:3, :]
    g_all = -jnp.exp(a_log) * _softplus(gsm + dtb)
    beta_all = _sigmoid(gsm)
    incl = _tri_ones(c, c, "incl")
    strict = _tri_ones(c, c, "strict")
    lmat = incl.astype(F32)
    scale = dk ** -0.5
    nck = r // c
    inst = [(ci, h) for ci in range(nck) for h in range(nh)]

    def l2n(v):
        return v * lax.rsqrt(jnp.sum(v * v, axis=-1, keepdims=True) + EPS)

    gcs = [ldot(lmat, g_all[ci * c:(ci + 1) * c, :]) for ci in range(nck)]
    gcts = [g.T for g in gcs]
    g_col = [gcs[ci][:, h:h + 1] for ci, h in inst]
    g_row = [gcts[ci][h:h + 1, :] for ci, h in inst]
    g_last = [gcs[ci][c - 1:c, h:h + 1] for ci, h in inst]
    beta = [beta_all[ci * c:(ci + 1) * c, nh + h:nh + h + 1] for ci, h in inst]
    qh = [l2n(qkv[ci * c:(ci + 1) * c, h * dk:(h + 1) * dk]) for ci, h in inst]
    kh = [l2n(qkv[ci * c:(ci + 1) * c, kw + h * dk:kw + (h + 1) * dk]) for ci, h in inst]
    vh = [qkv[ci * c:(ci + 1) * c, 2 * kw + h * dk:2 * kw + (h + 1) * dk] for ci, h in inst]
    decay = [jnp.where(incl, jnp.exp(jnp.where(incl, gc_ - gr_, 0.0)), 0.0) for gc_, gr_ in zip(g_col, g_row)]
    kb = [k * b for k, b in zip(kh, beta)]
    qs = [q * scale for q in qh]
    kk = _each(lambda a, b, k: bdot_nt(jnp.concatenate([a, b], axis=0), k), kb, qs, kh)
    ms = [jnp.where(strict, x[:c] * d, 0.0) for x, d in zip(kk, decay)]
    attn = [x[c:] * d for x, d in zip(kk, decay)]
    ainv = tri_inverse(ms)
    eg = [jnp.exp(g) for g in g_col]
    rhs = [jnp.concatenate([v * b, k_ * e], axis=1) for v, b, k_, e in zip(vh, beta, kb, eg)]
    sol = _each(hdot, ainv, rhs)
    qg = [q * e for q, e in zip(qs, eg)]
    k_end = [k * jnp.exp(gl - g) for k, gl, g in zip(kh, g_last, g_col)]
    e_last = [jnp.exp(gl) for gl in g_last]

    st = [state[h * dk:(h + 1) * dk, :] for h in range(nh)]
    outs = [[] for _ in range(nh)]
    for ci in range(nck):
        idx = [ci * nh + h for h in range(nh)]
        ws = [bdot(jnp.concatenate([sol[i][:, dk:], qg[i]], axis=0), st[h]) for h, i in enumerate(idx)]
        v_new = [sol[i][:, :dk] - w_[:c] for i, w_ in zip(idx, ws)]
        av = [bdot(attn[i], v) for i, v in zip(idx, v_new)]
        kv = [bdot_tn(k_end[i], v) for i, v in zip(idx, v_new)]
        for h, i in enumerate(idx):
            o = ws[h][c:] + av[h]
            st[h] = st[h] * e_last[i] + kv[h]
            outs[h].append(_rms(o, nw) * _silu(z[ci * c:(ci + 1) * c, h * dk:(h + 1) * dk]))
    out = jnp.concatenate([jnp.concatenate(o, axis=0) for o in outs], axis=1)
    return jnp.concatenate(st, axis=0), [out]


def make_hgrn_tile(layer, depth):
    def hgrn_tile(params, state, ins, halos):
        lbp, nwp = params
        (pb,) = ins
        r = pb.shape[0]
        c = HGRN_CHUNK
        kw = HGRN_HEADS * HGRN_DK
        rows = [lbp[i:i + 1, :] for i in range(depth)]
        mx = functools.reduce(jnp.maximum, rows)
        ex = [jnp.exp(x - mx) for x in rows]
        den = functools.reduce(lambda a, b: a + b, ex)
        soft = [e / den for e in ex]
        lb = functools.reduce(lambda a, b: a + b, soft[:layer + 1]) - soft[0]
        nw = nwp[0:1, :]
        q = _silu(pb[:, :kw])
        fr = pb[:, kw:2 * kw]
        logf = jnp.log(lb + (1.0 - lb) * _sigmoid(fr))
        k = (1.0 - lb) * _sigmoid(-fr)
        v = pb[:, 2 * kw:3 * kw]
        gate = pb[:, 3 * kw:]
        incl = _tri_ones(r, c, "incl")
        masks = jnp.concatenate([incl.astype(F32), _tri_ones(r, c, "upto").astype(F32),
                                 _tri_ones(r, c, "all").astype(F32)], axis=0)
        sums = ldot(masks, logf)
        g_cum, g_ref, g_end = sums[:r], sums[r:2 * r], sums[2 * r:]
        qs = q * jnp.exp(g_cum - g_ref)
        ks = k * jnp.exp(g_ref - g_cum)
        qg = q * jnp.exp(g_cum)
        k_end = k * jnp.exp(g_end - g_cum)
        e_end = jnp.exp(g_end)
        sls = [slice(h * HGRN_DK, (h + 1) * HGRN_DK) for h in range(HGRN_HEADS)]
        attn = [jnp.where(incl, bdot_nt(qs[:, sl], ks[:, sl]), 0.0) for sl in sls]
        o_intra = [bdot(a, v[:, sl]) for a, sl in zip(attn, sls)]
        s_t = [state[sl, :] for sl in sls]
        o_inter = [[] for _ in sls]
        for j in range(r // c):
            rs = slice(j * c, (j + 1) * c)
            oi = [bdot_nt(qg[rs, sl], s) for sl, s in zip(sls, s_t)]
            kv = [bdot_tn(v[rs, sl], k_end[rs, sl]) for sl in sls]
            s_t = [s * e_end[j * c:j * c + 1, sl] + x for s, sl, x in zip(s_t, sls, kv)]
            for lst, x in zip(o_inter, oi):
                lst.append(x)
        outs = [_rms(oa + jnp.concatenate(ob, axis=0), nw) * _silu(gate[:, sl])
                for oa, ob, sl in zip(o_intra, o_inter, sls)]
        return jnp.concatenate(s_t, axis=0), [jnp.concatenate(outs, axis=1)]
    return hgrn_tile


def ssd_tile(params, state, ins, halos):
    conv_w, pv, ps = params
    (pc,), (hc,) = ins, halos
    r = pc.shape[0]
    c = SSD_CHUNK
    inner = SSD_HEADS * SSD_P
    gw = inner // SSD_GROUPS
    z = pc[:, :inner]
    xbc = _silu(_causal_conv(conv_w, hc[:, inner:inner + 1024], pc[:, inner:inner + 1024], 4) + pv[0:1, :])
    ssm = pc[:, inner + 1024:]
    xs = xbc[:, :inner]
    bm = xbc[:, inner:inner + SSD_GROUPS * SSD_STATE]
    cm = xbc[:, inner + SSD_GROUPS * SSD_STATE:]
    a_log, dtb, dsk = ps[0:1, :], ps[1:2, :], ps[2:3, :]
    nw = pv[1:2, :inner]
    dt = _softplus(ssm + dtb)
    da = dt * (-jnp.exp(a_log))
    expand = (lax.div(_iota((128, inner), 1), SSD_P) == _iota((128, inner), 0)).astype(F32)
    xdt = xs * rdot(dt, expand)
    d_e = rdot(jnp.concatenate([dsk] * 8, axis=0), expand)[0:1, :]
    incl = _tri_ones(c, c, "incl")
    lmat = incl.astype(F32)
    st = [state[g * SSD_STATE:(g + 1) * SSD_STATE, :] for g in range(SSD_GROUPS)]
    hpg = SSD_HEADS // SSD_GROUPS
    nck = r // c
    groups = range(SSD_GROUPS)
    cg = [(ci, g) for ci in range(nck) for g in groups]
    rows = [slice(ci * c, (ci + 1) * c) for ci in range(nck)]
    gls = [slice(g * gw, (g + 1) * gw) for g in groups]
    acs = [ldot(lmat, da[rs, :]) for rs in rows]
    acs_t = [a.T for a in acs]
    acs_e = [rdot(a, expand) for a in acs]
    last_e = [a[c - 1:c, :] for a in acs_e]
    bm_g = [bm[rows[ci], g * SSD_STATE:(g + 1) * SSD_STATE] for ci, g in cg]
    cm_g = [cm[rows[ci], g * SSD_STATE:(g + 1) * SSD_STATE] for ci, g in cg]
    cb = _each(bdot_nt, cm_g, bm_g)
    heads = [(i, ci, g * hpg + hg) for i, (ci, g) in enumerate(cg) for hg in range(hpg)]
    seg = [jnp.where(incl, jnp.exp(jnp.where(incl, acs[ci][:, hh:hh + 1] - acs_t[ci][hh:hh + 1, :], 0.0)), 0.0)
           for _, ci, hh in heads]
    yd = [bdot(cb[i] * sg, xdt[rows[ci], hh * SSD_P:(hh + 1) * SSD_P]) for (i, ci, hh), sg in zip(heads, seg)]
    y_diag = [jnp.concatenate(yd[i * hpg:(i + 1) * hpg], axis=1) for i in range(len(cg))]
    xw = [xdt[rows[ci], gls[g]] * jnp.exp(last_e[ci][:, gls[g]] - acs_e[ci][:, gls[g]]) for ci, g in cg]
    e_acs = [jnp.exp(acs_e[ci][:, gls[g]]) for ci, g in cg]
    e_last = [jnp.exp(last_e[ci][:, gls[g]]) for ci, g in cg]
    kv = _each(bdot_tn, bm_g, xw)
    ys = []
    for ci in range(nck):
        idx = [ci * SSD_GROUPS + g for g in groups]
        y_off = [bdot(cm_g[i], st[g]) * e_acs[i] for g, i in zip(groups, idx)]
        st = [st[g] * e_last[i] + kv[i] for g, i in zip(groups, idx)]
        ys.append(jnp.concatenate([y_diag[i] + yo for i, yo in zip(idx, y_off)], axis=1))
    y = jnp.concatenate(ys, axis=0) + d_e * xs
    yz = y * _silu(z)
    out = jnp.concatenate([_rms(yz[:, g * gw:(g + 1) * gw], nw[:, g * gw:(g + 1) * gw])
                           for g in range(SSD_GROUPS)], axis=1)
    return jnp.concatenate(st, axis=0), [out]


def convglu_tile(params, state, ins, halos):
    (cw,) = params
    (u,), (hu,) = ins, halos
    y = _causal_conv(cw, hu, u, 3) + cw[3:4, :]
    return None, [_silu(y[:, :FFN_HIDDEN]) * y[:, FFN_HIDDEN:]]


def convglu_tile_t(params, state, ins, halos):
    _, (act,) = convglu_tile(params, state, ins, halos)
    return None, [act, act.T]


def _halo_map(nt, r):
    return lambda b, n: (jnp.maximum((b * nt + n) * (r // 8) - 1, 0), 0)


def seq_fwd(name, tile_fn, params, ins, use_halo, out_specs, state_shape, nb, s, r):
    nt = s // r
    n_p, n_i, n_o = len(params), len(ins), len(out_specs)
    has_state = state_shape is not None

    def body(*refs):
        p_refs, i_refs = refs[:n_p], refs[n_p:n_p + n_i]
        h_refs = refs[n_p + n_i:n_p + 2 * n_i] if use_halo else ()
        k = n_p + n_i + len(h_refs)
        o_refs = refs[k:k + n_o]
        n = pl.program_id(1)
        state = None
        if has_state:
            sv_ref, st_ref = refs[k + n_o], refs[k + n_o + 1]

            @pl.when(n == 0)
            def _():
                st_ref[...] = jnp.zeros(state_shape, F32)

            state = st_ref[...]
            sv_ref[0, 0] = state
        pv = [p[...] for p in p_refs]
        iv = [i[...].astype(F32) for i in i_refs]
        hv = [jnp.where(n > 0, h[...].astype(F32), 0.0) for h in h_refs]
        new_state, ov = tile_fn(pv, state, iv, hv)
        for o_ref, o in zip(o_refs, ov):
            o_ref[...] = o.astype(o_ref.dtype)
        if has_state:
            st_ref[...] = new_state

    row = lambda b, n: (b * nt + n, 0)
    in_specs = [pl.BlockSpec(p.shape, lambda b, n: (0, 0)) for p in params]
    in_specs += [pl.BlockSpec((r, a.shape[1]), row) for a in ins]
    if use_halo:
        in_specs += [pl.BlockSpec((8, a.shape[1]), _halo_map(nt, r)) for a in ins]
    col = lambda b, n: (0, b * nt + n)
    out_shape, o_specs = [], []
    for w, dt, *transposed in out_specs:
        out_shape.append(jax.ShapeDtypeStruct((w, nb * s) if transposed else (nb * s, w), dt))
        o_specs.append(pl.BlockSpec((w, r), col) if transposed else pl.BlockSpec((r, w), row))
    scratch = []
    if has_state:
        out_shape.append(jax.ShapeDtypeStruct((nb, nt) + tuple(state_shape), F32))
        o_specs.append(pl.BlockSpec((1, 1) + tuple(state_shape), lambda b, n: (b, n, 0, 0)))
        scratch.append(pltpu.VMEM(tuple(state_shape), F32))
    args = list(params) + list(ins) + (list(ins) if use_halo else [])
    return pl.pallas_call(body, grid=(nb, nt), in_specs=in_specs, out_specs=o_specs, out_shape=out_shape,
                          scratch_shapes=scratch, compiler_params=_cparams(("arbitrary", "arbitrary")),
                          name=name)(*args)


def seq_bwd(name, tile_fn, params, ins, use_halo, states, douts, din_dtypes, state_shape, nb, s, r):
    nt = s // r
    n_p, n_i, n_o = len(params), len(ins), len(douts)
    has_state = state_shape is not None

    def body(*refs):
        p_refs, i_refs = refs[:n_p], refs[n_p:n_p + n_i]
        h_refs = refs[n_p + n_i:n_p + 2 * n_i] if use_halo else ()
        k = n_p + n_i + len(h_refs)
        sv_ref = None
        if has_state:
            sv_ref = refs[k]
            k += 1
        do_refs = refs[k:k + n_o]
        k += n_o
        di_refs, dp_refs = refs[k:k + n_i], refs[k + n_i:k + n_i + n_p]
        k += n_i + n_p
        dst_ref = None
        if has_state:
            dst_ref = refs[k]
            k += 1
        dh_refs = refs[k:k + len(h_refs)]
        b, nn = pl.program_id(0), pl.program_id(1)
        n = nt - 1 - nn

        @pl.when((b == 0) & (nn == 0))
        def _():
            for dp in dp_refs:
                dp[...] = jnp.zeros(dp.shape, F32)

        @pl.when(nn == 0)
        def _():
            if has_state:
                dst_ref[...] = jnp.zeros(state_shape, F32)
            for dh in dh_refs:
                dh[...] = jnp.zeros(dh.shape, F32)

        pv = [p[...] for p in p_refs]
        iv = [i[...].astype(F32) for i in i_refs]
        hv = [jnp.where(n > 0, h[...].astype(F32), 0.0) for h in h_refs]
        if has_state:
            f = lambda pv_, st_, iv_, hv_: tile_fn(pv_, st_, iv_, hv_)
            _, vjp = jax.vjp(f, pv, sv_ref[0, 0], iv, hv)
            dpv, dst, div, dhv = vjp((dst_ref[...], [d[...].astype(F32) for d in do_refs]))
            dst_ref[...] = dst
        else:
            f = lambda pv_, iv_, hv_: tile_fn(pv_, None, iv_, hv_)[1]
            _, vjp = jax.vjp(f, pv, iv, hv)
            dpv, div, dhv = vjp([d[...].astype(F32) for d in do_refs])
        for j, (di_ref, d) in enumerate(zip(di_refs, div)):
            if use_halo:
                d = jnp.concatenate([d[:r - 8], d[r - 8:] + dh_refs[j][...]], axis=0)
            di_ref[...] = d.astype(di_ref.dtype)
        for dh_ref, d in zip(dh_refs, dhv):
            dh_ref[...] = d
        for dp_ref, d in zip(dp_refs, dpv):
            dp_ref[...] += d

    row = lambda b, nn: (b * nt + nt - 1 - nn, 0)
    hmap = _halo_map(nt, r)
    in_specs = [pl.BlockSpec(p.shape, lambda b, nn: (0, 0)) for p in params]
    in_specs += [pl.BlockSpec((r, a.shape[1]), row) for a in ins]
    if use_halo:
        in_specs += [pl.BlockSpec((8, a.shape[1]), lambda b, nn: hmap(b, nt - 1 - nn)) for a in ins]
    args = list(params) + list(ins) + (list(ins) if use_halo else [])
    scratch = []
    if has_state:
        in_specs.append(pl.BlockSpec((1, 1) + tuple(state_shape), lambda b, nn: (b, nt - 1 - nn, 0, 0)))
        args.append(states)
        scratch.append(pltpu.VMEM(tuple(state_shape), F32))
    in_specs += [pl.BlockSpec((r, d.shape[1]), row) for d in douts]
    args += list(douts)
    if use_halo:
        scratch += [pltpu.VMEM((8, a.shape[1]), F32) for a in ins]
    out_shape = [jax.ShapeDtypeStruct(a.shape, dt) for a, dt in zip(ins, din_dtypes)]
    out_shape += [jax.ShapeDtypeStruct(p.shape, F32) for p in params]
    o_specs = [pl.BlockSpec((r, a.shape[1]), row) for a in ins]
    o_specs += [pl.BlockSpec(p.shape, lambda b, nn: (0, 0)) for p in params]
    res = pl.pallas_call(body, grid=(nb, nt), in_specs=in_specs, out_specs=o_specs, out_shape=out_shape,
                         scratch_shapes=scratch, compiler_params=_cparams(("arbitrary", "arbitrary")),
                         name=name)(*args)
    return res[:n_i], res[n_i:]


def matmul(name, a, b, mode, out_dtype=F32, addend=None, tm=512, tn=None, tk=None):
    if mode == "nn":
        (m, kd), (_, n) = a.shape, b.shape
    elif mode == "nt":
        (m, kd), (n, _) = a.shape, b.shape
    else:
        (kd, m), (_, n) = a.shape, b.shape
    tm, tn, tk = min(tm, m), min(tn or n, n), min(tk or kd, kd)
    nk = kd // tk
    assert m % tm == 0 and n % tn == 0 and kd % tk == 0
    dims = {"nn": ((1,), (0,)), "nt": ((1,), (1,)), "tn": ((0,), (0,))}[mode]
    has_add = addend is not None

    def body(*refs):
        a_ref, b_ref = refs[0], refs[1]
        add_ref = refs[2] if has_add else None
        o_ref = refs[2 + has_add]
        part = lax.dot_general(a_ref[...].astype(BF16), b_ref[...].astype(BF16), (dims, ((), ())),
                               preferred_element_type=F32)

        def finish(acc):
            if has_add:
                acc = acc + add_ref[...]
            o_ref[...] = acc.astype(o_ref.dtype)

        if nk == 1:
            finish(part)
        else:
            acc_ref = refs[3 + has_add]
            k = pl.program_id(2)

            @pl.when(k == 0)
            def _():
                acc_ref[...] = part

            @pl.when(k > 0)
            def _():
                acc_ref[...] += part

            @pl.when(k == nk - 1)
            def _():
                finish(acc_ref[...])

    if mode == "tn":
        a_spec = pl.BlockSpec((tk, tm), lambda j, i, k: (k, i))
    else:
        a_spec = pl.BlockSpec((tm, tk), lambda j, i, k: (i, k))
    if mode == "nt":
        b_spec = pl.BlockSpec((tn, tk), lambda j, i, k: (j, k))
    else:
        b_spec = pl.BlockSpec((tk, tn), lambda j, i, k: (k, j))
    o_spec = pl.BlockSpec((tm, tn), lambda j, i, k: (i, j))
    in_specs, args = [a_spec, b_spec], [a, b]
    if has_add:
        in_specs.append(o_spec)
        args.append(addend)
    scratch = [pltpu.VMEM((tm, tn), F32)] if nk > 1 else []
    return pl.pallas_call(body, grid=(n // tn, m // tm, nk), in_specs=in_specs, out_specs=o_spec,
                          out_shape=jax.ShapeDtypeStruct((m, n), out_dtype), scratch_shapes=scratch,
                          compiler_params=_cparams(("parallel", "parallel", "arbitrary")), name=name)(*args)


def _normmod(x, nw, shift, scale):
    return _rms(x, nw) * (1.0 + scale) + shift


def _row_specs(nb, s, tr, d):
    nt = s // tr
    row = pl.BlockSpec((tr, d), lambda b, i: (b * nt + i, 0))
    per_seq = pl.BlockSpec((1, 1, d), lambda b, i: (b, 0, 0))
    full = pl.BlockSpec((1, d), lambda b, i: (0, 0))
    return nt, row, per_seq, full


def normmod_fwd(name, x, nw, shift, scale, nb, s, tr=512):
    d, tr = x.shape[1], min(tr, s)
    nt, row, per_seq, full = _row_specs(nb, s, tr, d)

    def body(x_ref, nw_ref, sh_ref, sc_ref, h_ref, ht_ref):
        h = _normmod(x_ref[...], nw_ref[...], sh_ref[0], sc_ref[0])
        h_ref[...] = h.astype(h_ref.dtype)
        ht_ref[...] = h.T.astype(ht_ref.dtype)

    return pl.pallas_call(body, grid=(nb, nt), in_specs=[row, full, per_seq, per_seq],
                          out_specs=[row, pl.BlockSpec((d, tr), lambda b, i: (0, b * nt + i))],
                          out_shape=[jax.ShapeDtypeStruct(x.shape, BF16), jax.ShapeDtypeStruct(x.shape[::-1], BF16)],
                          compiler_params=_cparams(("parallel", "parallel")), name=name)(x, nw, shift, scale)


def normmod_bwd(name, x, nw, shift, scale, dh, dres, nb, s, tr=512):
    d, tr = x.shape[1], min(tr, s)
    nt, row, per_seq, full = _row_specs(nb, s, tr, d)

    def body(x_ref, nw_ref, sh_ref, sc_ref, dh_ref, dres_ref, dx_ref, dnw_ref, dsh_ref, dsc_ref):
        b, i = pl.program_id(0), pl.program_id(1)

        @pl.when((b == 0) & (i == 0))
        def _():
            dnw_ref[...] = jnp.zeros(dnw_ref.shape, F32)

        @pl.when(i == 0)
        def _():
            dsh_ref[...] = jnp.zeros(dsh_ref.shape, F32)
            dsc_ref[...] = jnp.zeros(dsc_ref.shape, F32)

        _, vjp = jax.vjp(_normmod, x_ref[...], nw_ref[...], sh_ref[0], sc_ref[0])
        dx, dnw, dsh, dsc = vjp(dh_ref[...])
        dx_ref[...] = dres_ref[...] + dx
        dnw_ref[...] += dnw
        dsh_ref[0] += dsh
        dsc_ref[0] += dsc

    out_shape = [jax.ShapeDtypeStruct(x.shape, F32), jax.ShapeDtypeStruct((1, d), F32),
                 jax.ShapeDtypeStruct((nb, 1, d), F32), jax.ShapeDtypeStruct((nb, 1, d), F32)]
    return pl.pallas_call(body, grid=(nb, nt), in_specs=[row, full, per_seq, per_seq, row, row],
                          out_specs=[row, full, per_seq, per_seq], out_shape=out_shape,
                          compiler_params=_cparams(("arbitrary", "arbitrary")),
                          name=name)(x, nw, shift, scale, dh, dres)


def _merge(oa, ob, oc, graw, gate1, wa, wb, wc, wo):
    d = wo.shape[0]
    g = _sigmoid(graw)
    merged = g[:, :d] * bdot(oa, wa) + g[:, d:2 * d] * bdot(ob, wb) + g[:, 2 * d:] * bdot(oc, wc)
    return gate1 * bdot(merged, wo)


def _merge_specs(nb, s, tr, d, wbr):
    nt, row, per_seq, _ = _row_specs(nb, s, tr, d)
    o_spec = pl.BlockSpec((tr, wbr), lambda b, i: (b * nt + i, 0))
    g_spec = pl.BlockSpec((tr, 3 * d), lambda b, i: (b * nt + i, 0))
    wbr_spec = pl.BlockSpec((wbr, d), lambda b, i: (0, 0))
    wo_spec = pl.BlockSpec((d, d), lambda b, i: (0, 0))
    return nt, row, per_seq, o_spec, g_spec, wbr_spec, wo_spec


def merge_fwd(name, x, oa, ob, oc, pg, gate1, wa, wb, wc, wo, nb, s, tr=512):
    d, tr = x.shape[1], min(tr, s)
    nt, row, per_seq, o_spec, g_spec, wbr_spec, wo_spec = _merge_specs(nb, s, tr, d, oa.shape[1])

    def body(x_ref, oa_ref, ob_ref, oc_ref, pg_ref, g1_ref, wa_ref, wb_ref, wc_ref, wo_ref, x1_ref):
        x1_ref[...] = x_ref[...] + _merge(oa_ref[...], ob_ref[...], oc_ref[...], pg_ref[...], g1_ref[0],
                                          wa_ref[...], wb_ref[...], wc_ref[...], wo_ref[...])

    return pl.pallas_call(body, grid=(nb, nt),
                          in_specs=[row, o_spec, o_spec, o_spec, g_spec, per_seq, wbr_spec, wbr_spec, wbr_spec, wo_spec],
                          out_specs=row, out_shape=jax.ShapeDtypeStruct(x.shape, F32),
                          compiler_params=_cparams(("parallel", "parallel")),
                          name=name)(x, oa, ob, oc, pg, gate1, wa, wb, wc, wo)


def merge_bwd(name, oa, ob, oc, pg, gate1, wa, wb, wc, wo, dx1, nb, s, tr=256):
    d, tr = dx1.shape[1], min(tr, s)
    wbr = oa.shape[1]
    nt, row, per_seq, o_spec, g_spec, wbr_spec, wo_spec = _merge_specs(nb, s, tr, d, wbr)

    def body(oa_ref, ob_ref, oc_ref, pg_ref, g1_ref, wa_ref, wb_ref, wc_ref, wo_ref, dx_ref,
             doa_ref, dob_ref, doc_ref, dpg_ref, dg1_ref, dwa_ref, dwb_ref, dwc_ref, dwo_ref):
        b, i = pl.program_id(0), pl.program_id(1)

        @pl.when((b == 0) & (i == 0))
        def _():
            for r in (dwa_ref, dwb_ref, dwc_ref, dwo_ref):
                r[...] = jnp.zeros(r.shape, F32)

        @pl.when(i == 0)
        def _():
            dg1_ref[...] = jnp.zeros(dg1_ref.shape, F32)

        args = [oa_ref[...].astype(F32), ob_ref[...].astype(F32), oc_ref[...].astype(F32), pg_ref[...], g1_ref[0],
                wa_ref[...].astype(F32), wb_ref[...].astype(F32), wc_ref[...].astype(F32), wo_ref[...].astype(F32)]
        _, vjp = jax.vjp(_merge, *args)
        doa, dob, doc, dpg, dg1, dwa, dwb, dwc, dwo = vjp(dx_ref[...])
        doa_ref[...] = doa
        dob_ref[...] = dob
        doc_ref[...] = doc
        dpg_ref[...] = dpg
        dg1_ref[0] += dg1
        dwa_ref[...] += dwa
        dwb_ref[...] += dwb
        dwc_ref[...] += dwc
        dwo_ref[...] += dwo

    t = nb * s
    out_shape = ([jax.ShapeDtypeStruct((t, wbr), F32)] * 3
                 + [jax.ShapeDtypeStruct((t, 3 * d), F32), jax.ShapeDtypeStruct((nb, 1, d), F32)]
                 + [jax.ShapeDtypeStruct((wbr, d), F32)] * 3 + [jax.ShapeDtypeStruct((d, d), F32)])
    return pl.pallas_call(body, grid=(nb, nt),
                          in_specs=[o_spec, o_spec, o_spec, g_spec, per_seq, wbr_spec, wbr_spec, wbr_spec, wo_spec, row],
                          out_specs=[o_spec, o_spec, o_spec, g_spec, per_seq, wbr_spec, wbr_spec, wbr_spec, wo_spec],
                          out_shape=out_shape, compiler_params=_cparams(("arbitrary", "arbitrary")),
                          name=name)(oa, ob, oc, pg, gate1, wa, wb, wc, wo, dx1)


def resid_fwd(name, x, f, gate, nb, s, tr=512):
    d, tr = x.shape[1], min(tr, s)
    nt, row, per_seq, _ = _row_specs(nb, s, tr, d)

    def body(x_ref, f_ref, g_ref, o_ref):
        o_ref[...] = x_ref[...] + g_ref[0] * f_ref[...]

    return pl.pallas_call(body, grid=(nb, nt), in_specs=[row, row, per_seq], out_specs=row,
                          out_shape=jax.ShapeDtypeStruct(x.shape, F32),
                          compiler_params=_cparams(("parallel", "parallel")), name=name)(x, f, gate)


def resid_bwd(name, dx, f, gate, nb, s, tr=512):
    d, tr = dx.shape[1], min(tr, s)
    nt, row, per_seq, _ = _row_specs(nb, s, tr, d)

    def body(dx_ref, f_ref, g_ref, df_ref, dg_ref):
        @pl.when(pl.program_id(1) == 0)
        def _():
            dg_ref[...] = jnp.zeros(dg_ref.shape, F32)

        df_ref[...] = (g_ref[0] * dx_ref[...]).astype(df_ref.dtype)
        dg_ref[0] += jnp.sum(dx_ref[...] * f_ref[...], axis=0, keepdims=True)

    return pl.pallas_call(body, grid=(nb, nt), in_specs=[row, row, per_seq], out_specs=[row, per_seq],
                          out_shape=[jax.ShapeDtypeStruct(dx.shape, BF16), jax.ShapeDtypeStruct((nb, 1, d), F32)],
                          compiler_params=_cparams(("arbitrary", "arbitrary")), name=name)(dx, f, gate)


def loss_head(name, x, fw, target, tr=512):
    t, d = x.shape
    row = pl.BlockSpec((tr, d), lambda i: (i, 0))
    full = pl.BlockSpec((1, d), lambda i: (0, 0))

    def loss_fn(xv, fwv, tv):
        err = _rms(xv, fwv) - tv
        return 0.5 * jnp.sum(jnp.mean(err * err, axis=-1))

    def body(x_ref, fw_ref, t_ref, dx_ref, l_ref, dfw_ref):
        @pl.when(pl.program_id(0) == 0)
        def _():
            l_ref[...] = jnp.zeros(l_ref.shape, F32)
            dfw_ref[...] = jnp.zeros(dfw_ref.shape, F32)

        val, (dx, dfw) = jax.value_and_grad(loss_fn, argnums=(0, 1))(x_ref[...], fw_ref[...], t_ref[...])
        dx_ref[...] = dx
        l_ref[...] += val
        dfw_ref[...] += dfw

    return pl.pallas_call(body, grid=(t // tr,), in_specs=[row, full, row],
                          out_specs=[row, pl.BlockSpec((1, 128), lambda i: (0, 0)), full],
                          out_shape=[jax.ShapeDtypeStruct((t, d), F32), jax.ShapeDtypeStruct((1, 128), F32),
                                     jax.ShapeDtypeStruct((1, d), F32)],
                          compiler_params=_cparams(("arbitrary",)), name=name)(x, fw, target)


def _row_tile(rows, cols, n_arrays):
    budget = 24 * 1024 * 1024 // (8 * cols * max(n_arrays, 1))
    tr = rows
    while tr > max(budget, 16) and tr % 2 == 0 and (tr // 2) % 16 == 0:
        tr //= 2
    return tr


def elementwise(name, fn, ins, out_dtypes):
    rows, cols = ins[0].shape
    tr = _row_tile(rows, cols, len(ins) + len(out_dtypes))
    spec = pl.BlockSpec((tr, cols), lambda i: (i, 0))
    n_in = len(ins)

    def body(*refs):
        outs = fn(*[r[...] for r in refs[:n_in]])
        for o_ref, o in zip(refs[n_in:], outs):
            o_ref[...] = o.astype(o_ref.dtype)

    return pl.pallas_call(body, grid=(rows // tr,), in_specs=[spec] * n_in, out_specs=[spec] * len(out_dtypes),
                          out_shape=[jax.ShapeDtypeStruct((rows, cols), dt) for dt in out_dtypes],
                          compiler_params=_cparams(("parallel",)), name=name)(*ins)


def _adamw(w, g, m, v):
    m = ADAM_B1 * m + (1.0 - ADAM_B1) * g
    v = ADAM_B2 * v + (1.0 - ADAM_B2) * (g * g)
    m_hat = m / (1.0 - ADAM_B1 ** ADAM_STEP)
    v_hat = v / (1.0 - ADAM_B2 ** ADAM_STEP)
    delta = -ADAM_LR * (m_hat / (jnp.sqrt(v_hat) + ADAM_EPS) + ADAM_WD * w)
    return delta, m, v


def adamw(name, w, g, m, v):
    return elementwise(name, _adamw, [w, g, m, v], [F32, F32, F32])


_ANY = pl.BlockSpec(memory_space=pl.ANY)


def _coords():
    return lax.axis_index("x"), lax.axis_index("y"), lax.axis_index("c")


def allgather8(name, arrays, halves):
    n = len(arrays)

    def body(*refs):
        in_refs, out_refs = refs[:n], refs[n:2 * n]
        send_sems, recv_sems, local_sems = refs[2 * n:]
        x, y, c = _coords()
        me, sibling = (x, y, c), (x, y, 1 - c)
        chips = [(1 - x, y), (x, 1 - y), (1 - x, 1 - y)]

        def blk(i, px, py, pc):
            return out_refs[i].at[4 * px + 2 * py + pc]

        def piece(i):
            return in_refs[i].at[c] if halves[i] else in_refs[i]

        def copy(i, k, block, to, src=None):
            return pltpu.make_async_remote_copy(
                src_ref=blk(i, *block) if src is None else src, dst_ref=blk(i, *block),
                send_sem=send_sems.at[7 * i + k], recv_sem=recv_sems.at[7 * i + k],
                device_id=to, device_id_type=MESH)

        mine = [pltpu.make_async_copy(piece(i), blk(i, *me), local_sems.at[i]) for i in range(n)]
        for cp in mine:
            cp.start()
        first = []
        for i in range(n):
            first.append(copy(i, 0, me, sibling, src=piece(i)))
            first += [copy(i, 1 + j, me, (*chip, c), src=piece(i)) for j, chip in enumerate(chips)]
        for cp in first:
            cp.start()
        passed = []
        for j, chip in enumerate(chips):
            for i in range(n):
                copy(i, 1 + j, (*chip, c), me).wait_recv()
                fwd = copy(i, 4 + j, (*chip, c), sibling)
                fwd.start()
                passed.append(fwd)
        for i in range(n):
            copy(i, 0, sibling, me).wait_recv()
            for j, chip in enumerate(chips):
                copy(i, 4 + j, (*chip, 1 - c), me).wait_recv()
        for cp in first + passed:
            cp.wait_send()
        for cp in mine:
            cp.wait()

    out_shape = []
    for a, hv in zip(arrays, halves):
        out_shape.append(jax.ShapeDtypeStruct((N_DEV,) + tuple(a.shape[1:] if hv else a.shape), a.dtype))
    return pl.pallas_call(
        body, in_specs=[_ANY] * n, out_specs=[_ANY] * n, out_shape=out_shape,
        scratch_shapes=[pltpu.SemaphoreType.DMA((7 * n,)), pltpu.SemaphoreType.DMA((7 * n,)),
                        pltpu.SemaphoreType.DMA((n,))],
        name=name)(*arrays)


def exchange(name, ins, out_shapes, plan, local_plan=()):
    n_in, n_out = len(ins), len(out_shapes)

    def body(*refs):
        in_refs, out_refs = refs[:n_in], refs[n_in:n_in + n_out]
        send_sems, recv_sems, local_sems = refs[n_in + n_out:]
        x, y, c = _coords()
        copies = []
        for k, fn in enumerate(plan):
            src, dst, peer = fn(in_refs, out_refs, x, y, c)
            copies.append(pltpu.make_async_remote_copy(src_ref=src, dst_ref=dst, send_sem=send_sems.at[k],
                                                       recv_sem=recv_sems.at[k], device_id=peer,
                                                       device_id_type=MESH))
        local = []
        for k, fn in enumerate(local_plan):
            src, dst = fn(in_refs, out_refs, x, y, c)
            local.append(pltpu.make_async_copy(src, dst, local_sems.at[k]))
        for cp in copies + local:
            cp.start()
        for cp in copies + local:
            cp.wait()

    return pl.pallas_call(
        body, in_specs=[_ANY] * n_in, out_specs=[_ANY] * n_out, out_shape=out_shapes,
        scratch_shapes=[pltpu.SemaphoreType.DMA((len(plan),)), pltpu.SemaphoreType.DMA((len(plan),)),
                        pltpu.SemaphoreType.DMA((max(len(local_plan), 1),))],
        name=name)(*ins)


def sum_halves(name, gs, recv, c_arr):
    _, _, hr, cs = gs.shape
    tr = _row_tile(hr, cs, 4)

    def body(c_ref, g_ref, r_ref, qf_ref, qb_ref):
        q = g_ref[0, 0] + r_ref[0, 0]
        qf_ref[0] = q
        qb_ref[0] = q.astype(BF16)

    grid_spec = pltpu.PrefetchScalarGridSpec(
        num_scalar_prefetch=1, grid=(N_CHIPS, hr // tr),
        in_specs=[pl.BlockSpec((1, 1, tr, cs), lambda j, i, c_ref: (j, c_ref[0], i, 0)),
                  pl.BlockSpec((1, 1, tr, cs), lambda j, i, c_ref: (j, 0, i, 0))],
        out_specs=[pl.BlockSpec((1, tr, cs), lambda j, i, c_ref: (j, i, 0))] * 2)
    return pl.pallas_call(body, grid_spec=grid_spec,
                          out_shape=[jax.ShapeDtypeStruct((N_CHIPS, hr, cs), F32),
                                     jax.ShapeDtypeStruct((N_CHIPS, hr, cs), BF16)],
                          compiler_params=_cparams(("parallel", "parallel")), name=name)(c_arr, gs, recv)


def sum_chips(name, qf, recv, chip_arr):
    _, hr, cs = qf.shape
    tr = _row_tile(hr, cs, 4)

    def body(chip_ref, q_ref, a_ref, b_ref, c_ref, o_ref):
        o_ref[...] = q_ref[0] + a_ref[...].astype(F32) + b_ref[...].astype(F32) + c_ref[...].astype(F32)

    row = pl.BlockSpec((tr, cs), lambda i, chip_ref: (i, 0))
    grid_spec = pltpu.PrefetchScalarGridSpec(
        num_scalar_prefetch=1, grid=(hr // tr,),
        in_specs=[pl.BlockSpec((1, tr, cs), lambda i, chip_ref: (chip_ref[0], i, 0)), row, row, row],
        out_specs=row)
    return pl.pallas_call(body, grid_spec=grid_spec, out_shape=jax.ShapeDtypeStruct((hr, cs), F32),
                          compiler_params=_cparams(("parallel",)), name=name)(chip_arr, qf, *recv)


def adamw_halves(name, w, m, v, g_mine, g_other, c_arr):
    _, hr, cs = w.shape
    tr = _row_tile(hr, cs, 9)

    def body(c_ref, w_ref, m_ref, v_ref, gm_ref, go_ref, g_ref, d_ref, nm_ref, nv_ref):
        g = jnp.where(pl.program_id(0) == c_ref[0], gm_ref[...], go_ref[...])
        delta, nm, nv = _adamw(w_ref[0], g, m_ref[0], v_ref[0])
        g_ref[0], d_ref[0], nm_ref[0], nv_ref[0] = g, delta, nm, nv

    half = pl.BlockSpec((1, tr, cs), lambda h, i, c_ref: (h, i, 0))
    row = pl.BlockSpec((tr, cs), lambda h, i, c_ref: (i, 0))
    grid_spec = pltpu.PrefetchScalarGridSpec(num_scalar_prefetch=1, grid=(2, hr // tr),
                                             in_specs=[half, half, half, row, row], out_specs=[half] * 4)
    return pl.pallas_call(body, grid_spec=grid_spec, out_shape=[jax.ShapeDtypeStruct(w.shape, F32)] * 4,
                          compiler_params=_cparams(("parallel", "parallel")),
                          name=name)(c_arr, w, m, v, g_mine, g_other)


def sum8(name, g):
    _, rows, cols = g.shape
    tr = _row_tile(rows, cols, 9)

    def body(*refs):
        acc = refs[0][0]
        for r in refs[1:N_DEV]:
            acc = acc + r[0]
        refs[N_DEV][...] = acc

    in_specs = [pl.BlockSpec((1, tr, cols), functools.partial(lambda k, i: (k, i, 0), k)) for k in range(N_DEV)]
    return pl.pallas_call(body, grid=(rows // tr,), in_specs=in_specs,
                          out_specs=pl.BlockSpec((tr, cols), lambda i: (i, 0)),
                          out_shape=jax.ShapeDtypeStruct((rows, cols), F32),
                          compiler_params=_cparams(("parallel",)), name=name)(*([g] * N_DEV))


_QKV, _AB, _GZ = (0, 1536), (1536, 1544), (1544, 2056)
_HG = (2056, 4104)
_SZ, _XBC, _DT = (4104, 4616), (4616, 5640), (5640, 5648)
_GATES = (5648, 8720)


def _cols(w, rng):
    return w[..., rng[0]:rng[1]]


def _split_w_in(w):
    pad = jnp.zeros(w.shape[:-1] + (120,), w.dtype)
    return (_cols(w, _GATES),
            jnp.concatenate([_cols(w, _QKV), _cols(w, _GZ), _cols(w, _AB), pad], axis=-1),
            _cols(w, _HG),
            jnp.concatenate([_cols(w, _SZ), _cols(w, _XBC), _cols(w, _DT), pad], axis=-1))


def _join_w_in(g, a, b, c):
    return jnp.concatenate([a[..., 0:1536], a[..., 2048:2056], a[..., 1536:2048], b,
                            c[..., 0:512], c[..., 512:1536], c[..., 1536:1544], g], axis=-1)


def _rows8(rows, width):
    out = [jnp.pad(r.astype(F32), (0, width - r.shape[0])) for r in rows]
    out += [jnp.zeros((width,), F32)] * (8 - len(out))
    return jnp.stack(out)


class _Packer:
    def __init__(self):
        self.items, self.size = [], 0

    def add(self, name, shape):
        n = 1
        for d in shape:
            n *= d
        self.items.append((name, tuple(shape), self.size, n))
        self.size += n

    def rows(self):
        return -(-self.size // 8192) * 8

    def pack(self, values):
        flat = [values[name].astype(F32).reshape(-1) for name, _, _, _ in self.items]
        flat.append(jnp.zeros((self.rows() * 1024 - self.size,), F32))
        return jnp.concatenate(flat).reshape(self.rows(), 1024)

    def unpack(self, buf):
        flat = buf.reshape(-1)
        return {name: flat[off:off + n].reshape(shape) for name, shape, off, n in self.items}


def _stack_by_chip(g, axis):
    l, r, c = g.shape
    if axis == 2:
        cs = c // N_CHIPS
        g = g.reshape(l, r, N_CHIPS, cs).transpose(2, 0, 1, 3).reshape(N_CHIPS, 2, l * r // 2, cs)
    else:
        rs = r // N_CHIPS
        g = g.reshape(l, N_CHIPS, rs, c).transpose(1, 0, 2, 3).reshape(N_CHIPS, 2, l * rs // 2, c)
    return g


def _unstack_gathered(w8, l, axis):
    _, hr, cs = w8.shape
    w = w8.reshape(N_CHIPS, l, 2 * hr // l, cs)
    if axis == 2:
        return w.transpose(1, 2, 0, 3).reshape(l, 2 * hr // l, N_CHIPS * cs)
    return w.transpose(1, 0, 2, 3).reshape(l, N_CHIPS * 2 * hr // l, cs)


_BIG = (("w_in", 2), ("w_br_a", 2), ("w_br_b", 2), ("w_br_c", 2), ("w_out", 1), ("ffn_w_up", 2), ("ffn_w_down", 1))
_SMALL = ("b_ada", "norm1_w", "gdn_conv_w", "gdn_a_log", "gdn_dt_bias", "gdn_norm_w", "hgrn_lb_param",
          "hgrn_norm_w", "ssd_conv_w", "ssd_conv_b", "ssd_a_log", "ssd_dt_bias", "ssd_d", "ssd_norm_w",
          "norm2_w", "ffn_conv_w", "ffn_conv_b", "final_norm_w")
_WEIGHTS = ("w_ada", "b_ada", "norm1_w", "w_in", "gdn_conv_w", "gdn_a_log", "gdn_dt_bias", "gdn_norm_w",
            "hgrn_lb_param", "hgrn_norm_w", "ssd_conv_w", "ssd_conv_b", "ssd_a_log", "ssd_dt_bias", "ssd_d",
            "ssd_norm_w", "w_br_a", "w_br_b", "w_br_c", "w_out", "norm2_w", "ffn_w_up", "ffn_conv_w",
            "ffn_conv_b", "ffn_w_down", "final_norm_w")
_R_GDN, _R_HGRN, _R_SSD, _R_FFN = 128, 128, 128, 256


def _reduce_scatter(grads):
    n = len(grads)
    c_arr = lax.axis_index("c").astype(jnp.int32).reshape(1)

    plan = [functools.partial(lambda i, ins, outs, x, y, c: (ins[i].at[:, pl.ds(1 - c, 1)], outs[i], (x, y, 1 - c)), i)
            for i in range(n)]
    recv = exchange("rs_d2d", grads, [jax.ShapeDtypeStruct((N_CHIPS, 1) + g.shape[2:], F32) for g in grads], plan)
    q = [sum_halves("rs_sum_d2d%d" % i, g, r, c_arr) for i, (g, r) in enumerate(zip(grads, recv))]
    qf, qb = [a for a, _ in q], [b for _, b in q]

    masks = ((1, 0), (0, 1), (1, 1))

    def ici(i, k, ins, outs, x, y, c):
        px = 1 - x if masks[k][0] else x
        py = 1 - y if masks[k][1] else y
        return ins[i].at[2 * px + py], outs[3 * i + k], (px, py, c)

    plan = [functools.partial(ici, i, k) for i in range(n) for k in range(3)]
    shapes = [jax.ShapeDtypeStruct(g.shape[2:], BF16) for g in grads for _ in range(3)]
    res = exchange("rs_ici", qb, shapes, plan)
    chip_arr = (2 * lax.axis_index("x") + lax.axis_index("y")).astype(jnp.int32).reshape(1)
    red = [sum_chips("rs_sum_ici%d" % i, qf[i], res[3 * i:3 * i + 3], chip_arr) for i in range(n)]

    plan = [functools.partial(lambda i, ins, outs, x, y, c: (ins[i], outs[i], (x, y, 1 - c)), i) for i in range(n)]
    other = exchange("rs_swap", red, [jax.ShapeDtypeStruct(r.shape, F32) for r in red], plan)
    return red, other, c_arr


def kernel(x, c, w_ada, b_ada, norm1_w, w_in, gdn_conv_w, gdn_a_log, gdn_dt_bias, gdn_norm_w, hgrn_lb_param, hgrn_norm_w, ssd_conv_w, ssd_conv_b, ssd_a_log, ssd_dt_bias, ssd_d, ssd_norm_w, w_br_a, w_br_b, w_br_c, w_out, norm2_w, ffn_w_up, ffn_conv_w, ffn_conv_b, ffn_w_down, final_norm_w, loss_target, m_w_ada, m_b_ada, m_norm1_w, m_w_in, m_gdn_conv_w, m_gdn_a_log, m_gdn_dt_bias, m_gdn_norm_w, m_hgrn_lb_param, m_hgrn_norm_w, m_ssd_conv_w, m_ssd_conv_b, m_ssd_a_log, m_ssd_dt_bias, m_ssd_d, m_ssd_norm_w, m_w_br_a, m_w_br_b, m_w_br_c, m_w_out, m_norm2_w, m_ffn_w_up, m_ffn_conv_w, m_ffn_conv_b, m_ffn_w_down, m_final_norm_w, v_w_ada, v_b_ada, v_norm1_w, v_w_in, v_gdn_conv_w, v_gdn_a_log, v_gdn_dt_bias, v_gdn_norm_w, v_hgrn_lb_param, v_hgrn_norm_w, v_ssd_conv_w, v_ssd_conv_b, v_ssd_a_log, v_ssd_dt_bias, v_ssd_d, v_ssd_norm_w, v_w_br_a, v_w_br_b, v_w_br_c, v_w_out, v_norm2_w, v_ffn_w_up, v_ffn_conv_w, v_ffn_conv_b, v_ffn_w_down, v_final_norm_w):
    loc = dict(locals())
    w = {k: loc[k] for k in _WEIGHTS}
    mom = {k: loc["m_" + k] for k in _WEIGHTS}
    var = {k: loc["v_" + k] for k in _WEIGHTS}
    nb, s, d = x.shape
    t = nb * s
    depth = w_ada.shape[0]
    chip = 2 * lax.axis_index("x") + lax.axis_index("y")
    dev = 2 * chip + lax.axis_index("c")
    x0 = x.reshape(t, d)
    target = loss_target.reshape(t, d)

    small_in = [c, gdn_conv_w.reshape(depth * 4, -1), ssd_conv_w.reshape(depth * 4, -1),
                ffn_conv_w.reshape(depth * 3, -1)]
    c_all, gcw, scw, fcw = allgather8("ag_small", small_in, [False] * 4)
    c_all = c_all.reshape(N_DEV * nb, d)

    def conv_full(g, taps):
        g = g[::2].reshape(N_CHIPS, depth, taps, -1)
        return g.transpose(1, 2, 0, 3).reshape(depth, taps, -1)

    gdn_cw, ssd_cw, ffn_cw = conv_full(gcw, 4), conv_full(scw, 4), conv_full(fcw, 3)

    big_in = []
    for name, axis in _BIG:
        a = w[name].astype(BF16)
        big_in.append(a.reshape(2, a.shape[0] * a.shape[1] // 2, a.shape[2]))
    big = allgather8("ag_weights", big_in, [True] * len(_BIG))
    wf = {name: _unstack_gathered(g, depth, axis) for (name, axis), g in zip(_BIG, big)}
    w_g, w_a, w_b, w_c = _split_w_in(wf["w_in"])

    (c_act,) = elementwise("silu_c", lambda v: (_silu(v),), [c_all], [F32])
    mod_cols = jnp.concatenate([matmul("ada_fwd%d" % l, c_act, w_ada[l], "nn") for l in range(depth)], axis=0)
    (mod8,) = allgather8("ag_mod", [mod_cols], [False])
    mod = mod8[::2].reshape(N_CHIPS, depth, N_DEV * nb, -1).transpose(1, 2, 0, 3).reshape(depth, N_DEV * nb, 6 * d)
    mod = lax.dynamic_slice_in_dim(mod, dev * nb, nb, axis=1) + b_ada[:, None, :]

    def mod_part(l, k):
        return mod[l, :, k * d:(k + 1) * d].reshape(nb, 1, d)

    saved = []
    xl = x0
    for l in range(depth):
        sfx = str(l)
        sv = {"x0": xl}
        shift1, scale1, gate1, shift2, scale2, gate2 = [mod_part(l, k) for k in range(6)]
        sv["mods"] = (shift1, scale1, gate1, shift2, scale2, gate2)
        h, h_t = normmod_fwd("norm1_fwd" + sfx, xl, norm1_w[l][None], shift1, scale1, nb, s)
        pg = matmul("proj_g" + sfx, h, w_g[l], "nn")
        pa = matmul("proj_a" + sfx, h, w_a[l], "nn")
        pb = matmul("proj_b" + sfx, h, w_b[l], "nn")
        pc = matmul("proj_c" + sfx, h, w_c[l], "nn")
        gdn_p = [_rows8(list(gdn_cw[l]), 1536), _rows8([gdn_a_log[l], gdn_dt_bias[l], gdn_norm_w[l]], 128)]
        hgrn_p = [_rows8(list(hgrn_lb_param), 512), _rows8([hgrn_norm_w[l]], 128)]
        ssd_p = [_rows8(list(ssd_cw[l]), 1024), _rows8([ssd_conv_b[l], ssd_norm_w[l]], 1024),
                 _rows8([ssd_a_log[l], ssd_dt_bias[l], ssd_d[l]], 128)]
        ffn_p = [_rows8(list(ffn_cw[l]) + [ffn_conv_b[l]], 2 * FFN_HIDDEN)]
        hgrn_fn = make_hgrn_tile(l, depth)
        oa, st_a = seq_fwd("gdn_fwd" + sfx, gdn_tile, gdn_p, [pa], True, [(512, BF16)], (512, 128), nb, s, _R_GDN)
        ob, st_b = seq_fwd("hgrn_fwd" + sfx, hgrn_fn, hgrn_p, [pb], False, [(512, BF16)], (512, 128), nb, s, _R_HGRN)
        oc, st_c = seq_fwd("ssd_fwd" + sfx, ssd_tile, ssd_p, [pc], True, [(512, BF16)], (256, 256), nb, s, _R_SSD)
        x1 = merge_fwd("merge_fwd" + sfx, xl, oa, ob, oc, pg, gate1, wf["w_br_a"][l], wf["w_br_b"][l],
                       wf["w_br_c"][l], wf["w_out"][l], nb, s)
        h2, h2_t = normmod_fwd("norm2_fwd" + sfx, x1, norm2_w[l][None], shift2, scale2, nb, s)
        u = matmul("ffn_up" + sfx, h2, wf["ffn_w_up"][l], "nn", tn=FFN_HIDDEN)
        act, act_t = seq_fwd("convglu_fwd" + sfx, convglu_tile_t, ffn_p, [u], True,
                             [(FFN_HIDDEN, BF16), (FFN_HIDDEN, BF16, "T")], None, nb, s, _R_FFN)
        f = matmul("ffn_down" + sfx, act, wf["ffn_w_down"][l], "nn")
        xl = resid_fwd("resid_fwd" + sfx, x1, f, gate2, nb, s)
        sv.update(h_t=h_t, h2_t=h2_t, act_t=act_t, pg=pg, pa=pa, pb=pb, pc=pc, oa=oa, ob=ob, oc=oc, st_a=st_a, st_b=st_b, st_c=st_c, x1=x1,
                  u=u, f=f, gdn_p=gdn_p, hgrn_p=hgrn_p, ssd_p=ssd_p, ffn_p=ffn_p, hgrn_fn=hgrn_fn)
        saved.append(sv)

    dx, loss_part, d_final = loss_head("loss_head", xl, final_norm_w[None], target)

    gfull = {name: [None] * depth for name, _ in _BIG}
    sg = {}
    dmod = [None] * depth
    d_lb = None
    for l in reversed(range(depth)):
        sfx = str(l)
        sv = saved[l]
        shift1, scale1, gate1, shift2, scale2, gate2 = sv["mods"]
        df, dgate2 = resid_bwd("resid_bwd" + sfx, dx, sv["f"], gate2, nb, s)
        dact = matmul("ffn_down_dx" + sfx, df, wf["ffn_w_down"][l], "nt")
        gfull["ffn_w_down"][l] = matmul("ffn_down_dw" + sfx, sv["act_t"], df, "nn", tm=1408, tn=512, tk=4096)
        (du,), (dcw,) = seq_bwd("convglu_bwd" + sfx, convglu_tile, sv["ffn_p"], [sv["u"]], True, None, [dact], [BF16],
                                None, nb, s, _R_FFN)
        dh2 = matmul("ffn_up_dx" + sfx, du, wf["ffn_w_up"][l], "nt")
        gfull["ffn_w_up"][l] = matmul("ffn_up_dw" + sfx, sv["h2_t"], du, "nn", tm=1024, tn=512, tk=4096)
        dx1, dnw2, dshift2, dscale2 = normmod_bwd("norm2_bwd" + sfx, sv["x1"], norm2_w[l][None], shift2, scale2, dh2, dx,
                                                  nb, s)
        doa, dob, doc, dpg, dgate1, dwa, dwb, dwc, dwo = merge_bwd(
            "merge_bwd" + sfx, sv["oa"], sv["ob"], sv["oc"], sv["pg"], gate1, wf["w_br_a"][l], wf["w_br_b"][l],
            wf["w_br_c"][l], wf["w_out"][l], dx1, nb, s)
        gfull["w_br_a"][l], gfull["w_br_b"][l], gfull["w_br_c"][l], gfull["w_out"][l] = dwa, dwb, dwc, dwo
        (dpa,), (dgcw, dgpk) = seq_bwd("gdn_bwd" + sfx, gdn_tile, sv["gdn_p"], [sv["pa"]], True, sv["st_a"], [doa],
                                       [F32], (512, 128), nb, s, _R_GDN)
        (dpb,), (dlbp, dhnw) = seq_bwd("hgrn_bwd" + sfx, sv["hgrn_fn"], sv["hgrn_p"], [sv["pb"]], False, sv["st_b"],
                                       [dob], [F32], (512, 128), nb, s, _R_HGRN)
        (dpc,), (dscw, dspv, dsps) = seq_bwd("ssd_bwd" + sfx, ssd_tile, sv["ssd_p"], [sv["pc"]], True, sv["st_c"],
                                             [doc], [F32], (256, 256), nb, s, _R_SSD)
        dh = matmul("proj_g_dx" + sfx, dpg, w_g[l], "nt")
        dh = matmul("proj_a_dx" + sfx, dpa, w_a[l], "nt", addend=dh)
        dh = matmul("proj_b_dx" + sfx, dpb, w_b[l], "nt", addend=dh)
        dh = matmul("proj_c_dx" + sfx, dpc, w_c[l], "nt", addend=dh)
        gfull["w_in"][l] = _join_w_in(
            matmul("proj_g_dw" + sfx, sv["h_t"], dpg, "nn", tm=1024, tn=512, tk=4096),
            matmul("proj_a_dw" + sfx, sv["h_t"], dpa, "nn", tm=1024, tk=1024),
            matmul("proj_b_dw" + sfx, sv["h_t"], dpb, "nn", tm=1024, tn=512, tk=4096),
            matmul("proj_c_dw" + sfx, sv["h_t"], dpc, "nn", tm=1024, tk=1024))
        dx, dnw1, dshift1, dscale1 = normmod_bwd("norm1_bwd" + sfx, sv["x0"], norm1_w[l][None], shift1, scale1, dh, dx1,
                                                 nb, s)
        dmod[l] = jnp.concatenate([dshift1, dscale1, dgate1, dshift2, dscale2, dgate2], axis=-1).reshape(nb, 6 * d)
        d_lb = dlbp[:depth] if d_lb is None else d_lb + dlbp[:depth]
        sg[l] = dict(norm1_w=dnw1[0], norm2_w=dnw2[0], gdn_conv_w=dgcw[:4], gdn_a_log=dgpk[0, :4],
                     gdn_dt_bias=dgpk[1, :4], gdn_norm_w=dgpk[2], hgrn_norm_w=dhnw[0], ssd_conv_w=dscw[:4],
                     ssd_conv_b=dspv[0], ssd_norm_w=dspv[1, :512], ssd_a_log=dsps[0, :8], ssd_dt_bias=dsps[1, :8],
                     ssd_d=dsps[2, :8], ffn_conv_w=dcw[:3], ffn_conv_b=dcw[3])
    grad_x = dx.reshape(nb, s, d)

    dmod = jnp.stack(dmod)
    (b_sum,) = elementwise("bias_rows", lambda *r: (functools.reduce(lambda p, q: p + q, r),),
                           [dmod[:, b].reshape(depth * 6, d) for b in range(nb)], [F32])
    per_layer = ("norm1_w", "norm2_w", "gdn_conv_w", "gdn_a_log", "gdn_dt_bias", "gdn_norm_w", "hgrn_norm_w",
                 "ssd_conv_w", "ssd_conv_b", "ssd_norm_w", "ssd_a_log", "ssd_dt_bias", "ssd_d", "ffn_conv_w", "ffn_conv_b")
    vals = {k: jnp.stack([sg[l][k] for l in range(depth)]) for k in per_layer}
    vals.update(loss=loss_part[0, :1], b_ada=b_sum.reshape(depth, 6 * d), hgrn_lb_param=d_lb, final_norm_w=d_final[0])
    gp = _Packer()
    for k, v in vals.items():
        gp.add(k, v.shape)
    packed8, dmod8 = allgather8("ag_grads", [gp.pack(vals), dmod.reshape(depth * nb, 6 * d)], [False, False])
    gs = gp.unpack(sum8("sum_small", packed8))
    loss = gs["loss"].reshape(())

    def my_cols(g):
        cs = g.shape[-1] // N_CHIPS
        return lax.dynamic_slice_in_dim(g, chip * cs, cs, axis=g.ndim - 1)

    for k in ("gdn_conv_w", "ssd_conv_w", "ffn_conv_w"):
        gs[k] = my_cols(gs[k])

    dmod_all = dmod8.reshape(N_DEV, depth, nb, 6 * d).transpose(1, 0, 2, 3).reshape(depth, N_DEV * nb, 6 * d)
    dmod_mine = lax.dynamic_slice_in_dim(dmod_all, chip * (6 * d // N_CHIPS), 6 * d // N_CHIPS, axis=2)
    g_w_ada = jnp.stack([matmul("ada_dw%d" % l, c_act, dmod_mine[l], "tn", tm=1024) for l in range(depth)])

    stacked = [_stack_by_chip(jnp.stack(gfull[name]), axis) for name, axis in _BIG]
    g_mine, g_other, c_arr = _reduce_scatter(stacked)
    grads, delta, new_m, new_v = {}, {}, {}, {}
    for (name, _), gm, go in zip(_BIG, g_mine, g_other):
        shp = w[name].shape
        halves = lambda a: a.reshape((2,) + gm.shape)
        res = adamw_halves("adamw_" + name, halves(w[name]), halves(mom[name]), halves(var[name]), gm, go, c_arr)
        grads[name], delta[name], new_m[name], new_v[name] = [r.reshape(shp) for r in res]
    grads["w_ada"] = g_w_ada
    for k in _SMALL:
        grads[k] = gs[k].reshape(w[k].shape)
    shp = w_ada.shape
    flat = lambda a: a.reshape(shp[0] * shp[1], shp[2])
    dl, nm, nv = adamw("adamw_w_ada", flat(w_ada), flat(g_w_ada), flat(m_w_ada), flat(v_w_ada))
    delta["w_ada"], new_m["w_ada"], new_v["w_ada"] = dl.reshape(shp), nm.reshape(shp), nv.reshape(shp)
    sp = _Packer()
    for k in _SMALL:
        sp.add(k, w[k].shape)
    dl, nm, nv = adamw("adamw_small", sp.pack(w), sp.pack(grads), sp.pack(mom), sp.pack(var))
    delta.update(sp.unpack(dl))
    new_m.update(sp.unpack(nm))
    new_v.update(sp.unpack(nv))

    return (loss, grad_x, *[grads[k] for k in _WEIGHTS], *[delta[k] for k in _WEIGHTS],
            *[new_m[k] for k in _WEIGHTS], *[new_v[k] for k in _WEIGHTS])
```

```python
import functools

import jax
import jax.numpy as jnp
from jax import lax
from jax.experimental import pallas as pl
from jax.experimental.pallas import tpu as pltpu

F32 = jnp.float32
BF16 = jnp.bfloat16
HI = lax.Precision.HIGHEST
MESH = pl.DeviceIdType.MESH

EPS = 1e-6
D_MODEL = 1024
GDN_HEADS, GDN_DK, GDN_CHUNK = 4, 128, 64
HGRN_HEADS, HGRN_DK, HGRN_CHUNK = 4, 128, 16
SSD_HEADS, SSD_P, SSD_GROUPS, SSD_STATE, SSD_CHUNK = 8, 64, 2, 128, 64
FFN_HIDDEN = 2816
N_CHIPS = 4
N_DEV = 8

ADAM_LR, ADAM_B1, ADAM_B2, ADAM_EPS, ADAM_WD, ADAM_STEP = 0.001, 0.9, 0.999, 1e-08, 0.01, 10

W_G, W_A, W_B, W_C = 3072, 2176, 2048, 1664
VMEM_LIMIT = 56 * 1024 * 1024


def _cparams(sem):
    return pltpu.CompilerParams(dimension_semantics=sem, vmem_limit_bytes=VMEM_LIMIT)


def _dg(a, b, ca, cb):
    return lax.dot_general(a.astype(BF16), b.astype(BF16), (((ca,), (cb,)), ((), ())),
                           preferred_element_type=F32)


@jax.custom_vjp
def bdot(a, b):
    return _dg(a, b, 1, 0)


bdot.defvjp(lambda a, b: (_dg(a, b, 1, 0), (a, b)),
            lambda r, g: (_dg(g, r[1], 1, 1), _dg(r[0], g, 0, 0)))


@jax.custom_vjp
def bdot_nt(a, b):
    return _dg(a, b, 1, 1)


bdot_nt.defvjp(lambda a, b: (_dg(a, b, 1, 1), (a, b)),
               lambda r, g: (_dg(g, r[1], 1, 0), _dg(g, r[0], 0, 0)))


@jax.custom_vjp
def bdot_tn(a, b):
    return _dg(a, b, 0, 0)


bdot_tn.defvjp(lambda a, b: (_dg(a, b, 0, 0), (a, b)),
               lambda r, g: (_dg(r[1], g, 1, 1), _dg(r[0], g, 1, 0)))


def _split(x, n):
    parts, rest = [], x
    for _ in range(n):
        p = rest.astype(BF16)
        parts.append(p)
        rest = rest - p.astype(F32)
    return parts


def _dgb(a, b, ca, cb):
    return lax.dot_general(a, b, (((ca,), (cb,)), ((), ())), preferred_element_type=F32)


def _dg3(a, b, ca, cb):
    (ah, al), (bh, bl) = _split(a, 2), _split(b, 2)
    return _dgb(jnp.concatenate([ah, ah, al], axis=ca), jnp.concatenate([bh, bl, bh], axis=cb), ca, cb)


@jax.custom_vjp
def hdot(a, b):
    return _dg3(a, b, 1, 0)


hdot.defvjp(lambda a, b: (_dg3(a, b, 1, 0), (a, b)),
            lambda r, g: (_dg3(g, r[1], 1, 1), _dg3(r[0], g, 0, 0)))


def _dge(e, x, ce, cx, e_first):
    eb = e.astype(BF16)
    es = jnp.concatenate([eb, eb, eb], axis=ce)
    xs = jnp.concatenate(_split(x, 3), axis=cx)
    return _dgb(es, xs, ce, cx) if e_first else _dgb(xs, es, cx, ce)


@jax.custom_vjp
def ldot(e, x):
    return _dge(e, x, 1, 0, True)


ldot.defvjp(lambda e, x: (_dge(e, x, 1, 0, True), e),
            lambda e, g: (jnp.zeros_like(e), _dge(e, g, 0, 0, True)))


@jax.custom_vjp
def rdot(x, e):
    return _dge(e, x, 0, 1, False)


rdot.defvjp(lambda x, e: (_dge(e, x, 0, 1, False), e),
            lambda e, g: (_dge(e, g, 1, 1, False), jnp.zeros_like(e)))


def _sigmoid(x):
    return 1.0 / (1.0 + jnp.exp(-x))


def _silu(x):
    return x * _sigmoid(x)


def _softplus(x):
    return jnp.maximum(x, 0.0) + jnp.log(1.0 + jnp.exp(-jnp.abs(x)))


def _rms(x, w):
    return x * lax.rsqrt(jnp.mean(x * x, axis=-1, keepdims=True) + EPS) * w


def _iota(shape, dim):
    return lax.broadcasted_iota(jnp.int32, shape, dim)


def _tri_ones(n, chunk, kind):
    i, j = _iota((n, n), 0), _iota((n, n), 1)
    same = lax.div(i, chunk) == lax.div(j, chunk)
    if kind == "incl":
        m = same & (j <= i)
    elif kind == "strict":
        m = same & (j < i)
    elif kind == "all":
        m = same
    else:
        m = same & (lax.rem(j, chunk) < (chunk // 2))
    return m


def _causal_conv(w, halo, x, width):
    r = x.shape[0]
    xin = jnp.concatenate([halo, x], axis=0)
    y = w[width - 1:width, :] * x
    for k in range(width - 1):
        off = 8 - (width - 1) + k
        y = y + w[k:k + 1, :] * xin[off:off + r, :]
    return y


def _each(fn, *lists):
    return [fn(*a) for a in zip(*lists)]


def _neumann(ms):
    n = ms[0].shape[0]
    eye = (_iota((n, n), 0) == _iota((n, n), 1)).astype(F32)
    accs = [eye - m for m in ms]
    ps = ms
    steps = 1
    while steps * 2 < n:
        ps = _each(hdot, ps, ps)
        accs = [acc + ap for acc, ap in zip(accs, _each(hdot, accs, ps))]
        steps *= 2
    return accs


@jax.custom_vjp
def tri_inverse(ms):
    return _neumann(ms)


def _tri_inverse_fwd(ms):
    ainvs = _neumann(ms)
    return ainvs, ainvs


def _tri_inverse_bwd(ainvs, gs):
    t = _each(lambda g, a: _dg3(g, a, 1, 1), gs, ainvs)
    return ([-x for x in _each(lambda a, y: _dg3(a, y, 0, 0), ainvs, t)],)


tri_inverse.defvjp(_tri_inverse_fwd, _tri_inverse_bwd)


def gdn_tile(params, state, ins, halos):
    conv_w, pk = params
    (pa,), (ha,) = ins, halos
    r = pa.shape[0]
    c, nh, dk = GDN_CHUNK, GDN_HEADS, GDN_DK
    kw = nh * dk
    qkv = _silu(_causal_conv(conv_w, ha[:, :3 * kw], pa[:, :3 * kw], 4))
    z = pa[:, 3 * kw:4 * kw]
    gsm = pa[:, 4 * kw:]
    a_log, dtb, nw = pk[0:1, :], pk[1:2, :], pk[2:3, :]
    g_all = -jnp.exp(a_log) * _softplus(gsm + dtb)
    beta_all = _sigmoid(gsm)
    incl = _tri_ones(c, c, "incl")
    strict = _tri_ones(c, c, "strict")
    lmat = incl.astype(F32)
    scale = dk ** -0.5
    nck = r // c
    inst = [(ci, h) for ci in range(nck) for h in range(nh)]

    def l2n(v):
        return v * lax.rsqrt(jnp.sum(v * v, axis=-1, keepdims=True) + EPS)

    gcs = [ldot(lmat, g_all[ci * c:(ci + 1) * c, :]) for ci in range(nck)]
    gcts = [g.T for g in gcs]
    g_col = [gcs[ci][:, h:h + 1] for ci, h in inst]
    g_row = [gcts[ci][h:h + 1, :] for ci, h in inst]
    g_last = [gcs[ci][c - 1:c, h:h + 1] for ci, h in inst]
    beta = [beta_all[ci * c:(ci + 1) * c, nh + h:nh + h + 1] for ci, h in inst]
    qh = [l2n(qkv[ci * c:(ci + 1) * c, h * dk:(h + 1) * dk]) for ci, h in inst]
    kh = [l2n(qkv[ci * c:(ci + 1) * c, kw + h * dk:kw + (h + 1) * dk]) for ci, h in inst]
    vh = [qkv[ci * c:(ci + 1) * c, 2 * kw + h * dk:2 * kw + (h + 1) * dk] for ci, h in inst]
    decay = [jnp.where(incl, jnp.exp(jnp.where(incl, gc_ - gr_, 0.0)), 0.0) for gc_, gr_ in zip(g_col, g_row)]
    kb = [k * b for k, b in zip(kh, beta)]
    qs = [q * scale for q in qh]
    kk = _each(lambda a, b, k: bdot_nt(jnp.concatenate([a, b], axis=0), k), kb, qs, kh)
    ms = [jnp.where(strict, x[:c] * d, 0.0) for x, d in zip(kk, decay)]
    attn = [x[c:] * d for x, d in zip(kk, decay)]
    ainv = tri_inverse(ms)
    eg = [jnp.exp(g) for g in g_col]
    rhs = [jnp.concatenate([v * b, k_ * e], axis=1) for v, b, k_, e in zip(vh, beta, kb, eg)]
    sol = _each(hdot, ainv, rhs)
    qg = [q * e for q, e in zip(qs, eg)]
    k_end = [k * jnp.exp(gl - g) for k, gl, g in zip(kh, g_last, g_col)]
    e_last = [jnp.exp(gl) for gl in g_last]

    st = [state[h * dk:(h + 1) * dk, :] for h in range(nh)]
    outs = [[] for _ in range(nh)]
    for ci in range(nck):
        idx = [ci * nh + h for h in range(nh)]
        ws = [bdot(jnp.concatenate([sol[i][:, dk:], qg[i]], axis=0), st[h]) for h, i in enumerate(idx)]
        v_new = [sol[i][:, :dk] - w_[:c] for i, w_ in zip(idx, ws)]
        av = [bdot(attn[i], v) for i, v in zip(idx, v_new)]
        kv = [bdot_tn(k_end[i], v) for i, v in zip(idx, v_new)]
        for h, i in enumerate(idx):
            o = ws[h][c:] + av[h]
            st[h] = st[h] * e_last[i] + kv[h]
            outs[h].append(_rms(o, nw) * _silu(z[ci * c:(ci + 1) * c, h * dk:(h + 1) * dk]))
    out = jnp.concatenate([jnp.concatenate(o, axis=0) for o in outs], axis=1)
    return jnp.concatenate(st, axis=0), [out]


def make_hgrn_tile(layer, depth):
    def hgrn_tile(params, state, ins, halos):
        lbp, nwp = params
        (pb,) = ins
        r = pb.shape[0]
        c = HGRN_CHUNK
        kw = HGRN_HEADS * HGRN_DK
        rows = [lbp[i:i + 1, :] for i in range(depth)]
        mx = functools.reduce(jnp.maximum, rows)
        ex = [jnp.exp(x - mx) for x in rows]
        den = functools.reduce(lambda a, b: a + b, ex)
        soft = [e / den for e in ex]
        lb = functools.reduce(lambda a, b: a + b, soft[:layer + 1]) - soft[0]
        nw = nwp[0:1, :]
        q = _silu(pb[:, :kw])
        fr = pb[:, kw:2 * kw]
        logf = jnp.log(lb + (1.0 - lb) * _sigmoid(fr))
        k = (1.0 - lb) * _sigmoid(-fr)
        v = pb[:, 2 * kw:3 * kw]
        gate = pb[:, 3 * kw:]
        incl = _tri_ones(r, c, "incl")
        masks = jnp.concatenate([incl.astype(F32), _tri_ones(r, c, "upto").astype(F32),
                                 _tri_ones(r, c, "all").astype(F32)], axis=0)
        sums = ldot(masks, logf)
        g_cum, g_ref, g_end = sums[:r], sums[r:2 * r], sums[2 * r:]
        qs = q * jnp.exp(g_cum - g_ref)
        ks = k * jnp.exp(g_ref - g_cum)
        qg = q * jnp.exp(g_cum)
        k_end = k * jnp.exp(g_end - g_cum)
        e_end = jnp.exp(g_end)
        sls = [slice(h * HGRN_DK, (h + 1) * HGRN_DK) for h in range(HGRN_HEADS)]
        attn = [jnp.where(incl, bdot_nt(qs[:, sl], ks[:, sl]), 0.0) for sl in sls]
        o_intra = [bdot(a, v[:, sl]) for a, sl in zip(attn, sls)]
        s_t = [state[sl, :] for sl in sls]
        o_inter = [[] for _ in sls]
        for j in range(r // c):
            rs = slice(j * c, (j + 1) * c)
            oi = [bdot_nt(qg[rs, sl], s) for sl, s in zip(sls, s_t)]
            kv = [bdot_tn(v[rs, sl], k_end[rs, sl]) for sl in sls]
            s_t = [s * e_end[j * c:j * c + 1, sl] + x for s, sl, x in zip(s_t, sls, kv)]
            for lst, x in zip(o_inter, oi):
                lst.append(x)
        outs = [_rms(oa + jnp.concatenate(ob, axis=0), nw) * _silu(gate[:, sl])
                for oa, ob, sl in zip(o_intra, o_inter, sls)]
        return jnp.concatenate(s_t, axis=0), [jnp.concatenate(outs, axis=1)]
    return hgrn_tile


def ssd_tile(params, state, ins, halos):
    conv_w, pv, ps = params
    (pc,), (hc,) = ins, halos
    r = pc.shape[0]
    c = SSD_CHUNK
    inner = SSD_HEADS * SSD_P
    gw = inner // SSD_GROUPS
    z = pc[:, :inner]
    xbc = _silu(_causal_conv(conv_w, hc[:, inner:inner + 1024], pc[:, inner:inner + 1024], 4) + pv[0:1, :])
    ssm = pc[:, inner + 1024:]
    xs = xbc[:, :inner]
    bm = xbc[:, inner:inner + SSD_GROUPS * SSD_STATE]
    cm = xbc[:, inner + SSD_GROUPS * SSD_STATE:]
    a_log, dtb, dsk = ps[0:1, :], ps[1:2, :], ps[2:3, :]
    nw = pv[1:2, :inner]
    dt = _softplus(ssm + dtb)
    da = dt * (-jnp.exp(a_log))
    expand = (lax.div(_iota((128, inner), 1), SSD_P) == _iota((128, inner), 0)).astype(F32)
    xdt = xs * rdot(dt, expand)
    d_e = rdot(jnp.concatenate([dsk] * 8, axis=0), expand)[0:1, :]
    incl = _tri_ones(c, c, "incl")
    lmat = incl.astype(F32)
    st = [state[g * SSD_STATE:(g + 1) * SSD_STATE, :] for g in range(SSD_GROUPS)]
    hpg = SSD_HEADS // SSD_GROUPS
    nck = r // c
    groups = range(SSD_GROUPS)
    cg = [(ci, g) for ci in range(nck) for g in groups]
    rows = [slice(ci * c, (ci + 1) * c) for ci in range(nck)]
    gls = [slice(g * gw, (g + 1) * gw) for g in groups]
    acs = [ldot(lmat, da[rs, :]) for rs in rows]
    acs_t = [a.T for a in acs]
    acs_e = [rdot(a, expand) for a in acs]
    last_e = [a[c - 1:c, :] for a in acs_e]
    bm_g = [bm[rows[ci], g * SSD_STATE:(g + 1) * SSD_STATE] for ci, g in cg]
    cm_g = [cm[rows[ci], g * SSD_STATE:(g + 1) * SSD_STATE] for ci, g in cg]
    cb = _each(bdot_nt, cm_g, bm_g)
    heads = [(i, ci, g * hpg + hg) for i, (ci, g) in enumerate(cg) for hg in range(hpg)]
    seg = [jnp.where(incl, jnp.exp(jnp.where(incl, acs[ci][:, hh:hh + 1] - acs_t[ci][hh:hh + 1, :], 0.0)), 0.0)
           for _, ci, hh in heads]
    yd = [bdot(cb[i] * sg, xdt[rows[ci], hh * SSD_P:(hh + 1) * SSD_P]) for (i, ci, hh), sg in zip(heads, seg)]
    y_diag = [jnp.concatenate(yd[i * hpg:(i + 1) * hpg], axis=1) for i in range(len(cg))]
    xw = [xdt[rows[ci], gls[g]] * jnp.exp(last_e[ci][:, gls[g]] - acs_e[ci][:, gls[g]]) for ci, g in cg]
    e_acs = [jnp.exp(acs_e[ci][:, gls[g]]) for ci, g in cg]
    e_last = [jnp.exp(last_e[ci][:, gls[g]]) for ci, g in cg]
    kv = _each(bdot_tn, bm_g, xw)
    ys = []
    for ci in range(nck):
        idx = [ci * SSD_GROUPS + g for g in groups]
        y_off = [bdot(cm_g[i], st[g]) * e_acs[i] for g, i in zip(groups, idx)]
        st = [st[g] * e_last[i] + kv[i] for g, i in zip(groups, idx)]
        ys.append(jnp.concatenate([y_diag[i] + yo for i, yo in zip(idx, y_off)], axis=1))
    y = jnp.concatenate(ys, axis=0) + d_e * xs
    yz = y * _silu(z)
    out = jnp.concatenate([_rms(yz[:, g * gw:(g + 1) * gw], nw[:, g * gw:(g + 1) * gw])
                           for g in range(SSD_GROUPS)], axis=1)
    return jnp.concatenate(st, axis=0), [out]


def convglu_tile(params, state, ins, halos):
    (cw,) = params
    (u,), (hu,) = ins, halos
    y = _causal_conv(cw, hu, u, 3) + cw[3:4, :]
    return None, [_silu(y[:, :FFN_HIDDEN]) * y[:, FFN_HIDDEN:]]


def convglu_tile_t(params, state, ins, halos):
    _, (act,) = convglu_tile(params, state, ins, halos)
    return None, [act, act.T]


def _halo_map(nt, r):
    return lambda b, n: (jnp.maximum((b * nt + n) * (r // 8) - 1, 0), 0)


def _exchange_copies(plan, local_plan, in_refs, out_refs, send_sems, recv_sems, local_sems):
    x, y, c = lax.axis_index("x"), lax.axis_index("y"), lax.axis_index("c")
    copies = []
    for k, fn in enumerate(plan):
        src, dst, peer = fn(in_refs, out_refs, x, y, c)
        copies.append(pltpu.make_async_remote_copy(src_ref=src, dst_ref=dst, send_sem=send_sems.at[k],
                                                   recv_sem=recv_sems.at[k], device_id=peer, device_id_type=MESH))
    for k, fn in enumerate(local_plan):
        src, dst = fn(in_refs, out_refs, x, y, c)
        copies.append(pltpu.make_async_copy(src, dst, local_sems.at[k]))
    return copies


def _exchange_sems(plan, local_plan):
    return [pltpu.SemaphoreType.DMA((len(plan),)), pltpu.SemaphoreType.DMA((len(plan),)),
            pltpu.SemaphoreType.DMA((max(len(local_plan), 1),))]


def _host_exchange(side, body, in_specs, o_specs, out_shape, scratch, args, grid):
    s_ins, s_shapes, plan, local_plan = side
    n_in, n_out, n_scr = len(in_specs), len(o_specs), len(scratch)
    k_in, k_out = len(s_ins), len(s_shapes)
    any_spec = pl.BlockSpec(memory_space=pl.ANY)

    def hosted(*refs):
        own_in, s_in = refs[:n_in], refs[n_in:n_in + k_in]
        o0 = n_in + k_in
        own_out, s_out = refs[o0:o0 + n_out], refs[o0 + n_out:o0 + n_out + k_out]
        rest = refs[o0 + n_out + k_out:]
        own_scr, sems = rest[:n_scr], rest[n_scr:]
        ids = [pl.program_id(a) for a in range(len(grid))]
        first = functools.reduce(lambda p, q: p & q, [i == 0 for i in ids])
        last = functools.reduce(lambda p, q: p & q, [i == g - 1 for i, g in zip(ids, grid)])

        @pl.when(first)
        def _():
            for cp in _exchange_copies(plan, local_plan, s_in, s_out, *sems):
                cp.start()

        body(*own_in, *own_out, *own_scr)

        @pl.when(last)
        def _():
            for cp in _exchange_copies(plan, local_plan, s_in, s_out, *sems):
                cp.wait()

    return (hosted, list(in_specs) + [any_spec] * k_in, list(o_specs) + [any_spec] * k_out,
            list(out_shape) + list(s_shapes), list(scratch) + _exchange_sems(plan, local_plan),
            list(args) + list(s_ins))


def seq_fwd(name, tile_fn, params, ins, use_halo, out_specs, state_shape, nb, s, r, side=None):
    nt = s // r
    n_p, n_i, n_o = len(params), len(ins), len(out_specs)
    has_state = state_shape is not None

    def body(*refs):
        p_refs, i_refs = refs[:n_p], refs[n_p:n_p + n_i]
        h_refs = refs[n_p + n_i:n_p + 2 * n_i] if use_halo else ()
        k = n_p + n_i + len(h_refs)
        o_refs = refs[k:k + n_o]
        n = pl.program_id(1)
        state = None
        if has_state:
            sv_ref, st_ref = refs[k + n_o], refs[k + n_o + 1]

            @pl.when(n == 0)
            def _():
                st_ref[...] = jnp.zeros(state_shape, F32)

            state = st_ref[...]
            sv_ref[0, 0] = state
        pv = [p[...] for p in p_refs]
        iv = [i[...].astype(F32) for i in i_refs]
        hv = [jnp.where(n > 0, h[...].astype(F32), 0.0) for h in h_refs]
        new_state, ov = tile_fn(pv, state, iv, hv)
        for o_ref, o in zip(o_refs, ov):
            o_ref[...] = o.astype(o_ref.dtype)
        if has_state:
            st_ref[...] = new_state

    row = lambda b, n: (b * nt + n, 0)
    in_specs = [pl.BlockSpec(p.shape, lambda b, n: (0, 0)) for p in params]
    in_specs += [pl.BlockSpec((r, a.shape[1]), row) for a in ins]
    if use_halo:
        in_specs += [pl.BlockSpec((8, a.shape[1]), _halo_map(nt, r)) for a in ins]
    col = lambda b, n: (0, b * nt + n)
    out_shape, o_specs = [], []
    for w, dt, *transposed in out_specs:
        out_shape.append(jax.ShapeDtypeStruct((w, nb * s) if transposed else (nb * s, w), dt))
        o_specs.append(pl.BlockSpec((w, r), col) if transposed else pl.BlockSpec((r, w), row))
    scratch = []
    if has_state:
        out_shape.append(jax.ShapeDtypeStruct((nb, nt) + tuple(state_shape), F32))
        o_specs.append(pl.BlockSpec((1, 1) + tuple(state_shape), lambda b, n: (b, n, 0, 0)))
        scratch.append(pltpu.VMEM(tuple(state_shape), F32))
    args = list(params) + list(ins) + (list(ins) if use_halo else [])
    if side is not None:
        body, in_specs, o_specs, out_shape, scratch, args = _host_exchange(
            side, body, in_specs, o_specs, out_shape, scratch, args, (nb, nt))
    return pl.pallas_call(body, grid=(nb, nt), in_specs=in_specs, out_specs=o_specs, out_shape=out_shape,
                          scratch_shapes=scratch, compiler_params=_cparams(("arbitrary", "arbitrary")),
                          name=name)(*args)


def seq_bwd(name, tile_fn, params, ins, use_halo, states, douts, din_dtypes, state_shape, nb, s, r, side=None):
    nt = s // r
    n_p, n_i, n_o = len(params), len(ins), len(douts)
    has_state = state_shape is not None

    def body(*refs):
        p_refs, i_refs = refs[:n_p], refs[n_p:n_p + n_i]
        h_refs = refs[n_p + n_i:n_p + 2 * n_i] if use_halo else ()
        k = n_p + n_i + len(h_refs)
        sv_ref = None
        if has_state:
            sv_ref = refs[k]
            k += 1
        do_refs = refs[k:k + n_o]
        k += n_o
        di_refs, dp_refs = refs[k:k + n_i], refs[k + n_i:k + n_i + n_p]
        k += n_i + n_p
        dst_ref = None
        if has_state:
            dst_ref = refs[k]
            k += 1
        dh_refs = refs[k:k + len(h_refs)]
        b, nn = pl.program_id(0), pl.program_id(1)
        n = nt - 1 - nn

        @pl.when((b == 0) & (nn == 0))
        def _():
            for dp in dp_refs:
                dp[...] = jnp.zeros(dp.shape, F32)

        @pl.when(nn == 0)
        def _():
            if has_state:
                dst_ref[...] = jnp.zeros(state_shape, F32)
            for dh in dh_refs:
                dh[...] = jnp.zeros(dh.shape, F32)

        pv = [p[...] for p in p_refs]
        iv = [i[...].astype(F32) for i in i_refs]
        hv = [jnp.where(n > 0, h[...].astype(F32), 0.0) for h in h_refs]
        if has_state:
            f = lambda pv_, st_, iv_, hv_: tile_fn(pv_, st_, iv_, hv_)
            _, vjp = jax.vjp(f, pv, sv_ref[0, 0], iv, hv)
            dpv, dst, div, dhv = vjp((dst_ref[...], [d[...].astype(F32) for d in do_refs]))
            dst_ref[...] = dst
        else:
            f = lambda pv_, iv_, hv_: tile_fn(pv_, None, iv_, hv_)[1]
            _, vjp = jax.vjp(f, pv, iv, hv)
            dpv, div, dhv = vjp([d[...].astype(F32) for d in do_refs])
        for j, (di_ref, d) in enumerate(zip(di_refs, div)):
            if use_halo:
                d = jnp.concatenate([d[:r - 8], d[r - 8:] + dh_refs[j][...]], axis=0)
            di_ref[...] = d.astype(di_ref.dtype)
        for dh_ref, d in zip(dh_refs, dhv):
            dh_ref[...] = d
        for dp_ref, d in zip(dp_refs, dpv):
            dp_ref[...] += d

    row = lambda b, nn: (b * nt + nt - 1 - nn, 0)
    hmap = _halo_map(nt, r)
    in_specs = [pl.BlockSpec(p.shape, lambda b, nn: (0, 0)) for p in params]
    in_specs += [pl.BlockSpec((r, a.shape[1]), row) for a in ins]
    if use_halo:
        in_specs += [pl.BlockSpec((8, a.shape[1]), lambda b, nn: hmap(b, nt - 1 - nn)) for a in ins]
    args = list(params) + list(ins) + (list(ins) if use_halo else [])
    scratch = []
    if has_state:
        in_specs.append(pl.BlockSpec((1, 1) + tuple(state_shape), lambda b, nn: (b, nt - 1 - nn, 0, 0)))
        args.append(states)
        scratch.append(pltpu.VMEM(tuple(state_shape), F32))
    in_specs += [pl.BlockSpec((r, d.shape[1]), row) for d in douts]
    args += list(douts)
    if use_halo:
        scratch += [pltpu.VMEM((8, a.shape[1]), F32) for a in ins]
    out_shape = [jax.ShapeDtypeStruct(a.shape, dt) for a, dt in zip(ins, din_dtypes)]
    out_shape += [jax.ShapeDtypeStruct(p.shape, F32) for p in params]
    o_specs = [pl.BlockSpec((r, a.shape[1]), row) for a in ins]
    o_specs += [pl.BlockSpec(p.shape, lambda b, nn: (0, 0)) for p in params]
    if side is not None:
        body, in_specs, o_specs, out_shape, scratch, args = _host_exchange(
            side, body, in_specs, o_specs, out_shape, scratch, args, (nb, nt))
    res = pl.pallas_call(body, grid=(nb, nt), in_specs=in_specs, out_specs=o_specs, out_shape=out_shape,
                         scratch_shapes=scratch, compiler_params=_cparams(("arbitrary", "arbitrary")),
                         name=name)(*args)
    if side is not None:
        return res[:n_i], res[n_i:n_i + n_p], res[n_i + n_p:]
    return res[:n_i], res[n_i:]


def matmul(name, a, b, mode, out_dtype=F32, addend=None, tm=512, tn=None, tk=None):
    if mode == "nn":
        (m, kd), (_, n) = a.shape, b.shape
    elif mode == "nt":
        (m, kd), (n, _) = a.shape, b.shape
    else:
        (kd, m), (_, n) = a.shape, b.shape
    tm, tn, tk = min(tm, m), min(tn or n, n), min(tk or kd, kd)
    nk = kd // tk
    assert m % tm == 0 and n % tn == 0 and kd % tk == 0
    dims = {"nn": ((1,), (0,)), "nt": ((1,), (1,)), "tn": ((0,), (0,))}[mode]
    has_add = addend is not None

    def body(*refs):
        a_ref, b_ref = refs[0], refs[1]
        add_ref = refs[2] if has_add else None
        o_ref = refs[2 + has_add]
        part = lax.dot_general(a_ref[...].astype(BF16), b_ref[...].astype(BF16), (dims, ((), ())),
                               preferred_element_type=F32)

        def finish(acc):
            if has_add:
                acc = acc + add_ref[...]
            o_ref[...] = acc.astype(o_ref.dtype)

        if nk == 1:
            finish(part)
        else:
            acc_ref = refs[3 + has_add]
            k = pl.program_id(2)

            @pl.when(k == 0)
            def _():
                acc_ref[...] = part

            @pl.when(k > 0)
            def _():
                acc_ref[...] += part

            @pl.when(k == nk - 1)
            def _():
                finish(acc_ref[...])

    if mode == "tn":
        a_spec = pl.BlockSpec((tk, tm), lambda j, i, k: (k, i))
    else:
        a_spec = pl.BlockSpec((tm, tk), lambda j, i, k: (i, k))
    if mode == "nt":
        b_spec = pl.BlockSpec((tn, tk), lambda j, i, k: (j, k))
    else:
        b_spec = pl.BlockSpec((tk, tn), lambda j, i, k: (k, j))
    o_spec = pl.BlockSpec((tm, tn), lambda j, i, k: (i, j))
    in_specs, args = [a_spec, b_spec], [a, b]
    if has_add:
        in_specs.append(o_spec)
        args.append(addend)
    scratch = [pltpu.VMEM((tm, tn), F32)] if nk > 1 else []
    return pl.pallas_call(body, grid=(n // tn, m // tm, nk), in_specs=in_specs, out_specs=o_spec,
                          out_shape=jax.ShapeDtypeStruct((m, n), out_dtype), scratch_shapes=scratch,
                          compiler_params=_cparams(("parallel", "parallel", "arbitrary")), name=name)(*args)


def _normmod(x, nw, shift, scale):
    return _rms(x, nw) * (1.0 + scale) + shift


def _row_specs(nb, s, tr, d):
    nt = s // tr
    row = pl.BlockSpec((tr, d), lambda b, i: (b * nt + i, 0))
    per_seq = pl.BlockSpec((1, 1, d), lambda b, i: (b, 0, 0))
    full = pl.BlockSpec((1, d), lambda b, i: (0, 0))
    return nt, row, per_seq, full


def normmod_fwd(name, x, nw, shift, scale, nb, s, tr=512):
    d, tr = x.shape[1], min(tr, s)
    nt, row, per_seq, full = _row_specs(nb, s, tr, d)

    def body(x_ref, nw_ref, sh_ref, sc_ref, h_ref, ht_ref):
        h = _normmod(x_ref[...], nw_ref[...], sh_ref[0], sc_ref[0])
        h_ref[...] = h.astype(h_ref.dtype)
        ht_ref[...] = h.T.astype(ht_ref.dtype)

    return pl.pallas_call(body, grid=(nb, nt), in_specs=[row, full, per_seq, per_seq],
                          out_specs=[row, pl.BlockSpec((d, tr), lambda b, i: (0, b * nt + i))],
                          out_shape=[jax.ShapeDtypeStruct(x.shape, BF16), jax.ShapeDtypeStruct(x.shape[::-1], BF16)],
                          compiler_params=_cparams(("parallel", "parallel")), name=name)(x, nw, shift, scale)


def normmod_bwd(name, x, nw, shift, scale, dh, dres, nb, s, tr=512):
    d, tr = x.shape[1], min(tr, s)
    nt, row, per_seq, full = _row_specs(nb, s, tr, d)

    def body(x_ref, nw_ref, sh_ref, sc_ref, dh_ref, dres_ref, dx_ref, dnw_ref, dsh_ref, dsc_ref):
        b, i = pl.program_id(0), pl.program_id(1)

        @pl.when((b == 0) & (i == 0))
        def _():
            dnw_ref[...] = jnp.zeros(dnw_ref.shape, F32)

        @pl.when(i == 0)
        def _():
            dsh_ref[...] = jnp.zeros(dsh_ref.shape, F32)
            dsc_ref[...] = jnp.zeros(dsc_ref.shape, F32)

        _, vjp = jax.vjp(_normmod, x_ref[...], nw_ref[...], sh_ref[0], sc_ref[0])
        dx, dnw, dsh, dsc = vjp(dh_ref[...])
        dx_ref[...] = dres_ref[...] + dx
        dnw_ref[...] += dnw
        dsh_ref[0] += dsh
        dsc_ref[0] += dsc

    out_shape = [jax.ShapeDtypeStruct(x.shape, F32), jax.ShapeDtypeStruct((1, d), F32),
                 jax.ShapeDtypeStruct((nb, 1, d), F32), jax.ShapeDtypeStruct((nb, 1, d), F32)]
    return pl.pallas_call(body, grid=(nb, nt), in_specs=[row, full, per_seq, per_seq, row, row],
                          out_specs=[row, full, per_seq, per_seq], out_shape=out_shape,
                          compiler_params=_cparams(("arbitrary", "arbitrary")),
                          name=name)(x, nw, shift, scale, dh, dres)


def _merge(oa, ob, oc, graw, gate1, wa, wb, wc, wo):
    d = wo.shape[0]
    g = _sigmoid(graw)
    merged = g[:, :d] * bdot(oa, wa) + g[:, d:2 * d] * bdot(ob, wb) + g[:, 2 * d:] * bdot(oc, wc)
    return gate1 * bdot(merged, wo)


def _merge_specs(nb, s, tr, d, wbr):
    nt, row, per_seq, _ = _row_specs(nb, s, tr, d)
    o_spec = pl.BlockSpec((tr, wbr), lambda b, i: (b * nt + i, 0))
    g_spec = pl.BlockSpec((tr, 3 * d), lambda b, i: (b * nt + i, 0))
    wbr_spec = pl.BlockSpec((wbr, d), lambda b, i: (0, 0))
    wo_spec = pl.BlockSpec((d, d), lambda b, i: (0, 0))
    return nt, row, per_seq, o_spec, g_spec, wbr_spec, wo_spec


def merge_fwd(name, x, oa, ob, oc, pg, gate1, wa, wb, wc, wo, nb, s, tr=512):
    d, tr = x.shape[1], min(tr, s)
    nt, row, per_seq, o_spec, g_spec, wbr_spec, wo_spec = _merge_specs(nb, s, tr, d, oa.shape[1])

    def body(x_ref, oa_ref, ob_ref, oc_ref, pg_ref, g1_ref, wa_ref, wb_ref, wc_ref, wo_ref, x1_ref):
        x1_ref[...] = x_ref[...] + _merge(oa_ref[...], ob_ref[...], oc_ref[...], pg_ref[...], g1_ref[0],
                                          wa_ref[...], wb_ref[...], wc_ref[...], wo_ref[...])

    return pl.pallas_call(body, grid=(nb, nt),
                          in_specs=[row, o_spec, o_spec, o_spec, g_spec, per_seq, wbr_spec, wbr_spec, wbr_spec, wo_spec],
                          out_specs=row, out_shape=jax.ShapeDtypeStruct(x.shape, F32),
                          compiler_params=_cparams(("parallel", "parallel")),
                          name=name)(x, oa, ob, oc, pg, gate1, wa, wb, wc, wo)


def merge_bwd(name, oa, ob, oc, pg, gate1, wa, wb, wc, wo, dx1, nb, s, tr=256):
    d, tr = dx1.shape[1], min(tr, s)
    wbr = oa.shape[1]
    nt, row, per_seq, o_spec, g_spec, wbr_spec, wo_spec = _merge_specs(nb, s, tr, d, wbr)

    def body(oa_ref, ob_ref, oc_ref, pg_ref, g1_ref, wa_ref, wb_ref, wc_ref, wo_ref, dx_ref,
             doa_ref, dob_ref, doc_ref, dpg_ref, dg1_ref, dwa_ref, dwb_ref, dwc_ref, dwo_ref):
        b, i = pl.program_id(0), pl.program_id(1)

        @pl.when((b == 0) & (i == 0))
        def _():
            for r in (dwa_ref, dwb_ref, dwc_ref, dwo_ref):
                r[...] = jnp.zeros(r.shape, F32)

        @pl.when(i == 0)
        def _():
            dg1_ref[...] = jnp.zeros(dg1_ref.shape, F32)

        args = [oa_ref[...].astype(F32), ob_ref[...].astype(F32), oc_ref[...].astype(F32), pg_ref[...], g1_ref[0],
                wa_ref[...].astype(F32), wb_ref[...].astype(F32), wc_ref[...].astype(F32), wo_ref[...].astype(F32)]
        _, vjp = jax.vjp(_merge, *args)
        doa, dob, doc, dpg, dg1, dwa, dwb, dwc, dwo = vjp(dx_ref[...])
        doa_ref[...] = doa
        dob_ref[...] = dob
        doc_ref[...] = doc
        dpg_ref[...] = dpg
        dg1_ref[0] += dg1
        dwa_ref[...] += dwa
        dwb_ref[...] += dwb
        dwc_ref[...] += dwc
        dwo_ref[...] += dwo

    t = nb * s
    out_shape = ([jax.ShapeDtypeStruct((t, wbr), F32)] * 3
                 + [jax.ShapeDtypeStruct((t, 3 * d), F32), jax.ShapeDtypeStruct((nb, 1, d), F32)]
                 + [jax.ShapeDtypeStruct((wbr, d), F32)] * 3 + [jax.ShapeDtypeStruct((d, d), F32)])
    return pl.pallas_call(body, grid=(nb, nt),
                          in_specs=[o_spec, o_spec, o_spec, g_spec, per_seq, wbr_spec, wbr_spec, wbr_spec, wo_spec, row],
                          out_specs=[o_spec, o_spec, o_spec, g_spec, per_seq, wbr_spec, wbr_spec, wbr_spec, wo_spec],
                          out_shape=out_shape, compiler_params=_cparams(("arbitrary", "arbitrary")),
                          name=name)(oa, ob, oc, pg, gate1, wa, wb, wc, wo, dx1)


def resid_fwd(name, x, f, gate, nb, s, tr=512):
    d, tr = x.shape[1], min(tr, s)
    nt, row, per_seq, _ = _row_specs(nb, s, tr, d)

    def body(x_ref, f_ref, g_ref, o_ref):
        o_ref[...] = x_ref[...] + g_ref[0] * f_ref[...]

    return pl.pallas_call(body, grid=(nb, nt), in_specs=[row, row, per_seq], out_specs=row,
                          out_shape=jax.ShapeDtypeStruct(x.shape, F32),
                          compiler_params=_cparams(("parallel", "parallel")), name=name)(x, f, gate)


def resid_bwd(name, dx, f, gate, nb, s, tr=512):
    d, tr = dx.shape[1], min(tr, s)
    nt, row, per_seq, _ = _row_specs(nb, s, tr, d)

    def body(dx_ref, f_ref, g_ref, df_ref, dg_ref):
        @pl.when(pl.program_id(1) == 0)
        def _():
            dg_ref[...] = jnp.zeros(dg_ref.shape, F32)

        df_ref[...] = (g_ref[0] * dx_ref[...]).astype(df_ref.dtype)
        dg_ref[0] += jnp.sum(dx_ref[...] * f_ref[...], axis=0, keepdims=True)

    return pl.pallas_call(body, grid=(nb, nt), in_specs=[row, row, per_seq], out_specs=[row, per_seq],
                          out_shape=[jax.ShapeDtypeStruct(dx.shape, BF16), jax.ShapeDtypeStruct((nb, 1, d), F32)],
                          compiler_params=_cparams(("arbitrary", "arbitrary")), name=name)(dx, f, gate)


def loss_head(name, x, fw, target, tr=512):
    t, d = x.shape
    row = pl.BlockSpec((tr, d), lambda i: (i, 0))
    full = pl.BlockSpec((1, d), lambda i: (0, 0))

    def loss_fn(xv, fwv, tv):
        err = _rms(xv, fwv) - tv
        return 0.5 * jnp.sum(jnp.mean(err * err, axis=-1))

    def body(x_ref, fw_ref, t_ref, dx_ref, l_ref, dfw_ref):
        @pl.when(pl.program_id(0) == 0)
        def _():
            l_ref[...] = jnp.zeros(l_ref.shape, F32)
            dfw_ref[...] = jnp.zeros(dfw_ref.shape, F32)

        val, (dx, dfw) = jax.value_and_grad(loss_fn, argnums=(0, 1))(x_ref[...], fw_ref[...], t_ref[...])
        dx_ref[...] = dx
        l_ref[...] += val
        dfw_ref[...] += dfw

    return pl.pallas_call(body, grid=(t // tr,), in_specs=[row, full, row],
                          out_specs=[row, pl.BlockSpec((1, 128), lambda i: (0, 0)), full],
                          out_shape=[jax.ShapeDtypeStruct((t, d), F32), jax.ShapeDtypeStruct((1, 128), F32),
                                     jax.ShapeDtypeStruct((1, d), F32)],
                          compiler_params=_cparams(("arbitrary",)), name=name)(x, fw, target)


def _row_tile(rows, cols, n_arrays):
    budget = 24 * 1024 * 1024 // (8 * cols * max(n_arrays, 1))
    tr = rows
    while tr > max(budget, 16) and tr % 2 == 0 and (tr // 2) % 16 == 0:
        tr //= 2
    return tr


def elementwise(name, fn, ins, out_dtypes):
    rows, cols = ins[0].shape
    tr = _row_tile(rows, cols, len(ins) + len(out_dtypes))
    spec = pl.BlockSpec((tr, cols), lambda i: (i, 0))
    n_in = len(ins)

    def body(*refs):
        outs = fn(*[r[...] for r in refs[:n_in]])
        for o_ref, o in zip(refs[n_in:], outs):
            o_ref[...] = o.astype(o_ref.dtype)

    return pl.pallas_call(body, grid=(rows // tr,), in_specs=[spec] * n_in, out_specs=[spec] * len(out_dtypes),
                          out_shape=[jax.ShapeDtypeStruct((rows, cols), dt) for dt in out_dtypes],
                          compiler_params=_cparams(("parallel",)), name=name)(*ins)


def _adamw(w, g, m, v):
    m = ADAM_B1 * m + (1.0 - ADAM_B1) * g
    v = ADAM_B2 * v + (1.0 - ADAM_B2) * (g * g)
    m_hat = m / (1.0 - ADAM_B1 ** ADAM_STEP)
    v_hat = v / (1.0 - ADAM_B2 ** ADAM_STEP)
    delta = -ADAM_LR * (m_hat / (jnp.sqrt(v_hat) + ADAM_EPS) + ADAM_WD * w)
    return delta, m, v


def adamw(name, w, g, m, v):
    return elementwise(name, _adamw, [w, g, m, v], [F32, F32, F32])


_ANY = pl.BlockSpec(memory_space=pl.ANY)


def _coords():
    return lax.axis_index("x"), lax.axis_index("y"), lax.axis_index("c")


def allgather8(name, arrays, halves):
    n = len(arrays)

    def body(*refs):
        in_refs, out_refs = refs[:n], refs[n:2 * n]
        send_sems, recv_sems, local_sems = refs[2 * n:]
        x, y, c = _coords()
        me, sibling = (x, y, c), (x, y, 1 - c)
        chips = [(1 - x, y), (x, 1 - y), (1 - x, 1 - y)]

        def blk(i, px, py, pc):
            return out_refs[i].at[4 * px + 2 * py + pc]

        def piece(i):
            return in_refs[i].at[c] if halves[i] else in_refs[i]

        def copy(i, k, block, to, src=None):
            return pltpu.make_async_remote_copy(
                src_ref=blk(i, *block) if src is None else src, dst_ref=blk(i, *block),
                send_sem=send_sems.at[7 * i + k], recv_sem=recv_sems.at[7 * i + k],
                device_id=to, device_id_type=MESH)

        mine = [pltpu.make_async_copy(piece(i), blk(i, *me), local_sems.at[i]) for i in range(n)]
        for cp in mine:
            cp.start()
        first = []
        for i in range(n):
            first.append(copy(i, 0, me, sibling, src=piece(i)))
            first += [copy(i, 1 + j, me, (*chip, c), src=piece(i)) for j, chip in enumerate(chips)]
        for cp in first:
            cp.start()
        passed = []
        for j, chip in enumerate(chips):
            for i in range(n):
                copy(i, 1 + j, (*chip, c), me).wait_recv()
                fwd = copy(i, 4 + j, (*chip, c), sibling)
                fwd.start()
                passed.append(fwd)
        for i in range(n):
            copy(i, 0, sibling, me).wait_recv()
            for j, chip in enumerate(chips):
                copy(i, 4 + j, (*chip, 1 - c), me).wait_recv()
        for cp in first + passed:
            cp.wait_send()
        for cp in mine:
            cp.wait()

    out_shape = []
    for a, hv in zip(arrays, halves):
        out_shape.append(jax.ShapeDtypeStruct((N_DEV,) + tuple(a.shape[1:] if hv else a.shape), a.dtype))
    return pl.pallas_call(
        body, in_specs=[_ANY] * n, out_specs=[_ANY] * n, out_shape=out_shape,
        scratch_shapes=[pltpu.SemaphoreType.DMA((7 * n,)), pltpu.SemaphoreType.DMA((7 * n,)),
                        pltpu.SemaphoreType.DMA((n,))],
        name=name)(*arrays)


def exchange(name, ins, out_shapes, plan, local_plan=(), in_place=False):
    n_in, n_out = len(ins), len(out_shapes)

    def body(*refs):
        copies = _exchange_copies(plan, local_plan, refs[:n_in], refs[n_in:n_in + n_out], *refs[n_in + n_out:])
        for cp in copies:
            cp.start()
        for cp in copies:
            cp.wait()

    return pl.pallas_call(
        body, in_specs=[_ANY] * n_in, out_specs=[_ANY] * n_out, out_shape=out_shapes,
        scratch_shapes=_exchange_sems(plan, local_plan),
        input_output_aliases={i: i for i in range(n_in)} if in_place else {},
        name=name)(*ins)


def sum_halves(name, gs, recv, c_arr):
    _, _, hr, cs = gs.shape
    tr = _row_tile(hr, cs, 4)

    def body(c_ref, g_ref, r_ref, qf_ref, qb_ref):
        q = g_ref[0, 0] + r_ref[0, 0]
        qf_ref[0] = q
        qb_ref[0] = q.astype(BF16)

    grid_spec = pltpu.PrefetchScalarGridSpec(
        num_scalar_prefetch=1, grid=(N_CHIPS, hr // tr),
        in_specs=[pl.BlockSpec((1, 1, tr, cs), lambda j, i, c_ref: (j, c_ref[0], i, 0)),
                  pl.BlockSpec((1, 1, tr, cs), lambda j, i, c_ref: (j, 0, i, 0))],
        out_specs=[pl.BlockSpec((1, tr, cs), lambda j, i, c_ref: (j, i, 0))] * 2)
    return pl.pallas_call(body, grid_spec=grid_spec,
                          out_shape=[jax.ShapeDtypeStruct((N_CHIPS, hr, cs), F32),
                                     jax.ShapeDtypeStruct((N_CHIPS, hr, cs), BF16)],
                          compiler_params=_cparams(("parallel", "parallel")), name=name)(c_arr, gs, recv)


def sum_chips(name, qf, recv, chip_arr):
    _, hr, cs = qf.shape
    tr = _row_tile(hr, cs, 4)

    def body(chip_ref, q_ref, a_ref, b_ref, c_ref, o_ref):
        o_ref[...] = q_ref[0] + a_ref[...].astype(F32) + b_ref[...].astype(F32) + c_ref[...].astype(F32)

    row = pl.BlockSpec((tr, cs), lambda i, chip_ref: (i, 0))
    grid_spec = pltpu.PrefetchScalarGridSpec(
        num_scalar_prefetch=1, grid=(hr // tr,),
        in_specs=[pl.BlockSpec((1, tr, cs), lambda i, chip_ref: (chip_ref[0], i, 0)), row, row, row],
        out_specs=row)
    return pl.pallas_call(body, grid_spec=grid_spec, out_shape=jax.ShapeDtypeStruct((hr, cs), F32),
                          compiler_params=_cparams(("parallel",)), name=name)(chip_arr, qf, *recv)


def adamw_halves(name, w, m, v, layer, g_mine, g_other, c_arr, prev=None):
    _, _, hr, cs = w.shape
    tr = _row_tile(hr, cs, 9)

    def body(c_ref, w_ref, m_ref, v_ref, gm_ref, go_ref, *rest):
        g_ref, d_ref, nm_ref, nv_ref = rest[-4:]
        g = jnp.where(pl.program_id(0) == c_ref[0], gm_ref[...], go_ref[...])
        delta, nm, nv = _adamw(w_ref[0, 0], g, m_ref[0, 0], v_ref[0, 0])
        g_ref[0, 0], d_ref[0, 0], nm_ref[0, 0], nv_ref[0, 0] = g, delta, nm, nv

    half = pl.BlockSpec((1, 1, tr, cs), lambda h, i, c_ref: (layer, h, i, 0))
    row = pl.BlockSpec((tr, cs), lambda h, i, c_ref: (i, 0))
    in_specs, args, aliases = [half, half, half, row, row], [c_arr, w, m, v, g_mine, g_other], {}
    if prev is not None:
        in_specs += [pl.BlockSpec(memory_space=pl.ANY)] * 4
        args += list(prev)
        aliases = {6 + k: k for k in range(4)}
    grid_spec = pltpu.PrefetchScalarGridSpec(num_scalar_prefetch=1, grid=(2, hr // tr),
                                             in_specs=in_specs, out_specs=[half] * 4)
    return pl.pallas_call(body, grid_spec=grid_spec, out_shape=[jax.ShapeDtypeStruct(w.shape, F32)] * 4,
                          input_output_aliases=aliases, compiler_params=_cparams(("parallel", "parallel")),
                          name=name)(*args)


def sum8(name, g):
    _, rows, cols = g.shape
    tr = _row_tile(rows, cols, 9)

    def body(*refs):
        acc = refs[0][0]
        for r in refs[1:N_DEV]:
            acc = acc + r[0]
        refs[N_DEV][...] = acc

    in_specs = [pl.BlockSpec((1, tr, cols), functools.partial(lambda k, i: (k, i, 0), k)) for k in range(N_DEV)]
    return pl.pallas_call(body, grid=(rows // tr,), in_specs=in_specs,
                          out_specs=pl.BlockSpec((tr, cols), lambda i: (i, 0)),
                          out_shape=jax.ShapeDtypeStruct((rows, cols), F32),
                          compiler_params=_cparams(("parallel",)), name=name)(*([g] * N_DEV))


_QKV, _AB, _GZ = (0, 1536), (1536, 1544), (1544, 2056)
_HG = (2056, 4104)
_SZ, _XBC, _DT = (4104, 4616), (4616, 5640), (5640, 5648)
_GATES = (5648, 8720)


def _cols(w, rng):
    return w[..., rng[0]:rng[1]]


def _split_w_in(w):
    pad = jnp.zeros(w.shape[:-1] + (120,), w.dtype)
    return (_cols(w, _GATES),
            jnp.concatenate([_cols(w, _QKV), _cols(w, _GZ), _cols(w, _AB), pad], axis=-1),
            _cols(w, _HG),
            jnp.concatenate([_cols(w, _SZ), _cols(w, _XBC), _cols(w, _DT), pad], axis=-1))


def _join_w_in(g, a, b, c):
    return jnp.concatenate([a[..., 0:1536], a[..., 2048:2056], a[..., 1536:2048], b,
                            c[..., 0:512], c[..., 512:1536], c[..., 1536:1544], g], axis=-1)


def _rows8(rows, width):
    out = [jnp.pad(r.astype(F32), (0, width - r.shape[0])) for r in rows]
    out += [jnp.zeros((width,), F32)] * (8 - len(out))
    return jnp.stack(out)


class _Packer:
    def __init__(self):
        self.items, self.size = [], 0

    def add(self, name, shape):
        n = 1
        for d in shape:
            n *= d
        self.items.append((name, tuple(shape), self.size, n))
        self.size += n

    def rows(self):
        return -(-self.size // 8192) * 8

    def pack(self, values):
        flat = [values[name].astype(F32).reshape(-1) for name, _, _, _ in self.items]
        flat.append(jnp.zeros((self.rows() * 1024 - self.size,), F32))
        return jnp.concatenate(flat).reshape(self.rows(), 1024)

    def unpack(self, buf):
        flat = buf.reshape(-1)
        return {name: flat[off:off + n].reshape(shape) for name, shape, off, n in self.items}


def _stack_by_chip(g, axis):
    l, r, c = g.shape
    if axis == 2:
        cs = c // N_CHIPS
        g = g.reshape(l, r, N_CHIPS, cs).transpose(2, 0, 1, 3).reshape(N_CHIPS, 2, l * r // 2, cs)
    else:
        rs = r // N_CHIPS
        g = g.reshape(l, N_CHIPS, rs, c).transpose(1, 0, 2, 3).reshape(N_CHIPS, 2, l * rs // 2, c)
    return g


def _unstack_gathered(w8, l, axis):
    _, hr, cs = w8.shape
    w = w8.reshape(N_CHIPS, l, 2 * hr // l, cs)
    if axis == 2:
        return w.transpose(1, 2, 0, 3).reshape(l, 2 * hr // l, N_CHIPS * cs)
    return w.transpose(1, 0, 2, 3).reshape(l, N_CHIPS * 2 * hr // l, cs)


_BIG = (("w_in", 2), ("w_br_a", 2), ("w_br_b", 2), ("w_br_c", 2), ("w_out", 1), ("ffn_w_up", 2), ("ffn_w_down", 1))
_SMALL = ("b_ada", "norm1_w", "gdn_conv_w", "gdn_a_log", "gdn_dt_bias", "gdn_norm_w", "hgrn_lb_param",
          "hgrn_norm_w", "ssd_conv_w", "ssd_conv_b", "ssd_a_log", "ssd_dt_bias", "ssd_d", "ssd_norm_w",
          "norm2_w", "ffn_conv_w", "ffn_conv_b", "final_norm_w")
_WEIGHTS = ("w_ada", "b_ada", "norm1_w", "w_in", "gdn_conv_w", "gdn_a_log", "gdn_dt_bias", "gdn_norm_w",
            "hgrn_lb_param", "hgrn_norm_w", "ssd_conv_w", "ssd_conv_b", "ssd_a_log", "ssd_dt_bias", "ssd_d",
            "ssd_norm_w", "w_br_a", "w_br_b", "w_br_c", "w_out", "norm2_w", "ffn_w_up", "ffn_conv_w",
            "ffn_conv_b", "ffn_w_down", "final_norm_w")
_R_GDN, _R_HGRN, _R_SSD, _R_FFN = 128, 128, 128, 256


_MASKS = ((1, 0), (0, 1), (1, 1))


def _flip(k, x, y):
    return (1 - x if _MASKS[k][0] else x), (1 - y if _MASKS[k][1] else y)


def _rs_begin(grads, tag):
    n = len(grads)
    c_arr = lax.axis_index("c").astype(jnp.int32).reshape(1)
    plan = [functools.partial(lambda i, ins, outs, x, y, c: (ins[i].at[:, pl.ds(1 - c, 1)], outs[i], (x, y, 1 - c)), i)
            for i in range(n)]
    recv = exchange("rs_d2d" + tag, grads, [jax.ShapeDtypeStruct((N_CHIPS, 1) + g.shape[2:], F32) for g in grads], plan)
    q = [sum_halves("rs_sum_d2d%s_%d" % (tag, i), g, r, c_arr) for i, (g, r) in enumerate(zip(grads, recv))]
    qf, qb = [a for a, _ in q], [b for _, b in q]

    def ici(i, k, ins, outs, x, y, c):
        px, py = _flip(k, x, y)
        return ins[i].at[2 * px + py], outs[3 * i + k], (px, py, c)

    plan = [functools.partial(ici, i, k) for i in range(n) for k in range(3)]
    shapes = [jax.ShapeDtypeStruct(g.shape[2:], BF16) for g in grads for _ in range(3)]
    return qf, (qb, shapes, plan, ())


def _rs_finish(qf, res, tag):
    n = len(qf)
    chip_arr = (2 * lax.axis_index("x") + lax.axis_index("y")).astype(jnp.int32).reshape(1)
    red = [sum_chips("rs_sum_ici%s_%d" % (tag, i), qf[i], res[3 * i:3 * i + 3], chip_arr) for i in range(n)]
    plan = [functools.partial(lambda i, ins, outs, x, y, c: (ins[i], outs[i], (x, y, 1 - c)), i) for i in range(n)]
    other = exchange("rs_swap" + tag, red, [jax.ShapeDtypeStruct(r.shape, F32) for r in red], plan)
    return red, other


def _gather_side(pieces):
    n = len(pieces)

    def send(i, k, ins, outs, x, y, c):
        px, py = _flip(k, x, y)
        return ins[i].at[c], outs[i].at[2 * (2 * x + y) + c], (px, py, c)

    def to_sibling(i, h, ins, outs, x, y, c):
        return ins[i].at[h], outs[i].at[2 * (2 * x + y) + h], (x, y, 1 - c)

    plan = [functools.partial(send, i, k) for i in range(n) for k in range(3)]
    plan += [functools.partial(to_sibling, i, h) for i in range(n) for h in range(2)]
    shapes = [jax.ShapeDtypeStruct((N_DEV,) + p.shape[1:], p.dtype) for p in pieces]
    return pieces, shapes, plan, ()


def _gather_finish(name, bufs):
    n = len(bufs)

    def pass_on(i, k, ins, outs, x, y, c):
        px, py = _flip(k, x, y)
        blk = 2 * (2 * px + py) + c
        return ins[i].at[blk], outs[i].at[blk], (x, y, 1 - c)

    plan = [functools.partial(pass_on, i, k) for i in range(n) for k in range(3)]
    return exchange(name, bufs, [jax.ShapeDtypeStruct(b.shape, b.dtype) for b in bufs], plan, in_place=True)


def kernel(x, c, w_ada, b_ada, norm1_w, w_in, gdn_conv_w, gdn_a_log, gdn_dt_bias, gdn_norm_w, hgrn_lb_param, hgrn_norm_w, ssd_conv_w, ssd_conv_b, ssd_a_log, ssd_dt_bias, ssd_d, ssd_norm_w, w_br_a, w_br_b, w_br_c, w_out, norm2_w, ffn_w_up, ffn_conv_w, ffn_conv_b, ffn_w_down, final_norm_w, loss_target, m_w_ada, m_b_ada, m_norm1_w, m_w_in, m_gdn_conv_w, m_gdn_a_log, m_gdn_dt_bias, m_gdn_norm_w, m_hgrn_lb_param, m_hgrn_norm_w, m_ssd_conv_w, m_ssd_conv_b, m_ssd_a_log, m_ssd_dt_bias, m_ssd_d, m_ssd_norm_w, m_w_br_a, m_w_br_b, m_w_br_c, m_w_out, m_norm2_w, m_ffn_w_up, m_ffn_conv_w, m_ffn_conv_b, m_ffn_w_down, m_final_norm_w, v_w_ada, v_b_ada, v_norm1_w, v_w_in, v_gdn_conv_w, v_gdn_a_log, v_gdn_dt_bias, v_gdn_norm_w, v_hgrn_lb_param, v_hgrn_norm_w, v_ssd_conv_w, v_ssd_conv_b, v_ssd_a_log, v_ssd_dt_bias, v_ssd_d, v_ssd_norm_w, v_w_br_a, v_w_br_b, v_w_br_c, v_w_out, v_norm2_w, v_ffn_w_up, v_ffn_conv_w, v_ffn_conv_b, v_ffn_w_down, v_final_norm_w):
    loc = dict(locals())
    w = {k: loc[k] for k in _WEIGHTS}
    mom = {k: loc["m_" + k] for k in _WEIGHTS}
    var = {k: loc["v_" + k] for k in _WEIGHTS}
    nb, s, d = x.shape
    t = nb * s
    depth = w_ada.shape[0]
    chip = 2 * lax.axis_index("x") + lax.axis_index("y")
    dev = 2 * chip + lax.axis_index("c")
    x0 = x.reshape(t, d)
    target = loss_target.reshape(t, d)

    small_in = [c, gdn_conv_w.reshape(depth * 4, -1), ssd_conv_w.reshape(depth * 4, -1),
                ffn_conv_w.reshape(depth * 3, -1)]
    c_all, gcw, scw, fcw = allgather8("ag_small", small_in, [False] * 4)
    c_all = c_all.reshape(N_DEV * nb, d)

    def conv_full(g, taps):
        g = g[::2].reshape(N_CHIPS, depth, taps, -1)
        return g.transpose(1, 2, 0, 3).reshape(depth, taps, -1)

    gdn_cw, ssd_cw, ffn_cw = conv_full(gcw, 4), conv_full(scw, 4), conv_full(fcw, 3)

    def layer_pieces(l):
        out = []
        for name, _ in _BIG:
            a = w[name][l].astype(BF16)
            out.append(a.reshape(2, a.shape[0] // 2, a.shape[1]))
        return out

    def full_weights(bufs):
        wl = {name: _unstack_gathered(g, 1, axis)[0] for (name, axis), g in zip(_BIG, bufs)}
        wl["w_g"], wl["w_a"], wl["w_b"], wl["w_c"] = _split_w_in(wl.pop("w_in"))
        return wl

    wls = [full_weights(allgather8("ag_weights0", layer_pieces(0), [True] * len(_BIG)))]

    (c_act,) = elementwise("silu_c", lambda v: (_silu(v),), [c_all], [F32])
    mod_cols = jnp.concatenate([matmul("ada_fwd%d" % l, c_act, w_ada[l], "nn") for l in range(depth)], axis=0)
    (mod8,) = allgather8("ag_mod", [mod_cols], [False])
    mod = mod8[::2].reshape(N_CHIPS, depth, N_DEV * nb, -1).transpose(1, 2, 0, 3).reshape(depth, N_DEV * nb, 6 * d)
    mod = lax.dynamic_slice_in_dim(mod, dev * nb, nb, axis=1) + b_ada[:, None, :]

    def mod_part(l, k):
        return mod[l, :, k * d:(k + 1) * d].reshape(nb, 1, d)

    saved = []
    xl = x0
    for l in range(depth):
        sfx = str(l)
        wl = wls[l]
        sv = {"x0": xl}
        shift1, scale1, gate1, shift2, scale2, gate2 = [mod_part(l, k) for k in range(6)]
        sv["mods"] = (shift1, scale1, gate1, shift2, scale2, gate2)
        h, h_t = normmod_fwd("norm1_fwd" + sfx, xl, norm1_w[l][None], shift1, scale1, nb, s)
        pg = matmul("proj_g" + sfx, h, wl["w_g"], "nn")
        pa = matmul("proj_a" + sfx, h, wl["w_a"], "nn")
        pb = matmul("proj_b" + sfx, h, wl["w_b"], "nn")
        pc = matmul("proj_c" + sfx, h, wl["w_c"], "nn")
        gdn_p = [_rows8(list(gdn_cw[l]), 1536), _rows8([gdn_a_log[l], gdn_dt_bias[l], gdn_norm_w[l]], 128)]
        hgrn_p = [_rows8(list(hgrn_lb_param), 512), _rows8([hgrn_norm_w[l]], 128)]
        ssd_p = [_rows8(list(ssd_cw[l]), 1024), _rows8([ssd_conv_b[l], ssd_norm_w[l]], 1024),
                 _rows8([ssd_a_log[l], ssd_dt_bias[l], ssd_d[l]], 128)]
        ffn_p = [_rows8(list(ffn_cw[l]) + [ffn_conv_b[l]], 2 * FFN_HIDDEN)]
        hgrn_fn = make_hgrn_tile(l, depth)
        if l + 1 < depth:
            oa, st_a, *bufs = seq_fwd("gdn_fwd" + sfx, gdn_tile, gdn_p, [pa], True, [(512, BF16)], (512, 128), nb, s,
                                      _R_GDN, side=_gather_side(layer_pieces(l + 1)))
            wls.append(full_weights(_gather_finish("ag_weights%d" % (l + 1), bufs)))
        else:
            oa, st_a = seq_fwd("gdn_fwd" + sfx, gdn_tile, gdn_p, [pa], True, [(512, BF16)], (512, 128), nb, s, _R_GDN)
        ob, st_b = seq_fwd("hgrn_fwd" + sfx, hgrn_fn, hgrn_p, [pb], False, [(512, BF16)], (512, 128), nb, s, _R_HGRN)
        oc, st_c = seq_fwd("ssd_fwd" + sfx, ssd_tile, ssd_p, [pc], True, [(512, BF16)], (256, 256), nb, s, _R_SSD)
        x1 = merge_fwd("merge_fwd" + sfx, xl, oa, ob, oc, pg, gate1, wl["w_br_a"], wl["w_br_b"], wl["w_br_c"],
                       wl["w_out"], nb, s)
        h2, h2_t = normmod_fwd("norm2_fwd" + sfx, x1, norm2_w[l][None], shift2, scale2, nb, s)
        u = matmul("ffn_up" + sfx, h2, wl["ffn_w_up"], "nn", tn=FFN_HIDDEN)
        act, act_t = seq_fwd("convglu_fwd" + sfx, convglu_tile_t, ffn_p, [u], True,
                             [(FFN_HIDDEN, BF16), (FFN_HIDDEN, BF16, "T")], None, nb, s, _R_FFN)
        f = matmul("ffn_down" + sfx, act, wl["ffn_w_down"], "nn")
        xl = resid_fwd("resid_fwd" + sfx, x1, f, gate2, nb, s)
        sv.update(h_t=h_t, h2_t=h2_t, act_t=act_t, pg=pg, pa=pa, pb=pb, pc=pc, oa=oa, ob=ob, oc=oc, st_a=st_a, st_b=st_b, st_c=st_c, x1=x1,
                  u=u, f=f, gdn_p=gdn_p, hgrn_p=hgrn_p, ssd_p=ssd_p, ffn_p=ffn_p, hgrn_fn=hgrn_fn)
        saved.append(sv)

    dx, loss_part, d_final = loss_head("loss_head", xl, final_norm_w[None], target)

    sg = {}
    dmod = [None] * depth
    d_lb = None
    reduced = [None] * depth
    pending = None
    for l in reversed(range(depth)):
        sfx = str(l)
        sv, wl = saved[l], wls[l]
        gfull = {}
        shift1, scale1, gate1, shift2, scale2, gate2 = sv["mods"]
        df, dgate2 = resid_bwd("resid_bwd" + sfx, dx, sv["f"], gate2, nb, s)
        dact = matmul("ffn_down_dx" + sfx, df, wl["ffn_w_down"], "nt")
        gfull["ffn_w_down"] = matmul("ffn_down_dw" + sfx, sv["act_t"], df, "nn", tm=1408, tn=512, tk=4096)
        (du,), (dcw,) = seq_bwd("convglu_bwd" + sfx, convglu_tile, sv["ffn_p"], [sv["u"]], True, None, [dact], [BF16],
                                None, nb, s, _R_FFN)
        dh2 = matmul("ffn_up_dx" + sfx, du, wl["ffn_w_up"], "nt")
        gfull["ffn_w_up"] = matmul("ffn_up_dw" + sfx, sv["h2_t"], du, "nn", tm=1024, tn=512, tk=4096)
        dx1, dnw2, dshift2, dscale2 = normmod_bwd("norm2_bwd" + sfx, sv["x1"], norm2_w[l][None], shift2, scale2, dh2, dx,
                                                  nb, s)
        doa, dob, doc, dpg, dgate1, dwa, dwb, dwc, dwo = merge_bwd(
            "merge_bwd" + sfx, sv["oa"], sv["ob"], sv["oc"], sv["pg"], gate1, wl["w_br_a"], wl["w_br_b"],
            wl["w_br_c"], wl["w_out"], dx1, nb, s)
        gfull["w_br_a"], gfull["w_br_b"], gfull["w_br_c"], gfull["w_out"] = dwa, dwb, dwc, dwo
        gdn_args = ("gdn_bwd" + sfx, gdn_tile, sv["gdn_p"], [sv["pa"]], True, sv["st_a"], [doa], [F32], (512, 128),
                    nb, s, _R_GDN)
        if pending is None:
            (dpa,), (dgcw, dgpk) = seq_bwd(*gdn_args)
        else:
            lp, qf, side = pending
            (dpa,), (dgcw, dgpk), res = seq_bwd(*gdn_args, side=side)
            reduced[lp] = _rs_finish(qf, res, str(lp))
        (dpb,), (dlbp, dhnw) = seq_bwd("hgrn_bwd" + sfx, sv["hgrn_fn"], sv["hgrn_p"], [sv["pb"]], False, sv["st_b"],
                                       [dob], [F32], (512, 128), nb, s, _R_HGRN)
        (dpc,), (dscw, dspv, dsps) = seq_bwd("ssd_bwd" + sfx, ssd_tile, sv["ssd_p"], [sv["pc"]], True, sv["st_c"],
                                             [doc], [F32], (256, 256), nb, s, _R_SSD)
        dh = matmul("proj_g_dx" + sfx, dpg, wl["w_g"], "nt")
        dh = matmul("proj_a_dx" + sfx, dpa, wl["w_a"], "nt", addend=dh)
        dh = matmul("proj_b_dx" + sfx, dpb, wl["w_b"], "nt", addend=dh)
        dh = matmul("proj_c_dx" + sfx, dpc, wl["w_c"], "nt", addend=dh)
        gfull["w_in"] = _join_w_in(
            matmul("proj_g_dw" + sfx, sv["h_t"], dpg, "nn", tm=1024, tn=512, tk=4096),
            matmul("proj_a_dw" + sfx, sv["h_t"], dpa, "nn", tm=1024, tk=1024),
            matmul("proj_b_dw" + sfx, sv["h_t"], dpb, "nn", tm=1024, tn=512, tk=4096),
            matmul("proj_c_dw" + sfx, sv["h_t"], dpc, "nn", tm=1024, tk=1024))
        dx, dnw1, dshift1, dscale1 = normmod_bwd("norm1_bwd" + sfx, sv["x0"], norm1_w[l][None], shift1, scale1, dh, dx1,
                                                 nb, s)
        dmod[l] = jnp.concatenate([dshift1, dscale1, dgate1, dshift2, dscale2, dgate2], axis=-1).reshape(nb, 6 * d)
        d_lb = dlbp[:depth] if d_lb is None else d_lb + dlbp[:depth]
        sg[l] = dict(norm1_w=dnw1[0], norm2_w=dnw2[0], gdn_conv_w=dgcw[:4], gdn_a_log=dgpk[0, :4],
                     gdn_dt_bias=dgpk[1, :4], gdn_norm_w=dgpk[2], hgrn_norm_w=dhnw[0], ssd_conv_w=dscw[:4],
                     ssd_conv_b=dspv[0], ssd_norm_w=dspv[1, :512], ssd_a_log=dsps[0, :8], ssd_dt_bias=dsps[1, :8],
                     ssd_d=dsps[2, :8], ffn_conv_w=dcw[:3], ffn_conv_b=dcw[3])
        qf, side = _rs_begin([_stack_by_chip(gfull[name][None], axis) for name, axis in _BIG], sfx)
        pending = (l, qf, side)
    lp, qf, side = pending
    reduced[lp] = _rs_finish(qf, exchange("rs_ici%d" % lp, *side), str(lp))
    grad_x = dx.reshape(nb, s, d)

    dmod = jnp.stack(dmod)
    (b_sum,) = elementwise("bias_rows", lambda *r: (functools.reduce(lambda p, q: p + q, r),),
                           [dmod[:, b].reshape(depth * 6, d) for b in range(nb)], [F32])
    per_layer = ("norm1_w", "norm2_w", "gdn_conv_w", "gdn_a_log", "gdn_dt_bias", "gdn_norm_w", "hgrn_norm_w",
                 "ssd_conv_w", "ssd_conv_b", "ssd_norm_w", "ssd_a_log", "ssd_dt_bias", "ssd_d", "ffn_conv_w", "ffn_conv_b")
    vals = {k: jnp.stack([sg[l][k] for l in range(depth)]) for k in per_layer}
    vals.update(loss=loss_part[0, :1], b_ada=b_sum.reshape(depth, 6 * d), hgrn_lb_param=d_lb, final_norm_w=d_final[0])
    gp = _Packer()
    for k, v in vals.items():
        gp.add(k, v.shape)
    packed8, dmod8 = allgather8("ag_grads", [gp.pack(vals), dmod.reshape(depth * nb, 6 * d)], [False, False])
    gs = gp.unpack(sum8("sum_small", packed8))
    loss = gs["loss"].reshape(())

    def my_cols(g):
        cs = g.shape[-1] // N_CHIPS
        return lax.dynamic_slice_in_dim(g, chip * cs, cs, axis=g.ndim - 1)

    for k in ("gdn_conv_w", "ssd_conv_w", "ffn_conv_w"):
        gs[k] = my_cols(gs[k])

    dmod_all = dmod8.reshape(N_DEV, depth, nb, 6 * d).transpose(1, 0, 2, 3).reshape(depth, N_DEV * nb, 6 * d)
    dmod_mine = lax.dynamic_slice_in_dim(dmod_all, chip * (6 * d // N_CHIPS), 6 * d // N_CHIPS, axis=2)
    g_w_ada = jnp.stack([matmul("ada_dw%d" % l, c_act, dmod_mine[l], "tn", tm=1024) for l in range(depth)])

    c_arr = lax.axis_index("c").astype(jnp.int32).reshape(1)
    grads, delta, new_m, new_v = {}, {}, {}, {}
    for i, (name, _) in enumerate(_BIG):
        shp = w[name].shape
        halves = lambda a: a.reshape((depth, 2) + reduced[0][0][i].shape)
        res = None
        for l in reversed(range(depth)):
            res = adamw_halves("adamw_%s%d" % (name, l), halves(w[name]), halves(mom[name]), halves(var[name]), l,
                               reduced[l][0][i], reduced[l][1][i], c_arr, prev=res)
        grads[name], delta[name], new_m[name], new_v[name] = [r.reshape(shp) for r in res]
    grads["w_ada"] = g_w_ada
    for k in _SMALL:
        grads[k] = gs[k].reshape(w[k].shape)
    shp = w_ada.shape
    flat = lambda a: a.reshape(shp[0] * shp[1], shp[2])
    dl, nm, nv = adamw("adamw_w_ada", flat(w_ada), flat(g_w_ada), flat(m_w_ada), flat(v_w_ada))
    delta["w_ada"], new_m["w_ada"], new_v["w_ada"] = dl.reshape(shp), nm.reshape(shp), nv.reshape(shp)
    sp = _Packer()
    for k in _SMALL:
        sp.add(k, w[k].shape)
    dl, nm, nv = adamw("adamw_small", sp.pack(w), sp.pack(grads), sp.pack(mom), sp.pack(var))
    delta.update(sp.unpack(dl))
    new_m.update(sp.unpack(nm))
    new_v.update(sp.unpack(nv))

    return (loss, grad_x, *[grads[k] for k in _WEIGHTS], *[delta[k] for k in _WEIGHTS],
            *[new_m[k] for k in _WEIGHTS], *[new_v[k] for k in _WEIGHTS])
```

```python
import functools

import jax
import jax.numpy as jnp
from jax import lax
from jax.experimental import pallas as pl
from jax.experimental.pallas import tpu as pltpu

F32 = jnp.float32
BF16 = jnp.bfloat16
HI = lax.Precision.HIGHEST
MESH = pl.DeviceIdType.MESH

EPS = 1e-6
D_MODEL = 1024
GDN_HEADS, GDN_DK, GDN_CHUNK = 4, 128, 64
HGRN_HEADS, HGRN_DK, HGRN_CHUNK = 4, 128, 16
SSD_HEADS, SSD_P, SSD_GROUPS, SSD_STATE, SSD_CHUNK = 8, 64, 2, 128, 64
FFN_HIDDEN = 2816
N_CHIPS = 4
N_DEV = 8

ADAM_LR, ADAM_B1, ADAM_B2, ADAM_EPS, ADAM_WD, ADAM_STEP = 0.001, 0.9, 0.999, 1e-08, 0.01, 10

W_G, W_A, W_B, W_C = 3072, 2176, 2048, 1664
VMEM_LIMIT = 56 * 1024 * 1024


def _cparams(sem):
    return pltpu.CompilerParams(dimension_semantics=sem, vmem_limit_bytes=VMEM_LIMIT)


def _dg(a, b, ca, cb):
    return lax.dot_general(a.astype(BF16), b.astype(BF16), (((ca,), (cb,)), ((), ())),
                           preferred_element_type=F32)


@jax.custom_vjp
def bdot(a, b):
    return _dg(a, b, 1, 0)


bdot.defvjp(lambda a, b: (_dg(a, b, 1, 0), (a, b)),
            lambda r, g: (_dg(g, r[1], 1, 1), _dg(r[0], g, 0, 0)))


@jax.custom_vjp
def bdot_nt(a, b):
    return _dg(a, b, 1, 1)


bdot_nt.defvjp(lambda a, b: (_dg(a, b, 1, 1), (a, b)),
               lambda r, g: (_dg(g, r[1], 1, 0), _dg(g, r[0], 0, 0)))


@jax.custom_vjp
def bdot_tn(a, b):
    return _dg(a, b, 0, 0)


bdot_tn.defvjp(lambda a, b: (_dg(a, b, 0, 0), (a, b)),
               lambda r, g: (_dg(r[1], g, 1, 1), _dg(r[0], g, 1, 0)))


def _split(x, n):
    parts, rest = [], x
    for _ in range(n):
        p = rest.astype(BF16)
        parts.append(p)
        rest = rest - p.astype(F32)
    return parts


def _dgb(a, b, ca, cb):
    return lax.dot_general(a, b, (((ca,), (cb,)), ((), ())), preferred_element_type=F32)


def _dg3(a, b, ca, cb):
    (ah, al), (bh, bl) = _split(a, 2), _split(b, 2)
    return _dgb(jnp.concatenate([ah, ah, al], axis=ca), jnp.concatenate([bh, bl, bh], axis=cb), ca, cb)


@jax.custom_vjp
def hdot(a, b):
    return _dg3(a, b, 1, 0)


hdot.defvjp(lambda a, b: (_dg3(a, b, 1, 0), (a, b)),
            lambda r, g: (_dg3(g, r[1], 1, 1), _dg3(r[0], g, 0, 0)))


def _dge(e, x, ce, cx, e_first):
    eb = e.astype(BF16)
    es = jnp.concatenate([eb, eb, eb], axis=ce)
    xs = jnp.concatenate(_split(x, 3), axis=cx)
    return _dgb(es, xs, ce, cx) if e_first else _dgb(xs, es, cx, ce)


@jax.custom_vjp
def ldot(e, x):
    return _dge(e, x, 1, 0, True)


ldot.defvjp(lambda e, x: (_dge(e, x, 1, 0, True), e),
            lambda e, g: (jnp.zeros_like(e), _dge(e, g, 0, 0, True)))


@jax.custom_vjp
def rdot(x, e):
    return _dge(e, x, 0, 1, False)


rdot.defvjp(lambda x, e: (_dge(e, x, 0, 1, False), e),
            lambda e, g: (_dge(e, g, 1, 1, False), jnp.zeros_like(e)))


def _sigmoid(x):
    return 1.0 / (1.0 + jnp.exp(-x))


def _silu(x):
    return x * _sigmoid(x)


def _softplus(x):
    return jnp.maximum(x, 0.0) + jnp.log(1.0 + jnp.exp(-jnp.abs(x)))


def _rms(x, w):
    return x * lax.rsqrt(jnp.mean(x * x, axis=-1, keepdims=True) + EPS) * w


def _iota(shape, dim):
    return lax.broadcasted_iota(jnp.int32, shape, dim)


def _tri_ones(n, chunk, kind):
    i, j = _iota((n, n), 0), _iota((n, n), 1)
    same = lax.div(i, chunk) == lax.div(j, chunk)
    if kind == "incl":
        m = same & (j <= i)
    elif kind == "strict":
        m = same & (j < i)
    elif kind == "all":
        m = same
    else:
        m = same & (lax.rem(j, chunk) < (chunk // 2))
    return m


def _causal_conv(w, halo, x, width):
    r = x.shape[0]
    xin = jnp.concatenate([halo, x], axis=0)
    y = w[width - 1:width, :] * x
    for k in range(width - 1):
        off = 8 - (width - 1) + k
        y = y + w[k:k + 1, :] * xin[off:off + r, :]
    return y


def _each(fn, *lists):
    return [fn(*a) for a in zip(*lists)]


def _neumann(ms):
    n = ms[0].shape[0]
    eye = (_iota((n, n), 0) == _iota((n, n), 1)).astype(F32)
    accs = [eye - m for m in ms]
    ps = ms
    steps = 1
    while steps * 2 < n:
        ps = _each(hdot, ps, ps)
        accs = [acc + ap for acc, ap in zip(accs, _each(hdot, accs, ps))]
        steps *= 2
    return accs


@jax.custom_vjp
def tri_inverse(ms):
    return _neumann(ms)


def _tri_inverse_fwd(ms):
    ainvs = _neumann(ms)
    return ainvs, ainvs


def _tri_inverse_bwd(ainvs, gs):
    t = _each(lambda g, a: _dg3(g, a, 1, 1), gs, ainvs)
    return ([-x for x in _each(lambda a, y: _dg3(a, y, 0, 0), ainvs, t)],)


tri_inverse.defvjp(_tri_inverse_fwd, _tri_inverse_bwd)


def gdn_tile(params, state, ins, halos):
    conv_w, pk = params
    (pa,), (ha,) = ins, halos
    r = pa.shape[0]
    c, nh, dk = GDN_CHUNK, GDN_HEADS, GDN_DK
    kw = nh * dk
    qkv = _silu(_causal_conv(conv_w, ha[:, :3 * kw], pa[:, :3 * kw], 4))
    z = pa[:, 3 * kw:4 * kw]
    gsm = pa[:, 4 * kw:]
    a_log, dtb, nw = pk[0:1, :], pk[1:2, :], pk[2:3, :]
    g_all = -jnp.exp(a_log) * _softplus(gsm + dtb)
    beta_all = _sigmoid(gsm)
    incl = _tri_ones(c, c, "incl")
    strict = _tri_ones(c, c, "strict")
    lmat = incl.astype(F32)
    scale = dk ** -0.5
    nck = r // c
    inst = [(ci, h) for ci in range(nck) for h in range(nh)]

    def l2n(v):
        return v * lax.rsqrt(jnp.sum(v * v, axis=-1, keepdims=True) + EPS)

    gcs = [ldot(lmat, g_all[ci * c:(ci + 1) * c, :]) for ci in range(nck)]
    gcts = [g.T for g in gcs]
    g_col = [gcs[ci][:, h:h + 1] for ci, h in inst]
    g_row = [gcts[ci][h:h + 1, :] for ci, h in inst]
    g_last = [gcs[ci][c - 1:c, h:h + 1] for ci, h in inst]
    beta = [beta_all[ci * c:(ci + 1) * c, nh + h:nh + h + 1] for ci, h in inst]
    qh = [l2n(qkv[ci * c:(ci + 1) * c, h * dk:(h + 1) * dk]) for ci, h in inst]
    kh = [l2n(qkv[ci * c:(ci + 1) * c, kw + h * dk:kw + (h + 1) * dk]) for ci, h in inst]
    vh = [qkv[ci * c:(ci + 1) * c, 2 * kw + h * dk:2 * kw + (h + 1) * dk] for ci, h in inst]
    decay = [jnp.where(incl, jnp.exp(jnp.where(incl, gc_ - gr_, 0.0)), 0.0) for gc_, gr_ in zip(g_col, g_row)]
    kb = [k * b for k, b in zip(kh, beta)]
    qs = [q * scale for q in qh]
    kk = _each(lambda a, b, k: bdot_nt(jnp.concatenate([a, b], axis=0), k), kb, qs, kh)
    ms = [jnp.where(strict, x[:c] * d, 0.0) for x, d in zip(kk, decay)]
    attn = [x[c:] * d for x, d in zip(kk, decay)]
    ainv = tri_inverse(ms)
    eg = [jnp.exp(g) for g in g_col]
    rhs = [jnp.concatenate([v * b, k_ * e], axis=1) for v, b, k_, e in zip(vh, beta, kb, eg)]
    sol = _each(hdot, ainv, rhs)
    qg = [q * e for q, e in zip(qs, eg)]
    k_end = [k * jnp.exp(gl - g) for k, gl, g in zip(kh, g_last, g_col)]
    e_last = [jnp.exp(gl) for gl in g_last]

    st = [state[h * dk:(h + 1) * dk, :] for h in range(nh)]
    outs = [[] for _ in range(nh)]
    for ci in range(nck):
        idx = [ci * nh + h for h in range(nh)]
        ws = [bdot(jnp.concatenate([sol[i][:, dk:], qg[i]], axis=0), st[h]) for h, i in enumerate(idx)]
        v_new = [sol[i][:, :dk] - w_[:c] for i, w_ in zip(idx, ws)]
        av = [bdot(attn[i], v) for i, v in zip(idx, v_new)]
        kv = [bdot_tn(k_end[i], v) for i, v in zip(idx, v_new)]
        for h, i in enumerate(idx):
            o = ws[h][c:] + av[h]
            st[h] = st[h] * e_last[i] + kv[h]
            outs[h].append(_rms(o, nw) * _silu(z[ci * c:(ci + 1) * c, h * dk:(h + 1) * dk]))
    out = jnp.concatenate([jnp.concatenate(o, axis=0) for o in outs], axis=1)
    return jnp.concatenate(st, axis=0), [out]


def make_hgrn_tile(layer, depth):
    def hgrn_tile(params, state, ins, halos):
        lbp, nwp = params
        (pb,) = ins
        r = pb.shape[0]
        c = HGRN_CHUNK
        kw = HGRN_HEADS * HGRN_DK
        rows = [lbp[i:i + 1, :] for i in range(depth)]
        mx = functools.reduce(jnp.maximum, rows)
        ex = [jnp.exp(x - mx) for x in rows]
        den = functools.reduce(lambda a, b: a + b, ex)
        soft = [e / den for e in ex]
        lb = functools.reduce(lambda a, b: a + b, soft[:layer + 1]) - soft[0]
        nw = nwp[0:1, :]
        q = _silu(pb[:, :kw])
        fr = pb[:, kw:2 * kw]
        logf = jnp.log(lb + (1.0 - lb) * _sigmoid(fr))
        k = (1.0 - lb) * _sigmoid(-fr)
        v = pb[:, 2 * kw:3 * kw]
        gate = pb[:, 3 * kw:]
        incl = _tri_ones(r, c, "incl")
        masks = jnp.concatenate([incl.astype(F32), _tri_ones(r, c, "upto").astype(F32),
                                 _tri_ones(r, c, "all").astype(F32)], axis=0)
        sums = ldot(masks, logf)
        g_cum, g_ref, g_end = sums[:r], sums[r:2 * r], sums[2 * r:]
        qs = q * jnp.exp(g_cum - g_ref)
        ks = k * jnp.exp(g_ref - g_cum)
        qg = q * jnp.exp(g_cum)
        k_end = k * jnp.exp(g_end - g_cum)
        e_end = jnp.exp(g_end)
        sls = [slice(h * HGRN_DK, (h + 1) * HGRN_DK) for h in range(HGRN_HEADS)]
        attn = [jnp.where(incl, bdot_nt(qs[:, sl], ks[:, sl]), 0.0) for sl in sls]
        o_intra = [bdot(a, v[:, sl]) for a, sl in zip(attn, sls)]
        nsub, dk = r // c, HGRN_DK
        own_block = lax.div(_iota((r, nsub * dk), 0), c) == lax.div(_iota((r, nsub * dk), 1), dk)

        def spread(a):
            return jnp.where(own_block, jnp.concatenate([a] * nsub, axis=1), 0.0)

        kv = [bdot_tn(v[:, sl], spread(k_end[:, sl])) for sl in sls]
        s_t = [state[sl, :] for sl in sls]
        entry = [[] for _ in sls]
        for j in range(nsub):
            for lst, s_h in zip(entry, s_t):
                lst.append(s_h)
            s_t = [s_h * e_end[j * c:j * c + 1, sl] + x[:, j * dk:(j + 1) * dk] for s_h, sl, x in zip(s_t, sls, kv)]
        o_inter = [bdot_nt(spread(qg[:, sl]), jnp.concatenate(e, axis=1)) for sl, e in zip(sls, entry)]
        outs = [_rms(oa + ob, nw) * _silu(gate[:, sl]) for oa, ob, sl in zip(o_intra, o_inter, sls)]
        return jnp.concatenate(s_t, axis=0), [jnp.concatenate(outs, axis=1)]
    return hgrn_tile


def ssd_tile(params, state, ins, halos):
    conv_w, pv, ps = params
    (pc,), (hc,) = ins, halos
    r = pc.shape[0]
    c = SSD_CHUNK
    inner = SSD_HEADS * SSD_P
    gw = inner // SSD_GROUPS
    z = pc[:, :inner]
    xbc = _silu(_causal_conv(conv_w, hc[:, inner:inner + 1024], pc[:, inner:inner + 1024], 4) + pv[0:1, :])
    ssm = pc[:, inner + 1024:]
    xs = xbc[:, :inner]
    bm = xbc[:, inner:inner + SSD_GROUPS * SSD_STATE]
    cm = xbc[:, inner + SSD_GROUPS * SSD_STATE:]
    a_log, dtb, dsk = ps[0:1, :], ps[1:2, :], ps[2:3, :]
    nw = pv[1:2, :inner]
    dt = _softplus(ssm + dtb)
    da = dt * (-jnp.exp(a_log))
    expand = (lax.div(_iota((128, inner), 1), SSD_P) == _iota((128, inner), 0)).astype(F32)
    xdt = xs * rdot(dt, expand)
    d_e = rdot(jnp.concatenate([dsk] * 8, axis=0), expand)[0:1, :]
    incl = _tri_ones(c, c, "incl")
    lmat = incl.astype(F32)
    st = [state[g * SSD_STATE:(g + 1) * SSD_STATE, :] for g in range(SSD_GROUPS)]
    hpg = SSD_HEADS // SSD_GROUPS
    nck = r // c
    groups = range(SSD_GROUPS)
    cg = [(ci, g) for ci in range(nck) for g in groups]
    rows = [slice(ci * c, (ci + 1) * c) for ci in range(nck)]
    gls = [slice(g * gw, (g + 1) * gw) for g in groups]
    acs = [ldot(lmat, da[rs, :]) for rs in rows]
    acs_t = [a.T for a in acs]
    acs_e = [rdot(a, expand) for a in acs]
    last_e = [a[c - 1:c, :] for a in acs_e]
    bm_g = [bm[rows[ci], g * SSD_STATE:(g + 1) * SSD_STATE] for ci, g in cg]
    cm_g = [cm[rows[ci], g * SSD_STATE:(g + 1) * SSD_STATE] for ci, g in cg]
    cb = _each(bdot_nt, cm_g, bm_g)
    heads = [(i, ci, g * hpg + hg) for i, (ci, g) in enumerate(cg) for hg in range(hpg)]
    seg = [jnp.where(incl, jnp.exp(jnp.where(incl, acs[ci][:, hh:hh + 1] - acs_t[ci][hh:hh + 1, :], 0.0)), 0.0)
           for _, ci, hh in heads]
    yd = [bdot(cb[i] * sg, xdt[rows[ci], hh * SSD_P:(hh + 1) * SSD_P]) for (i, ci, hh), sg in zip(heads, seg)]
    y_diag = [jnp.concatenate(yd[i * hpg:(i + 1) * hpg], axis=1) for i in range(len(cg))]
    xw = [xdt[rows[ci], gls[g]] * jnp.exp(last_e[ci][:, gls[g]] - acs_e[ci][:, gls[g]]) for ci, g in cg]
    e_acs = [jnp.exp(acs_e[ci][:, gls[g]]) for ci, g in cg]
    e_last = [jnp.exp(last_e[ci][:, gls[g]]) for ci, g in cg]
    kv = _each(bdot_tn, bm_g, xw)
    ys = []
    for ci in range(nck):
        idx = [ci * SSD_GROUPS + g for g in groups]
        y_off = [bdot(cm_g[i], st[g]) * e_acs[i] for g, i in zip(groups, idx)]
        st = [st[g] * e_last[i] + kv[i] for g, i in zip(groups, idx)]
        ys.append(jnp.concatenate([y_diag[i] + yo for i, yo in zip(idx, y_off)], axis=1))
    y = jnp.concatenate(ys, axis=0) + d_e * xs
    yz = y * _silu(z)
    out = jnp.concatenate([_rms(yz[:, g * gw:(g + 1) * gw], nw[:, g * gw:(g + 1) * gw])
                           for g in range(SSD_GROUPS)], axis=1)
    return jnp.concatenate(st, axis=0), [out]


def convglu_tile(params, state, ins, halos):
    (cw,) = params
    (u,), (hu,) = ins, halos
    y = _causal_conv(cw, hu, u, 3) + cw[3:4, :]
    return None, [_silu(y[:, :FFN_HIDDEN]) * y[:, FFN_HIDDEN:]]


def convglu_tile_t(params, state, ins, halos):
    _, (act,) = convglu_tile(params, state, ins, halos)
    return None, [act, act.T]


def _halo_map(nt, r):
    return lambda b, n: (jnp.maximum((b * nt + n) * (r // 8) - 1, 0), 0)


def _exchange_copies(plan, local_plan, in_refs, out_refs, send_sems, recv_sems, local_sems):
    x, y, c = lax.axis_index("x"), lax.axis_index("y"), lax.axis_index("c")
    copies = []
    for k, fn in enumerate(plan):
        src, dst, peer = fn(in_refs, out_refs, x, y, c)
        copies.append(pltpu.make_async_remote_copy(src_ref=src, dst_ref=dst, send_sem=send_sems.at[k],
                                                   recv_sem=recv_sems.at[k], device_id=peer, device_id_type=MESH))
    for k, fn in enumerate(local_plan):
        src, dst = fn(in_refs, out_refs, x, y, c)
        copies.append(pltpu.make_async_copy(src, dst, local_sems.at[k]))
    return copies


def _exchange_sems(plan, local_plan):
    return [pltpu.SemaphoreType.DMA((max(len(plan), 1),)), pltpu.SemaphoreType.DMA((max(len(plan), 1),)),
            pltpu.SemaphoreType.DMA((max(len(local_plan), 1),))]


def _host_exchange(side, body, in_specs, o_specs, out_shape, scratch, args, grid):
    s_ins, s_shapes, plan, local_plan, then = side
    n_in, n_out, n_scr = len(in_specs), len(o_specs), len(scratch)
    k_in, k_out = len(s_ins), len(s_shapes)
    any_spec = pl.BlockSpec(memory_space=pl.ANY)

    def hosted(*refs):
        own_in, s_in = refs[:n_in], refs[n_in:n_in + k_in]
        o0 = n_in + k_in
        own_out, s_out = refs[o0:o0 + n_out], refs[o0 + n_out:o0 + n_out + k_out]
        rest = refs[o0 + n_out + k_out:]
        own_scr, sems, sems_then = rest[:n_scr], rest[n_scr:n_scr + 3], rest[n_scr + 3:]
        ids = [pl.program_id(a) for a in range(len(grid))]
        first = functools.reduce(lambda p, q: p & q, [i == 0 for i in ids])
        last = functools.reduce(lambda p, q: p & q, [i == g - 1 for i, g in zip(ids, grid)])

        @pl.when(first)
        def _():
            for cp in _exchange_copies(plan, local_plan, s_in, s_out, *sems):
                cp.start()

        body(*own_in, *own_out, *own_scr)

        @pl.when(last)
        def _():
            for cp in _exchange_copies(plan, local_plan, s_in, s_out, *sems):
                cp.wait()
            passed = _exchange_copies(then, (), s_in, s_out, *sems_then)
            for cp in passed:
                cp.start()
            for cp in passed:
                cp.wait()

    return (hosted, list(in_specs) + [any_spec] * k_in, list(o_specs) + [any_spec] * k_out,
            list(out_shape) + list(s_shapes),
            list(scratch) + _exchange_sems(plan, local_plan) + _exchange_sems(then, ()),
            list(args) + list(s_ins))


def seq_fwd(name, tile_fn, params, ins, use_halo, out_specs, state_shape, nb, s, r, side=None):
    nt = s // r
    n_p, n_i, n_o = len(params), len(ins), len(out_specs)
    has_state = state_shape is not None

    def body(*refs):
        p_refs, i_refs = refs[:n_p], refs[n_p:n_p + n_i]
        h_refs = refs[n_p + n_i:n_p + 2 * n_i] if use_halo else ()
        k = n_p + n_i + len(h_refs)
        o_refs = refs[k:k + n_o]
        n = pl.program_id(1)
        state = None
        if has_state:
            sv_ref, st_ref = refs[k + n_o], refs[k + n_o + 1]

            @pl.when(n == 0)
            def _():
                st_ref[...] = jnp.zeros(state_shape, F32)

            state = st_ref[...]
            sv_ref[0, 0] = state
        pv = [p[...] for p in p_refs]
        iv = [i[...].astype(F32) for i in i_refs]
        hv = [jnp.where(n > 0, h[...].astype(F32), 0.0) for h in h_refs]
        new_state, ov = tile_fn(pv, state, iv, hv)
        for o_ref, o in zip(o_refs, ov):
            o_ref[...] = o.astype(o_ref.dtype)
        if has_state:
            st_ref[...] = new_state

    row = lambda b, n: (b * nt + n, 0)
    in_specs = [pl.BlockSpec(p.shape, lambda b, n: (0, 0)) for p in params]
    in_specs += [pl.BlockSpec((r, a.shape[1]), row) for a in ins]
    if use_halo:
        in_specs += [pl.BlockSpec((8, a.shape[1]), _halo_map(nt, r)) for a in ins]
    col = lambda b, n: (0, b * nt + n)
    out_shape, o_specs = [], []
    for w, dt, *transposed in out_specs:
        out_shape.append(jax.ShapeDtypeStruct((w, nb * s) if transposed else (nb * s, w), dt))
        o_specs.append(pl.BlockSpec((w, r), col) if transposed else pl.BlockSpec((r, w), row))
    scratch = []
    if has_state:
        out_shape.append(jax.ShapeDtypeStruct((nb, nt) + tuple(state_shape), F32))
        o_specs.append(pl.BlockSpec((1, 1) + tuple(state_shape), lambda b, n: (b, n, 0, 0)))
        scratch.append(pltpu.VMEM(tuple(state_shape), F32))
    args = list(params) + list(ins) + (list(ins) if use_halo else [])
    if side is not None:
        body, in_specs, o_specs, out_shape, scratch, args = _host_exchange(
            side, body, in_specs, o_specs, out_shape, scratch, args, (nb, nt))
    return pl.pallas_call(body, grid=(nb, nt), in_specs=in_specs, out_specs=o_specs, out_shape=out_shape,
                          scratch_shapes=scratch, compiler_params=_cparams(("arbitrary", "arbitrary")),
                          name=name)(*args)


def seq_bwd(name, tile_fn, params, ins, use_halo, states, douts, din_dtypes, state_shape, nb, s, r, side=None):
    nt = s // r
    n_p, n_i, n_o = len(params), len(ins), len(douts)
    has_state = state_shape is not None

    def body(*refs):
        p_refs, i_refs = refs[:n_p], refs[n_p:n_p + n_i]
        h_refs = refs[n_p + n_i:n_p + 2 * n_i] if use_halo else ()
        k = n_p + n_i + len(h_refs)
        sv_ref = None
        if has_state:
            sv_ref = refs[k]
            k += 1
        do_refs = refs[k:k + n_o]
        k += n_o
        di_refs, dp_refs = refs[k:k + n_i], refs[k + n_i:k + n_i + n_p]
        k += n_i + n_p
        dst_ref = None
        if has_state:
            dst_ref = refs[k]
            k += 1
        dh_refs = refs[k:k + len(h_refs)]
        b, nn = pl.program_id(0), pl.program_id(1)
        n = nt - 1 - nn

        @pl.when((b == 0) & (nn == 0))
        def _():
            for dp in dp_refs:
                dp[...] = jnp.zeros(dp.shape, F32)

        @pl.when(nn == 0)
        def _():
            if has_state:
                dst_ref[...] = jnp.zeros(state_shape, F32)
            for dh in dh_refs:
                dh[...] = jnp.zeros(dh.shape, F32)

        pv = [p[...] for p in p_refs]
        iv = [i[...].astype(F32) for i in i_refs]
        hv = [jnp.where(n > 0, h[...].astype(F32), 0.0) for h in h_refs]
        if has_state:
            f = lambda pv_, st_, iv_, hv_: tile_fn(pv_, st_, iv_, hv_)
            _, vjp = jax.vjp(f, pv, sv_ref[0, 0], iv, hv)
            dpv, dst, div, dhv = vjp((dst_ref[...], [d[...].astype(F32) for d in do_refs]))
            dst_ref[...] = dst
        else:
            f = lambda pv_, iv_, hv_: tile_fn(pv_, None, iv_, hv_)[1]
            _, vjp = jax.vjp(f, pv, iv, hv)
            dpv, div, dhv = vjp([d[...].astype(F32) for d in do_refs])
        for j, (di_ref, d) in enumerate(zip(di_refs, div)):
            if use_halo:
                d = jnp.concatenate([d[:r - 8], d[r - 8:] + dh_refs[j][...]], axis=0)
            di_ref[...] = d.astype(di_ref.dtype)
        for dh_ref, d in zip(dh_refs, dhv):
            dh_ref[...] = d
        for dp_ref, d in zip(dp_refs, dpv):
            dp_ref[...] += d

    row = lambda b, nn: (b * nt + nt - 1 - nn, 0)
    hmap = _halo_map(nt, r)
    in_specs = [pl.BlockSpec(p.shape, lambda b, nn: (0, 0)) for p in params]
    in_specs += [pl.BlockSpec((r, a.shape[1]), row) for a in ins]
    if use_halo:
        in_specs += [pl.BlockSpec((8, a.shape[1]), lambda b, nn: hmap(b, nt - 1 - nn)) for a in ins]
    args = list(params) + list(ins) + (list(ins) if use_halo else [])
    scratch = []
    if has_state:
        in_specs.append(pl.BlockSpec((1, 1) + tuple(state_shape), lambda b, nn: (b, nt - 1 - nn, 0, 0)))
        args.append(states)
        scratch.append(pltpu.VMEM(tuple(state_shape), F32))
    in_specs += [pl.BlockSpec((r, d.shape[1]), row) for d in douts]
    args += list(douts)
    if use_halo:
        scratch += [pltpu.VMEM((8, a.shape[1]), F32) for a in ins]
    out_shape = [jax.ShapeDtypeStruct(a.shape, dt) for a, dt in zip(ins, din_dtypes)]
    out_shape += [jax.ShapeDtypeStruct(p.shape, F32) for p in params]
    o_specs = [pl.BlockSpec((r, a.shape[1]), row) for a in ins]
    o_specs += [pl.BlockSpec(p.shape, lambda b, nn: (0, 0)) for p in params]
    if side is not None:
        body, in_specs, o_specs, out_shape, scratch, args = _host_exchange(
            side, body, in_specs, o_specs, out_shape, scratch, args, (nb, nt))
    res = pl.pallas_call(body, grid=(nb, nt), in_specs=in_specs, out_specs=o_specs, out_shape=out_shape,
                         scratch_shapes=scratch, compiler_params=_cparams(("arbitrary", "arbitrary")),
                         name=name)(*args)
    if side is not None:
        return res[:n_i], res[n_i:n_i + n_p], res[n_i + n_p:]
    return res[:n_i], res[n_i:]


def matmul(name, a, b, mode, out_dtype=F32, addend=None, tm=512, tn=None, tk=None):
    if mode == "nn":
        (m, kd), (_, n) = a.shape, b.shape
    elif mode == "nt":
        (m, kd), (n, _) = a.shape, b.shape
    else:
        (kd, m), (_, n) = a.shape, b.shape
    tm, tn, tk = min(tm, m), min(tn or n, n), min(tk or kd, kd)
    nk = kd // tk
    assert m % tm == 0 and n % tn == 0 and kd % tk == 0
    dims = {"nn": ((1,), (0,)), "nt": ((1,), (1,)), "tn": ((0,), (0,))}[mode]
    has_add = addend is not None

    def body(*refs):
        a_ref, b_ref = refs[0], refs[1]
        add_ref = refs[2] if has_add else None
        o_ref = refs[2 + has_add]
        part = lax.dot_general(a_ref[...].astype(BF16), b_ref[...].astype(BF16), (dims, ((), ())),
                               preferred_element_type=F32)

        def finish(acc):
            if has_add:
                acc = acc + add_ref[...]
            o_ref[...] = acc.astype(o_ref.dtype)

        if nk == 1:
            finish(part)
        else:
            acc_ref = refs[3 + has_add]
            k = pl.program_id(2)

            @pl.when(k == 0)
            def _():
                acc_ref[...] = part

            @pl.when(k > 0)
            def _():
                acc_ref[...] += part

            @pl.when(k == nk - 1)
            def _():
                finish(acc_ref[...])

    if mode == "tn":
        a_spec = pl.BlockSpec((tk, tm), lambda j, i, k: (k, i))
    else:
        a_spec = pl.BlockSpec((tm, tk), lambda j, i, k: (i, k))
    if mode == "nt":
        b_spec = pl.BlockSpec((tn, tk), lambda j, i, k: (j, k))
    else:
        b_spec = pl.BlockSpec((tk, tn), lambda j, i, k: (k, j))
    o_spec = pl.BlockSpec((tm, tn), lambda j, i, k: (i, j))
    in_specs, args = [a_spec, b_spec], [a, b]
    if has_add:
        in_specs.append(o_spec)
        args.append(addend)
    scratch = [pltpu.VMEM((tm, tn), F32)] if nk > 1 else []
    return pl.pallas_call(body, grid=(n // tn, m // tm, nk), in_specs=in_specs, out_specs=o_spec,
                          out_shape=jax.ShapeDtypeStruct((m, n), out_dtype), scratch_shapes=scratch,
                          compiler_params=_cparams(("parallel", "parallel", "arbitrary")), name=name)(*args)


def _normmod(x, nw, shift, scale):
    return _rms(x, nw) * (1.0 + scale) + shift


def _row_specs(nb, s, tr, d):
    nt = s // tr
    row = pl.BlockSpec((tr, d), lambda b, i: (b * nt + i, 0))
    per_seq = pl.BlockSpec((1, 1, d), lambda b, i: (b, 0, 0))
    full = pl.BlockSpec((1, d), lambda b, i: (0, 0))
    return nt, row, per_seq, full


def normmod_fwd(name, x, nw, shift, scale, nb, s, tr=512):
    d, tr = x.shape[1], min(tr, s)
    nt, row, per_seq, full = _row_specs(nb, s, tr, d)

    def body(x_ref, nw_ref, sh_ref, sc_ref, h_ref, ht_ref):
        h = _normmod(x_ref[...], nw_ref[...], sh_ref[0], sc_ref[0])
        h_ref[...] = h.astype(h_ref.dtype)
        ht_ref[...] = h.T.astype(ht_ref.dtype)

    return pl.pallas_call(body, grid=(nb, nt), in_specs=[row, full, per_seq, per_seq],
                          out_specs=[row, pl.BlockSpec((d, tr), lambda b, i: (0, b * nt + i))],
                          out_shape=[jax.ShapeDtypeStruct(x.shape, BF16), jax.ShapeDtypeStruct(x.shape[::-1], BF16)],
                          compiler_params=_cparams(("parallel", "parallel")), name=name)(x, nw, shift, scale)


def normmod_bwd(name, x, nw, shift, scale, dh, dres, nb, s, tr=512):
    d, tr = x.shape[1], min(tr, s)
    nt, row, per_seq, full = _row_specs(nb, s, tr, d)

    def body(x_ref, nw_ref, sh_ref, sc_ref, dh_ref, dres_ref, dx_ref, dnw_ref, dsh_ref, dsc_ref):
        b, i = pl.program_id(0), pl.program_id(1)

        @pl.when((b == 0) & (i == 0))
        def _():
            dnw_ref[...] = jnp.zeros(dnw_ref.shape, F32)

        @pl.when(i == 0)
        def _():
            dsh_ref[...] = jnp.zeros(dsh_ref.shape, F32)
            dsc_ref[...] = jnp.zeros(dsc_ref.shape, F32)

        _, vjp = jax.vjp(_normmod, x_ref[...], nw_ref[...], sh_ref[0], sc_ref[0])
        dx, dnw, dsh, dsc = vjp(dh_ref[...])
        dx_ref[...] = dres_ref[...] + dx
        dnw_ref[...] += dnw
        dsh_ref[0] += dsh
        dsc_ref[0] += dsc

    out_shape = [jax.ShapeDtypeStruct(x.shape, F32), jax.ShapeDtypeStruct((1, d), F32),
                 jax.ShapeDtypeStruct((nb, 1, d), F32), jax.ShapeDtypeStruct((nb, 1, d), F32)]
    return pl.pallas_call(body, grid=(nb, nt), in_specs=[row, full, per_seq, per_seq, row, row],
                          out_specs=[row, full, per_seq, per_seq], out_shape=out_shape,
                          compiler_params=_cparams(("arbitrary", "arbitrary")),
                          name=name)(x, nw, shift, scale, dh, dres)


def _merge(oa, ob, oc, graw, gate1, wa, wb, wc, wo):
    d = wo.shape[0]
    g = _sigmoid(graw)
    merged = g[:, :d] * bdot(oa, wa) + g[:, d:2 * d] * bdot(ob, wb) + g[:, 2 * d:] * bdot(oc, wc)
    return gate1 * bdot(merged, wo)


def _merge_specs(nb, s, tr, d, wbr):
    nt, row, per_seq, _ = _row_specs(nb, s, tr, d)
    o_spec = pl.BlockSpec((tr, wbr), lambda b, i: (b * nt + i, 0))
    g_spec = pl.BlockSpec((tr, 3 * d), lambda b, i: (b * nt + i, 0))
    wbr_spec = pl.BlockSpec((wbr, d), lambda b, i: (0, 0))
    wo_spec = pl.BlockSpec((d, d), lambda b, i: (0, 0))
    return nt, row, per_seq, o_spec, g_spec, wbr_spec, wo_spec


def merge_fwd(name, x, oa, ob, oc, pg, gate1, wa, wb, wc, wo, nb, s, tr=512):
    d, tr = x.shape[1], min(tr, s)
    nt, row, per_seq, o_spec, g_spec, wbr_spec, wo_spec = _merge_specs(nb, s, tr, d, oa.shape[1])

    def body(x_ref, oa_ref, ob_ref, oc_ref, pg_ref, g1_ref, wa_ref, wb_ref, wc_ref, wo_ref, x1_ref):
        x1_ref[...] = x_ref[...] + _merge(oa_ref[...], ob_ref[...], oc_ref[...], pg_ref[...], g1_ref[0],
                                          wa_ref[...], wb_ref[...], wc_ref[...], wo_ref[...])

    return pl.pallas_call(body, grid=(nb, nt),
                          in_specs=[row, o_spec, o_spec, o_spec, g_spec, per_seq, wbr_spec, wbr_spec, wbr_spec, wo_spec],
                          out_specs=row, out_shape=jax.ShapeDtypeStruct(x.shape, F32),
                          compiler_params=_cparams(("parallel", "parallel")),
                          name=name)(x, oa, ob, oc, pg, gate1, wa, wb, wc, wo)


def merge_bwd(name, oa, ob, oc, pg, gate1, wa, wb, wc, wo, dx1, nb, s, tr=256):
    d, tr = dx1.shape[1], min(tr, s)
    wbr = oa.shape[1]
    nt, row, per_seq, o_spec, g_spec, wbr_spec, wo_spec = _merge_specs(nb, s, tr, d, wbr)

    def body(oa_ref, ob_ref, oc_ref, pg_ref, g1_ref, wa_ref, wb_ref, wc_ref, wo_ref, dx_ref,
             doa_ref, dob_ref, doc_ref, dpg_ref, dg1_ref, dwa_ref, dwb_ref, dwc_ref, dwo_ref):
        b, i = pl.program_id(0), pl.program_id(1)

        @pl.when((b == 0) & (i == 0))
        def _():
            for r in (dwa_ref, dwb_ref, dwc_ref, dwo_ref):
                r[...] = jnp.zeros(r.shape, F32)

        @pl.when(i == 0)
        def _():
            dg1_ref[...] = jnp.zeros(dg1_ref.shape, F32)

        args = [oa_ref[...].astype(F32), ob_ref[...].astype(F32), oc_ref[...].astype(F32), pg_ref[...], g1_ref[0],
                wa_ref[...].astype(F32), wb_ref[...].astype(F32), wc_ref[...].astype(F32), wo_ref[...].astype(F32)]
        _, vjp = jax.vjp(_merge, *args)
        doa, dob, doc, dpg, dg1, dwa, dwb, dwc, dwo = vjp(dx_ref[...])
        doa_ref[...] = doa
        dob_ref[...] = dob
        doc_ref[...] = doc
        dpg_ref[...] = dpg
        dg1_ref[0] += dg1
        dwa_ref[...] += dwa
        dwb_ref[...] += dwb
        dwc_ref[...] += dwc
        dwo_ref[...] += dwo

    t = nb * s
    out_shape = ([jax.ShapeDtypeStruct((t, wbr), F32)] * 3
                 + [jax.ShapeDtypeStruct((t, 3 * d), F32), jax.ShapeDtypeStruct((nb, 1, d), F32)]
                 + [jax.ShapeDtypeStruct((wbr, d), F32)] * 3 + [jax.ShapeDtypeStruct((d, d), F32)])
    return pl.pallas_call(body, grid=(nb, nt),
                          in_specs=[o_spec, o_spec, o_spec, g_spec, per_seq, wbr_spec, wbr_spec, wbr_spec, wo_spec, row],
                          out_specs=[o_spec, o_spec, o_spec, g_spec, per_seq, wbr_spec, wbr_spec, wbr_spec, wo_spec],
                          out_shape=out_shape, compiler_params=_cparams(("arbitrary", "arbitrary")),
                          name=name)(oa, ob, oc, pg, gate1, wa, wb, wc, wo, dx1)


def resid_fwd(name, x, f, gate, nb, s, tr=512):
    d, tr = x.shape[1], min(tr, s)
    nt, row, per_seq, _ = _row_specs(nb, s, tr, d)

    def body(x_ref, f_ref, g_ref, o_ref):
        o_ref[...] = x_ref[...] + g_ref[0] * f_ref[...]

    return pl.pallas_call(body, grid=(nb, nt), in_specs=[row, row, per_seq], out_specs=row,
                          out_shape=jax.ShapeDtypeStruct(x.shape, F32),
                          compiler_params=_cparams(("parallel", "parallel")), name=name)(x, f, gate)


def resid_bwd(name, dx, f, gate, nb, s, tr=512):
    d, tr = dx.shape[1], min(tr, s)
    nt, row, per_seq, _ = _row_specs(nb, s, tr, d)

    def body(dx_ref, f_ref, g_ref, df_ref, dg_ref):
        @pl.when(pl.program_id(1) == 0)
        def _():
            dg_ref[...] = jnp.zeros(dg_ref.shape, F32)

        df_ref[...] = (g_ref[0] * dx_ref[...]).astype(df_ref.dtype)
        dg_ref[0] += jnp.sum(dx_ref[...] * f_ref[...], axis=0, keepdims=True)

    return pl.pallas_call(body, grid=(nb, nt), in_specs=[row, row, per_seq], out_specs=[row, per_seq],
                          out_shape=[jax.ShapeDtypeStruct(dx.shape, BF16), jax.ShapeDtypeStruct((nb, 1, d), F32)],
                          compiler_params=_cparams(("arbitrary", "arbitrary")), name=name)(dx, f, gate)


def loss_head(name, x, fw, target, tr=512):
    t, d = x.shape
    row = pl.BlockSpec((tr, d), lambda i: (i, 0))
    full = pl.BlockSpec((1, d), lambda i: (0, 0))

    def loss_fn(xv, fwv, tv):
        err = _rms(xv, fwv) - tv
        return 0.5 * jnp.sum(jnp.mean(err * err, axis=-1))

    def body(x_ref, fw_ref, t_ref, dx_ref, l_ref, dfw_ref):
        @pl.when(pl.program_id(0) == 0)
        def _():
            l_ref[...] = jnp.zeros(l_ref.shape, F32)
            dfw_ref[...] = jnp.zeros(dfw_ref.shape, F32)

        val, (dx, dfw) = jax.value_and_grad(loss_fn, argnums=(0, 1))(x_ref[...], fw_ref[...], t_ref[...])
        dx_ref[...] = dx
        l_ref[...] += val
        dfw_ref[...] += dfw

    return pl.pallas_call(body, grid=(t // tr,), in_specs=[row, full, row],
                          out_specs=[row, pl.BlockSpec((1, 128), lambda i: (0, 0)), full],
                          out_shape=[jax.ShapeDtypeStruct((t, d), F32), jax.ShapeDtypeStruct((1, 128), F32),
                                     jax.ShapeDtypeStruct((1, d), F32)],
                          compiler_params=_cparams(("arbitrary",)), name=name)(x, fw, target)


def _row_tile(rows, cols, n_arrays):
    budget = 24 * 1024 * 1024 // (8 * cols * max(n_arrays, 1))
    tr = rows
    while tr > max(budget, 16) and tr % 2 == 0 and (tr // 2) % 16 == 0:
        tr //= 2
    return tr


def elementwise(name, fn, ins, out_dtypes):
    rows, cols = ins[0].shape
    tr = _row_tile(rows, cols, len(ins) + len(out_dtypes))
    spec = pl.BlockSpec((tr, cols), lambda i: (i, 0))
    n_in = len(ins)

    def body(*refs):
        outs = fn(*[r[...] for r in refs[:n_in]])
        for o_ref, o in zip(refs[n_in:], outs):
            o_ref[...] = o.astype(o_ref.dtype)

    return pl.pallas_call(body, grid=(rows // tr,), in_specs=[spec] * n_in, out_specs=[spec] * len(out_dtypes),
                          out_shape=[jax.ShapeDtypeStruct((rows, cols), dt) for dt in out_dtypes],
                          compiler_params=_cparams(("parallel",)), name=name)(*ins)


def _adamw(w, g, m, v):
    m = ADAM_B1 * m + (1.0 - ADAM_B1) * g
    v = ADAM_B2 * v + (1.0 - ADAM_B2) * (g * g)
    m_hat = m / (1.0 - ADAM_B1 ** ADAM_STEP)
    v_hat = v / (1.0 - ADAM_B2 ** ADAM_STEP)
    delta = -ADAM_LR * (m_hat / (jnp.sqrt(v_hat) + ADAM_EPS) + ADAM_WD * w)
    return delta, m, v


def adamw(name, w, g, m, v):
    return elementwise(name, _adamw, [w, g, m, v], [F32, F32, F32])


_ANY = pl.BlockSpec(memory_space=pl.ANY)


def _coords():
    return lax.axis_index("x"), lax.axis_index("y"), lax.axis_index("c")


def allgather8(name, arrays, halves):
    n = len(arrays)

    def body(*refs):
        in_refs, out_refs = refs[:n], refs[n:2 * n]
        send_sems, recv_sems, local_sems = refs[2 * n:]
        x, y, c = _coords()
        me, sibling = (x, y, c), (x, y, 1 - c)
        chips = [(1 - x, y), (x, 1 - y), (1 - x, 1 - y)]

        def blk(i, px, py, pc):
            return out_refs[i].at[4 * px + 2 * py + pc]

        def piece(i):
            return in_refs[i].at[c] if halves[i] else in_refs[i]

        def copy(i, k, block, to, src=None):
            return pltpu.make_async_remote_copy(
                src_ref=blk(i, *block) if src is None else src, dst_ref=blk(i, *block),
                send_sem=send_sems.at[7 * i + k], recv_sem=recv_sems.at[7 * i + k],
                device_id=to, device_id_type=MESH)

        mine = [pltpu.make_async_copy(piece(i), blk(i, *me), local_sems.at[i]) for i in range(n)]
        for cp in mine:
            cp.start()
        first = []
        for i in range(n):
            first.append(copy(i, 0, me, sibling, src=piece(i)))
            first += [copy(i, 1 + j, me, (*chip, c), src=piece(i)) for j, chip in enumerate(chips)]
        for cp in first:
            cp.start()
        passed = []
        for j, chip in enumerate(chips):
            for i in range(n):
                copy(i, 1 + j, (*chip, c), me).wait_recv()
                fwd = copy(i, 4 + j, (*chip, c), sibling)
                fwd.start()
                passed.append(fwd)
        for i in range(n):
            copy(i, 0, sibling, me).wait_recv()
            for j, chip in enumerate(chips):
                copy(i, 4 + j, (*chip, 1 - c), me).wait_recv()
        for cp in first + passed:
            cp.wait_send()
        for cp in mine:
            cp.wait()

    out_shape = []
    for a, hv in zip(arrays, halves):
        out_shape.append(jax.ShapeDtypeStruct((N_DEV,) + tuple(a.shape[1:] if hv else a.shape), a.dtype))
    return pl.pallas_call(
        body, in_specs=[_ANY] * n, out_specs=[_ANY] * n, out_shape=out_shape,
        scratch_shapes=[pltpu.SemaphoreType.DMA((7 * n,)), pltpu.SemaphoreType.DMA((7 * n,)),
                        pltpu.SemaphoreType.DMA((n,))],
        name=name)(*arrays)


def exchange(name, ins, out_shapes, plan, local_plan=()):
    n_in, n_out = len(ins), len(out_shapes)

    def body(*refs):
        copies = _exchange_copies(plan, local_plan, refs[:n_in], refs[n_in:n_in + n_out], *refs[n_in + n_out:])
        for cp in copies:
            cp.start()
        for cp in copies:
            cp.wait()

    return pl.pallas_call(
        body, in_specs=[_ANY] * n_in, out_specs=[_ANY] * n_out, out_shape=out_shapes,
        scratch_shapes=_exchange_sems(plan, local_plan), name=name)(*ins)


def sum_halves(name, gs, recv, c_arr):
    _, _, hr, cs = gs.shape
    tr = _row_tile(hr, cs, 4)

    def body(c_ref, g_ref, r_ref, qf_ref, qb_ref):
        q = g_ref[0, 0] + r_ref[0, 0]
        qf_ref[0] = q
        qb_ref[0] = q.astype(BF16)

    grid_spec = pltpu.PrefetchScalarGridSpec(
        num_scalar_prefetch=1, grid=(N_CHIPS, hr // tr),
        in_specs=[pl.BlockSpec((1, 1, tr, cs), lambda j, i, c_ref: (j, c_ref[0], i, 0)),
                  pl.BlockSpec((1, 1, tr, cs), lambda j, i, c_ref: (j, 0, i, 0))],
        out_specs=[pl.BlockSpec((1, tr, cs), lambda j, i, c_ref: (j, i, 0))] * 2)
    return pl.pallas_call(body, grid_spec=grid_spec,
                          out_shape=[jax.ShapeDtypeStruct((N_CHIPS, hr, cs), F32),
                                     jax.ShapeDtypeStruct((N_CHIPS, hr, cs), BF16)],
                          compiler_params=_cparams(("parallel", "parallel")), name=name)(c_arr, gs, recv)


def sum_chips(name, qf, recv, chip_arr):
    _, hr, cs = qf.shape
    tr = _row_tile(hr, cs, 4)

    def body(chip_ref, q_ref, a_ref, b_ref, c_ref, o_ref):
        o_ref[...] = q_ref[0] + a_ref[...].astype(F32) + b_ref[...].astype(F32) + c_ref[...].astype(F32)

    row = pl.BlockSpec((tr, cs), lambda i, chip_ref: (i, 0))
    grid_spec = pltpu.PrefetchScalarGridSpec(
        num_scalar_prefetch=1, grid=(hr // tr,),
        in_specs=[pl.BlockSpec((1, tr, cs), lambda i, chip_ref: (chip_ref[0], i, 0)), row, row, row],
        out_specs=row)
    return pl.pallas_call(body, grid_spec=grid_spec, out_shape=jax.ShapeDtypeStruct((hr, cs), F32),
                          compiler_params=_cparams(("parallel",)), name=name)(chip_arr, qf, *recv)


def adamw_halves(name, w, m, v, layer, g_mine, g_other, c_arr, prev=None):
    _, _, hr, cs = w.shape
    tr = _row_tile(hr, cs, 9)

    def body(c_ref, w_ref, m_ref, v_ref, gm_ref, go_ref, *rest):
        g_ref, d_ref, nm_ref, nv_ref = rest[-4:]
        g = jnp.where(pl.program_id(0) == c_ref[0], gm_ref[...], go_ref[...])
        delta, nm, nv = _adamw(w_ref[0, 0], g, m_ref[0, 0], v_ref[0, 0])
        g_ref[0, 0], d_ref[0, 0], nm_ref[0, 0], nv_ref[0, 0] = g, delta, nm, nv

    half = pl.BlockSpec((1, 1, tr, cs), lambda h, i, c_ref: (layer, h, i, 0))
    row = pl.BlockSpec((tr, cs), lambda h, i, c_ref: (i, 0))
    in_specs, args, aliases = [half, half, half, row, row], [c_arr, w, m, v, g_mine, g_other], {}
    if prev is not None:
        in_specs += [pl.BlockSpec(memory_space=pl.ANY)] * 4
        args += list(prev)
        aliases = {6 + k: k for k in range(4)}
    grid_spec = pltpu.PrefetchScalarGridSpec(num_scalar_prefetch=1, grid=(2, hr // tr),
                                             in_specs=in_specs, out_specs=[half] * 4)
    return pl.pallas_call(body, grid_spec=grid_spec, out_shape=[jax.ShapeDtypeStruct(w.shape, F32)] * 4,
                          input_output_aliases=aliases, compiler_params=_cparams(("parallel", "parallel")),
                          name=name)(*args)


def sum8(name, g):
    _, rows, cols = g.shape
    tr = _row_tile(rows, cols, 9)

    def body(*refs):
        acc = refs[0][0]
        for r in refs[1:N_DEV]:
            acc = acc + r[0]
        refs[N_DEV][...] = acc

    in_specs = [pl.BlockSpec((1, tr, cols), functools.partial(lambda k, i: (k, i, 0), k)) for k in range(N_DEV)]
    return pl.pallas_call(body, grid=(rows // tr,), in_specs=in_specs,
                          out_specs=pl.BlockSpec((tr, cols), lambda i: (i, 0)),
                          out_shape=jax.ShapeDtypeStruct((rows, cols), F32),
                          compiler_params=_cparams(("parallel",)), name=name)(*([g] * N_DEV))


_QKV, _AB, _GZ = (0, 1536), (1536, 1544), (1544, 2056)
_HG = (2056, 4104)
_SZ, _XBC, _DT = (4104, 4616), (4616, 5640), (5640, 5648)
_GATES = (5648, 8720)


def _cols(w, rng):
    return w[..., rng[0]:rng[1]]


def _split_w_in(w):
    pad = jnp.zeros(w.shape[:-1] + (120,), w.dtype)
    return (_cols(w, _GATES),
            jnp.concatenate([_cols(w, _QKV), _cols(w, _GZ), _cols(w, _AB), pad], axis=-1),
            _cols(w, _HG),
            jnp.concatenate([_cols(w, _SZ), _cols(w, _XBC), _cols(w, _DT), pad], axis=-1))


def _join_w_in(g, a, b, c):
    return jnp.concatenate([a[..., 0:1536], a[..., 2048:2056], a[..., 1536:2048], b,
                            c[..., 0:512], c[..., 512:1536], c[..., 1536:1544], g], axis=-1)


def _rows8(rows, width):
    out = [jnp.pad(r.astype(F32), (0, width - r.shape[0])) for r in rows]
    out += [jnp.zeros((width,), F32)] * (8 - len(out))
    return jnp.stack(out)


class _Packer:
    def __init__(self):
        self.items, self.size = [], 0

    def add(self, name, shape):
        n = 1
        for d in shape:
            n *= d
        self.items.append((name, tuple(shape), self.size, n))
        self.size += n

    def rows(self):
        return -(-self.size // 8192) * 8

    def pack(self, values):
        flat = [values[name].astype(F32).reshape(-1) for name, _, _, _ in self.items]
        flat.append(jnp.zeros((self.rows() * 1024 - self.size,), F32))
        return jnp.concatenate(flat).reshape(self.rows(), 1024)

    def unpack(self, buf):
        flat = buf.reshape(-1)
        return {name: flat[off:off + n].reshape(shape) for name, shape, off, n in self.items}


def _stack_by_chip(g, axis):
    l, r, c = g.shape
    if axis == 2:
        cs = c // N_CHIPS
        g = g.reshape(l, r, N_CHIPS, cs).transpose(2, 0, 1, 3).reshape(N_CHIPS, 2, l * r // 2, cs)
    else:
        rs = r // N_CHIPS
        g = g.reshape(l, N_CHIPS, rs, c).transpose(1, 0, 2, 3).reshape(N_CHIPS, 2, l * rs // 2, c)
    return g


def _unstack_gathered(w8, l, axis):
    _, hr, cs = w8.shape
    w = w8.reshape(N_CHIPS, l, 2 * hr // l, cs)
    if axis == 2:
        return w.transpose(1, 2, 0, 3).reshape(l, 2 * hr // l, N_CHIPS * cs)
    return w.transpose(1, 0, 2, 3).reshape(l, N_CHIPS * 2 * hr // l, cs)


_BIG = (("w_in", 2), ("w_br_a", 2), ("w_br_b", 2), ("w_br_c", 2), ("w_out", 1), ("ffn_w_up", 2), ("ffn_w_down", 1))
_SMALL = ("b_ada", "norm1_w", "gdn_conv_w", "gdn_a_log", "gdn_dt_bias", "gdn_norm_w", "hgrn_lb_param",
          "hgrn_norm_w", "ssd_conv_w", "ssd_conv_b", "ssd_a_log", "ssd_dt_bias", "ssd_d", "ssd_norm_w",
          "norm2_w", "ffn_conv_w", "ffn_conv_b", "final_norm_w")
_WEIGHTS = ("w_ada", "b_ada", "norm1_w", "w_in", "gdn_conv_w", "gdn_a_log", "gdn_dt_bias", "gdn_norm_w",
            "hgrn_lb_param", "hgrn_norm_w", "ssd_conv_w", "ssd_conv_b", "ssd_a_log", "ssd_dt_bias", "ssd_d",
            "ssd_norm_w", "w_br_a", "w_br_b", "w_br_c", "w_out", "norm2_w", "ffn_w_up", "ffn_conv_w",
            "ffn_conv_b", "ffn_w_down", "final_norm_w")
_R_GDN, _R_HGRN, _R_SSD, _R_FFN = 256, 128, 256, 256


_MASKS = ((1, 0), (0, 1), (1, 1))


def _flip(k, x, y):
    return (1 - x if _MASKS[k][0] else x), (1 - y if _MASKS[k][1] else y)


def _rs_d2d(grads):
    plan = [functools.partial(lambda i, ins, outs, x, y, c: (ins[i].at[:, pl.ds(1 - c, 1)], outs[i], (x, y, 1 - c)), i)
            for i in range(len(grads))]
    return grads, [jax.ShapeDtypeStruct((N_CHIPS, 1) + g.shape[2:], F32) for g in grads], plan, (), ()


def _rs_ici(grads, recv, tag):
    n = len(grads)
    c_arr = lax.axis_index("c").astype(jnp.int32).reshape(1)
    q = [sum_halves("rs_sum_d2d%s_%d" % (tag, i), g, r, c_arr) for i, (g, r) in enumerate(zip(grads, recv))]
    qf, qb = [a for a, _ in q], [b for _, b in q]

    def ici(i, k, ins, outs, x, y, c):
        px, py = _flip(k, x, y)
        return ins[i].at[2 * px + py], outs[3 * i + k], (px, py, c)

    plan = [functools.partial(ici, i, k) for i in range(n) for k in range(3)]
    shapes = [jax.ShapeDtypeStruct(g.shape[2:], BF16) for g in grads for _ in range(3)]
    return qf, (qb, shapes, plan, (), ())


def _rs_finish(qf, res, tag):
    n = len(qf)
    chip_arr = (2 * lax.axis_index("x") + lax.axis_index("y")).astype(jnp.int32).reshape(1)
    red = [sum_chips("rs_sum_ici%s_%d" % (tag, i), qf[i], res[3 * i:3 * i + 3], chip_arr) for i in range(n)]
    plan = [functools.partial(lambda i, ins, outs, x, y, c: (ins[i], outs[i], (x, y, 1 - c)), i) for i in range(n)]
    other = exchange("rs_swap" + tag, red, [jax.ShapeDtypeStruct(r.shape, F32) for r in red], plan)
    return red, other


def _gather_side(pieces):
    n = len(pieces)

    def send(i, k, ins, outs, x, y, c):
        px, py = _flip(k, x, y)
        return ins[i].at[c], outs[i].at[2 * (2 * x + y) + c], (px, py, c)

    def to_sibling(i, h, ins, outs, x, y, c):
        return ins[i].at[h], outs[i].at[2 * (2 * x + y) + h], (x, y, 1 - c)

    def pass_on(i, k, ins, outs, x, y, c):
        px, py = _flip(k, x, y)
        blk = 2 * (2 * px + py) + c
        return outs[i].at[blk], outs[i].at[blk], (x, y, 1 - c)

    plan = [functools.partial(send, i, k) for i in range(n) for k in range(3)]
    plan += [functools.partial(to_sibling, i, h) for i in range(n) for h in range(2)]
    then = [functools.partial(pass_on, i, k) for i in range(n) for k in range(3)]
    shapes = [jax.ShapeDtypeStruct((N_DEV,) + p.shape[1:], p.dtype) for p in pieces]
    return pieces, shapes, plan, (), then


def kernel(x, c, w_ada, b_ada, norm1_w, w_in, gdn_conv_w, gdn_a_log, gdn_dt_bias, gdn_norm_w, hgrn_lb_param, hgrn_norm_w, ssd_conv_w, ssd_conv_b, ssd_a_log, ssd_dt_bias, ssd_d, ssd_norm_w, w_br_a, w_br_b, w_br_c, w_out, norm2_w, ffn_w_up, ffn_conv_w, ffn_conv_b, ffn_w_down, final_norm_w, loss_target, m_w_ada, m_b_ada, m_norm1_w, m_w_in, m_gdn_conv_w, m_gdn_a_log, m_gdn_dt_bias, m_gdn_norm_w, m_hgrn_lb_param, m_hgrn_norm_w, m_ssd_conv_w, m_ssd_conv_b, m_ssd_a_log, m_ssd_dt_bias, m_ssd_d, m_ssd_norm_w, m_w_br_a, m_w_br_b, m_w_br_c, m_w_out, m_norm2_w, m_ffn_w_up, m_ffn_conv_w, m_ffn_conv_b, m_ffn_w_down, m_final_norm_w, v_w_ada, v_b_ada, v_norm1_w, v_w_in, v_gdn_conv_w, v_gdn_a_log, v_gdn_dt_bias, v_gdn_norm_w, v_hgrn_lb_param, v_hgrn_norm_w, v_ssd_conv_w, v_ssd_conv_b, v_ssd_a_log, v_ssd_dt_bias, v_ssd_d, v_ssd_norm_w, v_w_br_a, v_w_br_b, v_w_br_c, v_w_out, v_norm2_w, v_ffn_w_up, v_ffn_conv_w, v_ffn_conv_b, v_ffn_w_down, v_final_norm_w):
    loc = dict(locals())
    w = {k: loc[k] for k in _WEIGHTS}
    mom = {k: loc["m_" + k] for k in _WEIGHTS}
    var = {k: loc["v_" + k] for k in _WEIGHTS}
    nb, s, d = x.shape
    t = nb * s
    depth = w_ada.shape[0]
    chip = 2 * lax.axis_index("x") + lax.axis_index("y")
    dev = 2 * chip + lax.axis_index("c")
    x0 = x.reshape(t, d)
    target = loss_target.reshape(t, d)

    small_in = [c, gdn_conv_w.reshape(depth * 4, -1), ssd_conv_w.reshape(depth * 4, -1),
                ffn_conv_w.reshape(depth * 3, -1)]
    c_all, gcw, scw, fcw = allgather8("ag_small", small_in, [False] * 4)
    c_all = c_all.reshape(N_DEV * nb, d)

    def conv_full(g, taps):
        g = g[::2].reshape(N_CHIPS, depth, taps, -1)
        return g.transpose(1, 2, 0, 3).reshape(depth, taps, -1)

    gdn_cw, ssd_cw, ffn_cw = conv_full(gcw, 4), conv_full(scw, 4), conv_full(fcw, 3)

    axis_of = dict(_BIG)
    first_needed, later = ("w_in",), tuple(n for n, _ in _BIG if n != "w_in")
    wls = [dict() for _ in range(depth)]

    def pieces(keys):
        out = []
        for l, name in keys:
            a = w[name][l].astype(BF16)
            out.append(a.reshape(2, a.shape[0] // 2, a.shape[1]))
        return out

    def arrived(keys, bufs):
        for (l, name), g in zip(keys, bufs):
            full = _unstack_gathered(g, 1, axis_of[name])[0]
            if name == "w_in":
                wls[l]["w_g"], wls[l]["w_a"], wls[l]["w_b"], wls[l]["w_c"] = _split_w_in(full)
            else:
                wls[l][name] = full

    keys0 = [(0, n) for n in first_needed]
    arrived(keys0, allgather8("ag_weights0", pieces(keys0), [True] * len(keys0)))

    (c_act,) = elementwise("silu_c", lambda v: (_silu(v),), [c_all], [F32])
    mod_cols = jnp.concatenate([matmul("ada_fwd%d" % l, c_act, w_ada[l], "nn") for l in range(depth)], axis=0)
    (mod8,) = allgather8("ag_mod", [mod_cols], [False])
    mod = mod8[::2].reshape(N_CHIPS, depth, N_DEV * nb, -1).transpose(1, 2, 0, 3).reshape(depth, N_DEV * nb, 6 * d)
    mod = lax.dynamic_slice_in_dim(mod, dev * nb, nb, axis=1) + b_ada[:, None, :]

    def mod_part(l, k):
        return mod[l, :, k * d:(k + 1) * d].reshape(nb, 1, d)

    saved = []
    xl = x0
    for l in range(depth):
        sfx = str(l)
        wl = wls[l]
        sv = {"x0": xl}
        shift1, scale1, gate1, shift2, scale2, gate2 = [mod_part(l, k) for k in range(6)]
        sv["mods"] = (shift1, scale1, gate1, shift2, scale2, gate2)
        h, h_t = normmod_fwd("norm1_fwd" + sfx, xl, norm1_w[l][None], shift1, scale1, nb, s)
        pg = matmul("proj_g" + sfx, h, wl["w_g"], "nn")
        pa = matmul("proj_a" + sfx, h, wl["w_a"], "nn")
        pb = matmul("proj_b" + sfx, h, wl["w_b"], "nn")
        pc = matmul("proj_c" + sfx, h, wl["w_c"], "nn")
        gdn_p = [_rows8(list(gdn_cw[l]), 1536), _rows8([gdn_a_log[l], gdn_dt_bias[l], gdn_norm_w[l]], 128)]
        hgrn_p = [_rows8(list(hgrn_lb_param), 512), _rows8([hgrn_norm_w[l]], 128)]
        ssd_p = [_rows8(list(ssd_cw[l]), 1024), _rows8([ssd_conv_b[l], ssd_norm_w[l]], 1024),
                 _rows8([ssd_a_log[l], ssd_dt_bias[l], ssd_d[l]], 128)]
        ffn_p = [_rows8(list(ffn_cw[l]) + [ffn_conv_b[l]], 2 * FFN_HIDDEN)]
        hgrn_fn = make_hgrn_tile(l, depth)
        keys = [(l, n) for n in later] + ([(l + 1, n) for n in first_needed] if l + 1 < depth else [])
        oa, st_a, *bufs = seq_fwd("gdn_fwd" + sfx, gdn_tile, gdn_p, [pa], True, [(512, BF16)], (512, 128), nb, s,
                                  _R_GDN, side=_gather_side(pieces(keys)))
        arrived(keys, bufs)
        ob, st_b = seq_fwd("hgrn_fwd" + sfx, hgrn_fn, hgrn_p, [pb], False, [(512, BF16)], (512, 128), nb, s, _R_HGRN)
        oc, st_c = seq_fwd("ssd_fwd" + sfx, ssd_tile, ssd_p, [pc], True, [(512, BF16)], (256, 256), nb, s, _R_SSD)
        x1 = merge_fwd("merge_fwd" + sfx, xl, oa, ob, oc, pg, gate1, wl["w_br_a"], wl["w_br_b"], wl["w_br_c"],
                       wl["w_out"], nb, s)
        h2, h2_t = normmod_fwd("norm2_fwd" + sfx, x1, norm2_w[l][None], shift2, scale2, nb, s)
        u = matmul("ffn_up" + sfx, h2, wl["ffn_w_up"], "nn", tn=FFN_HIDDEN)
        act, act_t = seq_fwd("convglu_fwd" + sfx, convglu_tile_t, ffn_p, [u], True,
                             [(FFN_HIDDEN, BF16), (FFN_HIDDEN, BF16, "T")], None, nb, s, _R_FFN)
        f = matmul("ffn_down" + sfx, act, wl["ffn_w_down"], "nn")
        xl = resid_fwd("resid_fwd" + sfx, x1, f, gate2, nb, s)
        sv.update(h_t=h_t, h2_t=h2_t, act_t=act_t, pg=pg, pa=pa, pb=pb, pc=pc, oa=oa, ob=ob, oc=oc, st_a=st_a, st_b=st_b, st_c=st_c, x1=x1,
                  u=u, f=f, gdn_p=gdn_p, hgrn_p=hgrn_p, ssd_p=ssd_p, ffn_p=ffn_p, hgrn_fn=hgrn_fn)
        saved.append(sv)

    dx, loss_part, d_final = loss_head("loss_head", xl, final_norm_w[None], target)

    sg = {}
    dmod = [None] * depth
    d_lb = None
    reduced = [None] * depth
    to_d2d = to_ici = None
    for l in reversed(range(depth)):
        sfx = str(l)
        sv, wl = saved[l], wls[l]
        gfull = {}
        shift1, scale1, gate1, shift2, scale2, gate2 = sv["mods"]
        df, dgate2 = resid_bwd("resid_bwd" + sfx, dx, sv["f"], gate2, nb, s)
        dact = matmul("ffn_down_dx" + sfx, df, wl["ffn_w_down"], "nt")
        gfull["ffn_w_down"] = matmul("ffn_down_dw" + sfx, sv["act_t"], df, "nn", tm=1408, tn=512, tk=4096)
        cg_args = ("convglu_bwd" + sfx, convglu_tile, sv["ffn_p"], [sv["u"]], True, None, [dact], [BF16], None, nb, s,
                   _R_FFN)
        if to_d2d is None:
            (du,), (dcw,) = seq_bwd(*cg_args)
        else:
            lp, stacked = to_d2d
            (du,), (dcw,), recv = seq_bwd(*cg_args, side=_rs_d2d(stacked))
            to_ici = (lp,) + _rs_ici(stacked, recv, str(lp))
        dh2 = matmul("ffn_up_dx" + sfx, du, wl["ffn_w_up"], "nt")
        gfull["ffn_w_up"] = matmul("ffn_up_dw" + sfx, sv["h2_t"], du, "nn", tm=1024, tn=512, tk=4096)
        dx1, dnw2, dshift2, dscale2 = normmod_bwd("norm2_bwd" + sfx, sv["x1"], norm2_w[l][None], shift2, scale2, dh2, dx,
                                                  nb, s)
        doa, dob, doc, dpg, dgate1, dwa, dwb, dwc, dwo = merge_bwd(
            "merge_bwd" + sfx, sv["oa"], sv["ob"], sv["oc"], sv["pg"], gate1, wl["w_br_a"], wl["w_br_b"],
            wl["w_br_c"], wl["w_out"], dx1, nb, s)
        gfull["w_br_a"], gfull["w_br_b"], gfull["w_br_c"], gfull["w_out"] = dwa, dwb, dwc, dwo
        gdn_args = ("gdn_bwd" + sfx, gdn_tile, sv["gdn_p"], [sv["pa"]], True, sv["st_a"], [doa], [F32], (512, 128),
                    nb, s, _R_GDN)
        if to_ici is None:
            (dpa,), (dgcw, dgpk) = seq_bwd(*gdn_args)
        else:
            lp, qf, side = to_ici
            (dpa,), (dgcw, dgpk), res = seq_bwd(*gdn_args, side=side)
            reduced[lp] = _rs_finish(qf, res, str(lp))
        (dpb,), (dlbp, dhnw) = seq_bwd("hgrn_bwd" + sfx, sv["hgrn_fn"], sv["hgrn_p"], [sv["pb"]], False, sv["st_b"],
                                       [dob], [F32], (512, 128), nb, s, _R_HGRN)
        (dpc,), (dscw, dspv, dsps) = seq_bwd("ssd_bwd" + sfx, ssd_tile, sv["ssd_p"], [sv["pc"]], True, sv["st_c"],
                                             [doc], [F32], (256, 256), nb, s, _R_SSD)
        dh = matmul("proj_g_dx" + sfx, dpg, wl["w_g"], "nt")
        dh = matmul("proj_a_dx" + sfx, dpa, wl["w_a"], "nt", addend=dh)
        dh = matmul("proj_b_dx" + sfx, dpb, wl["w_b"], "nt", addend=dh)
        dh = matmul("proj_c_dx" + sfx, dpc, wl["w_c"], "nt", addend=dh)
        gfull["w_in"] = _join_w_in(
            matmul("proj_g_dw" + sfx, sv["h_t"], dpg, "nn", tm=1024, tn=512, tk=4096),
            matmul("proj_a_dw" + sfx, sv["h_t"], dpa, "nn", tm=1024, tk=1024),
            matmul("proj_b_dw" + sfx, sv["h_t"], dpb, "nn", tm=1024, tn=512, tk=4096),
            matmul("proj_c_dw" + sfx, sv["h_t"], dpc, "nn", tm=1024, tk=1024))
        dx, dnw1, dshift1, dscale1 = normmod_bwd("norm1_bwd" + sfx, sv["x0"], norm1_w[l][None], shift1, scale1, dh, dx1,
                                                 nb, s)
        dmod[l] = jnp.concatenate([dshift1, dscale1, dgate1, dshift2, dscale2, dgate2], axis=-1).reshape(nb, 6 * d)
        d_lb = dlbp[:depth] if d_lb is None else d_lb + dlbp[:depth]
        sg[l] = dict(norm1_w=dnw1[0], norm2_w=dnw2[0], gdn_conv_w=dgcw[:4], gdn_a_log=dgpk[0, :4],
                     gdn_dt_bias=dgpk[1, :4], gdn_norm_w=dgpk[2], hgrn_norm_w=dhnw[0], ssd_conv_w=dscw[:4],
                     ssd_conv_b=dspv[0], ssd_norm_w=dspv[1, :512], ssd_a_log=dsps[0, :8], ssd_dt_bias=dsps[1, :8],
                     ssd_d=dsps[2, :8], ffn_conv_w=dcw[:3], ffn_conv_b=dcw[3])
        to_d2d = (l, [_stack_by_chip(gfull[name][None], axis) for name, axis in _BIG])
    lp, stacked = to_d2d
    recv = exchange("rs_d2d%d" % lp, *_rs_d2d(stacked)[:4])
    qf, side = _rs_ici(stacked, recv, str(lp))
    reduced[lp] = _rs_finish(qf, exchange("rs_ici%d" % lp, *side[:4]), str(lp))
    grad_x = dx.reshape(nb, s, d)

    dmod = jnp.stack(dmod)
    (b_sum,) = elementwise("bias_rows", lambda *r: (functools.reduce(lambda p, q: p + q, r),),
                           [dmod[:, b].reshape(depth * 6, d) for b in range(nb)], [F32])
    per_layer = ("norm1_w", "norm2_w", "gdn_conv_w", "gdn_a_log", "gdn_dt_bias", "gdn_norm_w", "hgrn_norm_w",
                 "ssd_conv_w", "ssd_conv_b", "ssd_norm_w", "ssd_a_log", "ssd_dt_bias", "ssd_d", "ffn_conv_w", "ffn_conv_b")
    vals = {k: jnp.stack([sg[l][k] for l in range(depth)]) for k in per_layer}
    vals.update(loss=loss_part[0, :1], b_ada=b_sum.reshape(depth, 6 * d), hgrn_lb_param=d_lb, final_norm_w=d_final[0])
    gp = _Packer()
    for k, v in vals.items():
        gp.add(k, v.shape)
    packed8, dmod8 = allgather8("ag_grads", [gp.pack(vals), dmod.reshape(depth * nb, 6 * d)], [False, False])
    gs = gp.unpack(sum8("sum_small", packed8))
    loss = gs["loss"].reshape(())

    def my_cols(g):
        cs = g.shape[-1] // N_CHIPS
        return lax.dynamic_slice_in_dim(g, chip * cs, cs, axis=g.ndim - 1)

    for k in ("gdn_conv_w", "ssd_conv_w", "ffn_conv_w"):
        gs[k] = my_cols(gs[k])

    dmod_all = dmod8.reshape(N_DEV, depth, nb, 6 * d).transpose(1, 0, 2, 3).reshape(depth, N_DEV * nb, 6 * d)
    dmod_mine = lax.dynamic_slice_in_dim(dmod_all, chip * (6 * d // N_CHIPS), 6 * d // N_CHIPS, axis=2)
    g_w_ada = jnp.stack([matmul("ada_dw%d" % l, c_act, dmod_mine[l], "tn", tm=1024) for l in range(depth)])

    c_arr = lax.axis_index("c").astype(jnp.int32).reshape(1)
    grads, delta, new_m, new_v = {}, {}, {}, {}
    for i, (name, _) in enumerate(_BIG):
        shp = w[name].shape
        halves = lambda a: a.reshape((depth, 2) + reduced[0][0][i].shape)
        res = None
        for l in reversed(range(depth)):
            res = adamw_halves("adamw_%s%d" % (name, l), halves(w[name]), halves(mom[name]), halves(var[name]), l,
                               reduced[l][0][i], reduced[l][1][i], c_arr, prev=res)
        grads[name], delta[name], new_m[name], new_v[name] = [r.reshape(shp) for r in res]
    grads["w_ada"] = g_w_ada
    for k in _SMALL:
        grads[k] = gs[k].reshape(w[k].shape)
    shp = w_ada.shape
    flat = lambda a: a.reshape(shp[0] * shp[1], shp[2])
    dl, nm, nv = adamw("adamw_w_ada", flat(w_ada), flat(g_w_ada), flat(m_w_ada), flat(v_w_ada))
    delta["w_ada"], new_m["w_ada"], new_v["w_ada"] = dl.reshape(shp), nm.reshape(shp), nv.reshape(shp)
    sp = _Packer()
    for k in _SMALL:
        sp.add(k, w[k].shape)
    dl, nm, nv = adamw("adamw_small", sp.pack(w), sp.pack(grads), sp.pack(mom), sp.pack(var))
    delta.update(sp.unpack(dl))
    new_m.update(sp.unpack(nm))
    new_v.update(sp.unpack(nv))

    return (loss, grad_x, *[grads[k] for k in _WEIGHTS], *[delta[k] for k in _WEIGHTS],
            *[new_m[k] for k in _WEIGHTS], *[new_v[k] for k in _WEIGHTS])
```

```python
import functools

import jax
import jax.numpy as jnp
from jax import lax
from jax.experimental import pallas as pl
from jax.experimental.pallas import tpu as pltpu

F32 = jnp.float32
BF16 = jnp.bfloat16
HI = lax.Precision.HIGHEST
MESH = pl.DeviceIdType.MESH

EPS = 1e-6
D_MODEL = 1024
GDN_HEADS, GDN_DK, GDN_CHUNK = 4, 128, 64
HGRN_HEADS, HGRN_DK, HGRN_CHUNK = 4, 128, 16
SSD_HEADS, SSD_P, SSD_GROUPS, SSD_STATE, SSD_CHUNK = 8, 64, 2, 128, 64
FFN_HIDDEN = 2816
N_CHIPS = 4
N_DEV = 8

ADAM_LR, ADAM_B1, ADAM_B2, ADAM_EPS, ADAM_WD, ADAM_STEP = 0.001, 0.9, 0.999, 1e-08, 0.01, 10

W_G, W_A, W_B, W_C = 3072, 2176, 2048, 1664
VMEM_LIMIT = 56 * 1024 * 1024


def _cparams(sem):
    return pltpu.CompilerParams(dimension_semantics=sem, vmem_limit_bytes=VMEM_LIMIT)


def _dg(a, b, ca, cb):
    return lax.dot_general(a.astype(BF16), b.astype(BF16), (((ca,), (cb,)), ((), ())),
                           preferred_element_type=F32)


@jax.custom_vjp
def bdot(a, b):
    return _dg(a, b, 1, 0)


bdot.defvjp(lambda a, b: (_dg(a, b, 1, 0), (a, b)),
            lambda r, g: (_dg(g, r[1], 1, 1), _dg(r[0], g, 0, 0)))


@jax.custom_vjp
def bdot_nt(a, b):
    return _dg(a, b, 1, 1)


bdot_nt.defvjp(lambda a, b: (_dg(a, b, 1, 1), (a, b)),
               lambda r, g: (_dg(g, r[1], 1, 0), _dg(g, r[0], 0, 0)))


@jax.custom_vjp
def bdot_tn(a, b):
    return _dg(a, b, 0, 0)


bdot_tn.defvjp(lambda a, b: (_dg(a, b, 0, 0), (a, b)),
               lambda r, g: (_dg(r[1], g, 1, 1), _dg(r[0], g, 1, 0)))


def _split(x, n):
    parts, rest = [], x
    for _ in range(n):
        p = rest.astype(BF16)
        parts.append(p)
        rest = rest - p.astype(F32)
    return parts


def _dgb(a, b, ca, cb):
    return lax.dot_general(a, b, (((ca,), (cb,)), ((), ())), preferred_element_type=F32)


def _dg3(a, b, ca, cb):
    (ah, al), (bh, bl) = _split(a, 2), _split(b, 2)
    return _dgb(jnp.concatenate([ah, ah, al], axis=ca), jnp.concatenate([bh, bl, bh], axis=cb), ca, cb)


@jax.custom_vjp
def hdot(a, b):
    return _dg3(a, b, 1, 0)


hdot.defvjp(lambda a, b: (_dg3(a, b, 1, 0), (a, b)),
            lambda r, g: (_dg3(g, r[1], 1, 1), _dg3(r[0], g, 0, 0)))


def _dge(e, x, ce, cx, e_first):
    eb = e.astype(BF16)
    es = jnp.concatenate([eb, eb, eb], axis=ce)
    xs = jnp.concatenate(_split(x, 3), axis=cx)
    return _dgb(es, xs, ce, cx) if e_first else _dgb(xs, es, cx, ce)


@jax.custom_vjp
def ldot(e, x):
    return _dge(e, x, 1, 0, True)


ldot.defvjp(lambda e, x: (_dge(e, x, 1, 0, True), e),
            lambda e, g: (jnp.zeros_like(e), _dge(e, g, 0, 0, True)))


@jax.custom_vjp
def rdot(x, e):
    return _dge(e, x, 0, 1, False)


rdot.defvjp(lambda x, e: (_dge(e, x, 0, 1, False), e),
            lambda e, g: (_dge(e, g, 1, 1, False), jnp.zeros_like(e)))


def _sigmoid(x):
    return 1.0 / (1.0 + jnp.exp(-x))


def _silu(x):
    return x * _sigmoid(x)


def _softplus(x):
    return jnp.maximum(x, 0.0) + jnp.log(1.0 + jnp.exp(-jnp.abs(x)))


def _rms(x, w):
    return x * lax.rsqrt(jnp.mean(x * x, axis=-1, keepdims=True) + EPS) * w


def _iota(shape, dim):
    return lax.broadcasted_iota(jnp.int32, shape, dim)


def _tri_ones(n, chunk, kind):
    i, j = _iota((n, n), 0), _iota((n, n), 1)
    same = lax.div(i, chunk) == lax.div(j, chunk)
    if kind == "incl":
        m = same & (j <= i)
    elif kind == "strict":
        m = same & (j < i)
    elif kind == "all":
        m = same
    else:
        m = same & (lax.rem(j, chunk) < (chunk // 2))
    return m


def _causal_conv(w, halo, x, width):
    r = x.shape[0]
    xin = jnp.concatenate([halo, x], axis=0)
    y = w[width - 1:width, :] * x
    for k in range(width - 1):
        off = 8 - (width - 1) + k
        y = y + w[k:k + 1, :] * xin[off:off + r, :]
    return y


def _each(fn, *lists):
    return [fn(*a) for a in zip(*lists)]


def _neumann(ms):
    n = ms[0].shape[0]
    eye = (_iota((n, n), 0) == _iota((n, n), 1)).astype(F32)
    accs = [eye - m for m in ms]
    ps = ms
    steps = 1
    while steps * 2 < n:
        ps = _each(hdot, ps, ps)
        accs = [acc + ap for acc, ap in zip(accs, _each(hdot, accs, ps))]
        steps *= 2
    return accs


@jax.custom_vjp
def tri_inverse(ms):
    return _neumann(ms)


def _tri_inverse_fwd(ms):
    ainvs = _neumann(ms)
    return ainvs, ainvs


def _tri_inverse_bwd(ainvs, gs):
    t = _each(lambda g, a: _dg3(g, a, 1, 1), gs, ainvs)
    return ([-x for x in _each(lambda a, y: _dg3(a, y, 0, 0), ainvs, t)],)


tri_inverse.defvjp(_tri_inverse_fwd, _tri_inverse_bwd)


def gdn_tile(params, state, ins, halos):
    conv_w, pk = params
    (pa,), (ha,) = ins, halos
    r = pa.shape[0]
    c, nh, dk = GDN_CHUNK, GDN_HEADS, GDN_DK
    kw = nh * dk
    qkv = _silu(_causal_conv(conv_w, ha[:, :3 * kw], pa[:, :3 * kw], 4))
    z = pa[:, 3 * kw:4 * kw]
    gsm = pa[:, 4 * kw:]
    a_log, dtb, nw = pk[0:1, :], pk[1:2, :], pk[2:3, :]
    g_all = -jnp.exp(a_log) * _softplus(gsm + dtb)
    beta_all = _sigmoid(gsm)
    incl = _tri_ones(c, c, "incl")
    strict = _tri_ones(c, c, "strict")
    lmat = incl.astype(F32)
    scale = dk ** -0.5
    nck = r // c
    inst = [(ci, h) for ci in range(nck) for h in range(nh)]

    def l2n(v):
        return v * lax.rsqrt(jnp.sum(v * v, axis=-1, keepdims=True) + EPS)

    gcs = [ldot(lmat, g_all[ci * c:(ci + 1) * c, :]) for ci in range(nck)]
    gcts = [g.T for g in gcs]
    g_col = [gcs[ci][:, h:h + 1] for ci, h in inst]
    g_row = [gcts[ci][h:h + 1, :] for ci, h in inst]
    g_last = [gcs[ci][c - 1:c, h:h + 1] for ci, h in inst]
    beta = [beta_all[ci * c:(ci + 1) * c, nh + h:nh + h + 1] for ci, h in inst]
    qh = [l2n(qkv[ci * c:(ci + 1) * c, h * dk:(h + 1) * dk]) for ci, h in inst]
    kh = [l2n(qkv[ci * c:(ci + 1) * c, kw + h * dk:kw + (h + 1) * dk]) for ci, h in inst]
    vh = [qkv[ci * c:(ci + 1) * c, 2 * kw + h * dk:2 * kw + (h + 1) * dk] for ci, h in inst]
    decay = [jnp.where(incl, jnp.exp(jnp.where(incl, gc_ - gr_, 0.0)), 0.0) for gc_, gr_ in zip(g_col, g_row)]
    kb = [k * b for k, b in zip(kh, beta)]
    qs = [q * scale for q in qh]
    kk = _each(lambda a, b, k: bdot_nt(jnp.concatenate([a, b], axis=0), k), kb, qs, kh)
    ms = [jnp.where(strict, x[:c] * d, 0.0) for x, d in zip(kk, decay)]
    attn = [x[c:] * d for x, d in zip(kk, decay)]
    ainv = tri_inverse(ms)
    eg = [jnp.exp(g) for g in g_col]
    rhs = [jnp.concatenate([v * b, k_ * e], axis=1) for v, b, k_, e in zip(vh, beta, kb, eg)]
    sol = _each(hdot, ainv, rhs)
    qg = [q * e for q, e in zip(qs, eg)]
    k_end = [k * jnp.exp(gl - g) for k, gl, g in zip(kh, g_last, g_col)]
    e_last = [jnp.exp(gl) for gl in g_last]

    st = [state[h * dk:(h + 1) * dk, :] for h in range(nh)]
    outs = [[] for _ in range(nh)]
    for ci in range(nck):
        idx = [ci * nh + h for h in range(nh)]
        ws = [bdot(jnp.concatenate([sol[i][:, dk:], qg[i]], axis=0), st[h]) for h, i in enumerate(idx)]
        v_new = [sol[i][:, :dk] - w_[:c] for i, w_ in zip(idx, ws)]
        av = [bdot(attn[i], v) for i, v in zip(idx, v_new)]
        kv = [bdot_tn(k_end[i], v) for i, v in zip(idx, v_new)]
        for h, i in enumerate(idx):
            o = ws[h][c:] + av[h]
            st[h] = st[h] * e_last[i] + kv[h]
            outs[h].append(_rms(o, nw) * _silu(z[ci * c:(ci + 1) * c, h * dk:(h + 1) * dk]))
    out = jnp.concatenate([jnp.concatenate(o, axis=0) for o in outs], axis=1)
    return jnp.concatenate(st, axis=0), [out]


def make_hgrn_tile(layer, depth):
    def hgrn_tile(params, state, ins, halos):
        lbp, nwp = params
        (pb,) = ins
        r = pb.shape[0]
        c = HGRN_CHUNK
        kw = HGRN_HEADS * HGRN_DK
        rows = [lbp[i:i + 1, :] for i in range(depth)]
        mx = functools.reduce(jnp.maximum, rows)
        ex = [jnp.exp(x - mx) for x in rows]
        den = functools.reduce(lambda a, b: a + b, ex)
        soft = [e / den for e in ex]
        lb = functools.reduce(lambda a, b: a + b, soft[:layer + 1]) - soft[0]
        nw = nwp[0:1, :]
        q = _silu(pb[:, :kw])
        fr = pb[:, kw:2 * kw]
        logf = jnp.log(lb + (1.0 - lb) * _sigmoid(fr))
        k = (1.0 - lb) * _sigmoid(-fr)
        v = pb[:, 2 * kw:3 * kw]
        gate = pb[:, 3 * kw:]
        incl = _tri_ones(r, c, "incl")
        masks = jnp.concatenate([incl.astype(F32), _tri_ones(r, c, "upto").astype(F32),
                                 _tri_ones(r, c, "all").astype(F32)], axis=0)
        sums = ldot(masks, logf)
        g_cum, g_ref, g_end = sums[:r], sums[r:2 * r], sums[2 * r:]
        qs = q * jnp.exp(g_cum - g_ref)
        ks = k * jnp.exp(g_ref - g_cum)
        qg = q * jnp.exp(g_cum)
        k_end = k * jnp.exp(g_end - g_cum)
        e_end = jnp.exp(g_end)
        sls = [slice(h * HGRN_DK, (h + 1) * HGRN_DK) for h in range(HGRN_HEADS)]
        attn = [jnp.where(incl, bdot_nt(qs[:, sl], ks[:, sl]), 0.0) for sl in sls]
        o_intra = [bdot(a, v[:, sl]) for a, sl in zip(attn, sls)]
        nsub, dk = r // c, HGRN_DK
        own_block = lax.div(_iota((r, nsub * dk), 0), c) == lax.div(_iota((r, nsub * dk), 1), dk)

        def spread(a):
            return jnp.where(own_block, jnp.concatenate([a] * nsub, axis=1), 0.0)

        kv = [bdot_tn(v[:, sl], spread(k_end[:, sl])) for sl in sls]
        s_t = [state[sl, :] for sl in sls]
        entry = [[] for _ in sls]
        for j in range(nsub):
            for lst, s_h in zip(entry, s_t):
                lst.append(s_h)
            s_t = [s_h * e_end[j * c:j * c + 1, sl] + x[:, j * dk:(j + 1) * dk] for s_h, sl, x in zip(s_t, sls, kv)]
        o_inter = [bdot_nt(spread(qg[:, sl]), jnp.concatenate(e, axis=1)) for sl, e in zip(sls, entry)]
        outs = [_rms(oa + ob, nw) * _silu(gate[:, sl]) for oa, ob, sl in zip(o_intra, o_inter, sls)]
        return jnp.concatenate(s_t, axis=0), [jnp.concatenate(outs, axis=1)]
    return hgrn_tile


def ssd_tile(params, state, ins, halos):
    conv_w, pv, ps = params
    (pc,), (hc,) = ins, halos
    r = pc.shape[0]
    c = SSD_CHUNK
    inner = SSD_HEADS * SSD_P
    gw = inner // SSD_GROUPS
    z = pc[:, :inner]
    xbc = _silu(_causal_conv(conv_w, hc[:, inner:inner + 1024], pc[:, inner:inner + 1024], 4) + pv[0:1, :])
    ssm = pc[:, inner + 1024:]
    xs = xbc[:, :inner]
    bm = xbc[:, inner:inner + SSD_GROUPS * SSD_STATE]
    cm = xbc[:, inner + SSD_GROUPS * SSD_STATE:]
    a_log, dtb, dsk = ps[0:1, :], ps[1:2, :], ps[2:3, :]
    nw = pv[1:2, :inner]
    dt = _softplus(ssm + dtb)
    da = dt * (-jnp.exp(a_log))
    expand = (lax.div(_iota((128, inner), 1), SSD_P) == _iota((128, inner), 0)).astype(F32)
    xdt = xs * rdot(dt, expand)
    d_e = rdot(jnp.concatenate([dsk] * 8, axis=0), expand)[0:1, :]
    incl = _tri_ones(c, c, "incl")
    lmat = incl.astype(F32)
    st = [state[g * SSD_STATE:(g + 1) * SSD_STATE, :] for g in range(SSD_GROUPS)]
    hpg = SSD_HEADS // SSD_GROUPS
    nck = r // c
    groups = range(SSD_GROUPS)
    cg = [(ci, g) for ci in range(nck) for g in groups]
    rows = [slice(ci * c, (ci + 1) * c) for ci in range(nck)]
    gls = [slice(g * gw, (g + 1) * gw) for g in groups]
    acs = [ldot(lmat, da[rs, :]) for rs in rows]
    acs_t = [a.T for a in acs]
    acs_e = [rdot(a, expand) for a in acs]
    last_e = [a[c - 1:c, :] for a in acs_e]
    bm_g = [bm[rows[ci], g * SSD_STATE:(g + 1) * SSD_STATE] for ci, g in cg]
    cm_g = [cm[rows[ci], g * SSD_STATE:(g + 1) * SSD_STATE] for ci, g in cg]
    cb = _each(bdot_nt, cm_g, bm_g)
    heads = [(i, ci, g * hpg + hg) for i, (ci, g) in enumerate(cg) for hg in range(hpg)]
    seg = [jnp.where(incl, jnp.exp(jnp.where(incl, acs[ci][:, hh:hh + 1] - acs_t[ci][hh:hh + 1, :], 0.0)), 0.0)
           for _, ci, hh in heads]
    yd = [bdot(cb[i] * sg, xdt[rows[ci], hh * SSD_P:(hh + 1) * SSD_P]) for (i, ci, hh), sg in zip(heads, seg)]
    y_diag = [jnp.concatenate(yd[i * hpg:(i + 1) * hpg], axis=1) for i in range(len(cg))]
    xw = [xdt[rows[ci], gls[g]] * jnp.exp(last_e[ci][:, gls[g]] - acs_e[ci][:, gls[g]]) for ci, g in cg]
    e_acs = [jnp.exp(acs_e[ci][:, gls[g]]) for ci, g in cg]
    e_last = [jnp.exp(last_e[ci][:, gls[g]]) for ci, g in cg]
    kv = _each(bdot_tn, bm_g, xw)
    ys = []
    for ci in range(nck):
        idx = [ci * SSD_GROUPS + g for g in groups]
        y_off = [bdot(cm_g[i], st[g]) * e_acs[i] for g, i in zip(groups, idx)]
        st = [st[g] * e_last[i] + kv[i] for g, i in zip(groups, idx)]
        ys.append(jnp.concatenate([y_diag[i] + yo for i, yo in zip(idx, y_off)], axis=1))
    y = jnp.concatenate(ys, axis=0) + d_e * xs
    yz = y * _silu(z)
    out = jnp.concatenate([_rms(yz[:, g * gw:(g + 1) * gw], nw[:, g * gw:(g + 1) * gw])
                           for g in range(SSD_GROUPS)], axis=1)
    return jnp.concatenate(st, axis=0), [out]


def convglu_tile(params, state, ins, halos):
    (cw,) = params
    (u,), (hu,) = ins, halos
    y = _causal_conv(cw, hu, u, 3) + cw[3:4, :]
    return None, [_silu(y[:, :FFN_HIDDEN]) * y[:, FFN_HIDDEN:]]


def convglu_tile_t(params, state, ins, halos):
    _, (act,) = convglu_tile(params, state, ins, halos)
    return None, [act, act.T]


def _halo_map(nt, r):
    return lambda b, n: (jnp.maximum((b * nt + n) * (r // 8) - 1, 0), 0)


def _exchange_copies(plan, local_plan, in_refs, out_refs, send_sems, recv_sems, local_sems):
    x, y, c = lax.axis_index("x"), lax.axis_index("y"), lax.axis_index("c")
    copies = []
    for k, fn in enumerate(plan):
        src, dst, peer = fn(in_refs, out_refs, x, y, c)
        copies.append(pltpu.make_async_remote_copy(src_ref=src, dst_ref=dst, send_sem=send_sems.at[k],
                                                   recv_sem=recv_sems.at[k], device_id=peer, device_id_type=MESH))
    for k, fn in enumerate(local_plan):
        src, dst = fn(in_refs, out_refs, x, y, c)
        copies.append(pltpu.make_async_copy(src, dst, local_sems.at[k]))
    return copies


def _exchange_sems(plan, local_plan):
    return [pltpu.SemaphoreType.DMA((max(len(plan), 1),)), pltpu.SemaphoreType.DMA((max(len(plan), 1),)),
            pltpu.SemaphoreType.DMA((max(len(local_plan), 1),))]


def _host_exchange(side, body, in_specs, o_specs, out_shape, scratch, args, grid):
    s_ins, s_shapes, plan, local_plan, then = side
    n_in, n_out, n_scr = len(in_specs), len(o_specs), len(scratch)
    k_in, k_out = len(s_ins), len(s_shapes)
    any_spec = pl.BlockSpec(memory_space=pl.ANY)

    def hosted(*refs):
        own_in, s_in = refs[:n_in], refs[n_in:n_in + k_in]
        o0 = n_in + k_in
        own_out, s_out = refs[o0:o0 + n_out], refs[o0 + n_out:o0 + n_out + k_out]
        rest = refs[o0 + n_out + k_out:]
        own_scr, sems, sems_then = rest[:n_scr], rest[n_scr:n_scr + 3], rest[n_scr + 3:]
        ids = [pl.program_id(a) for a in range(len(grid))]
        first = functools.reduce(lambda p, q: p & q, [i == 0 for i in ids])
        last = functools.reduce(lambda p, q: p & q, [i == g - 1 for i, g in zip(ids, grid)])

        @pl.when(first)
        def _():
            for cp in _exchange_copies(plan, local_plan, s_in, s_out, *sems):
                cp.start()

        body(*own_in, *own_out, *own_scr)

        @pl.when(last)
        def _():
            for cp in _exchange_copies(plan, local_plan, s_in, s_out, *sems):
                cp.wait()
            passed = _exchange_copies(then, (), s_in, s_out, *sems_then)
            for cp in passed:
                cp.start()
            for cp in passed:
                cp.wait()

    return (hosted, list(in_specs) + [any_spec] * k_in, list(o_specs) + [any_spec] * k_out,
            list(out_shape) + list(s_shapes),
            list(scratch) + _exchange_sems(plan, local_plan) + _exchange_sems(then, ()),
            list(args) + list(s_ins))


def seq_fwd(name, tile_fn, params, ins, use_halo, out_specs, state_shape, nb, s, r, side=None):
    nt = s // r
    n_p, n_i, n_o = len(params), len(ins), len(out_specs)
    has_state = state_shape is not None

    def body(*refs):
        p_refs, i_refs = refs[:n_p], refs[n_p:n_p + n_i]
        h_refs = refs[n_p + n_i:n_p + 2 * n_i] if use_halo else ()
        k = n_p + n_i + len(h_refs)
        o_refs = refs[k:k + n_o]
        n = pl.program_id(1)
        state = None
        if has_state:
            sv_ref, st_ref = refs[k + n_o], refs[k + n_o + 1]

            @pl.when(n == 0)
            def _():
                st_ref[...] = jnp.zeros(state_shape, F32)

            state = st_ref[...]
            sv_ref[0, 0] = state
        pv = [p[...] for p in p_refs]
        iv = [i[...].astype(F32) for i in i_refs]
        hv = [jnp.where(n > 0, h[...].astype(F32), 0.0) for h in h_refs]
        new_state, ov = tile_fn(pv, state, iv, hv)
        for o_ref, o in zip(o_refs, ov):
            o_ref[...] = o.astype(o_ref.dtype)
        if has_state:
            st_ref[...] = new_state

    row = lambda b, n: (b * nt + n, 0)
    in_specs = [pl.BlockSpec(p.shape, lambda b, n: (0, 0)) for p in params]
    in_specs += [pl.BlockSpec((r, a.shape[1]), row) for a in ins]
    if use_halo:
        in_specs += [pl.BlockSpec((8, a.shape[1]), _halo_map(nt, r)) for a in ins]
    col = lambda b, n: (0, b * nt + n)
    out_shape, o_specs = [], []
    for w, dt, *transposed in out_specs:
        out_shape.append(jax.ShapeDtypeStruct((w, nb * s) if transposed else (nb * s, w), dt))
        o_specs.append(pl.BlockSpec((w, r), col) if transposed else pl.BlockSpec((r, w), row))
    scratch = []
    if has_state:
        out_shape.append(jax.ShapeDtypeStruct((nb, nt) + tuple(state_shape), F32))
        o_specs.append(pl.BlockSpec((1, 1) + tuple(state_shape), lambda b, n: (b, n, 0, 0)))
        scratch.append(pltpu.VMEM(tuple(state_shape), F32))
    args = list(params) + list(ins) + (list(ins) if use_halo else [])
    if side is not None:
        body, in_specs, o_specs, out_shape, scratch, args = _host_exchange(
            side, body, in_specs, o_specs, out_shape, scratch, args, (nb, nt))
    return pl.pallas_call(body, grid=(nb, nt), in_specs=in_specs, out_specs=o_specs, out_shape=out_shape,
                          scratch_shapes=scratch, compiler_params=_cparams(("arbitrary", "arbitrary")),
                          name=name)(*args)


def seq_bwd(name, tile_fn, params, ins, use_halo, states, douts, din_dtypes, state_shape, nb, s, r, side=None):
    nt = s // r
    n_p, n_i, n_o = len(params), len(ins), len(douts)
    has_state = state_shape is not None

    def body(*refs):
        p_refs, i_refs = refs[:n_p], refs[n_p:n_p + n_i]
        h_refs = refs[n_p + n_i:n_p + 2 * n_i] if use_halo else ()
        k = n_p + n_i + len(h_refs)
        sv_ref = None
        if has_state:
            sv_ref = refs[k]
            k += 1
        do_refs = refs[k:k + n_o]
        k += n_o
        di_refs, dp_refs = refs[k:k + n_i], refs[k + n_i:k + n_i + n_p]
        k += n_i + n_p
        dst_ref = None
        if has_state:
            dst_ref = refs[k]
            k += 1
        dh_refs = refs[k:k + len(h_refs)]
        b, nn = pl.program_id(0), pl.program_id(1)
        n = nt - 1 - nn

        @pl.when((b == 0) & (nn == 0))
        def _():
            for dp in dp_refs:
                dp[...] = jnp.zeros(dp.shape, F32)

        @pl.when(nn == 0)
        def _():
            if has_state:
                dst_ref[...] = jnp.zeros(state_shape, F32)
            for dh in dh_refs:
                dh[...] = jnp.zeros(dh.shape, F32)

        pv = [p[...] for p in p_refs]
        iv = [i[...].astype(F32) for i in i_refs]
        hv = [jnp.where(n > 0, h[...].astype(F32), 0.0) for h in h_refs]
        if has_state:
            f = lambda pv_, st_, iv_, hv_: tile_fn(pv_, st_, iv_, hv_)
            _, vjp = jax.vjp(f, pv, sv_ref[0, 0], iv, hv)
            dpv, dst, div, dhv = vjp((dst_ref[...], [d[...].astype(F32) for d in do_refs]))
            dst_ref[...] = dst
        else:
            f = lambda pv_, iv_, hv_: tile_fn(pv_, None, iv_, hv_)[1]
            _, vjp = jax.vjp(f, pv, iv, hv)
            dpv, div, dhv = vjp([d[...].astype(F32) for d in do_refs])
        for j, (di_ref, d) in enumerate(zip(di_refs, div)):
            if use_halo:
                d = jnp.concatenate([d[:r - 8], d[r - 8:] + dh_refs[j][...]], axis=0)
            di_ref[...] = d.astype(di_ref.dtype)
        for dh_ref, d in zip(dh_refs, dhv):
            dh_ref[...] = d
        for dp_ref, d in zip(dp_refs, dpv):
            dp_ref[...] += d

    row = lambda b, nn: (b * nt + nt - 1 - nn, 0)
    hmap = _halo_map(nt, r)
    in_specs = [pl.BlockSpec(p.shape, lambda b, nn: (0, 0)) for p in params]
    in_specs += [pl.BlockSpec((r, a.shape[1]), row) for a in ins]
    if use_halo:
        in_specs += [pl.BlockSpec((8, a.shape[1]), lambda b, nn: hmap(b, nt - 1 - nn)) for a in ins]
    args = list(params) + list(ins) + (list(ins) if use_halo else [])
    scratch = []
    if has_state:
        in_specs.append(pl.BlockSpec((1, 1) + tuple(state_shape), lambda b, nn: (b, nt - 1 - nn, 0, 0)))
        args.append(states)
        scratch.append(pltpu.VMEM(tuple(state_shape), F32))
    in_specs += [pl.BlockSpec((r, d.shape[1]), row) for d in douts]
    args += list(douts)
    if use_halo:
        scratch += [pltpu.VMEM((8, a.shape[1]), F32) for a in ins]
    out_shape = [jax.ShapeDtypeStruct(a.shape, dt) for a, dt in zip(ins, din_dtypes)]
    out_shape += [jax.ShapeDtypeStruct(p.shape, F32) for p in params]
    o_specs = [pl.BlockSpec((r, a.shape[1]), row) for a in ins]
    o_specs += [pl.BlockSpec(p.shape, lambda b, nn: (0, 0)) for p in params]
    if side is not None:
        body, in_specs, o_specs, out_shape, scratch, args = _host_exchange(
            side, body, in_specs, o_specs, out_shape, scratch, args, (nb, nt))
    res = pl.pallas_call(body, grid=(nb, nt), in_specs=in_specs, out_specs=o_specs, out_shape=out_shape,
                         scratch_shapes=scratch, compiler_params=_cparams(("arbitrary", "arbitrary")),
                         name=name)(*args)
    if side is not None:
        return res[:n_i], res[n_i:n_i + n_p], res[n_i + n_p:]
    return res[:n_i], res[n_i:]


def matmul(name, a, b, mode, out_dtype=F32, addend=None, resid=None, tm=512, tn=None, tk=None):
    if mode == "nn":
        (m, kd), (_, n) = a.shape, b.shape
    elif mode == "nt":
        (m, kd), (n, _) = a.shape, b.shape
    else:
        (kd, m), (_, n) = a.shape, b.shape
    tm, tn, tk = min(tm, m if resid is None else resid[2]), min(tn or n, n), min(tk or kd, kd)
    nk = kd // tk
    assert m % tm == 0 and n % tn == 0 and kd % tk == 0
    dims = {"nn": ((1,), (0,)), "nt": ((1,), (1,)), "tn": ((0,), (0,))}[mode]
    extra = [] if addend is None else [addend]
    if resid is not None:
        extra = [resid[0], resid[1]]
    n_in, n_out = 2 + len(extra), 1 if resid is None else 2

    def body(*refs):
        a_ref, b_ref = refs[0], refs[1]
        part = lax.dot_general(a_ref[...].astype(BF16), b_ref[...].astype(BF16), (dims, ((), ())),
                               preferred_element_type=F32)

        def finish(acc):
            if resid is not None:
                refs[n_in][...] = refs[2][...] + refs[3][0] * acc
                refs[n_in + 1][...] = acc.astype(BF16)
            else:
                if addend is not None:
                    acc = acc + refs[2][...]
                refs[n_in][...] = acc.astype(refs[n_in].dtype)

        if nk == 1:
            finish(part)
        else:
            acc_ref = refs[n_in + n_out]
            k = pl.program_id(2)

            @pl.when(k == 0)
            def _():
                acc_ref[...] = part

            @pl.when(k > 0)
            def _():
                acc_ref[...] += part

            @pl.when(k == nk - 1)
            def _():
                finish(acc_ref[...])

    if mode == "tn":
        a_spec = pl.BlockSpec((tk, tm), lambda j, i, k: (k, i))
    else:
        a_spec = pl.BlockSpec((tm, tk), lambda j, i, k: (i, k))
    if mode == "nt":
        b_spec = pl.BlockSpec((tn, tk), lambda j, i, k: (j, k))
    else:
        b_spec = pl.BlockSpec((tk, tn), lambda j, i, k: (k, j))
    o_spec = pl.BlockSpec((tm, tn), lambda j, i, k: (i, j))
    in_specs = [a_spec, b_spec] + [o_spec] * (len(extra) > 0)
    out_specs, out_shape = o_spec, jax.ShapeDtypeStruct((m, n), out_dtype)
    if resid is not None:
        rows = resid[2]
        assert rows % tm == 0
        in_specs.append(pl.BlockSpec((1, 1, tn), lambda j, i, k: (lax.div(i * tm, rows), 0, j)))
        out_specs, out_shape = [o_spec, o_spec], [out_shape, jax.ShapeDtypeStruct((m, n), BF16)]
    scratch = [pltpu.VMEM((tm, tn), F32)] if nk > 1 else []
    return pl.pallas_call(body, grid=(n // tn, m // tm, nk), in_specs=in_specs, out_specs=out_specs,
                          out_shape=out_shape, scratch_shapes=scratch,
                          compiler_params=_cparams(("parallel", "parallel", "arbitrary")), name=name)(a, b, *extra)


def _normmod(x, nw, shift, scale):
    return _rms(x, nw) * (1.0 + scale) + shift


def _row_specs(nb, s, tr, d):
    nt = s // tr
    row = pl.BlockSpec((tr, d), lambda b, i: (b * nt + i, 0))
    per_seq = pl.BlockSpec((1, 1, d), lambda b, i: (b, 0, 0))
    full = pl.BlockSpec((1, d), lambda b, i: (0, 0))
    return nt, row, per_seq, full


def normmod_fwd(name, x, nw, shift, scale, nb, s, tr=512):
    d, tr = x.shape[1], min(tr, s)
    nt, row, per_seq, full = _row_specs(nb, s, tr, d)

    def body(x_ref, nw_ref, sh_ref, sc_ref, h_ref, ht_ref):
        h = _normmod(x_ref[...], nw_ref[...], sh_ref[0], sc_ref[0])
        h_ref[...] = h.astype(h_ref.dtype)
        ht_ref[...] = h.T.astype(ht_ref.dtype)

    return pl.pallas_call(body, grid=(nb, nt), in_specs=[row, full, per_seq, per_seq],
                          out_specs=[row, pl.BlockSpec((d, tr), lambda b, i: (0, b * nt + i))],
                          out_shape=[jax.ShapeDtypeStruct(x.shape, BF16), jax.ShapeDtypeStruct(x.shape[::-1], BF16)],
                          compiler_params=_cparams(("parallel", "parallel")), name=name)(x, nw, shift, scale)


def normmod_bwd(name, x, nw, shift, scale, dh, dres, nb, s, tr=512):
    d, tr = x.shape[1], min(tr, s)
    nt, row, per_seq, full = _row_specs(nb, s, tr, d)

    def body(x_ref, nw_ref, sh_ref, sc_ref, dh_ref, dres_ref, dx_ref, dnw_ref, dsh_ref, dsc_ref):
        b, i = pl.program_id(0), pl.program_id(1)

        @pl.when((b == 0) & (i == 0))
        def _():
            dnw_ref[...] = jnp.zeros(dnw_ref.shape, F32)

        @pl.when(i == 0)
        def _():
            dsh_ref[...] = jnp.zeros(dsh_ref.shape, F32)
            dsc_ref[...] = jnp.zeros(dsc_ref.shape, F32)

        _, vjp = jax.vjp(_normmod, x_ref[...], nw_ref[...], sh_ref[0], sc_ref[0])
        dx, dnw, dsh, dsc = vjp(dh_ref[...])
        dx_ref[...] = dres_ref[...] + dx
        dnw_ref[...] += dnw
        dsh_ref[0] += dsh
        dsc_ref[0] += dsc

    out_shape = [jax.ShapeDtypeStruct(x.shape, F32), jax.ShapeDtypeStruct((1, d), F32),
                 jax.ShapeDtypeStruct((nb, 1, d), F32), jax.ShapeDtypeStruct((nb, 1, d), F32)]
    return pl.pallas_call(body, grid=(nb, nt), in_specs=[row, full, per_seq, per_seq, row, row],
                          out_specs=[row, full, per_seq, per_seq], out_shape=out_shape,
                          compiler_params=_cparams(("arbitrary", "arbitrary")),
                          name=name)(x, nw, shift, scale, dh, dres)


def _merge(oa, ob, oc, graw, gate1, wa, wb, wc, wo):
    d = wo.shape[0]
    g = _sigmoid(graw)
    merged = g[:, :d] * bdot(oa, wa) + g[:, d:2 * d] * bdot(ob, wb) + g[:, 2 * d:] * bdot(oc, wc)
    return gate1 * bdot(merged, wo)


def _merge_specs(nb, s, tr, d, wbr):
    nt, row, per_seq, _ = _row_specs(nb, s, tr, d)
    o_spec = pl.BlockSpec((tr, wbr), lambda b, i: (b * nt + i, 0))
    g_spec = pl.BlockSpec((tr, 3 * d), lambda b, i: (b * nt + i, 0))
    wbr_spec = pl.BlockSpec((wbr, d), lambda b, i: (0, 0))
    wo_spec = pl.BlockSpec((d, d), lambda b, i: (0, 0))
    return nt, row, per_seq, o_spec, g_spec, wbr_spec, wo_spec


def merge_fwd(name, x, oa, ob, oc, pg, gate1, wa, wb, wc, wo, nb, s, tr=512):
    d, tr = x.shape[1], min(tr, s)
    nt, row, per_seq, o_spec, g_spec, wbr_spec, wo_spec = _merge_specs(nb, s, tr, d, oa.shape[1])

    def body(x_ref, oa_ref, ob_ref, oc_ref, pg_ref, g1_ref, wa_ref, wb_ref, wc_ref, wo_ref, x1_ref):
        x1_ref[...] = x_ref[...] + _merge(oa_ref[...], ob_ref[...], oc_ref[...], pg_ref[...], g1_ref[0],
                                          wa_ref[...], wb_ref[...], wc_ref[...], wo_ref[...])

    return pl.pallas_call(body, grid=(nb, nt),
                          in_specs=[row, o_spec, o_spec, o_spec, g_spec, per_seq, wbr_spec, wbr_spec, wbr_spec, wo_spec],
                          out_specs=row, out_shape=jax.ShapeDtypeStruct(x.shape, F32),
                          compiler_params=_cparams(("parallel", "parallel")),
                          name=name)(x, oa, ob, oc, pg, gate1, wa, wb, wc, wo)


def merge_bwd(name, oa, ob, oc, pg, gate1, wa, wb, wc, wo, dx1, nb, s, tr=256):
    d, tr = dx1.shape[1], min(tr, s)
    wbr = oa.shape[1]
    nt, row, per_seq, o_spec, g_spec, wbr_spec, wo_spec = _merge_specs(nb, s, tr, d, wbr)

    def body(oa_ref, ob_ref, oc_ref, pg_ref, g1_ref, wa_ref, wb_ref, wc_ref, wo_ref, dx_ref,
             doa_ref, dob_ref, doc_ref, dpg_ref, dg1_ref, dwa_ref, dwb_ref, dwc_ref, dwo_ref):
        b, i = pl.program_id(0), pl.program_id(1)

        @pl.when((b == 0) & (i == 0))
        def _():
            for r in (dwa_ref, dwb_ref, dwc_ref, dwo_ref):
                r[...] = jnp.zeros(r.shape, F32)

        @pl.when(i == 0)
        def _():
            dg1_ref[...] = jnp.zeros(dg1_ref.shape, F32)

        args = [oa_ref[...].astype(F32), ob_ref[...].astype(F32), oc_ref[...].astype(F32), pg_ref[...], g1_ref[0],
                wa_ref[...].astype(F32), wb_ref[...].astype(F32), wc_ref[...].astype(F32), wo_ref[...].astype(F32)]
        _, vjp = jax.vjp(_merge, *args)
        doa, dob, doc, dpg, dg1, dwa, dwb, dwc, dwo = vjp(dx_ref[...])
        doa_ref[...] = doa
        dob_ref[...] = dob
        doc_ref[...] = doc
        dpg_ref[...] = dpg
        dg1_ref[0] += dg1
        dwa_ref[...] += dwa
        dwb_ref[...] += dwb
        dwc_ref[...] += dwc
        dwo_ref[...] += dwo

    t = nb * s
    out_shape = ([jax.ShapeDtypeStruct((t, wbr), F32)] * 3
                 + [jax.ShapeDtypeStruct((t, 3 * d), F32), jax.ShapeDtypeStruct((nb, 1, d), F32)]
                 + [jax.ShapeDtypeStruct((wbr, d), F32)] * 3 + [jax.ShapeDtypeStruct((d, d), F32)])
    return pl.pallas_call(body, grid=(nb, nt),
                          in_specs=[o_spec, o_spec, o_spec, g_spec, per_seq, wbr_spec, wbr_spec, wbr_spec, wo_spec, row],
                          out_specs=[o_spec, o_spec, o_spec, g_spec, per_seq, wbr_spec, wbr_spec, wbr_spec, wo_spec],
                          out_shape=out_shape, compiler_params=_cparams(("arbitrary", "arbitrary")),
                          name=name)(oa, ob, oc, pg, gate1, wa, wb, wc, wo, dx1)


def resid_bwd(name, dx, f, gate, nb, s, tr=512):
    d, tr = dx.shape[1], min(tr, s)
    nt, row, per_seq, _ = _row_specs(nb, s, tr, d)

    def body(dx_ref, f_ref, g_ref, df_ref, dg_ref):
        @pl.when(pl.program_id(1) == 0)
        def _():
            dg_ref[...] = jnp.zeros(dg_ref.shape, F32)

        df_ref[...] = (g_ref[0] * dx_ref[...]).astype(df_ref.dtype)
        dg_ref[0] += jnp.sum(dx_ref[...] * f_ref[...], axis=0, keepdims=True)

    return pl.pallas_call(body, grid=(nb, nt), in_specs=[row, row, per_seq], out_specs=[row, per_seq],
                          out_shape=[jax.ShapeDtypeStruct(dx.shape, BF16), jax.ShapeDtypeStruct((nb, 1, d), F32)],
                          compiler_params=_cparams(("arbitrary", "arbitrary")), name=name)(dx, f, gate)


def loss_head(name, x, fw, target, tr=512):
    t, d = x.shape
    row = pl.BlockSpec((tr, d), lambda i: (i, 0))
    full = pl.BlockSpec((1, d), lambda i: (0, 0))

    def loss_fn(xv, fwv, tv):
        err = _rms(xv, fwv) - tv
        return 0.5 * jnp.sum(jnp.mean(err * err, axis=-1))

    def body(x_ref, fw_ref, t_ref, dx_ref, l_ref, dfw_ref):
        @pl.when(pl.program_id(0) == 0)
        def _():
            l_ref[...] = jnp.zeros(l_ref.shape, F32)
            dfw_ref[...] = jnp.zeros(dfw_ref.shape, F32)

        val, (dx, dfw) = jax.value_and_grad(loss_fn, argnums=(0, 1))(x_ref[...], fw_ref[...], t_ref[...])
        dx_ref[...] = dx
        l_ref[...] += val
        dfw_ref[...] += dfw

    return pl.pallas_call(body, grid=(t // tr,), in_specs=[row, full, row],
                          out_specs=[row, pl.BlockSpec((1, 128), lambda i: (0, 0)), full],
                          out_shape=[jax.ShapeDtypeStruct((t, d), F32), jax.ShapeDtypeStruct((1, 128), F32),
                                     jax.ShapeDtypeStruct((1, d), F32)],
                          compiler_params=_cparams(("arbitrary",)), name=name)(x, fw, target)


def _row_tile(rows, cols, n_arrays):
    budget = 24 * 1024 * 1024 // (8 * cols * max(n_arrays, 1))
    tr = rows
    while tr > max(budget, 16) and tr % 2 == 0 and (tr // 2) % 16 == 0:
        tr //= 2
    return tr


def elementwise(name, fn, ins, out_dtypes):
    rows, cols = ins[0].shape
    tr = _row_tile(rows, cols, len(ins) + len(out_dtypes))
    spec = pl.BlockSpec((tr, cols), lambda i: (i, 0))
    n_in = len(ins)

    def body(*refs):
        outs = fn(*[r[...] for r in refs[:n_in]])
        for o_ref, o in zip(refs[n_in:], outs):
            o_ref[...] = o.astype(o_ref.dtype)

    return pl.pallas_call(body, grid=(rows // tr,), in_specs=[spec] * n_in, out_specs=[spec] * len(out_dtypes),
                          out_shape=[jax.ShapeDtypeStruct((rows, cols), dt) for dt in out_dtypes],
                          compiler_params=_cparams(("parallel",)), name=name)(*ins)


def _adamw(w, g, m, v):
    m = ADAM_B1 * m + (1.0 - ADAM_B1) * g
    v = ADAM_B2 * v + (1.0 - ADAM_B2) * (g * g)
    m_hat = m / (1.0 - ADAM_B1 ** ADAM_STEP)
    v_hat = v / (1.0 - ADAM_B2 ** ADAM_STEP)
    delta = -ADAM_LR * (m_hat / (jnp.sqrt(v_hat) + ADAM_EPS) + ADAM_WD * w)
    return delta, m, v


def adamw(name, w, g, m, v):
    return elementwise(name, _adamw, [w, g, m, v], [F32, F32, F32])


_ANY = pl.BlockSpec(memory_space=pl.ANY)


def _coords():
    return lax.axis_index("x"), lax.axis_index("y"), lax.axis_index("c")


def allgather8(name, arrays, halves):
    n = len(arrays)

    def body(*refs):
        in_refs, out_refs = refs[:n], refs[n:2 * n]
        send_sems, recv_sems, local_sems = refs[2 * n:]
        x, y, c = _coords()
        me, sibling = (x, y, c), (x, y, 1 - c)
        chips = [(1 - x, y), (x, 1 - y), (1 - x, 1 - y)]

        def blk(i, px, py, pc):
            return out_refs[i].at[4 * px + 2 * py + pc]

        def piece(i):
            return in_refs[i].at[c] if halves[i] else in_refs[i]

        def copy(i, k, block, to, src=None):
            return pltpu.make_async_remote_copy(
                src_ref=blk(i, *block) if src is None else src, dst_ref=blk(i, *block),
                send_sem=send_sems.at[7 * i + k], recv_sem=recv_sems.at[7 * i + k],
                device_id=to, device_id_type=MESH)

        mine = [pltpu.make_async_copy(piece(i), blk(i, *me), local_sems.at[i]) for i in range(n)]
        for cp in mine:
            cp.start()
        first = []
        for i in range(n):
            first.append(copy(i, 0, me, sibling, src=piece(i)))
            first += [copy(i, 1 + j, me, (*chip, c), src=piece(i)) for j, chip in enumerate(chips)]
        for cp in first:
            cp.start()
        passed = []
        for j, chip in enumerate(chips):
            for i in range(n):
                copy(i, 1 + j, (*chip, c), me).wait_recv()
                fwd = copy(i, 4 + j, (*chip, c), sibling)
                fwd.start()
                passed.append(fwd)
        for i in range(n):
            copy(i, 0, sibling, me).wait_recv()
            for j, chip in enumerate(chips):
                copy(i, 4 + j, (*chip, 1 - c), me).wait_recv()
        for cp in first + passed:
            cp.wait_send()
        for cp in mine:
            cp.wait()

    out_shape = []
    for a, hv in zip(arrays, halves):
        out_shape.append(jax.ShapeDtypeStruct((N_DEV,) + tuple(a.shape[1:] if hv else a.shape), a.dtype))
    return pl.pallas_call(
        body, in_specs=[_ANY] * n, out_specs=[_ANY] * n, out_shape=out_shape,
        scratch_shapes=[pltpu.SemaphoreType.DMA((7 * n,)), pltpu.SemaphoreType.DMA((7 * n,)),
                        pltpu.SemaphoreType.DMA((n,))],
        name=name)(*arrays)


def exchange(name, ins, out_shapes, plan, local_plan=()):
    n_in, n_out = len(ins), len(out_shapes)

    def body(*refs):
        copies = _exchange_copies(plan, local_plan, refs[:n_in], refs[n_in:n_in + n_out], *refs[n_in + n_out:])
        for cp in copies:
            cp.start()
        for cp in copies:
            cp.wait()

    return pl.pallas_call(
        body, in_specs=[_ANY] * n_in, out_specs=[_ANY] * n_out, out_shape=out_shapes,
        scratch_shapes=_exchange_sems(plan, local_plan), name=name)(*ins)


def sum_halves(name, gs, recv, c_arr):
    _, _, hr, cs = gs.shape
    tr = _row_tile(hr, cs, 4)

    def body(c_ref, g_ref, r_ref, qf_ref, qb_ref):
        q = g_ref[0, 0] + r_ref[0, 0]
        qf_ref[0] = q
        qb_ref[0] = q.astype(BF16)

    grid_spec = pltpu.PrefetchScalarGridSpec(
        num_scalar_prefetch=1, grid=(N_CHIPS, hr // tr),
        in_specs=[pl.BlockSpec((1, 1, tr, cs), lambda j, i, c_ref: (j, c_ref[0], i, 0)),
                  pl.BlockSpec((1, 1, tr, cs), lambda j, i, c_ref: (j, 0, i, 0))],
        out_specs=[pl.BlockSpec((1, tr, cs), lambda j, i, c_ref: (j, i, 0))] * 2)
    return pl.pallas_call(body, grid_spec=grid_spec,
                          out_shape=[jax.ShapeDtypeStruct((N_CHIPS, hr, cs), F32),
                                     jax.ShapeDtypeStruct((N_CHIPS, hr, cs), BF16)],
                          compiler_params=_cparams(("parallel", "parallel")), name=name)(c_arr, gs, recv)


def sum_chips(name, qf, recv, chip_arr):
    _, hr, cs = qf.shape
    tr = _row_tile(hr, cs, 4)

    def body(chip_ref, q_ref, a_ref, b_ref, c_ref, o_ref):
        o_ref[...] = q_ref[0] + a_ref[...].astype(F32) + b_ref[...].astype(F32) + c_ref[...].astype(F32)

    row = pl.BlockSpec((tr, cs), lambda i, chip_ref: (i, 0))
    grid_spec = pltpu.PrefetchScalarGridSpec(
        num_scalar_prefetch=1, grid=(hr // tr,),
        in_specs=[pl.BlockSpec((1, tr, cs), lambda i, chip_ref: (chip_ref[0], i, 0)), row, row, row],
        out_specs=row)
    return pl.pallas_call(body, grid_spec=grid_spec, out_shape=jax.ShapeDtypeStruct((hr, cs), F32),
                          compiler_params=_cparams(("parallel",)), name=name)(chip_arr, qf, *recv)


def adamw_halves(name, w, m, v, layer, g_mine, g_other, c_arr, prev=None):
    _, _, hr, cs = w.shape
    tr = _row_tile(hr, cs, 9)

    def body(c_ref, w_ref, m_ref, v_ref, gm_ref, go_ref, *rest):
        g_ref, d_ref, nm_ref, nv_ref = rest[-4:]
        g = jnp.where(pl.program_id(0) == c_ref[0], gm_ref[...], go_ref[...])
        delta, nm, nv = _adamw(w_ref[0, 0], g, m_ref[0, 0], v_ref[0, 0])
        g_ref[0, 0], d_ref[0, 0], nm_ref[0, 0], nv_ref[0, 0] = g, delta, nm, nv

    half = pl.BlockSpec((1, 1, tr, cs), lambda h, i, c_ref: (layer, h, i, 0))
    row = pl.BlockSpec((tr, cs), lambda h, i, c_ref: (i, 0))
    in_specs, args, aliases = [half, half, half, row, row], [c_arr, w, m, v, g_mine, g_other], {}
    if prev is not None:
        in_specs += [pl.BlockSpec(memory_space=pl.ANY)] * 4
        args += list(prev)
        aliases = {6 + k: k for k in range(4)}
    grid_spec = pltpu.PrefetchScalarGridSpec(num_scalar_prefetch=1, grid=(2, hr // tr),
                                             in_specs=in_specs, out_specs=[half] * 4)
    return pl.pallas_call(body, grid_spec=grid_spec, out_shape=[jax.ShapeDtypeStruct(w.shape, F32)] * 4,
                          input_output_aliases=aliases, compiler_params=_cparams(("parallel", "parallel")),
                          name=name)(*args)


def sum8(name, g):
    _, rows, cols = g.shape
    tr = _row_tile(rows, cols, 9)

    def body(*refs):
        acc = refs[0][0]
        for r in refs[1:N_DEV]:
            acc = acc + r[0]
        refs[N_DEV][...] = acc

    in_specs = [pl.BlockSpec((1, tr, cols), functools.partial(lambda k, i: (k, i, 0), k)) for k in range(N_DEV)]
    return pl.pallas_call(body, grid=(rows // tr,), in_specs=in_specs,
                          out_specs=pl.BlockSpec((tr, cols), lambda i: (i, 0)),
                          out_shape=jax.ShapeDtypeStruct((rows, cols), F32),
                          compiler_params=_cparams(("parallel",)), name=name)(*([g] * N_DEV))


_QKV, _AB, _GZ = (0, 1536), (1536, 1544), (1544, 2056)
_HG = (2056, 4104)
_SZ, _XBC, _DT = (4104, 4616), (4616, 5640), (5640, 5648)
_GATES = (5648, 8720)


def _split_w_in(w8):
    _, hr, cs = w8.shape
    w4 = w8.reshape(N_CHIPS, 2 * hr, cs)

    def cols(rng):
        lo, hi = rng
        return [w4[j][:, max(lo, j * cs) - j * cs:min(hi, (j + 1) * cs) - j * cs]
                for j in range(N_CHIPS) if max(lo, j * cs) < min(hi, (j + 1) * cs)]

    pad = [jnp.zeros((2 * hr, 120), w8.dtype)]
    return (jnp.concatenate(cols(_GATES), axis=1),
            jnp.concatenate(cols(_QKV) + cols(_GZ) + cols(_AB) + pad, axis=1),
            jnp.concatenate(cols(_HG), axis=1),
            jnp.concatenate(cols(_SZ) + cols(_XBC) + cols(_DT) + pad, axis=1))


def _stack_w_in(g, a, b, c):
    segments = [(a, 0, 1536), (a, 2048, 2056), (a, 1536, 2048), (b, 0, 2048), (c, 0, 512), (c, 512, 1536),
                (c, 1536, 1544), (g, 0, 3072)]
    cs = sum(s1 - s0 for _, s0, s1 in segments) // N_CHIPS
    chips = []
    for j in range(N_CHIPS):
        parts, off = [], 0
        for arr, s0, s1 in segments:
            u0, u1 = max(j * cs, off), min((j + 1) * cs, off + s1 - s0)
            if u0 < u1:
                parts.append(arr[:, s0 + u0 - off:s0 + u1 - off])
            off += s1 - s0
        chips.append(jnp.concatenate(parts, axis=1))
    rows = g.shape[0]
    return jnp.stack(chips).reshape(N_CHIPS, 2, rows // 2, cs)


def _rows8(rows, width):
    out = [jnp.pad(r.astype(F32), (0, width - r.shape[0])) for r in rows]
    out += [jnp.zeros((width,), F32)] * (8 - len(out))
    return jnp.stack(out)


class _Packer:
    def __init__(self):
        self.items, self.size = [], 0

    def add(self, name, shape):
        n = 1
        for d in shape:
            n *= d
        self.items.append((name, tuple(shape), self.size, n))
        self.size += n

    def rows(self):
        return -(-self.size // 8192) * 8

    def pack(self, values):
        flat = [values[name].astype(F32).reshape(-1) for name, _, _, _ in self.items]
        flat.append(jnp.zeros((self.rows() * 1024 - self.size,), F32))
        return jnp.concatenate(flat).reshape(self.rows(), 1024)

    def unpack(self, buf):
        flat = buf.reshape(-1)
        return {name: flat[off:off + n].reshape(shape) for name, shape, off, n in self.items}


def _stack_by_chip(g, axis):
    l, r, c = g.shape
    if axis == 2:
        cs = c // N_CHIPS
        g = g.reshape(l, r, N_CHIPS, cs).transpose(2, 0, 1, 3).reshape(N_CHIPS, 2, l * r // 2, cs)
    else:
        rs = r // N_CHIPS
        g = g.reshape(l, N_CHIPS, rs, c).transpose(1, 0, 2, 3).reshape(N_CHIPS, 2, l * rs // 2, c)
    return g


def _unstack_gathered(w8, l, axis):
    _, hr, cs = w8.shape
    w = w8.reshape(N_CHIPS, l, 2 * hr // l, cs)
    if axis == 2:
        return w.transpose(1, 2, 0, 3).reshape(l, 2 * hr // l, N_CHIPS * cs)
    return w.transpose(1, 0, 2, 3).reshape(l, N_CHIPS * 2 * hr // l, cs)


_BIG = (("w_in", 2), ("w_br_a", 2), ("w_br_b", 2), ("w_br_c", 2), ("w_out", 1), ("ffn_w_up", 2), ("ffn_w_down", 1))
_SMALL = ("b_ada", "norm1_w", "gdn_conv_w", "gdn_a_log", "gdn_dt_bias", "gdn_norm_w", "hgrn_lb_param",
          "hgrn_norm_w", "ssd_conv_w", "ssd_conv_b", "ssd_a_log", "ssd_dt_bias", "ssd_d", "ssd_norm_w",
          "norm2_w", "ffn_conv_w", "ffn_conv_b", "final_norm_w")
_WEIGHTS = ("w_ada", "b_ada", "norm1_w", "w_in", "gdn_conv_w", "gdn_a_log", "gdn_dt_bias", "gdn_norm_w",
            "hgrn_lb_param", "hgrn_norm_w", "ssd_conv_w", "ssd_conv_b", "ssd_a_log", "ssd_dt_bias", "ssd_d",
            "ssd_norm_w", "w_br_a", "w_br_b", "w_br_c", "w_out", "norm2_w", "ffn_w_up", "ffn_conv_w",
            "ffn_conv_b", "ffn_w_down", "final_norm_w")
_R_GDN, _R_HGRN, _R_SSD, _R_FFN = 256, 128, 256, 256


_MASKS = ((1, 0), (0, 1), (1, 1))


def _flip(k, x, y):
    return (1 - x if _MASKS[k][0] else x), (1 - y if _MASKS[k][1] else y)


def _rs_d2d(grads):
    plan = [functools.partial(lambda i, ins, outs, x, y, c: (ins[i].at[:, pl.ds(1 - c, 1)], outs[i], (x, y, 1 - c)), i)
            for i in range(len(grads))]
    return grads, [jax.ShapeDtypeStruct((N_CHIPS, 1) + g.shape[2:], F32) for g in grads], plan, (), ()


def _rs_ici(grads, recv, tag):
    n = len(grads)
    c_arr = lax.axis_index("c").astype(jnp.int32).reshape(1)
    q = [sum_halves("rs_sum_d2d%s_%d" % (tag, i), g, r, c_arr) for i, (g, r) in enumerate(zip(grads, recv))]
    qf, qb = [a for a, _ in q], [b for _, b in q]

    def ici(i, k, ins, outs, x, y, c):
        px, py = _flip(k, x, y)
        return ins[i].at[2 * px + py], outs[3 * i + k], (px, py, c)

    plan = [functools.partial(ici, i, k) for i in range(n) for k in range(3)]
    shapes = [jax.ShapeDtypeStruct(g.shape[2:], BF16) for g in grads for _ in range(3)]
    return qf, (qb, shapes, plan, (), ())


def _rs_finish(qf, res, tag):
    n = len(qf)
    chip_arr = (2 * lax.axis_index("x") + lax.axis_index("y")).astype(jnp.int32).reshape(1)
    red = [sum_chips("rs_sum_ici%s_%d" % (tag, i), qf[i], res[3 * i:3 * i + 3], chip_arr) for i in range(n)]
    plan = [functools.partial(lambda i, ins, outs, x, y, c: (ins[i], outs[i], (x, y, 1 - c)), i) for i in range(n)]
    other = exchange("rs_swap" + tag, red, [jax.ShapeDtypeStruct(r.shape, F32) for r in red], plan)
    return red, other


def _gather_side(pieces):
    n = len(pieces)

    def send(i, k, ins, outs, x, y, c):
        px, py = _flip(k, x, y)
        return ins[i].at[c], outs[i].at[2 * (2 * x + y) + c], (px, py, c)

    def to_sibling(i, h, ins, outs, x, y, c):
        return ins[i].at[h], outs[i].at[2 * (2 * x + y) + h], (x, y, 1 - c)

    def pass_on(i, k, ins, outs, x, y, c):
        px, py = _flip(k, x, y)
        blk = 2 * (2 * px + py) + c
        return outs[i].at[blk], outs[i].at[blk], (x, y, 1 - c)

    plan = [functools.partial(send, i, k) for i in range(n) for k in range(3)]
    plan += [functools.partial(to_sibling, i, h) for i in range(n) for h in range(2)]
    then = [functools.partial(pass_on, i, k) for i in range(n) for k in range(3)]
    shapes = [jax.ShapeDtypeStruct((N_DEV,) + p.shape[1:], p.dtype) for p in pieces]
    return pieces, shapes, plan, (), then


def kernel(x, c, w_ada, b_ada, norm1_w, w_in, gdn_conv_w, gdn_a_log, gdn_dt_bias, gdn_norm_w, hgrn_lb_param, hgrn_norm_w, ssd_conv_w, ssd_conv_b, ssd_a_log, ssd_dt_bias, ssd_d, ssd_norm_w, w_br_a, w_br_b, w_br_c, w_out, norm2_w, ffn_w_up, ffn_conv_w, ffn_conv_b, ffn_w_down, final_norm_w, loss_target, m_w_ada, m_b_ada, m_norm1_w, m_w_in, m_gdn_conv_w, m_gdn_a_log, m_gdn_dt_bias, m_gdn_norm_w, m_hgrn_lb_param, m_hgrn_norm_w, m_ssd_conv_w, m_ssd_conv_b, m_ssd_a_log, m_ssd_dt_bias, m_ssd_d, m_ssd_norm_w, m_w_br_a, m_w_br_b, m_w_br_c, m_w_out, m_norm2_w, m_ffn_w_up, m_ffn_conv_w, m_ffn_conv_b, m_ffn_w_down, m_final_norm_w, v_w_ada, v_b_ada, v_norm1_w, v_w_in, v_gdn_conv_w, v_gdn_a_log, v_gdn_dt_bias, v_gdn_norm_w, v_hgrn_lb_param, v_hgrn_norm_w, v_ssd_conv_w, v_ssd_conv_b, v_ssd_a_log, v_ssd_dt_bias, v_ssd_d, v_ssd_norm_w, v_w_br_a, v_w_br_b, v_w_br_c, v_w_out, v_norm2_w, v_ffn_w_up, v_ffn_conv_w, v_ffn_conv_b, v_ffn_w_down, v_final_norm_w):
    loc = dict(locals())
    w = {k: loc[k] for k in _WEIGHTS}
    mom = {k: loc["m_" + k] for k in _WEIGHTS}
    var = {k: loc["v_" + k] for k in _WEIGHTS}
    nb, s, d = x.shape
    t = nb * s
    depth = w_ada.shape[0]
    chip = 2 * lax.axis_index("x") + lax.axis_index("y")
    dev = 2 * chip + lax.axis_index("c")
    x0 = x.reshape(t, d)
    target = loss_target.reshape(t, d)

    small_in = [c, gdn_conv_w.reshape(depth * 4, -1), ssd_conv_w.reshape(depth * 4, -1),
                ffn_conv_w.reshape(depth * 3, -1)]
    c_all, gcw, scw, fcw = allgather8("ag_small", small_in, [False] * 4)
    c_all = c_all.reshape(N_DEV * nb, d)

    def conv_full(g, taps):
        g = g[::2].reshape(N_CHIPS, depth, taps, -1)
        return g.transpose(1, 2, 0, 3).reshape(depth, taps, -1)

    gdn_cw, ssd_cw, ffn_cw = conv_full(gcw, 4), conv_full(scw, 4), conv_full(fcw, 3)

    axis_of = dict(_BIG)
    first_needed, later = ("w_in",), tuple(n for n, _ in _BIG if n != "w_in")
    wls = [dict() for _ in range(depth)]

    def pieces(keys):
        out = []
        for l, name in keys:
            a = w[name][l].astype(BF16)
            out.append(a.reshape(2, a.shape[0] // 2, a.shape[1]))
        return out

    def arrived(keys, bufs):
        for (l, name), g in zip(keys, bufs):
            if name == "w_in":
                wls[l]["w_g"], wls[l]["w_a"], wls[l]["w_b"], wls[l]["w_c"] = _split_w_in(g)
            else:
                wls[l][name] = _unstack_gathered(g, 1, axis_of[name])[0]

    keys0 = [(0, n) for n in first_needed]
    arrived(keys0, allgather8("ag_weights0", pieces(keys0), [True] * len(keys0)))

    (c_act,) = elementwise("silu_c", lambda v: (_silu(v),), [c_all], [F32])
    mod_cols = jnp.concatenate([matmul("ada_fwd%d" % l, c_act, w_ada[l], "nn") for l in range(depth)], axis=0)
    (mod8,) = allgather8("ag_mod", [mod_cols], [False])
    mod = mod8[::2].reshape(N_CHIPS, depth, N_DEV * nb, -1).transpose(1, 2, 0, 3).reshape(depth, N_DEV * nb, 6 * d)
    mod = lax.dynamic_slice_in_dim(mod, dev * nb, nb, axis=1) + b_ada[:, None, :]

    def mod_part(l, k):
        return mod[l, :, k * d:(k + 1) * d].reshape(nb, 1, d)

    saved = []
    xl = x0
    for l in range(depth):
        sfx = str(l)
        wl = wls[l]
        sv = {"x0": xl}
        shift1, scale1, gate1, shift2, scale2, gate2 = [mod_part(l, k) for k in range(6)]
        sv["mods"] = (shift1, scale1, gate1, shift2, scale2, gate2)
        h, h_t = normmod_fwd("norm1_fwd" + sfx, xl, norm1_w[l][None], shift1, scale1, nb, s)
        pg = matmul("proj_g" + sfx, h, wl["w_g"], "nn")
        pa = matmul("proj_a" + sfx, h, wl["w_a"], "nn")
        pb = matmul("proj_b" + sfx, h, wl["w_b"], "nn")
        pc = matmul("proj_c" + sfx, h, wl["w_c"], "nn")
        gdn_p = [_rows8(list(gdn_cw[l]), 1536), _rows8([gdn_a_log[l], gdn_dt_bias[l], gdn_norm_w[l]], 128)]
        hgrn_p = [_rows8(list(hgrn_lb_param), 512), _rows8([hgrn_norm_w[l]], 128)]
        ssd_p = [_rows8(list(ssd_cw[l]), 1024), _rows8([ssd_conv_b[l], ssd_norm_w[l]], 1024),
                 _rows8([ssd_a_log[l], ssd_dt_bias[l], ssd_d[l]], 128)]
        ffn_p = [_rows8(list(ffn_cw[l]) + [ffn_conv_b[l]], 2 * FFN_HIDDEN)]
        hgrn_fn = make_hgrn_tile(l, depth)
        keys = [(l, n) for n in later] + ([(l + 1, n) for n in first_needed] if l + 1 < depth else [])
        oa, st_a, *bufs = seq_fwd("gdn_fwd" + sfx, gdn_tile, gdn_p, [pa], True, [(512, BF16)], (512, 128), nb, s,
                                  _R_GDN, side=_gather_side(pieces(keys)))
        arrived(keys, bufs)
        ob, st_b = seq_fwd("hgrn_fwd" + sfx, hgrn_fn, hgrn_p, [pb], False, [(512, BF16)], (512, 128), nb, s, _R_HGRN)
        oc, st_c = seq_fwd("ssd_fwd" + sfx, ssd_tile, ssd_p, [pc], True, [(512, BF16)], (256, 256), nb, s, _R_SSD)
        x1 = merge_fwd("merge_fwd" + sfx, xl, oa, ob, oc, pg, gate1, wl["w_br_a"], wl["w_br_b"], wl["w_br_c"],
                       wl["w_out"], nb, s)
        h2, h2_t = normmod_fwd("norm2_fwd" + sfx, x1, norm2_w[l][None], shift2, scale2, nb, s)
        u = matmul("ffn_up" + sfx, h2, wl["ffn_w_up"], "nn", tn=FFN_HIDDEN)
        act, act_t = seq_fwd("convglu_fwd" + sfx, convglu_tile_t, ffn_p, [u], True,
                             [(FFN_HIDDEN, BF16), (FFN_HIDDEN, BF16, "T")], None, nb, s, _R_FFN)
        xl, f = matmul("ffn_down" + sfx, act, wl["ffn_w_down"], "nn", resid=(x1, gate2, s))
        sv.update(h_t=h_t, h2_t=h2_t, act_t=act_t, pg=pg, pa=pa, pb=pb, pc=pc, oa=oa, ob=ob, oc=oc, st_a=st_a, st_b=st_b, st_c=st_c, x1=x1,
                  u=u, f=f, gdn_p=gdn_p, hgrn_p=hgrn_p, ssd_p=ssd_p, ffn_p=ffn_p, hgrn_fn=hgrn_fn)
        saved.append(sv)

    dx, loss_part, d_final = loss_head("loss_head", xl, final_norm_w[None], target)

    sg = {}
    dmod = [None] * depth
    d_lb = None
    reduced = [None] * depth
    to_d2d = to_ici = None
    for l in reversed(range(depth)):
        sfx = str(l)
        sv, wl = saved[l], wls[l]
        gfull = {}
        shift1, scale1, gate1, shift2, scale2, gate2 = sv["mods"]
        df, dgate2 = resid_bwd("resid_bwd" + sfx, dx, sv["f"], gate2, nb, s)
        dact = matmul("ffn_down_dx" + sfx, df, wl["ffn_w_down"], "nt")
        gfull["ffn_w_down"] = matmul("ffn_down_dw" + sfx, sv["act_t"], df, "nn", tm=1408, tn=512, tk=4096)
        cg_args = ("convglu_bwd" + sfx, convglu_tile, sv["ffn_p"], [sv["u"]], True, None, [dact], [BF16], None, nb, s,
                   _R_FFN)
        if to_d2d is None:
            (du,), (dcw,) = seq_bwd(*cg_args)
        else:
            lp, stacked = to_d2d
            (du,), (dcw,), recv = seq_bwd(*cg_args, side=_rs_d2d(stacked))
            to_ici = (lp,) + _rs_ici(stacked, recv, str(lp))
        dh2 = matmul("ffn_up_dx" + sfx, du, wl["ffn_w_up"], "nt")
        gfull["ffn_w_up"] = matmul("ffn_up_dw" + sfx, sv["h2_t"], du, "nn", tm=1024, tn=512, tk=4096)
        dx1, dnw2, dshift2, dscale2 = normmod_bwd("norm2_bwd" + sfx, sv["x1"], norm2_w[l][None], shift2, scale2, dh2, dx,
                                                  nb, s)
        doa, dob, doc, dpg, dgate1, dwa, dwb, dwc, dwo = merge_bwd(
            "merge_bwd" + sfx, sv["oa"], sv["ob"], sv["oc"], sv["pg"], gate1, wl["w_br_a"], wl["w_br_b"],
            wl["w_br_c"], wl["w_out"], dx1, nb, s)
        gfull["w_br_a"], gfull["w_br_b"], gfull["w_br_c"], gfull["w_out"] = dwa, dwb, dwc, dwo
        gdn_args = ("gdn_bwd" + sfx, gdn_tile, sv["gdn_p"], [sv["pa"]], True, sv["st_a"], [doa], [F32], (512, 128),
                    nb, s, _R_GDN)
        if to_ici is None:
            (dpa,), (dgcw, dgpk) = seq_bwd(*gdn_args)
        else:
            lp, qf, side = to_ici
            (dpa,), (dgcw, dgpk), res = seq_bwd(*gdn_args, side=side)
            reduced[lp] = _rs_finish(qf, res, str(lp))
        (dpb,), (dlbp, dhnw) = seq_bwd("hgrn_bwd" + sfx, sv["hgrn_fn"], sv["hgrn_p"], [sv["pb"]], False, sv["st_b"],
                                       [dob], [F32], (512, 128), nb, s, _R_HGRN)
        (dpc,), (dscw, dspv, dsps) = seq_bwd("ssd_bwd" + sfx, ssd_tile, sv["ssd_p"], [sv["pc"]], True, sv["st_c"],
                                             [doc], [F32], (256, 256), nb, s, _R_SSD)
        dh = matmul("proj_g_dx" + sfx, dpg, wl["w_g"], "nt")
        dh = matmul("proj_a_dx" + sfx, dpa, wl["w_a"], "nt", addend=dh)
        dh = matmul("proj_b_dx" + sfx, dpb, wl["w_b"], "nt", addend=dh)
        dh = matmul("proj_c_dx" + sfx, dpc, wl["w_c"], "nt", addend=dh)
        stacked_w_in = _stack_w_in(
            matmul("proj_g_dw" + sfx, sv["h_t"], dpg, "nn", tm=1024, tn=512, tk=4096),
            matmul("proj_a_dw" + sfx, sv["h_t"], dpa, "nn", tm=1024, tk=1024),
            matmul("proj_b_dw" + sfx, sv["h_t"], dpb, "nn", tm=1024, tn=512, tk=4096),
            matmul("proj_c_dw" + sfx, sv["h_t"], dpc, "nn", tm=1024, tk=1024))
        dx, dnw1, dshift1, dscale1 = normmod_bwd("norm1_bwd" + sfx, sv["x0"], norm1_w[l][None], shift1, scale1, dh, dx1,
                                                 nb, s)
        dmod[l] = jnp.concatenate([dshift1, dscale1, dgate1, dshift2, dscale2, dgate2], axis=-1).reshape(nb, 6 * d)
        d_lb = dlbp[:depth] if d_lb is None else d_lb + dlbp[:depth]
        sg[l] = dict(norm1_w=dnw1[0], norm2_w=dnw2[0], gdn_conv_w=dgcw[:4], gdn_a_log=dgpk[0, :4],
                     gdn_dt_bias=dgpk[1, :4], gdn_norm_w=dgpk[2], hgrn_norm_w=dhnw[0], ssd_conv_w=dscw[:4],
                     ssd_conv_b=dspv[0], ssd_norm_w=dspv[1, :512], ssd_a_log=dsps[0, :8], ssd_dt_bias=dsps[1, :8],
                     ssd_d=dsps[2, :8], ffn_conv_w=dcw[:3], ffn_conv_b=dcw[3])
        to_d2d = (l, [stacked_w_in if name == "w_in" else _stack_by_chip(gfull[name][None], axis)
                      for name, axis in _BIG])
    lp, stacked = to_d2d
    recv = exchange("rs_d2d%d" % lp, *_rs_d2d(stacked)[:4])
    qf, side = _rs_ici(stacked, recv, str(lp))
    reduced[lp] = _rs_finish(qf, exchange("rs_ici%d" % lp, *side[:4]), str(lp))
    grad_x = dx.reshape(nb, s, d)

    dmod = jnp.stack(dmod)
    (b_sum,) = elementwise("bias_rows", lambda *r: (functools.reduce(lambda p, q: p + q, r),),
                           [dmod[:, b].reshape(depth * 6, d) for b in range(nb)], [F32])
    per_layer = ("norm1_w", "norm2_w", "gdn_conv_w", "gdn_a_log", "gdn_dt_bias", "gdn_norm_w", "hgrn_norm_w",
                 "ssd_conv_w", "ssd_conv_b", "ssd_norm_w", "ssd_a_log", "ssd_dt_bias", "ssd_d", "ffn_conv_w", "ffn_conv_b")
    vals = {k: jnp.stack([sg[l][k] for l in range(depth)]) for k in per_layer}
    vals.update(loss=loss_part[0, :1], b_ada=b_sum.reshape(depth, 6 * d), hgrn_lb_param=d_lb, final_norm_w=d_final[0])
    gp = _Packer()
    for k, v in vals.items():
        gp.add(k, v.shape)
    packed8, dmod8 = allgather8("ag_grads", [gp.pack(vals), dmod.reshape(depth * nb, 6 * d)], [False, False])
    gs = gp.unpack(sum8("sum_small", packed8))
    loss = gs["loss"].reshape(())

    def my_cols(g):
        cs = g.shape[-1] // N_CHIPS
        return lax.dynamic_slice_in_dim(g, chip * cs, cs, axis=g.ndim - 1)

    for k in ("gdn_conv_w", "ssd_conv_w", "ffn_conv_w"):
        gs[k] = my_cols(gs[k])

    dmod_all = dmod8.reshape(N_DEV, depth, nb, 6 * d).transpose(1, 0, 2, 3).reshape(depth, N_DEV * nb, 6 * d)
    dmod_mine = lax.dynamic_slice_in_dim(dmod_all, chip * (6 * d // N_CHIPS), 6 * d // N_CHIPS, axis=2)
    g_w_ada = jnp.stack([matmul("ada_dw%d" % l, c_act, dmod_mine[l], "tn", tm=1024) for l in range(depth)])

    c_arr = lax.axis_index("c").astype(jnp.int32).reshape(1)
    grads, delta, new_m, new_v = {}, {}, {}, {}
    for i, (name, _) in enumerate(_BIG):
        shp = w[name].shape
        halves = lambda a: a.reshape((depth, 2) + reduced[0][0][i].shape)
        res = None
        for l in reversed(range(depth)):
            res = adamw_halves("adamw_%s%d" % (name, l), halves(w[name]), halves(mom[name]), halves(var[name]), l,
                               reduced[l][0][i], reduced[l][1][i], c_arr, prev=res)
        grads[name], delta[name], new_m[name], new_v[name] = [r.reshape(shp) for r in res]
    grads["w_ada"] = g_w_ada
    for k in _SMALL:
        grads[k] = gs[k].reshape(w[k].shape)
    shp = w_ada.shape
    flat = lambda a: a.reshape(shp[0] * shp[1], shp[2])
    dl, nm, nv = adamw("adamw_w_ada", flat(w_ada), flat(g_w_ada), flat(m_w_ada), flat(v_w_ada))
    delta["w_ada"], new_m["w_ada"], new_v["w_ada"] = dl.reshape(shp), nm.reshape(shp), nv.reshape(shp)
    sp = _Packer()
    for k in _SMALL:
        sp.add(k, w[k].shape)
    dl, nm, nv = adamw("adamw_small", sp.pack(w), sp.pack(grads), sp.pack(mom), sp.pack(var))
    delta.update(sp.unpack(dl))
    new_m.update(sp.unpack(nm))
    new_v.update(sp.unpack(nv))

    return (loss, grad_x, *[grads[k] for k in _WEIGHTS], *[delta[k] for k in _WEIGHTS],
            *[new_m[k] for k in _WEIGHTS], *[new_v[k] for k in _WEIGHTS])
```

```python
import functools

import jax
import jax.numpy as jnp
from jax import lax
from jax.experimental import pallas as pl
from jax.experimental.pallas import tpu as pltpu

F32 = jnp.float32
BF16 = jnp.bfloat16
HI = lax.Precision.HIGHEST
MESH = pl.DeviceIdType.MESH

EPS = 1e-6
D_MODEL = 1024
GDN_HEADS, GDN_DK, GDN_CHUNK = 4, 128, 64
HGRN_HEADS, HGRN_DK, HGRN_CHUNK = 4, 128, 16
SSD_HEADS, SSD_P, SSD_GROUPS, SSD_STATE, SSD_CHUNK = 8, 64, 2, 128, 64
FFN_HIDDEN = 2816
N_CHIPS = 4
N_DEV = 8

ADAM_LR, ADAM_B1, ADAM_B2, ADAM_EPS, ADAM_WD, ADAM_STEP = 0.001, 0.9, 0.999, 1e-08, 0.01, 10

W_G, W_A, W_B, W_C = 3072, 2176, 2048, 1664
VMEM_LIMIT = 56 * 1024 * 1024


def _cparams(sem):
    return pltpu.CompilerParams(dimension_semantics=sem, vmem_limit_bytes=VMEM_LIMIT)


def _dg(a, b, ca, cb):
    return lax.dot_general(a.astype(BF16), b.astype(BF16), (((ca,), (cb,)), ((), ())),
                           preferred_element_type=F32)


@jax.custom_vjp
def bdot(a, b):
    return _dg(a, b, 1, 0)


bdot.defvjp(lambda a, b: (_dg(a, b, 1, 0), (a, b)),
            lambda r, g: (_dg(g, r[1], 1, 1), _dg(r[0], g, 0, 0)))


@jax.custom_vjp
def bdot_nt(a, b):
    return _dg(a, b, 1, 1)


bdot_nt.defvjp(lambda a, b: (_dg(a, b, 1, 1), (a, b)),
               lambda r, g: (_dg(g, r[1], 1, 0), _dg(g, r[0], 0, 0)))


@jax.custom_vjp
def bdot_tn(a, b):
    return _dg(a, b, 0, 0)


bdot_tn.defvjp(lambda a, b: (_dg(a, b, 0, 0), (a, b)),
               lambda r, g: (_dg(r[1], g, 1, 1), _dg(r[0], g, 1, 0)))


def _split(x, n):
    parts, rest = [], x
    for _ in range(n):
        p = rest.astype(BF16)
        parts.append(p)
        rest = rest - p.astype(F32)
    return parts


def _dgb(a, b, ca, cb):
    return lax.dot_general(a, b, (((ca,), (cb,)), ((), ())), preferred_element_type=F32)


def _dg3(a, b, ca, cb):
    (ah, al), (bh, bl) = _split(a, 2), _split(b, 2)
    return _dgb(jnp.concatenate([ah, ah, al], axis=ca), jnp.concatenate([bh, bl, bh], axis=cb), ca, cb)


@jax.custom_vjp
def hdot(a, b):
    return _dg3(a, b, 1, 0)


hdot.defvjp(lambda a, b: (_dg3(a, b, 1, 0), (a, b)),
            lambda r, g: (_dg3(g, r[1], 1, 1), _dg3(r[0], g, 0, 0)))


def _dge(e, x, ce, cx, e_first):
    eb = e.astype(BF16)
    es = jnp.concatenate([eb, eb, eb], axis=ce)
    xs = jnp.concatenate(_split(x, 3), axis=cx)
    return _dgb(es, xs, ce, cx) if e_first else _dgb(xs, es, cx, ce)


@jax.custom_vjp
def ldot(e, x):
    return _dge(e, x, 1, 0, True)


ldot.defvjp(lambda e, x: (_dge(e, x, 1, 0, True), e),
            lambda e, g: (jnp.zeros_like(e), _dge(e, g, 0, 0, True)))


@jax.custom_vjp
def rdot(x, e):
    return _dge(e, x, 0, 1, False)


rdot.defvjp(lambda x, e: (_dge(e, x, 0, 1, False), e),
            lambda e, g: (_dge(e, g, 1, 1, False), jnp.zeros_like(e)))


def _sigmoid(x):
    return 1.0 / (1.0 + jnp.exp(-x))


def _silu(x):
    return x * _sigmoid(x)


def _softplus(x):
    return jnp.maximum(x, 0.0) + jnp.log(1.0 + jnp.exp(-jnp.abs(x)))


def _rms(x, w):
    return x * lax.rsqrt(jnp.mean(x * x, axis=-1, keepdims=True) + EPS) * w


def _iota(shape, dim):
    return lax.broadcasted_iota(jnp.int32, shape, dim)


def _tri_ones(n, chunk, kind):
    i, j = _iota((n, n), 0), _iota((n, n), 1)
    same = lax.div(i, chunk) == lax.div(j, chunk)
    if kind == "incl":
        m = same & (j <= i)
    elif kind == "strict":
        m = same & (j < i)
    elif kind == "all":
        m = same
    else:
        m = same & (lax.rem(j, chunk) < (chunk // 2))
    return m


def _causal_conv(w, halo, x, width):
    r = x.shape[0]
    xin = jnp.concatenate([halo, x], axis=0)
    y = w[width - 1:width, :] * x
    for k in range(width - 1):
        off = 8 - (width - 1) + k
        y = y + w[k:k + 1, :] * xin[off:off + r, :]
    return y


def _each(fn, *lists):
    return [fn(*a) for a in zip(*lists)]


def _neumann(ms):
    n = ms[0].shape[0]
    eye = (_iota((n, n), 0) == _iota((n, n), 1)).astype(F32)
    accs = [eye - m for m in ms]
    ps = ms
    steps = 1
    while steps * 2 < n:
        ps = _each(hdot, ps, ps)
        accs = [acc + ap for acc, ap in zip(accs, _each(hdot, accs, ps))]
        steps *= 2
    return accs


@jax.custom_vjp
def tri_inverse(ms):
    return _neumann(ms)


def _tri_inverse_fwd(ms):
    ainvs = _neumann(ms)
    return ainvs, ainvs


def _tri_inverse_bwd(ainvs, gs):
    t = _each(lambda g, a: _dg3(g, a, 1, 1), gs, ainvs)
    return ([-x for x in _each(lambda a, y: _dg3(a, y, 0, 0), ainvs, t)],)


tri_inverse.defvjp(_tri_inverse_fwd, _tri_inverse_bwd)


def gdn_tile(params, state, ins, halos):
    conv_w, pk = params
    (pa,), (ha,) = ins, halos
    r = pa.shape[0]
    c, nh, dk = GDN_CHUNK, GDN_HEADS, GDN_DK
    kw = nh * dk
    qkv = _silu(_causal_conv(conv_w, ha[:, :3 * kw], pa[:, :3 * kw], 4))
    z = pa[:, 3 * kw:4 * kw]
    gsm = pa[:, 4 * kw:]
    a_log, dtb, nw = pk[0:1, :], pk[1:2, :], pk[2:3, :]
    g_all = -jnp.exp(a_log) * _softplus(gsm + dtb)
    beta_all = _sigmoid(gsm)
    incl = _tri_ones(c, c, "incl")
    strict = _tri_ones(c, c, "strict")
    lmat = incl.astype(F32)
    scale = dk ** -0.5
    nck = r // c
    inst = [(ci, h) for ci in range(nck) for h in range(nh)]

    def l2n(v):
        return v * lax.rsqrt(jnp.sum(v * v, axis=-1, keepdims=True) + EPS)

    gcs = [ldot(lmat, g_all[ci * c:(ci + 1) * c, :]) for ci in range(nck)]
    gcts = [g.T for g in gcs]
    g_col = [gcs[ci][:, h:h + 1] for ci, h in inst]
    g_row = [gcts[ci][h:h + 1, :] for ci, h in inst]
    g_last = [gcs[ci][c - 1:c, h:h + 1] for ci, h in inst]
    beta = [beta_all[ci * c:(ci + 1) * c, nh + h:nh + h + 1] for ci, h in inst]
    qh = [l2n(qkv[ci * c:(ci + 1) * c, h * dk:(h + 1) * dk]) for ci, h in inst]
    kh = [l2n(qkv[ci * c:(ci + 1) * c, kw + h * dk:kw + (h + 1) * dk]) for ci, h in inst]
    vh = [qkv[ci * c:(ci + 1) * c, 2 * kw + h * dk:2 * kw + (h + 1) * dk] for ci, h in inst]
    decay = [jnp.where(incl, jnp.exp(jnp.where(incl, gc_ - gr_, 0.0)), 0.0) for gc_, gr_ in zip(g_col, g_row)]
    kb = [k * b for k, b in zip(kh, beta)]
    qs = [q * scale for q in qh]
    kk = _each(lambda a, b, k: bdot_nt(jnp.concatenate([a, b], axis=0), k), kb, qs, kh)
    ms = [jnp.where(strict, x[:c] * d, 0.0) for x, d in zip(kk, decay)]
    attn = [x[c:] * d for x, d in zip(kk, decay)]
    ainv = tri_inverse(ms)
    eg = [jnp.exp(g) for g in g_col]
    rhs = [jnp.concatenate([v * b, k_ * e], axis=1) for v, b, k_, e in zip(vh, beta, kb, eg)]
    sol = _each(hdot, ainv, rhs)
    qg = [q * e for q, e in zip(qs, eg)]
    k_end = [k * jnp.exp(gl - g) for k, gl, g in zip(kh, g_last, g_col)]
    e_last = [jnp.exp(gl) for gl in g_last]

    st = [state[h * dk:(h + 1) * dk, :] for h in range(nh)]
    outs = [[] for _ in range(nh)]
    for ci in range(nck):
        idx = [ci * nh + h for h in range(nh)]
        ws = [bdot(jnp.concatenate([sol[i][:, dk:], qg[i]], axis=0), st[h]) for h, i in enumerate(idx)]
        v_new = [sol[i][:, :dk] - w_[:c] for i, w_ in zip(idx, ws)]
        av = [bdot(attn[i], v) for i, v in zip(idx, v_new)]
        kv = [bdot_tn(k_end[i], v) for i, v in zip(idx, v_new)]
        for h, i in enumerate(idx):
            o = ws[h][c:] + av[h]
            st[h] = st[h] * e_last[i] + kv[h]
            outs[h].append(_rms(o, nw) * _silu(z[ci * c:(ci + 1) * c, h * dk:(h + 1) * dk]))
    out = jnp.concatenate([jnp.concatenate(o, axis=0) for o in outs], axis=1)
    return jnp.concatenate(st, axis=0), [out]


def make_hgrn_tile(layer, depth):
    def hgrn_tile(params, state, ins, halos):
        lbp, nwp = params
        (pb,) = ins
        r = pb.shape[0]
        c = HGRN_CHUNK
        kw = HGRN_HEADS * HGRN_DK
        rows = [lbp[i:i + 1, :] for i in range(depth)]
        mx = functools.reduce(jnp.maximum, rows)
        ex = [jnp.exp(x - mx) for x in rows]
        den = functools.reduce(lambda a, b: a + b, ex)
        soft = [e / den for e in ex]
        lb = functools.reduce(lambda a, b: a + b, soft[:layer + 1]) - soft[0]
        nw = nwp[0:1, :]
        q = _silu(pb[:, :kw])
        fr = pb[:, kw:2 * kw]
        logf = jnp.log(lb + (1.0 - lb) * _sigmoid(fr))
        k = (1.0 - lb) * _sigmoid(-fr)
        v = pb[:, 2 * kw:3 * kw]
        gate = pb[:, 3 * kw:]
        incl = _tri_ones(r, c, "incl")
        masks = jnp.concatenate([incl.astype(F32), _tri_ones(r, c, "upto").astype(F32),
                                 _tri_ones(r, c, "all").astype(F32)], axis=0)
        sums = ldot(masks, logf)
        g_cum, g_ref, g_end = sums[:r], sums[r:2 * r], sums[2 * r:]
        qs = q * jnp.exp(g_cum - g_ref)
        ks = k * jnp.exp(g_ref - g_cum)
        qg = q * jnp.exp(g_cum)
        k_end = k * jnp.exp(g_end - g_cum)
        e_end = jnp.exp(g_end)
        sls = [slice(h * HGRN_DK, (h + 1) * HGRN_DK) for h in range(HGRN_HEADS)]
        attn = [jnp.where(incl, bdot_nt(qs[:, sl], ks[:, sl]), 0.0) for sl in sls]
        o_intra = [bdot(a, v[:, sl]) for a, sl in zip(attn, sls)]
        nsub, dk = r // c, HGRN_DK
        own_block = lax.div(_iota((r, nsub * dk), 0), c) == lax.div(_iota((r, nsub * dk), 1), dk)

        def spread(a):
            return jnp.where(own_block, jnp.concatenate([a] * nsub, axis=1), 0.0)

        kv = [bdot_tn(v[:, sl], spread(k_end[:, sl])) for sl in sls]
        s_t = [state[sl, :] for sl in sls]
        entry = [[] for _ in sls]
        for j in range(nsub):
            for lst, s_h in zip(entry, s_t):
                lst.append(s_h)
            s_t = [s_h * e_end[j * c:j * c + 1, sl] + x[:, j * dk:(j + 1) * dk] for s_h, sl, x in zip(s_t, sls, kv)]
        o_inter = [bdot_nt(spread(qg[:, sl]), jnp.concatenate(e, axis=1)) for sl, e in zip(sls, entry)]
        outs = [_rms(oa + ob, nw) * _silu(gate[:, sl]) for oa, ob, sl in zip(o_intra, o_inter, sls)]
        return jnp.concatenate(s_t, axis=0), [jnp.concatenate(outs, axis=1)]
    return hgrn_tile


def ssd_tile(params, state, ins, halos):
    conv_w, pv, ps = params
    (pc,), (hc,) = ins, halos
    r = pc.shape[0]
    c = SSD_CHUNK
    inner = SSD_HEADS * SSD_P
    gw = inner // SSD_GROUPS
    z = pc[:, :inner]
    xbc = _silu(_causal_conv(conv_w, hc[:, inner:inner + 1024], pc[:, inner:inner + 1024], 4) + pv[0:1, :])
    ssm = pc[:, inner + 1024:]
    xs = xbc[:, :inner]
    bm = xbc[:, inner:inner + SSD_GROUPS * SSD_STATE]
    cm = xbc[:, inner + SSD_GROUPS * SSD_STATE:]
    a_log, dtb, dsk = ps[0:1, :], ps[1:2, :], ps[2:3, :]
    nw = pv[1:2, :inner]
    dt = _softplus(ssm + dtb)
    da = dt * (-jnp.exp(a_log))
    expand = (lax.div(_iota((128, inner), 1), SSD_P) == _iota((128, inner), 0)).astype(F32)
    xdt = xs * rdot(dt, expand)
    d_e = rdot(jnp.concatenate([dsk] * 8, axis=0), expand)[0:1, :]
    incl = _tri_ones(c, c, "incl")
    lmat = incl.astype(F32)
    st = [state[g * SSD_STATE:(g + 1) * SSD_STATE, :] for g in range(SSD_GROUPS)]
    hpg = SSD_HEADS // SSD_GROUPS
    nck = r // c
    groups = range(SSD_GROUPS)
    cg = [(ci, g) for ci in range(nck) for g in groups]
    rows = [slice(ci * c, (ci + 1) * c) for ci in range(nck)]
    gls = [slice(g * gw, (g + 1) * gw) for g in groups]
    acs = [ldot(lmat, da[rs, :]) for rs in rows]
    acs_t = [a.T for a in acs]
    acs_e = [rdot(a, expand) for a in acs]
    last_e = [a[c - 1:c, :] for a in acs_e]
    bm_g = [bm[rows[ci], g * SSD_STATE:(g + 1) * SSD_STATE] for ci, g in cg]
    cm_g = [cm[rows[ci], g * SSD_STATE:(g + 1) * SSD_STATE] for ci, g in cg]
    cb = _each(bdot_nt, cm_g, bm_g)
    heads = [(i, ci, g * hpg + hg) for i, (ci, g) in enumerate(cg) for hg in range(hpg)]
    seg = [jnp.where(incl, jnp.exp(jnp.where(incl, acs[ci][:, hh:hh + 1] - acs_t[ci][hh:hh + 1, :], 0.0)), 0.0)
           for _, ci, hh in heads]
    yd = [bdot(cb[i] * sg, xdt[rows[ci], hh * SSD_P:(hh + 1) * SSD_P]) for (i, ci, hh), sg in zip(heads, seg)]
    y_diag = [jnp.concatenate(yd[i * hpg:(i + 1) * hpg], axis=1) for i in range(len(cg))]
    xw = [xdt[rows[ci], gls[g]] * jnp.exp(last_e[ci][:, gls[g]] - acs_e[ci][:, gls[g]]) for ci, g in cg]
    e_acs = [jnp.exp(acs_e[ci][:, gls[g]]) for ci, g in cg]
    e_last = [jnp.exp(last_e[ci][:, gls[g]]) for ci, g in cg]
    kv = _each(bdot_tn, bm_g, xw)
    ys = []
    for ci in range(nck):
        idx = [ci * SSD_GROUPS + g for g in groups]
        y_off = [bdot(cm_g[i], st[g]) * e_acs[i] for g, i in zip(groups, idx)]
        st = [st[g] * e_last[i] + kv[i] for g, i in zip(groups, idx)]
        ys.append(jnp.concatenate([y_diag[i] + yo for i, yo in zip(idx, y_off)], axis=1))
    y = jnp.concatenate(ys, axis=0) + d_e * xs
    yz = y * _silu(z)
    out = jnp.concatenate([_rms(yz[:, g * gw:(g + 1) * gw], nw[:, g * gw:(g + 1) * gw])
                           for g in range(SSD_GROUPS)], axis=1)
    return jnp.concatenate(st, axis=0), [out]


def convglu_tile(params, state, ins, halos):
    (cw,) = params
    (u,), (hu,) = ins, halos
    y = _causal_conv(cw, hu, u, 3) + cw[3:4, :]
    return None, [_silu(y[:, :FFN_HIDDEN]) * y[:, FFN_HIDDEN:]]


def convglu_tile_t(params, state, ins, halos):
    _, (act,) = convglu_tile(params, state, ins, halos)
    return None, [act, act.T]


def _halo_map(nt, r):
    return lambda b, n: (jnp.maximum((b * nt + n) * (r // 8) - 1, 0), 0)


def _exchange_copies(plan, local_plan, in_refs, out_refs, send_sems, recv_sems, local_sems):
    x, y, c = lax.axis_index("x"), lax.axis_index("y"), lax.axis_index("c")
    copies = []
    for k, fn in enumerate(plan):
        src, dst, peer = fn(in_refs, out_refs, x, y, c)
        copies.append(pltpu.make_async_remote_copy(src_ref=src, dst_ref=dst, send_sem=send_sems.at[k],
                                                   recv_sem=recv_sems.at[k], device_id=peer, device_id_type=MESH))
    for k, fn in enumerate(local_plan):
        src, dst = fn(in_refs, out_refs, x, y, c)
        copies.append(pltpu.make_async_copy(src, dst, local_sems.at[k]))
    return copies


def _exchange_sems(plan, local_plan):
    return [pltpu.SemaphoreType.DMA((max(len(plan), 1),)), pltpu.SemaphoreType.DMA((max(len(plan), 1),)),
            pltpu.SemaphoreType.DMA((max(len(local_plan), 1),))]


def _host_exchange(side, body, in_specs, o_specs, out_shape, scratch, args, grid):
    s_ins, s_shapes, plan, local_plan, then = side
    n_in, n_out, n_scr = len(in_specs), len(o_specs), len(scratch)
    k_in, k_out = len(s_ins), len(s_shapes)
    any_spec = pl.BlockSpec(memory_space=pl.ANY)

    def hosted(*refs):
        own_in, s_in = refs[:n_in], refs[n_in:n_in + k_in]
        o0 = n_in + k_in
        own_out, s_out = refs[o0:o0 + n_out], refs[o0 + n_out:o0 + n_out + k_out]
        rest = refs[o0 + n_out + k_out:]
        own_scr, sems, sems_then = rest[:n_scr], rest[n_scr:n_scr + 3], rest[n_scr + 3:]
        ids = [pl.program_id(a) for a in range(len(grid))]
        first = functools.reduce(lambda p, q: p & q, [i == 0 for i in ids])
        last = functools.reduce(lambda p, q: p & q, [i == g - 1 for i, g in zip(ids, grid)])

        @pl.when(first)
        def _():
            for cp in _exchange_copies(plan, local_plan, s_in, s_out, *sems):
                cp.start()

        body(*own_in, *own_out, *own_scr)

        @pl.when(last)
        def _():
            for cp in _exchange_copies(plan, local_plan, s_in, s_out, *sems):
                cp.wait()
            passed = _exchange_copies(then, (), s_in, s_out, *sems_then)
            for cp in passed:
                cp.start()
            for cp in passed:
                cp.wait()

    return (hosted, list(in_specs) + [any_spec] * k_in, list(o_specs) + [any_spec] * k_out,
            list(out_shape) + list(s_shapes),
            list(scratch) + _exchange_sems(plan, local_plan) + _exchange_sems(then, ()),
            list(args) + list(s_ins))


def seq_fwd(name, tile_fn, params, ins, use_halo, out_specs, state_shape, nb, s, r, side=None):
    nt = s // r
    n_p, n_i, n_o = len(params), len(ins), len(out_specs)
    has_state = state_shape is not None

    def body(*refs):
        p_refs, i_refs = refs[:n_p], refs[n_p:n_p + n_i]
        h_refs = refs[n_p + n_i:n_p + 2 * n_i] if use_halo else ()
        k = n_p + n_i + len(h_refs)
        o_refs = refs[k:k + n_o]
        n = pl.program_id(1)
        state = None
        if has_state:
            sv_ref, st_ref = refs[k + n_o], refs[k + n_o + 1]

            @pl.when(n == 0)
            def _():
                st_ref[...] = jnp.zeros(state_shape, F32)

            state = st_ref[...]
            sv_ref[0, 0] = state
        pv = [p[...] for p in p_refs]
        iv = [i[...].astype(F32) for i in i_refs]
        hv = [jnp.where(n > 0, h[...].astype(F32), 0.0) for h in h_refs]
        new_state, ov = tile_fn(pv, state, iv, hv)
        for o_ref, o in zip(o_refs, ov):
            o_ref[...] = o.astype(o_ref.dtype)
        if has_state:
            st_ref[...] = new_state

    row = lambda b, n: (b * nt + n, 0)
    in_specs = [pl.BlockSpec(p.shape, lambda b, n: (0, 0)) for p in params]
    in_specs += [pl.BlockSpec((r, a.shape[1]), row) for a in ins]
    if use_halo:
        in_specs += [pl.BlockSpec((8, a.shape[1]), _halo_map(nt, r)) for a in ins]
    col = lambda b, n: (0, b * nt + n)
    out_shape, o_specs = [], []
    for w, dt, *transposed in out_specs:
        out_shape.append(jax.ShapeDtypeStruct((w, nb * s) if transposed else (nb * s, w), dt))
        o_specs.append(pl.BlockSpec((w, r), col) if transposed else pl.BlockSpec((r, w), row))
    scratch = []
    if has_state:
        out_shape.append(jax.ShapeDtypeStruct((nb, nt) + tuple(state_shape), F32))
        o_specs.append(pl.BlockSpec((1, 1) + tuple(state_shape), lambda b, n: (b, n, 0, 0)))
        scratch.append(pltpu.VMEM(tuple(state_shape), F32))
    args = list(params) + list(ins) + (list(ins) if use_halo else [])
    if side is not None:
        body, in_specs, o_specs, out_shape, scratch, args = _host_exchange(
            side, body, in_specs, o_specs, out_shape, scratch, args, (nb, nt))
    return pl.pallas_call(body, grid=(nb, nt), in_specs=in_specs, out_specs=o_specs, out_shape=out_shape,
                          scratch_shapes=scratch, compiler_params=_cparams(("arbitrary", "arbitrary")),
                          name=name)(*args)


def seq_bwd(name, tile_fn, params, ins, use_halo, states, douts, din_dtypes, state_shape, nb, s, r, side=None):
    nt = s // r
    n_p, n_i, n_o = len(params), len(ins), len(douts)
    has_state = state_shape is not None

    def body(*refs):
        p_refs, i_refs = refs[:n_p], refs[n_p:n_p + n_i]
        h_refs = refs[n_p + n_i:n_p + 2 * n_i] if use_halo else ()
        k = n_p + n_i + len(h_refs)
        sv_ref = None
        if has_state:
            sv_ref = refs[k]
            k += 1
        do_refs = refs[k:k + n_o]
        k += n_o
        di_refs, dp_refs = refs[k:k + n_i], refs[k + n_i:k + n_i + n_p]
        k += n_i + n_p
        dst_ref = None
        if has_state:
            dst_ref = refs[k]
            k += 1
        dh_refs = refs[k:k + len(h_refs)]
        b, nn = pl.program_id(0), pl.program_id(1)
        n = nt - 1 - nn

        @pl.when((b == 0) & (nn == 0))
        def _():
            for dp in dp_refs:
                dp[...] = jnp.zeros(dp.shape, F32)

        @pl.when(nn == 0)
        def _():
            if has_state:
                dst_ref[...] = jnp.zeros(state_shape, F32)
            for dh in dh_refs:
                dh[...] = jnp.zeros(dh.shape, F32)

        pv = [p[...] for p in p_refs]
        iv = [i[...].astype(F32) for i in i_refs]
        hv = [jnp.where(n > 0, h[...].astype(F32), 0.0) for h in h_refs]
        if has_state:
            f = lambda pv_, st_, iv_, hv_: tile_fn(pv_, st_, iv_, hv_)
            _, vjp = jax.vjp(f, pv, sv_ref[0, 0], iv, hv)
            dpv, dst, div, dhv = vjp((dst_ref[...], [d[...].astype(F32) for d in do_refs]))
            dst_ref[...] = dst
        else:
            f = lambda pv_, iv_, hv_: tile_fn(pv_, None, iv_, hv_)[1]
            _, vjp = jax.vjp(f, pv, iv, hv)
            dpv, div, dhv = vjp([d[...].astype(F32) for d in do_refs])
        for j, (di_ref, d) in enumerate(zip(di_refs, div)):
            if use_halo:
                d = jnp.concatenate([d[:r - 8], d[r - 8:] + dh_refs[j][...]], axis=0)
            di_ref[...] = d.astype(di_ref.dtype)
        for dh_ref, d in zip(dh_refs, dhv):
            dh_ref[...] = d
        for dp_ref, d in zip(dp_refs, dpv):
            dp_ref[...] += d

    row = lambda b, nn: (b * nt + nt - 1 - nn, 0)
    hmap = _halo_map(nt, r)
    in_specs = [pl.BlockSpec(p.shape, lambda b, nn: (0, 0)) for p in params]
    in_specs += [pl.BlockSpec((r, a.shape[1]), row) for a in ins]
    if use_halo:
        in_specs += [pl.BlockSpec((8, a.shape[1]), lambda b, nn: hmap(b, nt - 1 - nn)) for a in ins]
    args = list(params) + list(ins) + (list(ins) if use_halo else [])
    scratch = []
    if has_state:
        in_specs.append(pl.BlockSpec((1, 1) + tuple(state_shape), lambda b, nn: (b, nt - 1 - nn, 0, 0)))
        args.append(states)
        scratch.append(pltpu.VMEM(tuple(state_shape), F32))
    in_specs += [pl.BlockSpec((r, d.shape[1]), row) for d in douts]
    args += list(douts)
    if use_halo:
        scratch += [pltpu.VMEM((8, a.shape[1]), F32) for a in ins]
    out_shape = [jax.ShapeDtypeStruct(a.shape, dt) for a, dt in zip(ins, din_dtypes)]
    out_shape += [jax.ShapeDtypeStruct(p.shape, F32) for p in params]
    o_specs = [pl.BlockSpec((r, a.shape[1]), row) for a in ins]
    o_specs += [pl.BlockSpec(p.shape, lambda b, nn: (0, 0)) for p in params]
    if side is not None:
        body, in_specs, o_specs, out_shape, scratch, args = _host_exchange(
            side, body, in_specs, o_specs, out_shape, scratch, args, (nb, nt))
    res = pl.pallas_call(body, grid=(nb, nt), in_specs=in_specs, out_specs=o_specs, out_shape=out_shape,
                         scratch_shapes=scratch, compiler_params=_cparams(("arbitrary", "arbitrary")),
                         name=name)(*args)
    if side is not None:
        return res[:n_i], res[n_i:n_i + n_p], res[n_i + n_p:]
    return res[:n_i], res[n_i:]


def matmul(name, a, b, mode, out_dtype=F32, addend=None, resid=None, tm=512, tn=None, tk=None):
    if mode == "nn":
        (m, kd), (_, n) = a.shape, b.shape
    elif mode == "nt":
        (m, kd), (n, _) = a.shape, b.shape
    else:
        (kd, m), (_, n) = a.shape, b.shape
    tm, tn, tk = min(tm, m if resid is None else resid[2]), min(tn or n, n), min(tk or kd, kd)
    nk = kd // tk
    assert m % tm == 0 and n % tn == 0 and kd % tk == 0
    dims = {"nn": ((1,), (0,)), "nt": ((1,), (1,)), "tn": ((0,), (0,))}[mode]
    extra = [] if addend is None else [addend]
    if resid is not None:
        extra = [resid[0], resid[1]]
    n_in, n_out = 2 + len(extra), 1 if resid is None else 2

    def body(*refs):
        a_ref, b_ref = refs[0], refs[1]
        part = lax.dot_general(a_ref[...].astype(BF16), b_ref[...].astype(BF16), (dims, ((), ())),
                               preferred_element_type=F32)

        def finish(acc):
            if resid is not None:
                refs[n_in][...] = refs[2][...] + refs[3][0] * acc
                refs[n_in + 1][...] = acc.astype(BF16)
            else:
                if addend is not None:
                    acc = acc + refs[2][...]
                refs[n_in][...] = acc.astype(refs[n_in].dtype)

        if nk == 1:
            finish(part)
        else:
            acc_ref = refs[n_in + n_out]
            k = pl.program_id(2)

            @pl.when(k == 0)
            def _():
                acc_ref[...] = part

            @pl.when(k > 0)
            def _():
                acc_ref[...] += part

            @pl.when(k == nk - 1)
            def _():
                finish(acc_ref[...])

    if mode == "tn":
        a_spec = pl.BlockSpec((tk, tm), lambda j, i, k: (k, i))
    else:
        a_spec = pl.BlockSpec((tm, tk), lambda j, i, k: (i, k))
    if mode == "nt":
        b_spec = pl.BlockSpec((tn, tk), lambda j, i, k: (j, k))
    else:
        b_spec = pl.BlockSpec((tk, tn), lambda j, i, k: (k, j))
    o_spec = pl.BlockSpec((tm, tn), lambda j, i, k: (i, j))
    in_specs = [a_spec, b_spec] + [o_spec] * (len(extra) > 0)
    out_specs, out_shape = o_spec, jax.ShapeDtypeStruct((m, n), out_dtype)
    if resid is not None:
        rows = resid[2]
        assert rows % tm == 0
        in_specs.append(pl.BlockSpec((1, 1, tn), lambda j, i, k: (lax.div(i * tm, rows), 0, j)))
        out_specs, out_shape = [o_spec, o_spec], [out_shape, jax.ShapeDtypeStruct((m, n), BF16)]
    scratch = [pltpu.VMEM((tm, tn), F32)] if nk > 1 else []
    return pl.pallas_call(body, grid=(n // tn, m // tm, nk), in_specs=in_specs, out_specs=out_specs,
                          out_shape=out_shape, scratch_shapes=scratch,
                          compiler_params=_cparams(("parallel", "parallel", "arbitrary")), name=name)(a, b, *extra)


def _normmod(x, nw, shift, scale):
    return _rms(x, nw) * (1.0 + scale) + shift


def _row_specs(nb, s, tr, d):
    nt = s // tr
    row = pl.BlockSpec((tr, d), lambda b, i: (b * nt + i, 0))
    per_seq = pl.BlockSpec((1, 1, d), lambda b, i: (b, 0, 0))
    full = pl.BlockSpec((1, d), lambda b, i: (0, 0))
    return nt, row, per_seq, full


def normmod_fwd(name, x, nw, shift, scale, nb, s, tr=512):
    d, tr = x.shape[1], min(tr, s)
    nt, row, per_seq, full = _row_specs(nb, s, tr, d)

    def body(x_ref, nw_ref, sh_ref, sc_ref, h_ref, ht_ref):
        h = _normmod(x_ref[...], nw_ref[...], sh_ref[0], sc_ref[0])
        h_ref[...] = h.astype(h_ref.dtype)
        ht_ref[...] = h.T.astype(ht_ref.dtype)

    return pl.pallas_call(body, grid=(nb, nt), in_specs=[row, full, per_seq, per_seq],
                          out_specs=[row, pl.BlockSpec((d, tr), lambda b, i: (0, b * nt + i))],
                          out_shape=[jax.ShapeDtypeStruct(x.shape, BF16), jax.ShapeDtypeStruct(x.shape[::-1], BF16)],
                          compiler_params=_cparams(("parallel", "parallel")), name=name)(x, nw, shift, scale)


def normmod_bwd(name, x, nw, shift, scale, dh, dres, nb, s, tr=512):
    d, tr = x.shape[1], min(tr, s)
    nt, row, per_seq, full = _row_specs(nb, s, tr, d)

    def body(x_ref, nw_ref, sh_ref, sc_ref, dh_ref, dres_ref, dx_ref, dnw_ref, dsh_ref, dsc_ref):
        b, i = pl.program_id(0), pl.program_id(1)

        @pl.when((b == 0) & (i == 0))
        def _():
            dnw_ref[...] = jnp.zeros(dnw_ref.shape, F32)

        @pl.when(i == 0)
        def _():
            dsh_ref[...] = jnp.zeros(dsh_ref.shape, F32)
            dsc_ref[...] = jnp.zeros(dsc_ref.shape, F32)

        _, vjp = jax.vjp(_normmod, x_ref[...], nw_ref[...], sh_ref[0], sc_ref[0])
        dx, dnw, dsh, dsc = vjp(dh_ref[...])
        dx_ref[...] = dres_ref[...] + dx
        dnw_ref[...] += dnw
        dsh_ref[0] += dsh
        dsc_ref[0] += dsc

    out_shape = [jax.ShapeDtypeStruct(x.shape, F32), jax.ShapeDtypeStruct((1, d), F32),
                 jax.ShapeDtypeStruct((nb, 1, d), F32), jax.ShapeDtypeStruct((nb, 1, d), F32)]
    return pl.pallas_call(body, grid=(nb, nt), in_specs=[row, full, per_seq, per_seq, row, row],
                          out_specs=[row, full, per_seq, per_seq], out_shape=out_shape,
                          compiler_params=_cparams(("arbitrary", "arbitrary")),
                          name=name)(x, nw, shift, scale, dh, dres)


def _merge(oa, ob, oc, graw, gate1, wa, wb, wc, wo):
    d = wo.shape[0]
    g = _sigmoid(graw)
    merged = g[:, :d] * bdot(oa, wa) + g[:, d:2 * d] * bdot(ob, wb) + g[:, 2 * d:] * bdot(oc, wc)
    return gate1 * bdot(merged, wo)


def _merge_specs(nb, s, tr, d, wbr):
    nt, row, per_seq, _ = _row_specs(nb, s, tr, d)
    o_spec = pl.BlockSpec((tr, wbr), lambda b, i: (b * nt + i, 0))
    g_spec = pl.BlockSpec((tr, 3 * d), lambda b, i: (b * nt + i, 0))
    wbr_spec = pl.BlockSpec((wbr, d), lambda b, i: (0, 0))
    wo_spec = pl.BlockSpec((d, d), lambda b, i: (0, 0))
    return nt, row, per_seq, o_spec, g_spec, wbr_spec, wo_spec


def merge_fwd(name, x, oa, ob, oc, pg, gate1, wa, wb, wc, wo, nb, s, tr=512):
    d, tr = x.shape[1], min(tr, s)
    nt, row, per_seq, o_spec, g_spec, wbr_spec, wo_spec = _merge_specs(nb, s, tr, d, oa.shape[1])

    def body(x_ref, oa_ref, ob_ref, oc_ref, pg_ref, g1_ref, wa_ref, wb_ref, wc_ref, wo_ref, x1_ref):
        x1_ref[...] = x_ref[...] + _merge(oa_ref[...], ob_ref[...], oc_ref[...], pg_ref[...], g1_ref[0],
                                          wa_ref[...], wb_ref[...], wc_ref[...], wo_ref[...])

    return pl.pallas_call(body, grid=(nb, nt),
                          in_specs=[row, o_spec, o_spec, o_spec, g_spec, per_seq, wbr_spec, wbr_spec, wbr_spec, wo_spec],
                          out_specs=row, out_shape=jax.ShapeDtypeStruct(x.shape, F32),
                          compiler_params=_cparams(("parallel", "parallel")),
                          name=name)(x, oa, ob, oc, pg, gate1, wa, wb, wc, wo)


def merge_bwd(name, oa, ob, oc, pg, gate1, wa, wb, wc, wo, dx1, nb, s, tr=256):
    d, tr = dx1.shape[1], min(tr, s)
    wbr = oa.shape[1]
    nt, row, per_seq, o_spec, g_spec, wbr_spec, wo_spec = _merge_specs(nb, s, tr, d, wbr)

    def body(oa_ref, ob_ref, oc_ref, pg_ref, g1_ref, wa_ref, wb_ref, wc_ref, wo_ref, dx_ref,
             doa_ref, dob_ref, doc_ref, dpg_ref, dg1_ref, dwa_ref, dwb_ref, dwc_ref, dwo_ref):
        b, i = pl.program_id(0), pl.program_id(1)

        @pl.when((b == 0) & (i == 0))
        def _():
            for r in (dwa_ref, dwb_ref, dwc_ref, dwo_ref):
                r[...] = jnp.zeros(r.shape, F32)

        @pl.when(i == 0)
        def _():
            dg1_ref[...] = jnp.zeros(dg1_ref.shape, F32)

        args = [oa_ref[...].astype(F32), ob_ref[...].astype(F32), oc_ref[...].astype(F32), pg_ref[...], g1_ref[0],
                wa_ref[...].astype(F32), wb_ref[...].astype(F32), wc_ref[...].astype(F32), wo_ref[...].astype(F32)]
        _, vjp = jax.vjp(_merge, *args)
        doa, dob, doc, dpg, dg1, dwa, dwb, dwc, dwo = vjp(dx_ref[...])
        doa_ref[...] = doa
        dob_ref[...] = dob
        doc_ref[...] = doc
        dpg_ref[...] = dpg
        dg1_ref[0] += dg1
        dwa_ref[...] += dwa
        dwb_ref[...] += dwb
        dwc_ref[...] += dwc
        dwo_ref[...] += dwo

    t = nb * s
    out_shape = ([jax.ShapeDtypeStruct((t, wbr), F32)] * 3
                 + [jax.ShapeDtypeStruct((t, 3 * d), F32), jax.ShapeDtypeStruct((nb, 1, d), F32)]
                 + [jax.ShapeDtypeStruct((wbr, d), F32)] * 3 + [jax.ShapeDtypeStruct((d, d), F32)])
    return pl.pallas_call(body, grid=(nb, nt),
                          in_specs=[o_spec, o_spec, o_spec, g_spec, per_seq, wbr_spec, wbr_spec, wbr_spec, wo_spec, row],
                          out_specs=[o_spec, o_spec, o_spec, g_spec, per_seq, wbr_spec, wbr_spec, wbr_spec, wo_spec],
                          out_shape=out_shape, compiler_params=_cparams(("arbitrary", "arbitrary")),
                          name=name)(oa, ob, oc, pg, gate1, wa, wb, wc, wo, dx1)


def resid_bwd(name, dx, f, gate, nb, s, tr=512):
    d, tr = dx.shape[1], min(tr, s)
    nt, row, per_seq, _ = _row_specs(nb, s, tr, d)

    def body(dx_ref, f_ref, g_ref, df_ref, dg_ref):
        @pl.when(pl.program_id(1) == 0)
        def _():
            dg_ref[...] = jnp.zeros(dg_ref.shape, F32)

        df_ref[...] = (g_ref[0] * dx_ref[...]).astype(df_ref.dtype)
        dg_ref[0] += jnp.sum(dx_ref[...] * f_ref[...], axis=0, keepdims=True)

    return pl.pallas_call(body, grid=(nb, nt), in_specs=[row, row, per_seq], out_specs=[row, per_seq],
                          out_shape=[jax.ShapeDtypeStruct(dx.shape, BF16), jax.ShapeDtypeStruct((nb, 1, d), F32)],
                          compiler_params=_cparams(("arbitrary", "arbitrary")), name=name)(dx, f, gate)


def loss_head(name, x, fw, target, tr=512):
    t, d = x.shape
    row = pl.BlockSpec((tr, d), lambda i: (i, 0))
    full = pl.BlockSpec((1, d), lambda i: (0, 0))

    def loss_fn(xv, fwv, tv):
        err = _rms(xv, fwv) - tv
        return 0.5 * jnp.sum(jnp.mean(err * err, axis=-1))

    def body(x_ref, fw_ref, t_ref, dx_ref, l_ref, dfw_ref):
        @pl.when(pl.program_id(0) == 0)
        def _():
            l_ref[...] = jnp.zeros(l_ref.shape, F32)
            dfw_ref[...] = jnp.zeros(dfw_ref.shape, F32)

        val, (dx, dfw) = jax.value_and_grad(loss_fn, argnums=(0, 1))(x_ref[...], fw_ref[...], t_ref[...])
        dx_ref[...] = dx
        l_ref[...] += val
        dfw_ref[...] += dfw

    return pl.pallas_call(body, grid=(t // tr,), in_specs=[row, full, row],
                          out_specs=[row, pl.BlockSpec((1, 128), lambda i: (0, 0)), full],
                          out_shape=[jax.ShapeDtypeStruct((t, d), F32), jax.ShapeDtypeStruct((1, 128), F32),
                                     jax.ShapeDtypeStruct((1, d), F32)],
                          compiler_params=_cparams(("arbitrary",)), name=name)(x, fw, target)


def _row_tile(rows, cols, n_arrays):
    budget = 24 * 1024 * 1024 // (8 * cols * max(n_arrays, 1))
    tr = rows
    while tr > max(budget, 16) and tr % 2 == 0 and (tr // 2) % 16 == 0:
        tr //= 2
    return tr


def elementwise(name, fn, ins, out_dtypes):
    rows, cols = ins[0].shape
    tr = _row_tile(rows, cols, len(ins) + len(out_dtypes))
    spec = pl.BlockSpec((tr, cols), lambda i: (i, 0))
    n_in = len(ins)

    def body(*refs):
        outs = fn(*[r[...] for r in refs[:n_in]])
        for o_ref, o in zip(refs[n_in:], outs):
            o_ref[...] = o.astype(o_ref.dtype)

    return pl.pallas_call(body, grid=(rows // tr,), in_specs=[spec] * n_in, out_specs=[spec] * len(out_dtypes),
                          out_shape=[jax.ShapeDtypeStruct((rows, cols), dt) for dt in out_dtypes],
                          compiler_params=_cparams(("parallel",)), name=name)(*ins)


def _adamw(w, g, m, v):
    m = ADAM_B1 * m + (1.0 - ADAM_B1) * g
    v = ADAM_B2 * v + (1.0 - ADAM_B2) * (g * g)
    m_hat = m / (1.0 - ADAM_B1 ** ADAM_STEP)
    v_hat = v / (1.0 - ADAM_B2 ** ADAM_STEP)
    delta = -ADAM_LR * (m_hat / (jnp.sqrt(v_hat) + ADAM_EPS) + ADAM_WD * w)
    return delta, m, v


def adamw(name, w, g, m, v):
    return elementwise(name, _adamw, [w, g, m, v], [F32, F32, F32])


_ANY = pl.BlockSpec(memory_space=pl.ANY)


def _coords():
    return lax.axis_index("x"), lax.axis_index("y"), lax.axis_index("c")


def allgather8(name, arrays, halves):
    n = len(arrays)

    def body(*refs):
        in_refs, out_refs = refs[:n], refs[n:2 * n]
        send_sems, recv_sems, local_sems = refs[2 * n:]
        x, y, c = _coords()
        me, sibling = (x, y, c), (x, y, 1 - c)
        chips = [(1 - x, y), (x, 1 - y), (1 - x, 1 - y)]

        def blk(i, px, py, pc):
            return out_refs[i].at[4 * px + 2 * py + pc]

        def piece(i):
            return in_refs[i].at[c] if halves[i] else in_refs[i]

        def copy(i, k, block, to, src=None):
            return pltpu.make_async_remote_copy(
                src_ref=blk(i, *block) if src is None else src, dst_ref=blk(i, *block),
                send_sem=send_sems.at[7 * i + k], recv_sem=recv_sems.at[7 * i + k],
                device_id=to, device_id_type=MESH)

        mine = [pltpu.make_async_copy(piece(i), blk(i, *me), local_sems.at[i]) for i in range(n)]
        for cp in mine:
            cp.start()
        first = []
        for i in range(n):
            first.append(copy(i, 0, me, sibling, src=piece(i)))
            first += [copy(i, 1 + j, me, (*chip, c), src=piece(i)) for j, chip in enumerate(chips)]
        for cp in first:
            cp.start()
        passed = []
        for j, chip in enumerate(chips):
            for i in range(n):
                copy(i, 1 + j, (*chip, c), me).wait_recv()
                fwd = copy(i, 4 + j, (*chip, c), sibling)
                fwd.start()
                passed.append(fwd)
        for i in range(n):
            copy(i, 0, sibling, me).wait_recv()
            for j, chip in enumerate(chips):
                copy(i, 4 + j, (*chip, 1 - c), me).wait_recv()
        for cp in first + passed:
            cp.wait_send()
        for cp in mine:
            cp.wait()

    out_shape = []
    for a, hv in zip(arrays, halves):
        out_shape.append(jax.ShapeDtypeStruct((N_DEV,) + tuple(a.shape[1:] if hv else a.shape), a.dtype))
    return pl.pallas_call(
        body, in_specs=[_ANY] * n, out_specs=[_ANY] * n, out_shape=out_shape,
        scratch_shapes=[pltpu.SemaphoreType.DMA((7 * n,)), pltpu.SemaphoreType.DMA((7 * n,)),
                        pltpu.SemaphoreType.DMA((n,))],
        name=name)(*arrays)


def exchange(name, ins, out_shapes, plan, local_plan=()):
    n_in, n_out = len(ins), len(out_shapes)

    def body(*refs):
        copies = _exchange_copies(plan, local_plan, refs[:n_in], refs[n_in:n_in + n_out], *refs[n_in + n_out:])
        for cp in copies:
            cp.start()
        for cp in copies:
            cp.wait()

    return pl.pallas_call(
        body, in_specs=[_ANY] * n_in, out_specs=[_ANY] * n_out, out_shape=out_shapes,
        scratch_shapes=_exchange_sems(plan, local_plan), name=name)(*ins)


def sum_halves(name, gs, recv, c_arr):
    _, _, hr, cs = gs.shape
    tr = _row_tile(hr, cs, 4)

    def body(c_ref, g_ref, r_ref, qf_ref, qb_ref):
        q = g_ref[0, 0] + r_ref[0, 0]
        qf_ref[0] = q
        qb_ref[0] = q.astype(BF16)

    grid_spec = pltpu.PrefetchScalarGridSpec(
        num_scalar_prefetch=1, grid=(N_CHIPS, hr // tr),
        in_specs=[pl.BlockSpec((1, 1, tr, cs), lambda j, i, c_ref: (j, c_ref[0], i, 0)),
                  pl.BlockSpec((1, 1, tr, cs), lambda j, i, c_ref: (j, 0, i, 0))],
        out_specs=[pl.BlockSpec((1, tr, cs), lambda j, i, c_ref: (j, i, 0))] * 2)
    return pl.pallas_call(body, grid_spec=grid_spec,
                          out_shape=[jax.ShapeDtypeStruct((N_CHIPS, hr, cs), F32),
                                     jax.ShapeDtypeStruct((N_CHIPS, hr, cs), BF16)],
                          compiler_params=_cparams(("parallel", "parallel")), name=name)(c_arr, gs, recv)


def sum_chips(name, qf, recv, chip_arr):
    _, hr, cs = qf.shape
    tr = _row_tile(hr, cs, 4)

    def body(chip_ref, q_ref, a_ref, b_ref, c_ref, o_ref):
        o_ref[...] = q_ref[0] + a_ref[...].astype(F32) + b_ref[...].astype(F32) + c_ref[...].astype(F32)

    row = pl.BlockSpec((tr, cs), lambda i, chip_ref: (i, 0))
    grid_spec = pltpu.PrefetchScalarGridSpec(
        num_scalar_prefetch=1, grid=(hr // tr,),
        in_specs=[pl.BlockSpec((1, tr, cs), lambda i, chip_ref: (chip_ref[0], i, 0)), row, row, row],
        out_specs=row)
    return pl.pallas_call(body, grid_spec=grid_spec, out_shape=jax.ShapeDtypeStruct((hr, cs), F32),
                          compiler_params=_cparams(("parallel",)), name=name)(chip_arr, qf, *recv)


def adamw_halves(name, w, m, v, layer, g_mine, g_other, c_arr, prev=None):
    _, _, hr, cs = w.shape
    tr = _row_tile(hr, cs, 9)

    def body(c_ref, w_ref, m_ref, v_ref, gm_ref, go_ref, *rest):
        g_ref, d_ref, nm_ref, nv_ref = rest[-4:]
        g = jnp.where(pl.program_id(0) == c_ref[0], gm_ref[...], go_ref[...])
        delta, nm, nv = _adamw(w_ref[0, 0], g, m_ref[0, 0], v_ref[0, 0])
        g_ref[0, 0], d_ref[0, 0], nm_ref[0, 0], nv_ref[0, 0] = g, delta, nm, nv

    half = pl.BlockSpec((1, 1, tr, cs), lambda h, i, c_ref: (layer, h, i, 0))
    row = pl.BlockSpec((tr, cs), lambda h, i, c_ref: (i, 0))
    in_specs, args, aliases = [half, half, half, row, row], [c_arr, w, m, v, g_mine, g_other], {}
    if prev is not None:
        in_specs += [pl.BlockSpec(memory_space=pl.ANY)] * 4
        args += list(prev)
        aliases = {6 + k: k for k in range(4)}
    grid_spec = pltpu.PrefetchScalarGridSpec(num_scalar_prefetch=1, grid=(2, hr // tr),
                                             in_specs=in_specs, out_specs=[half] * 4)
    return pl.pallas_call(body, grid_spec=grid_spec, out_shape=[jax.ShapeDtypeStruct(w.shape, F32)] * 4,
                          input_output_aliases=aliases, compiler_params=_cparams(("parallel", "parallel")),
                          name=name)(*args)


def sum8(name, g):
    _, rows, cols = g.shape
    tr = _row_tile(rows, cols, 9)

    def body(*refs):
        acc = refs[0][0]
        for r in refs[1:N_DEV]:
            acc = acc + r[0]
        refs[N_DEV][...] = acc

    in_specs = [pl.BlockSpec((1, tr, cols), functools.partial(lambda k, i: (k, i, 0), k)) for k in range(N_DEV)]
    return pl.pallas_call(body, grid=(rows // tr,), in_specs=in_specs,
                          out_specs=pl.BlockSpec((tr, cols), lambda i: (i, 0)),
                          out_shape=jax.ShapeDtypeStruct((rows, cols), F32),
                          compiler_params=_cparams(("parallel",)), name=name)(*([g] * N_DEV))


_QKV, _AB, _GZ = (0, 1536), (1536, 1544), (1544, 2056)
_HG = (2056, 4104)
_SZ, _XBC, _DT = (4104, 4616), (4616, 5640), (5640, 5648)
_GATES = (5648, 8720)


def _split_w_in(w8):
    _, hr, cs = w8.shape
    w4 = w8.reshape(N_CHIPS, 2 * hr, cs)

    def cols(rng):
        lo, hi = rng
        return [w4[j][:, max(lo, j * cs) - j * cs:min(hi, (j + 1) * cs) - j * cs]
                for j in range(N_CHIPS) if max(lo, j * cs) < min(hi, (j + 1) * cs)]

    pad = [jnp.zeros((2 * hr, 120), w8.dtype)]
    return (jnp.concatenate(cols(_GATES), axis=1),
            jnp.concatenate(cols(_QKV) + cols(_GZ) + cols(_AB) + pad, axis=1),
            jnp.concatenate(cols(_HG), axis=1),
            jnp.concatenate(cols(_SZ) + cols(_XBC) + cols(_DT) + pad, axis=1))


def _stack_w_in(g, a, b, c):
    segments = [(a, 0, 1536), (a, 2048, 2056), (a, 1536, 2048), (b, 0, 2048), (c, 0, 512), (c, 512, 1536),
                (c, 1536, 1544), (g, 0, 3072)]
    cs = sum(s1 - s0 for _, s0, s1 in segments) // N_CHIPS
    chips = []
    for j in range(N_CHIPS):
        parts, off = [], 0
        for arr, s0, s1 in segments:
            u0, u1 = max(j * cs, off), min((j + 1) * cs, off + s1 - s0)
            if u0 < u1:
                parts.append(arr[:, s0 + u0 - off:s0 + u1 - off])
            off += s1 - s0
        chips.append(jnp.concatenate(parts, axis=1))
    rows = g.shape[0]
    return jnp.stack(chips).reshape(N_CHIPS, 2, rows // 2, cs)


def _rows8(rows, width):
    out = [jnp.pad(r.astype(F32), (0, width - r.shape[0])) for r in rows]
    out += [jnp.zeros((width,), F32)] * (8 - len(out))
    return jnp.stack(out)


class _Packer:
    def __init__(self):
        self.items, self.size = [], 0

    def add(self, name, shape):
        n = 1
        for d in shape:
            n *= d
        self.items.append((name, tuple(shape), self.size, n))
        self.size += n

    def rows(self):
        return -(-self.size // 8192) * 8

    def pack(self, values):
        flat = [values[name].astype(F32).reshape(-1) for name, _, _, _ in self.items]
        flat.append(jnp.zeros((self.rows() * 1024 - self.size,), F32))
        return jnp.concatenate(flat).reshape(self.rows(), 1024)

    def unpack(self, buf):
        flat = buf.reshape(-1)
        return {name: flat[off:off + n].reshape(shape) for name, shape, off, n in self.items}


def _stack_by_chip(g, axis):
    l, r, c = g.shape
    if axis == 2:
        cs = c // N_CHIPS
        g = g.reshape(l, r, N_CHIPS, cs).transpose(2, 0, 1, 3).reshape(N_CHIPS, 2, l * r // 2, cs)
    else:
        rs = r // N_CHIPS
        g = g.reshape(l, N_CHIPS, rs, c).transpose(1, 0, 2, 3).reshape(N_CHIPS, 2, l * rs // 2, c)
    return g


def _unstack_gathered(w8, l, axis):
    _, hr, cs = w8.shape
    w = w8.reshape(N_CHIPS, l, 2 * hr // l, cs)
    if axis == 2:
        return w.transpose(1, 2, 0, 3).reshape(l, 2 * hr // l, N_CHIPS * cs)
    return w.transpose(1, 0, 2, 3).reshape(l, N_CHIPS * 2 * hr // l, cs)


_BIG = (("w_in", 2), ("w_br_a", 2), ("w_br_b", 2), ("w_br_c", 2), ("w_out", 1), ("ffn_w_up", 2), ("ffn_w_down", 1))
_SMALL = ("b_ada", "norm1_w", "gdn_conv_w", "gdn_a_log", "gdn_dt_bias", "gdn_norm_w", "hgrn_lb_param",
          "hgrn_norm_w", "ssd_conv_w", "ssd_conv_b", "ssd_a_log", "ssd_dt_bias", "ssd_d", "ssd_norm_w",
          "norm2_w", "ffn_conv_w", "ffn_conv_b", "final_norm_w")
_WEIGHTS = ("w_ada", "b_ada", "norm1_w", "w_in", "gdn_conv_w", "gdn_a_log", "gdn_dt_bias", "gdn_norm_w",
            "hgrn_lb_param", "hgrn_norm_w", "ssd_conv_w", "ssd_conv_b", "ssd_a_log", "ssd_dt_bias", "ssd_d",
            "ssd_norm_w", "w_br_a", "w_br_b", "w_br_c", "w_out", "norm2_w", "ffn_w_up", "ffn_conv_w",
            "ffn_conv_b", "ffn_w_down", "final_norm_w")
_R_GDN, _R_HGRN, _R_SSD, _R_FFN = 256, 128, 256, 256


_MASKS = ((1, 0), (0, 1), (1, 1))


def _flip(k, x, y):
    return (1 - x if _MASKS[k][0] else x), (1 - y if _MASKS[k][1] else y)


def _rs_d2d(grads):
    plan = [functools.partial(lambda i, ins, outs, x, y, c: (ins[i].at[:, pl.ds(1 - c, 1)], outs[i], (x, y, 1 - c)), i)
            for i in range(len(grads))]
    return grads, [jax.ShapeDtypeStruct((N_CHIPS, 1) + g.shape[2:], F32) for g in grads], plan, (), ()


def _rs_ici(grads, recv, tag):
    n = len(grads)
    c_arr = lax.axis_index("c").astype(jnp.int32).reshape(1)
    q = [sum_halves("rs_sum_d2d%s_%d" % (tag, i), g, r, c_arr) for i, (g, r) in enumerate(zip(grads, recv))]
    qf, qb = [a for a, _ in q], [b for _, b in q]

    def ici(i, k, ins, outs, x, y, c):
        px, py = _flip(k, x, y)
        return ins[i].at[2 * px + py], outs[3 * i + k], (px, py, c)

    plan = [functools.partial(ici, i, k) for i in range(n) for k in range(3)]
    shapes = [jax.ShapeDtypeStruct(g.shape[2:], BF16) for g in grads for _ in range(3)]
    return qf, (qb, shapes, plan, (), ())


def _rs_finish(qf, res, tag):
    n = len(qf)
    chip_arr = (2 * lax.axis_index("x") + lax.axis_index("y")).astype(jnp.int32).reshape(1)
    red = [sum_chips("rs_sum_ici%s_%d" % (tag, i), qf[i], res[3 * i:3 * i + 3], chip_arr) for i in range(n)]
    plan = [functools.partial(lambda i, ins, outs, x, y, c: (ins[i], outs[i], (x, y, 1 - c)), i) for i in range(n)]
    other = exchange("rs_swap" + tag, red, [jax.ShapeDtypeStruct(r.shape, F32) for r in red], plan)
    return red, other


def _join_sides(sides):
    ins, shapes, plan = [], [], []
    for s_ins, s_shapes, s_plan, _, _ in sides:
        def shifted(fn, i0, i1, o0, o1, in_refs, out_refs, x, y, c):
            return fn(in_refs[i0:i1], out_refs[o0:o1], x, y, c)

        i0, o0 = len(ins), len(shapes)
        plan += [functools.partial(shifted, fn, i0, i0 + len(s_ins), o0, o0 + len(s_shapes)) for fn in s_plan]
        ins += list(s_ins)
        shapes += list(s_shapes)
    return ins, shapes, plan, (), ()


def _gather_side(pieces):
    n = len(pieces)

    def send(i, k, ins, outs, x, y, c):
        px, py = _flip(k, x, y)
        return ins[i].at[c], outs[i].at[2 * (2 * x + y) + c], (px, py, c)

    def to_sibling(i, h, ins, outs, x, y, c):
        return ins[i].at[h], outs[i].at[2 * (2 * x + y) + h], (x, y, 1 - c)

    def pass_on(i, k, ins, outs, x, y, c):
        px, py = _flip(k, x, y)
        blk = 2 * (2 * px + py) + c
        return outs[i].at[blk], outs[i].at[blk], (x, y, 1 - c)

    plan = [functools.partial(send, i, k) for i in range(n) for k in range(3)]
    plan += [functools.partial(to_sibling, i, h) for i in range(n) for h in range(2)]
    then = [functools.partial(pass_on, i, k) for i in range(n) for k in range(3)]
    shapes = [jax.ShapeDtypeStruct((N_DEV,) + p.shape[1:], p.dtype) for p in pieces]
    return pieces, shapes, plan, (), then


def kernel(x, c, w_ada, b_ada, norm1_w, w_in, gdn_conv_w, gdn_a_log, gdn_dt_bias, gdn_norm_w, hgrn_lb_param, hgrn_norm_w, ssd_conv_w, ssd_conv_b, ssd_a_log, ssd_dt_bias, ssd_d, ssd_norm_w, w_br_a, w_br_b, w_br_c, w_out, norm2_w, ffn_w_up, ffn_conv_w, ffn_conv_b, ffn_w_down, final_norm_w, loss_target, m_w_ada, m_b_ada, m_norm1_w, m_w_in, m_gdn_conv_w, m_gdn_a_log, m_gdn_dt_bias, m_gdn_norm_w, m_hgrn_lb_param, m_hgrn_norm_w, m_ssd_conv_w, m_ssd_conv_b, m_ssd_a_log, m_ssd_dt_bias, m_ssd_d, m_ssd_norm_w, m_w_br_a, m_w_br_b, m_w_br_c, m_w_out, m_norm2_w, m_ffn_w_up, m_ffn_conv_w, m_ffn_conv_b, m_ffn_w_down, m_final_norm_w, v_w_ada, v_b_ada, v_norm1_w, v_w_in, v_gdn_conv_w, v_gdn_a_log, v_gdn_dt_bias, v_gdn_norm_w, v_hgrn_lb_param, v_hgrn_norm_w, v_ssd_conv_w, v_ssd_conv_b, v_ssd_a_log, v_ssd_dt_bias, v_ssd_d, v_ssd_norm_w, v_w_br_a, v_w_br_b, v_w_br_c, v_w_out, v_norm2_w, v_ffn_w_up, v_ffn_conv_w, v_ffn_conv_b, v_ffn_w_down, v_final_norm_w):
    loc = dict(locals())
    w = {k: loc[k] for k in _WEIGHTS}
    mom = {k: loc["m_" + k] for k in _WEIGHTS}
    var = {k: loc["v_" + k] for k in _WEIGHTS}
    nb, s, d = x.shape
    t = nb * s
    depth = w_ada.shape[0]
    chip = 2 * lax.axis_index("x") + lax.axis_index("y")
    dev = 2 * chip + lax.axis_index("c")
    x0 = x.reshape(t, d)
    target = loss_target.reshape(t, d)

    small_in = [c, gdn_conv_w.reshape(depth * 4, -1), ssd_conv_w.reshape(depth * 4, -1),
                ffn_conv_w.reshape(depth * 3, -1)]
    c_all, gcw, scw, fcw = allgather8("ag_small", small_in, [False] * 4)
    c_all = c_all.reshape(N_DEV * nb, d)

    def conv_full(g, taps):
        g = g[::2].reshape(N_CHIPS, depth, taps, -1)
        return g.transpose(1, 2, 0, 3).reshape(depth, taps, -1)

    gdn_cw, ssd_cw, ffn_cw = conv_full(gcw, 4), conv_full(scw, 4), conv_full(fcw, 3)

    axis_of = dict(_BIG)
    first_needed, later = ("w_in",), tuple(n for n, _ in _BIG if n != "w_in")
    wls = [dict() for _ in range(depth)]

    def pieces(keys):
        out = []
        for l, name in keys:
            a = w[name][l].astype(BF16)
            out.append(a.reshape(2, a.shape[0] // 2, a.shape[1]))
        return out

    def arrived(keys, bufs):
        for (l, name), g in zip(keys, bufs):
            if name == "w_in":
                wls[l]["w_g"], wls[l]["w_a"], wls[l]["w_b"], wls[l]["w_c"] = _split_w_in(g)
            else:
                wls[l][name] = _unstack_gathered(g, 1, axis_of[name])[0]

    keys0 = [(0, n) for n in first_needed]
    arrived(keys0, allgather8("ag_weights0", pieces(keys0), [True] * len(keys0)))

    (c_act,) = elementwise("silu_c", lambda v: (_silu(v),), [c_all], [F32])
    mod_cols = jnp.concatenate([matmul("ada_fwd%d" % l, c_act, w_ada[l], "nn") for l in range(depth)], axis=0)
    (mod8,) = allgather8("ag_mod", [mod_cols], [False])
    mod = mod8[::2].reshape(N_CHIPS, depth, N_DEV * nb, -1).transpose(1, 2, 0, 3).reshape(depth, N_DEV * nb, 6 * d)
    mod = lax.dynamic_slice_in_dim(mod, dev * nb, nb, axis=1) + b_ada[:, None, :]

    def mod_part(l, k):
        return mod[l, :, k * d:(k + 1) * d].reshape(nb, 1, d)

    saved = []
    xl = x0
    for l in range(depth):
        sfx = str(l)
        wl = wls[l]
        sv = {"x0": xl}
        shift1, scale1, gate1, shift2, scale2, gate2 = [mod_part(l, k) for k in range(6)]
        sv["mods"] = (shift1, scale1, gate1, shift2, scale2, gate2)
        h, h_t = normmod_fwd("norm1_fwd" + sfx, xl, norm1_w[l][None], shift1, scale1, nb, s)
        pg = matmul("proj_g" + sfx, h, wl["w_g"], "nn")
        pa = matmul("proj_a" + sfx, h, wl["w_a"], "nn")
        pb = matmul("proj_b" + sfx, h, wl["w_b"], "nn")
        pc = matmul("proj_c" + sfx, h, wl["w_c"], "nn")
        gdn_p = [_rows8(list(gdn_cw[l]), 1536), _rows8([gdn_a_log[l], gdn_dt_bias[l], gdn_norm_w[l]], 128)]
        hgrn_p = [_rows8(list(hgrn_lb_param), 512), _rows8([hgrn_norm_w[l]], 128)]
        ssd_p = [_rows8(list(ssd_cw[l]), 1024), _rows8([ssd_conv_b[l], ssd_norm_w[l]], 1024),
                 _rows8([ssd_a_log[l], ssd_dt_bias[l], ssd_d[l]], 128)]
        ffn_p = [_rows8(list(ffn_cw[l]) + [ffn_conv_b[l]], 2 * FFN_HIDDEN)]
        hgrn_fn = make_hgrn_tile(l, depth)
        keys = [(l, n) for n in later] + ([(l + 1, n) for n in first_needed] if l + 1 < depth else [])
        oa, st_a, *bufs = seq_fwd("gdn_fwd" + sfx, gdn_tile, gdn_p, [pa], True, [(512, BF16)], (512, 128), nb, s,
                                  _R_GDN, side=_gather_side(pieces(keys)))
        arrived(keys, bufs)
        ob, st_b = seq_fwd("hgrn_fwd" + sfx, hgrn_fn, hgrn_p, [pb], False, [(512, BF16)], (512, 128), nb, s, _R_HGRN)
        oc, st_c = seq_fwd("ssd_fwd" + sfx, ssd_tile, ssd_p, [pc], True, [(512, BF16)], (256, 256), nb, s, _R_SSD)
        x1 = merge_fwd("merge_fwd" + sfx, xl, oa, ob, oc, pg, gate1, wl["w_br_a"], wl["w_br_b"], wl["w_br_c"],
                       wl["w_out"], nb, s)
        h2, h2_t = normmod_fwd("norm2_fwd" + sfx, x1, norm2_w[l][None], shift2, scale2, nb, s)
        u = matmul("ffn_up" + sfx, h2, wl["ffn_w_up"], "nn", tn=FFN_HIDDEN)
        act, act_t = seq_fwd("convglu_fwd" + sfx, convglu_tile_t, ffn_p, [u], True,
                             [(FFN_HIDDEN, BF16), (FFN_HIDDEN, BF16, "T")], None, nb, s, _R_FFN)
        xl, f = matmul("ffn_down" + sfx, act, wl["ffn_w_down"], "nn", resid=(x1, gate2, s))
        sv.update(h_t=h_t, h2_t=h2_t, act_t=act_t, pg=pg, pa=pa, pb=pb, pc=pc, oa=oa, ob=ob, oc=oc, st_a=st_a, st_b=st_b, st_c=st_c, x1=x1,
                  u=u, f=f, gdn_p=gdn_p, hgrn_p=hgrn_p, ssd_p=ssd_p, ffn_p=ffn_p, hgrn_fn=hgrn_fn)
        saved.append(sv)

    dx, loss_part, d_final = loss_head("loss_head", xl, final_norm_w[None], target)

    sg = {}
    dmod = [None] * depth
    d_lb = None
    reduced = {}
    early = [n for n, _ in _BIG if n != "w_in"]
    to_d2d = to_ici = None

    def finish(keys, qf, res, tag):
        for key, mine, other in zip(keys, *_rs_finish(qf, res, tag)):
            reduced[key] = (mine, other)
    for l in reversed(range(depth)):
        sfx = str(l)
        sv, wl = saved[l], wls[l]
        gfull = {}
        shift1, scale1, gate1, shift2, scale2, gate2 = sv["mods"]
        df, dgate2 = resid_bwd("resid_bwd" + sfx, dx, sv["f"], gate2, nb, s)
        dact = matmul("ffn_down_dx" + sfx, df, wl["ffn_w_down"], "nt")
        gfull["ffn_w_down"] = matmul("ffn_down_dw" + sfx, sv["act_t"], df, "nn", tm=1408, tn=512, tk=4096)
        cg_args = ("convglu_bwd" + sfx, convglu_tile, sv["ffn_p"], [sv["u"]], True, None, [dact], [BF16], None, nb, s,
                   _R_FFN)
        if to_d2d is None:
            (du,), (dcw,) = seq_bwd(*cg_args)
        else:
            lp, stacked = to_d2d
            (du,), (dcw,), recv = seq_bwd(*cg_args, side=_rs_d2d(stacked))
            to_ici = ([(lp, "w_in")],) + _rs_ici(stacked, recv, "i%d" % lp) + ("i%d" % lp,)
        dh2 = matmul("ffn_up_dx" + sfx, du, wl["ffn_w_up"], "nt")
        gfull["ffn_w_up"] = matmul("ffn_up_dw" + sfx, sv["h2_t"], du, "nn", tm=1024, tn=512, tk=4096)
        dx1, dnw2, dshift2, dscale2 = normmod_bwd("norm2_bwd" + sfx, sv["x1"], norm2_w[l][None], shift2, scale2, dh2, dx,
                                                  nb, s)
        doa, dob, doc, dpg, dgate1, dwa, dwb, dwc, dwo = merge_bwd(
            "merge_bwd" + sfx, sv["oa"], sv["ob"], sv["oc"], sv["pg"], gate1, wl["w_br_a"], wl["w_br_b"],
            wl["w_br_c"], wl["w_out"], dx1, nb, s)
        gfull["w_br_a"], gfull["w_br_b"], gfull["w_br_c"], gfull["w_out"] = dwa, dwb, dwc, dwo
        gdn_args = ("gdn_bwd" + sfx, gdn_tile, sv["gdn_p"], [sv["pa"]], True, sv["st_a"], [doa], [F32], (512, 128),
                    nb, s, _R_GDN)
        stacked = [_stack_by_chip(gfull[n][None], axis_of[n]) for n in early]
        recv = exchange("rs_d2d_e" + sfx, *_rs_d2d(stacked)[:4])
        hosted = [([(l, n) for n in early],) + _rs_ici(stacked, recv, "e" + sfx) + ("e" + sfx,)]
        if to_ici is not None:
            hosted.append(to_ici)
        (dpa,), (dgcw, dgpk), res = seq_bwd(*gdn_args, side=_join_sides([h[2] for h in hosted]))
        for keys, qf, side, tag in hosted:
            finish(keys, qf, res[:len(side[1])], tag)
            res = res[len(side[1]):]
        (dpb,), (dlbp, dhnw) = seq_bwd("hgrn_bwd" + sfx, sv["hgrn_fn"], sv["hgrn_p"], [sv["pb"]], False, sv["st_b"],
                                       [dob], [F32], (512, 128), nb, s, _R_HGRN)
        (dpc,), (dscw, dspv, dsps) = seq_bwd("ssd_bwd" + sfx, ssd_tile, sv["ssd_p"], [sv["pc"]], True, sv["st_c"],
                                             [doc], [F32], (256, 256), nb, s, _R_SSD)
        dh = matmul("proj_g_dx" + sfx, dpg, wl["w_g"], "nt")
        dh = matmul("proj_a_dx" + sfx, dpa, wl["w_a"], "nt", addend=dh)
        dh = matmul("proj_b_dx" + sfx, dpb, wl["w_b"], "nt", addend=dh)
        dh = matmul("proj_c_dx" + sfx, dpc, wl["w_c"], "nt", addend=dh)
        stacked_w_in = _stack_w_in(
            matmul("proj_g_dw" + sfx, sv["h_t"], dpg, "nn", tm=1024, tn=512, tk=4096),
            matmul("proj_a_dw" + sfx, sv["h_t"], dpa, "nn", tm=1024, tk=1024),
            matmul("proj_b_dw" + sfx, sv["h_t"], dpb, "nn", tm=1024, tn=512, tk=4096),
            matmul("proj_c_dw" + sfx, sv["h_t"], dpc, "nn", tm=1024, tk=1024))
        dx, dnw1, dshift1, dscale1 = normmod_bwd("norm1_bwd" + sfx, sv["x0"], norm1_w[l][None], shift1, scale1, dh, dx1,
                                                 nb, s)
        dmod[l] = jnp.concatenate([dshift1, dscale1, dgate1, dshift2, dscale2, dgate2], axis=-1).reshape(nb, 6 * d)
        d_lb = dlbp[:depth] if d_lb is None else d_lb + dlbp[:depth]
        sg[l] = dict(norm1_w=dnw1[0], norm2_w=dnw2[0], gdn_conv_w=dgcw[:4], gdn_a_log=dgpk[0, :4],
                     gdn_dt_bias=dgpk[1, :4], gdn_norm_w=dgpk[2], hgrn_norm_w=dhnw[0], ssd_conv_w=dscw[:4],
                     ssd_conv_b=dspv[0], ssd_norm_w=dspv[1, :512], ssd_a_log=dsps[0, :8], ssd_dt_bias=dsps[1, :8],
                     ssd_d=dsps[2, :8], ffn_conv_w=dcw[:3], ffn_conv_b=dcw[3])
        to_d2d = (l, [stacked_w_in])
    lp, stacked = to_d2d
    recv = exchange("rs_d2d_i%d" % lp, *_rs_d2d(stacked)[:4])
    qf, side = _rs_ici(stacked, recv, "i%d" % lp)
    finish([(lp, "w_in")], qf, exchange("rs_ici_i%d" % lp, *side[:4]), "i%d" % lp)
    grad_x = dx.reshape(nb, s, d)

    dmod = jnp.stack(dmod)
    (b_sum,) = elementwise("bias_rows", lambda *r: (functools.reduce(lambda p, q: p + q, r),),
                           [dmod[:, b].reshape(depth * 6, d) for b in range(nb)], [F32])
    per_layer = ("norm1_w", "norm2_w", "gdn_conv_w", "gdn_a_log", "gdn_dt_bias", "gdn_norm_w", "hgrn_norm_w",
                 "ssd_conv_w", "ssd_conv_b", "ssd_norm_w", "ssd_a_log", "ssd_dt_bias", "ssd_d", "ffn_conv_w", "ffn_conv_b")
    vals = {k: jnp.stack([sg[l][k] for l in range(depth)]) for k in per_layer}
    vals.update(loss=loss_part[0, :1], b_ada=b_sum.reshape(depth, 6 * d), hgrn_lb_param=d_lb, final_norm_w=d_final[0])
    gp = _Packer()
    for k, v in vals.items():
        gp.add(k, v.shape)
    packed8, dmod8 = allgather8("ag_grads", [gp.pack(vals), dmod.reshape(depth * nb, 6 * d)], [False, False])
    gs = gp.unpack(sum8("sum_small", packed8))
    loss = gs["loss"].reshape(())

    def my_cols(g):
        cs = g.shape[-1] // N_CHIPS
        return lax.dynamic_slice_in_dim(g, chip * cs, cs, axis=g.ndim - 1)

    for k in ("gdn_conv_w", "ssd_conv_w", "ffn_conv_w"):
        gs[k] = my_cols(gs[k])

    dmod_all = dmod8.reshape(N_DEV, depth, nb, 6 * d).transpose(1, 0, 2, 3).reshape(depth, N_DEV * nb, 6 * d)
    dmod_mine = lax.dynamic_slice_in_dim(dmod_all, chip * (6 * d // N_CHIPS), 6 * d // N_CHIPS, axis=2)
    g_w_ada = jnp.stack([matmul("ada_dw%d" % l, c_act, dmod_mine[l], "tn", tm=1024) for l in range(depth)])

    c_arr = lax.axis_index("c").astype(jnp.int32).reshape(1)
    grads, delta, new_m, new_v = {}, {}, {}, {}
    for i, (name, _) in enumerate(_BIG):
        shp = w[name].shape
        halves = lambda a: a.reshape((depth, 2) + reduced[(0, name)][0].shape)
        res = None
        for l in reversed(range(depth)):
            res = adamw_halves("adamw_%s%d" % (name, l), halves(w[name]), halves(mom[name]), halves(var[name]), l,
                               *reduced[(l, name)], c_arr, prev=res)
        grads[name], delta[name], new_m[name], new_v[name] = [r.reshape(shp) for r in res]
    grads["w_ada"] = g_w_ada
    for k in _SMALL:
        grads[k] = gs[k].reshape(w[k].shape)
    shp = w_ada.shape
    flat = lambda a: a.reshape(shp[0] * shp[1], shp[2])
    dl, nm, nv = adamw("adamw_w_ada", flat(w_ada), flat(g_w_ada), flat(m_w_ada), flat(v_w_ada))
    delta["w_ada"], new_m["w_ada"], new_v["w_ada"] = dl.reshape(shp), nm.reshape(shp), nv.reshape(shp)
    sp = _Packer()
    for k in _SMALL:
        sp.add(k, w[k].shape)
    dl, nm, nv = adamw("adamw_small", sp.pack(w), sp.pack(grads), sp.pack(mom), sp.pack(var))
    delta.update(sp.unpack(dl))
    new_m.update(sp.unpack(nm))
    new_v.update(sp.unpack(nv))

    return (loss, grad_x, *[grads[k] for k in _WEIGHTS], *[delta[k] for k in _WEIGHTS],
            *[new_m[k] for k in _WEIGHTS], *[new_v[k] for k in _WEIGHTS])
```

```python
import functools

import jax
import jax.numpy as jnp
from jax import lax
from jax.experimental import pallas as pl
from jax.experimental.pallas import tpu as pltpu

F32 = jnp.float32
BF16 = jnp.bfloat16
HI = lax.Precision.HIGHEST
MESH = pl.DeviceIdType.MESH

EPS = 1e-6
D_MODEL = 1024
GDN_HEADS, GDN_DK, GDN_CHUNK = 4, 128, 64
HGRN_HEADS, HGRN_DK, HGRN_CHUNK = 4, 128, 16
SSD_HEADS, SSD_P, SSD_GROUPS, SSD_STATE, SSD_CHUNK = 8, 64, 2, 128, 64
FFN_HIDDEN = 2816
N_CHIPS = 4
N_DEV = 8

ADAM_LR, ADAM_B1, ADAM_B2, ADAM_EPS, ADAM_WD, ADAM_STEP = 0.001, 0.9, 0.999, 1e-08, 0.01, 10

W_G, W_A, W_B, W_C = 3072, 2176, 2048, 1664
VMEM_LIMIT = 56 * 1024 * 1024


def _cparams(sem):
    return pltpu.CompilerParams(dimension_semantics=sem, vmem_limit_bytes=VMEM_LIMIT)


def _dg(a, b, ca, cb):
    return lax.dot_general(a.astype(BF16), b.astype(BF16), (((ca,), (cb,)), ((), ())),
                           preferred_element_type=F32)


@jax.custom_vjp
def bdot(a, b):
    return _dg(a, b, 1, 0)


bdot.defvjp(lambda a, b: (_dg(a, b, 1, 0), (a, b)),
            lambda r, g: (_dg(g, r[1], 1, 1), _dg(r[0], g, 0, 0)))


@jax.custom_vjp
def bdot_nt(a, b):
    return _dg(a, b, 1, 1)


bdot_nt.defvjp(lambda a, b: (_dg(a, b, 1, 1), (a, b)),
               lambda r, g: (_dg(g, r[1], 1, 0), _dg(g, r[0], 0, 0)))


@jax.custom_vjp
def bdot_tn(a, b):
    return _dg(a, b, 0, 0)


bdot_tn.defvjp(lambda a, b: (_dg(a, b, 0, 0), (a, b)),
               lambda r, g: (_dg(r[1], g, 1, 1), _dg(r[0], g, 1, 0)))


def _split(x, n):
    parts, rest = [], x
    for _ in range(n):
        p = rest.astype(BF16)
        parts.append(p)
        rest = rest - p.astype(F32)
    return parts


def _dgb(a, b, ca, cb):
    return lax.dot_general(a, b, (((ca,), (cb,)), ((), ())), preferred_element_type=F32)


def _dg3(a, b, ca, cb):
    (ah, al), (bh, bl) = _split(a, 2), _split(b, 2)
    return _dgb(jnp.concatenate([ah, ah, al], axis=ca), jnp.concatenate([bh, bl, bh], axis=cb), ca, cb)


@jax.custom_vjp
def hdot(a, b):
    return _dg3(a, b, 1, 0)


hdot.defvjp(lambda a, b: (_dg3(a, b, 1, 0), (a, b)),
            lambda r, g: (_dg3(g, r[1], 1, 1), _dg3(r[0], g, 0, 0)))


def _dge(e, x, ce, cx, e_first):
    eb = e.astype(BF16)
    es = jnp.concatenate([eb, eb, eb], axis=ce)
    xs = jnp.concatenate(_split(x, 3), axis=cx)
    return _dgb(es, xs, ce, cx) if e_first else _dgb(xs, es, cx, ce)


@jax.custom_vjp
def ldot(e, x):
    return _dge(e, x, 1, 0, True)


ldot.defvjp(lambda e, x: (_dge(e, x, 1, 0, True), e),
            lambda e, g: (jnp.zeros_like(e), _dge(e, g, 0, 0, True)))


@jax.custom_vjp
def rdot(x, e):
    return _dge(e, x, 0, 1, False)


rdot.defvjp(lambda x, e: (_dge(e, x, 0, 1, False), e),
            lambda e, g: (_dge(e, g, 1, 1, False), jnp.zeros_like(e)))


def _sigmoid(x):
    return 1.0 / (1.0 + jnp.exp(-x))


def _silu(x):
    return x * _sigmoid(x)


def _softplus(x):
    return jnp.maximum(x, 0.0) + jnp.log(1.0 + jnp.exp(-jnp.abs(x)))


def _rms(x, w):
    return x * lax.rsqrt(jnp.mean(x * x, axis=-1, keepdims=True) + EPS) * w


def _iota(shape, dim):
    return lax.broadcasted_iota(jnp.int32, shape, dim)


def _tri_ones(n, chunk, kind):
    i, j = _iota((n, n), 0), _iota((n, n), 1)
    same = lax.div(i, chunk) == lax.div(j, chunk)
    if kind == "incl":
        m = same & (j <= i)
    elif kind == "strict":
        m = same & (j < i)
    elif kind == "all":
        m = same
    else:
        m = same & (lax.rem(j, chunk) < (chunk // 2))
    return m


def _causal_conv(w, halo, x, width):
    r = x.shape[0]
    xin = jnp.concatenate([halo, x], axis=0)
    y = w[width - 1:width, :] * x
    for k in range(width - 1):
        off = 8 - (width - 1) + k
        y = y + w[k:k + 1, :] * xin[off:off + r, :]
    return y


def _each(fn, *lists):
    return [fn(*a) for a in zip(*lists)]


def _neumann(ms):
    n = ms[0].shape[0]
    eye = (_iota((n, n), 0) == _iota((n, n), 1)).astype(F32)
    accs = [eye - m for m in ms]
    ps = ms
    steps = 1
    while steps * 2 < n:
        ps = _each(hdot, ps, ps)
        accs = [acc + ap for acc, ap in zip(accs, _each(hdot, accs, ps))]
        steps *= 2
    return accs


@jax.custom_vjp
def tri_inverse(ms):
    return _neumann(ms)


def _tri_inverse_fwd(ms):
    ainvs = _neumann(ms)
    return ainvs, ainvs


def _tri_inverse_bwd(ainvs, gs):
    t = _each(lambda g, a: _dg3(g, a, 1, 1), gs, ainvs)
    return ([-x for x in _each(lambda a, y: _dg3(a, y, 0, 0), ainvs, t)],)


tri_inverse.defvjp(_tri_inverse_fwd, _tri_inverse_bwd)


def gdn_tile(params, state, ins, halos):
    conv_w, pk = params
    (pa,), (ha,) = ins, halos
    r = pa.shape[0]
    c, nh, dk = GDN_CHUNK, GDN_HEADS, GDN_DK
    kw = nh * dk
    qkv = _silu(_causal_conv(conv_w, ha[:, :3 * kw], pa[:, :3 * kw], 4))
    z = pa[:, 3 * kw:4 * kw]
    gsm = pa[:, 4 * kw:]
    a_log, dtb, nw = pk[0:1, :], pk[1:2, :], pk[2:3, :]
    g_all = -jnp.exp(a_log) * _softplus(gsm + dtb)
    beta_all = _sigmoid(gsm)
    incl = _tri_ones(c, c, "incl")
    strict = _tri_ones(c, c, "strict")
    lmat = incl.astype(F32)
    scale = dk ** -0.5
    nck = r // c
    inst = [(ci, h) for ci in range(nck) for h in range(nh)]

    def l2n(v):
        return v * lax.rsqrt(jnp.sum(v * v, axis=-1, keepdims=True) + EPS)

    gcs = [ldot(lmat, g_all[ci * c:(ci + 1) * c, :]) for ci in range(nck)]
    gcts = [g.T for g in gcs]
    g_col = [gcs[ci][:, h:h + 1] for ci, h in inst]
    g_row = [gcts[ci][h:h + 1, :] for ci, h in inst]
    g_last = [gcs[ci][c - 1:c, h:h + 1] for ci, h in inst]
    beta = [beta_all[ci * c:(ci + 1) * c, nh + h:nh + h + 1] for ci, h in inst]
    qh = [l2n(qkv[ci * c:(ci + 1) * c, h * dk:(h + 1) * dk]) for ci, h in inst]
    kh = [l2n(qkv[ci * c:(ci + 1) * c, kw + h * dk:kw + (h + 1) * dk]) for ci, h in inst]
    vh = [qkv[ci * c:(ci + 1) * c, 2 * kw + h * dk:2 * kw + (h + 1) * dk] for ci, h in inst]
    decay = [jnp.where(incl, jnp.exp(jnp.where(incl, gc_ - gr_, 0.0)), 0.0) for gc_, gr_ in zip(g_col, g_row)]
    kb = [k * b for k, b in zip(kh, beta)]
    qs = [q * scale for q in qh]
    kk = _each(lambda a, b, k: bdot_nt(jnp.concatenate([a, b], axis=0), k), kb, qs, kh)
    ms = [jnp.where(strict, x[:c] * d, 0.0) for x, d in zip(kk, decay)]
    attn = [x[c:] * d for x, d in zip(kk, decay)]
    ainv = tri_inverse(ms)
    eg = [jnp.exp(g) for g in g_col]
    rhs = [jnp.concatenate([v * b, k_ * e], axis=1) for v, b, k_, e in zip(vh, beta, kb, eg)]
    sol = _each(hdot, ainv, rhs)
    qg = [q * e for q, e in zip(qs, eg)]
    k_end = [k * jnp.exp(gl - g) for k, gl, g in zip(kh, g_last, g_col)]
    e_last = [jnp.exp(gl) for gl in g_last]

    st = [state[h * dk:(h + 1) * dk, :] for h in range(nh)]
    outs = [[] for _ in range(nh)]
    for ci in range(nck):
        idx = [ci * nh + h for h in range(nh)]
        ws = [bdot(jnp.concatenate([sol[i][:, dk:], qg[i]], axis=0), st[h]) for h, i in enumerate(idx)]
        v_new = [sol[i][:, :dk] - w_[:c] for i, w_ in zip(idx, ws)]
        av = [bdot(attn[i], v) for i, v in zip(idx, v_new)]
        kv = [bdot_tn(k_end[i], v) for i, v in zip(idx, v_new)]
        for h, i in enumerate(idx):
            o = ws[h][c:] + av[h]
            st[h] = st[h] * e_last[i] + kv[h]
            outs[h].append(_rms(o, nw) * _silu(z[ci * c:(ci + 1) * c, h * dk:(h + 1) * dk]))
    out = jnp.concatenate([jnp.concatenate(o, axis=0) for o in outs], axis=1)
    return jnp.concatenate(st, axis=0), [out]


def make_hgrn_tile(layer, depth):
    def hgrn_tile(params, state, ins, halos):
        lbp, nwp = params
        (pb,) = ins
        r = pb.shape[0]
        c = HGRN_CHUNK
        kw = HGRN_HEADS * HGRN_DK
        rows = [lbp[i:i + 1, :] for i in range(depth)]
        mx = functools.reduce(jnp.maximum, rows)
        ex = [jnp.exp(x - mx) for x in rows]
        den = functools.reduce(lambda a, b: a + b, ex)
        soft = [e / den for e in ex]
        lb = functools.reduce(lambda a, b: a + b, soft[:layer + 1]) - soft[0]
        nw = nwp[0:1, :]
        q = _silu(pb[:, :kw])
        fr = pb[:, kw:2 * kw]
        logf = jnp.log(lb + (1.0 - lb) * _sigmoid(fr))
        k = (1.0 - lb) * _sigmoid(-fr)
        v = pb[:, 2 * kw:3 * kw]
        gate = pb[:, 3 * kw:]
        incl = _tri_ones(r, c, "incl")
        masks = jnp.concatenate([incl.astype(F32), _tri_ones(r, c, "upto").astype(F32),
                                 _tri_ones(r, c, "all").astype(F32)], axis=0)
        sums = ldot(masks, logf)
        g_cum, g_ref, g_end = sums[:r], sums[r:2 * r], sums[2 * r:]
        qs = q * jnp.exp(g_cum - g_ref)
        ks = k * jnp.exp(g_ref - g_cum)
        qg = q * jnp.exp(g_cum)
        k_end = k * jnp.exp(g_end - g_cum)
        e_end = jnp.exp(g_end)
        sls = [slice(h * HGRN_DK, (h + 1) * HGRN_DK) for h in range(HGRN_HEADS)]
        attn = [jnp.where(incl, bdot_nt(qs[:, sl], ks[:, sl]), 0.0) for sl in sls]
        o_intra = [bdot(a, v[:, sl]) for a, sl in zip(attn, sls)]
        nsub, dk = r // c, HGRN_DK
        own_block = lax.div(_iota((r, nsub * dk), 0), c) == lax.div(_iota((r, nsub * dk), 1), dk)

        def spread(a):
            return jnp.where(own_block, jnp.concatenate([a] * nsub, axis=1), 0.0)

        kv = [bdot_tn(v[:, sl], spread(k_end[:, sl])) for sl in sls]
        s_t = [state[sl, :] for sl in sls]
        entry = [[] for _ in sls]
        for j in range(nsub):
            for lst, s_h in zip(entry, s_t):
                lst.append(s_h)
            s_t = [s_h * e_end[j * c:j * c + 1, sl] + x[:, j * dk:(j + 1) * dk] for s_h, sl, x in zip(s_t, sls, kv)]
        o_inter = [bdot_nt(spread(qg[:, sl]), jnp.concatenate(e, axis=1)) for sl, e in zip(sls, entry)]
        outs = [_rms(oa + ob, nw) * _silu(gate[:, sl]) for oa, ob, sl in zip(o_intra, o_inter, sls)]
        return jnp.concatenate(s_t, axis=0), [jnp.concatenate(outs, axis=1)]
    return hgrn_tile


def ssd_tile(params, state, ins, halos):
    conv_w, pv, ps = params
    (pc,), (hc,) = ins, halos
    r = pc.shape[0]
    c = SSD_CHUNK
    inner = SSD_HEADS * SSD_P
    gw = inner // SSD_GROUPS
    z = pc[:, :inner]
    xbc = _silu(_causal_conv(conv_w, hc[:, inner:inner + 1024], pc[:, inner:inner + 1024], 4) + pv[0:1, :])
    ssm = pc[:, inner + 1024:]
    xs = xbc[:, :inner]
    bm = xbc[:, inner:inner + SSD_GROUPS * SSD_STATE]
    cm = xbc[:, inner + SSD_GROUPS * SSD_STATE:]
    a_log, dtb, dsk = ps[0:1, :], ps[1:2, :], ps[2:3, :]
    nw = pv[1:2, :inner]
    dt = _softplus(ssm + dtb)
    da = dt * (-jnp.exp(a_log))
    expand = (lax.div(_iota((128, inner), 1), SSD_P) == _iota((128, inner), 0)).astype(F32)
    xdt = xs * rdot(dt, expand)
    d_e = rdot(jnp.concatenate([dsk] * 8, axis=0), expand)[0:1, :]
    incl = _tri_ones(c, c, "incl")
    lmat = incl.astype(F32)
    st = [state[g * SSD_STATE:(g + 1) * SSD_STATE, :] for g in range(SSD_GROUPS)]
    hpg = SSD_HEADS // SSD_GROUPS
    nck = r // c
    groups = range(SSD_GROUPS)
    cg = [(ci, g) for ci in range(nck) for g in groups]
    rows = [slice(ci * c, (ci + 1) * c) for ci in range(nck)]
    gls = [slice(g * gw, (g + 1) * gw) for g in groups]
    acs = [ldot(lmat, da[rs, :]) for rs in rows]
    acs_t = [a.T for a in acs]
    acs_e = [rdot(a, expand) for a in acs]
    last_e = [a[c - 1:c, :] for a in acs_e]
    bm_g = [bm[rows[ci], g * SSD_STATE:(g + 1) * SSD_STATE] for ci, g in cg]
    cm_g = [cm[rows[ci], g * SSD_STATE:(g + 1) * SSD_STATE] for ci, g in cg]
    cb = _each(bdot_nt, cm_g, bm_g)
    heads = [(i, ci, g * hpg + hg) for i, (ci, g) in enumerate(cg) for hg in range(hpg)]
    seg = [jnp.where(incl, jnp.exp(jnp.where(incl, acs[ci][:, hh:hh + 1] - acs_t[ci][hh:hh + 1, :], 0.0)), 0.0)
           for _, ci, hh in heads]
    yd = [bdot(cb[i] * sg, xdt[rows[ci], hh * SSD_P:(hh + 1) * SSD_P]) for (i, ci, hh), sg in zip(heads, seg)]
    y_diag = [jnp.concatenate(yd[i * hpg:(i + 1) * hpg], axis=1) for i in range(len(cg))]
    xw = [xdt[rows[ci], gls[g]] * jnp.exp(last_e[ci][:, gls[g]] - acs_e[ci][:, gls[g]]) for ci, g in cg]
    e_acs = [jnp.exp(acs_e[ci][:, gls[g]]) for ci, g in cg]
    e_last = [jnp.exp(last_e[ci][:, gls[g]]) for ci, g in cg]
    kv = _each(bdot_tn, bm_g, xw)
    ys = []
    for ci in range(nck):
        idx = [ci * SSD_GROUPS + g for g in groups]
        y_off = [bdot(cm_g[i], st[g]) * e_acs[i] for g, i in zip(groups, idx)]
        st = [st[g] * e_last[i] + kv[i] for g, i in zip(groups, idx)]
        ys.append(jnp.concatenate([y_diag[i] + yo for i, yo in zip(idx, y_off)], axis=1))
    y = jnp.concatenate(ys, axis=0) + d_e * xs
    yz = y * _silu(z)
    out = jnp.concatenate([_rms(yz[:, g * gw:(g + 1) * gw], nw[:, g * gw:(g + 1) * gw])
                           for g in range(SSD_GROUPS)], axis=1)
    return jnp.concatenate(st, axis=0), [out]


def convglu_tile(params, state, ins, halos):
    (cw,) = params
    (u,), (hu,) = ins, halos
    y = _causal_conv(cw, hu, u, 3) + cw[3:4, :]
    return None, [_silu(y[:, :FFN_HIDDEN]) * y[:, FFN_HIDDEN:]]


def convglu_tile_t(params, state, ins, halos):
    _, (act,) = convglu_tile(params, state, ins, halos)
    return None, [act, act.T]


def _halo_map(nt, r):
    return lambda b, n: (jnp.maximum((b * nt + n) * (r // 8) - 1, 0), 0)


def _exchange_copies(plan, local_plan, in_refs, out_refs, send_sems, recv_sems, local_sems):
    x, y, c = lax.axis_index("x"), lax.axis_index("y"), lax.axis_index("c")
    copies = []
    for k, fn in enumerate(plan):
        src, dst, peer = fn(in_refs, out_refs, x, y, c)
        copies.append(pltpu.make_async_remote_copy(src_ref=src, dst_ref=dst, send_sem=send_sems.at[k],
                                                   recv_sem=recv_sems.at[k], device_id=peer, device_id_type=MESH))
    for k, fn in enumerate(local_plan):
        src, dst = fn(in_refs, out_refs, x, y, c)
        copies.append(pltpu.make_async_copy(src, dst, local_sems.at[k]))
    return copies


def _exchange_sems(plan, local_plan):
    return [pltpu.SemaphoreType.DMA((max(len(plan), 1),)), pltpu.SemaphoreType.DMA((max(len(plan), 1),)),
            pltpu.SemaphoreType.DMA((max(len(local_plan), 1),))]


def _host_exchange(side, body, in_specs, o_specs, out_shape, scratch, args, grid):
    s_ins, s_shapes, plan, local_plan, then = side
    n_in, n_out, n_scr = len(in_specs), len(o_specs), len(scratch)
    k_in, k_out = len(s_ins), len(s_shapes)
    any_spec = pl.BlockSpec(memory_space=pl.ANY)

    def hosted(*refs):
        own_in, s_in = refs[:n_in], refs[n_in:n_in + k_in]
        o0 = n_in + k_in
        own_out, s_out = refs[o0:o0 + n_out], refs[o0 + n_out:o0 + n_out + k_out]
        rest = refs[o0 + n_out + k_out:]
        own_scr, sems, sems_then = rest[:n_scr], rest[n_scr:n_scr + 3], rest[n_scr + 3:]
        ids = [pl.program_id(a) for a in range(len(grid))]
        first = functools.reduce(lambda p, q: p & q, [i == 0 for i in ids])
        last = functools.reduce(lambda p, q: p & q, [i == g - 1 for i, g in zip(ids, grid)])

        @pl.when(first)
        def _():
            for cp in _exchange_copies(plan, local_plan, s_in, s_out, *sems):
                cp.start()

        body(*own_in, *own_out, *own_scr)

        @pl.when(last)
        def _():
            for cp in _exchange_copies(plan, local_plan, s_in, s_out, *sems):
                cp.wait()
            passed = _exchange_copies(then, (), s_in, s_out, *sems_then)
            for cp in passed:
                cp.start()
            for cp in passed:
                cp.wait()

    return (hosted, list(in_specs) + [any_spec] * k_in, list(o_specs) + [any_spec] * k_out,
            list(out_shape) + list(s_shapes),
            list(scratch) + _exchange_sems(plan, local_plan) + _exchange_sems(then, ()),
            list(args) + list(s_ins))


def seq_fwd(name, tile_fn, params, ins, use_halo, out_specs, state_shape, nb, s, r, side=None):
    nt = s // r
    n_p, n_i, n_o = len(params), len(ins), len(out_specs)
    has_state = state_shape is not None

    def body(*refs):
        p_refs, i_refs = refs[:n_p], refs[n_p:n_p + n_i]
        h_refs = refs[n_p + n_i:n_p + 2 * n_i] if use_halo else ()
        k = n_p + n_i + len(h_refs)
        o_refs = refs[k:k + n_o]
        n = pl.program_id(1)
        state = None
        if has_state:
            sv_ref, st_ref = refs[k + n_o], refs[k + n_o + 1]

            @pl.when(n == 0)
            def _():
                st_ref[...] = jnp.zeros(state_shape, F32)

            state = st_ref[...]
            sv_ref[0, 0] = state
        pv = [p[...] for p in p_refs]
        iv = [i[...].astype(F32) for i in i_refs]
        hv = [jnp.where(n > 0, h[...].astype(F32), 0.0) for h in h_refs]
        new_state, ov = tile_fn(pv, state, iv, hv)
        for o_ref, o in zip(o_refs, ov):
            o_ref[...] = o.astype(o_ref.dtype)
        if has_state:
            st_ref[...] = new_state

    row = lambda b, n: (b * nt + n, 0)
    in_specs = [pl.BlockSpec(p.shape, lambda b, n: (0, 0)) for p in params]
    in_specs += [pl.BlockSpec((r, a.shape[1]), row) for a in ins]
    if use_halo:
        in_specs += [pl.BlockSpec((8, a.shape[1]), _halo_map(nt, r)) for a in ins]
    col = lambda b, n: (0, b * nt + n)
    out_shape, o_specs = [], []
    for w, dt, *transposed in out_specs:
        out_shape.append(jax.ShapeDtypeStruct((w, nb * s) if transposed else (nb * s, w), dt))
        o_specs.append(pl.BlockSpec((w, r), col) if transposed else pl.BlockSpec((r, w), row))
    scratch = []
    if has_state:
        out_shape.append(jax.ShapeDtypeStruct((nb, nt) + tuple(state_shape), F32))
        o_specs.append(pl.BlockSpec((1, 1) + tuple(state_shape), lambda b, n: (b, n, 0, 0)))
        scratch.append(pltpu.VMEM(tuple(state_shape), F32))
    args = list(params) + list(ins) + (list(ins) if use_halo else [])
    if side is not None:
        body, in_specs, o_specs, out_shape, scratch, args = _host_exchange(
            side, body, in_specs, o_specs, out_shape, scratch, args, (nb, nt))
    return pl.pallas_call(body, grid=(nb, nt), in_specs=in_specs, out_specs=o_specs, out_shape=out_shape,
                          scratch_shapes=scratch, compiler_params=_cparams(("arbitrary", "arbitrary")),
                          name=name)(*args)


def seq_bwd(name, tile_fn, params, ins, use_halo, states, douts, din_dtypes, state_shape, nb, s, r, side=None):
    nt = s // r
    n_p, n_i, n_o = len(params), len(ins), len(douts)
    has_state = state_shape is not None

    def body(*refs):
        p_refs, i_refs = refs[:n_p], refs[n_p:n_p + n_i]
        h_refs = refs[n_p + n_i:n_p + 2 * n_i] if use_halo else ()
        k = n_p + n_i + len(h_refs)
        sv_ref = None
        if has_state:
            sv_ref = refs[k]
            k += 1
        do_refs = refs[k:k + n_o]
        k += n_o
        di_refs, dp_refs = refs[k:k + n_i], refs[k + n_i:k + n_i + n_p]
        k += n_i + n_p
        dst_ref = None
        if has_state:
            dst_ref = refs[k]
            k += 1
        dh_refs = refs[k:k + len(h_refs)]
        b, nn = pl.program_id(0), pl.program_id(1)
        n = nt - 1 - nn

        @pl.when((b == 0) & (nn == 0))
        def _():
            for dp in dp_refs:
                dp[...] = jnp.zeros(dp.shape, F32)

        @pl.when(nn == 0)
        def _():
            if has_state:
                dst_ref[...] = jnp.zeros(state_shape, F32)
            for dh in dh_refs:
                dh[...] = jnp.zeros(dh.shape, F32)

        pv = [p[...] for p in p_refs]
        iv = [i[...].astype(F32) for i in i_refs]
        hv = [jnp.where(n > 0, h[...].astype(F32), 0.0) for h in h_refs]
        if has_state:
            f = lambda pv_, st_, iv_, hv_: tile_fn(pv_, st_, iv_, hv_)
            _, vjp = jax.vjp(f, pv, sv_ref[0, 0], iv, hv)
            dpv, dst, div, dhv = vjp((dst_ref[...], [d[...].astype(F32) for d in do_refs]))
            dst_ref[...] = dst
        else:
            f = lambda pv_, iv_, hv_: tile_fn(pv_, None, iv_, hv_)[1]
            _, vjp = jax.vjp(f, pv, iv, hv)
            dpv, div, dhv = vjp([d[...].astype(F32) for d in do_refs])
        for j, (di_ref, d) in enumerate(zip(di_refs, div)):
            if use_halo:
                d = jnp.concatenate([d[:r - 8], d[r - 8:] + dh_refs[j][...]], axis=0)
            di_ref[...] = d.astype(di_ref.dtype)
        for dh_ref, d in zip(dh_refs, dhv):
            dh_ref[...] = d
        for dp_ref, d in zip(dp_refs, dpv):
            dp_ref[...] += d

    row = lambda b, nn: (b * nt + nt - 1 - nn, 0)
    hmap = _halo_map(nt, r)
    in_specs = [pl.BlockSpec(p.shape, lambda b, nn: (0, 0)) for p in params]
    in_specs += [pl.BlockSpec((r, a.shape[1]), row) for a in ins]
    if use_halo:
        in_specs += [pl.BlockSpec((8, a.shape[1]), lambda b, nn: hmap(b, nt - 1 - nn)) for a in ins]
    args = list(params) + list(ins) + (list(ins) if use_halo else [])
    scratch = []
    if has_state:
        in_specs.append(pl.BlockSpec((1, 1) + tuple(state_shape), lambda b, nn: (b, nt - 1 - nn, 0, 0)))
        args.append(states)
        scratch.append(pltpu.VMEM(tuple(state_shape), F32))
    in_specs += [pl.BlockSpec((r, d.shape[1]), row) for d in douts]
    args += list(douts)
    if use_halo:
        scratch += [pltpu.VMEM((8, a.shape[1]), F32) for a in ins]
    out_shape = [jax.ShapeDtypeStruct(a.shape, dt) for a, dt in zip(ins, din_dtypes)]
    out_shape += [jax.ShapeDtypeStruct(p.shape, F32) for p in params]
    o_specs = [pl.BlockSpec((r, a.shape[1]), row) for a in ins]
    o_specs += [pl.BlockSpec(p.shape, lambda b, nn: (0, 0)) for p in params]
    if side is not None:
        body, in_specs, o_specs, out_shape, scratch, args = _host_exchange(
            side, body, in_specs, o_specs, out_shape, scratch, args, (nb, nt))
    res = pl.pallas_call(body, grid=(nb, nt), in_specs=in_specs, out_specs=o_specs, out_shape=out_shape,
                         scratch_shapes=scratch, compiler_params=_cparams(("arbitrary", "arbitrary")),
                         name=name)(*args)
    if side is not None:
        return res[:n_i], res[n_i:n_i + n_p], res[n_i + n_p:]
    return res[:n_i], res[n_i:]


def matmul(name, a, b, mode, out_dtype=F32, addend=None, resid=None, tm=512, tn=None, tk=None):
    if mode == "nn":
        (m, kd), (_, n) = a.shape, b.shape
    elif mode == "nt":
        (m, kd), (n, _) = a.shape, b.shape
    else:
        (kd, m), (_, n) = a.shape, b.shape
    tm, tn, tk = min(tm, m if resid is None else resid[2]), min(tn or n, n), min(tk or kd, kd)
    nk = kd // tk
    assert m % tm == 0 and n % tn == 0 and kd % tk == 0
    dims = {"nn": ((1,), (0,)), "nt": ((1,), (1,)), "tn": ((0,), (0,))}[mode]
    extra = [] if addend is None else [addend]
    if resid is not None:
        extra = [resid[0], resid[1]]
    n_in, n_out = 2 + len(extra), 1 if resid is None else 2

    def body(*refs):
        a_ref, b_ref = refs[0], refs[1]
        part = lax.dot_general(a_ref[...].astype(BF16), b_ref[...].astype(BF16), (dims, ((), ())),
                               preferred_element_type=F32)

        def finish(acc):
            if resid is not None:
                refs[n_in][...] = refs[2][...] + refs[3][0] * acc
                refs[n_in + 1][...] = acc.astype(BF16)
            else:
                if addend is not None:
                    acc = acc + refs[2][...]
                refs[n_in][...] = acc.astype(refs[n_in].dtype)

        if nk == 1:
            finish(part)
        else:
            acc_ref = refs[n_in + n_out]
            k = pl.program_id(2)

            @pl.when(k == 0)
            def _():
                acc_ref[...] = part

            @pl.when(k > 0)
            def _():
                acc_ref[...] += part

            @pl.when(k == nk - 1)
            def _():
                finish(acc_ref[...])

    if mode == "tn":
        a_spec = pl.BlockSpec((tk, tm), lambda j, i, k: (k, i))
    else:
        a_spec = pl.BlockSpec((tm, tk), lambda j, i, k: (i, k))
    if mode == "nt":
        b_spec = pl.BlockSpec((tn, tk), lambda j, i, k: (j, k))
    else:
        b_spec = pl.BlockSpec((tk, tn), lambda j, i, k: (k, j))
    o_spec = pl.BlockSpec((tm, tn), lambda j, i, k: (i, j))
    in_specs = [a_spec, b_spec] + [o_spec] * (len(extra) > 0)
    out_specs, out_shape = o_spec, jax.ShapeDtypeStruct((m, n), out_dtype)
    if resid is not None:
        rows = resid[2]
        assert rows % tm == 0
        in_specs.append(pl.BlockSpec((1, 1, tn), lambda j, i, k: (lax.div(i * tm, rows), 0, j)))
        out_specs, out_shape = [o_spec, o_spec], [out_shape, jax.ShapeDtypeStruct((m, n), BF16)]
    scratch = [pltpu.VMEM((tm, tn), F32)] if nk > 1 else []
    return pl.pallas_call(body, grid=(n // tn, m // tm, nk), in_specs=in_specs, out_specs=out_specs,
                          out_shape=out_shape, scratch_shapes=scratch,
                          compiler_params=_cparams(("parallel", "parallel", "arbitrary")), name=name)(a, b, *extra)


def _normmod(x, nw, shift, scale):
    return _rms(x, nw) * (1.0 + scale) + shift


def _row_specs(nb, s, tr, d):
    nt = s // tr
    row = pl.BlockSpec((tr, d), lambda b, i: (b * nt + i, 0))
    per_seq = pl.BlockSpec((1, 1, d), lambda b, i: (b, 0, 0))
    full = pl.BlockSpec((1, d), lambda b, i: (0, 0))
    return nt, row, per_seq, full


def normmod_fwd(name, x, nw, shift, scale, nb, s, tr=512):
    d, tr = x.shape[1], min(tr, s)
    nt, row, per_seq, full = _row_specs(nb, s, tr, d)

    def body(x_ref, nw_ref, sh_ref, sc_ref, h_ref, ht_ref):
        h = _normmod(x_ref[...], nw_ref[...], sh_ref[0], sc_ref[0])
        h_ref[...] = h.astype(h_ref.dtype)
        ht_ref[...] = h.T.astype(ht_ref.dtype)

    return pl.pallas_call(body, grid=(nb, nt), in_specs=[row, full, per_seq, per_seq],
                          out_specs=[row, pl.BlockSpec((d, tr), lambda b, i: (0, b * nt + i))],
                          out_shape=[jax.ShapeDtypeStruct(x.shape, BF16), jax.ShapeDtypeStruct(x.shape[::-1], BF16)],
                          compiler_params=_cparams(("parallel", "parallel")), name=name)(x, nw, shift, scale)


def normmod_bwd(name, x, nw, shift, scale, dh, dres, nb, s, tr=512):
    d, tr = x.shape[1], min(tr, s)
    nt, row, per_seq, full = _row_specs(nb, s, tr, d)

    def body(x_ref, nw_ref, sh_ref, sc_ref, dh_ref, dres_ref, dx_ref, dnw_ref, dsh_ref, dsc_ref):
        b, i = pl.program_id(0), pl.program_id(1)

        @pl.when((b == 0) & (i == 0))
        def _():
            dnw_ref[...] = jnp.zeros(dnw_ref.shape, F32)

        @pl.when(i == 0)
        def _():
            dsh_ref[...] = jnp.zeros(dsh_ref.shape, F32)
            dsc_ref[...] = jnp.zeros(dsc_ref.shape, F32)

        _, vjp = jax.vjp(_normmod, x_ref[...], nw_ref[...], sh_ref[0], sc_ref[0])
        dx, dnw, dsh, dsc = vjp(dh_ref[...])
        dx_ref[...] = dres_ref[...] + dx
        dnw_ref[...] += dnw
        dsh_ref[0] += dsh
        dsc_ref[0] += dsc

    out_shape = [jax.ShapeDtypeStruct(x.shape, F32), jax.ShapeDtypeStruct((1, d), F32),
                 jax.ShapeDtypeStruct((nb, 1, d), F32), jax.ShapeDtypeStruct((nb, 1, d), F32)]
    return pl.pallas_call(body, grid=(nb, nt), in_specs=[row, full, per_seq, per_seq, row, row],
                          out_specs=[row, full, per_seq, per_seq], out_shape=out_shape,
                          compiler_params=_cparams(("arbitrary", "arbitrary")),
                          name=name)(x, nw, shift, scale, dh, dres)


def _merge_specs(nb, s, tr, d, wbr):
    nt, row, per_seq, _ = _row_specs(nb, s, tr, d)
    o_spec = pl.BlockSpec((tr, wbr), lambda b, i: (b * nt + i, 0))
    g_spec = pl.BlockSpec((tr, 3 * d), lambda b, i: (b * nt + i, 0))
    wbr_spec = pl.BlockSpec((wbr, d), lambda b, i: (0, 0))
    wo_spec = pl.BlockSpec((d, d), lambda b, i: (0, 0))
    return nt, row, per_seq, o_spec, g_spec, wbr_spec, wo_spec


def merge_fwd(name, x, oa, ob, oc, pg, gate1, wa, wb, wc, wo, nb, s, tr=512):
    d, tr = x.shape[1], min(tr, s)
    nt, row, per_seq, o_spec, g_spec, wbr_spec, wo_spec = _merge_specs(nb, s, tr, d, oa.shape[1])

    def body(x_ref, oa_ref, ob_ref, oc_ref, pg_ref, g1_ref, wa_ref, wb_ref, wc_ref, wo_ref,
             x1_ref, ya_ref, yb_ref, yc_ref, mg_ref, mx_ref):
        g = _sigmoid(pg_ref[...])
        ys = [_dg(o[...], w_[...], 1, 0) for o, w_ in ((oa_ref, wa_ref), (ob_ref, wb_ref), (oc_ref, wc_ref))]
        merged = g[:, :d] * ys[0] + g[:, d:2 * d] * ys[1] + g[:, 2 * d:] * ys[2]
        mix = _dg(merged, wo_ref[...], 1, 0)
        x1_ref[...] = x_ref[...] + g1_ref[0] * mix
        for r, v in zip((ya_ref, yb_ref, yc_ref, mg_ref, mx_ref), ys + [merged, mix]):
            r[...] = v.astype(r.dtype)

    return pl.pallas_call(body, grid=(nb, nt),
                          in_specs=[row, o_spec, o_spec, o_spec, g_spec, per_seq, wbr_spec, wbr_spec, wbr_spec, wo_spec],
                          out_specs=[row] * 6,
                          out_shape=[jax.ShapeDtypeStruct(x.shape, F32)] + [jax.ShapeDtypeStruct(x.shape, BF16)] * 5,
                          compiler_params=_cparams(("parallel", "parallel")),
                          name=name)(x, oa, ob, oc, pg, gate1, wa, wb, wc, wo)


def merge_bwd(name, oa, ob, oc, pg, gate1, wa, wb, wc, wo, ys, merged, mix, dx1, nb, s, tr=256):
    d, tr = dx1.shape[1], min(tr, s)
    wbr = oa.shape[1]
    nt, row, per_seq, o_spec, g_spec, wbr_spec, wo_spec = _merge_specs(nb, s, tr, d, wbr)

    def body(oa_ref, ob_ref, oc_ref, pg_ref, g1_ref, wa_ref, wb_ref, wc_ref, wo_ref, ya_ref, yb_ref, yc_ref, mg_ref,
             mx_ref, dx_ref, doa_ref, dob_ref, doc_ref, dpg_ref, dg1_ref, dwa_ref, dwb_ref, dwc_ref, dwo_ref):
        b, i = pl.program_id(0), pl.program_id(1)

        @pl.when((b == 0) & (i == 0))
        def _():
            for r in (dwa_ref, dwb_ref, dwc_ref, dwo_ref):
                r[...] = jnp.zeros(r.shape, F32)

        @pl.when(i == 0)
        def _():
            dg1_ref[...] = jnp.zeros(dg1_ref.shape, F32)

        dx = dx_ref[...]
        dg1_ref[0] += jnp.sum(dx * mx_ref[...].astype(F32), axis=0, keepdims=True)
        dmix = g1_ref[0] * dx
        dmerged = _dg(dmix, wo_ref[...], 1, 1)
        dwo_ref[...] += _dg(mg_ref[...], dmix, 0, 0)
        g = _sigmoid(pg_ref[...])
        branches = ((oa_ref, wa_ref, ya_ref, doa_ref, dwa_ref), (ob_ref, wb_ref, yb_ref, dob_ref, dwb_ref),
                    (oc_ref, wc_ref, yc_ref, doc_ref, dwc_ref))
        dgs = []
        for k, (o_ref, w_ref, y_ref, do_ref, dw_ref) in enumerate(branches):
            gk = g[:, k * d:(k + 1) * d]
            dy = dmerged * gk
            dgs.append(dmerged * y_ref[...].astype(F32) * gk * (1.0 - gk))
            do_ref[...] = _dg(dy, w_ref[...], 1, 1)
            dw_ref[...] += _dg(o_ref[...], dy, 0, 0)
        dpg_ref[...] = jnp.concatenate(dgs, axis=1)

    t = nb * s
    out_shape = ([jax.ShapeDtypeStruct((t, wbr), F32)] * 3
                 + [jax.ShapeDtypeStruct((t, 3 * d), F32), jax.ShapeDtypeStruct((nb, 1, d), F32)]
                 + [jax.ShapeDtypeStruct((wbr, d), F32)] * 3 + [jax.ShapeDtypeStruct((d, d), F32)])
    return pl.pallas_call(body, grid=(nb, nt),
                          in_specs=[o_spec, o_spec, o_spec, g_spec, per_seq, wbr_spec, wbr_spec, wbr_spec, wo_spec]
                          + [row] * 6,
                          out_specs=[o_spec, o_spec, o_spec, g_spec, per_seq, wbr_spec, wbr_spec, wbr_spec, wo_spec],
                          out_shape=out_shape, compiler_params=_cparams(("arbitrary", "arbitrary")),
                          name=name)(oa, ob, oc, pg, gate1, wa, wb, wc, wo, *ys, merged, mix, dx1)


def resid_bwd(name, dx, f, gate, nb, s, tr=512):
    d, tr = dx.shape[1], min(tr, s)
    nt, row, per_seq, _ = _row_specs(nb, s, tr, d)

    def body(dx_ref, f_ref, g_ref, df_ref, dg_ref):
        @pl.when(pl.program_id(1) == 0)
        def _():
            dg_ref[...] = jnp.zeros(dg_ref.shape, F32)

        df_ref[...] = (g_ref[0] * dx_ref[...]).astype(df_ref.dtype)
        dg_ref[0] += jnp.sum(dx_ref[...] * f_ref[...], axis=0, keepdims=True)

    return pl.pallas_call(body, grid=(nb, nt), in_specs=[row, row, per_seq], out_specs=[row, per_seq],
                          out_shape=[jax.ShapeDtypeStruct(dx.shape, BF16), jax.ShapeDtypeStruct((nb, 1, d), F32)],
                          compiler_params=_cparams(("arbitrary", "arbitrary")), name=name)(dx, f, gate)


def loss_head(name, x, fw, target, tr=512):
    t, d = x.shape
    row = pl.BlockSpec((tr, d), lambda i: (i, 0))
    full = pl.BlockSpec((1, d), lambda i: (0, 0))

    def loss_fn(xv, fwv, tv):
        err = _rms(xv, fwv) - tv
        return 0.5 * jnp.sum(jnp.mean(err * err, axis=-1))

    def body(x_ref, fw_ref, t_ref, dx_ref, l_ref, dfw_ref):
        @pl.when(pl.program_id(0) == 0)
        def _():
            l_ref[...] = jnp.zeros(l_ref.shape, F32)
            dfw_ref[...] = jnp.zeros(dfw_ref.shape, F32)

        val, (dx, dfw) = jax.value_and_grad(loss_fn, argnums=(0, 1))(x_ref[...], fw_ref[...], t_ref[...])
        dx_ref[...] = dx
        l_ref[...] += val
        dfw_ref[...] += dfw

    return pl.pallas_call(body, grid=(t // tr,), in_specs=[row, full, row],
                          out_specs=[row, pl.BlockSpec((1, 128), lambda i: (0, 0)), full],
                          out_shape=[jax.ShapeDtypeStruct((t, d), F32), jax.ShapeDtypeStruct((1, 128), F32),
                                     jax.ShapeDtypeStruct((1, d), F32)],
                          compiler_params=_cparams(("arbitrary",)), name=name)(x, fw, target)


def _row_tile(rows, cols, n_arrays):
    budget = 24 * 1024 * 1024 // (8 * cols * max(n_arrays, 1))
    tr = rows
    while tr > max(budget, 16) and tr % 2 == 0 and (tr // 2) % 16 == 0:
        tr //= 2
    return tr


def elementwise(name, fn, ins, out_dtypes):
    rows, cols = ins[0].shape
    tr = _row_tile(rows, cols, len(ins) + len(out_dtypes))
    spec = pl.BlockSpec((tr, cols), lambda i: (i, 0))
    n_in = len(ins)

    def body(*refs):
        outs = fn(*[r[...] for r in refs[:n_in]])
        for o_ref, o in zip(refs[n_in:], outs):
            o_ref[...] = o.astype(o_ref.dtype)

    return pl.pallas_call(body, grid=(rows // tr,), in_specs=[spec] * n_in, out_specs=[spec] * len(out_dtypes),
                          out_shape=[jax.ShapeDtypeStruct((rows, cols), dt) for dt in out_dtypes],
                          compiler_params=_cparams(("parallel",)), name=name)(*ins)


def _adamw(w, g, m, v):
    m = ADAM_B1 * m + (1.0 - ADAM_B1) * g
    v = ADAM_B2 * v + (1.0 - ADAM_B2) * (g * g)
    m_hat = m / (1.0 - ADAM_B1 ** ADAM_STEP)
    v_hat = v / (1.0 - ADAM_B2 ** ADAM_STEP)
    delta = -ADAM_LR * (m_hat / (jnp.sqrt(v_hat) + ADAM_EPS) + ADAM_WD * w)
    return delta, m, v


def adamw(name, w, g, m, v):
    return elementwise(name, _adamw, [w, g, m, v], [F32, F32, F32])


_ANY = pl.BlockSpec(memory_space=pl.ANY)


def _coords():
    return lax.axis_index("x"), lax.axis_index("y"), lax.axis_index("c")


def allgather8(name, arrays, halves):
    n = len(arrays)

    def body(*refs):
        in_refs, out_refs = refs[:n], refs[n:2 * n]
        send_sems, recv_sems, local_sems = refs[2 * n:]
        x, y, c = _coords()
        me, sibling = (x, y, c), (x, y, 1 - c)
        chips = [(1 - x, y), (x, 1 - y), (1 - x, 1 - y)]

        def blk(i, px, py, pc):
            return out_refs[i].at[4 * px + 2 * py + pc]

        def piece(i):
            return in_refs[i].at[c] if halves[i] else in_refs[i]

        def copy(i, k, block, to, src=None):
            return pltpu.make_async_remote_copy(
                src_ref=blk(i, *block) if src is None else src, dst_ref=blk(i, *block),
                send_sem=send_sems.at[7 * i + k], recv_sem=recv_sems.at[7 * i + k],
                device_id=to, device_id_type=MESH)

        mine = [pltpu.make_async_copy(piece(i), blk(i, *me), local_sems.at[i]) for i in range(n)]
        for cp in mine:
            cp.start()
        first = []
        for i in range(n):
            first.append(copy(i, 0, me, sibling, src=piece(i)))
            first += [copy(i, 1 + j, me, (*chip, c), src=piece(i)) for j, chip in enumerate(chips)]
        for cp in first:
            cp.start()
        passed = []
        for j, chip in enumerate(chips):
            for i in range(n):
                copy(i, 1 + j, (*chip, c), me).wait_recv()
                fwd = copy(i, 4 + j, (*chip, c), sibling)
                fwd.start()
                passed.append(fwd)
        for i in range(n):
            copy(i, 0, sibling, me).wait_recv()
            for j, chip in enumerate(chips):
                copy(i, 4 + j, (*chip, 1 - c), me).wait_recv()
        for cp in first + passed:
            cp.wait_send()
        for cp in mine:
            cp.wait()

    out_shape = []
    for a, hv in zip(arrays, halves):
        out_shape.append(jax.ShapeDtypeStruct((N_DEV,) + tuple(a.shape[1:] if hv else a.shape), a.dtype))
    return pl.pallas_call(
        body, in_specs=[_ANY] * n, out_specs=[_ANY] * n, out_shape=out_shape,
        scratch_shapes=[pltpu.SemaphoreType.DMA((7 * n,)), pltpu.SemaphoreType.DMA((7 * n,)),
                        pltpu.SemaphoreType.DMA((n,))],
        name=name)(*arrays)


def exchange(name, ins, out_shapes, plan, local_plan=()):
    n_in, n_out = len(ins), len(out_shapes)

    def body(*refs):
        copies = _exchange_copies(plan, local_plan, refs[:n_in], refs[n_in:n_in + n_out], *refs[n_in + n_out:])
        for cp in copies:
            cp.start()
        for cp in copies:
            cp.wait()

    return pl.pallas_call(
        body, in_specs=[_ANY] * n_in, out_specs=[_ANY] * n_out, out_shape=out_shapes,
        scratch_shapes=_exchange_sems(plan, local_plan), name=name)(*ins)


def sum_halves(name, gs, recv, c_arr):
    _, _, hr, cs = gs.shape
    tr = _row_tile(hr, cs, 4)

    def body(c_ref, g_ref, r_ref, qf_ref, qb_ref):
        q = g_ref[0, 0] + r_ref[0, 0]
        qf_ref[0] = q
        qb_ref[0] = q.astype(BF16)

    grid_spec = pltpu.PrefetchScalarGridSpec(
        num_scalar_prefetch=1, grid=(N_CHIPS, hr // tr),
        in_specs=[pl.BlockSpec((1, 1, tr, cs), lambda j, i, c_ref: (j, c_ref[0], i, 0)),
                  pl.BlockSpec((1, 1, tr, cs), lambda j, i, c_ref: (j, 0, i, 0))],
        out_specs=[pl.BlockSpec((1, tr, cs), lambda j, i, c_ref: (j, i, 0))] * 2)
    return pl.pallas_call(body, grid_spec=grid_spec,
                          out_shape=[jax.ShapeDtypeStruct((N_CHIPS, hr, cs), F32),
                                     jax.ShapeDtypeStruct((N_CHIPS, hr, cs), BF16)],
                          compiler_params=_cparams(("parallel", "parallel")), name=name)(c_arr, gs, recv)


def sum_chips(name, qf, recv, chip_arr):
    _, hr, cs = qf.shape
    tr = _row_tile(hr, cs, 4)

    def body(chip_ref, q_ref, a_ref, b_ref, c_ref, o_ref):
        o_ref[...] = q_ref[0] + a_ref[...].astype(F32) + b_ref[...].astype(F32) + c_ref[...].astype(F32)

    row = pl.BlockSpec((tr, cs), lambda i, chip_ref: (i, 0))
    grid_spec = pltpu.PrefetchScalarGridSpec(
        num_scalar_prefetch=1, grid=(hr // tr,),
        in_specs=[pl.BlockSpec((1, tr, cs), lambda i, chip_ref: (chip_ref[0], i, 0)), row, row, row],
        out_specs=row)
    return pl.pallas_call(body, grid_spec=grid_spec, out_shape=jax.ShapeDtypeStruct((hr, cs), F32),
                          compiler_params=_cparams(("parallel",)), name=name)(chip_arr, qf, *recv)


def adamw_halves(name, w, m, v, layer, g_mine, g_other, c_arr, prev=None):
    _, _, hr, cs = w.shape
    tr = _row_tile(hr, cs, 9)

    def body(c_ref, w_ref, m_ref, v_ref, gm_ref, go_ref, *rest):
        g_ref, d_ref, nm_ref, nv_ref = rest[-4:]
        g = jnp.where(pl.program_id(0) == c_ref[0], gm_ref[...], go_ref[...])
        delta, nm, nv = _adamw(w_ref[0, 0], g, m_ref[0, 0], v_ref[0, 0])
        g_ref[0, 0], d_ref[0, 0], nm_ref[0, 0], nv_ref[0, 0] = g, delta, nm, nv

    half = pl.BlockSpec((1, 1, tr, cs), lambda h, i, c_ref: (layer, h, i, 0))
    row = pl.BlockSpec((tr, cs), lambda h, i, c_ref: (i, 0))
    in_specs, args, aliases = [half, half, half, row, row], [c_arr, w, m, v, g_mine, g_other], {}
    if prev is not None:
        in_specs += [pl.BlockSpec(memory_space=pl.ANY)] * 4
        args += list(prev)
        aliases = {6 + k: k for k in range(4)}
    grid_spec = pltpu.PrefetchScalarGridSpec(num_scalar_prefetch=1, grid=(2, hr // tr),
                                             in_specs=in_specs, out_specs=[half] * 4)
    return pl.pallas_call(body, grid_spec=grid_spec, out_shape=[jax.ShapeDtypeStruct(w.shape, F32)] * 4,
                          input_output_aliases=aliases, compiler_params=_cparams(("parallel", "parallel")),
                          name=name)(*args)


def sum8(name, g):
    _, rows, cols = g.shape
    tr = _row_tile(rows, cols, 9)

    def body(*refs):
        acc = refs[0][0]
        for r in refs[1:N_DEV]:
            acc = acc + r[0]
        refs[N_DEV][...] = acc

    in_specs = [pl.BlockSpec((1, tr, cols), functools.partial(lambda k, i: (k, i, 0), k)) for k in range(N_DEV)]
    return pl.pallas_call(body, grid=(rows // tr,), in_specs=in_specs,
                          out_specs=pl.BlockSpec((tr, cols), lambda i: (i, 0)),
                          out_shape=jax.ShapeDtypeStruct((rows, cols), F32),
                          compiler_params=_cparams(("parallel",)), name=name)(*([g] * N_DEV))


_QKV, _AB, _GZ = (0, 1536), (1536, 1544), (1544, 2056)
_HG = (2056, 4104)
_SZ, _XBC, _DT = (4104, 4616), (4616, 5640), (5640, 5648)
_GATES = (5648, 8720)


def _split_w_in(w8):
    _, hr, cs = w8.shape
    w4 = w8.reshape(N_CHIPS, 2 * hr, cs)

    def cols(rng):
        lo, hi = rng
        return [w4[j][:, max(lo, j * cs) - j * cs:min(hi, (j + 1) * cs) - j * cs]
                for j in range(N_CHIPS) if max(lo, j * cs) < min(hi, (j + 1) * cs)]

    pad = [jnp.zeros((2 * hr, 120), w8.dtype)]
    return (jnp.concatenate(cols(_GATES), axis=1),
            jnp.concatenate(cols(_QKV) + cols(_GZ) + cols(_AB) + pad, axis=1),
            jnp.concatenate(cols(_HG), axis=1),
            jnp.concatenate(cols(_SZ) + cols(_XBC) + cols(_DT) + pad, axis=1))


def _stack_w_in(g, a, b, c):
    segments = [(a, 0, 1536), (a, 2048, 2056), (a, 1536, 2048), (b, 0, 2048), (c, 0, 512), (c, 512, 1536),
                (c, 1536, 1544), (g, 0, 3072)]
    cs = sum(s1 - s0 for _, s0, s1 in segments) // N_CHIPS
    chips = []
    for j in range(N_CHIPS):
        parts, off = [], 0
        for arr, s0, s1 in segments:
            u0, u1 = max(j * cs, off), min((j + 1) * cs, off + s1 - s0)
            if u0 < u1:
                parts.append(arr[:, s0 + u0 - off:s0 + u1 - off])
            off += s1 - s0
        chips.append(jnp.concatenate(parts, axis=1))
    rows = g.shape[0]
    return jnp.stack(chips).reshape(N_CHIPS, 2, rows // 2, cs)


def _rows8(rows, width):
    out = [jnp.pad(r.astype(F32), (0, width - r.shape[0])) for r in rows]
    out += [jnp.zeros((width,), F32)] * (8 - len(out))
    return jnp.stack(out)


class _Packer:
    def __init__(self):
        self.items, self.size = [], 0

    def add(self, name, shape):
        n = 1
        for d in shape:
            n *= d
        self.items.append((name, tuple(shape), self.size, n))
        self.size += n

    def rows(self):
        return -(-self.size // 8192) * 8

    def pack(self, values):
        flat = [values[name].astype(F32).reshape(-1) for name, _, _, _ in self.items]
        flat.append(jnp.zeros((self.rows() * 1024 - self.size,), F32))
        return jnp.concatenate(flat).reshape(self.rows(), 1024)

    def unpack(self, buf):
        flat = buf.reshape(-1)
        return {name: flat[off:off + n].reshape(shape) for name, shape, off, n in self.items}


def _stack_by_chip(g, axis):
    l, r, c = g.shape
    if axis == 2:
        cs = c // N_CHIPS
        g = g.reshape(l, r, N_CHIPS, cs).transpose(2, 0, 1, 3).reshape(N_CHIPS, 2, l * r // 2, cs)
    else:
        rs = r // N_CHIPS
        g = g.reshape(l, N_CHIPS, rs, c).transpose(1, 0, 2, 3).reshape(N_CHIPS, 2, l * rs // 2, c)
    return g


def _unstack_gathered(w8, l, axis):
    _, hr, cs = w8.shape
    w = w8.reshape(N_CHIPS, l, 2 * hr // l, cs)
    if axis == 2:
        return w.transpose(1, 2, 0, 3).reshape(l, 2 * hr // l, N_CHIPS * cs)
    return w.transpose(1, 0, 2, 3).reshape(l, N_CHIPS * 2 * hr // l, cs)


_BIG = (("w_in", 2), ("w_br_a", 2), ("w_br_b", 2), ("w_br_c", 2), ("w_out", 1), ("ffn_w_up", 2), ("ffn_w_down", 1))
_SMALL = ("b_ada", "norm1_w", "gdn_conv_w", "gdn_a_log", "gdn_dt_bias", "gdn_norm_w", "hgrn_lb_param",
          "hgrn_norm_w", "ssd_conv_w", "ssd_conv_b", "ssd_a_log", "ssd_dt_bias", "ssd_d", "ssd_norm_w",
          "norm2_w", "ffn_conv_w", "ffn_conv_b", "final_norm_w")
_WEIGHTS = ("w_ada", "b_ada", "norm1_w", "w_in", "gdn_conv_w", "gdn_a_log", "gdn_dt_bias", "gdn_norm_w",
            "hgrn_lb_param", "hgrn_norm_w", "ssd_conv_w", "ssd_conv_b", "ssd_a_log", "ssd_dt_bias", "ssd_d",
            "ssd_norm_w", "w_br_a", "w_br_b", "w_br_c", "w_out", "norm2_w", "ffn_w_up", "ffn_conv_w",
            "ffn_conv_b", "ffn_w_down", "final_norm_w")
_R_GDN, _R_HGRN, _R_SSD, _R_FFN = 256, 128, 256, 256


_MASKS = ((1, 0), (0, 1), (1, 1))


def _flip(k, x, y):
    return (1 - x if _MASKS[k][0] else x), (1 - y if _MASKS[k][1] else y)


def _rs_d2d(grads):
    plan = [functools.partial(lambda i, ins, outs, x, y, c: (ins[i].at[:, pl.ds(1 - c, 1)], outs[i], (x, y, 1 - c)), i)
            for i in range(len(grads))]
    return grads, [jax.ShapeDtypeStruct((N_CHIPS, 1) + g.shape[2:], F32) for g in grads], plan, (), ()


def _rs_ici(grads, recv, tag):
    n = len(grads)
    c_arr = lax.axis_index("c").astype(jnp.int32).reshape(1)
    q = [sum_halves("rs_sum_d2d%s_%d" % (tag, i), g, r, c_arr) for i, (g, r) in enumerate(zip(grads, recv))]
    qf, qb = [a for a, _ in q], [b for _, b in q]

    def ici(i, k, ins, outs, x, y, c):
        px, py = _flip(k, x, y)
        return ins[i].at[2 * px + py], outs[3 * i + k], (px, py, c)

    plan = [functools.partial(ici, i, k) for i in range(n) for k in range(3)]
    shapes = [jax.ShapeDtypeStruct(g.shape[2:], BF16) for g in grads for _ in range(3)]
    return qf, (qb, shapes, plan, (), ())


def _rs_finish(qf, res, tag):
    n = len(qf)
    chip_arr = (2 * lax.axis_index("x") + lax.axis_index("y")).astype(jnp.int32).reshape(1)
    red = [sum_chips("rs_sum_ici%s_%d" % (tag, i), qf[i], res[3 * i:3 * i + 3], chip_arr) for i in range(n)]
    plan = [functools.partial(lambda i, ins, outs, x, y, c: (ins[i], outs[i], (x, y, 1 - c)), i) for i in range(n)]
    other = exchange("rs_swap" + tag, red, [jax.ShapeDtypeStruct(r.shape, F32) for r in red], plan)
    return red, other


def _join_sides(sides):
    ins, shapes, plan = [], [], []
    for s_ins, s_shapes, s_plan, _, _ in sides:
        def shifted(fn, i0, i1, o0, o1, in_refs, out_refs, x, y, c):
            return fn(in_refs[i0:i1], out_refs[o0:o1], x, y, c)

        i0, o0 = len(ins), len(shapes)
        plan += [functools.partial(shifted, fn, i0, i0 + len(s_ins), o0, o0 + len(s_shapes)) for fn in s_plan]
        ins += list(s_ins)
        shapes += list(s_shapes)
    return ins, shapes, plan, (), ()


def _gather_side(pieces):
    n = len(pieces)

    def send(i, k, ins, outs, x, y, c):
        px, py = _flip(k, x, y)
        return ins[i].at[c], outs[i].at[2 * (2 * x + y) + c], (px, py, c)

    def to_sibling(i, h, ins, outs, x, y, c):
        return ins[i].at[h], outs[i].at[2 * (2 * x + y) + h], (x, y, 1 - c)

    def pass_on(i, k, ins, outs, x, y, c):
        px, py = _flip(k, x, y)
        blk = 2 * (2 * px + py) + c
        return outs[i].at[blk], outs[i].at[blk], (x, y, 1 - c)

    plan = [functools.partial(send, i, k) for i in range(n) for k in range(3)]
    plan += [functools.partial(to_sibling, i, h) for i in range(n) for h in range(2)]
    then = [functools.partial(pass_on, i, k) for i in range(n) for k in range(3)]
    shapes = [jax.ShapeDtypeStruct((N_DEV,) + p.shape[1:], p.dtype) for p in pieces]
    return pieces, shapes, plan, (), then


def kernel(x, c, w_ada, b_ada, norm1_w, w_in, gdn_conv_w, gdn_a_log, gdn_dt_bias, gdn_norm_w, hgrn_lb_param, hgrn_norm_w, ssd_conv_w, ssd_conv_b, ssd_a_log, ssd_dt_bias, ssd_d, ssd_norm_w, w_br_a, w_br_b, w_br_c, w_out, norm2_w, ffn_w_up, ffn_conv_w, ffn_conv_b, ffn_w_down, final_norm_w, loss_target, m_w_ada, m_b_ada, m_norm1_w, m_w_in, m_gdn_conv_w, m_gdn_a_log, m_gdn_dt_bias, m_gdn_norm_w, m_hgrn_lb_param, m_hgrn_norm_w, m_ssd_conv_w, m_ssd_conv_b, m_ssd_a_log, m_ssd_dt_bias, m_ssd_d, m_ssd_norm_w, m_w_br_a, m_w_br_b, m_w_br_c, m_w_out, m_norm2_w, m_ffn_w_up, m_ffn_conv_w, m_ffn_conv_b, m_ffn_w_down, m_final_norm_w, v_w_ada, v_b_ada, v_norm1_w, v_w_in, v_gdn_conv_w, v_gdn_a_log, v_gdn_dt_bias, v_gdn_norm_w, v_hgrn_lb_param, v_hgrn_norm_w, v_ssd_conv_w, v_ssd_conv_b, v_ssd_a_log, v_ssd_dt_bias, v_ssd_d, v_ssd_norm_w, v_w_br_a, v_w_br_b, v_w_br_c, v_w_out, v_norm2_w, v_ffn_w_up, v_ffn_conv_w, v_ffn_conv_b, v_ffn_w_down, v_final_norm_w):
    loc = dict(locals())
    w = {k: loc[k] for k in _WEIGHTS}
    mom = {k: loc["m_" + k] for k in _WEIGHTS}
    var = {k: loc["v_" + k] for k in _WEIGHTS}
    nb, s, d = x.shape
    t = nb * s
    depth = w_ada.shape[0]
    chip = 2 * lax.axis_index("x") + lax.axis_index("y")
    dev = 2 * chip + lax.axis_index("c")
    x0 = x.reshape(t, d)
    target = loss_target.reshape(t, d)

    small_in = [c, gdn_conv_w.reshape(depth * 4, -1), ssd_conv_w.reshape(depth * 4, -1),
                ffn_conv_w.reshape(depth * 3, -1)]
    c_all, gcw, scw, fcw = allgather8("ag_small", small_in, [False] * 4)
    c_all = c_all.reshape(N_DEV * nb, d)

    def conv_full(g, taps):
        g = g[::2].reshape(N_CHIPS, depth, taps, -1)
        return g.transpose(1, 2, 0, 3).reshape(depth, taps, -1)

    gdn_cw, ssd_cw, ffn_cw = conv_full(gcw, 4), conv_full(scw, 4), conv_full(fcw, 3)

    axis_of = dict(_BIG)
    first_needed, later = ("w_in",), tuple(n for n, _ in _BIG if n != "w_in")
    wls = [dict() for _ in range(depth)]

    def pieces(keys):
        out = []
        for l, name in keys:
            a = w[name][l].astype(BF16)
            out.append(a.reshape(2, a.shape[0] // 2, a.shape[1]))
        return out

    def arrived(keys, bufs):
        for (l, name), g in zip(keys, bufs):
            if name == "w_in":
                wls[l]["w_g"], wls[l]["w_a"], wls[l]["w_b"], wls[l]["w_c"] = _split_w_in(g)
            else:
                wls[l][name] = _unstack_gathered(g, 1, axis_of[name])[0]

    keys0 = [(0, n) for n in first_needed]
    arrived(keys0, allgather8("ag_weights0", pieces(keys0), [True] * len(keys0)))

    (c_act,) = elementwise("silu_c", lambda v: (_silu(v),), [c_all], [F32])
    mod_cols = jnp.concatenate([matmul("ada_fwd%d" % l, c_act, w_ada[l], "nn") for l in range(depth)], axis=0)
    (mod8,) = allgather8("ag_mod", [mod_cols], [False])
    mod = mod8[::2].reshape(N_CHIPS, depth, N_DEV * nb, -1).transpose(1, 2, 0, 3).reshape(depth, N_DEV * nb, 6 * d)
    mod = lax.dynamic_slice_in_dim(mod, dev * nb, nb, axis=1) + b_ada[:, None, :]

    def mod_part(l, k):
        return mod[l, :, k * d:(k + 1) * d].reshape(nb, 1, d)

    saved = []
    xl = x0
    for l in range(depth):
        sfx = str(l)
        wl = wls[l]
        sv = {"x0": xl}
        shift1, scale1, gate1, shift2, scale2, gate2 = [mod_part(l, k) for k in range(6)]
        sv["mods"] = (shift1, scale1, gate1, shift2, scale2, gate2)
        h, h_t = normmod_fwd("norm1_fwd" + sfx, xl, norm1_w[l][None], shift1, scale1, nb, s)
        pg = matmul("proj_g" + sfx, h, wl["w_g"], "nn")
        pa = matmul("proj_a" + sfx, h, wl["w_a"], "nn")
        pb = matmul("proj_b" + sfx, h, wl["w_b"], "nn")
        pc = matmul("proj_c" + sfx, h, wl["w_c"], "nn")
        gdn_p = [_rows8(list(gdn_cw[l]), 1536), _rows8([gdn_a_log[l], gdn_dt_bias[l], gdn_norm_w[l]], 128)]
        hgrn_p = [_rows8(list(hgrn_lb_param), 512), _rows8([hgrn_norm_w[l]], 128)]
        ssd_p = [_rows8(list(ssd_cw[l]), 1024), _rows8([ssd_conv_b[l], ssd_norm_w[l]], 1024),
                 _rows8([ssd_a_log[l], ssd_dt_bias[l], ssd_d[l]], 128)]
        ffn_p = [_rows8(list(ffn_cw[l]) + [ffn_conv_b[l]], 2 * FFN_HIDDEN)]
        hgrn_fn = make_hgrn_tile(l, depth)
        keys = [(l, n) for n in later] + ([(l + 1, n) for n in first_needed] if l + 1 < depth else [])
        oa, st_a, *bufs = seq_fwd("gdn_fwd" + sfx, gdn_tile, gdn_p, [pa], True, [(512, BF16)], (512, 128), nb, s,
                                  _R_GDN, side=_gather_side(pieces(keys)))
        arrived(keys, bufs)
        ob, st_b = seq_fwd("hgrn_fwd" + sfx, hgrn_fn, hgrn_p, [pb], False, [(512, BF16)], (512, 128), nb, s, _R_HGRN)
        oc, st_c = seq_fwd("ssd_fwd" + sfx, ssd_tile, ssd_p, [pc], True, [(512, BF16)], (256, 256), nb, s, _R_SSD)
        x1, *merge_saved = merge_fwd("merge_fwd" + sfx, xl, oa, ob, oc, pg, gate1, wl["w_br_a"], wl["w_br_b"],
                                     wl["w_br_c"], wl["w_out"], nb, s)
        h2, h2_t = normmod_fwd("norm2_fwd" + sfx, x1, norm2_w[l][None], shift2, scale2, nb, s)
        u = matmul("ffn_up" + sfx, h2, wl["ffn_w_up"], "nn", tn=FFN_HIDDEN)
        act, act_t = seq_fwd("convglu_fwd" + sfx, convglu_tile_t, ffn_p, [u], True,
                             [(FFN_HIDDEN, BF16), (FFN_HIDDEN, BF16, "T")], None, nb, s, _R_FFN)
        xl, f = matmul("ffn_down" + sfx, act, wl["ffn_w_down"], "nn", resid=(x1, gate2, s))
        sv.update(merge_saved=merge_saved, h_t=h_t, h2_t=h2_t, act_t=act_t, pg=pg, pa=pa, pb=pb, pc=pc, oa=oa, ob=ob, oc=oc, st_a=st_a, st_b=st_b, st_c=st_c, x1=x1,
                  u=u, f=f, gdn_p=gdn_p, hgrn_p=hgrn_p, ssd_p=ssd_p, ffn_p=ffn_p, hgrn_fn=hgrn_fn)
        saved.append(sv)

    dx, loss_part, d_final = loss_head("loss_head", xl, final_norm_w[None], target)

    sg = {}
    dmod = [None] * depth
    d_lb = None
    reduced = {}
    early = [n for n, _ in _BIG if n != "w_in"]
    to_d2d = to_ici = None

    def finish(keys, qf, res, tag):
        for key, mine, other in zip(keys, *_rs_finish(qf, res, tag)):
            reduced[key] = (mine, other)
    for l in reversed(range(depth)):
        sfx = str(l)
        sv, wl = saved[l], wls[l]
        gfull = {}
        shift1, scale1, gate1, shift2, scale2, gate2 = sv["mods"]
        df, dgate2 = resid_bwd("resid_bwd" + sfx, dx, sv["f"], gate2, nb, s)
        dact = matmul("ffn_down_dx" + sfx, df, wl["ffn_w_down"], "nt")
        gfull["ffn_w_down"] = matmul("ffn_down_dw" + sfx, sv["act_t"], df, "nn", tm=1408, tn=512, tk=4096)
        cg_args = ("convglu_bwd" + sfx, convglu_tile, sv["ffn_p"], [sv["u"]], True, None, [dact], [BF16], None, nb, s,
                   _R_FFN)
        if to_d2d is None:
            (du,), (dcw,) = seq_bwd(*cg_args)
        else:
            lp, stacked = to_d2d
            (du,), (dcw,), recv = seq_bwd(*cg_args, side=_rs_d2d(stacked))
            to_ici = ([(lp, "w_in")],) + _rs_ici(stacked, recv, "i%d" % lp) + ("i%d" % lp,)
        dh2 = matmul("ffn_up_dx" + sfx, du, wl["ffn_w_up"], "nt")
        gfull["ffn_w_up"] = matmul("ffn_up_dw" + sfx, sv["h2_t"], du, "nn", tm=1024, tn=512, tk=4096)
        dx1, dnw2, dshift2, dscale2 = normmod_bwd("norm2_bwd" + sfx, sv["x1"], norm2_w[l][None], shift2, scale2, dh2, dx,
                                                  nb, s)
        doa, dob, doc, dpg, dgate1, dwa, dwb, dwc, dwo = merge_bwd(
            "merge_bwd" + sfx, sv["oa"], sv["ob"], sv["oc"], sv["pg"], gate1, wl["w_br_a"], wl["w_br_b"],
            wl["w_br_c"], wl["w_out"], sv["merge_saved"][:3], *sv["merge_saved"][3:], dx1, nb, s)
        gfull["w_br_a"], gfull["w_br_b"], gfull["w_br_c"], gfull["w_out"] = dwa, dwb, dwc, dwo
        gdn_args = ("gdn_bwd" + sfx, gdn_tile, sv["gdn_p"], [sv["pa"]], True, sv["st_a"], [doa], [F32], (512, 128),
                    nb, s, _R_GDN)
        stacked = [_stack_by_chip(gfull[n][None], axis_of[n]) for n in early]
        recv = exchange("rs_d2d_e" + sfx, *_rs_d2d(stacked)[:4])
        hosted = [([(l, n) for n in early],) + _rs_ici(stacked, recv, "e" + sfx) + ("e" + sfx,)]
        if to_ici is not None:
            hosted.append(to_ici)
        (dpa,), (dgcw, dgpk), res = seq_bwd(*gdn_args, side=_join_sides([h[2] for h in hosted]))
        for keys, qf, side, tag in hosted:
            finish(keys, qf, res[:len(side[1])], tag)
            res = res[len(side[1]):]
        (dpb,), (dlbp, dhnw) = seq_bwd("hgrn_bwd" + sfx, sv["hgrn_fn"], sv["hgrn_p"], [sv["pb"]], False, sv["st_b"],
                                       [dob], [F32], (512, 128), nb, s, _R_HGRN)
        (dpc,), (dscw, dspv, dsps) = seq_bwd("ssd_bwd" + sfx, ssd_tile, sv["ssd_p"], [sv["pc"]], True, sv["st_c"],
                                             [doc], [F32], (256, 256), nb, s, _R_SSD)
        dh = matmul("proj_g_dx" + sfx, dpg, wl["w_g"], "nt")
        dh = matmul("proj_a_dx" + sfx, dpa, wl["w_a"], "nt", addend=dh)
        dh = matmul("proj_b_dx" + sfx, dpb, wl["w_b"], "nt", addend=dh)
        dh = matmul("proj_c_dx" + sfx, dpc, wl["w_c"], "nt", addend=dh)
        stacked_w_in = _stack_w_in(
            matmul("proj_g_dw" + sfx, sv["h_t"], dpg, "nn", tm=1024, tn=512, tk=4096),
            matmul("proj_a_dw" + sfx, sv["h_t"], dpa, "nn", tm=1024, tk=1024),
            matmul("proj_b_dw" + sfx, sv["h_t"], dpb, "nn", tm=1024, tn=512, tk=4096),
            matmul("proj_c_dw" + sfx, sv["h_t"], dpc, "nn", tm=1024, tk=1024))
        dx, dnw1, dshift1, dscale1 = normmod_bwd("norm1_bwd" + sfx, sv["x0"], norm1_w[l][None], shift1, scale1, dh, dx1,
                                                 nb, s)
        dmod[l] = jnp.concatenate([dshift1, dscale1, dgate1, dshift2, dscale2, dgate2], axis=-1).reshape(nb, 6 * d)
        d_lb = dlbp[:depth] if d_lb is None else d_lb + dlbp[:depth]
        sg[l] = dict(norm1_w=dnw1[0], norm2_w=dnw2[0], gdn_conv_w=dgcw[:4], gdn_a_log=dgpk[0, :4],
                     gdn_dt_bias=dgpk[1, :4], gdn_norm_w=dgpk[2], hgrn_norm_w=dhnw[0], ssd_conv_w=dscw[:4],
                     ssd_conv_b=dspv[0], ssd_norm_w=dspv[1, :512], ssd_a_log=dsps[0, :8], ssd_dt_bias=dsps[1, :8],
                     ssd_d=dsps[2, :8], ffn_conv_w=dcw[:3], ffn_conv_b=dcw[3])
        to_d2d = (l, [stacked_w_in])
    lp, stacked = to_d2d
    recv = exchange("rs_d2d_i%d" % lp, *_rs_d2d(stacked)[:4])
    qf, side = _rs_ici(stacked, recv, "i%d" % lp)
    finish([(lp, "w_in")], qf, exchange("rs_ici_i%d" % lp, *side[:4]), "i%d" % lp)
    grad_x = dx.reshape(nb, s, d)

    dmod = jnp.stack(dmod)
    (b_sum,) = elementwise("bias_rows", lambda *r: (functools.reduce(lambda p, q: p + q, r),),
                           [dmod[:, b].reshape(depth * 6, d) for b in range(nb)], [F32])
    per_layer = ("norm1_w", "norm2_w", "gdn_conv_w", "gdn_a_log", "gdn_dt_bias", "gdn_norm_w", "hgrn_norm_w",
                 "ssd_conv_w", "ssd_conv_b", "ssd_norm_w", "ssd_a_log", "ssd_dt_bias", "ssd_d", "ffn_conv_w", "ffn_conv_b")
    vals = {k: jnp.stack([sg[l][k] for l in range(depth)]) for k in per_layer}
    vals.update(loss=loss_part[0, :1], b_ada=b_sum.reshape(depth, 6 * d), hgrn_lb_param=d_lb, final_norm_w=d_final[0])
    gp = _Packer()
    for k, v in vals.items():
        gp.add(k, v.shape)
    packed8, dmod8 = allgather8("ag_grads", [gp.pack(vals), dmod.reshape(depth * nb, 6 * d)], [False, False])
    gs = gp.unpack(sum8("sum_small", packed8))
    loss = gs["loss"].reshape(())

    def my_cols(g):
        cs = g.shape[-1] // N_CHIPS
        return lax.dynamic_slice_in_dim(g, chip * cs, cs, axis=g.ndim - 1)

    for k in ("gdn_conv_w", "ssd_conv_w", "ffn_conv_w"):
        gs[k] = my_cols(gs[k])

    dmod_all = dmod8.reshape(N_DEV, depth, nb, 6 * d).transpose(1, 0, 2, 3).reshape(depth, N_DEV * nb, 6 * d)
    dmod_mine = lax.dynamic_slice_in_dim(dmod_all, chip * (6 * d // N_CHIPS), 6 * d // N_CHIPS, axis=2)
    g_w_ada = jnp.stack([matmul("ada_dw%d" % l, c_act, dmod_mine[l], "tn", tm=1024) for l in range(depth)])

    c_arr = lax.axis_index("c").astype(jnp.int32).reshape(1)
    grads, delta, new_m, new_v = {}, {}, {}, {}
    for i, (name, _) in enumerate(_BIG):
        shp = w[name].shape
        halves = lambda a: a.reshape((depth, 2) + reduced[(0, name)][0].shape)
        res = None
        for l in reversed(range(depth)):
            res = adamw_halves("adamw_%s%d" % (name, l), halves(w[name]), halves(mom[name]), halves(var[name]), l,
                               *reduced[(l, name)], c_arr, prev=res)
        grads[name], delta[name], new_m[name], new_v[name] = [r.reshape(shp) for r in res]
    grads["w_ada"] = g_w_ada
    for k in _SMALL:
        grads[k] = gs[k].reshape(w[k].shape)
    shp = w_ada.shape
    flat = lambda a: a.reshape(shp[0] * shp[1], shp[2])
    dl, nm, nv = adamw("adamw_w_ada", flat(w_ada), flat(g_w_ada), flat(m_w_ada), flat(v_w_ada))
    delta["w_ada"], new_m["w_ada"], new_v["w_ada"] = dl.reshape(shp), nm.reshape(shp), nv.reshape(shp)
    sp = _Packer()
    for k in _SMALL:
        sp.add(k, w[k].shape)
    dl, nm, nv = adamw("adamw_small", sp.pack(w), sp.pack(grads), sp.pack(mom), sp.pack(var))
    delta.update(sp.unpack(dl))
    new_m.update(sp.unpack(nm))
    new_v.update(sp.unpack(nv))

    return (loss, grad_x, *[grads[k] for k in _WEIGHTS], *[delta[k] for k in _WEIGHTS],
            *[new_m[k] for k in _WEIGHTS], *[new_v[k] for k in _WEIGHTS])
```

```python
import functools

import jax
import jax.numpy as jnp
from jax import lax
from jax.experimental import pallas as pl
from jax.experimental.pallas import tpu as pltpu

F32 = jnp.float32
BF16 = jnp.bfloat16
HI = lax.Precision.HIGHEST
MESH = pl.DeviceIdType.MESH

EPS = 1e-6
D_MODEL = 1024
GDN_HEADS, GDN_DK, GDN_CHUNK = 4, 128, 64
HGRN_HEADS, HGRN_DK, HGRN_CHUNK = 4, 128, 16
SSD_HEADS, SSD_P, SSD_GROUPS, SSD_STATE, SSD_CHUNK = 8, 64, 2, 128, 64
FFN_HIDDEN = 2816
N_CHIPS = 4
N_DEV = 8

ADAM_LR, ADAM_B1, ADAM_B2, ADAM_EPS, ADAM_WD, ADAM_STEP = 0.001, 0.9, 0.999, 1e-08, 0.01, 10

W_G, W_A, W_B, W_C = 3072, 2176, 2048, 1664
VMEM_LIMIT = 56 * 1024 * 1024


def _cparams(sem):
    return pltpu.CompilerParams(dimension_semantics=sem, vmem_limit_bytes=VMEM_LIMIT)


def _dg(a, b, ca, cb):
    return lax.dot_general(a.astype(BF16), b.astype(BF16), (((ca,), (cb,)), ((), ())),
                           preferred_element_type=F32)


@jax.custom_vjp
def bdot(a, b):
    return _dg(a, b, 1, 0)


bdot.defvjp(lambda a, b: (_dg(a, b, 1, 0), (a, b)),
            lambda r, g: (_dg(g, r[1], 1, 1), _dg(r[0], g, 0, 0)))


@jax.custom_vjp
def bdot_nt(a, b):
    return _dg(a, b, 1, 1)


bdot_nt.defvjp(lambda a, b: (_dg(a, b, 1, 1), (a, b)),
               lambda r, g: (_dg(g, r[1], 1, 0), _dg(g, r[0], 0, 0)))


@jax.custom_vjp
def bdot_tn(a, b):
    return _dg(a, b, 0, 0)


bdot_tn.defvjp(lambda a, b: (_dg(a, b, 0, 0), (a, b)),
               lambda r, g: (_dg(r[1], g, 1, 1), _dg(r[0], g, 1, 0)))


def _split(x, n):
    parts, rest = [], x
    for _ in range(n):
        p = rest.astype(BF16)
        parts.append(p)
        rest = rest - p.astype(F32)
    return parts


def _dgb(a, b, ca, cb):
    return lax.dot_general(a, b, (((ca,), (cb,)), ((), ())), preferred_element_type=F32)


def _dg3(a, b, ca, cb):
    (ah, al), (bh, bl) = _split(a, 2), _split(b, 2)
    return _dgb(jnp.concatenate([ah, ah, al], axis=ca), jnp.concatenate([bh, bl, bh], axis=cb), ca, cb)


@jax.custom_vjp
def hdot(a, b):
    return _dg3(a, b, 1, 0)


hdot.defvjp(lambda a, b: (_dg3(a, b, 1, 0), (a, b)),
            lambda r, g: (_dg3(g, r[1], 1, 1), _dg3(r[0], g, 0, 0)))


def _dge(e, x, ce, cx, e_first):
    eb = e.astype(BF16)
    es = jnp.concatenate([eb, eb, eb], axis=ce)
    xs = jnp.concatenate(_split(x, 3), axis=cx)
    return _dgb(es, xs, ce, cx) if e_first else _dgb(xs, es, cx, ce)


@jax.custom_vjp
def ldot(e, x):
    return _dge(e, x, 1, 0, True)


ldot.defvjp(lambda e, x: (_dge(e, x, 1, 0, True), e),
            lambda e, g: (jnp.zeros_like(e), _dge(e, g, 0, 0, True)))


@jax.custom_vjp
def rdot(x, e):
    return _dge(e, x, 0, 1, False)


rdot.defvjp(lambda x, e: (_dge(e, x, 0, 1, False), e),
            lambda e, g: (_dge(e, g, 1, 1, False), jnp.zeros_like(e)))


def _sigmoid(x):
    return 1.0 / (1.0 + jnp.exp(-x))


def _silu(x):
    return x * _sigmoid(x)


def _softplus(x):
    return jnp.maximum(x, 0.0) + jnp.log(1.0 + jnp.exp(-jnp.abs(x)))


def _rms(x, w):
    return x * lax.rsqrt(jnp.mean(x * x, axis=-1, keepdims=True) + EPS) * w


def _iota(shape, dim):
    return lax.broadcasted_iota(jnp.int32, shape, dim)


def _tri_ones(n, chunk, kind):
    i, j = _iota((n, n), 0), _iota((n, n), 1)
    same = lax.div(i, chunk) == lax.div(j, chunk)
    if kind == "incl":
        m = same & (j <= i)
    elif kind == "strict":
        m = same & (j < i)
    elif kind == "all":
        m = same
    else:
        m = same & (lax.rem(j, chunk) < (chunk // 2))
    return m


def _causal_conv(w, halo, x, width):
    r = x.shape[0]
    xin = jnp.concatenate([halo, x], axis=0)
    y = w[width - 1:width, :] * x
    for k in range(width - 1):
        off = 8 - (width - 1) + k
        y = y + w[k:k + 1, :] * xin[off:off + r, :]
    return y


def _each(fn, *lists):
    return [fn(*a) for a in zip(*lists)]


def _neumann(ms):
    n = ms[0].shape[0]
    eye = (_iota((n, n), 0) == _iota((n, n), 1)).astype(F32)
    accs = [eye - m for m in ms]
    ps = ms
    steps = 1
    while steps * 2 < n:
        ps = _each(hdot, ps, ps)
        accs = [acc + ap for acc, ap in zip(accs, _each(hdot, accs, ps))]
        steps *= 2
    return accs


@jax.custom_vjp
def tri_inverse(ms):
    return _neumann(ms)


def _tri_inverse_fwd(ms):
    ainvs = _neumann(ms)
    return ainvs, ainvs


def _tri_inverse_bwd(ainvs, gs):
    t = _each(lambda g, a: _dg3(g, a, 1, 1), gs, ainvs)
    return ([-x for x in _each(lambda a, y: _dg3(a, y, 0, 0), ainvs, t)],)


tri_inverse.defvjp(_tri_inverse_fwd, _tri_inverse_bwd)


def gdn_tile(params, state, ins, halos):
    conv_w, pk = params
    (pa,), (ha,) = ins, halos
    r = pa.shape[0]
    c, nh, dk = GDN_CHUNK, GDN_HEADS, GDN_DK
    kw = nh * dk
    qkv = _silu(_causal_conv(conv_w, ha[:, :3 * kw], pa[:, :3 * kw], 4))
    z = pa[:, 3 * kw:4 * kw]
    gsm = pa[:, 4 * kw:]
    a_log, dtb, nw = pk[0:1, :], pk[1:2, :], pk[2:3, :]
    g_all = -jnp.exp(a_log) * _softplus(gsm + dtb)
    beta_all = _sigmoid(gsm)
    incl = _tri_ones(c, c, "incl")
    strict = _tri_ones(c, c, "strict")
    lmat = incl.astype(F32)
    scale = dk ** -0.5
    nck = r // c
    inst = [(ci, h) for ci in range(nck) for h in range(nh)]

    def l2n(v):
        return v * lax.rsqrt(jnp.sum(v * v, axis=-1, keepdims=True) + EPS)

    gcs = [ldot(lmat, g_all[ci * c:(ci + 1) * c, :]) for ci in range(nck)]
    gcts = [g.T for g in gcs]
    g_col = [gcs[ci][:, h:h + 1] for ci, h in inst]
    g_row = [gcts[ci][h:h + 1, :] for ci, h in inst]
    g_last = [gcs[ci][c - 1:c, h:h + 1] for ci, h in inst]
    beta = [beta_all[ci * c:(ci + 1) * c, nh + h:nh + h + 1] for ci, h in inst]
    qh = [l2n(qkv[ci * c:(ci + 1) * c, h * dk:(h + 1) * dk]) for ci, h in inst]
    kh = [l2n(qkv[ci * c:(ci + 1) * c, kw + h * dk:kw + (h + 1) * dk]) for ci, h in inst]
    vh = [qkv[ci * c:(ci + 1) * c, 2 * kw + h * dk:2 * kw + (h + 1) * dk] for ci, h in inst]
    decay = [jnp.where(incl, jnp.exp(jnp.where(incl, gc_ - gr_, 0.0)), 0.0) for gc_, gr_ in zip(g_col, g_row)]
    kb = [k * b for k, b in zip(kh, beta)]
    qs = [q * scale for q in qh]
    kk = _each(lambda a, b, k: bdot_nt(jnp.concatenate([a, b], axis=0), k), kb, qs, kh)
    ms = [jnp.where(strict, x[:c] * d, 0.0) for x, d in zip(kk, decay)]
    attn = [x[c:] * d for x, d in zip(kk, decay)]
    ainv = tri_inverse(ms)
    eg = [jnp.exp(g) for g in g_col]
    rhs = [jnp.concatenate([v * b, k_ * e], axis=1) for v, b, k_, e in zip(vh, beta, kb, eg)]
    sol = _each(hdot, ainv, rhs)
    qg = [q * e for q, e in zip(qs, eg)]
    k_end = [k * jnp.exp(gl - g) for k, gl, g in zip(kh, g_last, g_col)]
    e_last = [jnp.exp(gl) for gl in g_last]

    st = [state[h * dk:(h + 1) * dk, :] for h in range(nh)]
    outs = [[] for _ in range(nh)]
    for ci in range(nck):
        idx = [ci * nh + h for h in range(nh)]
        ws = [bdot(jnp.concatenate([sol[i][:, dk:], qg[i]], axis=0), st[h]) for h, i in enumerate(idx)]
        v_new = [sol[i][:, :dk] - w_[:c] for i, w_ in zip(idx, ws)]
        av = [bdot(attn[i], v) for i, v in zip(idx, v_new)]
        kv = [bdot_tn(k_end[i], v) for i, v in zip(idx, v_new)]
        for h, i in enumerate(idx):
            o = ws[h][c:] + av[h]
            st[h] = st[h] * e_last[i] + kv[h]
            outs[h].append(_rms(o, nw) * _silu(z[ci * c:(ci + 1) * c, h * dk:(h + 1) * dk]))
    out = jnp.concatenate([jnp.concatenate(o, axis=0) for o in outs], axis=1)
    return jnp.concatenate(st, axis=0), [out]


def make_hgrn_tile(layer, depth):
    def hgrn_tile(params, state, ins, halos):
        lbp, nwp = params
        (pb,) = ins
        r = pb.shape[0]
        c = HGRN_CHUNK
        kw = HGRN_HEADS * HGRN_DK
        rows = [lbp[i:i + 1, :] for i in range(depth)]
        mx = functools.reduce(jnp.maximum, rows)
        ex = [jnp.exp(x - mx) for x in rows]
        den = functools.reduce(lambda a, b: a + b, ex)
        soft = [e / den for e in ex]
        lb = functools.reduce(lambda a, b: a + b, soft[:layer + 1]) - soft[0]
        nw = nwp[0:1, :]
        q = _silu(pb[:, :kw])
        fr = pb[:, kw:2 * kw]
        logf = jnp.log(lb + (1.0 - lb) * _sigmoid(fr))
        k = (1.0 - lb) * _sigmoid(-fr)
        v = pb[:, 2 * kw:3 * kw]
        gate = pb[:, 3 * kw:]
        incl = _tri_ones(r, c, "incl")
        masks = jnp.concatenate([incl.astype(F32), _tri_ones(r, c, "upto").astype(F32),
                                 _tri_ones(r, c, "all").astype(F32)], axis=0)
        sums = ldot(masks, logf)
        g_cum, g_ref, g_end = sums[:r], sums[r:2 * r], sums[2 * r:]
        qs = q * jnp.exp(g_cum - g_ref)
        ks = k * jnp.exp(g_ref - g_cum)
        qg = q * jnp.exp(g_cum)
        k_end = k * jnp.exp(g_end - g_cum)
        e_end = jnp.exp(g_end)
        sls = [slice(h * HGRN_DK, (h + 1) * HGRN_DK) for h in range(HGRN_HEADS)]
        attn = [jnp.where(incl, bdot_nt(qs[:, sl], ks[:, sl]), 0.0) for sl in sls]
        o_intra = [bdot(a, v[:, sl]) for a, sl in zip(attn, sls)]
        nsub, dk = r // c, HGRN_DK
        own_block = lax.div(_iota((r, nsub * dk), 0), c) == lax.div(_iota((r, nsub * dk), 1), dk)

        def spread(a):
            return jnp.where(own_block, jnp.concatenate([a] * nsub, axis=1), 0.0)

        kv = [bdot_tn(v[:, sl], spread(k_end[:, sl])) for sl in sls]
        s_t = [state[sl, :] for sl in sls]
        entry = [[] for _ in sls]
        for j in range(nsub):
            for lst, s_h in zip(entry, s_t):
                lst.append(s_h)
            s_t = [s_h * e_end[j * c:j * c + 1, sl] + x[:, j * dk:(j + 1) * dk] for s_h, sl, x in zip(s_t, sls, kv)]
        o_inter = [bdot_nt(spread(qg[:, sl]), jnp.concatenate(e, axis=1)) for sl, e in zip(sls, entry)]
        outs = [_rms(oa + ob, nw) * _silu(gate[:, sl]) for oa, ob, sl in zip(o_intra, o_inter, sls)]
        return jnp.concatenate(s_t, axis=0), [jnp.concatenate(outs, axis=1)]
    return hgrn_tile


def ssd_tile(params, state, ins, halos):
    conv_w, pv, ps = params
    (pc,), (hc,) = ins, halos
    r = pc.shape[0]
    c = SSD_CHUNK
    inner = SSD_HEADS * SSD_P
    gw = inner // SSD_GROUPS
    z = pc[:, :inner]
    xbc = _silu(_causal_conv(conv_w, hc[:, inner:inner + 1024], pc[:, inner:inner + 1024], 4) + pv[0:1, :])
    ssm = pc[:, inner + 1024:]
    xs = xbc[:, :inner]
    bm = xbc[:, inner:inner + SSD_GROUPS * SSD_STATE]
    cm = xbc[:, inner + SSD_GROUPS * SSD_STATE:]
    a_log, dtb, dsk = ps[0:1, :], ps[1:2, :], ps[2:3, :]
    nw = pv[1:2, :inner]
    dt = _softplus(ssm + dtb)
    da = dt * (-jnp.exp(a_log))
    expand = (lax.div(_iota((128, inner), 1), SSD_P) == _iota((128, inner), 0)).astype(F32)
    xdt = xs * rdot(dt, expand)
    d_e = rdot(jnp.concatenate([dsk] * 8, axis=0), expand)[0:1, :]
    incl = _tri_ones(c, c, "incl")
    lmat = incl.astype(F32)
    st = [state[g * SSD_STATE:(g + 1) * SSD_STATE, :] for g in range(SSD_GROUPS)]
    hpg = SSD_HEADS // SSD_GROUPS
    nck = r // c
    groups = range(SSD_GROUPS)
    cg = [(ci, g) for ci in range(nck) for g in groups]
    rows = [slice(ci * c, (ci + 1) * c) for ci in range(nck)]
    gls = [slice(g * gw, (g + 1) * gw) for g in groups]
    acs = [ldot(lmat, da[rs, :]) for rs in rows]
    acs_t = [a.T for a in acs]
    acs_e = [rdot(a, expand) for a in acs]
    last_e = [a[c - 1:c, :] for a in acs_e]
    bm_g = [bm[rows[ci], g * SSD_STATE:(g + 1) * SSD_STATE] for ci, g in cg]
    cm_g = [cm[rows[ci], g * SSD_STATE:(g + 1) * SSD_STATE] for ci, g in cg]
    cb = _each(bdot_nt, cm_g, bm_g)
    heads = [(i, ci, g * hpg + hg) for i, (ci, g) in enumerate(cg) for hg in range(hpg)]
    seg = [jnp.where(incl, jnp.exp(jnp.where(incl, acs[ci][:, hh:hh + 1] - acs_t[ci][hh:hh + 1, :], 0.0)), 0.0)
           for _, ci, hh in heads]
    yd = [bdot(cb[i] * sg, xdt[rows[ci], hh * SSD_P:(hh + 1) * SSD_P]) for (i, ci, hh), sg in zip(heads, seg)]
    y_diag = [jnp.concatenate(yd[i * hpg:(i + 1) * hpg], axis=1) for i in range(len(cg))]
    xw = [xdt[rows[ci], gls[g]] * jnp.exp(last_e[ci][:, gls[g]] - acs_e[ci][:, gls[g]]) for ci, g in cg]
    e_acs = [jnp.exp(acs_e[ci][:, gls[g]]) for ci, g in cg]
    e_last = [jnp.exp(last_e[ci][:, gls[g]]) for ci, g in cg]
    kv = _each(bdot_tn, bm_g, xw)
    ys = []
    for ci in range(nck):
        idx = [ci * SSD_GROUPS + g for g in groups]
        y_off = [bdot(cm_g[i], st[g]) * e_acs[i] for g, i in zip(groups, idx)]
        st = [st[g] * e_last[i] + kv[i] for g, i in zip(groups, idx)]
        ys.append(jnp.concatenate([y_diag[i] + yo for i, yo in zip(idx, y_off)], axis=1))
    y = jnp.concatenate(ys, axis=0) + d_e * xs
    yz = y * _silu(z)
    out = jnp.concatenate([_rms(yz[:, g * gw:(g + 1) * gw], nw[:, g * gw:(g + 1) * gw])
                           for g in range(SSD_GROUPS)], axis=1)
    return jnp.concatenate(st, axis=0), [out]


def convglu_tile(params, state, ins, halos):
    (cw,) = params
    (u,), (hu,) = ins, halos
    y = _causal_conv(cw, hu, u, 3) + cw[3:4, :]
    return None, [_silu(y[:, :FFN_HIDDEN]) * y[:, FFN_HIDDEN:]]


def convglu_tile_t(params, state, ins, halos):
    _, (act,) = convglu_tile(params, state, ins, halos)
    return None, [act, act.T]


def _halo_map(nt, r):
    return lambda b, n: (jnp.maximum((b * nt + n) * (r // 8) - 1, 0), 0)


def _exchange_copies(plan, local_plan, in_refs, out_refs, send_sems, recv_sems, local_sems):
    x, y, c = lax.axis_index("x"), lax.axis_index("y"), lax.axis_index("c")
    copies = []
    for k, fn in enumerate(plan):
        src, dst, peer = fn(in_refs, out_refs, x, y, c)
        copies.append(pltpu.make_async_remote_copy(src_ref=src, dst_ref=dst, send_sem=send_sems.at[k],
                                                   recv_sem=recv_sems.at[k], device_id=peer, device_id_type=MESH))
    for k, fn in enumerate(local_plan):
        src, dst = fn(in_refs, out_refs, x, y, c)
        copies.append(pltpu.make_async_copy(src, dst, local_sems.at[k]))
    return copies


def _exchange_sems(plan, local_plan):
    return [pltpu.SemaphoreType.DMA((max(len(plan), 1),)), pltpu.SemaphoreType.DMA((max(len(plan), 1),)),
            pltpu.SemaphoreType.DMA((max(len(local_plan), 1),))]


def _host_exchange(side, body, in_specs, o_specs, out_shape, scratch, args, grid):
    s_ins, s_shapes, plan, local_plan, then = side
    n_in, n_out, n_scr = len(in_specs), len(o_specs), len(scratch)
    k_in, k_out = len(s_ins), len(s_shapes)
    any_spec = pl.BlockSpec(memory_space=pl.ANY)

    def hosted(*refs):
        own_in, s_in = refs[:n_in], refs[n_in:n_in + k_in]
        o0 = n_in + k_in
        own_out, s_out = refs[o0:o0 + n_out], refs[o0 + n_out:o0 + n_out + k_out]
        rest = refs[o0 + n_out + k_out:]
        own_scr, sems, sems_then = rest[:n_scr], rest[n_scr:n_scr + 3], rest[n_scr + 3:]
        ids = [pl.program_id(a) for a in range(len(grid))]
        first = functools.reduce(lambda p, q: p & q, [i == 0 for i in ids])
        last = functools.reduce(lambda p, q: p & q, [i == g - 1 for i, g in zip(ids, grid)])

        @pl.when(first)
        def _():
            for cp in _exchange_copies(plan, local_plan, s_in, s_out, *sems):
                cp.start()

        body(*own_in, *own_out, *own_scr)

        @pl.when(last)
        def _():
            for cp in _exchange_copies(plan, local_plan, s_in, s_out, *sems):
                cp.wait()
            passed = _exchange_copies(then, (), s_in, s_out, *sems_then)
            for cp in passed:
                cp.start()
            for cp in passed:
                cp.wait()

    return (hosted, list(in_specs) + [any_spec] * k_in, list(o_specs) + [any_spec] * k_out,
            list(out_shape) + list(s_shapes),
            list(scratch) + _exchange_sems(plan, local_plan) + _exchange_sems(then, ()),
            list(args) + list(s_ins))


def seq_fwd(name, tile_fn, params, ins, use_halo, out_specs, state_shape, nb, s, r, side=None):
    nt = s // r
    n_p, n_i, n_o = len(params), len(ins), len(out_specs)
    has_state = state_shape is not None

    def body(*refs):
        p_refs, i_refs = refs[:n_p], refs[n_p:n_p + n_i]
        h_refs = refs[n_p + n_i:n_p + 2 * n_i] if use_halo else ()
        k = n_p + n_i + len(h_refs)
        o_refs = refs[k:k + n_o]
        n = pl.program_id(1)
        state = None
        if has_state:
            sv_ref, st_ref = refs[k + n_o], refs[k + n_o + 1]

            @pl.when(n == 0)
            def _():
                st_ref[...] = jnp.zeros(state_shape, F32)

            state = st_ref[...]
            sv_ref[0, 0] = state
        pv = [p[...] for p in p_refs]
        iv = [i[...].astype(F32) for i in i_refs]
        hv = [jnp.where(n > 0, h[...].astype(F32), 0.0) for h in h_refs]
        new_state, ov = tile_fn(pv, state, iv, hv)
        for o_ref, o in zip(o_refs, ov):
            o_ref[...] = o.astype(o_ref.dtype)
        if has_state:
            st_ref[...] = new_state

    row = lambda b, n: (b * nt + n, 0)
    in_specs = [pl.BlockSpec(p.shape, lambda b, n: (0, 0)) for p in params]
    in_specs += [pl.BlockSpec((r, a.shape[1]), row) for a in ins]
    if use_halo:
        in_specs += [pl.BlockSpec((8, a.shape[1]), _halo_map(nt, r)) for a in ins]
    col = lambda b, n: (0, b * nt + n)
    out_shape, o_specs = [], []
    for w, dt, *transposed in out_specs:
        out_shape.append(jax.ShapeDtypeStruct((w, nb * s) if transposed else (nb * s, w), dt))
        o_specs.append(pl.BlockSpec((w, r), col) if transposed else pl.BlockSpec((r, w), row))
    scratch = []
    if has_state:
        out_shape.append(jax.ShapeDtypeStruct((nb, nt) + tuple(state_shape), F32))
        o_specs.append(pl.BlockSpec((1, 1) + tuple(state_shape), lambda b, n: (b, n, 0, 0)))
        scratch.append(pltpu.VMEM(tuple(state_shape), F32))
    args = list(params) + list(ins) + (list(ins) if use_halo else [])
    if side is not None:
        body, in_specs, o_specs, out_shape, scratch, args = _host_exchange(
            side, body, in_specs, o_specs, out_shape, scratch, args, (nb, nt))
    return pl.pallas_call(body, grid=(nb, nt), in_specs=in_specs, out_specs=o_specs, out_shape=out_shape,
                          scratch_shapes=scratch, compiler_params=_cparams(("arbitrary", "arbitrary")),
                          name=name)(*args)


def seq_bwd(name, tile_fn, params, ins, use_halo, states, douts, din_dtypes, state_shape, nb, s, r, side=None):
    nt = s // r
    n_p, n_i, n_o = len(params), len(ins), len(douts)
    has_state = state_shape is not None

    def body(*refs):
        p_refs, i_refs = refs[:n_p], refs[n_p:n_p + n_i]
        h_refs = refs[n_p + n_i:n_p + 2 * n_i] if use_halo else ()
        k = n_p + n_i + len(h_refs)
        sv_ref = None
        if has_state:
            sv_ref = refs[k]
            k += 1
        do_refs = refs[k:k + n_o]
        k += n_o
        di_refs, dp_refs = refs[k:k + n_i], refs[k + n_i:k + n_i + n_p]
        k += n_i + n_p
        dst_ref = None
        if has_state:
            dst_ref = refs[k]
            k += 1
        dh_refs = refs[k:k + len(h_refs)]
        b, nn = pl.program_id(0), pl.program_id(1)
        n = nt - 1 - nn

        @pl.when((b == 0) & (nn == 0))
        def _():
            for dp in dp_refs:
                dp[...] = jnp.zeros(dp.shape, F32)

        @pl.when(nn == 0)
        def _():
            if has_state:
                dst_ref[...] = jnp.zeros(state_shape, F32)
            for dh in dh_refs:
                dh[...] = jnp.zeros(dh.shape, F32)

        pv = [p[...] for p in p_refs]
        iv = [i[...].astype(F32) for i in i_refs]
        hv = [jnp.where(n > 0, h[...].astype(F32), 0.0) for h in h_refs]
        if has_state:
            f = lambda pv_, st_, iv_, hv_: tile_fn(pv_, st_, iv_, hv_)
            _, vjp = jax.vjp(f, pv, sv_ref[0, 0], iv, hv)
            dpv, dst, div, dhv = vjp((dst_ref[...], [d[...].astype(F32) for d in do_refs]))
            dst_ref[...] = dst
        else:
            f = lambda pv_, iv_, hv_: tile_fn(pv_, None, iv_, hv_)[1]
            _, vjp = jax.vjp(f, pv, iv, hv)
            dpv, div, dhv = vjp([d[...].astype(F32) for d in do_refs])
        for j, (di_ref, d) in enumerate(zip(di_refs, div)):
            if use_halo:
                d = jnp.concatenate([d[:r - 8], d[r - 8:] + dh_refs[j][...]], axis=0)
            di_ref[...] = d.astype(di_ref.dtype)
        for dh_ref, d in zip(dh_refs, dhv):
            dh_ref[...] = d
        for dp_ref, d in zip(dp_refs, dpv):
            dp_ref[...] += d

    row = lambda b, nn: (b * nt + nt - 1 - nn, 0)
    hmap = _halo_map(nt, r)
    in_specs = [pl.BlockSpec(p.shape, lambda b, nn: (0, 0)) for p in params]
    in_specs += [pl.BlockSpec((r, a.shape[1]), row) for a in ins]
    if use_halo:
        in_specs += [pl.BlockSpec((8, a.shape[1]), lambda b, nn: hmap(b, nt - 1 - nn)) for a in ins]
    args = list(params) + list(ins) + (list(ins) if use_halo else [])
    scratch = []
    if has_state:
        in_specs.append(pl.BlockSpec((1, 1) + tuple(state_shape), lambda b, nn: (b, nt - 1 - nn, 0, 0)))
        args.append(states)
        scratch.append(pltpu.VMEM(tuple(state_shape), F32))
    in_specs += [pl.BlockSpec((r, d.shape[1]), row) for d in douts]
    args += list(douts)
    if use_halo:
        scratch += [pltpu.VMEM((8, a.shape[1]), F32) for a in ins]
    out_shape = [jax.ShapeDtypeStruct(a.shape, dt) for a, dt in zip(ins, din_dtypes)]
    out_shape += [jax.ShapeDtypeStruct(p.shape, F32) for p in params]
    o_specs = [pl.BlockSpec((r, a.shape[1]), row) for a in ins]
    o_specs += [pl.BlockSpec(p.shape, lambda b, nn: (0, 0)) for p in params]
    if side is not None:
        body, in_specs, o_specs, out_shape, scratch, args = _host_exchange(
            side, body, in_specs, o_specs, out_shape, scratch, args, (nb, nt))
    res = pl.pallas_call(body, grid=(nb, nt), in_specs=in_specs, out_specs=o_specs, out_shape=out_shape,
                         scratch_shapes=scratch, compiler_params=_cparams(("arbitrary", "arbitrary")),
                         name=name)(*args)
    if side is not None:
        return res[:n_i], res[n_i:n_i + n_p], res[n_i + n_p:]
    return res[:n_i], res[n_i:]


def matmul(name, a, b, mode, out_dtype=F32, addend=None, resid=None, tm=512, tn=None, tk=None):
    if mode == "nn":
        (m, kd), (_, n) = a.shape, b.shape
    elif mode == "nt":
        (m, kd), (n, _) = a.shape, b.shape
    else:
        (kd, m), (_, n) = a.shape, b.shape
    tm, tn, tk = min(tm, m if resid is None else resid[2]), min(tn or n, n), min(tk or kd, kd)
    nk = kd // tk
    assert m % tm == 0 and n % tn == 0 and kd % tk == 0
    dims = {"nn": ((1,), (0,)), "nt": ((1,), (1,)), "tn": ((0,), (0,))}[mode]
    extra = [] if addend is None else [addend]
    if resid is not None:
        extra = [resid[0], resid[1]]
    n_in, n_out = 2 + len(extra), 1 if resid is None else 2

    def body(*refs):
        a_ref, b_ref = refs[0], refs[1]
        part = lax.dot_general(a_ref[...].astype(BF16), b_ref[...].astype(BF16), (dims, ((), ())),
                               preferred_element_type=F32)

        def finish(acc):
            if resid is not None:
                refs[n_in][...] = refs[2][...] + refs[3][0] * acc
                refs[n_in + 1][...] = acc.astype(BF16)
            else:
                if addend is not None:
                    acc = acc + refs[2][...]
                refs[n_in][...] = acc.astype(refs[n_in].dtype)

        if nk == 1:
            finish(part)
        else:
            acc_ref = refs[n_in + n_out]
            k = pl.program_id(2)

            @pl.when(k == 0)
            def _():
                acc_ref[...] = part

            @pl.when(k > 0)
            def _():
                acc_ref[...] += part

            @pl.when(k == nk - 1)
            def _():
                finish(acc_ref[...])

    if mode == "tn":
        a_spec = pl.BlockSpec((tk, tm), lambda j, i, k: (k, i))
    else:
        a_spec = pl.BlockSpec((tm, tk), lambda j, i, k: (i, k))
    if mode == "nt":
        b_spec = pl.BlockSpec((tn, tk), lambda j, i, k: (j, k))
    else:
        b_spec = pl.BlockSpec((tk, tn), lambda j, i, k: (k, j))
    o_spec = pl.BlockSpec((tm, tn), lambda j, i, k: (i, j))
    in_specs = [a_spec, b_spec] + [o_spec] * (len(extra) > 0)
    out_specs, out_shape = o_spec, jax.ShapeDtypeStruct((m, n), out_dtype)
    if resid is not None:
        rows = resid[2]
        assert rows % tm == 0
        in_specs.append(pl.BlockSpec((1, 1, tn), lambda j, i, k: (lax.div(i * tm, rows), 0, j)))
        out_specs, out_shape = [o_spec, o_spec], [out_shape, jax.ShapeDtypeStruct((m, n), BF16)]
    scratch = [pltpu.VMEM((tm, tn), F32)] if nk > 1 else []
    return pl.pallas_call(body, grid=(n // tn, m // tm, nk), in_specs=in_specs, out_specs=out_specs,
                          out_shape=out_shape, scratch_shapes=scratch,
                          compiler_params=_cparams(("parallel", "parallel", "arbitrary")), name=name)(a, b, *extra)


def _normmod(x, nw, shift, scale):
    return _rms(x, nw) * (1.0 + scale) + shift


def _row_specs(nb, s, tr, d):
    nt = s // tr
    row = pl.BlockSpec((tr, d), lambda b, i: (b * nt + i, 0))
    per_seq = pl.BlockSpec((1, 1, d), lambda b, i: (b, 0, 0))
    full = pl.BlockSpec((1, d), lambda b, i: (0, 0))
    return nt, row, per_seq, full


def normmod_fwd(name, x, nw, shift, scale, nb, s, tr=512):
    d, tr = x.shape[1], min(tr, s)
    nt, row, per_seq, full = _row_specs(nb, s, tr, d)

    def body(x_ref, nw_ref, sh_ref, sc_ref, h_ref, ht_ref):
        h = _normmod(x_ref[...], nw_ref[...], sh_ref[0], sc_ref[0])
        h_ref[...] = h.astype(h_ref.dtype)
        ht_ref[...] = h.T.astype(ht_ref.dtype)

    return pl.pallas_call(body, grid=(nb, nt), in_specs=[row, full, per_seq, per_seq],
                          out_specs=[row, pl.BlockSpec((d, tr), lambda b, i: (0, b * nt + i))],
                          out_shape=[jax.ShapeDtypeStruct(x.shape, BF16), jax.ShapeDtypeStruct(x.shape[::-1], BF16)],
                          compiler_params=_cparams(("parallel", "parallel")), name=name)(x, nw, shift, scale)


def normmod_bwd(name, x, nw, shift, scale, dh, dres, nb, s, tr=512):
    d, tr = x.shape[1], min(tr, s)
    nt, row, per_seq, full = _row_specs(nb, s, tr, d)

    def body(x_ref, nw_ref, sh_ref, sc_ref, dh_ref, dres_ref, dx_ref, dnw_ref, dsh_ref, dsc_ref):
        b, i = pl.program_id(0), pl.program_id(1)

        @pl.when((b == 0) & (i == 0))
        def _():
            dnw_ref[...] = jnp.zeros(dnw_ref.shape, F32)

        @pl.when(i == 0)
        def _():
            dsh_ref[...] = jnp.zeros(dsh_ref.shape, F32)
            dsc_ref[...] = jnp.zeros(dsc_ref.shape, F32)

        _, vjp = jax.vjp(_normmod, x_ref[...], nw_ref[...], sh_ref[0], sc_ref[0])
        dx, dnw, dsh, dsc = vjp(dh_ref[...])
        dx_ref[...] = dres_ref[...] + dx
        dnw_ref[...] += dnw
        dsh_ref[0] += dsh
        dsc_ref[0] += dsc

    out_shape = [jax.ShapeDtypeStruct(x.shape, F32), jax.ShapeDtypeStruct((1, d), F32),
                 jax.ShapeDtypeStruct((nb, 1, d), F32), jax.ShapeDtypeStruct((nb, 1, d), F32)]
    return pl.pallas_call(body, grid=(nb, nt), in_specs=[row, full, per_seq, per_seq, row, row],
                          out_specs=[row, full, per_seq, per_seq], out_shape=out_shape,
                          compiler_params=_cparams(("arbitrary", "arbitrary")),
                          name=name)(x, nw, shift, scale, dh, dres)


def _merge_specs(nb, s, tr, d, wbr):
    nt, row, per_seq, _ = _row_specs(nb, s, tr, d)
    o_spec = pl.BlockSpec((tr, wbr), lambda b, i: (b * nt + i, 0))
    g_spec = pl.BlockSpec((tr, 3 * d), lambda b, i: (b * nt + i, 0))
    wbr_spec = pl.BlockSpec((wbr, d), lambda b, i: (0, 0))
    wo_spec = pl.BlockSpec((d, d), lambda b, i: (0, 0))
    return nt, row, per_seq, o_spec, g_spec, wbr_spec, wo_spec


def merge_fwd(name, x, oa, ob, oc, pg, gate1, wa, wb, wc, wo, nb, s, tr=512):
    d, tr = x.shape[1], min(tr, s)
    nt, row, per_seq, o_spec, g_spec, wbr_spec, wo_spec = _merge_specs(nb, s, tr, d, oa.shape[1])

    def body(x_ref, oa_ref, ob_ref, oc_ref, pg_ref, g1_ref, wa_ref, wb_ref, wc_ref, wo_ref,
             x1_ref, ya_ref, yb_ref, yc_ref, mg_ref, mx_ref):
        g = _sigmoid(pg_ref[...])
        ys = [_dg(o[...], w_[...], 1, 0) for o, w_ in ((oa_ref, wa_ref), (ob_ref, wb_ref), (oc_ref, wc_ref))]
        merged = g[:, :d] * ys[0] + g[:, d:2 * d] * ys[1] + g[:, 2 * d:] * ys[2]
        mix = _dg(merged, wo_ref[...], 1, 0)
        x1_ref[...] = x_ref[...] + g1_ref[0] * mix
        for r, v in zip((ya_ref, yb_ref, yc_ref, mg_ref, mx_ref), ys + [merged, mix]):
            r[...] = v.astype(r.dtype)

    return pl.pallas_call(body, grid=(nb, nt),
                          in_specs=[row, o_spec, o_spec, o_spec, g_spec, per_seq, wbr_spec, wbr_spec, wbr_spec, wo_spec],
                          out_specs=[row] * 6,
                          out_shape=[jax.ShapeDtypeStruct(x.shape, F32)] + [jax.ShapeDtypeStruct(x.shape, BF16)] * 5,
                          compiler_params=_cparams(("parallel", "parallel")),
                          name=name)(x, oa, ob, oc, pg, gate1, wa, wb, wc, wo)


def merge_bwd(name, oa, ob, oc, pg, gate1, wa, wb, wc, wo, ys, merged, mix, dx1, nb, s, tr=256):
    d, tr = dx1.shape[1], min(tr, s)
    wbr = oa.shape[1]
    nt, row, per_seq, o_spec, g_spec, wbr_spec, wo_spec = _merge_specs(nb, s, tr, d, wbr)

    def body(oa_ref, ob_ref, oc_ref, pg_ref, g1_ref, wa_ref, wb_ref, wc_ref, wo_ref, ya_ref, yb_ref, yc_ref, mg_ref,
             mx_ref, dx_ref, doa_ref, dob_ref, doc_ref, dpg_ref, dg1_ref, dwa_ref, dwb_ref, dwc_ref, dwo_ref):
        b, i = pl.program_id(0), pl.program_id(1)

        @pl.when((b == 0) & (i == 0))
        def _():
            for r in (dwa_ref, dwb_ref, dwc_ref, dwo_ref):
                r[...] = jnp.zeros(r.shape, F32)

        @pl.when(i == 0)
        def _():
            dg1_ref[...] = jnp.zeros(dg1_ref.shape, F32)

        dx = dx_ref[...]
        dg1_ref[0] += jnp.sum(dx * mx_ref[...].astype(F32), axis=0, keepdims=True)
        dmix = g1_ref[0] * dx
        dmerged = _dg(dmix, wo_ref[...], 1, 1)
        dwo_ref[...] += _dg(mg_ref[...], dmix, 0, 0)
        g = _sigmoid(pg_ref[...])
        branches = ((oa_ref, wa_ref, ya_ref, doa_ref, dwa_ref), (ob_ref, wb_ref, yb_ref, dob_ref, dwb_ref),
                    (oc_ref, wc_ref, yc_ref, doc_ref, dwc_ref))
        dgs = []
        for k, (o_ref, w_ref, y_ref, do_ref, dw_ref) in enumerate(branches):
            gk = g[:, k * d:(k + 1) * d]
            dy = dmerged * gk
            dgs.append(dmerged * y_ref[...].astype(F32) * gk * (1.0 - gk))
            do_ref[...] = _dg(dy, w_ref[...], 1, 1)
            dw_ref[...] += _dg(o_ref[...], dy, 0, 0)
        dpg_ref[...] = jnp.concatenate(dgs, axis=1)

    t = nb * s
    out_shape = ([jax.ShapeDtypeStruct((t, wbr), F32)] * 3
                 + [jax.ShapeDtypeStruct((t, 3 * d), F32), jax.ShapeDtypeStruct((nb, 1, d), F32)]
                 + [jax.ShapeDtypeStruct((wbr, d), F32)] * 3 + [jax.ShapeDtypeStruct((d, d), F32)])
    return pl.pallas_call(body, grid=(nb, nt),
                          in_specs=[o_spec, o_spec, o_spec, g_spec, per_seq, wbr_spec, wbr_spec, wbr_spec, wo_spec]
                          + [row] * 6,
                          out_specs=[o_spec, o_spec, o_spec, g_spec, per_seq, wbr_spec, wbr_spec, wbr_spec, wo_spec],
                          out_shape=out_shape, compiler_params=_cparams(("arbitrary", "arbitrary")),
                          name=name)(oa, ob, oc, pg, gate1, wa, wb, wc, wo, *ys, merged, mix, dx1)


def resid_bwd(name, dx, f, gate, nb, s, tr=512):
    d, tr = dx.shape[1], min(tr, s)
    nt, row, per_seq, _ = _row_specs(nb, s, tr, d)

    def body(dx_ref, f_ref, g_ref, df_ref, dg_ref):
        @pl.when(pl.program_id(1) == 0)
        def _():
            dg_ref[...] = jnp.zeros(dg_ref.shape, F32)

        df_ref[...] = (g_ref[0] * dx_ref[...]).astype(df_ref.dtype)
        dg_ref[0] += jnp.sum(dx_ref[...] * f_ref[...], axis=0, keepdims=True)

    return pl.pallas_call(body, grid=(nb, nt), in_specs=[row, row, per_seq], out_specs=[row, per_seq],
                          out_shape=[jax.ShapeDtypeStruct(dx.shape, BF16), jax.ShapeDtypeStruct((nb, 1, d), F32)],
                          compiler_params=_cparams(("arbitrary", "arbitrary")), name=name)(dx, f, gate)


def loss_head(name, x, fw, target, tr=512):
    t, d = x.shape
    row = pl.BlockSpec((tr, d), lambda i: (i, 0))
    full = pl.BlockSpec((1, d), lambda i: (0, 0))

    def loss_fn(xv, fwv, tv):
        err = _rms(xv, fwv) - tv
        return 0.5 * jnp.sum(jnp.mean(err * err, axis=-1))

    def body(x_ref, fw_ref, t_ref, dx_ref, l_ref, dfw_ref):
        @pl.when(pl.program_id(0) == 0)
        def _():
            l_ref[...] = jnp.zeros(l_ref.shape, F32)
            dfw_ref[...] = jnp.zeros(dfw_ref.shape, F32)

        val, (dx, dfw) = jax.value_and_grad(loss_fn, argnums=(0, 1))(x_ref[...], fw_ref[...], t_ref[...])
        dx_ref[...] = dx
        l_ref[...] += val
        dfw_ref[...] += dfw

    return pl.pallas_call(body, grid=(t // tr,), in_specs=[row, full, row],
                          out_specs=[row, pl.BlockSpec((1, 128), lambda i: (0, 0)), full],
                          out_shape=[jax.ShapeDtypeStruct((t, d), F32), jax.ShapeDtypeStruct((1, 128), F32),
                                     jax.ShapeDtypeStruct((1, d), F32)],
                          compiler_params=_cparams(("arbitrary",)), name=name)(x, fw, target)


def _row_tile(rows, cols, n_arrays):
    budget = 24 * 1024 * 1024 // (8 * cols * max(n_arrays, 1))
    tr = rows
    while tr > max(budget, 16) and tr % 2 == 0 and (tr // 2) % 16 == 0:
        tr //= 2
    return tr


def elementwise(name, fn, ins, out_dtypes):
    rows, cols = ins[0].shape
    tr = _row_tile(rows, cols, len(ins) + len(out_dtypes))
    spec = pl.BlockSpec((tr, cols), lambda i: (i, 0))
    n_in = len(ins)

    def body(*refs):
        outs = fn(*[r[...] for r in refs[:n_in]])
        for o_ref, o in zip(refs[n_in:], outs):
            o_ref[...] = o.astype(o_ref.dtype)

    return pl.pallas_call(body, grid=(rows // tr,), in_specs=[spec] * n_in, out_specs=[spec] * len(out_dtypes),
                          out_shape=[jax.ShapeDtypeStruct((rows, cols), dt) for dt in out_dtypes],
                          compiler_params=_cparams(("parallel",)), name=name)(*ins)


def _adamw(w, g, m, v):
    m = ADAM_B1 * m + (1.0 - ADAM_B1) * g
    v = ADAM_B2 * v + (1.0 - ADAM_B2) * (g * g)
    m_hat = m / (1.0 - ADAM_B1 ** ADAM_STEP)
    v_hat = v / (1.0 - ADAM_B2 ** ADAM_STEP)
    delta = -ADAM_LR * (m_hat / (jnp.sqrt(v_hat) + ADAM_EPS) + ADAM_WD * w)
    return delta, m, v


def adamw(name, w, g, m, v):
    return elementwise(name, _adamw, [w, g, m, v], [F32, F32, F32])


_ANY = pl.BlockSpec(memory_space=pl.ANY)


def _coords():
    return lax.axis_index("x"), lax.axis_index("y"), lax.axis_index("c")


def allgather8(name, arrays, halves):
    n = len(arrays)

    def body(*refs):
        in_refs, out_refs = refs[:n], refs[n:2 * n]
        send_sems, recv_sems, local_sems = refs[2 * n:]
        x, y, c = _coords()
        me, sibling = (x, y, c), (x, y, 1 - c)
        chips = [(1 - x, y), (x, 1 - y), (1 - x, 1 - y)]

        def blk(i, px, py, pc):
            return out_refs[i].at[4 * px + 2 * py + pc]

        def piece(i):
            return in_refs[i].at[c] if halves[i] else in_refs[i]

        def copy(i, k, block, to, src=None):
            return pltpu.make_async_remote_copy(
                src_ref=blk(i, *block) if src is None else src, dst_ref=blk(i, *block),
                send_sem=send_sems.at[7 * i + k], recv_sem=recv_sems.at[7 * i + k],
                device_id=to, device_id_type=MESH)

        mine = [pltpu.make_async_copy(piece(i), blk(i, *me), local_sems.at[i]) for i in range(n)]
        for cp in mine:
            cp.start()
        first = []
        for i in range(n):
            first.append(copy(i, 0, me, sibling, src=piece(i)))
            first += [copy(i, 1 + j, me, (*chip, c), src=piece(i)) for j, chip in enumerate(chips)]
        for cp in first:
            cp.start()
        passed = []
        for j, chip in enumerate(chips):
            for i in range(n):
                copy(i, 1 + j, (*chip, c), me).wait_recv()
                fwd = copy(i, 4 + j, (*chip, c), sibling)
                fwd.start()
                passed.append(fwd)
        for i in range(n):
            copy(i, 0, sibling, me).wait_recv()
            for j, chip in enumerate(chips):
                copy(i, 4 + j, (*chip, 1 - c), me).wait_recv()
        for cp in first + passed:
            cp.wait_send()
        for cp in mine:
            cp.wait()

    out_shape = []
    for a, hv in zip(arrays, halves):
        out_shape.append(jax.ShapeDtypeStruct((N_DEV,) + tuple(a.shape[1:] if hv else a.shape), a.dtype))
    return pl.pallas_call(
        body, in_specs=[_ANY] * n, out_specs=[_ANY] * n, out_shape=out_shape,
        scratch_shapes=[pltpu.SemaphoreType.DMA((7 * n,)), pltpu.SemaphoreType.DMA((7 * n,)),
                        pltpu.SemaphoreType.DMA((n,))],
        name=name)(*arrays)


def exchange(name, ins, out_shapes, plan, local_plan=()):
    n_in, n_out = len(ins), len(out_shapes)

    def body(*refs):
        copies = _exchange_copies(plan, local_plan, refs[:n_in], refs[n_in:n_in + n_out], *refs[n_in + n_out:])
        for cp in copies:
            cp.start()
        for cp in copies:
            cp.wait()

    return pl.pallas_call(
        body, in_specs=[_ANY] * n_in, out_specs=[_ANY] * n_out, out_shape=out_shapes,
        scratch_shapes=_exchange_sems(plan, local_plan), name=name)(*ins)


def sum_halves(name, gs, recv, c_arr, chip_arr):
    _, _, hr, cs = gs.shape
    tr = _row_tile(hr, cs, 4)

    def body(c_ref, chip_ref, g_ref, r_ref, qf_ref, qb_ref):
        q = g_ref[0, 0] + r_ref[0, 0]
        qb_ref[0] = q.astype(BF16)

        @pl.when(pl.program_id(1) == chip_ref[0])
        def _():
            qf_ref[...] = q

    grid_spec = pltpu.PrefetchScalarGridSpec(
        num_scalar_prefetch=2, grid=(hr // tr, N_CHIPS),
        in_specs=[pl.BlockSpec((1, 1, tr, cs), lambda i, j, c_ref, chip_ref: (j, c_ref[0], i, 0)),
                  pl.BlockSpec((1, 1, tr, cs), lambda i, j, c_ref, chip_ref: (j, 0, i, 0))],
        out_specs=[pl.BlockSpec((tr, cs), lambda i, j, c_ref, chip_ref: (i, 0)),
                   pl.BlockSpec((1, tr, cs), lambda i, j, c_ref, chip_ref: (j, i, 0))])
    return pl.pallas_call(body, grid_spec=grid_spec,
                          out_shape=[jax.ShapeDtypeStruct((hr, cs), F32),
                                     jax.ShapeDtypeStruct((N_CHIPS, hr, cs), BF16)],
                          compiler_params=_cparams(("parallel", "arbitrary")), name=name)(c_arr, chip_arr, gs, recv)


def sum_chips(name, qf, recv):
    (total,) = elementwise(name, lambda q, a, b, c: (q + a.astype(F32) + b.astype(F32) + c.astype(F32),),
                           [qf] + list(recv), [F32])
    return total


def adamw_halves(name, w, m, v, layer, g_mine, g_other, c_arr, prev=None):
    _, _, hr, cs = w.shape
    tr = _row_tile(hr, cs, 9)

    def body(c_ref, w_ref, m_ref, v_ref, gm_ref, go_ref, *rest):
        g_ref, d_ref, nm_ref, nv_ref = rest[-4:]
        g = jnp.where(pl.program_id(0) == c_ref[0], gm_ref[...], go_ref[...])
        delta, nm, nv = _adamw(w_ref[0, 0], g, m_ref[0, 0], v_ref[0, 0])
        g_ref[0, 0], d_ref[0, 0], nm_ref[0, 0], nv_ref[0, 0] = g, delta, nm, nv

    half = pl.BlockSpec((1, 1, tr, cs), lambda h, i, c_ref: (layer, h, i, 0))
    row = pl.BlockSpec((tr, cs), lambda h, i, c_ref: (i, 0))
    in_specs, args, aliases = [half, half, half, row, row], [c_arr, w, m, v, g_mine, g_other], {}
    if prev is not None:
        in_specs += [pl.BlockSpec(memory_space=pl.ANY)] * 4
        args += list(prev)
        aliases = {6 + k: k for k in range(4)}
    grid_spec = pltpu.PrefetchScalarGridSpec(num_scalar_prefetch=1, grid=(2, hr // tr),
                                             in_specs=in_specs, out_specs=[half] * 4)
    return pl.pallas_call(body, grid_spec=grid_spec, out_shape=[jax.ShapeDtypeStruct(w.shape, F32)] * 4,
                          input_output_aliases=aliases, compiler_params=_cparams(("parallel", "parallel")),
                          name=name)(*args)


def sum8(name, g):
    _, rows, cols = g.shape
    tr = _row_tile(rows, cols, 9)

    def body(*refs):
        acc = refs[0][0]
        for r in refs[1:N_DEV]:
            acc = acc + r[0]
        refs[N_DEV][...] = acc

    in_specs = [pl.BlockSpec((1, tr, cols), functools.partial(lambda k, i: (k, i, 0), k)) for k in range(N_DEV)]
    return pl.pallas_call(body, grid=(rows // tr,), in_specs=in_specs,
                          out_specs=pl.BlockSpec((tr, cols), lambda i: (i, 0)),
                          out_shape=jax.ShapeDtypeStruct((rows, cols), F32),
                          compiler_params=_cparams(("parallel",)), name=name)(*([g] * N_DEV))


_QKV, _AB, _GZ = (0, 1536), (1536, 1544), (1544, 2056)
_HG = (2056, 4104)
_SZ, _XBC, _DT = (4104, 4616), (4616, 5640), (5640, 5648)
_GATES = (5648, 8720)


def _split_w_in(w8):
    _, hr, cs = w8.shape
    w4 = w8.reshape(N_CHIPS, 2 * hr, cs)

    def cols(rng):
        lo, hi = rng
        return [w4[j][:, max(lo, j * cs) - j * cs:min(hi, (j + 1) * cs) - j * cs]
                for j in range(N_CHIPS) if max(lo, j * cs) < min(hi, (j + 1) * cs)]

    pad = [jnp.zeros((2 * hr, 120), w8.dtype)]
    return (jnp.concatenate(cols(_GATES), axis=1),
            jnp.concatenate(cols(_QKV) + cols(_GZ) + cols(_AB) + pad, axis=1),
            jnp.concatenate(cols(_HG), axis=1),
            jnp.concatenate(cols(_SZ) + cols(_XBC) + cols(_DT) + pad, axis=1))


def _stack_w_in(g, a, b, c):
    segments = [(a, 0, 1536), (a, 2048, 2056), (a, 1536, 2048), (b, 0, 2048), (c, 0, 512), (c, 512, 1536),
                (c, 1536, 1544), (g, 0, 3072)]
    cs = sum(s1 - s0 for _, s0, s1 in segments) // N_CHIPS
    chips = []
    for j in range(N_CHIPS):
        parts, off = [], 0
        for arr, s0, s1 in segments:
            u0, u1 = max(j * cs, off), min((j + 1) * cs, off + s1 - s0)
            if u0 < u1:
                parts.append(arr[:, s0 + u0 - off:s0 + u1 - off])
            off += s1 - s0
        chips.append(jnp.concatenate(parts, axis=1))
    rows = g.shape[0]
    return jnp.stack(chips).reshape(N_CHIPS, 2, rows // 2, cs)


def _rows8(rows, width):
    out = [jnp.pad(r.astype(F32), (0, width - r.shape[0])) for r in rows]
    out += [jnp.zeros((width,), F32)] * (8 - len(out))
    return jnp.stack(out)


class _Packer:
    def __init__(self):
        self.items, self.size = [], 0

    def add(self, name, shape):
        n = 1
        for d in shape:
            n *= d
        self.items.append((name, tuple(shape), self.size, n))
        self.size += n

    def rows(self):
        return -(-self.size // 8192) * 8

    def pack(self, values):
        flat = [values[name].astype(F32).reshape(-1) for name, _, _, _ in self.items]
        flat.append(jnp.zeros((self.rows() * 1024 - self.size,), F32))
        return jnp.concatenate(flat).reshape(self.rows(), 1024)

    def unpack(self, buf):
        flat = buf.reshape(-1)
        return {name: flat[off:off + n].reshape(shape) for name, shape, off, n in self.items}


def _stack_by_chip(g, axis):
    l, r, c = g.shape
    if axis == 2:
        cs = c // N_CHIPS
        g = g.reshape(l, r, N_CHIPS, cs).transpose(2, 0, 1, 3).reshape(N_CHIPS, 2, l * r // 2, cs)
    else:
        rs = r // N_CHIPS
        g = g.reshape(l, N_CHIPS, rs, c).transpose(1, 0, 2, 3).reshape(N_CHIPS, 2, l * rs // 2, c)
    return g


def _unstack_gathered(w8, l, axis):
    _, hr, cs = w8.shape
    w = w8.reshape(N_CHIPS, l, 2 * hr // l, cs)
    if axis == 2:
        return w.transpose(1, 2, 0, 3).reshape(l, 2 * hr // l, N_CHIPS * cs)
    return w.transpose(1, 0, 2, 3).reshape(l, N_CHIPS * 2 * hr // l, cs)


_BIG = (("w_in", 2), ("w_br_a", 2), ("w_br_b", 2), ("w_br_c", 2), ("w_out", 1), ("ffn_w_up", 2), ("ffn_w_down", 1))
_SMALL = ("b_ada", "norm1_w", "gdn_conv_w", "gdn_a_log", "gdn_dt_bias", "gdn_norm_w", "hgrn_lb_param",
          "hgrn_norm_w", "ssd_conv_w", "ssd_conv_b", "ssd_a_log", "ssd_dt_bias", "ssd_d", "ssd_norm_w",
          "norm2_w", "ffn_conv_w", "ffn_conv_b", "final_norm_w")
_WEIGHTS = ("w_ada", "b_ada", "norm1_w", "w_in", "gdn_conv_w", "gdn_a_log", "gdn_dt_bias", "gdn_norm_w",
            "hgrn_lb_param", "hgrn_norm_w", "ssd_conv_w", "ssd_conv_b", "ssd_a_log", "ssd_dt_bias", "ssd_d",
            "ssd_norm_w", "w_br_a", "w_br_b", "w_br_c", "w_out", "norm2_w", "ffn_w_up", "ffn_conv_w",
            "ffn_conv_b", "ffn_w_down", "final_norm_w")
_R_GDN, _R_HGRN, _R_SSD, _R_FFN = 256, 128, 256, 256


_MASKS = ((1, 0), (0, 1), (1, 1))


def _flip(k, x, y):
    return (1 - x if _MASKS[k][0] else x), (1 - y if _MASKS[k][1] else y)


def _rs_d2d(grads):
    plan = [functools.partial(lambda i, ins, outs, x, y, c: (ins[i].at[:, pl.ds(1 - c, 1)], outs[i], (x, y, 1 - c)), i)
            for i in range(len(grads))]
    return grads, [jax.ShapeDtypeStruct((N_CHIPS, 1) + g.shape[2:], F32) for g in grads], plan, (), ()


def _rs_ici(grads, recv, tag):
    n = len(grads)
    c_arr = lax.axis_index("c").astype(jnp.int32).reshape(1)
    chip_arr = (2 * lax.axis_index("x") + lax.axis_index("y")).astype(jnp.int32).reshape(1)
    q = [sum_halves("rs_sum_d2d%s_%d" % (tag, i), g, r, c_arr, chip_arr) for i, (g, r) in enumerate(zip(grads, recv))]
    qf, qb = [a for a, _ in q], [b for _, b in q]

    def ici(i, k, ins, outs, x, y, c):
        px, py = _flip(k, x, y)
        return ins[i].at[2 * px + py], outs[3 * i + k], (px, py, c)

    plan = [functools.partial(ici, i, k) for i in range(n) for k in range(3)]
    shapes = [jax.ShapeDtypeStruct(g.shape[2:], BF16) for g in grads for _ in range(3)]
    return qf, (qb, shapes, plan, (), ())


def _rs_finish(qf, res, tag):
    n = len(qf)
    red = [sum_chips("rs_sum_ici%s_%d" % (tag, i), qf[i], res[3 * i:3 * i + 3]) for i in range(n)]
    plan = [functools.partial(lambda i, ins, outs, x, y, c: (ins[i], outs[i], (x, y, 1 - c)), i) for i in range(n)]
    other = exchange("rs_swap" + tag, red, [jax.ShapeDtypeStruct(r.shape, F32) for r in red], plan)
    return red, other


def _join_sides(sides):
    ins, shapes, plan = [], [], []
    for s_ins, s_shapes, s_plan, _, _ in sides:
        def shifted(fn, i0, i1, o0, o1, in_refs, out_refs, x, y, c):
            return fn(in_refs[i0:i1], out_refs[o0:o1], x, y, c)

        i0, o0 = len(ins), len(shapes)
        plan += [functools.partial(shifted, fn, i0, i0 + len(s_ins), o0, o0 + len(s_shapes)) for fn in s_plan]
        ins += list(s_ins)
        shapes += list(s_shapes)
    return ins, shapes, plan, (), ()


def _gather_side(pieces):
    n = len(pieces)

    def send(i, k, ins, outs, x, y, c):
        px, py = _flip(k, x, y)
        return ins[i].at[c], outs[i].at[2 * (2 * x + y) + c], (px, py, c)

    def to_sibling(i, h, ins, outs, x, y, c):
        return ins[i].at[h], outs[i].at[2 * (2 * x + y) + h], (x, y, 1 - c)

    def pass_on(i, k, ins, outs, x, y, c):
        px, py = _flip(k, x, y)
        blk = 2 * (2 * px + py) + c
        return outs[i].at[blk], outs[i].at[blk], (x, y, 1 - c)

    plan = [functools.partial(send, i, k) for i in range(n) for k in range(3)]
    plan += [functools.partial(to_sibling, i, h) for i in range(n) for h in range(2)]
    then = [functools.partial(pass_on, i, k) for i in range(n) for k in range(3)]
    shapes = [jax.ShapeDtypeStruct((N_DEV,) + p.shape[1:], p.dtype) for p in pieces]
    return pieces, shapes, plan, (), then


def kernel(x, c, w_ada, b_ada, norm1_w, w_in, gdn_conv_w, gdn_a_log, gdn_dt_bias, gdn_norm_w, hgrn_lb_param, hgrn_norm_w, ssd_conv_w, ssd_conv_b, ssd_a_log, ssd_dt_bias, ssd_d, ssd_norm_w, w_br_a, w_br_b, w_br_c, w_out, norm2_w, ffn_w_up, ffn_conv_w, ffn_conv_b, ffn_w_down, final_norm_w, loss_target, m_w_ada, m_b_ada, m_norm1_w, m_w_in, m_gdn_conv_w, m_gdn_a_log, m_gdn_dt_bias, m_gdn_norm_w, m_hgrn_lb_param, m_hgrn_norm_w, m_ssd_conv_w, m_ssd_conv_b, m_ssd_a_log, m_ssd_dt_bias, m_ssd_d, m_ssd_norm_w, m_w_br_a, m_w_br_b, m_w_br_c, m_w_out, m_norm2_w, m_ffn_w_up, m_ffn_conv_w, m_ffn_conv_b, m_ffn_w_down, m_final_norm_w, v_w_ada, v_b_ada, v_norm1_w, v_w_in, v_gdn_conv_w, v_gdn_a_log, v_gdn_dt_bias, v_gdn_norm_w, v_hgrn_lb_param, v_hgrn_norm_w, v_ssd_conv_w, v_ssd_conv_b, v_ssd_a_log, v_ssd_dt_bias, v_ssd_d, v_ssd_norm_w, v_w_br_a, v_w_br_b, v_w_br_c, v_w_out, v_norm2_w, v_ffn_w_up, v_ffn_conv_w, v_ffn_conv_b, v_ffn_w_down, v_final_norm_w):
    loc = dict(locals())
    w = {k: loc[k] for k in _WEIGHTS}
    mom = {k: loc["m_" + k] for k in _WEIGHTS}
    var = {k: loc["v_" + k] for k in _WEIGHTS}
    nb, s, d = x.shape
    t = nb * s
    depth = w_ada.shape[0]
    chip = 2 * lax.axis_index("x") + lax.axis_index("y")
    dev = 2 * chip + lax.axis_index("c")
    x0 = x.reshape(t, d)
    target = loss_target.reshape(t, d)

    small_in = [c, gdn_conv_w.reshape(depth * 4, -1), ssd_conv_w.reshape(depth * 4, -1),
                ffn_conv_w.reshape(depth * 3, -1)]
    c_all, gcw, scw, fcw = allgather8("ag_small", small_in, [False] * 4)
    c_all = c_all.reshape(N_DEV * nb, d)

    def conv_full(g, taps):
        g = g[::2].reshape(N_CHIPS, depth, taps, -1)
        return g.transpose(1, 2, 0, 3).reshape(depth, taps, -1)

    gdn_cw, ssd_cw, ffn_cw = conv_full(gcw, 4), conv_full(scw, 4), conv_full(fcw, 3)

    axis_of = dict(_BIG)
    first_needed, later = ("w_in",), tuple(n for n, _ in _BIG if n != "w_in")
    wls = [dict() for _ in range(depth)]

    def pieces(keys):
        out = []
        for l, name in keys:
            a = w[name][l].astype(BF16)
            out.append(a.reshape(2, a.shape[0] // 2, a.shape[1]))
        return out

    def arrived(keys, bufs):
        for (l, name), g in zip(keys, bufs):
            if name == "w_in":
                wls[l]["w_g"], wls[l]["w_a"], wls[l]["w_b"], wls[l]["w_c"] = _split_w_in(g)
            else:
                wls[l][name] = _unstack_gathered(g, 1, axis_of[name])[0]

    keys0 = [(0, n) for n in first_needed]
    arrived(keys0, allgather8("ag_weights0", pieces(keys0), [True] * len(keys0)))

    (c_act,) = elementwise("silu_c", lambda v: (_silu(v),), [c_all], [F32])
    mod_cols = jnp.concatenate([matmul("ada_fwd%d" % l, c_act, w_ada[l], "nn") for l in range(depth)], axis=0)
    (mod8,) = allgather8("ag_mod", [mod_cols], [False])
    mod = mod8[::2].reshape(N_CHIPS, depth, N_DEV * nb, -1).transpose(1, 2, 0, 3).reshape(depth, N_DEV * nb, 6 * d)
    mod = lax.dynamic_slice_in_dim(mod, dev * nb, nb, axis=1) + b_ada[:, None, :]

    def mod_part(l, k):
        return mod[l, :, k * d:(k + 1) * d].reshape(nb, 1, d)

    saved = []
    xl = x0
    for l in range(depth):
        sfx = str(l)
        wl = wls[l]
        sv = {"x0": xl}
        shift1, scale1, gate1, shift2, scale2, gate2 = [mod_part(l, k) for k in range(6)]
        sv["mods"] = (shift1, scale1, gate1, shift2, scale2, gate2)
        h, h_t = normmod_fwd("norm1_fwd" + sfx, xl, norm1_w[l][None], shift1, scale1, nb, s)
        pg = matmul("proj_g" + sfx, h, wl["w_g"], "nn")
        pa = matmul("proj_a" + sfx, h, wl["w_a"], "nn")
        pb = matmul("proj_b" + sfx, h, wl["w_b"], "nn")
        pc = matmul("proj_c" + sfx, h, wl["w_c"], "nn")
        gdn_p = [_rows8(list(gdn_cw[l]), 1536), _rows8([gdn_a_log[l], gdn_dt_bias[l], gdn_norm_w[l]], 128)]
        hgrn_p = [_rows8(list(hgrn_lb_param), 512), _rows8([hgrn_norm_w[l]], 128)]
        ssd_p = [_rows8(list(ssd_cw[l]), 1024), _rows8([ssd_conv_b[l], ssd_norm_w[l]], 1024),
                 _rows8([ssd_a_log[l], ssd_dt_bias[l], ssd_d[l]], 128)]
        ffn_p = [_rows8(list(ffn_cw[l]) + [ffn_conv_b[l]], 2 * FFN_HIDDEN)]
        hgrn_fn = make_hgrn_tile(l, depth)
        keys = [(l, n) for n in later] + ([(l + 1, n) for n in first_needed] if l + 1 < depth else [])
        oa, st_a, *bufs = seq_fwd("gdn_fwd" + sfx, gdn_tile, gdn_p, [pa], True, [(512, BF16)], (512, 128), nb, s,
                                  _R_GDN, side=_gather_side(pieces(keys)))
        arrived(keys, bufs)
        ob, st_b = seq_fwd("hgrn_fwd" + sfx, hgrn_fn, hgrn_p, [pb], False, [(512, BF16)], (512, 128), nb, s, _R_HGRN)
        oc, st_c = seq_fwd("ssd_fwd" + sfx, ssd_tile, ssd_p, [pc], True, [(512, BF16)], (256, 256), nb, s, _R_SSD)
        x1, *merge_saved = merge_fwd("merge_fwd" + sfx, xl, oa, ob, oc, pg, gate1, wl["w_br_a"], wl["w_br_b"],
                                     wl["w_br_c"], wl["w_out"], nb, s)
        h2, h2_t = normmod_fwd("norm2_fwd" + sfx, x1, norm2_w[l][None], shift2, scale2, nb, s)
        u = matmul("ffn_up" + sfx, h2, wl["ffn_w_up"], "nn", tn=FFN_HIDDEN)
        act, act_t = seq_fwd("convglu_fwd" + sfx, convglu_tile_t, ffn_p, [u], True,
                             [(FFN_HIDDEN, BF16), (FFN_HIDDEN, BF16, "T")], None, nb, s, _R_FFN)
        xl, f = matmul("ffn_down" + sfx, act, wl["ffn_w_down"], "nn", resid=(x1, gate2, s))
        sv.update(merge_saved=merge_saved, h_t=h_t, h2_t=h2_t, act_t=act_t, pg=pg, pa=pa, pb=pb, pc=pc, oa=oa, ob=ob, oc=oc, st_a=st_a, st_b=st_b, st_c=st_c, x1=x1,
                  u=u, f=f, gdn_p=gdn_p, hgrn_p=hgrn_p, ssd_p=ssd_p, ffn_p=ffn_p, hgrn_fn=hgrn_fn)
        saved.append(sv)

    dx, loss_part, d_final = loss_head("loss_head", xl, final_norm_w[None], target)

    sg = {}
    dmod = [None] * depth
    d_lb = None
    reduced = {}
    early = [n for n, _ in _BIG if n != "w_in"]
    to_d2d = to_ici = None

    def finish(keys, qf, res, tag):
        for key, mine, other in zip(keys, *_rs_finish(qf, res, tag)):
            reduced[key] = (mine, other)
    for l in reversed(range(depth)):
        sfx = str(l)
        sv, wl = saved[l], wls[l]
        gfull = {}
        shift1, scale1, gate1, shift2, scale2, gate2 = sv["mods"]
        df, dgate2 = resid_bwd("resid_bwd" + sfx, dx, sv["f"], gate2, nb, s)
        dact = matmul("ffn_down_dx" + sfx, df, wl["ffn_w_down"], "nt")
        gfull["ffn_w_down"] = matmul("ffn_down_dw" + sfx, sv["act_t"], df, "nn", tm=1408, tn=512, tk=4096)
        cg_args = ("convglu_bwd" + sfx, convglu_tile, sv["ffn_p"], [sv["u"]], True, None, [dact], [BF16], None, nb, s,
                   _R_FFN)
        if to_d2d is None:
            (du,), (dcw,) = seq_bwd(*cg_args)
        else:
            lp, stacked = to_d2d
            (du,), (dcw,), recv = seq_bwd(*cg_args, side=_rs_d2d(stacked))
            to_ici = ([(lp, "w_in")],) + _rs_ici(stacked, recv, "i%d" % lp) + ("i%d" % lp,)
        dh2 = matmul("ffn_up_dx" + sfx, du, wl["ffn_w_up"], "nt")
        gfull["ffn_w_up"] = matmul("ffn_up_dw" + sfx, sv["h2_t"], du, "nn", tm=1024, tn=512, tk=4096)
        dx1, dnw2, dshift2, dscale2 = normmod_bwd("norm2_bwd" + sfx, sv["x1"], norm2_w[l][None], shift2, scale2, dh2, dx,
                                                  nb, s)
        doa, dob, doc, dpg, dgate1, dwa, dwb, dwc, dwo = merge_bwd(
            "merge_bwd" + sfx, sv["oa"], sv["ob"], sv["oc"], sv["pg"], gate1, wl["w_br_a"], wl["w_br_b"],
            wl["w_br_c"], wl["w_out"], sv["merge_saved"][:3], *sv["merge_saved"][3:], dx1, nb, s)
        gfull["w_br_a"], gfull["w_br_b"], gfull["w_br_c"], gfull["w_out"] = dwa, dwb, dwc, dwo
        gdn_args = ("gdn_bwd" + sfx, gdn_tile, sv["gdn_p"], [sv["pa"]], True, sv["st_a"], [doa], [F32], (512, 128),
                    nb, s, _R_GDN)
        stacked = [_stack_by_chip(gfull[n][None], axis_of[n]) for n in early]
        (dpc,), (dscw, dspv, dsps), recv = seq_bwd("ssd_bwd" + sfx, ssd_tile, sv["ssd_p"], [sv["pc"]], True, sv["st_c"],
                                                   [doc], [F32], (256, 256), nb, s, _R_SSD, side=_rs_d2d(stacked))
        hosted = [([(l, n) for n in early],) + _rs_ici(stacked, recv, "e" + sfx) + ("e" + sfx,)]
        if to_ici is not None:
            hosted.append(to_ici)
        (dpa,), (dgcw, dgpk), res = seq_bwd(*gdn_args, side=_join_sides([h[2] for h in hosted]))
        for keys, qf, side, tag in hosted:
            finish(keys, qf, res[:len(side[1])], tag)
            res = res[len(side[1]):]
        (dpb,), (dlbp, dhnw) = seq_bwd("hgrn_bwd" + sfx, sv["hgrn_fn"], sv["hgrn_p"], [sv["pb"]], False, sv["st_b"],
                                       [dob], [F32], (512, 128), nb, s, _R_HGRN)
        dh = matmul("proj_g_dx" + sfx, dpg, wl["w_g"], "nt")
        dh = matmul("proj_a_dx" + sfx, dpa, wl["w_a"], "nt", addend=dh)
        dh = matmul("proj_b_dx" + sfx, dpb, wl["w_b"], "nt", addend=dh)
        dh = matmul("proj_c_dx" + sfx, dpc, wl["w_c"], "nt", addend=dh)
        stacked_w_in = _stack_w_in(
            matmul("proj_g_dw" + sfx, sv["h_t"], dpg, "nn", tm=1024, tn=512, tk=4096),
            matmul("proj_a_dw" + sfx, sv["h_t"], dpa, "nn", tm=1024, tk=1024),
            matmul("proj_b_dw" + sfx, sv["h_t"], dpb, "nn", tm=1024, tn=512, tk=4096),
            matmul("proj_c_dw" + sfx, sv["h_t"], dpc, "nn", tm=1024, tk=1024))
        dx, dnw1, dshift1, dscale1 = normmod_bwd("norm1_bwd" + sfx, sv["x0"], norm1_w[l][None], shift1, scale1, dh, dx1,
                                                 nb, s)
        dmod[l] = jnp.concatenate([dshift1, dscale1, dgate1, dshift2, dscale2, dgate2], axis=-1).reshape(nb, 6 * d)
        d_lb = dlbp[:depth] if d_lb is None else d_lb + dlbp[:depth]
        sg[l] = dict(norm1_w=dnw1[0], norm2_w=dnw2[0], gdn_conv_w=dgcw[:4], gdn_a_log=dgpk[0, :4],
                     gdn_dt_bias=dgpk[1, :4], gdn_norm_w=dgpk[2], hgrn_norm_w=dhnw[0], ssd_conv_w=dscw[:4],
                     ssd_conv_b=dspv[0], ssd_norm_w=dspv[1, :512], ssd_a_log=dsps[0, :8], ssd_dt_bias=dsps[1, :8],
                     ssd_d=dsps[2, :8], ffn_conv_w=dcw[:3], ffn_conv_b=dcw[3])
        to_d2d = (l, [stacked_w_in])
    lp, stacked = to_d2d
    recv = exchange("rs_d2d_i%d" % lp, *_rs_d2d(stacked)[:4])
    qf, side = _rs_ici(stacked, recv, "i%d" % lp)
    finish([(lp, "w_in")], qf, exchange("rs_ici_i%d" % lp, *side[:4]), "i%d" % lp)
    grad_x = dx.reshape(nb, s, d)

    dmod = jnp.stack(dmod)
    (b_sum,) = elementwise("bias_rows", lambda *r: (functools.reduce(lambda p, q: p + q, r),),
                           [dmod[:, b].reshape(depth * 6, d) for b in range(nb)], [F32])
    per_layer = ("norm1_w", "norm2_w", "gdn_conv_w", "gdn_a_log", "gdn_dt_bias", "gdn_norm_w", "hgrn_norm_w",
                 "ssd_conv_w", "ssd_conv_b", "ssd_norm_w", "ssd_a_log", "ssd_dt_bias", "ssd_d", "ffn_conv_w", "ffn_conv_b")
    vals = {k: jnp.stack([sg[l][k] for l in range(depth)]) for k in per_layer}
    vals.update(loss=loss_part[0, :1], b_ada=b_sum.reshape(depth, 6 * d), hgrn_lb_param=d_lb, final_norm_w=d_final[0])
    gp = _Packer()
    for k, v in vals.items():
        gp.add(k, v.shape)
    packed8, dmod8 = allgather8("ag_grads", [gp.pack(vals), dmod.reshape(depth * nb, 6 * d)], [False, False])
    gs = gp.unpack(sum8("sum_small", packed8))
    loss = gs["loss"].reshape(())

    def my_cols(g):
        cs = g.shape[-1] // N_CHIPS
        return lax.dynamic_slice_in_dim(g, chip * cs, cs, axis=g.ndim - 1)

    for k in ("gdn_conv_w", "ssd_conv_w", "ffn_conv_w"):
        gs[k] = my_cols(gs[k])

    dmod_all = dmod8.reshape(N_DEV, depth, nb, 6 * d).transpose(1, 0, 2, 3).reshape(depth, N_DEV * nb, 6 * d)
    dmod_mine = lax.dynamic_slice_in_dim(dmod_all, chip * (6 * d // N_CHIPS), 6 * d // N_CHIPS, axis=2)
    g_w_ada = jnp.stack([matmul("ada_dw%d" % l, c_act, dmod_mine[l], "tn", tm=1024) for l in range(depth)])

    c_arr = lax.axis_index("c").astype(jnp.int32).reshape(1)
    grads, delta, new_m, new_v = {}, {}, {}, {}
    for i, (name, _) in enumerate(_BIG):
        shp = w[name].shape
        halves = lambda a: a.reshape((depth, 2) + reduced[(0, name)][0].shape)
        res = None
        for l in reversed(range(depth)):
            res = adamw_halves("adamw_%s%d" % (name, l), halves(w[name]), halves(mom[name]), halves(var[name]), l,
                               *reduced[(l, name)], c_arr, prev=res)
        grads[name], delta[name], new_m[name], new_v[name] = [r.reshape(shp) for r in res]
    grads["w_ada"] = g_w_ada
    for k in _SMALL:
        grads[k] = gs[k].reshape(w[k].shape)
    shp = w_ada.shape
    flat = lambda a: a.reshape(shp[0] * shp[1], shp[2])
    dl, nm, nv = adamw("adamw_w_ada", flat(w_ada), flat(g_w_ada), flat(m_w_ada), flat(v_w_ada))
    delta["w_ada"], new_m["w_ada"], new_v["w_ada"] = dl.reshape(shp), nm.reshape(shp), nv.reshape(shp)
    sp = _Packer()
    for k in _SMALL:
        sp.add(k, w[k].shape)
    dl, nm, nv = adamw("adamw_small", sp.pack(w), sp.pack(grads), sp.pack(mom), sp.pack(var))
    delta.update(sp.unpack(dl))
    new_m.update(sp.unpack(nm))
    new_v.update(sp.unpack(nv))

    return (loss, grad_x, *[grads[k] for k in _WEIGHTS], *[delta[k] for k in _WEIGHTS],
            *[new_m[k] for k in _WEIGHTS], *[new_v[k] for k in _WEIGHTS])
```

```python
import functools

import jax
import jax.numpy as jnp
from jax import lax
from jax.experimental import pallas as pl
from jax.experimental.pallas import tpu as pltpu

F32 = jnp.float32
BF16 = jnp.bfloat16
HI = lax.Precision.HIGHEST
MESH = pl.DeviceIdType.MESH

EPS = 1e-6
D_MODEL = 1024
GDN_HEADS, GDN_DK, GDN_CHUNK = 4, 128, 64
HGRN_HEADS, HGRN_DK, HGRN_CHUNK = 4, 128, 16
SSD_HEADS, SSD_P, SSD_GROUPS, SSD_STATE, SSD_CHUNK = 8, 64, 2, 128, 64
FFN_HIDDEN = 2816
N_CHIPS = 4
N_DEV = 8

ADAM_LR, ADAM_B1, ADAM_B2, ADAM_EPS, ADAM_WD, ADAM_STEP = 0.001, 0.9, 0.999, 1e-08, 0.01, 10

W_G, W_A, W_B, W_C = 3072, 2176, 2048, 1664
VMEM_LIMIT = 56 * 1024 * 1024


def _cparams(sem):
    return pltpu.CompilerParams(dimension_semantics=sem, vmem_limit_bytes=VMEM_LIMIT)


def _dg(a, b, ca, cb):
    return lax.dot_general(a.astype(BF16), b.astype(BF16), (((ca,), (cb,)), ((), ())),
                           preferred_element_type=F32)


@jax.custom_vjp
def bdot(a, b):
    return _dg(a, b, 1, 0)


bdot.defvjp(lambda a, b: (_dg(a, b, 1, 0), (a, b)),
            lambda r, g: (_dg(g, r[1], 1, 1), _dg(r[0], g, 0, 0)))


@jax.custom_vjp
def bdot_nt(a, b):
    return _dg(a, b, 1, 1)


bdot_nt.defvjp(lambda a, b: (_dg(a, b, 1, 1), (a, b)),
               lambda r, g: (_dg(g, r[1], 1, 0), _dg(g, r[0], 0, 0)))


@jax.custom_vjp
def bdot_tn(a, b):
    return _dg(a, b, 0, 0)


bdot_tn.defvjp(lambda a, b: (_dg(a, b, 0, 0), (a, b)),
               lambda r, g: (_dg(r[1], g, 1, 1), _dg(r[0], g, 1, 0)))


def _split(x, n):
    parts, rest = [], x
    for _ in range(n):
        p = rest.astype(BF16)
        parts.append(p)
        rest = rest - p.astype(F32)
    return parts


def _dgb(a, b, ca, cb):
    return lax.dot_general(a, b, (((ca,), (cb,)), ((), ())), preferred_element_type=F32)


def _dg3(a, b, ca, cb):
    (ah, al), (bh, bl) = _split(a, 2), _split(b, 2)
    return _dgb(jnp.concatenate([ah, ah, al], axis=ca), jnp.concatenate([bh, bl, bh], axis=cb), ca, cb)


@jax.custom_vjp
def hdot(a, b):
    return _dg3(a, b, 1, 0)


hdot.defvjp(lambda a, b: (_dg3(a, b, 1, 0), (a, b)),
            lambda r, g: (_dg3(g, r[1], 1, 1), _dg3(r[0], g, 0, 0)))


def _dge(e, x, ce, cx, e_first):
    eb = e.astype(BF16)
    es = jnp.concatenate([eb, eb, eb], axis=ce)
    xs = jnp.concatenate(_split(x, 3), axis=cx)
    return _dgb(es, xs, ce, cx) if e_first else _dgb(xs, es, cx, ce)


@jax.custom_vjp
def ldot(e, x):
    return _dge(e, x, 1, 0, True)


ldot.defvjp(lambda e, x: (_dge(e, x, 1, 0, True), e),
            lambda e, g: (jnp.zeros_like(e), _dge(e, g, 0, 0, True)))


@jax.custom_vjp
def rdot(x, e):
    return _dge(e, x, 0, 1, False)


rdot.defvjp(lambda x, e: (_dge(e, x, 0, 1, False), e),
            lambda e, g: (_dge(e, g, 1, 1, False), jnp.zeros_like(e)))


@jax.custom_vjp
def _sigmoid(x):
    return 1.0 / (1.0 + jnp.exp(-x))


def _sigmoid_fwd(x):
    g = 1.0 / (1.0 + jnp.exp(-x))
    return g, g


_sigmoid.defvjp(_sigmoid_fwd, lambda g, ct: (ct * g * (1.0 - g),))


def _silu(x):
    return x * _sigmoid(x)


def _softplus(x):
    return jnp.maximum(x, 0.0) + jnp.log(1.0 + jnp.exp(-jnp.abs(x)))


def _rms(x, w):
    return x * lax.rsqrt(jnp.mean(x * x, axis=-1, keepdims=True) + EPS) * w


def _iota(shape, dim):
    return lax.broadcasted_iota(jnp.int32, shape, dim)


def _tri_ones(n, chunk, kind):
    i, j = _iota((n, n), 0), _iota((n, n), 1)
    same = lax.div(i, chunk) == lax.div(j, chunk)
    if kind == "incl":
        m = same & (j <= i)
    elif kind == "strict":
        m = same & (j < i)
    elif kind == "all":
        m = same
    else:
        m = same & (lax.rem(j, chunk) < (chunk // 2))
    return m


def _causal_conv(w, halo, x, width):
    r = x.shape[0]
    xin = jnp.concatenate([halo, x], axis=0)
    y = w[width - 1:width, :] * x
    for k in range(width - 1):
        off = 8 - (width - 1) + k
        y = y + w[k:k + 1, :] * xin[off:off + r, :]
    return y


def _each(fn, *lists):
    return [fn(*a) for a in zip(*lists)]


def _neumann(ms):
    n = ms[0].shape[0]
    eye = (_iota((n, n), 0) == _iota((n, n), 1)).astype(F32)
    accs = [eye - m for m in ms]
    ps = ms
    steps = 1
    while steps * 2 < n:
        ps = _each(hdot, ps, ps)
        accs = [acc + ap for acc, ap in zip(accs, _each(hdot, accs, ps))]
        steps *= 2
    return accs


@jax.custom_vjp
def tri_inverse(ms):
    return _neumann(ms)


def _tri_inverse_fwd(ms):
    ainvs = _neumann(ms)
    return ainvs, ainvs


def _tri_inverse_bwd(ainvs, gs):
    t = _each(lambda g, a: _dg3(g, a, 1, 1), gs, ainvs)
    return ([-x for x in _each(lambda a, y: _dg3(a, y, 0, 0), ainvs, t)],)


tri_inverse.defvjp(_tri_inverse_fwd, _tri_inverse_bwd)


def gdn_tile(params, state, ins, halos):
    conv_w, pk = params
    (pa,), (ha,) = ins, halos
    r = pa.shape[0]
    c, nh, dk = GDN_CHUNK, GDN_HEADS, GDN_DK
    kw = nh * dk
    qkv = _silu(_causal_conv(conv_w, ha[:, :3 * kw], pa[:, :3 * kw], 4))
    z = pa[:, 3 * kw:4 * kw]
    gsm = pa[:, 4 * kw:]
    a_log, dtb, nw = pk[0:1, :], pk[1:2, :], pk[2:3, :]
    g_all = -jnp.exp(a_log) * _softplus(gsm + dtb)
    beta_all = _sigmoid(gsm)
    incl = _tri_ones(c, c, "incl")
    strict = _tri_ones(c, c, "strict")
    lmat = incl.astype(F32)
    scale = dk ** -0.5
    nck = r // c
    inst = [(ci, h) for ci in range(nck) for h in range(nh)]

    def l2n(v):
        return v * lax.rsqrt(jnp.sum(v * v, axis=-1, keepdims=True) + EPS)

    gcs = [ldot(lmat, g_all[ci * c:(ci + 1) * c, :]) for ci in range(nck)]
    gcts = [g.T for g in gcs]
    g_col = [gcs[ci][:, h:h + 1] for ci, h in inst]
    g_row = [gcts[ci][h:h + 1, :] for ci, h in inst]
    g_last = [gcs[ci][c - 1:c, h:h + 1] for ci, h in inst]
    beta = [beta_all[ci * c:(ci + 1) * c, nh + h:nh + h + 1] for ci, h in inst]
    qh = [l2n(qkv[ci * c:(ci + 1) * c, h * dk:(h + 1) * dk]) for ci, h in inst]
    kh = [l2n(qkv[ci * c:(ci + 1) * c, kw + h * dk:kw + (h + 1) * dk]) for ci, h in inst]
    vh = [qkv[ci * c:(ci + 1) * c, 2 * kw + h * dk:2 * kw + (h + 1) * dk] for ci, h in inst]
    decay = [jnp.where(incl, jnp.exp(jnp.where(incl, gc_ - gr_, 0.0)), 0.0) for gc_, gr_ in zip(g_col, g_row)]
    kb = [k * b for k, b in zip(kh, beta)]
    qs = [q * scale for q in qh]
    kk = _each(lambda a, b, k: bdot_nt(jnp.concatenate([a, b], axis=0), k), kb, qs, kh)
    ms = [jnp.where(strict, x[:c] * d, 0.0) for x, d in zip(kk, decay)]
    attn = [x[c:] * d for x, d in zip(kk, decay)]
    ainv = tri_inverse(ms)
    eg = [jnp.exp(g) for g in g_col]
    rhs = [jnp.concatenate([v * b, k_ * e], axis=1) for v, b, k_, e in zip(vh, beta, kb, eg)]
    sol = _each(hdot, ainv, rhs)
    qg = [q * e for q, e in zip(qs, eg)]
    k_end = [k * jnp.exp(gl - g) for k, gl, g in zip(kh, g_last, g_col)]
    e_last = [jnp.exp(gl) for gl in g_last]

    st = [state[h * dk:(h + 1) * dk, :] for h in range(nh)]
    outs = [[] for _ in range(nh)]
    for ci in range(nck):
        idx = [ci * nh + h for h in range(nh)]
        ws = [bdot(jnp.concatenate([sol[i][:, dk:], qg[i]], axis=0), st[h]) for h, i in enumerate(idx)]
        v_new = [sol[i][:, :dk] - w_[:c] for i, w_ in zip(idx, ws)]
        av = [bdot(attn[i], v) for i, v in zip(idx, v_new)]
        kv = [bdot_tn(k_end[i], v) for i, v in zip(idx, v_new)]
        for h, i in enumerate(idx):
            o = ws[h][c:] + av[h]
            st[h] = st[h] * e_last[i] + kv[h]
            outs[h].append(_rms(o, nw) * _silu(z[ci * c:(ci + 1) * c, h * dk:(h + 1) * dk]))
    out = jnp.concatenate([jnp.concatenate(o, axis=0) for o in outs], axis=1)
    return jnp.concatenate(st, axis=0), [out]


def make_hgrn_tile(layer, depth):
    def hgrn_tile(params, state, ins, halos):
        lbp, nwp = params
        (pb,) = ins
        r = pb.shape[0]
        c = HGRN_CHUNK
        kw = HGRN_HEADS * HGRN_DK
        rows = [lbp[i:i + 1, :] for i in range(depth)]
        mx = functools.reduce(jnp.maximum, rows)
        ex = [jnp.exp(x - mx) for x in rows]
        den = functools.reduce(lambda a, b: a + b, ex)
        soft = [e / den for e in ex]
        lb = functools.reduce(lambda a, b: a + b, soft[:layer + 1]) - soft[0]
        nw = nwp[0:1, :]
        q = _silu(pb[:, :kw])
        fr = pb[:, kw:2 * kw]
        logf = jnp.log(lb + (1.0 - lb) * _sigmoid(fr))
        k = (1.0 - lb) * _sigmoid(-fr)
        v = pb[:, 2 * kw:3 * kw]
        gate = pb[:, 3 * kw:]
        incl = _tri_ones(r, c, "incl")
        masks = jnp.concatenate([incl.astype(F32), _tri_ones(r, c, "upto").astype(F32),
                                 _tri_ones(r, c, "all").astype(F32)], axis=0)
        sums = ldot(masks, logf)
        g_cum, g_ref, g_end = sums[:r], sums[r:2 * r], sums[2 * r:]
        qs = q * jnp.exp(g_cum - g_ref)
        ks = k * jnp.exp(g_ref - g_cum)
        qg = q * jnp.exp(g_cum)
        k_end = k * jnp.exp(g_end - g_cum)
        e_end = jnp.exp(g_end)
        sls = [slice(h * HGRN_DK, (h + 1) * HGRN_DK) for h in range(HGRN_HEADS)]
        attn = [jnp.where(incl, bdot_nt(qs[:, sl], ks[:, sl]), 0.0) for sl in sls]
        o_intra = [bdot(a, v[:, sl]) for a, sl in zip(attn, sls)]
        nsub, dk = r // c, HGRN_DK
        own_block = lax.div(_iota((r, nsub * dk), 0), c) == lax.div(_iota((r, nsub * dk), 1), dk)

        def spread(a):
            return jnp.where(own_block, jnp.concatenate([a] * nsub, axis=1), 0.0)

        kv = [bdot_tn(v[:, sl], spread(k_end[:, sl])) for sl in sls]
        s_t = [state[sl, :] for sl in sls]
        entry = [[] for _ in sls]
        for j in range(nsub):
            for lst, s_h in zip(entry, s_t):
                lst.append(s_h)
            s_t = [s_h * e_end[j * c:j * c + 1, sl] + x[:, j * dk:(j + 1) * dk] for s_h, sl, x in zip(s_t, sls, kv)]
        o_inter = [bdot_nt(spread(qg[:, sl]), jnp.concatenate(e, axis=1)) for sl, e in zip(sls, entry)]
        outs = [_rms(oa + ob, nw) * _silu(gate[:, sl]) for oa, ob, sl in zip(o_intra, o_inter, sls)]
        return jnp.concatenate(s_t, axis=0), [jnp.concatenate(outs, axis=1)]
    return hgrn_tile


def ssd_tile(params, state, ins, halos):
    conv_w, pv, ps = params
    (pc,), (hc,) = ins, halos
    r = pc.shape[0]
    c = SSD_CHUNK
    inner = SSD_HEADS * SSD_P
    gw = inner // SSD_GROUPS
    z = pc[:, :inner]
    xbc = _silu(_causal_conv(conv_w, hc[:, inner:inner + 1024], pc[:, inner:inner + 1024], 4) + pv[0:1, :])
    ssm = pc[:, inner + 1024:]
    xs = xbc[:, :inner]
    bm = xbc[:, inner:inner + SSD_GROUPS * SSD_STATE]
    cm = xbc[:, inner + SSD_GROUPS * SSD_STATE:]
    a_log, dtb, dsk = ps[0:1, :], ps[1:2, :], ps[2:3, :]
    nw = pv[1:2, :inner]
    dt = _softplus(ssm + dtb)
    da = dt * (-jnp.exp(a_log))
    expand = (lax.div(_iota((128, inner), 1), SSD_P) == _iota((128, inner), 0)).astype(F32)
    xdt = xs * rdot(dt, expand)
    d_e = rdot(jnp.concatenate([dsk] * 8, axis=0), expand)[0:1, :]
    incl = _tri_ones(c, c, "incl")
    lmat = incl.astype(F32)
    st = [state[g * SSD_STATE:(g + 1) * SSD_STATE, :] for g in range(SSD_GROUPS)]
    hpg = SSD_HEADS // SSD_GROUPS
    nck = r // c
    groups = range(SSD_GROUPS)
    cg = [(ci, g) for ci in range(nck) for g in groups]
    rows = [slice(ci * c, (ci + 1) * c) for ci in range(nck)]
    gls = [slice(g * gw, (g + 1) * gw) for g in groups]
    acs = [ldot(lmat, da[rs, :]) for rs in rows]
    acs_t = [a.T for a in acs]
    acs_e = [rdot(a, expand) for a in acs]
    last_e = [a[c - 1:c, :] for a in acs_e]
    bm_g = [bm[rows[ci], g * SSD_STATE:(g + 1) * SSD_STATE] for ci, g in cg]
    cm_g = [cm[rows[ci], g * SSD_STATE:(g + 1) * SSD_STATE] for ci, g in cg]
    cb = _each(bdot_nt, cm_g, bm_g)
    heads = [(i, ci, g * hpg + hg) for i, (ci, g) in enumerate(cg) for hg in range(hpg)]
    seg = [jnp.where(incl, jnp.exp(jnp.where(incl, acs[ci][:, hh:hh + 1] - acs_t[ci][hh:hh + 1, :], 0.0)), 0.0)
           for _, ci, hh in heads]
    yd = [bdot(cb[i] * sg, xdt[rows[ci], hh * SSD_P:(hh + 1) * SSD_P]) for (i, ci, hh), sg in zip(heads, seg)]
    y_diag = [jnp.concatenate(yd[i * hpg:(i + 1) * hpg], axis=1) for i in range(len(cg))]
    xw = [xdt[rows[ci], gls[g]] * jnp.exp(last_e[ci][:, gls[g]] - acs_e[ci][:, gls[g]]) for ci, g in cg]
    e_acs = [jnp.exp(acs_e[ci][:, gls[g]]) for ci, g in cg]
    e_last = [jnp.exp(last_e[ci][:, gls[g]]) for ci, g in cg]
    kv = _each(bdot_tn, bm_g, xw)
    ys = []
    for ci in range(nck):
        idx = [ci * SSD_GROUPS + g for g in groups]
        y_off = [bdot(cm_g[i], st[g]) * e_acs[i] for g, i in zip(groups, idx)]
        st = [st[g] * e_last[i] + kv[i] for g, i in zip(groups, idx)]
        ys.append(jnp.concatenate([y_diag[i] + yo for i, yo in zip(idx, y_off)], axis=1))
    y = jnp.concatenate(ys, axis=0) + d_e * xs
    yz = y * _silu(z)
    out = jnp.concatenate([_rms(yz[:, g * gw:(g + 1) * gw], nw[:, g * gw:(g + 1) * gw])
                           for g in range(SSD_GROUPS)], axis=1)
    return jnp.concatenate(st, axis=0), [out]


def convglu_tile(params, state, ins, halos):
    (cw,) = params
    (u,), (hu,) = ins, halos
    y = _causal_conv(cw, hu, u, 3) + cw[3:4, :]
    return None, [_silu(y[:, :FFN_HIDDEN]) * y[:, FFN_HIDDEN:]]


def convglu_tile_t(params, state, ins, halos):
    _, (act,) = convglu_tile(params, state, ins, halos)
    return None, [act, act.T]


def _halo_map(nt, r):
    return lambda b, n: (jnp.maximum((b * nt + n) * (r // 8) - 1, 0), 0)


def _exchange_copies(plan, local_plan, in_refs, out_refs, send_sems, recv_sems, local_sems):
    x, y, c = lax.axis_index("x"), lax.axis_index("y"), lax.axis_index("c")
    copies = []
    for k, fn in enumerate(plan):
        src, dst, peer = fn(in_refs, out_refs, x, y, c)
        copies.append(pltpu.make_async_remote_copy(src_ref=src, dst_ref=dst, send_sem=send_sems.at[k],
                                                   recv_sem=recv_sems.at[k], device_id=peer, device_id_type=MESH))
    for k, fn in enumerate(local_plan):
        src, dst = fn(in_refs, out_refs, x, y, c)
        copies.append(pltpu.make_async_copy(src, dst, local_sems.at[k]))
    return copies


def _exchange_sems(plan, local_plan):
    return [pltpu.SemaphoreType.DMA((max(len(plan), 1),)), pltpu.SemaphoreType.DMA((max(len(plan), 1),)),
            pltpu.SemaphoreType.DMA((max(len(local_plan), 1),))]


def _host_exchange(side, body, in_specs, o_specs, out_shape, scratch, args, grid):
    s_ins, s_shapes, plan, local_plan, then = side
    n_in, n_out, n_scr = len(in_specs), len(o_specs), len(scratch)
    k_in, k_out = len(s_ins), len(s_shapes)
    any_spec = pl.BlockSpec(memory_space=pl.ANY)

    def hosted(*refs):
        own_in, s_in = refs[:n_in], refs[n_in:n_in + k_in]
        o0 = n_in + k_in
        own_out, s_out = refs[o0:o0 + n_out], refs[o0 + n_out:o0 + n_out + k_out]
        rest = refs[o0 + n_out + k_out:]
        own_scr, sems, sems_then = rest[:n_scr], rest[n_scr:n_scr + 3], rest[n_scr + 3:]
        ids = [pl.program_id(a) for a in range(len(grid))]
        first = functools.reduce(lambda p, q: p & q, [i == 0 for i in ids])
        last = functools.reduce(lambda p, q: p & q, [i == g - 1 for i, g in zip(ids, grid)])

        @pl.when(first)
        def _():
            for cp in _exchange_copies(plan, local_plan, s_in, s_out, *sems):
                cp.start()

        body(*own_in, *own_out, *own_scr)

        @pl.when(last)
        def _():
            for cp in _exchange_copies(plan, local_plan, s_in, s_out, *sems):
                cp.wait()
            passed = _exchange_copies(then, (), s_in, s_out, *sems_then)
            for cp in passed:
                cp.start()
            for cp in passed:
                cp.wait()

    return (hosted, list(in_specs) + [any_spec] * k_in, list(o_specs) + [any_spec] * k_out,
            list(out_shape) + list(s_shapes),
            list(scratch) + _exchange_sems(plan, local_plan) + _exchange_sems(then, ()),
            list(args) + list(s_ins))


def seq_fwd(name, tile_fn, params, ins, use_halo, out_specs, state_shape, nb, s, r, side=None):
    nt = s // r
    n_p, n_i, n_o = len(params), len(ins), len(out_specs)
    has_state = state_shape is not None

    def body(*refs):
        p_refs, i_refs = refs[:n_p], refs[n_p:n_p + n_i]
        h_refs = refs[n_p + n_i:n_p + 2 * n_i] if use_halo else ()
        k = n_p + n_i + len(h_refs)
        o_refs = refs[k:k + n_o]
        n = pl.program_id(1)
        state = None
        if has_state:
            sv_ref, st_ref = refs[k + n_o], refs[k + n_o + 1]

            @pl.when(n == 0)
            def _():
                st_ref[...] = jnp.zeros(state_shape, F32)

            state = st_ref[...]
            sv_ref[0, 0] = state
        pv = [p[...] for p in p_refs]
        iv = [i[...].astype(F32) for i in i_refs]
        hv = [jnp.where(n > 0, h[...].astype(F32), 0.0) for h in h_refs]
        new_state, ov = tile_fn(pv, state, iv, hv)
        for o_ref, o in zip(o_refs, ov):
            o_ref[...] = o.astype(o_ref.dtype)
        if has_state:
            st_ref[...] = new_state

    row = lambda b, n: (b * nt + n, 0)
    in_specs = [pl.BlockSpec(p.shape, lambda b, n: (0, 0)) for p in params]
    in_specs += [pl.BlockSpec((r, a.shape[1]), row) for a in ins]
    if use_halo:
        in_specs += [pl.BlockSpec((8, a.shape[1]), _halo_map(nt, r)) for a in ins]
    col = lambda b, n: (0, b * nt + n)
    out_shape, o_specs = [], []
    for w, dt, *transposed in out_specs:
        out_shape.append(jax.ShapeDtypeStruct((w, nb * s) if transposed else (nb * s, w), dt))
        o_specs.append(pl.BlockSpec((w, r), col) if transposed else pl.BlockSpec((r, w), row))
    scratch = []
    if has_state:
        out_shape.append(jax.ShapeDtypeStruct((nb, nt) + tuple(state_shape), F32))
        o_specs.append(pl.BlockSpec((1, 1) + tuple(state_shape), lambda b, n: (b, n, 0, 0)))
        scratch.append(pltpu.VMEM(tuple(state_shape), F32))
    args = list(params) + list(ins) + (list(ins) if use_halo else [])
    if side is not None:
        body, in_specs, o_specs, out_shape, scratch, args = _host_exchange(
            side, body, in_specs, o_specs, out_shape, scratch, args, (nb, nt))
    return pl.pallas_call(body, grid=(nb, nt), in_specs=in_specs, out_specs=o_specs, out_shape=out_shape,
                          scratch_shapes=scratch, compiler_params=_cparams(("arbitrary", "arbitrary")),
                          name=name)(*args)


def seq_bwd(name, tile_fn, params, ins, use_halo, states, douts, din_dtypes, state_shape, nb, s, r, side=None):
    nt = s // r
    n_p, n_i, n_o = len(params), len(ins), len(douts)
    has_state = state_shape is not None

    def body(*refs):
        p_refs, i_refs = refs[:n_p], refs[n_p:n_p + n_i]
        h_refs = refs[n_p + n_i:n_p + 2 * n_i] if use_halo else ()
        k = n_p + n_i + len(h_refs)
        sv_ref = None
        if has_state:
            sv_ref = refs[k]
            k += 1
        do_refs = refs[k:k + n_o]
        k += n_o
        di_refs, dp_refs = refs[k:k + n_i], refs[k + n_i:k + n_i + n_p]
        k += n_i + n_p
        dst_ref = None
        if has_state:
            dst_ref = refs[k]
            k += 1
        dh_refs = refs[k:k + len(h_refs)]
        b, nn = pl.program_id(0), pl.program_id(1)
        n = nt - 1 - nn

        @pl.when((b == 0) & (nn == 0))
        def _():
            for dp in dp_refs:
                dp[...] = jnp.zeros(dp.shape, F32)

        @pl.when(nn == 0)
        def _():
            if has_state:
                dst_ref[...] = jnp.zeros(state_shape, F32)
            for dh in dh_refs:
                dh[...] = jnp.zeros(dh.shape, F32)

        pv = [p[...] for p in p_refs]
        iv = [i[...].astype(F32) for i in i_refs]
        hv = [jnp.where(n > 0, h[...].astype(F32), 0.0) for h in h_refs]
        if has_state:
            f = lambda pv_, st_, iv_, hv_: tile_fn(pv_, st_, iv_, hv_)
            _, vjp = jax.vjp(f, pv, sv_ref[0, 0], iv, hv)
            dpv, dst, div, dhv = vjp((dst_ref[...], [d[...].astype(F32) for d in do_refs]))
            dst_ref[...] = dst
        else:
            f = lambda pv_, iv_, hv_: tile_fn(pv_, None, iv_, hv_)[1]
            _, vjp = jax.vjp(f, pv, iv, hv)
            dpv, div, dhv = vjp([d[...].astype(F32) for d in do_refs])
        for j, (di_ref, d) in enumerate(zip(di_refs, div)):
            if use_halo:
                d = jnp.concatenate([d[:r - 8], d[r - 8:] + dh_refs[j][...]], axis=0)
            di_ref[...] = d.astype(di_ref.dtype)
        for dh_ref, d in zip(dh_refs, dhv):
            dh_ref[...] = d
        for dp_ref, d in zip(dp_refs, dpv):
            dp_ref[...] += d

    row = lambda b, nn: (b * nt + nt - 1 - nn, 0)
    hmap = _halo_map(nt, r)
    in_specs = [pl.BlockSpec(p.shape, lambda b, nn: (0, 0)) for p in params]
    in_specs += [pl.BlockSpec((r, a.shape[1]), row) for a in ins]
    if use_halo:
        in_specs += [pl.BlockSpec((8, a.shape[1]), lambda b, nn: hmap(b, nt - 1 - nn)) for a in ins]
    args = list(params) + list(ins) + (list(ins) if use_halo else [])
    scratch = []
    if has_state:
        in_specs.append(pl.BlockSpec((1, 1) + tuple(state_shape), lambda b, nn: (b, nt - 1 - nn, 0, 0)))
        args.append(states)
        scratch.append(pltpu.VMEM(tuple(state_shape), F32))
    in_specs += [pl.BlockSpec((r, d.shape[1]), row) for d in douts]
    args += list(douts)
    if use_halo:
        scratch += [pltpu.VMEM((8, a.shape[1]), F32) for a in ins]
    out_shape = [jax.ShapeDtypeStruct(a.shape, dt) for a, dt in zip(ins, din_dtypes)]
    out_shape += [jax.ShapeDtypeStruct(p.shape, F32) for p in params]
    o_specs = [pl.BlockSpec((r, a.shape[1]), row) for a in ins]
    o_specs += [pl.BlockSpec(p.shape, lambda b, nn: (0, 0)) for p in params]
    if side is not None:
        body, in_specs, o_specs, out_shape, scratch, args = _host_exchange(
            side, body, in_specs, o_specs, out_shape, scratch, args, (nb, nt))
    res = pl.pallas_call(body, grid=(nb, nt), in_specs=in_specs, out_specs=o_specs, out_shape=out_shape,
                         scratch_shapes=scratch, compiler_params=_cparams(("arbitrary", "arbitrary")),
                         name=name)(*args)
    if side is not None:
        return res[:n_i], res[n_i:n_i + n_p], res[n_i + n_p:]
    return res[:n_i], res[n_i:]


def matmul(name, a, b, mode, out_dtype=F32, addend=None, resid=None, tm=512, tn=None, tk=None):
    if mode == "nn":
        (m, kd), (_, n) = a.shape, b.shape
    elif mode == "nt":
        (m, kd), (n, _) = a.shape, b.shape
    else:
        (kd, m), (_, n) = a.shape, b.shape
    tm, tn, tk = min(tm, m if resid is None else resid[2]), min(tn or n, n), min(tk or kd, kd)
    nk = kd // tk
    assert m % tm == 0 and n % tn == 0 and kd % tk == 0
    dims = {"nn": ((1,), (0,)), "nt": ((1,), (1,)), "tn": ((0,), (0,))}[mode]
    extra = [] if addend is None else [addend]
    if resid is not None:
        extra = [resid[0], resid[1]]
    n_in, n_out = 2 + len(extra), 1 if resid is None else 2

    def body(*refs):
        a_ref, b_ref = refs[0], refs[1]
        part = lax.dot_general(a_ref[...].astype(BF16), b_ref[...].astype(BF16), (dims, ((), ())),
                               preferred_element_type=F32)

        def finish(acc):
            if resid is not None:
                refs[n_in][...] = refs[2][...] + refs[3][0] * acc
                refs[n_in + 1][...] = acc.astype(BF16)
            else:
                if addend is not None:
                    acc = acc + refs[2][...]
                refs[n_in][...] = acc.astype(refs[n_in].dtype)

        if nk == 1:
            finish(part)
        else:
            acc_ref = refs[n_in + n_out]
            k = pl.program_id(2)

            @pl.when(k == 0)
            def _():
                acc_ref[...] = part

            @pl.when(k > 0)
            def _():
                acc_ref[...] += part

            @pl.when(k == nk - 1)
            def _():
                finish(acc_ref[...])

    if mode == "tn":
        a_spec = pl.BlockSpec((tk, tm), lambda j, i, k: (k, i))
    else:
        a_spec = pl.BlockSpec((tm, tk), lambda j, i, k: (i, k))
    if mode == "nt":
        b_spec = pl.BlockSpec((tn, tk), lambda j, i, k: (j, k))
    else:
        b_spec = pl.BlockSpec((tk, tn), lambda j, i, k: (k, j))
    o_spec = pl.BlockSpec((tm, tn), lambda j, i, k: (i, j))
    in_specs = [a_spec, b_spec] + [o_spec] * (len(extra) > 0)
    out_specs, out_shape = o_spec, jax.ShapeDtypeStruct((m, n), out_dtype)
    if resid is not None:
        rows = resid[2]
        assert rows % tm == 0
        in_specs.append(pl.BlockSpec((1, 1, tn), lambda j, i, k: (lax.div(i * tm, rows), 0, j)))
        out_specs, out_shape = [o_spec, o_spec], [out_shape, jax.ShapeDtypeStruct((m, n), BF16)]
    scratch = [pltpu.VMEM((tm, tn), F32)] if nk > 1 else []
    return pl.pallas_call(body, grid=(n // tn, m // tm, nk), in_specs=in_specs, out_specs=out_specs,
                          out_shape=out_shape, scratch_shapes=scratch,
                          compiler_params=_cparams(("parallel", "parallel", "arbitrary")), name=name)(a, b, *extra)


def _normmod(x, nw, shift, scale):
    return _rms(x, nw) * (1.0 + scale) + shift


def _row_specs(nb, s, tr, d):
    nt = s // tr
    row = pl.BlockSpec((tr, d), lambda b, i: (b * nt + i, 0))
    per_seq = pl.BlockSpec((1, 1, d), lambda b, i: (b, 0, 0))
    full = pl.BlockSpec((1, d), lambda b, i: (0, 0))
    return nt, row, per_seq, full


def normmod_fwd(name, x, nw, shift, scale, nb, s, tr=512):
    d, tr = x.shape[1], min(tr, s)
    nt, row, per_seq, full = _row_specs(nb, s, tr, d)

    def body(x_ref, nw_ref, sh_ref, sc_ref, h_ref, ht_ref):
        h = _normmod(x_ref[...], nw_ref[...], sh_ref[0], sc_ref[0])
        h_ref[...] = h.astype(h_ref.dtype)
        ht_ref[...] = h.T.astype(ht_ref.dtype)

    return pl.pallas_call(body, grid=(nb, nt), in_specs=[row, full, per_seq, per_seq],
                          out_specs=[row, pl.BlockSpec((d, tr), lambda b, i: (0, b * nt + i))],
                          out_shape=[jax.ShapeDtypeStruct(x.shape, BF16), jax.ShapeDtypeStruct(x.shape[::-1], BF16)],
                          compiler_params=_cparams(("parallel", "parallel")), name=name)(x, nw, shift, scale)


def normmod_bwd(name, x, nw, shift, scale, dh, dres, nb, s, tr=512):
    d, tr = x.shape[1], min(tr, s)
    nt, row, per_seq, full = _row_specs(nb, s, tr, d)

    def body(x_ref, nw_ref, sh_ref, sc_ref, dh_ref, dres_ref, dx_ref, dnw_ref, dsh_ref, dsc_ref):
        b, i = pl.program_id(0), pl.program_id(1)

        @pl.when((b == 0) & (i == 0))
        def _():
            dnw_ref[...] = jnp.zeros(dnw_ref.shape, F32)

        @pl.when(i == 0)
        def _():
            dsh_ref[...] = jnp.zeros(dsh_ref.shape, F32)
            dsc_ref[...] = jnp.zeros(dsc_ref.shape, F32)

        _, vjp = jax.vjp(_normmod, x_ref[...], nw_ref[...], sh_ref[0], sc_ref[0])
        dx, dnw, dsh, dsc = vjp(dh_ref[...])
        dx_ref[...] = dres_ref[...] + dx
        dnw_ref[...] += dnw
        dsh_ref[0] += dsh
        dsc_ref[0] += dsc

    out_shape = [jax.ShapeDtypeStruct(x.shape, F32), jax.ShapeDtypeStruct((1, d), F32),
                 jax.ShapeDtypeStruct((nb, 1, d), F32), jax.ShapeDtypeStruct((nb, 1, d), F32)]
    return pl.pallas_call(body, grid=(nb, nt), in_specs=[row, full, per_seq, per_seq, row, row],
                          out_specs=[row, full, per_seq, per_seq], out_shape=out_shape,
                          compiler_params=_cparams(("arbitrary", "arbitrary")),
                          name=name)(x, nw, shift, scale, dh, dres)


def _merge_specs(nb, s, tr, d, wbr):
    nt, row, per_seq, _ = _row_specs(nb, s, tr, d)
    o_spec = pl.BlockSpec((tr, wbr), lambda b, i: (b * nt + i, 0))
    g_spec = pl.BlockSpec((tr, 3 * d), lambda b, i: (b * nt + i, 0))
    wbr_spec = pl.BlockSpec((wbr, d), lambda b, i: (0, 0))
    wo_spec = pl.BlockSpec((d, d), lambda b, i: (0, 0))
    return nt, row, per_seq, o_spec, g_spec, wbr_spec, wo_spec


def merge_fwd(name, x, oa, ob, oc, pg, gate1, wa, wb, wc, wo, nb, s, tr=512):
    d, tr = x.shape[1], min(tr, s)
    nt, row, per_seq, o_spec, g_spec, wbr_spec, wo_spec = _merge_specs(nb, s, tr, d, oa.shape[1])

    def body(x_ref, oa_ref, ob_ref, oc_ref, pg_ref, g1_ref, wa_ref, wb_ref, wc_ref, wo_ref,
             x1_ref, ya_ref, yb_ref, yc_ref, mg_ref, mx_ref):
        g = _sigmoid(pg_ref[...])
        ys = [_dg(o[...], w_[...], 1, 0) for o, w_ in ((oa_ref, wa_ref), (ob_ref, wb_ref), (oc_ref, wc_ref))]
        merged = g[:, :d] * ys[0] + g[:, d:2 * d] * ys[1] + g[:, 2 * d:] * ys[2]
        mix = _dg(merged, wo_ref[...], 1, 0)
        x1_ref[...] = x_ref[...] + g1_ref[0] * mix
        for r, v in zip((ya_ref, yb_ref, yc_ref, mg_ref, mx_ref), ys + [merged, mix]):
            r[...] = v.astype(r.dtype)

    return pl.pallas_call(body, grid=(nb, nt),
                          in_specs=[row, o_spec, o_spec, o_spec, g_spec, per_seq, wbr_spec, wbr_spec, wbr_spec, wo_spec],
                          out_specs=[row] * 6,
                          out_shape=[jax.ShapeDtypeStruct(x.shape, F32)] + [jax.ShapeDtypeStruct(x.shape, BF16)] * 5,
                          compiler_params=_cparams(("parallel", "parallel")),
                          name=name)(x, oa, ob, oc, pg, gate1, wa, wb, wc, wo)


def merge_bwd(name, oa, ob, oc, pg, gate1, wa, wb, wc, wo, ys, merged, mix, dx1, nb, s, tr=256):
    d, tr = dx1.shape[1], min(tr, s)
    wbr = oa.shape[1]
    nt, row, per_seq, o_spec, g_spec, wbr_spec, wo_spec = _merge_specs(nb, s, tr, d, wbr)

    def body(oa_ref, ob_ref, oc_ref, pg_ref, g1_ref, wa_ref, wb_ref, wc_ref, wo_ref, ya_ref, yb_ref, yc_ref, mg_ref,
             mx_ref, dx_ref, doa_ref, dob_ref, doc_ref, dpg_ref, dg1_ref, dwa_ref, dwb_ref, dwc_ref, dwo_ref):
        b, i = pl.program_id(0), pl.program_id(1)

        @pl.when((b == 0) & (i == 0))
        def _():
            for r in (dwa_ref, dwb_ref, dwc_ref, dwo_ref):
                r[...] = jnp.zeros(r.shape, F32)

        @pl.when(i == 0)
        def _():
            dg1_ref[...] = jnp.zeros(dg1_ref.shape, F32)

        dx = dx_ref[...]
        dg1_ref[0] += jnp.sum(dx * mx_ref[...].astype(F32), axis=0, keepdims=True)
        dmix = g1_ref[0] * dx
        dmerged = _dg(dmix, wo_ref[...], 1, 1)
        dwo_ref[...] += _dg(mg_ref[...], dmix, 0, 0)
        g = _sigmoid(pg_ref[...])
        branches = ((oa_ref, wa_ref, ya_ref, doa_ref, dwa_ref), (ob_ref, wb_ref, yb_ref, dob_ref, dwb_ref),
                    (oc_ref, wc_ref, yc_ref, doc_ref, dwc_ref))
        dgs = []
        for k, (o_ref, w_ref, y_ref, do_ref, dw_ref) in enumerate(branches):
            gk = g[:, k * d:(k + 1) * d]
            dy = dmerged * gk
            dgs.append(dmerged * y_ref[...].astype(F32) * gk * (1.0 - gk))
            do_ref[...] = _dg(dy, w_ref[...], 1, 1)
            dw_ref[...] += _dg(o_ref[...], dy, 0, 0)
        dpg_ref[...] = jnp.concatenate(dgs, axis=1)

    t = nb * s
    out_shape = ([jax.ShapeDtypeStruct((t, wbr), F32)] * 3
                 + [jax.ShapeDtypeStruct((t, 3 * d), F32), jax.ShapeDtypeStruct((nb, 1, d), F32)]
                 + [jax.ShapeDtypeStruct((wbr, d), F32)] * 3 + [jax.ShapeDtypeStruct((d, d), F32)])
    return pl.pallas_call(body, grid=(nb, nt),
                          in_specs=[o_spec, o_spec, o_spec, g_spec, per_seq, wbr_spec, wbr_spec, wbr_spec, wo_spec]
                          + [row] * 6,
                          out_specs=[o_spec, o_spec, o_spec, g_spec, per_seq, wbr_spec, wbr_spec, wbr_spec, wo_spec],
                          out_shape=out_shape, compiler_params=_cparams(("arbitrary", "arbitrary")),
                          name=name)(oa, ob, oc, pg, gate1, wa, wb, wc, wo, *ys, merged, mix, dx1)


def resid_bwd(name, dx, f, gate, nb, s, tr=512):
    d, tr = dx.shape[1], min(tr, s)
    nt, row, per_seq, _ = _row_specs(nb, s, tr, d)

    def body(dx_ref, f_ref, g_ref, df_ref, dg_ref):
        @pl.when(pl.program_id(1) == 0)
        def _():
            dg_ref[...] = jnp.zeros(dg_ref.shape, F32)

        df_ref[...] = (g_ref[0] * dx_ref[...]).astype(df_ref.dtype)
        dg_ref[0] += jnp.sum(dx_ref[...] * f_ref[...], axis=0, keepdims=True)

    return pl.pallas_call(body, grid=(nb, nt), in_specs=[row, row, per_seq], out_specs=[row, per_seq],
                          out_shape=[jax.ShapeDtypeStruct(dx.shape, BF16), jax.ShapeDtypeStruct((nb, 1, d), F32)],
                          compiler_params=_cparams(("arbitrary", "arbitrary")), name=name)(dx, f, gate)


def loss_head(name, x, fw, target, tr=512):
    t, d = x.shape
    row = pl.BlockSpec((tr, d), lambda i: (i, 0))
    full = pl.BlockSpec((1, d), lambda i: (0, 0))

    def loss_fn(xv, fwv, tv):
        err = _rms(xv, fwv) - tv
        return 0.5 * jnp.sum(jnp.mean(err * err, axis=-1))

    def body(x_ref, fw_ref, t_ref, dx_ref, l_ref, dfw_ref):
        @pl.when(pl.program_id(0) == 0)
        def _():
            l_ref[...] = jnp.zeros(l_ref.shape, F32)
            dfw_ref[...] = jnp.zeros(dfw_ref.shape, F32)

        val, (dx, dfw) = jax.value_and_grad(loss_fn, argnums=(0, 1))(x_ref[...], fw_ref[...], t_ref[...])
        dx_ref[...] = dx
        l_ref[...] += val
        dfw_ref[...] += dfw

    return pl.pallas_call(body, grid=(t // tr,), in_specs=[row, full, row],
                          out_specs=[row, pl.BlockSpec((1, 128), lambda i: (0, 0)), full],
                          out_shape=[jax.ShapeDtypeStruct((t, d), F32), jax.ShapeDtypeStruct((1, 128), F32),
                                     jax.ShapeDtypeStruct((1, d), F32)],
                          compiler_params=_cparams(("arbitrary",)), name=name)(x, fw, target)


def _row_tile(rows, cols, n_arrays):
    budget = 24 * 1024 * 1024 // (8 * cols * max(n_arrays, 1))
    tr = rows
    while tr > max(budget, 16) and tr % 2 == 0 and (tr // 2) % 16 == 0:
        tr //= 2
    return tr


def elementwise(name, fn, ins, out_dtypes):
    rows, cols = ins[0].shape
    tr = _row_tile(rows, cols, len(ins) + len(out_dtypes))
    spec = pl.BlockSpec((tr, cols), lambda i: (i, 0))
    n_in = len(ins)

    def body(*refs):
        outs = fn(*[r[...] for r in refs[:n_in]])
        for o_ref, o in zip(refs[n_in:], outs):
            o_ref[...] = o.astype(o_ref.dtype)

    return pl.pallas_call(body, grid=(rows // tr,), in_specs=[spec] * n_in, out_specs=[spec] * len(out_dtypes),
                          out_shape=[jax.ShapeDtypeStruct((rows, cols), dt) for dt in out_dtypes],
                          compiler_params=_cparams(("parallel",)), name=name)(*ins)


def _adamw(w, g, m, v):
    m = ADAM_B1 * m + (1.0 - ADAM_B1) * g
    v = ADAM_B2 * v + (1.0 - ADAM_B2) * (g * g)
    m_hat = m / (1.0 - ADAM_B1 ** ADAM_STEP)
    v_hat = v / (1.0 - ADAM_B2 ** ADAM_STEP)
    delta = -ADAM_LR * (m_hat / (jnp.sqrt(v_hat) + ADAM_EPS) + ADAM_WD * w)
    return delta, m, v


def adamw(name, w, g, m, v):
    return elementwise(name, _adamw, [w, g, m, v], [F32, F32, F32])


_ANY = pl.BlockSpec(memory_space=pl.ANY)


def _coords():
    return lax.axis_index("x"), lax.axis_index("y"), lax.axis_index("c")


def allgather8(name, arrays, halves):
    n = len(arrays)

    def body(*refs):
        in_refs, out_refs = refs[:n], refs[n:2 * n]
        send_sems, recv_sems, local_sems = refs[2 * n:]
        x, y, c = _coords()
        me, sibling = (x, y, c), (x, y, 1 - c)
        chips = [(1 - x, y), (x, 1 - y), (1 - x, 1 - y)]

        def blk(i, px, py, pc):
            return out_refs[i].at[4 * px + 2 * py + pc]

        def piece(i):
            return in_refs[i].at[c] if halves[i] else in_refs[i]

        def copy(i, k, block, to, src=None):
            return pltpu.make_async_remote_copy(
                src_ref=blk(i, *block) if src is None else src, dst_ref=blk(i, *block),
                send_sem=send_sems.at[7 * i + k], recv_sem=recv_sems.at[7 * i + k],
                device_id=to, device_id_type=MESH)

        mine = [pltpu.make_async_copy(piece(i), blk(i, *me), local_sems.at[i]) for i in range(n)]
        for cp in mine:
            cp.start()
        first = []
        for i in range(n):
            first.append(copy(i, 0, me, sibling, src=piece(i)))
            first += [copy(i, 1 + j, me, (*chip, c), src=piece(i)) for j, chip in enumerate(chips)]
        for cp in first:
            cp.start()
        passed = []
        for j, chip in enumerate(chips):
            for i in range(n):
                copy(i, 1 + j, (*chip, c), me).wait_recv()
                fwd = copy(i, 4 + j, (*chip, c), sibling)
                fwd.start()
                passed.append(fwd)
        for i in range(n):
            copy(i, 0, sibling, me).wait_recv()
            for j, chip in enumerate(chips):
                copy(i, 4 + j, (*chip, 1 - c), me).wait_recv()
        for cp in first + passed:
            cp.wait_send()
        for cp in mine:
            cp.wait()

    out_shape = []
    for a, hv in zip(arrays, halves):
        out_shape.append(jax.ShapeDtypeStruct((N_DEV,) + tuple(a.shape[1:] if hv else a.shape), a.dtype))
    return pl.pallas_call(
        body, in_specs=[_ANY] * n, out_specs=[_ANY] * n, out_shape=out_shape,
        scratch_shapes=[pltpu.SemaphoreType.DMA((7 * n,)), pltpu.SemaphoreType.DMA((7 * n,)),
                        pltpu.SemaphoreType.DMA((n,))],
        name=name)(*arrays)


def exchange(name, ins, out_shapes, plan, local_plan=()):
    n_in, n_out = len(ins), len(out_shapes)

    def body(*refs):
        copies = _exchange_copies(plan, local_plan, refs[:n_in], refs[n_in:n_in + n_out], *refs[n_in + n_out:])
        for cp in copies:
            cp.start()
        for cp in copies:
            cp.wait()

    return pl.pallas_call(
        body, in_specs=[_ANY] * n_in, out_specs=[_ANY] * n_out, out_shape=out_shapes,
        scratch_shapes=_exchange_sems(plan, local_plan), name=name)(*ins)


def sum_halves(name, gs, recv, c_arr, chip_arr):
    _, _, hr, cs = gs.shape
    tr = _row_tile(hr, cs, 4)

    def body(c_ref, chip_ref, g_ref, r_ref, qf_ref, qb_ref):
        q = g_ref[0, 0] + r_ref[0, 0]
        qb_ref[0] = q.astype(BF16)

        @pl.when(pl.program_id(1) == chip_ref[0])
        def _():
            qf_ref[...] = q

    grid_spec = pltpu.PrefetchScalarGridSpec(
        num_scalar_prefetch=2, grid=(hr // tr, N_CHIPS),
        in_specs=[pl.BlockSpec((1, 1, tr, cs), lambda i, j, c_ref, chip_ref: (j, c_ref[0], i, 0)),
                  pl.BlockSpec((1, 1, tr, cs), lambda i, j, c_ref, chip_ref: (j, 0, i, 0))],
        out_specs=[pl.BlockSpec((tr, cs), lambda i, j, c_ref, chip_ref: (i, 0)),
                   pl.BlockSpec((1, tr, cs), lambda i, j, c_ref, chip_ref: (j, i, 0))])
    return pl.pallas_call(body, grid_spec=grid_spec,
                          out_shape=[jax.ShapeDtypeStruct((hr, cs), F32),
                                     jax.ShapeDtypeStruct((N_CHIPS, hr, cs), BF16)],
                          compiler_params=_cparams(("parallel", "arbitrary")), name=name)(c_arr, chip_arr, gs, recv)


def sum_chips(name, qf, recv):
    (total,) = elementwise(name, lambda q, a, b, c: (q + a.astype(F32) + b.astype(F32) + c.astype(F32),),
                           [qf] + list(recv), [F32])
    return total


def adamw_halves(name, w, m, v, layer, g_mine, g_other, c_arr, prev=None):
    _, _, hr, cs = w.shape
    tr = _row_tile(hr, cs, 9)

    def body(c_ref, w_ref, m_ref, v_ref, gm_ref, go_ref, *rest):
        g_ref, d_ref, nm_ref, nv_ref = rest[-4:]
        g = jnp.where(pl.program_id(0) == c_ref[0], gm_ref[...], go_ref[...])
        delta, nm, nv = _adamw(w_ref[0, 0], g, m_ref[0, 0], v_ref[0, 0])
        g_ref[0, 0], d_ref[0, 0], nm_ref[0, 0], nv_ref[0, 0] = g, delta, nm, nv

    half = pl.BlockSpec((1, 1, tr, cs), lambda h, i, c_ref: (layer, h, i, 0))
    row = pl.BlockSpec((tr, cs), lambda h, i, c_ref: (i, 0))
    in_specs, args, aliases = [half, half, half, row, row], [c_arr, w, m, v, g_mine, g_other], {}
    if prev is not None:
        in_specs += [pl.BlockSpec(memory_space=pl.ANY)] * 4
        args += list(prev)
        aliases = {6 + k: k for k in range(4)}
    grid_spec = pltpu.PrefetchScalarGridSpec(num_scalar_prefetch=1, grid=(2, hr // tr),
                                             in_specs=in_specs, out_specs=[half] * 4)
    return pl.pallas_call(body, grid_spec=grid_spec, out_shape=[jax.ShapeDtypeStruct(w.shape, F32)] * 4,
                          input_output_aliases=aliases, compiler_params=_cparams(("parallel", "parallel")),
                          name=name)(*args)


def sum8(name, g):
    _, rows, cols = g.shape
    tr = _row_tile(rows, cols, 9)

    def body(*refs):
        acc = refs[0][0]
        for r in refs[1:N_DEV]:
            acc = acc + r[0]
        refs[N_DEV][...] = acc

    in_specs = [pl.BlockSpec((1, tr, cols), functools.partial(lambda k, i: (k, i, 0), k)) for k in range(N_DEV)]
    return pl.pallas_call(body, grid=(rows // tr,), in_specs=in_specs,
                          out_specs=pl.BlockSpec((tr, cols), lambda i: (i, 0)),
                          out_shape=jax.ShapeDtypeStruct((rows, cols), F32),
                          compiler_params=_cparams(("parallel",)), name=name)(*([g] * N_DEV))


_QKV, _AB, _GZ = (0, 1536), (1536, 1544), (1544, 2056)
_HG = (2056, 4104)
_SZ, _XBC, _DT = (4104, 4616), (4616, 5640), (5640, 5648)
_GATES = (5648, 8720)


def _split_w_in(w8):
    _, hr, cs = w8.shape
    w4 = w8.reshape(N_CHIPS, 2 * hr, cs)

    def cols(rng):
        lo, hi = rng
        return [w4[j][:, max(lo, j * cs) - j * cs:min(hi, (j + 1) * cs) - j * cs]
                for j in range(N_CHIPS) if max(lo, j * cs) < min(hi, (j + 1) * cs)]

    pad = [jnp.zeros((2 * hr, 120), w8.dtype)]
    return (jnp.concatenate(cols(_GATES), axis=1),
            jnp.concatenate(cols(_QKV) + cols(_GZ) + cols(_AB) + pad, axis=1),
            jnp.concatenate(cols(_HG), axis=1),
            jnp.concatenate(cols(_SZ) + cols(_XBC) + cols(_DT) + pad, axis=1))


def _stack_w_in(g, a, b, c):
    segments = [(a, 0, 1536), (a, 2048, 2056), (a, 1536, 2048), (b, 0, 2048), (c, 0, 512), (c, 512, 1536),
                (c, 1536, 1544), (g, 0, 3072)]
    cs = sum(s1 - s0 for _, s0, s1 in segments) // N_CHIPS
    chips = []
    for j in range(N_CHIPS):
        parts, off = [], 0
        for arr, s0, s1 in segments:
            u0, u1 = max(j * cs, off), min((j + 1) * cs, off + s1 - s0)
            if u0 < u1:
                parts.append(arr[:, s0 + u0 - off:s0 + u1 - off])
            off += s1 - s0
        chips.append(jnp.concatenate(parts, axis=1))
    rows = g.shape[0]
    return jnp.stack(chips).reshape(N_CHIPS, 2, rows // 2, cs)


def _rows8(rows, width):
    out = [jnp.pad(r.astype(F32), (0, width - r.shape[0])) for r in rows]
    out += [jnp.zeros((width,), F32)] * (8 - len(out))
    return jnp.stack(out)


class _Packer:
    def __init__(self):
        self.items, self.size = [], 0

    def add(self, name, shape):
        n = 1
        for d in shape:
            n *= d
        self.items.append((name, tuple(shape), self.size, n))
        self.size += n

    def rows(self):
        return -(-self.size // 8192) * 8

    def pack(self, values):
        flat = [values[name].astype(F32).reshape(-1) for name, _, _, _ in self.items]
        flat.append(jnp.zeros((self.rows() * 1024 - self.size,), F32))
        return jnp.concatenate(flat).reshape(self.rows(), 1024)

    def unpack(self, buf):
        flat = buf.reshape(-1)
        return {name: flat[off:off + n].reshape(shape) for name, shape, off, n in self.items}


def _stack_by_chip(g, axis):
    l, r, c = g.shape
    if axis == 2:
        cs = c // N_CHIPS
        g = g.reshape(l, r, N_CHIPS, cs).transpose(2, 0, 1, 3).reshape(N_CHIPS, 2, l * r // 2, cs)
    else:
        rs = r // N_CHIPS
        g = g.reshape(l, N_CHIPS, rs, c).transpose(1, 0, 2, 3).reshape(N_CHIPS, 2, l * rs // 2, c)
    return g


def _unstack_gathered(w8, l, axis):
    _, hr, cs = w8.shape
    w = w8.reshape(N_CHIPS, l, 2 * hr // l, cs)
    if axis == 2:
        return w.transpose(1, 2, 0, 3).reshape(l, 2 * hr // l, N_CHIPS * cs)
    return w.transpose(1, 0, 2, 3).reshape(l, N_CHIPS * 2 * hr // l, cs)


_BIG = (("w_in", 2), ("w_br_a", 2), ("w_br_b", 2), ("w_br_c", 2), ("w_out", 1), ("ffn_w_up", 2), ("ffn_w_down", 1))
_SMALL = ("b_ada", "norm1_w", "gdn_conv_w", "gdn_a_log", "gdn_dt_bias", "gdn_norm_w", "hgrn_lb_param",
          "hgrn_norm_w", "ssd_conv_w", "ssd_conv_b", "ssd_a_log", "ssd_dt_bias", "ssd_d", "ssd_norm_w",
          "norm2_w", "ffn_conv_w", "ffn_conv_b", "final_norm_w")
_WEIGHTS = ("w_ada", "b_ada", "norm1_w", "w_in", "gdn_conv_w", "gdn_a_log", "gdn_dt_bias", "gdn_norm_w",
            "hgrn_lb_param", "hgrn_norm_w", "ssd_conv_w", "ssd_conv_b", "ssd_a_log", "ssd_dt_bias", "ssd_d",
            "ssd_norm_w", "w_br_a", "w_br_b", "w_br_c", "w_out", "norm2_w", "ffn_w_up", "ffn_conv_w",
            "ffn_conv_b", "ffn_w_down", "final_norm_w")
_R_GDN, _R_HGRN, _R_SSD, _R_FFN = 256, 128, 256, 256


_MASKS = ((1, 0), (0, 1), (1, 1))


def _flip(k, x, y):
    return (1 - x if _MASKS[k][0] else x), (1 - y if _MASKS[k][1] else y)


def _rs_d2d(grads):
    plan = [functools.partial(lambda i, ins, outs, x, y, c: (ins[i].at[:, pl.ds(1 - c, 1)], outs[i], (x, y, 1 - c)), i)
            for i in range(len(grads))]
    return grads, [jax.ShapeDtypeStruct((N_CHIPS, 1) + g.shape[2:], F32) for g in grads], plan, (), ()


def _rs_ici(grads, recv, tag):
    n = len(grads)
    c_arr = lax.axis_index("c").astype(jnp.int32).reshape(1)
    chip_arr = (2 * lax.axis_index("x") + lax.axis_index("y")).astype(jnp.int32).reshape(1)
    q = [sum_halves("rs_sum_d2d%s_%d" % (tag, i), g, r, c_arr, chip_arr) for i, (g, r) in enumerate(zip(grads, recv))]
    qf, qb = [a for a, _ in q], [b for _, b in q]

    def ici(i, k, ins, outs, x, y, c):
        px, py = _flip(k, x, y)
        return ins[i].at[2 * px + py], outs[3 * i + k], (px, py, c)

    plan = [functools.partial(ici, i, k) for i in range(n) for k in range(3)]
    shapes = [jax.ShapeDtypeStruct(g.shape[2:], BF16) for g in grads for _ in range(3)]
    return qf, (qb, shapes, plan, (), ())


def _rs_finish(qf, res, tag):
    n = len(qf)
    red = [sum_chips("rs_sum_ici%s_%d" % (tag, i), qf[i], res[3 * i:3 * i + 3]) for i in range(n)]
    plan = [functools.partial(lambda i, ins, outs, x, y, c: (ins[i], outs[i], (x, y, 1 - c)), i) for i in range(n)]
    other = exchange("rs_swap" + tag, red, [jax.ShapeDtypeStruct(r.shape, F32) for r in red], plan)
    return red, other


def _join_sides(sides):
    ins, shapes, plan = [], [], []
    for s_ins, s_shapes, s_plan, _, _ in sides:
        def shifted(fn, i0, i1, o0, o1, in_refs, out_refs, x, y, c):
            return fn(in_refs[i0:i1], out_refs[o0:o1], x, y, c)

        i0, o0 = len(ins), len(shapes)
        plan += [functools.partial(shifted, fn, i0, i0 + len(s_ins), o0, o0 + len(s_shapes)) for fn in s_plan]
        ins += list(s_ins)
        shapes += list(s_shapes)
    return ins, shapes, plan, (), ()


def _gather_side(pieces):
    n = len(pieces)

    def send(i, k, ins, outs, x, y, c):
        px, py = _flip(k, x, y)
        return ins[i].at[c], outs[i].at[2 * (2 * x + y) + c], (px, py, c)

    def to_sibling(i, h, ins, outs, x, y, c):
        return ins[i].at[h], outs[i].at[2 * (2 * x + y) + h], (x, y, 1 - c)

    def pass_on(i, k, ins, outs, x, y, c):
        px, py = _flip(k, x, y)
        blk = 2 * (2 * px + py) + c
        return outs[i].at[blk], outs[i].at[blk], (x, y, 1 - c)

    plan = [functools.partial(send, i, k) for i in range(n) for k in range(3)]
    plan += [functools.partial(to_sibling, i, h) for i in range(n) for h in range(2)]
    then = [functools.partial(pass_on, i, k) for i in range(n) for k in range(3)]
    shapes = [jax.ShapeDtypeStruct((N_DEV,) + p.shape[1:], p.dtype) for p in pieces]
    return pieces, shapes, plan, (), then


def kernel(x, c, w_ada, b_ada, norm1_w, w_in, gdn_conv_w, gdn_a_log, gdn_dt_bias, gdn_norm_w, hgrn_lb_param, hgrn_norm_w, ssd_conv_w, ssd_conv_b, ssd_a_log, ssd_dt_bias, ssd_d, ssd_norm_w, w_br_a, w_br_b, w_br_c, w_out, norm2_w, ffn_w_up, ffn_conv_w, ffn_conv_b, ffn_w_down, final_norm_w, loss_target, m_w_ada, m_b_ada, m_norm1_w, m_w_in, m_gdn_conv_w, m_gdn_a_log, m_gdn_dt_bias, m_gdn_norm_w, m_hgrn_lb_param, m_hgrn_norm_w, m_ssd_conv_w, m_ssd_conv_b, m_ssd_a_log, m_ssd_dt_bias, m_ssd_d, m_ssd_norm_w, m_w_br_a, m_w_br_b, m_w_br_c, m_w_out, m_norm2_w, m_ffn_w_up, m_ffn_conv_w, m_ffn_conv_b, m_ffn_w_down, m_final_norm_w, v_w_ada, v_b_ada, v_norm1_w, v_w_in, v_gdn_conv_w, v_gdn_a_log, v_gdn_dt_bias, v_gdn_norm_w, v_hgrn_lb_param, v_hgrn_norm_w, v_ssd_conv_w, v_ssd_conv_b, v_ssd_a_log, v_ssd_dt_bias, v_ssd_d, v_ssd_norm_w, v_w_br_a, v_w_br_b, v_w_br_c, v_w_out, v_norm2_w, v_ffn_w_up, v_ffn_conv_w, v_ffn_conv_b, v_ffn_w_down, v_final_norm_w):
    loc = dict(locals())
    w = {k: loc[k] for k in _WEIGHTS}
    mom = {k: loc["m_" + k] for k in _WEIGHTS}
    var = {k: loc["v_" + k] for k in _WEIGHTS}
    nb, s, d = x.shape
    t = nb * s
    depth = w_ada.shape[0]
    chip = 2 * lax.axis_index("x") + lax.axis_index("y")
    dev = 2 * chip + lax.axis_index("c")
    x0 = x.reshape(t, d)
    target = loss_target.reshape(t, d)

    small_in = [c, gdn_conv_w.reshape(depth * 4, -1), ssd_conv_w.reshape(depth * 4, -1),
                ffn_conv_w.reshape(depth * 3, -1)]
    c_all, gcw, scw, fcw = allgather8("ag_small", small_in, [False] * 4)
    c_all = c_all.reshape(N_DEV * nb, d)

    def conv_full(g, taps):
        g = g[::2].reshape(N_CHIPS, depth, taps, -1)
        return g.transpose(1, 2, 0, 3).reshape(depth, taps, -1)

    gdn_cw, ssd_cw, ffn_cw = conv_full(gcw, 4), conv_full(scw, 4), conv_full(fcw, 3)

    axis_of = dict(_BIG)
    first_needed, later = ("w_in",), tuple(n for n, _ in _BIG if n != "w_in")
    wls = [dict() for _ in range(depth)]

    def pieces(keys):
        out = []
        for l, name in keys:
            a = w[name][l].astype(BF16)
            out.append(a.reshape(2, a.shape[0] // 2, a.shape[1]))
        return out

    def arrived(keys, bufs):
        for (l, name), g in zip(keys, bufs):
            if name == "w_in":
                wls[l]["w_g"], wls[l]["w_a"], wls[l]["w_b"], wls[l]["w_c"] = _split_w_in(g)
            else:
                wls[l][name] = _unstack_gathered(g, 1, axis_of[name])[0]

    keys0 = [(0, n) for n in first_needed]
    arrived(keys0, allgather8("ag_weights0", pieces(keys0), [True] * len(keys0)))

    (c_act,) = elementwise("silu_c", lambda v: (_silu(v),), [c_all], [F32])
    mod_cols = jnp.concatenate([matmul("ada_fwd%d" % l, c_act, w_ada[l], "nn") for l in range(depth)], axis=0)
    (mod8,) = allgather8("ag_mod", [mod_cols], [False])
    mod = mod8[::2].reshape(N_CHIPS, depth, N_DEV * nb, -1).transpose(1, 2, 0, 3).reshape(depth, N_DEV * nb, 6 * d)
    mod = lax.dynamic_slice_in_dim(mod, dev * nb, nb, axis=1) + b_ada[:, None, :]

    def mod_part(l, k):
        return mod[l, :, k * d:(k + 1) * d].reshape(nb, 1, d)

    saved = []
    xl = x0
    for l in range(depth):
        sfx = str(l)
        wl = wls[l]
        sv = {"x0": xl}
        shift1, scale1, gate1, shift2, scale2, gate2 = [mod_part(l, k) for k in range(6)]
        sv["mods"] = (shift1, scale1, gate1, shift2, scale2, gate2)
        h, h_t = normmod_fwd("norm1_fwd" + sfx, xl, norm1_w[l][None], shift1, scale1, nb, s)
        pg = matmul("proj_g" + sfx, h, wl["w_g"], "nn")
        pa = matmul("proj_a" + sfx, h, wl["w_a"], "nn")
        pb = matmul("proj_b" + sfx, h, wl["w_b"], "nn")
        pc = matmul("proj_c" + sfx, h, wl["w_c"], "nn")
        gdn_p = [_rows8(list(gdn_cw[l]), 1536), _rows8([gdn_a_log[l], gdn_dt_bias[l], gdn_norm_w[l]], 128)]
        hgrn_p = [_rows8(list(hgrn_lb_param), 512), _rows8([hgrn_norm_w[l]], 128)]
        ssd_p = [_rows8(list(ssd_cw[l]), 1024), _rows8([ssd_conv_b[l], ssd_norm_w[l]], 1024),
                 _rows8([ssd_a_log[l], ssd_dt_bias[l], ssd_d[l]], 128)]
        ffn_p = [_rows8(list(ffn_cw[l]) + [ffn_conv_b[l]], 2 * FFN_HIDDEN)]
        hgrn_fn = make_hgrn_tile(l, depth)
        keys = [(l, n) for n in later] + ([(l + 1, n) for n in first_needed] if l + 1 < depth else [])
        oa, st_a, *bufs = seq_fwd("gdn_fwd" + sfx, gdn_tile, gdn_p, [pa], True, [(512, BF16)], (512, 128), nb, s,
                                  _R_GDN, side=_gather_side(pieces(keys)))
        arrived(keys, bufs)
        ob, st_b = seq_fwd("hgrn_fwd" + sfx, hgrn_fn, hgrn_p, [pb], False, [(512, BF16)], (512, 128), nb, s, _R_HGRN)
        oc, st_c = seq_fwd("ssd_fwd" + sfx, ssd_tile, ssd_p, [pc], True, [(512, BF16)], (256, 256), nb, s, _R_SSD)
        x1, *merge_saved = merge_fwd("merge_fwd" + sfx, xl, oa, ob, oc, pg, gate1, wl["w_br_a"], wl["w_br_b"],
                                     wl["w_br_c"], wl["w_out"], nb, s)
        h2, h2_t = normmod_fwd("norm2_fwd" + sfx, x1, norm2_w[l][None], shift2, scale2, nb, s)
        u = matmul("ffn_up" + sfx, h2, wl["ffn_w_up"], "nn", tn=FFN_HIDDEN)
        act, act_t = seq_fwd("convglu_fwd" + sfx, convglu_tile_t, ffn_p, [u], True,
                             [(FFN_HIDDEN, BF16), (FFN_HIDDEN, BF16, "T")], None, nb, s, _R_FFN)
        xl, f = matmul("ffn_down" + sfx, act, wl["ffn_w_down"], "nn", resid=(x1, gate2, s))
        sv.update(merge_saved=merge_saved, h_t=h_t, h2_t=h2_t, act_t=act_t, pg=pg, pa=pa, pb=pb, pc=pc, oa=oa, ob=ob, oc=oc, st_a=st_a, st_b=st_b, st_c=st_c, x1=x1,
                  u=u, f=f, gdn_p=gdn_p, hgrn_p=hgrn_p, ssd_p=ssd_p, ffn_p=ffn_p, hgrn_fn=hgrn_fn)
        saved.append(sv)

    dx, loss_part, d_final = loss_head("loss_head", xl, final_norm_w[None], target)

    sg = {}
    dmod = [None] * depth
    d_lb = None
    reduced = {}
    early = [n for n, _ in _BIG if n != "w_in"]
    to_d2d = to_ici = None

    def finish(keys, qf, res, tag):
        for key, mine, other in zip(keys, *_rs_finish(qf, res, tag)):
            reduced[key] = (mine, other)
    for l in reversed(range(depth)):
        sfx = str(l)
        sv, wl = saved[l], wls[l]
        gfull = {}
        shift1, scale1, gate1, shift2, scale2, gate2 = sv["mods"]
        df, dgate2 = resid_bwd("resid_bwd" + sfx, dx, sv["f"], gate2, nb, s)
        dact = matmul("ffn_down_dx" + sfx, df, wl["ffn_w_down"], "nt")
        gfull["ffn_w_down"] = matmul("ffn_down_dw" + sfx, sv["act_t"], df, "nn", tm=1408, tn=512, tk=4096)
        cg_args = ("convglu_bwd" + sfx, convglu_tile, sv["ffn_p"], [sv["u"]], True, None, [dact], [BF16], None, nb, s,
                   _R_FFN)
        if to_d2d is None:
            (du,), (dcw,) = seq_bwd(*cg_args)
        else:
            lp, stacked = to_d2d
            (du,), (dcw,), recv = seq_bwd(*cg_args, side=_rs_d2d(stacked))
            to_ici = ([(lp, "w_in")],) + _rs_ici(stacked, recv, "i%d" % lp) + ("i%d" % lp,)
        dh2 = matmul("ffn_up_dx" + sfx, du, wl["ffn_w_up"], "nt")
        gfull["ffn_w_up"] = matmul("ffn_up_dw" + sfx, sv["h2_t"], du, "nn", tm=1024, tn=512, tk=4096)
        dx1, dnw2, dshift2, dscale2 = normmod_bwd("norm2_bwd" + sfx, sv["x1"], norm2_w[l][None], shift2, scale2, dh2, dx,
                                                  nb, s)
        doa, dob, doc, dpg, dgate1, dwa, dwb, dwc, dwo = merge_bwd(
            "merge_bwd" + sfx, sv["oa"], sv["ob"], sv["oc"], sv["pg"], gate1, wl["w_br_a"], wl["w_br_b"],
            wl["w_br_c"], wl["w_out"], sv["merge_saved"][:3], *sv["merge_saved"][3:], dx1, nb, s)
        gfull["w_br_a"], gfull["w_br_b"], gfull["w_br_c"], gfull["w_out"] = dwa, dwb, dwc, dwo
        gdn_args = ("gdn_bwd" + sfx, gdn_tile, sv["gdn_p"], [sv["pa"]], True, sv["st_a"], [doa], [F32], (512, 128),
                    nb, s, _R_GDN)
        stacked = [_stack_by_chip(gfull[n][None], axis_of[n]) for n in early]
        (dpc,), (dscw, dspv, dsps), recv = seq_bwd("ssd_bwd" + sfx, ssd_tile, sv["ssd_p"], [sv["pc"]], True, sv["st_c"],
                                                   [doc], [F32], (256, 256), nb, s, _R_SSD, side=_rs_d2d(stacked))
        hosted = [([(l, n) for n in early],) + _rs_ici(stacked, recv, "e" + sfx) + ("e" + sfx,)]
        if to_ici is not None:
            hosted.append(to_ici)
        (dpa,), (dgcw, dgpk), res = seq_bwd(*gdn_args, side=_join_sides([h[2] for h in hosted]))
        for keys, qf, side, tag in hosted:
            finish(keys, qf, res[:len(side[1])], tag)
            res = res[len(side[1]):]
        (dpb,), (dlbp, dhnw) = seq_bwd("hgrn_bwd" + sfx, sv["hgrn_fn"], sv["hgrn_p"], [sv["pb"]], False, sv["st_b"],
                                       [dob], [F32], (512, 128), nb, s, _R_HGRN)
        dh = matmul("proj_g_dx" + sfx, dpg, wl["w_g"], "nt")
        dh = matmul("proj_a_dx" + sfx, dpa, wl["w_a"], "nt", addend=dh)
        dh = matmul("proj_b_dx" + sfx, dpb, wl["w_b"], "nt", addend=dh)
        dh = matmul("proj_c_dx" + sfx, dpc, wl["w_c"], "nt", addend=dh)
        stacked_w_in = _stack_w_in(
            matmul("proj_g_dw" + sfx, sv["h_t"], dpg, "nn", tm=1024, tn=512, tk=4096),
            matmul("proj_a_dw" + sfx, sv["h_t"], dpa, "nn", tm=1024, tk=1024),
            matmul("proj_b_dw" + sfx, sv["h_t"], dpb, "nn", tm=1024, tn=512, tk=4096),
            matmul("proj_c_dw" + sfx, sv["h_t"], dpc, "nn", tm=1024, tk=1024))
        dx, dnw1, dshift1, dscale1 = normmod_bwd("norm1_bwd" + sfx, sv["x0"], norm1_w[l][None], shift1, scale1, dh, dx1,
                                                 nb, s)
        dmod[l] = jnp.concatenate([dshift1, dscale1, dgate1, dshift2, dscale2, dgate2], axis=-1).reshape(nb, 6 * d)
        d_lb = dlbp[:depth] if d_lb is None else d_lb + dlbp[:depth]
        sg[l] = dict(norm1_w=dnw1[0], norm2_w=dnw2[0], gdn_conv_w=dgcw[:4], gdn_a_log=dgpk[0, :4],
                     gdn_dt_bias=dgpk[1, :4], gdn_norm_w=dgpk[2], hgrn_norm_w=dhnw[0], ssd_conv_w=dscw[:4],
                     ssd_conv_b=dspv[0], ssd_norm_w=dspv[1, :512], ssd_a_log=dsps[0, :8], ssd_dt_bias=dsps[1, :8],
                     ssd_d=dsps[2, :8], ffn_conv_w=dcw[:3], ffn_conv_b=dcw[3])
        to_d2d = (l, [stacked_w_in])
    lp, stacked = to_d2d
    recv = exchange("rs_d2d_i%d" % lp, *_rs_d2d(stacked)[:4])
    qf, side = _rs_ici(stacked, recv, "i%d" % lp)
    finish([(lp, "w_in")], qf, exchange("rs_ici_i%d" % lp, *side[:4]), "i%d" % lp)
    grad_x = dx.reshape(nb, s, d)

    dmod = jnp.stack(dmod)
    (b_sum,) = elementwise("bias_rows", lambda *r: (functools.reduce(lambda p, q: p + q, r),),
                           [dmod[:, b].reshape(depth * 6, d) for b in range(nb)], [F32])
    per_layer = ("norm1_w", "norm2_w", "gdn_conv_w", "gdn_a_log", "gdn_dt_bias", "gdn_norm_w", "hgrn_norm_w",
                 "ssd_conv_w", "ssd_conv_b", "ssd_norm_w", "ssd_a_log", "ssd_dt_bias", "ssd_d", "ffn_conv_w", "ffn_conv_b")
    vals = {k: jnp.stack([sg[l][k] for l in range(depth)]) for k in per_layer}
    vals.update(loss=loss_part[0, :1], b_ada=b_sum.reshape(depth, 6 * d), hgrn_lb_param=d_lb, final_norm_w=d_final[0])
    gp = _Packer()
    for k, v in vals.items():
        gp.add(k, v.shape)
    packed8, dmod8 = allgather8("ag_grads", [gp.pack(vals), dmod.reshape(depth * nb, 6 * d)], [False, False])
    gs = gp.unpack(sum8("sum_small", packed8))
    loss = gs["loss"].reshape(())

    def my_cols(g):
        cs = g.shape[-1] // N_CHIPS
        return lax.dynamic_slice_in_dim(g, chip * cs, cs, axis=g.ndim - 1)

    for k in ("gdn_conv_w", "ssd_conv_w", "ffn_conv_w"):
        gs[k] = my_cols(gs[k])

    dmod_all = dmod8.reshape(N_DEV, depth, nb, 6 * d).transpose(1, 0, 2, 3).reshape(depth, N_DEV * nb, 6 * d)
    dmod_mine = lax.dynamic_slice_in_dim(dmod_all, chip * (6 * d // N_CHIPS), 6 * d // N_CHIPS, axis=2)
    g_w_ada = jnp.stack([matmul("ada_dw%d" % l, c_act, dmod_mine[l], "tn", tm=1024) for l in range(depth)])

    c_arr = lax.axis_index("c").astype(jnp.int32).reshape(1)
    grads, delta, new_m, new_v = {}, {}, {}, {}
    for i, (name, _) in enumerate(_BIG):
        shp = w[name].shape
        halves = lambda a: a.reshape((depth, 2) + reduced[(0, name)][0].shape)
        res = None
        for l in reversed(range(depth)):
            res = adamw_halves("adamw_%s%d" % (name, l), halves(w[name]), halves(mom[name]), halves(var[name]), l,
                               *reduced[(l, name)], c_arr, prev=res)
        grads[name], delta[name], new_m[name], new_v[name] = [r.reshape(shp) for r in res]
    grads["w_ada"] = g_w_ada
    for k in _SMALL:
        grads[k] = gs[k].reshape(w[k].shape)
    shp = w_ada.shape
    flat = lambda a: a.reshape(shp[0] * shp[1], shp[2])
    dl, nm, nv = adamw("adamw_w_ada", flat(w_ada), flat(g_w_ada), flat(m_w_ada), flat(v_w_ada))
    delta["w_ada"], new_m["w_ada"], new_v["w_ada"] = dl.reshape(shp), nm.reshape(shp), nv.reshape(shp)
    sp = _Packer()
    for k in _SMALL:
        sp.add(k, w[k].shape)
    dl, nm, nv = adamw("adamw_small", sp.pack(w), sp.pack(grads), sp.pack(mom), sp.pack(var))
    delta.update(sp.unpack(dl))
    new_m.update(sp.unpack(nm))
    new_v.update(sp.unpack(nv))

    return (loss, grad_x, *[grads[k] for k in _WEIGHTS], *[delta[k] for k in _WEIGHTS],
            *[new_m[k] for k in _WEIGHTS], *[new_v[k] for k in _WEIGHTS])
```

```python
import functools

import jax
import jax.numpy as jnp
from jax import lax
from jax.experimental import pallas as pl
from jax.experimental.pallas import tpu as pltpu

F32 = jnp.float32
BF16 = jnp.bfloat16
HI = lax.Precision.HIGHEST
MESH = pl.DeviceIdType.MESH

EPS = 1e-6
D_MODEL = 1024
GDN_HEADS, GDN_DK, GDN_CHUNK = 4, 128, 64
HGRN_HEADS, HGRN_DK, HGRN_CHUNK = 4, 128, 16
SSD_HEADS, SSD_P, SSD_GROUPS, SSD_STATE, SSD_CHUNK = 8, 64, 2, 128, 64
FFN_HIDDEN = 2816
N_CHIPS = 4
N_DEV = 8

ADAM_LR, ADAM_B1, ADAM_B2, ADAM_EPS, ADAM_WD, ADAM_STEP = 0.001, 0.9, 0.999, 1e-08, 0.01, 10

W_G, W_A, W_B, W_C = 3072, 2176, 2048, 1664
VMEM_LIMIT = 56 * 1024 * 1024


def _cparams(sem):
    return pltpu.CompilerParams(dimension_semantics=sem, vmem_limit_bytes=VMEM_LIMIT)


def _dg(a, b, ca, cb):
    return lax.dot_general(a.astype(BF16), b.astype(BF16), (((ca,), (cb,)), ((), ())),
                           preferred_element_type=F32)


@jax.custom_vjp
def bdot(a, b):
    return _dg(a, b, 1, 0)


bdot.defvjp(lambda a, b: (_dg(a, b, 1, 0), (a, b)),
            lambda r, g: (_dg(g, r[1], 1, 1), _dg(r[0], g, 0, 0)))


@jax.custom_vjp
def bdot_nt(a, b):
    return _dg(a, b, 1, 1)


bdot_nt.defvjp(lambda a, b: (_dg(a, b, 1, 1), (a, b)),
               lambda r, g: (_dg(g, r[1], 1, 0), _dg(g, r[0], 0, 0)))


@jax.custom_vjp
def bdot_tn(a, b):
    return _dg(a, b, 0, 0)


bdot_tn.defvjp(lambda a, b: (_dg(a, b, 0, 0), (a, b)),
               lambda r, g: (_dg(r[1], g, 1, 1), _dg(r[0], g, 1, 0)))


def _split(x, n):
    parts, rest = [], x
    for _ in range(n):
        p = rest.astype(BF16)
        parts.append(p)
        rest = rest - p.astype(F32)
    return parts


def _dgb(a, b, ca, cb):
    return lax.dot_general(a, b, (((ca,), (cb,)), ((), ())), preferred_element_type=F32)


def _dg3(a, b, ca, cb):
    (ah, al), (bh, bl) = _split(a, 2), _split(b, 2)
    return _dgb(jnp.concatenate([ah, ah, al], axis=ca), jnp.concatenate([bh, bl, bh], axis=cb), ca, cb)


@jax.custom_vjp
def hdot(a, b):
    return _dg3(a, b, 1, 0)


hdot.defvjp(lambda a, b: (_dg3(a, b, 1, 0), (a, b)),
            lambda r, g: (_dg3(g, r[1], 1, 1), _dg3(r[0], g, 0, 0)))


def _dge(e, x, ce, cx, e_first):
    eb = e.astype(BF16)
    es = jnp.concatenate([eb, eb, eb], axis=ce)
    xs = jnp.concatenate(_split(x, 3), axis=cx)
    return _dgb(es, xs, ce, cx) if e_first else _dgb(xs, es, cx, ce)


@jax.custom_vjp
def ldot(e, x):
    return _dge(e, x, 1, 0, True)


ldot.defvjp(lambda e, x: (_dge(e, x, 1, 0, True), e),
            lambda e, g: (jnp.zeros_like(e), _dge(e, g, 0, 0, True)))


@jax.custom_vjp
def rdot(x, e):
    return _dge(e, x, 0, 1, False)


rdot.defvjp(lambda x, e: (_dge(e, x, 0, 1, False), e),
            lambda e, g: (_dge(e, g, 1, 1, False), jnp.zeros_like(e)))


@jax.custom_vjp
def _sigmoid(x):
    return 1.0 / (1.0 + jnp.exp(-x))


def _sigmoid_fwd(x):
    g = 1.0 / (1.0 + jnp.exp(-x))
    return g, g


_sigmoid.defvjp(_sigmoid_fwd, lambda g, ct: (ct * g * (1.0 - g),))


def _silu(x):
    return x * _sigmoid(x)


def _softplus(x):
    return jnp.maximum(x, 0.0) + jnp.log(1.0 + jnp.exp(-jnp.abs(x)))


def _rms(x, w):
    return x * lax.rsqrt(jnp.mean(x * x, axis=-1, keepdims=True) + EPS) * w


def _iota(shape, dim):
    return lax.broadcasted_iota(jnp.int32, shape, dim)


def _tri_ones(n, chunk, kind):
    i, j = _iota((n, n), 0), _iota((n, n), 1)
    same = lax.div(i, chunk) == lax.div(j, chunk)
    if kind == "incl":
        m = same & (j <= i)
    elif kind == "strict":
        m = same & (j < i)
    elif kind == "all":
        m = same
    else:
        m = same & (lax.rem(j, chunk) < (chunk // 2))
    return m


def _causal_conv(w, halo, x, width):
    r = x.shape[0]
    xin = jnp.concatenate([halo, x], axis=0)
    y = w[width - 1:width, :] * x
    for k in range(width - 1):
        off = 8 - (width - 1) + k
        y = y + w[k:k + 1, :] * xin[off:off + r, :]
    return y


def _each(fn, *lists):
    return [fn(*a) for a in zip(*lists)]


def _neumann(ms):
    n = ms[0].shape[0]
    eye = (_iota((n, n), 0) == _iota((n, n), 1)).astype(F32)
    accs = [eye - m for m in ms]
    ps = ms
    steps = 1
    while steps * 2 < n:
        ps = _each(hdot, ps, ps)
        accs = [acc + ap for acc, ap in zip(accs, _each(hdot, accs, ps))]
        steps *= 2
    return accs


@jax.custom_vjp
def tri_inverse(ms):
    return _neumann(ms)


def _tri_inverse_fwd(ms):
    ainvs = _neumann(ms)
    return ainvs, ainvs


def _tri_inverse_bwd(ainvs, gs):
    t = _each(lambda g, a: _dg3(g, a, 1, 1), gs, ainvs)
    return ([-x for x in _each(lambda a, y: _dg3(a, y, 0, 0), ainvs, t)],)


tri_inverse.defvjp(_tri_inverse_fwd, _tri_inverse_bwd)


def gdn_tile(params, state, ins, halos):
    conv_w, pk = params
    (pa,), (ha,) = ins, halos
    r = pa.shape[0]
    c, nh, dk = GDN_CHUNK, GDN_HEADS, GDN_DK
    kw = nh * dk
    qkv = _silu(_causal_conv(conv_w, ha[:, :3 * kw], pa[:, :3 * kw], 4))
    z = pa[:, 3 * kw:4 * kw]
    gsm = pa[:, 4 * kw:]
    a_log, dtb, nw = pk[0:1, :], pk[1:2, :], pk[2:3, :]
    g_all = -jnp.exp(a_log) * _softplus(gsm + dtb)
    beta_all = _sigmoid(gsm)
    incl = _tri_ones(c, c, "incl")
    strict = _tri_ones(c, c, "strict")
    lmat = incl.astype(F32)
    scale = dk ** -0.5
    nck = r // c
    inst = [(ci, h) for ci in range(nck) for h in range(nh)]

    def l2n(v):
        return v * lax.rsqrt(jnp.sum(v * v, axis=-1, keepdims=True) + EPS)

    gcs = [ldot(lmat, g_all[ci * c:(ci + 1) * c, :]) for ci in range(nck)]
    gcts = [g.T for g in gcs]
    g_col = [gcs[ci][:, h:h + 1] for ci, h in inst]
    g_row = [gcts[ci][h:h + 1, :] for ci, h in inst]
    g_last = [gcs[ci][c - 1:c, h:h + 1] for ci, h in inst]
    beta = [beta_all[ci * c:(ci + 1) * c, nh + h:nh + h + 1] for ci, h in inst]
    qh = [l2n(qkv[ci * c:(ci + 1) * c, h * dk:(h + 1) * dk]) for ci, h in inst]
    kh = [l2n(qkv[ci * c:(ci + 1) * c, kw + h * dk:kw + (h + 1) * dk]) for ci, h in inst]
    vh = [qkv[ci * c:(ci + 1) * c, 2 * kw + h * dk:2 * kw + (h + 1) * dk] for ci, h in inst]
    decay = [jnp.where(incl, jnp.exp(jnp.where(incl, gc_ - gr_, 0.0)), 0.0) for gc_, gr_ in zip(g_col, g_row)]
    kb = [k * b for k, b in zip(kh, beta)]
    qs = [q * scale for q in qh]
    kk = _each(lambda a, b, k: bdot_nt(jnp.concatenate([a, b], axis=0), k), kb, qs, kh)
    ms = [jnp.where(strict, x[:c] * d, 0.0) for x, d in zip(kk, decay)]
    attn = [x[c:] * d for x, d in zip(kk, decay)]
    ainv = tri_inverse(ms)
    eg = [jnp.exp(g) for g in g_col]
    rhs = [jnp.concatenate([v * b, k_ * e], axis=1) for v, b, k_, e in zip(vh, beta, kb, eg)]
    sol = _each(hdot, ainv, rhs)
    qg = [q * e for q, e in zip(qs, eg)]
    k_end = [k * jnp.exp(gl - g) for k, gl, g in zip(kh, g_last, g_col)]
    e_last = [jnp.exp(gl) for gl in g_last]

    st = [state[h * dk:(h + 1) * dk, :] for h in range(nh)]
    outs = [[] for _ in range(nh)]
    for ci in range(nck):
        idx = [ci * nh + h for h in range(nh)]
        ws = [bdot(jnp.concatenate([sol[i][:, dk:], qg[i]], axis=0), st[h]) for h, i in enumerate(idx)]
        v_new = [sol[i][:, :dk] - w_[:c] for i, w_ in zip(idx, ws)]
        av = [bdot(attn[i], v) for i, v in zip(idx, v_new)]
        kv = [bdot_tn(k_end[i], v) for i, v in zip(idx, v_new)]
        for h, i in enumerate(idx):
            o = ws[h][c:] + av[h]
            st[h] = st[h] * e_last[i] + kv[h]
            outs[h].append(_rms(o, nw) * _silu(z[ci * c:(ci + 1) * c, h * dk:(h + 1) * dk]))
    out = jnp.concatenate([jnp.concatenate(o, axis=0) for o in outs], axis=1)
    return jnp.concatenate(st, axis=0), [out]


def make_hgrn_tile(layer, depth):
    def hgrn_tile(params, state, ins, halos):
        lbp, nwp = params
        (pb,) = ins
        r = pb.shape[0]
        c = HGRN_CHUNK
        kw = HGRN_HEADS * HGRN_DK
        rows = [lbp[i:i + 1, :] for i in range(depth)]
        mx = functools.reduce(jnp.maximum, rows)
        ex = [jnp.exp(x - mx) for x in rows]
        den = functools.reduce(lambda a, b: a + b, ex)
        soft = [e / den for e in ex]
        lb = functools.reduce(lambda a, b: a + b, soft[:layer + 1]) - soft[0]
        nw = nwp[0:1, :]
        q = _silu(pb[:, :kw])
        fr = pb[:, kw:2 * kw]
        logf = jnp.log(lb + (1.0 - lb) * _sigmoid(fr))
        k = (1.0 - lb) * _sigmoid(-fr)
        v = pb[:, 2 * kw:3 * kw]
        gate = pb[:, 3 * kw:]
        incl = _tri_ones(r, c, "incl")
        masks = jnp.concatenate([incl.astype(F32), _tri_ones(r, c, "upto").astype(F32),
                                 _tri_ones(r, c, "all").astype(F32)], axis=0)
        sums = ldot(masks, logf)
        g_cum, g_ref, g_end = sums[:r], sums[r:2 * r], sums[2 * r:]
        qs = q * jnp.exp(g_cum - g_ref)
        ks = k * jnp.exp(g_ref - g_cum)
        qg = q * jnp.exp(g_cum)
        k_end = k * jnp.exp(g_end - g_cum)
        e_end = jnp.exp(g_end)
        sls = [slice(h * HGRN_DK, (h + 1) * HGRN_DK) for h in range(HGRN_HEADS)]
        attn = [jnp.where(incl, bdot_nt(qs[:, sl], ks[:, sl]), 0.0) for sl in sls]
        o_intra = [bdot(a, v[:, sl]) for a, sl in zip(attn, sls)]
        nsub, dk = r // c, HGRN_DK
        own_block = lax.div(_iota((r, nsub * dk), 0), c) == lax.div(_iota((r, nsub * dk), 1), dk)

        def spread(a):
            return jnp.where(own_block, jnp.concatenate([a] * nsub, axis=1), 0.0)

        kv = [bdot_tn(v[:, sl], spread(k_end[:, sl])) for sl in sls]
        s_t = [state[sl, :] for sl in sls]
        entry = [[] for _ in sls]
        for j in range(nsub):
            for lst, s_h in zip(entry, s_t):
                lst.append(s_h)
            s_t = [s_h * e_end[j * c:j * c + 1, sl] + x[:, j * dk:(j + 1) * dk] for s_h, sl, x in zip(s_t, sls, kv)]
        o_inter = [bdot_nt(spread(qg[:, sl]), jnp.concatenate(e, axis=1)) for sl, e in zip(sls, entry)]
        outs = [_rms(oa + ob, nw) * _silu(gate[:, sl]) for oa, ob, sl in zip(o_intra, o_inter, sls)]
        return jnp.concatenate(s_t, axis=0), [jnp.concatenate(outs, axis=1)]
    return hgrn_tile


def ssd_tile(params, state, ins, halos):
    conv_w, pv, ps = params
    (pc,), (hc,) = ins, halos
    r = pc.shape[0]
    c = SSD_CHUNK
    inner = SSD_HEADS * SSD_P
    gw = inner // SSD_GROUPS
    z = pc[:, :inner]
    xbc = _silu(_causal_conv(conv_w, hc[:, inner:inner + 1024], pc[:, inner:inner + 1024], 4) + pv[0:1, :])
    ssm = pc[:, inner + 1024:]
    xs = xbc[:, :inner]
    bm = xbc[:, inner:inner + SSD_GROUPS * SSD_STATE]
    cm = xbc[:, inner + SSD_GROUPS * SSD_STATE:]
    a_log, dtb, dsk = ps[0:1, :], ps[1:2, :], ps[2:3, :]
    nw = pv[1:2, :inner]
    dt = _softplus(ssm + dtb)
    da = dt * (-jnp.exp(a_log))
    expand = (lax.div(_iota((128, inner), 1), SSD_P) == _iota((128, inner), 0)).astype(F32)
    xdt = xs * rdot(dt, expand)
    d_e = rdot(jnp.concatenate([dsk] * 8, axis=0), expand)[0:1, :]
    incl = _tri_ones(c, c, "incl")
    lmat = incl.astype(F32)
    st = [state[g * SSD_STATE:(g + 1) * SSD_STATE, :] for g in range(SSD_GROUPS)]
    hpg = SSD_HEADS // SSD_GROUPS
    nck = r // c
    groups = range(SSD_GROUPS)
    cg = [(ci, g) for ci in range(nck) for g in groups]
    rows = [slice(ci * c, (ci + 1) * c) for ci in range(nck)]
    gls = [slice(g * gw, (g + 1) * gw) for g in groups]
    acs = [ldot(lmat, da[rs, :]) for rs in rows]
    acs_t = [a.T for a in acs]
    acs_e = [rdot(a, expand) for a in acs]
    last_e = [a[c - 1:c, :] for a in acs_e]
    bm_g = [bm[rows[ci], g * SSD_STATE:(g + 1) * SSD_STATE] for ci, g in cg]
    cm_g = [cm[rows[ci], g * SSD_STATE:(g + 1) * SSD_STATE] for ci, g in cg]
    cb = _each(bdot_nt, cm_g, bm_g)
    heads = [(i, ci, g * hpg + hg) for i, (ci, g) in enumerate(cg) for hg in range(hpg)]
    seg = [jnp.where(incl, jnp.exp(jnp.where(incl, acs[ci][:, hh:hh + 1] - acs_t[ci][hh:hh + 1, :], 0.0)), 0.0)
           for _, ci, hh in heads]
    yd = [bdot(cb[i] * sg, xdt[rows[ci], hh * SSD_P:(hh + 1) * SSD_P]) for (i, ci, hh), sg in zip(heads, seg)]
    y_diag = [jnp.concatenate(yd[i * hpg:(i + 1) * hpg], axis=1) for i in range(len(cg))]
    xw = [xdt[rows[ci], gls[g]] * jnp.exp(last_e[ci][:, gls[g]] - acs_e[ci][:, gls[g]]) for ci, g in cg]
    e_acs = [jnp.exp(acs_e[ci][:, gls[g]]) for ci, g in cg]
    e_last = [jnp.exp(last_e[ci][:, gls[g]]) for ci, g in cg]
    kv = _each(bdot_tn, bm_g, xw)
    ys = []
    for ci in range(nck):
        idx = [ci * SSD_GROUPS + g for g in groups]
        y_off = [bdot(cm_g[i], st[g]) * e_acs[i] for g, i in zip(groups, idx)]
        st = [st[g] * e_last[i] + kv[i] for g, i in zip(groups, idx)]
        ys.append(jnp.concatenate([y_diag[i] + yo for i, yo in zip(idx, y_off)], axis=1))
    y = jnp.concatenate(ys, axis=0) + d_e * xs
    yz = y * _silu(z)
    out = jnp.concatenate([_rms(yz[:, g * gw:(g + 1) * gw], nw[:, g * gw:(g + 1) * gw])
                           for g in range(SSD_GROUPS)], axis=1)
    return jnp.concatenate(st, axis=0), [out]


def convglu_tile(params, state, ins, halos):
    (cw,) = params
    (u,), (hu,) = ins, halos
    y = _causal_conv(cw, hu, u, 3) + cw[3:4, :]
    return None, [_silu(y[:, :FFN_HIDDEN]) * y[:, FFN_HIDDEN:]]


def convglu_tile_t(params, state, ins, halos):
    _, (act,) = convglu_tile(params, state, ins, halos)
    return None, [act, act.T]


def _halo_map(nt, r):
    return lambda b, n: (jnp.maximum((b * nt + n) * (r // 8) - 1, 0), 0)


def _exchange_copies(plan, local_plan, in_refs, out_refs, send_sems, recv_sems, local_sems):
    x, y, c = lax.axis_index("x"), lax.axis_index("y"), lax.axis_index("c")
    copies = []
    for k, fn in enumerate(plan):
        src, dst, peer = fn(in_refs, out_refs, x, y, c)
        copies.append(pltpu.make_async_remote_copy(src_ref=src, dst_ref=dst, send_sem=send_sems.at[k],
                                                   recv_sem=recv_sems.at[k], device_id=peer, device_id_type=MESH))
    for k, fn in enumerate(local_plan):
        src, dst = fn(in_refs, out_refs, x, y, c)
        copies.append(pltpu.make_async_copy(src, dst, local_sems.at[k]))
    return copies


def _exchange_sems(plan, local_plan):
    return [pltpu.SemaphoreType.DMA((max(len(plan), 1),)), pltpu.SemaphoreType.DMA((max(len(plan), 1),)),
            pltpu.SemaphoreType.DMA((max(len(local_plan), 1),))]


def _host_exchange(side, body, in_specs, o_specs, out_shape, scratch, args, grid):
    s_ins, s_shapes, plan, local_plan, then = side
    n_in, n_out, n_scr = len(in_specs), len(o_specs), len(scratch)
    k_in, k_out = len(s_ins), len(s_shapes)
    any_spec = pl.BlockSpec(memory_space=pl.ANY)

    def hosted(*refs):
        own_in, s_in = refs[:n_in], refs[n_in:n_in + k_in]
        o0 = n_in + k_in
        own_out, s_out = refs[o0:o0 + n_out], refs[o0 + n_out:o0 + n_out + k_out]
        rest = refs[o0 + n_out + k_out:]
        own_scr, sems, sems_then = rest[:n_scr], rest[n_scr:n_scr + 3], rest[n_scr + 3:]
        ids = [pl.program_id(a) for a in range(len(grid))]
        first = functools.reduce(lambda p, q: p & q, [i == 0 for i in ids])
        last = functools.reduce(lambda p, q: p & q, [i == g - 1 for i, g in zip(ids, grid)])

        @pl.when(first)
        def _():
            for cp in _exchange_copies(plan, local_plan, s_in, s_out, *sems):
                cp.start()

        body(*own_in, *own_out, *own_scr)

        @pl.when(last)
        def _():
            for cp in _exchange_copies(plan, local_plan, s_in, s_out, *sems):
                cp.wait()
            passed = _exchange_copies(then, (), s_in, s_out, *sems_then)
            for cp in passed:
                cp.start()
            for cp in passed:
                cp.wait()

    return (hosted, list(in_specs) + [any_spec] * k_in, list(o_specs) + [any_spec] * k_out,
            list(out_shape) + list(s_shapes),
            list(scratch) + _exchange_sems(plan, local_plan) + _exchange_sems(then, ()),
            list(args) + list(s_ins))


def seq_fwd(name, tile_fn, params, ins, use_halo, out_specs, state_shape, nb, s, r, side=None):
    nt = s // r
    n_p, n_i, n_o = len(params), len(ins), len(out_specs)
    has_state = state_shape is not None

    def body(*refs):
        p_refs, i_refs = refs[:n_p], refs[n_p:n_p + n_i]
        h_refs = refs[n_p + n_i:n_p + 2 * n_i] if use_halo else ()
        k = n_p + n_i + len(h_refs)
        o_refs = refs[k:k + n_o]
        n = pl.program_id(1)
        state = None
        if has_state:
            sv_ref, st_ref = refs[k + n_o], refs[k + n_o + 1]

            @pl.when(n == 0)
            def _():
                st_ref[...] = jnp.zeros(state_shape, F32)

            state = st_ref[...]
            sv_ref[0, 0] = state
        pv = [p[...] for p in p_refs]
        iv = [i[...].astype(F32) for i in i_refs]
        hv = [jnp.where(n > 0, h[...].astype(F32), 0.0) for h in h_refs]
        new_state, ov = tile_fn(pv, state, iv, hv)
        for o_ref, o in zip(o_refs, ov):
            o_ref[...] = o.astype(o_ref.dtype)
        if has_state:
            st_ref[...] = new_state

    row = lambda b, n: (b * nt + n, 0)
    in_specs = [pl.BlockSpec(p.shape, lambda b, n: (0, 0)) for p in params]
    in_specs += [pl.BlockSpec((r, a.shape[1]), row) for a in ins]
    if use_halo:
        in_specs += [pl.BlockSpec((8, a.shape[1]), _halo_map(nt, r)) for a in ins]
    col = lambda b, n: (0, b * nt + n)
    out_shape, o_specs = [], []
    for w, dt, *transposed in out_specs:
        out_shape.append(jax.ShapeDtypeStruct((w, nb * s) if transposed else (nb * s, w), dt))
        o_specs.append(pl.BlockSpec((w, r), col) if transposed else pl.BlockSpec((r, w), row))
    scratch = []
    if has_state:
        out_shape.append(jax.ShapeDtypeStruct((nb, nt) + tuple(state_shape), F32))
        o_specs.append(pl.BlockSpec((1, 1) + tuple(state_shape), lambda b, n: (b, n, 0, 0)))
        scratch.append(pltpu.VMEM(tuple(state_shape), F32))
    args = list(params) + list(ins) + (list(ins) if use_halo else [])
    if side is not None:
        body, in_specs, o_specs, out_shape, scratch, args = _host_exchange(
            side, body, in_specs, o_specs, out_shape, scratch, args, (nb, nt))
    return pl.pallas_call(body, grid=(nb, nt), in_specs=in_specs, out_specs=o_specs, out_shape=out_shape,
                          scratch_shapes=scratch, compiler_params=_cparams(("arbitrary", "arbitrary")),
                          name=name)(*args)


def seq_bwd(name, tile_fn, params, ins, use_halo, states, douts, din_dtypes, state_shape, nb, s, r, side=None):
    nt = s // r
    n_p, n_i, n_o = len(params), len(ins), len(douts)
    has_state = state_shape is not None

    def body(*refs):
        p_refs, i_refs = refs[:n_p], refs[n_p:n_p + n_i]
        h_refs = refs[n_p + n_i:n_p + 2 * n_i] if use_halo else ()
        k = n_p + n_i + len(h_refs)
        sv_ref = None
        if has_state:
            sv_ref = refs[k]
            k += 1
        do_refs = refs[k:k + n_o]
        k += n_o
        di_refs, dp_refs = refs[k:k + n_i], refs[k + n_i:k + n_i + n_p]
        k += n_i + n_p
        dst_ref = None
        if has_state:
            dst_ref = refs[k]
            k += 1
        dh_refs = refs[k:k + len(h_refs)]
        b, nn = pl.program_id(0), pl.program_id(1)
        n = nt - 1 - nn

        @pl.when((b == 0) & (nn == 0))
        def _():
            for dp in dp_refs:
                dp[...] = jnp.zeros(dp.shape, F32)

        @pl.when(nn == 0)
        def _():
            if has_state:
                dst_ref[...] = jnp.zeros(state_shape, F32)
            for dh in dh_refs:
                dh[...] = jnp.zeros(dh.shape, F32)

        pv = [p[...] for p in p_refs]
        iv = [i[...].astype(F32) for i in i_refs]
        hv = [jnp.where(n > 0, h[...].astype(F32), 0.0) for h in h_refs]
        if has_state:
            f = lambda pv_, st_, iv_, hv_: tile_fn(pv_, st_, iv_, hv_)
            _, vjp = jax.vjp(f, pv, sv_ref[0, 0], iv, hv)
            dpv, dst, div, dhv = vjp((dst_ref[...], [d[...].astype(F32) for d in do_refs]))
            dst_ref[...] = dst
        else:
            f = lambda pv_, iv_, hv_: tile_fn(pv_, None, iv_, hv_)[1]
            _, vjp = jax.vjp(f, pv, iv, hv)
            dpv, div, dhv = vjp([d[...].astype(F32) for d in do_refs])
        for j, (di_ref, d) in enumerate(zip(di_refs, div)):
            if use_halo:
                d = jnp.concatenate([d[:r - 8], d[r - 8:] + dh_refs[j][...]], axis=0)
            di_ref[...] = d.astype(di_ref.dtype)
        for dh_ref, d in zip(dh_refs, dhv):
            dh_ref[...] = d
        for dp_ref, d in zip(dp_refs, dpv):
            dp_ref[...] += d

    row = lambda b, nn: (b * nt + nt - 1 - nn, 0)
    hmap = _halo_map(nt, r)
    in_specs = [pl.BlockSpec(p.shape, lambda b, nn: (0, 0)) for p in params]
    in_specs += [pl.BlockSpec((r, a.shape[1]), row) for a in ins]
    if use_halo:
        in_specs += [pl.BlockSpec((8, a.shape[1]), lambda b, nn: hmap(b, nt - 1 - nn)) for a in ins]
    args = list(params) + list(ins) + (list(ins) if use_halo else [])
    scratch = []
    if has_state:
        in_specs.append(pl.BlockSpec((1, 1) + tuple(state_shape), lambda b, nn: (b, nt - 1 - nn, 0, 0)))
        args.append(states)
        scratch.append(pltpu.VMEM(tuple(state_shape), F32))
    in_specs += [pl.BlockSpec((r, d.shape[1]), row) for d in douts]
    args += list(douts)
    if use_halo:
        scratch += [pltpu.VMEM((8, a.shape[1]), F32) for a in ins]
    out_shape = [jax.ShapeDtypeStruct(a.shape, dt) for a, dt in zip(ins, din_dtypes)]
    out_shape += [jax.ShapeDtypeStruct(p.shape, F32) for p in params]
    o_specs = [pl.BlockSpec((r, a.shape[1]), row) for a in ins]
    o_specs += [pl.BlockSpec(p.shape, lambda b, nn: (0, 0)) for p in params]
    if side is not None:
        body, in_specs, o_specs, out_shape, scratch, args = _host_exchange(
            side, body, in_specs, o_specs, out_shape, scratch, args, (nb, nt))
    res = pl.pallas_call(body, grid=(nb, nt), in_specs=in_specs, out_specs=o_specs, out_shape=out_shape,
                         scratch_shapes=scratch, compiler_params=_cparams(("arbitrary", "arbitrary")),
                         name=name)(*args)
    if side is not None:
        return res[:n_i], res[n_i:n_i + n_p], res[n_i + n_p:]
    return res[:n_i], res[n_i:]


def matmul(name, a, b, mode, out_dtype=F32, addend=None, resid=None, tm=512, tn=None, tk=None):
    if mode == "nn":
        (m, kd), (_, n) = a.shape, b.shape
    elif mode == "nt":
        (m, kd), (n, _) = a.shape, b.shape
    else:
        (kd, m), (_, n) = a.shape, b.shape
    tm, tn, tk = min(tm, m if resid is None else resid[2]), min(tn or n, n), min(tk or kd, kd)
    nk = kd // tk
    assert m % tm == 0 and n % tn == 0 and kd % tk == 0
    dims = {"nn": ((1,), (0,)), "nt": ((1,), (1,)), "tn": ((0,), (0,))}[mode]
    extra = [] if addend is None else [addend]
    if resid is not None:
        extra = [resid[0], resid[1]]
    n_in, n_out = 2 + len(extra), 1 if resid is None else 2

    def body(*refs):
        a_ref, b_ref = refs[0], refs[1]
        part = lax.dot_general(a_ref[...].astype(BF16), b_ref[...].astype(BF16), (dims, ((), ())),
                               preferred_element_type=F32)

        def finish(acc):
            if resid is not None:
                refs[n_in][...] = refs[2][...] + refs[3][0] * acc
                refs[n_in + 1][...] = acc.astype(BF16)
            else:
                if addend is not None:
                    acc = acc + refs[2][...]
                refs[n_in][...] = acc.astype(refs[n_in].dtype)

        if nk == 1:
            finish(part)
        else:
            acc_ref = refs[n_in + n_out]
            k = pl.program_id(2)

            @pl.when(k == 0)
            def _():
                acc_ref[...] = part

            @pl.when(k > 0)
            def _():
                acc_ref[...] += part

            @pl.when(k == nk - 1)
            def _():
                finish(acc_ref[...])

    if mode == "tn":
        a_spec = pl.BlockSpec((tk, tm), lambda j, i, k: (k, i))
    else:
        a_spec = pl.BlockSpec((tm, tk), lambda j, i, k: (i, k))
    if mode == "nt":
        b_spec = pl.BlockSpec((tn, tk), lambda j, i, k: (j, k))
    else:
        b_spec = pl.BlockSpec((tk, tn), lambda j, i, k: (k, j))
    o_spec = pl.BlockSpec((tm, tn), lambda j, i, k: (i, j))
    in_specs = [a_spec, b_spec] + [o_spec] * (len(extra) > 0)
    out_specs, out_shape = o_spec, jax.ShapeDtypeStruct((m, n), out_dtype)
    if resid is not None:
        rows = resid[2]
        assert rows % tm == 0
        in_specs.append(pl.BlockSpec((1, 1, tn), lambda j, i, k: (lax.div(i * tm, rows), 0, j)))
        out_specs, out_shape = [o_spec, o_spec], [out_shape, jax.ShapeDtypeStruct((m, n), BF16)]
    scratch = [pltpu.VMEM((tm, tn), F32)] if nk > 1 else []
    return pl.pallas_call(body, grid=(n // tn, m // tm, nk), in_specs=in_specs, out_specs=out_specs,
                          out_shape=out_shape, scratch_shapes=scratch,
                          compiler_params=_cparams(("parallel", "parallel", "arbitrary")), name=name)(a, b, *extra)


def _normmod(x, nw, shift, scale):
    return _rms(x, nw) * (1.0 + scale) + shift


def _row_specs(nb, s, tr, d):
    nt = s // tr
    row = pl.BlockSpec((tr, d), lambda b, i: (b * nt + i, 0))
    per_seq = pl.BlockSpec((1, 1, d), lambda b, i: (b, 0, 0))
    full = pl.BlockSpec((1, d), lambda b, i: (0, 0))
    return nt, row, per_seq, full


def normmod_fwd(name, x, nw, shift, scale, nb, s, tr=512):
    d, tr = x.shape[1], min(tr, s)
    nt, row, per_seq, full = _row_specs(nb, s, tr, d)

    def body(x_ref, nw_ref, sh_ref, sc_ref, h_ref, ht_ref):
        h = _normmod(x_ref[...], nw_ref[...], sh_ref[0], sc_ref[0])
        h_ref[...] = h.astype(h_ref.dtype)
        ht_ref[...] = h.T.astype(ht_ref.dtype)

    return pl.pallas_call(body, grid=(nb, nt), in_specs=[row, full, per_seq, per_seq],
                          out_specs=[row, pl.BlockSpec((d, tr), lambda b, i: (0, b * nt + i))],
                          out_shape=[jax.ShapeDtypeStruct(x.shape, BF16), jax.ShapeDtypeStruct(x.shape[::-1], BF16)],
                          compiler_params=_cparams(("parallel", "parallel")), name=name)(x, nw, shift, scale)


def normmod_bwd(name, x, nw, shift, scale, dh, dres, nb, s, tr=512):
    d, tr = x.shape[1], min(tr, s)
    nt, row, per_seq, full = _row_specs(nb, s, tr, d)

    def body(x_ref, nw_ref, sh_ref, sc_ref, dh_ref, dres_ref, dx_ref, dnw_ref, dsh_ref, dsc_ref):
        b, i = pl.program_id(0), pl.program_id(1)

        @pl.when((b == 0) & (i == 0))
        def _():
            dnw_ref[...] = jnp.zeros(dnw_ref.shape, F32)

        @pl.when(i == 0)
        def _():
            dsh_ref[...] = jnp.zeros(dsh_ref.shape, F32)
            dsc_ref[...] = jnp.zeros(dsc_ref.shape, F32)

        _, vjp = jax.vjp(_normmod, x_ref[...], nw_ref[...], sh_ref[0], sc_ref[0])
        dx, dnw, dsh, dsc = vjp(dh_ref[...])
        dx_ref[...] = dres_ref[...] + dx
        dnw_ref[...] += dnw
        dsh_ref[0] += dsh
        dsc_ref[0] += dsc

    out_shape = [jax.ShapeDtypeStruct(x.shape, F32), jax.ShapeDtypeStruct((1, d), F32),
                 jax.ShapeDtypeStruct((nb, 1, d), F32), jax.ShapeDtypeStruct((nb, 1, d), F32)]
    return pl.pallas_call(body, grid=(nb, nt), in_specs=[row, full, per_seq, per_seq, row, row],
                          out_specs=[row, full, per_seq, per_seq], out_shape=out_shape,
                          compiler_params=_cparams(("arbitrary", "arbitrary")),
                          name=name)(x, nw, shift, scale, dh, dres)


def _merge_specs(nb, s, tr, d, wbr):
    nt, row, per_seq, _ = _row_specs(nb, s, tr, d)
    o_spec = pl.BlockSpec((tr, wbr), lambda b, i: (b * nt + i, 0))
    g_spec = pl.BlockSpec((tr, 3 * d), lambda b, i: (b * nt + i, 0))
    wbr_spec = pl.BlockSpec((wbr, d), lambda b, i: (0, 0))
    wo_spec = pl.BlockSpec((d, d), lambda b, i: (0, 0))
    return nt, row, per_seq, o_spec, g_spec, wbr_spec, wo_spec


def merge_fwd(name, x, oa, ob, oc, pg, gate1, wa, wb, wc, wo, nb, s, tr=512):
    d, tr = x.shape[1], min(tr, s)
    nt, row, per_seq, o_spec, g_spec, wbr_spec, wo_spec = _merge_specs(nb, s, tr, d, oa.shape[1])

    def body(x_ref, oa_ref, ob_ref, oc_ref, pg_ref, g1_ref, wa_ref, wb_ref, wc_ref, wo_ref,
             x1_ref, ya_ref, yb_ref, yc_ref, mg_ref, mx_ref):
        g = _sigmoid(pg_ref[...])
        ys = [_dg(o[...], w_[...], 1, 0) for o, w_ in ((oa_ref, wa_ref), (ob_ref, wb_ref), (oc_ref, wc_ref))]
        merged = g[:, :d] * ys[0] + g[:, d:2 * d] * ys[1] + g[:, 2 * d:] * ys[2]
        mix = _dg(merged, wo_ref[...], 1, 0)
        x1_ref[...] = x_ref[...] + g1_ref[0] * mix
        for r, v in zip((ya_ref, yb_ref, yc_ref, mg_ref, mx_ref), ys + [merged, mix]):
            r[...] = v.astype(r.dtype)

    return pl.pallas_call(body, grid=(nb, nt),
                          in_specs=[row, o_spec, o_spec, o_spec, g_spec, per_seq, wbr_spec, wbr_spec, wbr_spec, wo_spec],
                          out_specs=[row] * 6,
                          out_shape=[jax.ShapeDtypeStruct(x.shape, F32)] + [jax.ShapeDtypeStruct(x.shape, BF16)] * 5,
                          compiler_params=_cparams(("parallel", "parallel")),
                          name=name)(x, oa, ob, oc, pg, gate1, wa, wb, wc, wo)


def merge_bwd(name, oa, ob, oc, pg, gate1, wa, wb, wc, wo, ys, merged, mix, dx1, nb, s, tr=256):
    d, tr = dx1.shape[1], min(tr, s)
    wbr = oa.shape[1]
    nt, row, per_seq, o_spec, g_spec, wbr_spec, wo_spec = _merge_specs(nb, s, tr, d, wbr)

    def body(oa_ref, ob_ref, oc_ref, pg_ref, g1_ref, wa_ref, wb_ref, wc_ref, wo_ref, ya_ref, yb_ref, yc_ref, mg_ref,
             mx_ref, dx_ref, doa_ref, dob_ref, doc_ref, dpg_ref, dg1_ref, dwa_ref, dwb_ref, dwc_ref, dwo_ref):
        b, i = pl.program_id(0), pl.program_id(1)

        @pl.when((b == 0) & (i == 0))
        def _():
            for r in (dwa_ref, dwb_ref, dwc_ref, dwo_ref):
                r[...] = jnp.zeros(r.shape, F32)

        @pl.when(i == 0)
        def _():
            dg1_ref[...] = jnp.zeros(dg1_ref.shape, F32)

        dx = dx_ref[...]
        dg1_ref[0] += jnp.sum(dx * mx_ref[...].astype(F32), axis=0, keepdims=True)
        dmix = g1_ref[0] * dx
        dmerged = _dg(dmix, wo_ref[...], 1, 1)
        dwo_ref[...] += _dg(mg_ref[...], dmix, 0, 0)
        g = _sigmoid(pg_ref[...])
        branches = ((oa_ref, wa_ref, ya_ref, doa_ref, dwa_ref), (ob_ref, wb_ref, yb_ref, dob_ref, dwb_ref),
                    (oc_ref, wc_ref, yc_ref, doc_ref, dwc_ref))
        dgs = []
        for k, (o_ref, w_ref, y_ref, do_ref, dw_ref) in enumerate(branches):
            gk = g[:, k * d:(k + 1) * d]
            dy = dmerged * gk
            dgs.append(dmerged * y_ref[...].astype(F32) * gk * (1.0 - gk))
            do_ref[...] = _dg(dy, w_ref[...], 1, 1)
            dw_ref[...] += _dg(o_ref[...], dy, 0, 0)
        dpg_ref[...] = jnp.concatenate(dgs, axis=1).astype(dpg_ref.dtype)

    t = nb * s
    out_shape = ([jax.ShapeDtypeStruct((t, wbr), F32)] * 3
                 + [jax.ShapeDtypeStruct((t, 3 * d), BF16), jax.ShapeDtypeStruct((nb, 1, d), F32)]
                 + [jax.ShapeDtypeStruct((wbr, d), F32)] * 3 + [jax.ShapeDtypeStruct((d, d), F32)])
    return pl.pallas_call(body, grid=(nb, nt),
                          in_specs=[o_spec, o_spec, o_spec, g_spec, per_seq, wbr_spec, wbr_spec, wbr_spec, wo_spec]
                          + [row] * 6,
                          out_specs=[o_spec, o_spec, o_spec, g_spec, per_seq, wbr_spec, wbr_spec, wbr_spec, wo_spec],
                          out_shape=out_shape, compiler_params=_cparams(("arbitrary", "arbitrary")),
                          name=name)(oa, ob, oc, pg, gate1, wa, wb, wc, wo, *ys, merged, mix, dx1)


def resid_bwd(name, dx, f, gate, nb, s, tr=512):
    d, tr = dx.shape[1], min(tr, s)
    nt, row, per_seq, _ = _row_specs(nb, s, tr, d)

    def body(dx_ref, f_ref, g_ref, df_ref, dg_ref):
        @pl.when(pl.program_id(1) == 0)
        def _():
            dg_ref[...] = jnp.zeros(dg_ref.shape, F32)

        df_ref[...] = (g_ref[0] * dx_ref[...]).astype(df_ref.dtype)
        dg_ref[0] += jnp.sum(dx_ref[...] * f_ref[...], axis=0, keepdims=True)

    return pl.pallas_call(body, grid=(nb, nt), in_specs=[row, row, per_seq], out_specs=[row, per_seq],
                          out_shape=[jax.ShapeDtypeStruct(dx.shape, BF16), jax.ShapeDtypeStruct((nb, 1, d), F32)],
                          compiler_params=_cparams(("arbitrary", "arbitrary")), name=name)(dx, f, gate)


def loss_head(name, x, fw, target, tr=512):
    t, d = x.shape
    row = pl.BlockSpec((tr, d), lambda i: (i, 0))
    full = pl.BlockSpec((1, d), lambda i: (0, 0))

    def loss_fn(xv, fwv, tv):
        err = _rms(xv, fwv) - tv
        return 0.5 * jnp.sum(jnp.mean(err * err, axis=-1))

    def body(x_ref, fw_ref, t_ref, dx_ref, l_ref, dfw_ref):
        @pl.when(pl.program_id(0) == 0)
        def _():
            l_ref[...] = jnp.zeros(l_ref.shape, F32)
            dfw_ref[...] = jnp.zeros(dfw_ref.shape, F32)

        val, (dx, dfw) = jax.value_and_grad(loss_fn, argnums=(0, 1))(x_ref[...], fw_ref[...], t_ref[...])
        dx_ref[...] = dx
        l_ref[...] += val
        dfw_ref[...] += dfw

    return pl.pallas_call(body, grid=(t // tr,), in_specs=[row, full, row],
                          out_specs=[row, pl.BlockSpec((1, 128), lambda i: (0, 0)), full],
                          out_shape=[jax.ShapeDtypeStruct((t, d), F32), jax.ShapeDtypeStruct((1, 128), F32),
                                     jax.ShapeDtypeStruct((1, d), F32)],
                          compiler_params=_cparams(("arbitrary",)), name=name)(x, fw, target)


def _row_tile(rows, cols, n_arrays):
    budget = 24 * 1024 * 1024 // (8 * cols * max(n_arrays, 1))
    tr = rows
    while tr > max(budget, 16) and tr % 2 == 0 and (tr // 2) % 16 == 0:
        tr //= 2
    return tr


def elementwise(name, fn, ins, out_dtypes):
    rows, cols = ins[0].shape
    tr = _row_tile(rows, cols, len(ins) + len(out_dtypes))
    spec = pl.BlockSpec((tr, cols), lambda i: (i, 0))
    n_in = len(ins)

    def body(*refs):
        outs = fn(*[r[...] for r in refs[:n_in]])
        for o_ref, o in zip(refs[n_in:], outs):
            o_ref[...] = o.astype(o_ref.dtype)

    return pl.pallas_call(body, grid=(rows // tr,), in_specs=[spec] * n_in, out_specs=[spec] * len(out_dtypes),
                          out_shape=[jax.ShapeDtypeStruct((rows, cols), dt) for dt in out_dtypes],
                          compiler_params=_cparams(("parallel",)), name=name)(*ins)


def _adamw(w, g, m, v):
    m = ADAM_B1 * m + (1.0 - ADAM_B1) * g
    v = ADAM_B2 * v + (1.0 - ADAM_B2) * (g * g)
    m_hat = m / (1.0 - ADAM_B1 ** ADAM_STEP)
    v_hat = v / (1.0 - ADAM_B2 ** ADAM_STEP)
    delta = -ADAM_LR * (m_hat / (jnp.sqrt(v_hat) + ADAM_EPS) + ADAM_WD * w)
    return delta, m, v


def adamw(name, w, g, m, v):
    return elementwise(name, _adamw, [w, g, m, v], [F32, F32, F32])


_ANY = pl.BlockSpec(memory_space=pl.ANY)


def _coords():
    return lax.axis_index("x"), lax.axis_index("y"), lax.axis_index("c")


def allgather8(name, arrays, halves):
    n = len(arrays)

    def body(*refs):
        in_refs, out_refs = refs[:n], refs[n:2 * n]
        send_sems, recv_sems, local_sems = refs[2 * n:]
        x, y, c = _coords()
        me, sibling = (x, y, c), (x, y, 1 - c)
        chips = [(1 - x, y), (x, 1 - y), (1 - x, 1 - y)]

        def blk(i, px, py, pc):
            return out_refs[i].at[4 * px + 2 * py + pc]

        def piece(i):
            return in_refs[i].at[c] if halves[i] else in_refs[i]

        def copy(i, k, block, to, src=None):
            return pltpu.make_async_remote_copy(
                src_ref=blk(i, *block) if src is None else src, dst_ref=blk(i, *block),
                send_sem=send_sems.at[7 * i + k], recv_sem=recv_sems.at[7 * i + k],
                device_id=to, device_id_type=MESH)

        mine = [pltpu.make_async_copy(piece(i), blk(i, *me), local_sems.at[i]) for i in range(n)]
        for cp in mine:
            cp.start()
        first = []
        for i in range(n):
            first.append(copy(i, 0, me, sibling, src=piece(i)))
            first += [copy(i, 1 + j, me, (*chip, c), src=piece(i)) for j, chip in enumerate(chips)]
        for cp in first:
            cp.start()
        passed = []
        for j, chip in enumerate(chips):
            for i in range(n):
                copy(i, 1 + j, (*chip, c), me).wait_recv()
                fwd = copy(i, 4 + j, (*chip, c), sibling)
                fwd.start()
                passed.append(fwd)
        for i in range(n):
            copy(i, 0, sibling, me).wait_recv()
            for j, chip in enumerate(chips):
                copy(i, 4 + j, (*chip, 1 - c), me).wait_recv()
        for cp in first + passed:
            cp.wait_send()
        for cp in mine:
            cp.wait()

    out_shape = []
    for a, hv in zip(arrays, halves):
        out_shape.append(jax.ShapeDtypeStruct((N_DEV,) + tuple(a.shape[1:] if hv else a.shape), a.dtype))
    return pl.pallas_call(
        body, in_specs=[_ANY] * n, out_specs=[_ANY] * n, out_shape=out_shape,
        scratch_shapes=[pltpu.SemaphoreType.DMA((7 * n,)), pltpu.SemaphoreType.DMA((7 * n,)),
                        pltpu.SemaphoreType.DMA((n,))],
        name=name)(*arrays)


def exchange(name, ins, out_shapes, plan, local_plan=()):
    n_in, n_out = len(ins), len(out_shapes)

    def body(*refs):
        copies = _exchange_copies(plan, local_plan, refs[:n_in], refs[n_in:n_in + n_out], *refs[n_in + n_out:])
        for cp in copies:
            cp.start()
        for cp in copies:
            cp.wait()

    return pl.pallas_call(
        body, in_specs=[_ANY] * n_in, out_specs=[_ANY] * n_out, out_shape=out_shapes,
        scratch_shapes=_exchange_sems(plan, local_plan), name=name)(*ins)


def sum_halves(name, gs, recv, c_arr, chip_arr):
    _, _, hr, cs = gs.shape
    tr = _row_tile(hr, cs, 4)

    def body(c_ref, chip_ref, g_ref, r_ref, qf_ref, qb_ref):
        q = g_ref[0, 0] + r_ref[0, 0]
        qb_ref[0] = q.astype(BF16)

        @pl.when(pl.program_id(1) == chip_ref[0])
        def _():
            qf_ref[...] = q

    grid_spec = pltpu.PrefetchScalarGridSpec(
        num_scalar_prefetch=2, grid=(hr // tr, N_CHIPS),
        in_specs=[pl.BlockSpec((1, 1, tr, cs), lambda i, j, c_ref, chip_ref: (j, c_ref[0], i, 0)),
                  pl.BlockSpec((1, 1, tr, cs), lambda i, j, c_ref, chip_ref: (j, 0, i, 0))],
        out_specs=[pl.BlockSpec((tr, cs), lambda i, j, c_ref, chip_ref: (i, 0)),
                   pl.BlockSpec((1, tr, cs), lambda i, j, c_ref, chip_ref: (j, i, 0))])
    return pl.pallas_call(body, grid_spec=grid_spec,
                          out_shape=[jax.ShapeDtypeStruct((hr, cs), F32),
                                     jax.ShapeDtypeStruct((N_CHIPS, hr, cs), BF16)],
                          compiler_params=_cparams(("parallel", "arbitrary")), name=name)(c_arr, chip_arr, gs, recv)


def sum_chips(name, qf, recv):
    (total,) = elementwise(name, lambda q, a, b, c: (q + a.astype(F32) + b.astype(F32) + c.astype(F32),),
                           [qf] + list(recv), [F32])
    return total


def adamw_halves(name, w, m, v, layer, g_mine, g_other, c_arr, prev=None):
    _, _, hr, cs = w.shape
    tr = _row_tile(hr, cs, 9)

    def body(c_ref, w_ref, m_ref, v_ref, gm_ref, go_ref, *rest):
        g_ref, d_ref, nm_ref, nv_ref = rest[-4:]
        g = jnp.where(pl.program_id(0) == c_ref[0], gm_ref[...], go_ref[...])
        delta, nm, nv = _adamw(w_ref[0, 0], g, m_ref[0, 0], v_ref[0, 0])
        g_ref[0, 0], d_ref[0, 0], nm_ref[0, 0], nv_ref[0, 0] = g, delta, nm, nv

    half = pl.BlockSpec((1, 1, tr, cs), lambda h, i, c_ref: (layer, h, i, 0))
    row = pl.BlockSpec((tr, cs), lambda h, i, c_ref: (i, 0))
    in_specs, args, aliases = [half, half, half, row, row], [c_arr, w, m, v, g_mine, g_other], {}
    if prev is not None:
        in_specs += [pl.BlockSpec(memory_space=pl.ANY)] * 4
        args += list(prev)
        aliases = {6 + k: k for k in range(4)}
    grid_spec = pltpu.PrefetchScalarGridSpec(num_scalar_prefetch=1, grid=(2, hr // tr),
                                             in_specs=in_specs, out_specs=[half] * 4)
    return pl.pallas_call(body, grid_spec=grid_spec, out_shape=[jax.ShapeDtypeStruct(w.shape, F32)] * 4,
                          input_output_aliases=aliases, compiler_params=_cparams(("parallel", "parallel")),
                          name=name)(*args)


def sum8(name, g):
    _, rows, cols = g.shape
    tr = _row_tile(rows, cols, 9)

    def body(*refs):
        acc = refs[0][0]
        for r in refs[1:N_DEV]:
            acc = acc + r[0]
        refs[N_DEV][...] = acc

    in_specs = [pl.BlockSpec((1, tr, cols), functools.partial(lambda k, i: (k, i, 0), k)) for k in range(N_DEV)]
    return pl.pallas_call(body, grid=(rows // tr,), in_specs=in_specs,
                          out_specs=pl.BlockSpec((tr, cols), lambda i: (i, 0)),
                          out_shape=jax.ShapeDtypeStruct((rows, cols), F32),
                          compiler_params=_cparams(("parallel",)), name=name)(*([g] * N_DEV))


_QKV, _AB, _GZ = (0, 1536), (1536, 1544), (1544, 2056)
_HG = (2056, 4104)
_SZ, _XBC, _DT = (4104, 4616), (4616, 5640), (5640, 5648)
_GATES = (5648, 8720)


def _split_w_in(w8):
    _, hr, cs = w8.shape
    w4 = w8.reshape(N_CHIPS, 2 * hr, cs)

    def cols(rng):
        lo, hi = rng
        return [w4[j][:, max(lo, j * cs) - j * cs:min(hi, (j + 1) * cs) - j * cs]
                for j in range(N_CHIPS) if max(lo, j * cs) < min(hi, (j + 1) * cs)]

    pad = [jnp.zeros((2 * hr, 120), w8.dtype)]
    return (jnp.concatenate(cols(_GATES), axis=1),
            jnp.concatenate(cols(_QKV) + cols(_GZ) + cols(_AB) + pad, axis=1),
            jnp.concatenate(cols(_HG), axis=1),
            jnp.concatenate(cols(_SZ) + cols(_XBC) + cols(_DT) + pad, axis=1))


def _stack_w_in(g, a, b, c):
    segments = [(a, 0, 1536), (a, 2048, 2056), (a, 1536, 2048), (b, 0, 2048), (c, 0, 512), (c, 512, 1536),
                (c, 1536, 1544), (g, 0, 3072)]
    cs = sum(s1 - s0 for _, s0, s1 in segments) // N_CHIPS
    chips = []
    for j in range(N_CHIPS):
        parts, off = [], 0
        for arr, s0, s1 in segments:
            u0, u1 = max(j * cs, off), min((j + 1) * cs, off + s1 - s0)
            if u0 < u1:
                parts.append(arr[:, s0 + u0 - off:s0 + u1 - off])
            off += s1 - s0
        chips.append(jnp.concatenate(parts, axis=1))
    rows = g.shape[0]
    return jnp.stack(chips).reshape(N_CHIPS, 2, rows // 2, cs)


def _rows8(rows, width):
    out = [jnp.pad(r.astype(F32), (0, width - r.shape[0])) for r in rows]
    out += [jnp.zeros((width,), F32)] * (8 - len(out))
    return jnp.stack(out)


class _Packer:
    def __init__(self):
        self.items, self.size = [], 0

    def add(self, name, shape):
        n = 1
        for d in shape:
            n *= d
        self.items.append((name, tuple(shape), self.size, n))
        self.size += n

    def rows(self):
        return -(-self.size // 8192) * 8

    def pack(self, values):
        flat = [values[name].astype(F32).reshape(-1) for name, _, _, _ in self.items]
        flat.append(jnp.zeros((self.rows() * 1024 - self.size,), F32))
        return jnp.concatenate(flat).reshape(self.rows(), 1024)

    def unpack(self, buf):
        flat = buf.reshape(-1)
        return {name: flat[off:off + n].reshape(shape) for name, shape, off, n in self.items}


def _stack_by_chip(g, axis):
    l, r, c = g.shape
    if axis == 2:
        cs = c // N_CHIPS
        g = g.reshape(l, r, N_CHIPS, cs).transpose(2, 0, 1, 3).reshape(N_CHIPS, 2, l * r // 2, cs)
    else:
        rs = r // N_CHIPS
        g = g.reshape(l, N_CHIPS, rs, c).transpose(1, 0, 2, 3).reshape(N_CHIPS, 2, l * rs // 2, c)
    return g


def _unstack_gathered(w8, l, axis):
    _, hr, cs = w8.shape
    w = w8.reshape(N_CHIPS, l, 2 * hr // l, cs)
    if axis == 2:
        return w.transpose(1, 2, 0, 3).reshape(l, 2 * hr // l, N_CHIPS * cs)
    return w.transpose(1, 0, 2, 3).reshape(l, N_CHIPS * 2 * hr // l, cs)


_BIG = (("w_in", 2), ("w_br_a", 2), ("w_br_b", 2), ("w_br_c", 2), ("w_out", 1), ("ffn_w_up", 2), ("ffn_w_down", 1))
_SMALL = ("b_ada", "norm1_w", "gdn_conv_w", "gdn_a_log", "gdn_dt_bias", "gdn_norm_w", "hgrn_lb_param",
          "hgrn_norm_w", "ssd_conv_w", "ssd_conv_b", "ssd_a_log", "ssd_dt_bias", "ssd_d", "ssd_norm_w",
          "norm2_w", "ffn_conv_w", "ffn_conv_b", "final_norm_w")
_WEIGHTS = ("w_ada", "b_ada", "norm1_w", "w_in", "gdn_conv_w", "gdn_a_log", "gdn_dt_bias", "gdn_norm_w",
            "hgrn_lb_param", "hgrn_norm_w", "ssd_conv_w", "ssd_conv_b", "ssd_a_log", "ssd_dt_bias", "ssd_d",
            "ssd_norm_w", "w_br_a", "w_br_b", "w_br_c", "w_out", "norm2_w", "ffn_w_up", "ffn_conv_w",
            "ffn_conv_b", "ffn_w_down", "final_norm_w")
_R_GDN, _R_HGRN, _R_SSD, _R_FFN = 256, 128, 256, 256


_MASKS = ((1, 0), (0, 1), (1, 1))


def _flip(k, x, y):
    return (1 - x if _MASKS[k][0] else x), (1 - y if _MASKS[k][1] else y)


def _rs_d2d(grads):
    plan = [functools.partial(lambda i, ins, outs, x, y, c: (ins[i].at[:, pl.ds(1 - c, 1)], outs[i], (x, y, 1 - c)), i)
            for i in range(len(grads))]
    return grads, [jax.ShapeDtypeStruct((N_CHIPS, 1) + g.shape[2:], F32) for g in grads], plan, (), ()


def _rs_ici(grads, recv, tag):
    n = len(grads)
    c_arr = lax.axis_index("c").astype(jnp.int32).reshape(1)
    chip_arr = (2 * lax.axis_index("x") + lax.axis_index("y")).astype(jnp.int32).reshape(1)
    q = [sum_halves("rs_sum_d2d%s_%d" % (tag, i), g, r, c_arr, chip_arr) for i, (g, r) in enumerate(zip(grads, recv))]
    qf, qb = [a for a, _ in q], [b for _, b in q]

    def ici(i, k, ins, outs, x, y, c):
        px, py = _flip(k, x, y)
        return ins[i].at[2 * px + py], outs[3 * i + k], (px, py, c)

    plan = [functools.partial(ici, i, k) for i in range(n) for k in range(3)]
    shapes = [jax.ShapeDtypeStruct(g.shape[2:], BF16) for g in grads for _ in range(3)]
    return qf, (qb, shapes, plan, (), ())


def _rs_finish(qf, res, tag):
    n = len(qf)
    red = [sum_chips("rs_sum_ici%s_%d" % (tag, i), qf[i], res[3 * i:3 * i + 3]) for i in range(n)]
    plan = [functools.partial(lambda i, ins, outs, x, y, c: (ins[i], outs[i], (x, y, 1 - c)), i) for i in range(n)]
    other = exchange("rs_swap" + tag, red, [jax.ShapeDtypeStruct(r.shape, F32) for r in red], plan)
    return red, other


def _join_sides(sides):
    ins, shapes, plan = [], [], []
    for s_ins, s_shapes, s_plan, _, _ in sides:
        def shifted(fn, i0, i1, o0, o1, in_refs, out_refs, x, y, c):
            return fn(in_refs[i0:i1], out_refs[o0:o1], x, y, c)

        i0, o0 = len(ins), len(shapes)
        plan += [functools.partial(shifted, fn, i0, i0 + len(s_ins), o0, o0 + len(s_shapes)) for fn in s_plan]
        ins += list(s_ins)
        shapes += list(s_shapes)
    return ins, shapes, plan, (), ()


def _gather_side(pieces):
    n = len(pieces)

    def send(i, k, ins, outs, x, y, c):
        px, py = _flip(k, x, y)
        return ins[i].at[c], outs[i].at[2 * (2 * x + y) + c], (px, py, c)

    def to_sibling(i, h, ins, outs, x, y, c):
        return ins[i].at[h], outs[i].at[2 * (2 * x + y) + h], (x, y, 1 - c)

    def pass_on(i, k, ins, outs, x, y, c):
        px, py = _flip(k, x, y)
        blk = 2 * (2 * px + py) + c
        return outs[i].at[blk], outs[i].at[blk], (x, y, 1 - c)

    plan = [functools.partial(send, i, k) for i in range(n) for k in range(3)]
    plan += [functools.partial(to_sibling, i, h) for i in range(n) for h in range(2)]
    then = [functools.partial(pass_on, i, k) for i in range(n) for k in range(3)]
    shapes = [jax.ShapeDtypeStruct((N_DEV,) + p.shape[1:], p.dtype) for p in pieces]
    return pieces, shapes, plan, (), then


def kernel(x, c, w_ada, b_ada, norm1_w, w_in, gdn_conv_w, gdn_a_log, gdn_dt_bias, gdn_norm_w, hgrn_lb_param, hgrn_norm_w, ssd_conv_w, ssd_conv_b, ssd_a_log, ssd_dt_bias, ssd_d, ssd_norm_w, w_br_a, w_br_b, w_br_c, w_out, norm2_w, ffn_w_up, ffn_conv_w, ffn_conv_b, ffn_w_down, final_norm_w, loss_target, m_w_ada, m_b_ada, m_norm1_w, m_w_in, m_gdn_conv_w, m_gdn_a_log, m_gdn_dt_bias, m_gdn_norm_w, m_hgrn_lb_param, m_hgrn_norm_w, m_ssd_conv_w, m_ssd_conv_b, m_ssd_a_log, m_ssd_dt_bias, m_ssd_d, m_ssd_norm_w, m_w_br_a, m_w_br_b, m_w_br_c, m_w_out, m_norm2_w, m_ffn_w_up, m_ffn_conv_w, m_ffn_conv_b, m_ffn_w_down, m_final_norm_w, v_w_ada, v_b_ada, v_norm1_w, v_w_in, v_gdn_conv_w, v_gdn_a_log, v_gdn_dt_bias, v_gdn_norm_w, v_hgrn_lb_param, v_hgrn_norm_w, v_ssd_conv_w, v_ssd_conv_b, v_ssd_a_log, v_ssd_dt_bias, v_ssd_d, v_ssd_norm_w, v_w_br_a, v_w_br_b, v_w_br_c, v_w_out, v_norm2_w, v_ffn_w_up, v_ffn_conv_w, v_ffn_conv_b, v_ffn_w_down, v_final_norm_w):
    loc = dict(locals())
    w = {k: loc[k] for k in _WEIGHTS}
    mom = {k: loc["m_" + k] for k in _WEIGHTS}
    var = {k: loc["v_" + k] for k in _WEIGHTS}
    nb, s, d = x.shape
    t = nb * s
    depth = w_ada.shape[0]
    chip = 2 * lax.axis_index("x") + lax.axis_index("y")
    dev = 2 * chip + lax.axis_index("c")
    x0 = x.reshape(t, d)
    target = loss_target.reshape(t, d)

    small_in = [c, gdn_conv_w.reshape(depth * 4, -1), ssd_conv_w.reshape(depth * 4, -1),
                ffn_conv_w.reshape(depth * 3, -1)]
    c_all, gcw, scw, fcw = allgather8("ag_small", small_in, [False] * 4)
    c_all = c_all.reshape(N_DEV * nb, d)

    def conv_full(g, taps):
        g = g[::2].reshape(N_CHIPS, depth, taps, -1)
        return g.transpose(1, 2, 0, 3).reshape(depth, taps, -1)

    gdn_cw, ssd_cw, ffn_cw = conv_full(gcw, 4), conv_full(scw, 4), conv_full(fcw, 3)

    axis_of = dict(_BIG)
    first_needed, later = ("w_in",), tuple(n for n, _ in _BIG if n != "w_in")
    wls = [dict() for _ in range(depth)]

    def pieces(keys):
        out = []
        for l, name in keys:
            a = w[name][l].astype(BF16)
            out.append(a.reshape(2, a.shape[0] // 2, a.shape[1]))
        return out

    def arrived(keys, bufs):
        for (l, name), g in zip(keys, bufs):
            if name == "w_in":
                wls[l]["w_g"], wls[l]["w_a"], wls[l]["w_b"], wls[l]["w_c"] = _split_w_in(g)
            else:
                wls[l][name] = _unstack_gathered(g, 1, axis_of[name])[0]

    keys0 = [(0, n) for n in first_needed]
    arrived(keys0, allgather8("ag_weights0", pieces(keys0), [True] * len(keys0)))

    (c_act,) = elementwise("silu_c", lambda v: (_silu(v),), [c_all], [F32])
    mod_cols = jnp.concatenate([matmul("ada_fwd%d" % l, c_act, w_ada[l], "nn") for l in range(depth)], axis=0)
    (mod8,) = allgather8("ag_mod", [mod_cols], [False])
    mod = mod8[::2].reshape(N_CHIPS, depth, N_DEV * nb, -1).transpose(1, 2, 0, 3).reshape(depth, N_DEV * nb, 6 * d)
    mod = lax.dynamic_slice_in_dim(mod, dev * nb, nb, axis=1) + b_ada[:, None, :]

    def mod_part(l, k):
        return mod[l, :, k * d:(k + 1) * d].reshape(nb, 1, d)

    saved = []
    xl = x0
    for l in range(depth):
        sfx = str(l)
        wl = wls[l]
        sv = {"x0": xl}
        shift1, scale1, gate1, shift2, scale2, gate2 = [mod_part(l, k) for k in range(6)]
        sv["mods"] = (shift1, scale1, gate1, shift2, scale2, gate2)
        h, h_t = normmod_fwd("norm1_fwd" + sfx, xl, norm1_w[l][None], shift1, scale1, nb, s)
        pg = matmul("proj_g" + sfx, h, wl["w_g"], "nn")
        pa = matmul("proj_a" + sfx, h, wl["w_a"], "nn")
        pb = matmul("proj_b" + sfx, h, wl["w_b"], "nn")
        pc = matmul("proj_c" + sfx, h, wl["w_c"], "nn")
        gdn_p = [_rows8(list(gdn_cw[l]), 1536), _rows8([gdn_a_log[l], gdn_dt_bias[l], gdn_norm_w[l]], 128)]
        hgrn_p = [_rows8(list(hgrn_lb_param), 512), _rows8([hgrn_norm_w[l]], 128)]
        ssd_p = [_rows8(list(ssd_cw[l]), 1024), _rows8([ssd_conv_b[l], ssd_norm_w[l]], 1024),
                 _rows8([ssd_a_log[l], ssd_dt_bias[l], ssd_d[l]], 128)]
        ffn_p = [_rows8(list(ffn_cw[l]) + [ffn_conv_b[l]], 2 * FFN_HIDDEN)]
        hgrn_fn = make_hgrn_tile(l, depth)
        keys = [(l, n) for n in later] + ([(l + 1, n) for n in first_needed] if l + 1 < depth else [])
        oa, st_a, *bufs = seq_fwd("gdn_fwd" + sfx, gdn_tile, gdn_p, [pa], True, [(512, BF16)], (512, 128), nb, s,
                                  _R_GDN, side=_gather_side(pieces(keys)))
        arrived(keys, bufs)
        ob, st_b = seq_fwd("hgrn_fwd" + sfx, hgrn_fn, hgrn_p, [pb], False, [(512, BF16)], (512, 128), nb, s, _R_HGRN)
        oc, st_c = seq_fwd("ssd_fwd" + sfx, ssd_tile, ssd_p, [pc], True, [(512, BF16)], (256, 256), nb, s, _R_SSD)
        x1, *merge_saved = merge_fwd("merge_fwd" + sfx, xl, oa, ob, oc, pg, gate1, wl["w_br_a"], wl["w_br_b"],
                                     wl["w_br_c"], wl["w_out"], nb, s)
        h2, h2_t = normmod_fwd("norm2_fwd" + sfx, x1, norm2_w[l][None], shift2, scale2, nb, s)
        u = matmul("ffn_up" + sfx, h2, wl["ffn_w_up"], "nn", tn=FFN_HIDDEN)
        act, act_t = seq_fwd("convglu_fwd" + sfx, convglu_tile_t, ffn_p, [u], True,
                             [(FFN_HIDDEN, BF16), (FFN_HIDDEN, BF16, "T")], None, nb, s, _R_FFN)
        xl, f = matmul("ffn_down" + sfx, act, wl["ffn_w_down"], "nn", resid=(x1, gate2, s))
        sv.update(merge_saved=merge_saved, h_t=h_t, h2_t=h2_t, act_t=act_t, pg=pg, pa=pa, pb=pb, pc=pc, oa=oa, ob=ob, oc=oc, st_a=st_a, st_b=st_b, st_c=st_c, x1=x1,
                  u=u, f=f, gdn_p=gdn_p, hgrn_p=hgrn_p, ssd_p=ssd_p, ffn_p=ffn_p, hgrn_fn=hgrn_fn)
        saved.append(sv)

    dx, loss_part, d_final = loss_head("loss_head", xl, final_norm_w[None], target)

    sg = {}
    dmod = [None] * depth
    d_lb = None
    reduced = {}
    early = [n for n, _ in _BIG if n != "w_in"]
    to_d2d = to_ici = None

    def finish(keys, qf, res, tag):
        for key, mine, other in zip(keys, *_rs_finish(qf, res, tag)):
            reduced[key] = (mine, other)
    for l in reversed(range(depth)):
        sfx = str(l)
        sv, wl = saved[l], wls[l]
        gfull = {}
        shift1, scale1, gate1, shift2, scale2, gate2 = sv["mods"]
        df, dgate2 = resid_bwd("resid_bwd" + sfx, dx, sv["f"], gate2, nb, s)
        dact = matmul("ffn_down_dx" + sfx, df, wl["ffn_w_down"], "nt")
        gfull["ffn_w_down"] = matmul("ffn_down_dw" + sfx, sv["act_t"], df, "nn", tm=1408, tn=512, tk=4096)
        cg_args = ("convglu_bwd" + sfx, convglu_tile, sv["ffn_p"], [sv["u"]], True, None, [dact], [BF16], None, nb, s,
                   _R_FFN)
        if to_d2d is None:
            (du,), (dcw,) = seq_bwd(*cg_args)
        else:
            lp, stacked = to_d2d
            (du,), (dcw,), recv = seq_bwd(*cg_args, side=_rs_d2d(stacked))
            to_ici = ([(lp, "w_in")],) + _rs_ici(stacked, recv, "i%d" % lp) + ("i%d" % lp,)
        dh2 = matmul("ffn_up_dx" + sfx, du, wl["ffn_w_up"], "nt")
        gfull["ffn_w_up"] = matmul("ffn_up_dw" + sfx, sv["h2_t"], du, "nn", tm=1024, tn=512, tk=4096)
        dx1, dnw2, dshift2, dscale2 = normmod_bwd("norm2_bwd" + sfx, sv["x1"], norm2_w[l][None], shift2, scale2, dh2, dx,
                                                  nb, s)
        doa, dob, doc, dpg, dgate1, dwa, dwb, dwc, dwo = merge_bwd(
            "merge_bwd" + sfx, sv["oa"], sv["ob"], sv["oc"], sv["pg"], gate1, wl["w_br_a"], wl["w_br_b"],
            wl["w_br_c"], wl["w_out"], sv["merge_saved"][:3], *sv["merge_saved"][3:], dx1, nb, s)
        gfull["w_br_a"], gfull["w_br_b"], gfull["w_br_c"], gfull["w_out"] = dwa, dwb, dwc, dwo
        gdn_args = ("gdn_bwd" + sfx, gdn_tile, sv["gdn_p"], [sv["pa"]], True, sv["st_a"], [doa], [BF16], (512, 128),
                    nb, s, _R_GDN)
        stacked = [_stack_by_chip(gfull[n][None], axis_of[n]) for n in early]
        (dpc,), (dscw, dspv, dsps), recv = seq_bwd("ssd_bwd" + sfx, ssd_tile, sv["ssd_p"], [sv["pc"]], True, sv["st_c"],
                                                   [doc], [BF16], (256, 256), nb, s, _R_SSD, side=_rs_d2d(stacked))
        hosted = [([(l, n) for n in early],) + _rs_ici(stacked, recv, "e" + sfx) + ("e" + sfx,)]
        if to_ici is not None:
            hosted.append(to_ici)
        (dpa,), (dgcw, dgpk), res = seq_bwd(*gdn_args, side=_join_sides([h[2] for h in hosted]))
        for keys, qf, side, tag in hosted:
            finish(keys, qf, res[:len(side[1])], tag)
            res = res[len(side[1]):]
        (dpb,), (dlbp, dhnw) = seq_bwd("hgrn_bwd" + sfx, sv["hgrn_fn"], sv["hgrn_p"], [sv["pb"]], False, sv["st_b"],
                                       [dob], [BF16], (512, 128), nb, s, _R_HGRN)
        dh = matmul("proj_g_dx" + sfx, dpg, wl["w_g"], "nt")
        dh = matmul("proj_a_dx" + sfx, dpa, wl["w_a"], "nt", addend=dh)
        dh = matmul("proj_b_dx" + sfx, dpb, wl["w_b"], "nt", addend=dh)
        dh = matmul("proj_c_dx" + sfx, dpc, wl["w_c"], "nt", addend=dh)
        stacked_w_in = _stack_w_in(
            matmul("proj_g_dw" + sfx, sv["h_t"], dpg, "nn", tm=1024, tn=512, tk=4096),
            matmul("proj_a_dw" + sfx, sv["h_t"], dpa, "nn", tm=1024, tk=1024),
            matmul("proj_b_dw" + sfx, sv["h_t"], dpb, "nn", tm=1024, tn=512, tk=4096),
            matmul("proj_c_dw" + sfx, sv["h_t"], dpc, "nn", tm=1024, tk=1024))
        dx, dnw1, dshift1, dscale1 = normmod_bwd("norm1_bwd" + sfx, sv["x0"], norm1_w[l][None], shift1, scale1, dh, dx1,
                                                 nb, s)
        dmod[l] = jnp.concatenate([dshift1, dscale1, dgate1, dshift2, dscale2, dgate2], axis=-1).reshape(nb, 6 * d)
        d_lb = dlbp[:depth] if d_lb is None else d_lb + dlbp[:depth]
        sg[l] = dict(norm1_w=dnw1[0], norm2_w=dnw2[0], gdn_conv_w=dgcw[:4], gdn_a_log=dgpk[0, :4],
                     gdn_dt_bias=dgpk[1, :4], gdn_norm_w=dgpk[2], hgrn_norm_w=dhnw[0], ssd_conv_w=dscw[:4],
                     ssd_conv_b=dspv[0], ssd_norm_w=dspv[1, :512], ssd_a_log=dsps[0, :8], ssd_dt_bias=dsps[1, :8],
                     ssd_d=dsps[2, :8], ffn_conv_w=dcw[:3], ffn_conv_b=dcw[3])
        to_d2d = (l, [stacked_w_in])
    lp, stacked = to_d2d
    recv = exchange("rs_d2d_i%d" % lp, *_rs_d2d(stacked)[:4])
    qf, side = _rs_ici(stacked, recv, "i%d" % lp)
    finish([(lp, "w_in")], qf, exchange("rs_ici_i%d" % lp, *side[:4]), "i%d" % lp)
    grad_x = dx.reshape(nb, s, d)

    dmod = jnp.stack(dmod)
    (b_sum,) = elementwise("bias_rows", lambda *r: (functools.reduce(lambda p, q: p + q, r),),
                           [dmod[:, b].reshape(depth * 6, d) for b in range(nb)], [F32])
    per_layer = ("norm1_w", "norm2_w", "gdn_conv_w", "gdn_a_log", "gdn_dt_bias", "gdn_norm_w", "hgrn_norm_w",
                 "ssd_conv_w", "ssd_conv_b", "ssd_norm_w", "ssd_a_log", "ssd_dt_bias", "ssd_d", "ffn_conv_w", "ffn_conv_b")
    vals = {k: jnp.stack([sg[l][k] for l in range(depth)]) for k in per_layer}
    vals.update(loss=loss_part[0, :1], b_ada=b_sum.reshape(depth, 6 * d), hgrn_lb_param=d_lb, final_norm_w=d_final[0])
    gp = _Packer()
    for k, v in vals.items():
        gp.add(k, v.shape)
    packed8, dmod8 = allgather8("ag_grads", [gp.pack(vals), dmod.reshape(depth * nb, 6 * d)], [False, False])
    gs = gp.unpack(sum8("sum_small", packed8))
    loss = gs["loss"].reshape(())

    def my_cols(g):
        cs = g.shape[-1] // N_CHIPS
        return lax.dynamic_slice_in_dim(g, chip * cs, cs, axis=g.ndim - 1)

    for k in ("gdn_conv_w", "ssd_conv_w", "ffn_conv_w"):
        gs[k] = my_cols(gs[k])

    dmod_all = dmod8.reshape(N_DEV, depth, nb, 6 * d).transpose(1, 0, 2, 3).reshape(depth, N_DEV * nb, 6 * d)
    dmod_mine = lax.dynamic_slice_in_dim(dmod_all, chip * (6 * d // N_CHIPS), 6 * d // N_CHIPS, axis=2)
    g_w_ada = jnp.stack([matmul("ada_dw%d" % l, c_act, dmod_mine[l], "tn", tm=1024) for l in range(depth)])

    c_arr = lax.axis_index("c").astype(jnp.int32).reshape(1)
    grads, delta, new_m, new_v = {}, {}, {}, {}
    for i, (name, _) in enumerate(_BIG):
        shp = w[name].shape
        halves = lambda a: a.reshape((depth, 2) + reduced[(0, name)][0].shape)
        res = None
        for l in reversed(range(depth)):
            res = adamw_halves("adamw_%s%d" % (name, l), halves(w[name]), halves(mom[name]), halves(var[name]), l,
                               *reduced[(l, name)], c_arr, prev=res)
        grads[name], delta[name], new_m[name], new_v[name] = [r.reshape(shp) for r in res]
    grads["w_ada"] = g_w_ada
    for k in _SMALL:
        grads[k] = gs[k].reshape(w[k].shape)
    shp = w_ada.shape
    flat = lambda a: a.reshape(shp[0] * shp[1], shp[2])
    dl, nm, nv = adamw("adamw_w_ada", flat(w_ada), flat(g_w_ada), flat(m_w_ada), flat(v_w_ada))
    delta["w_ada"], new_m["w_ada"], new_v["w_ada"] = dl.reshape(shp), nm.reshape(shp), nv.reshape(shp)
    sp = _Packer()
    for k in _SMALL:
        sp.add(k, w[k].shape)
    dl, nm, nv = adamw("adamw_small", sp.pack(w), sp.pack(grads), sp.pack(mom), sp.pack(var))
    delta.update(sp.unpack(dl))
    new_m.update(sp.unpack(nm))
    new_v.update(sp.unpack(nv))

    return (loss, grad_x, *[grads[k] for k in _WEIGHTS], *[delta[k] for k in _WEIGHTS],
            *[new_m[k] for k in _WEIGHTS], *[new_v[k] for k in _WEIGHTS])
```

```python
import functools

import jax
import jax.numpy as jnp
from jax import lax
from jax.experimental import pallas as pl
from jax.experimental.pallas import tpu as pltpu

F32 = jnp.float32
BF16 = jnp.bfloat16
HI = lax.Precision.HIGHEST
MESH = pl.DeviceIdType.MESH

EPS = 1e-6
D_MODEL = 1024
GDN_HEADS, GDN_DK, GDN_CHUNK = 4, 128, 64
HGRN_HEADS, HGRN_DK, HGRN_CHUNK = 4, 128, 16
SSD_HEADS, SSD_P, SSD_GROUPS, SSD_STATE, SSD_CHUNK = 8, 64, 2, 128, 64
FFN_HIDDEN = 2816
N_CHIPS = 4
N_DEV = 8

ADAM_LR, ADAM_B1, ADAM_B2, ADAM_EPS, ADAM_WD, ADAM_STEP = 0.001, 0.9, 0.999, 1e-08, 0.01, 10

W_G, W_A, W_B, W_C = 3072, 2176, 2048, 1664
VMEM_LIMIT = 56 * 1024 * 1024


def _cparams(sem):
    return pltpu.CompilerParams(dimension_semantics=sem, vmem_limit_bytes=VMEM_LIMIT)


def _dg(a, b, ca, cb):
    return lax.dot_general(a.astype(BF16), b.astype(BF16), (((ca,), (cb,)), ((), ())),
                           preferred_element_type=F32)


@jax.custom_vjp
def bdot(a, b):
    return _dg(a, b, 1, 0)


bdot.defvjp(lambda a, b: (_dg(a, b, 1, 0), (a, b)),
            lambda r, g: (_dg(g, r[1], 1, 1), _dg(r[0], g, 0, 0)))


@jax.custom_vjp
def bdot_nt(a, b):
    return _dg(a, b, 1, 1)


bdot_nt.defvjp(lambda a, b: (_dg(a, b, 1, 1), (a, b)),
               lambda r, g: (_dg(g, r[1], 1, 0), _dg(g, r[0], 0, 0)))


@jax.custom_vjp
def bdot_tn(a, b):
    return _dg(a, b, 0, 0)


bdot_tn.defvjp(lambda a, b: (_dg(a, b, 0, 0), (a, b)),
               lambda r, g: (_dg(r[1], g, 1, 1), _dg(r[0], g, 1, 0)))


def _split(x, n):
    parts, rest = [], x
    for _ in range(n):
        p = rest.astype(BF16)
        parts.append(p)
        rest = rest - p.astype(F32)
    return parts


def _dgb(a, b, ca, cb):
    return lax.dot_general(a, b, (((ca,), (cb,)), ((), ())), preferred_element_type=F32)


def _dg3(a, b, ca, cb):
    (ah, al), (bh, bl) = _split(a, 2), _split(b, 2)
    return _dgb(jnp.concatenate([ah, ah, al], axis=ca), jnp.concatenate([bh, bl, bh], axis=cb), ca, cb)


@jax.custom_vjp
def hdot(a, b):
    return _dg3(a, b, 1, 0)


hdot.defvjp(lambda a, b: (_dg3(a, b, 1, 0), (a, b)),
            lambda r, g: (_dg3(g, r[1], 1, 1), _dg3(r[0], g, 0, 0)))


def _dge(e, x, ce, cx, e_first):
    eb = e.astype(BF16)
    es = jnp.concatenate([eb, eb, eb], axis=ce)
    xs = jnp.concatenate(_split(x, 3), axis=cx)
    return _dgb(es, xs, ce, cx) if e_first else _dgb(xs, es, cx, ce)


@jax.custom_vjp
def ldot(e, x):
    return _dge(e, x, 1, 0, True)


ldot.defvjp(lambda e, x: (_dge(e, x, 1, 0, True), e),
            lambda e, g: (jnp.zeros_like(e), _dge(e, g, 0, 0, True)))


@jax.custom_vjp
def rdot(x, e):
    return _dge(e, x, 0, 1, False)


rdot.defvjp(lambda x, e: (_dge(e, x, 0, 1, False), e),
            lambda e, g: (_dge(e, g, 1, 1, False), jnp.zeros_like(e)))


@jax.custom_vjp
def _sigmoid(x):
    return 1.0 / (1.0 + jnp.exp(-x))


def _sigmoid_fwd(x):
    g = 1.0 / (1.0 + jnp.exp(-x))
    return g, g


_sigmoid.defvjp(_sigmoid_fwd, lambda g, ct: (ct * g * (1.0 - g),))


def _silu(x):
    return x * _sigmoid(x)


def _softplus(x):
    return jnp.maximum(x, 0.0) + jnp.log(1.0 + jnp.exp(-jnp.abs(x)))


def _rms(x, w):
    return x * lax.rsqrt(jnp.mean(x * x, axis=-1, keepdims=True) + EPS) * w


def _iota(shape, dim):
    return lax.broadcasted_iota(jnp.int32, shape, dim)


def _tri_ones(n, chunk, kind):
    i, j = _iota((n, n), 0), _iota((n, n), 1)
    same = lax.div(i, chunk) == lax.div(j, chunk)
    if kind == "incl":
        m = same & (j <= i)
    elif kind == "strict":
        m = same & (j < i)
    elif kind == "all":
        m = same
    else:
        m = same & (lax.rem(j, chunk) < (chunk // 2))
    return m


def _causal_conv(w, halo, x, width):
    r = x.shape[0]
    xin = jnp.concatenate([halo, x], axis=0)
    y = w[width - 1:width, :] * x
    for k in range(width - 1):
        off = 8 - (width - 1) + k
        y = y + w[k:k + 1, :] * xin[off:off + r, :]
    return y


def _each(fn, *lists):
    return [fn(*a) for a in zip(*lists)]


def _neumann(ms):
    n = ms[0].shape[0]
    eye = (_iota((n, n), 0) == _iota((n, n), 1)).astype(F32)
    accs = [eye - m for m in ms]
    ps = ms
    steps = 1
    while steps * 2 < n:
        ps = _each(hdot, ps, ps)
        accs = [acc + ap for acc, ap in zip(accs, _each(hdot, accs, ps))]
        steps *= 2
    return accs


@jax.custom_vjp
def tri_inverse(ms):
    return _neumann(ms)


def _tri_inverse_fwd(ms):
    ainvs = _neumann(ms)
    return ainvs, ainvs


def _tri_inverse_bwd(ainvs, gs):
    t = _each(lambda g, a: _dg3(g, a, 1, 1), gs, ainvs)
    return ([-x for x in _each(lambda a, y: _dg3(a, y, 0, 0), ainvs, t)],)


tri_inverse.defvjp(_tri_inverse_fwd, _tri_inverse_bwd)


def gdn_tile(params, state, ins, halos):
    conv_w, pk = params
    (pa,), (ha,) = ins, halos
    r = pa.shape[0]
    c, nh, dk = GDN_CHUNK, GDN_HEADS, GDN_DK
    kw = nh * dk
    qkv = _silu(_causal_conv(conv_w, ha[:, :3 * kw], pa[:, :3 * kw], 4))
    z = pa[:, 3 * kw:4 * kw]
    gsm = pa[:, 4 * kw:]
    a_log, dtb, nw = pk[0:1, :], pk[1:2, :], pk[2:3, :]
    g_all = -jnp.exp(a_log) * _softplus(gsm + dtb)
    beta_all = _sigmoid(gsm)
    incl = _tri_ones(c, c, "incl")
    strict = _tri_ones(c, c, "strict")
    lmat = incl.astype(F32)
    scale = dk ** -0.5
    nck = r // c
    inst = [(ci, h) for ci in range(nck) for h in range(nh)]

    def l2n(v):
        return v * lax.rsqrt(jnp.sum(v * v, axis=-1, keepdims=True) + EPS)

    gcs = [ldot(lmat, g_all[ci * c:(ci + 1) * c, :]) for ci in range(nck)]
    gcts = [g.T for g in gcs]
    g_col = [gcs[ci][:, h:h + 1] for ci, h in inst]
    g_row = [gcts[ci][h:h + 1, :] for ci, h in inst]
    g_last = [gcs[ci][c - 1:c, h:h + 1] for ci, h in inst]
    beta = [beta_all[ci * c:(ci + 1) * c, nh + h:nh + h + 1] for ci, h in inst]
    qh = [l2n(qkv[ci * c:(ci + 1) * c, h * dk:(h + 1) * dk]) for ci, h in inst]
    kh = [l2n(qkv[ci * c:(ci + 1) * c, kw + h * dk:kw + (h + 1) * dk]) for ci, h in inst]
    vh = [qkv[ci * c:(ci + 1) * c, 2 * kw + h * dk:2 * kw + (h + 1) * dk] for ci, h in inst]
    decay = [jnp.where(incl, jnp.exp(jnp.where(incl, gc_ - gr_, 0.0)), 0.0) for gc_, gr_ in zip(g_col, g_row)]
    kb = [k * b for k, b in zip(kh, beta)]
    qs = [q * scale for q in qh]
    kk = _each(lambda a, b, k: bdot_nt(jnp.concatenate([a, b], axis=0), k), kb, qs, kh)
    ms = [jnp.where(strict, x[:c] * d, 0.0) for x, d in zip(kk, decay)]
    attn = [x[c:] * d for x, d in zip(kk, decay)]
    ainv = tri_inverse(ms)
    eg = [jnp.exp(g) for g in g_col]
    rhs = [jnp.concatenate([v * b, k_ * e], axis=1) for v, b, k_, e in zip(vh, beta, kb, eg)]
    sol = _each(hdot, ainv, rhs)
    qg = [q * e for q, e in zip(qs, eg)]
    k_end = [k * jnp.exp(gl - g) for k, gl, g in zip(kh, g_last, g_col)]
    e_last = [jnp.exp(gl) for gl in g_last]

    st = [state[h * dk:(h + 1) * dk, :] for h in range(nh)]
    outs = [[] for _ in range(nh)]
    for ci in range(nck):
        idx = [ci * nh + h for h in range(nh)]
        ws = [bdot(jnp.concatenate([sol[i][:, dk:], qg[i]], axis=0), st[h]) for h, i in enumerate(idx)]
        v_new = [sol[i][:, :dk] - w_[:c] for i, w_ in zip(idx, ws)]
        av = [bdot(attn[i], v) for i, v in zip(idx, v_new)]
        kv = [bdot_tn(k_end[i], v) for i, v in zip(idx, v_new)]
        for h, i in enumerate(idx):
            o = ws[h][c:] + av[h]
            st[h] = st[h] * e_last[i] + kv[h]
            outs[h].append(_rms(o, nw) * _silu(z[ci * c:(ci + 1) * c, h * dk:(h + 1) * dk]))
    out = jnp.concatenate([jnp.concatenate(o, axis=0) for o in outs], axis=1)
    return jnp.concatenate(st, axis=0), [out]


def make_hgrn_tile(layer, depth):
    def hgrn_tile(params, state, ins, halos):
        lbp, nwp = params
        (pb,) = ins
        r = pb.shape[0]
        c = HGRN_CHUNK
        kw = HGRN_HEADS * HGRN_DK
        rows = [lbp[i:i + 1, :] for i in range(depth)]
        mx = functools.reduce(jnp.maximum, rows)
        ex = [jnp.exp(x - mx) for x in rows]
        den = functools.reduce(lambda a, b: a + b, ex)
        soft = [e / den for e in ex]
        lb = functools.reduce(lambda a, b: a + b, soft[:layer + 1]) - soft[0]
        nw = nwp[0:1, :]
        q = _silu(pb[:, :kw])
        fr = pb[:, kw:2 * kw]
        logf = jnp.log(lb + (1.0 - lb) * _sigmoid(fr))
        k = (1.0 - lb) * _sigmoid(-fr)
        v = pb[:, 2 * kw:3 * kw]
        gate = pb[:, 3 * kw:]
        incl = _tri_ones(r, c, "incl")
        masks = jnp.concatenate([incl.astype(F32), _tri_ones(r, c, "upto").astype(F32),
                                 _tri_ones(r, c, "all").astype(F32)], axis=0)
        sums = ldot(masks, logf)
        g_cum, g_ref, g_end = sums[:r], sums[r:2 * r], sums[2 * r:]
        qs = q * jnp.exp(g_cum - g_ref)
        ks = k * jnp.exp(g_ref - g_cum)
        qg = q * jnp.exp(g_cum)
        k_end = k * jnp.exp(g_end - g_cum)
        e_end = jnp.exp(g_end)
        sls = [slice(h * HGRN_DK, (h + 1) * HGRN_DK) for h in range(HGRN_HEADS)]
        attn = [jnp.where(incl, bdot_nt(qs[:, sl], ks[:, sl]), 0.0) for sl in sls]
        o_intra = [bdot(a, v[:, sl]) for a, sl in zip(attn, sls)]
        nsub, dk = r // c, HGRN_DK
        own_block = lax.div(_iota((r, nsub * dk), 0), c) == lax.div(_iota((r, nsub * dk), 1), dk)

        def spread(a):
            return jnp.where(own_block, jnp.concatenate([a] * nsub, axis=1), 0.0)

        kv = [bdot_tn(v[:, sl], spread(k_end[:, sl])) for sl in sls]
        s_t = [state[sl, :] for sl in sls]
        entry = [[] for _ in sls]
        for j in range(nsub):
            for lst, s_h in zip(entry, s_t):
                lst.append(s_h)
            s_t = [s_h * e_end[j * c:j * c + 1, sl] + x[:, j * dk:(j + 1) * dk] for s_h, sl, x in zip(s_t, sls, kv)]
        o_inter = [bdot_nt(spread(qg[:, sl]), jnp.concatenate(e, axis=1)) for sl, e in zip(sls, entry)]
        outs = [_rms(oa + ob, nw) * _silu(gate[:, sl]) for oa, ob, sl in zip(o_intra, o_inter, sls)]
        return jnp.concatenate(s_t, axis=0), [jnp.concatenate(outs, axis=1)]
    return hgrn_tile


def ssd_tile(params, state, ins, halos):
    conv_w, pv, ps = params
    (pc,), (hc,) = ins, halos
    r = pc.shape[0]
    c = SSD_CHUNK
    inner = SSD_HEADS * SSD_P
    gw = inner // SSD_GROUPS
    z = pc[:, :inner]
    xbc = _silu(_causal_conv(conv_w, hc[:, inner:inner + 1024], pc[:, inner:inner + 1024], 4) + pv[0:1, :])
    ssm = pc[:, inner + 1024:]
    xs = xbc[:, :inner]
    bm = xbc[:, inner:inner + SSD_GROUPS * SSD_STATE]
    cm = xbc[:, inner + SSD_GROUPS * SSD_STATE:]
    a_log, dtb, dsk = ps[0:1, :], ps[1:2, :], ps[2:3, :]
    nw = pv[1:2, :inner]
    dt = _softplus(ssm + dtb)
    da = dt * (-jnp.exp(a_log))
    expand = (lax.div(_iota((128, inner), 1), SSD_P) == _iota((128, inner), 0)).astype(F32)
    xdt = xs * rdot(dt, expand)
    d_e = rdot(jnp.concatenate([dsk] * 8, axis=0), expand)[0:1, :]
    incl = _tri_ones(c, c, "incl")
    lmat = incl.astype(F32)
    st = [state[g * SSD_STATE:(g + 1) * SSD_STATE, :] for g in range(SSD_GROUPS)]
    hpg = SSD_HEADS // SSD_GROUPS
    nck = r // c
    groups = range(SSD_GROUPS)
    cg = [(ci, g) for ci in range(nck) for g in groups]
    rows = [slice(ci * c, (ci + 1) * c) for ci in range(nck)]
    gls = [slice(g * gw, (g + 1) * gw) for g in groups]
    acs = [ldot(lmat, da[rs, :]) for rs in rows]
    acs_t = [a.T for a in acs]
    acs_e = [rdot(a, expand) for a in acs]
    last_e = [a[c - 1:c, :] for a in acs_e]
    bm_g = [bm[rows[ci], g * SSD_STATE:(g + 1) * SSD_STATE] for ci, g in cg]
    cm_g = [cm[rows[ci], g * SSD_STATE:(g + 1) * SSD_STATE] for ci, g in cg]
    cb = _each(bdot_nt, cm_g, bm_g)
    heads = [(i, ci, g * hpg + hg) for i, (ci, g) in enumerate(cg) for hg in range(hpg)]
    seg = [jnp.where(incl, jnp.exp(jnp.where(incl, acs[ci][:, hh:hh + 1] - acs_t[ci][hh:hh + 1, :], 0.0)), 0.0)
           for _, ci, hh in heads]
    yd = [bdot(cb[i] * sg, xdt[rows[ci], hh * SSD_P:(hh + 1) * SSD_P]) for (i, ci, hh), sg in zip(heads, seg)]
    y_diag = [jnp.concatenate(yd[i * hpg:(i + 1) * hpg], axis=1) for i in range(len(cg))]
    xw = [xdt[rows[ci], gls[g]] * jnp.exp(last_e[ci][:, gls[g]] - acs_e[ci][:, gls[g]]) for ci, g in cg]
    e_acs = [jnp.exp(acs_e[ci][:, gls[g]]) for ci, g in cg]
    e_last = [jnp.exp(last_e[ci][:, gls[g]]) for ci, g in cg]
    kv = _each(bdot_tn, bm_g, xw)
    ys = []
    for ci in range(nck):
        idx = [ci * SSD_GROUPS + g for g in groups]
        y_off = [bdot(cm_g[i], st[g]) * e_acs[i] for g, i in zip(groups, idx)]
        st = [st[g] * e_last[i] + kv[i] for g, i in zip(groups, idx)]
        ys.append(jnp.concatenate([y_diag[i] + yo for i, yo in zip(idx, y_off)], axis=1))
    y = jnp.concatenate(ys, axis=0) + d_e * xs
    yz = y * _silu(z)
    out = jnp.concatenate([_rms(yz[:, g * gw:(g + 1) * gw], nw[:, g * gw:(g + 1) * gw])
                           for g in range(SSD_GROUPS)], axis=1)
    return jnp.concatenate(st, axis=0), [out]


def convglu_tile(params, state, ins, halos):
    (cw,) = params
    (u,), (hu,) = ins, halos
    y = _causal_conv(cw, hu, u, 3) + cw[3:4, :]
    return None, [_silu(y[:, :FFN_HIDDEN]) * y[:, FFN_HIDDEN:]]


def convglu_tile_t(params, state, ins, halos):
    _, (act,) = convglu_tile(params, state, ins, halos)
    return None, [act, act.T]


def _halo_map(nt, r):
    return lambda b, n: (jnp.maximum((b * nt + n) * (r // 8) - 1, 0), 0)


def _exchange_copies(plan, local_plan, in_refs, out_refs, send_sems, recv_sems, local_sems):
    x, y, c = lax.axis_index("x"), lax.axis_index("y"), lax.axis_index("c")
    copies = []
    for k, fn in enumerate(plan):
        src, dst, peer = fn(in_refs, out_refs, x, y, c)
        copies.append(pltpu.make_async_remote_copy(src_ref=src, dst_ref=dst, send_sem=send_sems.at[k],
                                                   recv_sem=recv_sems.at[k], device_id=peer, device_id_type=MESH))
    for k, fn in enumerate(local_plan):
        src, dst = fn(in_refs, out_refs, x, y, c)
        copies.append(pltpu.make_async_copy(src, dst, local_sems.at[k]))
    return copies


def _exchange_sems(plan, local_plan):
    return [pltpu.SemaphoreType.DMA((max(len(plan), 1),)), pltpu.SemaphoreType.DMA((max(len(plan), 1),)),
            pltpu.SemaphoreType.DMA((max(len(local_plan), 1),))]


def _host_exchange(side, body, in_specs, o_specs, out_shape, scratch, args, grid):
    s_ins, s_shapes, plan, local_plan, then = side
    n_in, n_out, n_scr = len(in_specs), len(o_specs), len(scratch)
    k_in, k_out = len(s_ins), len(s_shapes)
    any_spec = pl.BlockSpec(memory_space=pl.ANY)

    def hosted(*refs):
        own_in, s_in = refs[:n_in], refs[n_in:n_in + k_in]
        o0 = n_in + k_in
        own_out, s_out = refs[o0:o0 + n_out], refs[o0 + n_out:o0 + n_out + k_out]
        rest = refs[o0 + n_out + k_out:]
        own_scr, sems, sems_then = rest[:n_scr], rest[n_scr:n_scr + 3], rest[n_scr + 3:]
        ids = [pl.program_id(a) for a in range(len(grid))]
        first = functools.reduce(lambda p, q: p & q, [i == 0 for i in ids])
        last = functools.reduce(lambda p, q: p & q, [i == g - 1 for i, g in zip(ids, grid)])

        @pl.when(first)
        def _():
            for cp in _exchange_copies(plan, local_plan, s_in, s_out, *sems):
                cp.start()

        body(*own_in, *own_out, *own_scr)

        @pl.when(last)
        def _():
            for cp in _exchange_copies(plan, local_plan, s_in, s_out, *sems):
                cp.wait()
            passed = _exchange_copies(then, (), s_in, s_out, *sems_then)
            for cp in passed:
                cp.start()
            for cp in passed:
                cp.wait()

    return (hosted, list(in_specs) + [any_spec] * k_in, list(o_specs) + [any_spec] * k_out,
            list(out_shape) + list(s_shapes),
            list(scratch) + _exchange_sems(plan, local_plan) + _exchange_sems(then, ()),
            list(args) + list(s_ins))


def seq_fwd(name, tile_fn, params, ins, use_halo, out_specs, state_shape, nb, s, r, side=None):
    nt = s // r
    n_p, n_i, n_o = len(params), len(ins), len(out_specs)
    has_state = state_shape is not None

    def body(*refs):
        p_refs, i_refs = refs[:n_p], refs[n_p:n_p + n_i]
        h_refs = refs[n_p + n_i:n_p + 2 * n_i] if use_halo else ()
        k = n_p + n_i + len(h_refs)
        o_refs = refs[k:k + n_o]
        n = pl.program_id(1)
        state = None
        if has_state:
            sv_ref, st_ref = refs[k + n_o], refs[k + n_o + 1]

            @pl.when(n == 0)
            def _():
                st_ref[...] = jnp.zeros(state_shape, F32)

            state = st_ref[...]
            sv_ref[0, 0] = state
        pv = [p[...] for p in p_refs]
        iv = [i[...].astype(F32) for i in i_refs]
        hv = [jnp.where(n > 0, h[...].astype(F32), 0.0) for h in h_refs]
        new_state, ov = tile_fn(pv, state, iv, hv)
        for o_ref, o in zip(o_refs, ov):
            o_ref[...] = o.astype(o_ref.dtype)
        if has_state:
            st_ref[...] = new_state

    row = lambda b, n: (b * nt + n, 0)
    in_specs = [pl.BlockSpec(p.shape, lambda b, n: (0, 0)) for p in params]
    in_specs += [pl.BlockSpec((r, a.shape[1]), row) for a in ins]
    if use_halo:
        in_specs += [pl.BlockSpec((8, a.shape[1]), _halo_map(nt, r)) for a in ins]
    col = lambda b, n: (0, b * nt + n)
    out_shape, o_specs = [], []
    for w, dt, *transposed in out_specs:
        out_shape.append(jax.ShapeDtypeStruct((w, nb * s) if transposed else (nb * s, w), dt))
        o_specs.append(pl.BlockSpec((w, r), col) if transposed else pl.BlockSpec((r, w), row))
    scratch = []
    if has_state:
        out_shape.append(jax.ShapeDtypeStruct((nb, nt) + tuple(state_shape), F32))
        o_specs.append(pl.BlockSpec((1, 1) + tuple(state_shape), lambda b, n: (b, n, 0, 0)))
        scratch.append(pltpu.VMEM(tuple(state_shape), F32))
    args = list(params) + list(ins) + (list(ins) if use_halo else [])
    if side is not None:
        body, in_specs, o_specs, out_shape, scratch, args = _host_exchange(
            side, body, in_specs, o_specs, out_shape, scratch, args, (nb, nt))
    return pl.pallas_call(body, grid=(nb, nt), in_specs=in_specs, out_specs=o_specs, out_shape=out_shape,
                          scratch_shapes=scratch, compiler_params=_cparams(("arbitrary", "arbitrary")),
                          name=name)(*args)


def seq_bwd(name, tile_fn, params, ins, use_halo, states, douts, din_dtypes, state_shape, nb, s, r, side=None):
    nt = s // r
    n_p, n_i, n_o = len(params), len(ins), len(douts)
    has_state = state_shape is not None

    def body(*refs):
        p_refs, i_refs = refs[:n_p], refs[n_p:n_p + n_i]
        h_refs = refs[n_p + n_i:n_p + 2 * n_i] if use_halo else ()
        k = n_p + n_i + len(h_refs)
        sv_ref = None
        if has_state:
            sv_ref = refs[k]
            k += 1
        do_refs = refs[k:k + n_o]
        k += n_o
        di_refs, dp_refs = refs[k:k + n_i], refs[k + n_i:k + n_i + n_p]
        k += n_i + n_p
        dst_ref = None
        if has_state:
            dst_ref = refs[k]
            k += 1
        dh_refs = refs[k:k + len(h_refs)]
        b, nn = pl.program_id(0), pl.program_id(1)
        n = nt - 1 - nn

        @pl.when((b == 0) & (nn == 0))
        def _():
            for dp in dp_refs:
                dp[...] = jnp.zeros(dp.shape, F32)

        @pl.when(nn == 0)
        def _():
            if has_state:
                dst_ref[...] = jnp.zeros(state_shape, F32)
            for dh in dh_refs:
                dh[...] = jnp.zeros(dh.shape, F32)

        pv = [p[...] for p in p_refs]
        iv = [i[...].astype(F32) for i in i_refs]
        hv = [jnp.where(n > 0, h[...].astype(F32), 0.0) for h in h_refs]
        if has_state:
            f = lambda pv_, st_, iv_, hv_: tile_fn(pv_, st_, iv_, hv_)
            _, vjp = jax.vjp(f, pv, sv_ref[0, 0], iv, hv)
            dpv, dst, div, dhv = vjp((dst_ref[...], [d[...].astype(F32) for d in do_refs]))
            dst_ref[...] = dst
        else:
            f = lambda pv_, iv_, hv_: tile_fn(pv_, None, iv_, hv_)[1]
            _, vjp = jax.vjp(f, pv, iv, hv)
            dpv, div, dhv = vjp([d[...].astype(F32) for d in do_refs])
        for j, (di_ref, d) in enumerate(zip(di_refs, div)):
            if use_halo:
                d = jnp.concatenate([d[:r - 8], d[r - 8:] + dh_refs[j][...]], axis=0)
            di_ref[...] = d.astype(di_ref.dtype)
        for dh_ref, d in zip(dh_refs, dhv):
            dh_ref[...] = d
        for dp_ref, d in zip(dp_refs, dpv):
            dp_ref[...] += d

    row = lambda b, nn: (b * nt + nt - 1 - nn, 0)
    hmap = _halo_map(nt, r)
    in_specs = [pl.BlockSpec(p.shape, lambda b, nn: (0, 0)) for p in params]
    in_specs += [pl.BlockSpec((r, a.shape[1]), row) for a in ins]
    if use_halo:
        in_specs += [pl.BlockSpec((8, a.shape[1]), lambda b, nn: hmap(b, nt - 1 - nn)) for a in ins]
    args = list(params) + list(ins) + (list(ins) if use_halo else [])
    scratch = []
    if has_state:
        in_specs.append(pl.BlockSpec((1, 1) + tuple(state_shape), lambda b, nn: (b, nt - 1 - nn, 0, 0)))
        args.append(states)
        scratch.append(pltpu.VMEM(tuple(state_shape), F32))
    in_specs += [pl.BlockSpec((r, d.shape[1]), row) for d in douts]
    args += list(douts)
    if use_halo:
        scratch += [pltpu.VMEM((8, a.shape[1]), F32) for a in ins]
    out_shape = [jax.ShapeDtypeStruct(a.shape, dt) for a, dt in zip(ins, din_dtypes)]
    out_shape += [jax.ShapeDtypeStruct(p.shape, F32) for p in params]
    o_specs = [pl.BlockSpec((r, a.shape[1]), row) for a in ins]
    o_specs += [pl.BlockSpec(p.shape, lambda b, nn: (0, 0)) for p in params]
    if side is not None:
        body, in_specs, o_specs, out_shape, scratch, args = _host_exchange(
            side, body, in_specs, o_specs, out_shape, scratch, args, (nb, nt))
    res = pl.pallas_call(body, grid=(nb, nt), in_specs=in_specs, out_specs=o_specs, out_shape=out_shape,
                         scratch_shapes=scratch, compiler_params=_cparams(("arbitrary", "arbitrary")),
                         name=name)(*args)
    if side is not None:
        return res[:n_i], res[n_i:n_i + n_p], res[n_i + n_p:]
    return res[:n_i], res[n_i:]


def matmul(name, a, b, mode, out_dtype=F32, addend=None, resid=None, tm=512, tn=None, tk=None):
    if mode == "nn":
        (m, kd), (_, n) = a.shape, b.shape
    elif mode == "nt":
        (m, kd), (n, _) = a.shape, b.shape
    else:
        (kd, m), (_, n) = a.shape, b.shape
    tm, tn, tk = min(tm, m if resid is None else resid[2]), min(tn or n, n), min(tk or kd, kd)
    nk = kd // tk
    assert m % tm == 0 and n % tn == 0 and kd % tk == 0
    dims = {"nn": ((1,), (0,)), "nt": ((1,), (1,)), "tn": ((0,), (0,))}[mode]
    extra = [] if addend is None else [addend]
    if resid is not None:
        extra = [resid[0], resid[1]]
    n_in, n_out = 2 + len(extra), 1 if resid is None else 2

    def body(*refs):
        a_ref, b_ref = refs[0], refs[1]
        part = lax.dot_general(a_ref[...].astype(BF16), b_ref[...].astype(BF16), (dims, ((), ())),
                               preferred_element_type=F32)

        def finish(acc):
            if resid is not None:
                refs[n_in][...] = refs[2][...] + refs[3][0] * acc
                refs[n_in + 1][...] = acc.astype(BF16)
            else:
                if addend is not None:
                    acc = acc + refs[2][...]
                refs[n_in][...] = acc.astype(refs[n_in].dtype)

        if nk == 1:
            finish(part)
        else:
            acc_ref = refs[n_in + n_out]
            k = pl.program_id(2)

            @pl.when(k == 0)
            def _():
                acc_ref[...] = part

            @pl.when(k > 0)
            def _():
                acc_ref[...] += part

            @pl.when(k == nk - 1)
            def _():
                finish(acc_ref[...])

    if mode == "tn":
        a_spec = pl.BlockSpec((tk, tm), lambda j, i, k: (k, i))
    else:
        a_spec = pl.BlockSpec((tm, tk), lambda j, i, k: (i, k))
    if mode == "nt":
        b_spec = pl.BlockSpec((tn, tk), lambda j, i, k: (j, k))
    else:
        b_spec = pl.BlockSpec((tk, tn), lambda j, i, k: (k, j))
    o_spec = pl.BlockSpec((tm, tn), lambda j, i, k: (i, j))
    in_specs = [a_spec, b_spec] + [o_spec] * (len(extra) > 0)
    out_specs, out_shape = o_spec, jax.ShapeDtypeStruct((m, n), out_dtype)
    if resid is not None:
        rows = resid[2]
        assert rows % tm == 0
        in_specs.append(pl.BlockSpec((1, 1, tn), lambda j, i, k: (lax.div(i * tm, rows), 0, j)))
        out_specs, out_shape = [o_spec, o_spec], [out_shape, jax.ShapeDtypeStruct((m, n), BF16)]
    scratch = [pltpu.VMEM((tm, tn), F32)] if nk > 1 else []
    return pl.pallas_call(body, grid=(n // tn, m // tm, nk), in_specs=in_specs, out_specs=out_specs,
                          out_shape=out_shape, scratch_shapes=scratch,
                          compiler_params=_cparams(("parallel", "parallel", "arbitrary")), name=name)(a, b, *extra)


def _normmod(x, nw, shift, scale):
    return _rms(x, nw) * (1.0 + scale) + shift


def _row_specs(nb, s, tr, d):
    nt = s // tr
    row = pl.BlockSpec((tr, d), lambda b, i: (b * nt + i, 0))
    per_seq = pl.BlockSpec((1, 1, d), lambda b, i: (b, 0, 0))
    full = pl.BlockSpec((1, d), lambda b, i: (0, 0))
    return nt, row, per_seq, full


def normmod_fwd(name, x, nw, shift, scale, nb, s, tr=512):
    d, tr = x.shape[1], min(tr, s)
    nt, row, per_seq, full = _row_specs(nb, s, tr, d)

    def body(x_ref, nw_ref, sh_ref, sc_ref, h_ref, ht_ref):
        h = _normmod(x_ref[...], nw_ref[...], sh_ref[0], sc_ref[0])
        h_ref[...] = h.astype(h_ref.dtype)
        ht_ref[...] = h.T.astype(ht_ref.dtype)

    return pl.pallas_call(body, grid=(nb, nt), in_specs=[row, full, per_seq, per_seq],
                          out_specs=[row, pl.BlockSpec((d, tr), lambda b, i: (0, b * nt + i))],
                          out_shape=[jax.ShapeDtypeStruct(x.shape, BF16), jax.ShapeDtypeStruct(x.shape[::-1], BF16)],
                          compiler_params=_cparams(("parallel", "parallel")), name=name)(x, nw, shift, scale)


def normmod_bwd(name, x, nw, shift, scale, dh, dres, nb, s, tr=512):
    d, tr = x.shape[1], min(tr, s)
    nt, row, per_seq, full = _row_specs(nb, s, tr, d)

    def body(x_ref, nw_ref, sh_ref, sc_ref, dh_ref, dres_ref, dx_ref, dnw_ref, dsh_ref, dsc_ref):
        b, i = pl.program_id(0), pl.program_id(1)

        @pl.when((b == 0) & (i == 0))
        def _():
            dnw_ref[...] = jnp.zeros(dnw_ref.shape, F32)

        @pl.when(i == 0)
        def _():
            dsh_ref[...] = jnp.zeros(dsh_ref.shape, F32)
            dsc_ref[...] = jnp.zeros(dsc_ref.shape, F32)

        _, vjp = jax.vjp(_normmod, x_ref[...], nw_ref[...], sh_ref[0], sc_ref[0])
        dx, dnw, dsh, dsc = vjp(dh_ref[...])
        dx_ref[...] = dres_ref[...] + dx
        dnw_ref[...] += dnw
        dsh_ref[0] += dsh
        dsc_ref[0] += dsc

    out_shape = [jax.ShapeDtypeStruct(x.shape, F32), jax.ShapeDtypeStruct((1, d), F32),
                 jax.ShapeDtypeStruct((nb, 1, d), F32), jax.ShapeDtypeStruct((nb, 1, d), F32)]
    return pl.pallas_call(body, grid=(nb, nt), in_specs=[row, full, per_seq, per_seq, row, row],
                          out_specs=[row, full, per_seq, per_seq], out_shape=out_shape,
                          compiler_params=_cparams(("arbitrary", "arbitrary")),
                          name=name)(x, nw, shift, scale, dh, dres)


def _merge_specs(nb, s, tr, d, wbr):
    nt, row, per_seq, _ = _row_specs(nb, s, tr, d)
    o_spec = pl.BlockSpec((tr, wbr), lambda b, i: (b * nt + i, 0))
    g_spec = pl.BlockSpec((tr, 3 * d), lambda b, i: (b * nt + i, 0))
    wbr_spec = pl.BlockSpec((wbr, d), lambda b, i: (0, 0))
    wo_spec = pl.BlockSpec((d, d), lambda b, i: (0, 0))
    return nt, row, per_seq, o_spec, g_spec, wbr_spec, wo_spec


def merge_fwd(name, x, oa, ob, oc, pg, gate1, wa, wb, wc, wo, nb, s, tr=512):
    d, tr = x.shape[1], min(tr, s)
    nt, row, per_seq, o_spec, g_spec, wbr_spec, wo_spec = _merge_specs(nb, s, tr, d, oa.shape[1])

    def body(x_ref, oa_ref, ob_ref, oc_ref, pg_ref, g1_ref, wa_ref, wb_ref, wc_ref, wo_ref,
             x1_ref, ya_ref, yb_ref, yc_ref, mg_ref, mx_ref):
        g = _sigmoid(pg_ref[...])
        ys = [_dg(o[...], w_[...], 1, 0) for o, w_ in ((oa_ref, wa_ref), (ob_ref, wb_ref), (oc_ref, wc_ref))]
        merged = g[:, :d] * ys[0] + g[:, d:2 * d] * ys[1] + g[:, 2 * d:] * ys[2]
        mix = _dg(merged, wo_ref[...], 1, 0)
        x1_ref[...] = x_ref[...] + g1_ref[0] * mix
        for r, v in zip((ya_ref, yb_ref, yc_ref, mg_ref, mx_ref), ys + [merged, mix]):
            r[...] = v.astype(r.dtype)

    return pl.pallas_call(body, grid=(nb, nt),
                          in_specs=[row, o_spec, o_spec, o_spec, g_spec, per_seq, wbr_spec, wbr_spec, wbr_spec, wo_spec],
                          out_specs=[row] * 6,
                          out_shape=[jax.ShapeDtypeStruct(x.shape, F32)] + [jax.ShapeDtypeStruct(x.shape, BF16)] * 5,
                          compiler_params=_cparams(("parallel", "parallel")),
                          name=name)(x, oa, ob, oc, pg, gate1, wa, wb, wc, wo)


def merge_bwd(name, oa, ob, oc, pg, gate1, wa, wb, wc, wo, ys, merged, mix, dx1, nb, s, tr=256):
    d, tr = dx1.shape[1], min(tr, s)
    wbr = oa.shape[1]
    nt, row, per_seq, o_spec, g_spec, wbr_spec, wo_spec = _merge_specs(nb, s, tr, d, wbr)

    def body(oa_ref, ob_ref, oc_ref, pg_ref, g1_ref, wa_ref, wb_ref, wc_ref, wo_ref, ya_ref, yb_ref, yc_ref, mg_ref,
             mx_ref, dx_ref, doa_ref, dob_ref, doc_ref, dpg_ref, dg1_ref, dwa_ref, dwb_ref, dwc_ref, dwo_ref):
        b, i = pl.program_id(0), pl.program_id(1)

        @pl.when((b == 0) & (i == 0))
        def _():
            for r in (dwa_ref, dwb_ref, dwc_ref, dwo_ref):
                r[...] = jnp.zeros(r.shape, F32)

        @pl.when(i == 0)
        def _():
            dg1_ref[...] = jnp.zeros(dg1_ref.shape, F32)

        dx = dx_ref[...]
        dg1_ref[0] += jnp.sum(dx * mx_ref[...].astype(F32), axis=0, keepdims=True)
        dmix = g1_ref[0] * dx
        dmerged = _dg(dmix, wo_ref[...], 1, 1)
        dwo_ref[...] += _dg(mg_ref[...], dmix, 0, 0)
        g = _sigmoid(pg_ref[...])
        branches = ((oa_ref, wa_ref, ya_ref, doa_ref, dwa_ref), (ob_ref, wb_ref, yb_ref, dob_ref, dwb_ref),
                    (oc_ref, wc_ref, yc_ref, doc_ref, dwc_ref))
        dgs = []
        for k, (o_ref, w_ref, y_ref, do_ref, dw_ref) in enumerate(branches):
            gk = g[:, k * d:(k + 1) * d]
            dy = dmerged * gk
            dgs.append(dmerged * y_ref[...].astype(F32) * gk * (1.0 - gk))
            do_ref[...] = _dg(dy, w_ref[...], 1, 1)
            dw_ref[...] += _dg(o_ref[...], dy, 0, 0)
        dpg_ref[...] = jnp.concatenate(dgs, axis=1).astype(dpg_ref.dtype)

    t = nb * s
    out_shape = ([jax.ShapeDtypeStruct((t, wbr), F32)] * 3
                 + [jax.ShapeDtypeStruct((t, 3 * d), BF16), jax.ShapeDtypeStruct((nb, 1, d), F32)]
                 + [jax.ShapeDtypeStruct((wbr, d), F32)] * 3 + [jax.ShapeDtypeStruct((d, d), F32)])
    return pl.pallas_call(body, grid=(nb, nt),
                          in_specs=[o_spec, o_spec, o_spec, g_spec, per_seq, wbr_spec, wbr_spec, wbr_spec, wo_spec]
                          + [row] * 6,
                          out_specs=[o_spec, o_spec, o_spec, g_spec, per_seq, wbr_spec, wbr_spec, wbr_spec, wo_spec],
                          out_shape=out_shape, compiler_params=_cparams(("arbitrary", "arbitrary")),
                          name=name)(oa, ob, oc, pg, gate1, wa, wb, wc, wo, *ys, merged, mix, dx1)


def resid_down_bwd(name, dx, f, gate, w_down, nb, s, tr=512):
    d, tr = dx.shape[1], min(tr, s)
    hidden = w_down.shape[0]
    nt, row, per_seq, _ = _row_specs(nb, s, tr, d)

    def body(dx_ref, f_ref, g_ref, w_ref, df_ref, dg_ref, da_ref):
        @pl.when(pl.program_id(1) == 0)
        def _():
            dg_ref[...] = jnp.zeros(dg_ref.shape, F32)

        dx_t = dx_ref[...]
        df = (g_ref[0] * dx_t).astype(BF16)
        df_ref[...] = df
        dg_ref[0] += jnp.sum(dx_t * f_ref[...], axis=0, keepdims=True)
        da_ref[...] = _dgb(df, w_ref[...], 1, 1)

    return pl.pallas_call(body, grid=(nb, nt),
                          in_specs=[row, row, per_seq, pl.BlockSpec((hidden, d), lambda b, i: (0, 0))],
                          out_specs=[row, per_seq, pl.BlockSpec((tr, hidden), lambda b, i: (b * nt + i, 0))],
                          out_shape=[jax.ShapeDtypeStruct(dx.shape, BF16), jax.ShapeDtypeStruct((nb, 1, d), F32),
                                     jax.ShapeDtypeStruct((dx.shape[0], hidden), F32)],
                          compiler_params=_cparams(("arbitrary", "arbitrary")), name=name)(dx, f, gate, w_down)


def loss_head(name, x, fw, target, tr=512):
    t, d = x.shape
    row = pl.BlockSpec((tr, d), lambda i: (i, 0))
    full = pl.BlockSpec((1, d), lambda i: (0, 0))

    def loss_fn(xv, fwv, tv):
        err = _rms(xv, fwv) - tv
        return 0.5 * jnp.sum(jnp.mean(err * err, axis=-1))

    def body(x_ref, fw_ref, t_ref, dx_ref, l_ref, dfw_ref):
        @pl.when(pl.program_id(0) == 0)
        def _():
            l_ref[...] = jnp.zeros(l_ref.shape, F32)
            dfw_ref[...] = jnp.zeros(dfw_ref.shape, F32)

        val, (dx, dfw) = jax.value_and_grad(loss_fn, argnums=(0, 1))(x_ref[...], fw_ref[...], t_ref[...])
        dx_ref[...] = dx
        l_ref[...] += val
        dfw_ref[...] += dfw

    return pl.pallas_call(body, grid=(t // tr,), in_specs=[row, full, row],
                          out_specs=[row, pl.BlockSpec((1, 128), lambda i: (0, 0)), full],
                          out_shape=[jax.ShapeDtypeStruct((t, d), F32), jax.ShapeDtypeStruct((1, 128), F32),
                                     jax.ShapeDtypeStruct((1, d), F32)],
                          compiler_params=_cparams(("arbitrary",)), name=name)(x, fw, target)


def _row_tile(rows, cols, n_arrays):
    budget = 24 * 1024 * 1024 // (8 * cols * max(n_arrays, 1))
    tr = rows
    while tr > max(budget, 16) and tr % 2 == 0 and (tr // 2) % 16 == 0:
        tr //= 2
    return tr


def elementwise(name, fn, ins, out_dtypes):
    rows, cols = ins[0].shape
    tr = _row_tile(rows, cols, len(ins) + len(out_dtypes))
    spec = pl.BlockSpec((tr, cols), lambda i: (i, 0))
    n_in = len(ins)

    def body(*refs):
        outs = fn(*[r[...] for r in refs[:n_in]])
        for o_ref, o in zip(refs[n_in:], outs):
            o_ref[...] = o.astype(o_ref.dtype)

    return pl.pallas_call(body, grid=(rows // tr,), in_specs=[spec] * n_in, out_specs=[spec] * len(out_dtypes),
                          out_shape=[jax.ShapeDtypeStruct((rows, cols), dt) for dt in out_dtypes],
                          compiler_params=_cparams(("parallel",)), name=name)(*ins)


def _adamw(w, g, m, v):
    m = ADAM_B1 * m + (1.0 - ADAM_B1) * g
    v = ADAM_B2 * v + (1.0 - ADAM_B2) * (g * g)
    m_hat = m / (1.0 - ADAM_B1 ** ADAM_STEP)
    v_hat = v / (1.0 - ADAM_B2 ** ADAM_STEP)
    delta = -ADAM_LR * (m_hat / (jnp.sqrt(v_hat) + ADAM_EPS) + ADAM_WD * w)
    return delta, m, v


def adamw(name, w, g, m, v):
    return elementwise(name, _adamw, [w, g, m, v], [F32, F32, F32])


_ANY = pl.BlockSpec(memory_space=pl.ANY)


def _coords():
    return lax.axis_index("x"), lax.axis_index("y"), lax.axis_index("c")


def allgather8(name, arrays, halves):
    n = len(arrays)

    def body(*refs):
        in_refs, out_refs = refs[:n], refs[n:2 * n]
        send_sems, recv_sems, local_sems = refs[2 * n:]
        x, y, c = _coords()
        me, sibling = (x, y, c), (x, y, 1 - c)
        chips = [(1 - x, y), (x, 1 - y), (1 - x, 1 - y)]

        def blk(i, px, py, pc):
            return out_refs[i].at[4 * px + 2 * py + pc]

        def piece(i):
            return in_refs[i].at[c] if halves[i] else in_refs[i]

        def copy(i, k, block, to, src=None):
            return pltpu.make_async_remote_copy(
                src_ref=blk(i, *block) if src is None else src, dst_ref=blk(i, *block),
                send_sem=send_sems.at[7 * i + k], recv_sem=recv_sems.at[7 * i + k],
                device_id=to, device_id_type=MESH)

        mine = [pltpu.make_async_copy(piece(i), blk(i, *me), local_sems.at[i]) for i in range(n)]
        for cp in mine:
            cp.start()
        first = []
        for i in range(n):
            first.append(copy(i, 0, me, sibling, src=piece(i)))
            first += [copy(i, 1 + j, me, (*chip, c), src=piece(i)) for j, chip in enumerate(chips)]
        for cp in first:
            cp.start()
        passed = []
        for j, chip in enumerate(chips):
            for i in range(n):
                copy(i, 1 + j, (*chip, c), me).wait_recv()
                fwd = copy(i, 4 + j, (*chip, c), sibling)
                fwd.start()
                passed.append(fwd)
        for i in range(n):
            copy(i, 0, sibling, me).wait_recv()
            for j, chip in enumerate(chips):
                copy(i, 4 + j, (*chip, 1 - c), me).wait_recv()
        for cp in first + passed:
            cp.wait_send()
        for cp in mine:
            cp.wait()

    out_shape = []
    for a, hv in zip(arrays, halves):
        out_shape.append(jax.ShapeDtypeStruct((N_DEV,) + tuple(a.shape[1:] if hv else a.shape), a.dtype))
    return pl.pallas_call(
        body, in_specs=[_ANY] * n, out_specs=[_ANY] * n, out_shape=out_shape,
        scratch_shapes=[pltpu.SemaphoreType.DMA((7 * n,)), pltpu.SemaphoreType.DMA((7 * n,)),
                        pltpu.SemaphoreType.DMA((n,))],
        name=name)(*arrays)


def exchange(name, ins, out_shapes, plan, local_plan=()):
    n_in, n_out = len(ins), len(out_shapes)

    def body(*refs):
        copies = _exchange_copies(plan, local_plan, refs[:n_in], refs[n_in:n_in + n_out], *refs[n_in + n_out:])
        for cp in copies:
            cp.start()
        for cp in copies:
            cp.wait()

    return pl.pallas_call(
        body, in_specs=[_ANY] * n_in, out_specs=[_ANY] * n_out, out_shape=out_shapes,
        scratch_shapes=_exchange_sems(plan, local_plan), name=name)(*ins)


def sum_halves(name, gs, recv, c_arr, chip_arr):
    _, _, hr, cs = gs.shape
    tr = _row_tile(hr, cs, 4)

    def body(c_ref, chip_ref, g_ref, r_ref, qf_ref, qb_ref):
        q = g_ref[0, 0] + r_ref[0, 0]
        qb_ref[0] = q.astype(BF16)

        @pl.when(pl.program_id(1) == chip_ref[0])
        def _():
            qf_ref[...] = q

    grid_spec = pltpu.PrefetchScalarGridSpec(
        num_scalar_prefetch=2, grid=(hr // tr, N_CHIPS),
        in_specs=[pl.BlockSpec((1, 1, tr, cs), lambda i, j, c_ref, chip_ref: (j, c_ref[0], i, 0)),
                  pl.BlockSpec((1, 1, tr, cs), lambda i, j, c_ref, chip_ref: (j, 0, i, 0))],
        out_specs=[pl.BlockSpec((tr, cs), lambda i, j, c_ref, chip_ref: (i, 0)),
                   pl.BlockSpec((1, tr, cs), lambda i, j, c_ref, chip_ref: (j, i, 0))])
    return pl.pallas_call(body, grid_spec=grid_spec,
                          out_shape=[jax.ShapeDtypeStruct((hr, cs), F32),
                                     jax.ShapeDtypeStruct((N_CHIPS, hr, cs), BF16)],
                          compiler_params=_cparams(("parallel", "arbitrary")), name=name)(c_arr, chip_arr, gs, recv)


def sum_chips(name, qf, recv):
    (total,) = elementwise(name, lambda q, a, b, c: (q + a.astype(F32) + b.astype(F32) + c.astype(F32),),
                           [qf] + list(recv), [F32])
    return total


def adamw_halves(name, w, m, v, layer, g_mine, g_other, c_arr, prev=None):
    _, _, hr, cs = w.shape
    tr = _row_tile(hr, cs, 9)

    def body(c_ref, w_ref, m_ref, v_ref, gm_ref, go_ref, *rest):
        g_ref, d_ref, nm_ref, nv_ref = rest[-4:]
        g = jnp.where(pl.program_id(0) == c_ref[0], gm_ref[...], go_ref[...])
        delta, nm, nv = _adamw(w_ref[0, 0], g, m_ref[0, 0], v_ref[0, 0])
        g_ref[0, 0], d_ref[0, 0], nm_ref[0, 0], nv_ref[0, 0] = g, delta, nm, nv

    half = pl.BlockSpec((1, 1, tr, cs), lambda h, i, c_ref: (layer, h, i, 0))
    row = pl.BlockSpec((tr, cs), lambda h, i, c_ref: (i, 0))
    in_specs, args, aliases = [half, half, half, row, row], [c_arr, w, m, v, g_mine, g_other], {}
    if prev is not None:
        in_specs += [pl.BlockSpec(memory_space=pl.ANY)] * 4
        args += list(prev)
        aliases = {6 + k: k for k in range(4)}
    grid_spec = pltpu.PrefetchScalarGridSpec(num_scalar_prefetch=1, grid=(2, hr // tr),
                                             in_specs=in_specs, out_specs=[half] * 4)
    return pl.pallas_call(body, grid_spec=grid_spec, out_shape=[jax.ShapeDtypeStruct(w.shape, F32)] * 4,
                          input_output_aliases=aliases, compiler_params=_cparams(("parallel", "parallel")),
                          name=name)(*args)


def sum8(name, g):
    _, rows, cols = g.shape
    tr = _row_tile(rows, cols, 9)

    def body(*refs):
        acc = refs[0][0]
        for r in refs[1:N_DEV]:
            acc = acc + r[0]
        refs[N_DEV][...] = acc

    in_specs = [pl.BlockSpec((1, tr, cols), functools.partial(lambda k, i: (k, i, 0), k)) for k in range(N_DEV)]
    return pl.pallas_call(body, grid=(rows // tr,), in_specs=in_specs,
                          out_specs=pl.BlockSpec((tr, cols), lambda i: (i, 0)),
                          out_shape=jax.ShapeDtypeStruct((rows, cols), F32),
                          compiler_params=_cparams(("parallel",)), name=name)(*([g] * N_DEV))


_QKV, _AB, _GZ = (0, 1536), (1536, 1544), (1544, 2056)
_HG = (2056, 4104)
_SZ, _XBC, _DT = (4104, 4616), (4616, 5640), (5640, 5648)
_GATES = (5648, 8720)


def _split_w_in(w8):
    _, hr, cs = w8.shape
    w4 = w8.reshape(N_CHIPS, 2 * hr, cs)

    def cols(rng):
        lo, hi = rng
        return [w4[j][:, max(lo, j * cs) - j * cs:min(hi, (j + 1) * cs) - j * cs]
                for j in range(N_CHIPS) if max(lo, j * cs) < min(hi, (j + 1) * cs)]

    pad = [jnp.zeros((2 * hr, 120), w8.dtype)]
    return (jnp.concatenate(cols(_GATES), axis=1),
            jnp.concatenate(cols(_QKV) + cols(_GZ) + cols(_AB) + pad, axis=1),
            jnp.concatenate(cols(_HG), axis=1),
            jnp.concatenate(cols(_SZ) + cols(_XBC) + cols(_DT) + pad, axis=1))


def _stack_w_in(g, a, b, c):
    segments = [(a, 0, 1536), (a, 2048, 2056), (a, 1536, 2048), (b, 0, 2048), (c, 0, 512), (c, 512, 1536),
                (c, 1536, 1544), (g, 0, 3072)]
    cs = sum(s1 - s0 for _, s0, s1 in segments) // N_CHIPS
    chips = []
    for j in range(N_CHIPS):
        parts, off = [], 0
        for arr, s0, s1 in segments:
            u0, u1 = max(j * cs, off), min((j + 1) * cs, off + s1 - s0)
            if u0 < u1:
                parts.append(arr[:, s0 + u0 - off:s0 + u1 - off])
            off += s1 - s0
        chips.append(jnp.concatenate(parts, axis=1))
    rows = g.shape[0]
    return jnp.stack(chips).reshape(N_CHIPS, 2, rows // 2, cs)


def _rows8(rows, width):
    out = [jnp.pad(r.astype(F32), (0, width - r.shape[0])) for r in rows]
    out += [jnp.zeros((width,), F32)] * (8 - len(out))
    return jnp.stack(out)


class _Packer:
    def __init__(self):
        self.items, self.size = [], 0

    def add(self, name, shape):
        n = 1
        for d in shape:
            n *= d
        self.items.append((name, tuple(shape), self.size, n))
        self.size += n

    def rows(self):
        return -(-self.size // 8192) * 8

    def pack(self, values):
        flat = [values[name].astype(F32).reshape(-1) for name, _, _, _ in self.items]
        flat.append(jnp.zeros((self.rows() * 1024 - self.size,), F32))
        return jnp.concatenate(flat).reshape(self.rows(), 1024)

    def unpack(self, buf):
        flat = buf.reshape(-1)
        return {name: flat[off:off + n].reshape(shape) for name, shape, off, n in self.items}


def _stack_by_chip(g, axis):
    l, r, c = g.shape
    if axis == 2:
        cs = c // N_CHIPS
        g = g.reshape(l, r, N_CHIPS, cs).transpose(2, 0, 1, 3).reshape(N_CHIPS, 2, l * r // 2, cs)
    else:
        rs = r // N_CHIPS
        g = g.reshape(l, N_CHIPS, rs, c).transpose(1, 0, 2, 3).reshape(N_CHIPS, 2, l * rs // 2, c)
    return g


def _unstack_gathered(w8, l, axis):
    _, hr, cs = w8.shape
    w = w8.reshape(N_CHIPS, l, 2 * hr // l, cs)
    if axis == 2:
        return w.transpose(1, 2, 0, 3).reshape(l, 2 * hr // l, N_CHIPS * cs)
    return w.transpose(1, 0, 2, 3).reshape(l, N_CHIPS * 2 * hr // l, cs)


_BIG = (("w_in", 2), ("w_br_a", 2), ("w_br_b", 2), ("w_br_c", 2), ("w_out", 1), ("ffn_w_up", 2), ("ffn_w_down", 1))
_SMALL = ("b_ada", "norm1_w", "gdn_conv_w", "gdn_a_log", "gdn_dt_bias", "gdn_norm_w", "hgrn_lb_param",
          "hgrn_norm_w", "ssd_conv_w", "ssd_conv_b", "ssd_a_log", "ssd_dt_bias", "ssd_d", "ssd_norm_w",
          "norm2_w", "ffn_conv_w", "ffn_conv_b", "final_norm_w")
_WEIGHTS = ("w_ada", "b_ada", "norm1_w", "w_in", "gdn_conv_w", "gdn_a_log", "gdn_dt_bias", "gdn_norm_w",
            "hgrn_lb_param", "hgrn_norm_w", "ssd_conv_w", "ssd_conv_b", "ssd_a_log", "ssd_dt_bias", "ssd_d",
            "ssd_norm_w", "w_br_a", "w_br_b", "w_br_c", "w_out", "norm2_w", "ffn_w_up", "ffn_conv_w",
            "ffn_conv_b", "ffn_w_down", "final_norm_w")
_R_GDN, _R_HGRN, _R_SSD, _R_FFN = 256, 128, 256, 256


_MASKS = ((1, 0), (0, 1), (1, 1))


def _flip(k, x, y):
    return (1 - x if _MASKS[k][0] else x), (1 - y if _MASKS[k][1] else y)


def _rs_d2d(grads):
    plan = [functools.partial(lambda i, ins, outs, x, y, c: (ins[i].at[:, pl.ds(1 - c, 1)], outs[i], (x, y, 1 - c)), i)
            for i in range(len(grads))]
    return grads, [jax.ShapeDtypeStruct((N_CHIPS, 1) + g.shape[2:], F32) for g in grads], plan, (), ()


def _rs_ici(grads, recv, tag):
    n = len(grads)
    c_arr = lax.axis_index("c").astype(jnp.int32).reshape(1)
    chip_arr = (2 * lax.axis_index("x") + lax.axis_index("y")).astype(jnp.int32).reshape(1)
    q = [sum_halves("rs_sum_d2d%s_%d" % (tag, i), g, r, c_arr, chip_arr) for i, (g, r) in enumerate(zip(grads, recv))]
    qf, qb = [a for a, _ in q], [b for _, b in q]

    def ici(i, k, ins, outs, x, y, c):
        px, py = _flip(k, x, y)
        return ins[i].at[2 * px + py], outs[3 * i + k], (px, py, c)

    plan = [functools.partial(ici, i, k) for i in range(n) for k in range(3)]
    shapes = [jax.ShapeDtypeStruct(g.shape[2:], BF16) for g in grads for _ in range(3)]
    return qf, (qb, shapes, plan, (), ())


def _rs_finish(qf, res, tag):
    n = len(qf)
    red = [sum_chips("rs_sum_ici%s_%d" % (tag, i), qf[i], res[3 * i:3 * i + 3]) for i in range(n)]
    plan = [functools.partial(lambda i, ins, outs, x, y, c: (ins[i], outs[i], (x, y, 1 - c)), i) for i in range(n)]
    other = exchange("rs_swap" + tag, red, [jax.ShapeDtypeStruct(r.shape, F32) for r in red], plan)
    return red, other


def _join_sides(sides):
    ins, shapes, plan = [], [], []
    for s_ins, s_shapes, s_plan, _, _ in sides:
        def shifted(fn, i0, i1, o0, o1, in_refs, out_refs, x, y, c):
            return fn(in_refs[i0:i1], out_refs[o0:o1], x, y, c)

        i0, o0 = len(ins), len(shapes)
        plan += [functools.partial(shifted, fn, i0, i0 + len(s_ins), o0, o0 + len(s_shapes)) for fn in s_plan]
        ins += list(s_ins)
        shapes += list(s_shapes)
    return ins, shapes, plan, (), ()


def _gather_side(pieces):
    n = len(pieces)

    def send(i, k, ins, outs, x, y, c):
        px, py = _flip(k, x, y)
        return ins[i].at[c], outs[i].at[2 * (2 * x + y) + c], (px, py, c)

    def to_sibling(i, h, ins, outs, x, y, c):
        return ins[i].at[h], outs[i].at[2 * (2 * x + y) + h], (x, y, 1 - c)

    def pass_on(i, k, ins, outs, x, y, c):
        px, py = _flip(k, x, y)
        blk = 2 * (2 * px + py) + c
        return outs[i].at[blk], outs[i].at[blk], (x, y, 1 - c)

    plan = [functools.partial(send, i, k) for i in range(n) for k in range(3)]
    plan += [functools.partial(to_sibling, i, h) for i in range(n) for h in range(2)]
    then = [functools.partial(pass_on, i, k) for i in range(n) for k in range(3)]
    shapes = [jax.ShapeDtypeStruct((N_DEV,) + p.shape[1:], p.dtype) for p in pieces]
    return pieces, shapes, plan, (), then


def kernel(x, c, w_ada, b_ada, norm1_w, w_in, gdn_conv_w, gdn_a_log, gdn_dt_bias, gdn_norm_w, hgrn_lb_param, hgrn_norm_w, ssd_conv_w, ssd_conv_b, ssd_a_log, ssd_dt_bias, ssd_d, ssd_norm_w, w_br_a, w_br_b, w_br_c, w_out, norm2_w, ffn_w_up, ffn_conv_w, ffn_conv_b, ffn_w_down, final_norm_w, loss_target, m_w_ada, m_b_ada, m_norm1_w, m_w_in, m_gdn_conv_w, m_gdn_a_log, m_gdn_dt_bias, m_gdn_norm_w, m_hgrn_lb_param, m_hgrn_norm_w, m_ssd_conv_w, m_ssd_conv_b, m_ssd_a_log, m_ssd_dt_bias, m_ssd_d, m_ssd_norm_w, m_w_br_a, m_w_br_b, m_w_br_c, m_w_out, m_norm2_w, m_ffn_w_up, m_ffn_conv_w, m_ffn_conv_b, m_ffn_w_down, m_final_norm_w, v_w_ada, v_b_ada, v_norm1_w, v_w_in, v_gdn_conv_w, v_gdn_a_log, v_gdn_dt_bias, v_gdn_norm_w, v_hgrn_lb_param, v_hgrn_norm_w, v_ssd_conv_w, v_ssd_conv_b, v_ssd_a_log, v_ssd_dt_bias, v_ssd_d, v_ssd_norm_w, v_w_br_a, v_w_br_b, v_w_br_c, v_w_out, v_norm2_w, v_ffn_w_up, v_ffn_conv_w, v_ffn_conv_b, v_ffn_w_down, v_final_norm_w):
    loc = dict(locals())
    w = {k: loc[k] for k in _WEIGHTS}
    mom = {k: loc["m_" + k] for k in _WEIGHTS}
    var = {k: loc["v_" + k] for k in _WEIGHTS}
    nb, s, d = x.shape
    t = nb * s
    depth = w_ada.shape[0]
    chip = 2 * lax.axis_index("x") + lax.axis_index("y")
    dev = 2 * chip + lax.axis_index("c")
    x0 = x.reshape(t, d)
    target = loss_target.reshape(t, d)

    small_in = [c, gdn_conv_w.reshape(depth * 4, -1), ssd_conv_w.reshape(depth * 4, -1),
                ffn_conv_w.reshape(depth * 3, -1)]
    c_all, gcw, scw, fcw = allgather8("ag_small", small_in, [False] * 4)
    c_all = c_all.reshape(N_DEV * nb, d)

    def conv_full(g, taps):
        g = g[::2].reshape(N_CHIPS, depth, taps, -1)
        return g.transpose(1, 2, 0, 3).reshape(depth, taps, -1)

    gdn_cw, ssd_cw, ffn_cw = conv_full(gcw, 4), conv_full(scw, 4), conv_full(fcw, 3)

    axis_of = dict(_BIG)
    first_needed, later = ("w_in",), tuple(n for n, _ in _BIG if n != "w_in")
    wls = [dict() for _ in range(depth)]

    def pieces(keys):
        out = []
        for l, name in keys:
            a = w[name][l].astype(BF16)
            out.append(a.reshape(2, a.shape[0] // 2, a.shape[1]))
        return out

    def arrived(keys, bufs):
        for (l, name), g in zip(keys, bufs):
            if name == "w_in":
                wls[l]["w_g"], wls[l]["w_a"], wls[l]["w_b"], wls[l]["w_c"] = _split_w_in(g)
            else:
                wls[l][name] = _unstack_gathered(g, 1, axis_of[name])[0]

    keys0 = [(0, n) for n in first_needed]
    arrived(keys0, allgather8("ag_weights0", pieces(keys0), [True] * len(keys0)))

    (c_act,) = elementwise("silu_c", lambda v: (_silu(v),), [c_all], [F32])
    mod_cols = jnp.concatenate([matmul("ada_fwd%d" % l, c_act, w_ada[l], "nn") for l in range(depth)], axis=0)
    (mod8,) = allgather8("ag_mod", [mod_cols], [False])
    mod = mod8[::2].reshape(N_CHIPS, depth, N_DEV * nb, -1).transpose(1, 2, 0, 3).reshape(depth, N_DEV * nb, 6 * d)
    mod = lax.dynamic_slice_in_dim(mod, dev * nb, nb, axis=1) + b_ada[:, None, :]

    def mod_part(l, k):
        return mod[l, :, k * d:(k + 1) * d].reshape(nb, 1, d)

    saved = []
    xl = x0
    for l in range(depth):
        sfx = str(l)
        wl = wls[l]
        sv = {"x0": xl}
        shift1, scale1, gate1, shift2, scale2, gate2 = [mod_part(l, k) for k in range(6)]
        sv["mods"] = (shift1, scale1, gate1, shift2, scale2, gate2)
        h, h_t = normmod_fwd("norm1_fwd" + sfx, xl, norm1_w[l][None], shift1, scale1, nb, s)
        pg = matmul("proj_g" + sfx, h, wl["w_g"], "nn")
        pa = matmul("proj_a" + sfx, h, wl["w_a"], "nn")
        pb = matmul("proj_b" + sfx, h, wl["w_b"], "nn")
        pc = matmul("proj_c" + sfx, h, wl["w_c"], "nn")
        gdn_p = [_rows8(list(gdn_cw[l]), 1536), _rows8([gdn_a_log[l], gdn_dt_bias[l], gdn_norm_w[l]], 128)]
        hgrn_p = [_rows8(list(hgrn_lb_param), 512), _rows8([hgrn_norm_w[l]], 128)]
        ssd_p = [_rows8(list(ssd_cw[l]), 1024), _rows8([ssd_conv_b[l], ssd_norm_w[l]], 1024),
                 _rows8([ssd_a_log[l], ssd_dt_bias[l], ssd_d[l]], 128)]
        ffn_p = [_rows8(list(ffn_cw[l]) + [ffn_conv_b[l]], 2 * FFN_HIDDEN)]
        hgrn_fn = make_hgrn_tile(l, depth)
        keys = [(l, n) for n in later] + ([(l + 1, n) for n in first_needed] if l + 1 < depth else [])
        oa, st_a, *bufs = seq_fwd("gdn_fwd" + sfx, gdn_tile, gdn_p, [pa], True, [(512, BF16)], (512, 128), nb, s,
                                  _R_GDN, side=_gather_side(pieces(keys)))
        arrived(keys, bufs)
        ob, st_b = seq_fwd("hgrn_fwd" + sfx, hgrn_fn, hgrn_p, [pb], False, [(512, BF16)], (512, 128), nb, s, _R_HGRN)
        oc, st_c = seq_fwd("ssd_fwd" + sfx, ssd_tile, ssd_p, [pc], True, [(512, BF16)], (256, 256), nb, s, _R_SSD)
        x1, *merge_saved = merge_fwd("merge_fwd" + sfx, xl, oa, ob, oc, pg, gate1, wl["w_br_a"], wl["w_br_b"],
                                     wl["w_br_c"], wl["w_out"], nb, s)
        h2, h2_t = normmod_fwd("norm2_fwd" + sfx, x1, norm2_w[l][None], shift2, scale2, nb, s)
        u = matmul("ffn_up" + sfx, h2, wl["ffn_w_up"], "nn", tn=FFN_HIDDEN)
        act, act_t = seq_fwd("convglu_fwd" + sfx, convglu_tile_t, ffn_p, [u], True,
                             [(FFN_HIDDEN, BF16), (FFN_HIDDEN, BF16, "T")], None, nb, s, _R_FFN)
        xl, f = matmul("ffn_down" + sfx, act, wl["ffn_w_down"], "nn", resid=(x1, gate2, s))
        sv.update(merge_saved=merge_saved, h_t=h_t, h2_t=h2_t, act_t=act_t, pg=pg, pa=pa, pb=pb, pc=pc, oa=oa, ob=ob, oc=oc, st_a=st_a, st_b=st_b, st_c=st_c, x1=x1,
                  u=u, f=f, gdn_p=gdn_p, hgrn_p=hgrn_p, ssd_p=ssd_p, ffn_p=ffn_p, hgrn_fn=hgrn_fn)
        saved.append(sv)

    dx, loss_part, d_final = loss_head("loss_head", xl, final_norm_w[None], target)

    sg = {}
    dmod = [None] * depth
    d_lb = None
    reduced = {}
    early = [n for n, _ in _BIG if n != "w_in"]
    to_d2d = to_ici = None

    def finish(keys, qf, res, tag):
        for key, mine, other in zip(keys, *_rs_finish(qf, res, tag)):
            reduced[key] = (mine, other)
    for l in reversed(range(depth)):
        sfx = str(l)
        sv, wl = saved[l], wls[l]
        gfull = {}
        shift1, scale1, gate1, shift2, scale2, gate2 = sv["mods"]
        df, dgate2, dact = resid_down_bwd("ffn_down_dx" + sfx, dx, sv["f"], gate2, wl["ffn_w_down"], nb, s)
        gfull["ffn_w_down"] = matmul("ffn_down_dw" + sfx, sv["act_t"], df, "nn", tm=1408, tn=512, tk=4096)
        cg_args = ("convglu_bwd" + sfx, convglu_tile, sv["ffn_p"], [sv["u"]], True, None, [dact], [BF16], None, nb, s,
                   _R_FFN)
        if to_d2d is None:
            (du,), (dcw,) = seq_bwd(*cg_args)
        else:
            lp, stacked = to_d2d
            (du,), (dcw,), recv = seq_bwd(*cg_args, side=_rs_d2d(stacked))
            to_ici = ([(lp, "w_in")],) + _rs_ici(stacked, recv, "i%d" % lp) + ("i%d" % lp,)
        dh2 = matmul("ffn_up_dx" + sfx, du, wl["ffn_w_up"], "nt")
        gfull["ffn_w_up"] = matmul("ffn_up_dw" + sfx, sv["h2_t"], du, "nn", tm=1024, tn=512, tk=4096)
        dx1, dnw2, dshift2, dscale2 = normmod_bwd("norm2_bwd" + sfx, sv["x1"], norm2_w[l][None], shift2, scale2, dh2, dx,
                                                  nb, s)
        doa, dob, doc, dpg, dgate1, dwa, dwb, dwc, dwo = merge_bwd(
            "merge_bwd" + sfx, sv["oa"], sv["ob"], sv["oc"], sv["pg"], gate1, wl["w_br_a"], wl["w_br_b"],
            wl["w_br_c"], wl["w_out"], sv["merge_saved"][:3], *sv["merge_saved"][3:], dx1, nb, s)
        gfull["w_br_a"], gfull["w_br_b"], gfull["w_br_c"], gfull["w_out"] = dwa, dwb, dwc, dwo
        gdn_args = ("gdn_bwd" + sfx, gdn_tile, sv["gdn_p"], [sv["pa"]], True, sv["st_a"], [doa], [BF16], (512, 128),
                    nb, s, _R_GDN)
        stacked = [_stack_by_chip(gfull[n][None], axis_of[n]) for n in early]
        (dpc,), (dscw, dspv, dsps), recv = seq_bwd("ssd_bwd" + sfx, ssd_tile, sv["ssd_p"], [sv["pc"]], True, sv["st_c"],
                                                   [doc], [BF16], (256, 256), nb, s, _R_SSD, side=_rs_d2d(stacked))
        hosted = [([(l, n) for n in early],) + _rs_ici(stacked, recv, "e" + sfx) + ("e" + sfx,)]
        if to_ici is not None:
            hosted.append(to_ici)
        (dpa,), (dgcw, dgpk), res = seq_bwd(*gdn_args, side=_join_sides([h[2] for h in hosted]))
        for keys, qf, side, tag in hosted:
            finish(keys, qf, res[:len(side[1])], tag)
            res = res[len(side[1]):]
        (dpb,), (dlbp, dhnw) = seq_bwd("hgrn_bwd" + sfx, sv["hgrn_fn"], sv["hgrn_p"], [sv["pb"]], False, sv["st_b"],
                                       [dob], [BF16], (512, 128), nb, s, _R_HGRN)
        dh = matmul("proj_g_dx" + sfx, dpg, wl["w_g"], "nt")
        dh = matmul("proj_a_dx" + sfx, dpa, wl["w_a"], "nt", addend=dh)
        dh = matmul("proj_b_dx" + sfx, dpb, wl["w_b"], "nt", addend=dh)
        dh = matmul("proj_c_dx" + sfx, dpc, wl["w_c"], "nt", addend=dh)
        stacked_w_in = _stack_w_in(
            matmul("proj_g_dw" + sfx, sv["h_t"], dpg, "nn", tm=1024, tn=512, tk=4096),
            matmul("proj_a_dw" + sfx, sv["h_t"], dpa, "nn", tm=1024, tk=1024),
            matmul("proj_b_dw" + sfx, sv["h_t"], dpb, "nn", tm=1024, tn=512, tk=4096),
            matmul("proj_c_dw" + sfx, sv["h_t"], dpc, "nn", tm=1024, tk=1024))
        dx, dnw1, dshift1, dscale1 = normmod_bwd("norm1_bwd" + sfx, sv["x0"], norm1_w[l][None], shift1, scale1, dh, dx1,
                                                 nb, s)
        dmod[l] = jnp.concatenate([dshift1, dscale1, dgate1, dshift2, dscale2, dgate2], axis=-1).reshape(nb, 6 * d)
        d_lb = dlbp[:depth] if d_lb is None else d_lb + dlbp[:depth]
        sg[l] = dict(norm1_w=dnw1[0], norm2_w=dnw2[0], gdn_conv_w=dgcw[:4], gdn_a_log=dgpk[0, :4],
                     gdn_dt_bias=dgpk[1, :4], gdn_norm_w=dgpk[2], hgrn_norm_w=dhnw[0], ssd_conv_w=dscw[:4],
                     ssd_conv_b=dspv[0], ssd_norm_w=dspv[1, :512], ssd_a_log=dsps[0, :8], ssd_dt_bias=dsps[1, :8],
                     ssd_d=dsps[2, :8], ffn_conv_w=dcw[:3], ffn_conv_b=dcw[3])
        to_d2d = (l, [stacked_w_in])
    lp, stacked = to_d2d
    recv = exchange("rs_d2d_i%d" % lp, *_rs_d2d(stacked)[:4])
    qf, side = _rs_ici(stacked, recv, "i%d" % lp)
    finish([(lp, "w_in")], qf, exchange("rs_ici_i%d" % lp, *side[:4]), "i%d" % lp)
    grad_x = dx.reshape(nb, s, d)

    dmod = jnp.stack(dmod)
    (b_sum,) = elementwise("bias_rows", lambda *r: (functools.reduce(lambda p, q: p + q, r),),
                           [dmod[:, b].reshape(depth * 6, d) for b in range(nb)], [F32])
    per_layer = ("norm1_w", "norm2_w", "gdn_conv_w", "gdn_a_log", "gdn_dt_bias", "gdn_norm_w", "hgrn_norm_w",
                 "ssd_conv_w", "ssd_conv_b", "ssd_norm_w", "ssd_a_log", "ssd_dt_bias", "ssd_d", "ffn_conv_w", "ffn_conv_b")
    vals = {k: jnp.stack([sg[l][k] for l in range(depth)]) for k in per_layer}
    vals.update(loss=loss_part[0, :1], b_ada=b_sum.reshape(depth, 6 * d), hgrn_lb_param=d_lb, final_norm_w=d_final[0])
    gp = _Packer()
    for k, v in vals.items():
        gp.add(k, v.shape)
    packed8, dmod8 = allgather8("ag_grads", [gp.pack(vals), dmod.reshape(depth * nb, 6 * d)], [False, False])
    gs = gp.unpack(sum8("sum_small", packed8))
    loss = gs["loss"].reshape(())

    def my_cols(g):
        cs = g.shape[-1] // N_CHIPS
        return lax.dynamic_slice_in_dim(g, chip * cs, cs, axis=g.ndim - 1)

    for k in ("gdn_conv_w", "ssd_conv_w", "ffn_conv_w"):
        gs[k] = my_cols(gs[k])

    dmod_all = dmod8.reshape(N_DEV, depth, nb, 6 * d).transpose(1, 0, 2, 3).reshape(depth, N_DEV * nb, 6 * d)
    dmod_mine = lax.dynamic_slice_in_dim(dmod_all, chip * (6 * d // N_CHIPS), 6 * d // N_CHIPS, axis=2)
    g_w_ada = jnp.stack([matmul("ada_dw%d" % l, c_act, dmod_mine[l], "tn", tm=1024) for l in range(depth)])

    c_arr = lax.axis_index("c").astype(jnp.int32).reshape(1)
    grads, delta, new_m, new_v = {}, {}, {}, {}
    for i, (name, _) in enumerate(_BIG):
        shp = w[name].shape
        halves = lambda a: a.reshape((depth, 2) + reduced[(0, name)][0].shape)
        res = None
        for l in reversed(range(depth)):
            res = adamw_halves("adamw_%s%d" % (name, l), halves(w[name]), halves(mom[name]), halves(var[name]), l,
                               *reduced[(l, name)], c_arr, prev=res)
        grads[name], delta[name], new_m[name], new_v[name] = [r.reshape(shp) for r in res]
    grads["w_ada"] = g_w_ada
    for k in _SMALL:
        grads[k] = gs[k].reshape(w[k].shape)
    shp = w_ada.shape
    flat = lambda a: a.reshape(shp[0] * shp[1], shp[2])
    dl, nm, nv = adamw("adamw_w_ada", flat(w_ada), flat(g_w_ada), flat(m_w_ada), flat(v_w_ada))
    delta["w_ada"], new_m["w_ada"], new_v["w_ada"] = dl.reshape(shp), nm.reshape(shp), nv.reshape(shp)
    sp = _Packer()
    for k in _SMALL:
        sp.add(k, w[k].shape)
    dl, nm, nv = adamw("adamw_small", sp.pack(w), sp.pack(grads), sp.pack(mom), sp.pack(var))
    delta.update(sp.unpack(dl))
    new_m.update(sp.unpack(nm))
    new_v.update(sp.unpack(nv))

    return (loss, grad_x, *[grads[k] for k in _WEIGHTS], *[delta[k] for k in _WEIGHTS],
            *[new_m[k] for k in _WEIGHTS], *[new_v[k] for k in _WEIGHTS])
```

```python
import functools

import jax
import jax.numpy as jnp
from jax import lax
from jax.experimental import pallas as pl
from jax.experimental.pallas import tpu as pltpu

F32 = jnp.float32
BF16 = jnp.bfloat16
HI = lax.Precision.HIGHEST
MESH = pl.DeviceIdType.MESH

EPS = 1e-6
D_MODEL = 1024
GDN_HEADS, GDN_DK, GDN_CHUNK = 4, 128, 64
HGRN_HEADS, HGRN_DK, HGRN_CHUNK = 4, 128, 16
SSD_HEADS, SSD_P, SSD_GROUPS, SSD_STATE, SSD_CHUNK = 8, 64, 2, 128, 64
FFN_HIDDEN = 2816
N_CHIPS = 4
N_DEV = 8

ADAM_LR, ADAM_B1, ADAM_B2, ADAM_EPS, ADAM_WD, ADAM_STEP = 0.001, 0.9, 0.999, 1e-08, 0.01, 10

W_G, W_A, W_B, W_C = 3072, 2176, 2048, 1664
VMEM_LIMIT = 56 * 1024 * 1024


def _cparams(sem):
    return pltpu.CompilerParams(dimension_semantics=sem, vmem_limit_bytes=VMEM_LIMIT)


def _dg(a, b, ca, cb):
    return lax.dot_general(a.astype(BF16), b.astype(BF16), (((ca,), (cb,)), ((), ())),
                           preferred_element_type=F32)


@jax.custom_vjp
def bdot(a, b):
    return _dg(a, b, 1, 0)


bdot.defvjp(lambda a, b: (_dg(a, b, 1, 0), (a, b)),
            lambda r, g: (_dg(g, r[1], 1, 1), _dg(r[0], g, 0, 0)))


@jax.custom_vjp
def bdot_nt(a, b):
    return _dg(a, b, 1, 1)


bdot_nt.defvjp(lambda a, b: (_dg(a, b, 1, 1), (a, b)),
               lambda r, g: (_dg(g, r[1], 1, 0), _dg(g, r[0], 0, 0)))


@jax.custom_vjp
def bdot_tn(a, b):
    return _dg(a, b, 0, 0)


bdot_tn.defvjp(lambda a, b: (_dg(a, b, 0, 0), (a, b)),
               lambda r, g: (_dg(r[1], g, 1, 1), _dg(r[0], g, 1, 0)))


def _split(x, n):
    parts, rest = [], x
    for _ in range(n):
        p = rest.astype(BF16)
        parts.append(p)
        rest = rest - p.astype(F32)
    return parts


def _dgb(a, b, ca, cb):
    return lax.dot_general(a, b, (((ca,), (cb,)), ((), ())), preferred_element_type=F32)


def _dg3(a, b, ca, cb):
    (ah, al), (bh, bl) = _split(a, 2), _split(b, 2)
    return _dgb(jnp.concatenate([ah, ah, al], axis=ca), jnp.concatenate([bh, bl, bh], axis=cb), ca, cb)


@jax.custom_vjp
def hdot(a, b):
    return _dg3(a, b, 1, 0)


hdot.defvjp(lambda a, b: (_dg3(a, b, 1, 0), (a, b)),
            lambda r, g: (_dg3(g, r[1], 1, 1), _dg3(r[0], g, 0, 0)))


def _dge(e, x, ce, cx, e_first):
    eb = e.astype(BF16)
    es = jnp.concatenate([eb, eb, eb], axis=ce)
    xs = jnp.concatenate(_split(x, 3), axis=cx)
    return _dgb(es, xs, ce, cx) if e_first else _dgb(xs, es, cx, ce)


@jax.custom_vjp
def ldot(e, x):
    return _dge(e, x, 1, 0, True)


ldot.defvjp(lambda e, x: (_dge(e, x, 1, 0, True), e),
            lambda e, g: (jnp.zeros_like(e), _dge(e, g, 0, 0, True)))


@jax.custom_vjp
def rdot(x, e):
    return _dge(e, x, 0, 1, False)


rdot.defvjp(lambda x, e: (_dge(e, x, 0, 1, False), e),
            lambda e, g: (_dge(e, g, 1, 1, False), jnp.zeros_like(e)))


@jax.custom_vjp
def _sigmoid(x):
    return 1.0 / (1.0 + jnp.exp(-x))


def _sigmoid_fwd(x):
    g = 1.0 / (1.0 + jnp.exp(-x))
    return g, g


_sigmoid.defvjp(_sigmoid_fwd, lambda g, ct: (ct * g * (1.0 - g),))


def _silu(x):
    return x * _sigmoid(x)


def _softplus(x):
    return jnp.maximum(x, 0.0) + jnp.log(1.0 + jnp.exp(-jnp.abs(x)))


def _rms(x, w):
    return x * lax.rsqrt(jnp.mean(x * x, axis=-1, keepdims=True) + EPS) * w


def _iota(shape, dim):
    return lax.broadcasted_iota(jnp.int32, shape, dim)


def _tri_ones(n, chunk, kind):
    i, j = _iota((n, n), 0), _iota((n, n), 1)
    same = lax.div(i, chunk) == lax.div(j, chunk)
    if kind == "incl":
        m = same & (j <= i)
    elif kind == "strict":
        m = same & (j < i)
    elif kind == "all":
        m = same
    else:
        m = same & (lax.rem(j, chunk) < (chunk // 2))
    return m


def _causal_conv(w, halo, x, width):
    r = x.shape[0]
    xin = jnp.concatenate([halo, x], axis=0)
    y = w[width - 1:width, :] * x
    for k in range(width - 1):
        off = 8 - (width - 1) + k
        y = y + w[k:k + 1, :] * xin[off:off + r, :]
    return y


def _each(fn, *lists):
    return [fn(*a) for a in zip(*lists)]


def _neumann(ms):
    n = ms[0].shape[0]
    eye = (_iota((n, n), 0) == _iota((n, n), 1)).astype(F32)
    accs = [eye - m for m in ms]
    ps = ms
    steps = 1
    while steps * 2 < n:
        ps = _each(hdot, ps, ps)
        accs = [acc + ap for acc, ap in zip(accs, _each(hdot, accs, ps))]
        steps *= 2
    return accs


@jax.custom_vjp
def tri_inverse(ms):
    return _neumann(ms)


def _tri_inverse_fwd(ms):
    ainvs = _neumann(ms)
    return ainvs, ainvs


def _tri_inverse_bwd(ainvs, gs):
    t = _each(lambda g, a: _dg3(g, a, 1, 1), gs, ainvs)
    return ([-x for x in _each(lambda a, y: _dg3(a, y, 0, 0), ainvs, t)],)


tri_inverse.defvjp(_tri_inverse_fwd, _tri_inverse_bwd)


def gdn_tile(params, state, ins, halos):
    conv_w, pk = params
    (pa,), (ha,) = ins, halos
    r = pa.shape[0]
    c, nh, dk = GDN_CHUNK, GDN_HEADS, GDN_DK
    kw = nh * dk
    qkv = _silu(_causal_conv(conv_w, ha[:, :3 * kw], pa[:, :3 * kw], 4))
    z = pa[:, 3 * kw:4 * kw]
    gsm = pa[:, 4 * kw:]
    a_log, dtb, nw = pk[0:1, :], pk[1:2, :], pk[2:3, :]
    g_all = -jnp.exp(a_log) * _softplus(gsm + dtb)
    beta_all = _sigmoid(gsm)
    incl = _tri_ones(c, c, "incl")
    strict = _tri_ones(c, c, "strict")
    lmat = incl.astype(F32)
    scale = dk ** -0.5
    nck = r // c
    inst = [(ci, h) for ci in range(nck) for h in range(nh)]

    def l2n(v):
        return v * lax.rsqrt(jnp.sum(v * v, axis=-1, keepdims=True) + EPS)

    gcs = [ldot(lmat, g_all[ci * c:(ci + 1) * c, :]) for ci in range(nck)]
    gcts = [g.T for g in gcs]
    g_col = [gcs[ci][:, h:h + 1] for ci, h in inst]
    g_row = [gcts[ci][h:h + 1, :] for ci, h in inst]
    g_last = [gcs[ci][c - 1:c, h:h + 1] for ci, h in inst]
    beta = [beta_all[ci * c:(ci + 1) * c, nh + h:nh + h + 1] for ci, h in inst]
    qh = [l2n(qkv[ci * c:(ci + 1) * c, h * dk:(h + 1) * dk]) for ci, h in inst]
    kh = [l2n(qkv[ci * c:(ci + 1) * c, kw + h * dk:kw + (h + 1) * dk]) for ci, h in inst]
    vh = [qkv[ci * c:(ci + 1) * c, 2 * kw + h * dk:2 * kw + (h + 1) * dk] for ci, h in inst]
    decay = [jnp.where(incl, jnp.exp(jnp.where(incl, gc_ - gr_, 0.0)), 0.0) for gc_, gr_ in zip(g_col, g_row)]
    kb = [k * b for k, b in zip(kh, beta)]
    qs = [q * scale for q in qh]
    kk = _each(lambda a, b, k: bdot_nt(jnp.concatenate([a, b], axis=0), k), kb, qs, kh)
    ms = [jnp.where(strict, x[:c] * d, 0.0) for x, d in zip(kk, decay)]
    attn = [x[c:] * d for x, d in zip(kk, decay)]
    ainv = tri_inverse(ms)
    eg = [jnp.exp(g) for g in g_col]
    rhs = [jnp.concatenate([v * b, k_ * e], axis=1) for v, b, k_, e in zip(vh, beta, kb, eg)]
    sol = _each(hdot, ainv, rhs)
    qg = [q * e for q, e in zip(qs, eg)]
    k_end = [k * jnp.exp(gl - g) for k, gl, g in zip(kh, g_last, g_col)]
    e_last = [jnp.exp(gl) for gl in g_last]

    st = [state[h * dk:(h + 1) * dk, :] for h in range(nh)]
    outs = [[] for _ in range(nh)]
    for ci in range(nck):
        idx = [ci * nh + h for h in range(nh)]
        ws = [bdot(jnp.concatenate([sol[i][:, dk:], qg[i]], axis=0), st[h]) for h, i in enumerate(idx)]
        v_new = [sol[i][:, :dk] - w_[:c] for i, w_ in zip(idx, ws)]
        av = [bdot(attn[i], v) for i, v in zip(idx, v_new)]
        kv = [bdot_tn(k_end[i], v) for i, v in zip(idx, v_new)]
        for h, i in enumerate(idx):
            o = ws[h][c:] + av[h]
            st[h] = st[h] * e_last[i] + kv[h]
            outs[h].append(_rms(o, nw) * _silu(z[ci * c:(ci + 1) * c, h * dk:(h + 1) * dk]))
    out = jnp.concatenate([jnp.concatenate(o, axis=0) for o in outs], axis=1)
    return jnp.concatenate(st, axis=0), [out]


def make_hgrn_tile(layer, depth):
    def hgrn_tile(params, state, ins, halos):
        lbp, nwp = params
        (pb,) = ins
        r = pb.shape[0]
        c = HGRN_CHUNK
        kw = HGRN_HEADS * HGRN_DK
        rows = [lbp[i:i + 1, :] for i in range(depth)]
        mx = functools.reduce(jnp.maximum, rows)
        ex = [jnp.exp(x - mx) for x in rows]
        den = functools.reduce(lambda a, b: a + b, ex)
        soft = [e / den for e in ex]
        lb = functools.reduce(lambda a, b: a + b, soft[:layer + 1]) - soft[0]
        nw = nwp[0:1, :]
        q = _silu(pb[:, :kw])
        fr = pb[:, kw:2 * kw]
        logf = jnp.log(lb + (1.0 - lb) * _sigmoid(fr))
        k = (1.0 - lb) * _sigmoid(-fr)
        v = pb[:, 2 * kw:3 * kw]
        gate = pb[:, 3 * kw:]
        incl = _tri_ones(r, c, "incl")
        masks = jnp.concatenate([incl.astype(F32), _tri_ones(r, c, "upto").astype(F32),
                                 _tri_ones(r, c, "all").astype(F32)], axis=0)
        sums = ldot(masks, logf)
        g_cum, g_ref, g_end = sums[:r], sums[r:2 * r], sums[2 * r:]
        qs = q * jnp.exp(g_cum - g_ref)
        ks = k * jnp.exp(g_ref - g_cum)
        qg = q * jnp.exp(g_cum)
        k_end = k * jnp.exp(g_end - g_cum)
        e_end = jnp.exp(g_end)
        sls = [slice(h * HGRN_DK, (h + 1) * HGRN_DK) for h in range(HGRN_HEADS)]
        attn = [jnp.where(incl, bdot_nt(qs[:, sl], ks[:, sl]), 0.0) for sl in sls]
        o_intra = [bdot(a, v[:, sl]) for a, sl in zip(attn, sls)]
        nsub, dk = r // c, HGRN_DK
        own_block = lax.div(_iota((r, nsub * dk), 0), c) == lax.div(_iota((r, nsub * dk), 1), dk)

        def spread(a):
            return jnp.where(own_block, jnp.concatenate([a] * nsub, axis=1), 0.0)

        kv = [bdot_tn(v[:, sl], spread(k_end[:, sl])) for sl in sls]
        s_t = [state[sl, :] for sl in sls]
        entry = [[] for _ in sls]
        for j in range(nsub):
            for lst, s_h in zip(entry, s_t):
                lst.append(s_h)
            s_t = [s_h * e_end[j * c:j * c + 1, sl] + x[:, j * dk:(j + 1) * dk] for s_h, sl, x in zip(s_t, sls, kv)]
        o_inter = [bdot_nt(spread(qg[:, sl]), jnp.concatenate(e, axis=1)) for sl, e in zip(sls, entry)]
        outs = [_rms(oa + ob, nw) * _silu(gate[:, sl]) for oa, ob, sl in zip(o_intra, o_inter, sls)]
        return jnp.concatenate(s_t, axis=0), [jnp.concatenate(outs, axis=1)]
    return hgrn_tile


def ssd_tile(params, state, ins, halos):
    conv_w, pv, ps = params
    (pc,), (hc,) = ins, halos
    r = pc.shape[0]
    c = SSD_CHUNK
    inner = SSD_HEADS * SSD_P
    gw = inner // SSD_GROUPS
    z = pc[:, :inner]
    xbc = _silu(_causal_conv(conv_w, hc[:, inner:inner + 1024], pc[:, inner:inner + 1024], 4) + pv[0:1, :])
    ssm = pc[:, inner + 1024:]
    xs = xbc[:, :inner]
    bm = xbc[:, inner:inner + SSD_GROUPS * SSD_STATE]
    cm = xbc[:, inner + SSD_GROUPS * SSD_STATE:]
    a_log, dtb, dsk = ps[0:1, :], ps[1:2, :], ps[2:3, :]
    nw = pv[1:2, :inner]
    dt = _softplus(ssm + dtb)
    da = dt * (-jnp.exp(a_log))
    expand = (lax.div(_iota((128, inner), 1), SSD_P) == _iota((128, inner), 0)).astype(F32)
    xdt = xs * rdot(dt, expand)
    d_e = rdot(jnp.concatenate([dsk] * 8, axis=0), expand)[0:1, :]
    incl = _tri_ones(c, c, "incl")
    lmat = incl.astype(F32)
    st = [state[g * SSD_STATE:(g + 1) * SSD_STATE, :] for g in range(SSD_GROUPS)]
    hpg = SSD_HEADS // SSD_GROUPS
    nck = r // c
    groups = range(SSD_GROUPS)
    cg = [(ci, g) for ci in range(nck) for g in groups]
    rows = [slice(ci * c, (ci + 1) * c) for ci in range(nck)]
    gls = [slice(g * gw, (g + 1) * gw) for g in groups]
    acs = [ldot(lmat, da[rs, :]) for rs in rows]
    acs_t = [a.T for a in acs]
    acs_e = [rdot(a, expand) for a in acs]
    last_e = [a[c - 1:c, :] for a in acs_e]
    bm_g = [bm[rows[ci], g * SSD_STATE:(g + 1) * SSD_STATE] for ci, g in cg]
    cm_g = [cm[rows[ci], g * SSD_STATE:(g + 1) * SSD_STATE] for ci, g in cg]
    cb = _each(bdot_nt, cm_g, bm_g)
    heads = [(i, ci, g * hpg + hg) for i, (ci, g) in enumerate(cg) for hg in range(hpg)]
    seg = [jnp.where(incl, jnp.exp(jnp.where(incl, acs[ci][:, hh:hh + 1] - acs_t[ci][hh:hh + 1, :], 0.0)), 0.0)
           for _, ci, hh in heads]
    yd = [bdot(cb[i] * sg, xdt[rows[ci], hh * SSD_P:(hh + 1) * SSD_P]) for (i, ci, hh), sg in zip(heads, seg)]
    y_diag = [jnp.concatenate(yd[i * hpg:(i + 1) * hpg], axis=1) for i in range(len(cg))]
    xw = [xdt[rows[ci], gls[g]] * jnp.exp(last_e[ci][:, gls[g]] - acs_e[ci][:, gls[g]]) for ci, g in cg]
    e_acs = [jnp.exp(acs_e[ci][:, gls[g]]) for ci, g in cg]
    e_last = [jnp.exp(last_e[ci][:, gls[g]]) for ci, g in cg]
    kv = _each(bdot_tn, bm_g, xw)
    ys = []
    for ci in range(nck):
        idx = [ci * SSD_GROUPS + g for g in groups]
        y_off = [bdot(cm_g[i], st[g]) * e_acs[i] for g, i in zip(groups, idx)]
        st = [st[g] * e_last[i] + kv[i] for g, i in zip(groups, idx)]
        ys.append(jnp.concatenate([y_diag[i] + yo for i, yo in zip(idx, y_off)], axis=1))
    y = jnp.concatenate(ys, axis=0) + d_e * xs
    yz = y * _silu(z)
    out = jnp.concatenate([_rms(yz[:, g * gw:(g + 1) * gw], nw[:, g * gw:(g + 1) * gw])
                           for g in range(SSD_GROUPS)], axis=1)
    return jnp.concatenate(st, axis=0), [out]


def convglu_tile(params, state, ins, halos):
    (cw,) = params
    (u,), (hu,) = ins, halos
    y = _causal_conv(cw, hu, u, 3) + cw[3:4, :]
    return None, [_silu(y[:, :FFN_HIDDEN]) * y[:, FFN_HIDDEN:]]


def convglu_tile_t(params, state, ins, halos):
    _, (act,) = convglu_tile(params, state, ins, halos)
    return None, [act, act.T]


def _halo_map(nt, r):
    return lambda b, n: (jnp.maximum((b * nt + n) * (r // 8) - 1, 0), 0)


def _exchange_copies(plan, local_plan, in_refs, out_refs, send_sems, recv_sems, local_sems):
    x, y, c = lax.axis_index("x"), lax.axis_index("y"), lax.axis_index("c")
    copies = []
    for k, fn in enumerate(plan):
        src, dst, peer = fn(in_refs, out_refs, x, y, c)
        copies.append(pltpu.make_async_remote_copy(src_ref=src, dst_ref=dst, send_sem=send_sems.at[k],
                                                   recv_sem=recv_sems.at[k], device_id=peer, device_id_type=MESH))
    for k, fn in enumerate(local_plan):
        src, dst = fn(in_refs, out_refs, x, y, c)
        copies.append(pltpu.make_async_copy(src, dst, local_sems.at[k]))
    return copies


def _exchange_sems(plan, local_plan):
    return [pltpu.SemaphoreType.DMA((max(len(plan), 1),)), pltpu.SemaphoreType.DMA((max(len(plan), 1),)),
            pltpu.SemaphoreType.DMA((max(len(local_plan), 1),))]


def _host_exchange(side, body, in_specs, o_specs, out_shape, scratch, args, grid):
    s_ins, s_shapes, plan, local_plan, then = side
    n_in, n_out, n_scr = len(in_specs), len(o_specs), len(scratch)
    k_in, k_out = len(s_ins), len(s_shapes)
    any_spec = pl.BlockSpec(memory_space=pl.ANY)

    def hosted(*refs):
        own_in, s_in = refs[:n_in], refs[n_in:n_in + k_in]
        o0 = n_in + k_in
        own_out, s_out = refs[o0:o0 + n_out], refs[o0 + n_out:o0 + n_out + k_out]
        rest = refs[o0 + n_out + k_out:]
        own_scr, sems, sems_then = rest[:n_scr], rest[n_scr:n_scr + 3], rest[n_scr + 3:]
        ids = [pl.program_id(a) for a in range(len(grid))]
        first = functools.reduce(lambda p, q: p & q, [i == 0 for i in ids])
        last = functools.reduce(lambda p, q: p & q, [i == g - 1 for i, g in zip(ids, grid)])

        @pl.when(first)
        def _():
            for cp in _exchange_copies(plan, local_plan, s_in, s_out, *sems):
                cp.start()

        body(*own_in, *own_out, *own_scr)

        @pl.when(last)
        def _():
            for cp in _exchange_copies(plan, local_plan, s_in, s_out, *sems):
                cp.wait()
            passed = _exchange_copies(then, (), s_in, s_out, *sems_then)
            for cp in passed:
                cp.start()
            for cp in passed:
                cp.wait()

    return (hosted, list(in_specs) + [any_spec] * k_in, list(o_specs) + [any_spec] * k_out,
            list(out_shape) + list(s_shapes),
            list(scratch) + _exchange_sems(plan, local_plan) + _exchange_sems(then, ()),
            list(args) + list(s_ins))


def seq_fwd(name, tile_fn, params, ins, use_halo, out_specs, state_shape, nb, s, r, side=None):
    nt = s // r
    n_p, n_i, n_o = len(params), len(ins), len(out_specs)
    has_state = state_shape is not None

    def body(*refs):
        p_refs, i_refs = refs[:n_p], refs[n_p:n_p + n_i]
        h_refs = refs[n_p + n_i:n_p + 2 * n_i] if use_halo else ()
        k = n_p + n_i + len(h_refs)
        o_refs = refs[k:k + n_o]
        n = pl.program_id(1)
        state = None
        if has_state:
            sv_ref, st_ref = refs[k + n_o], refs[k + n_o + 1]

            @pl.when(n == 0)
            def _():
                st_ref[...] = jnp.zeros(state_shape, F32)

            state = st_ref[...]
            sv_ref[0, 0] = state
        pv = [p[...] for p in p_refs]
        iv = [i[...].astype(F32) for i in i_refs]
        hv = [jnp.where(n > 0, h[...].astype(F32), 0.0) for h in h_refs]
        new_state, ov = tile_fn(pv, state, iv, hv)
        for o_ref, o in zip(o_refs, ov):
            o_ref[...] = o.astype(o_ref.dtype)
        if has_state:
            st_ref[...] = new_state

    row = lambda b, n: (b * nt + n, 0)
    in_specs = [pl.BlockSpec(p.shape, lambda b, n: (0, 0)) for p in params]
    in_specs += [pl.BlockSpec((r, a.shape[1]), row) for a in ins]
    if use_halo:
        in_specs += [pl.BlockSpec((8, a.shape[1]), _halo_map(nt, r)) for a in ins]
    col = lambda b, n: (0, b * nt + n)
    out_shape, o_specs = [], []
    for w, dt, *transposed in out_specs:
        out_shape.append(jax.ShapeDtypeStruct((w, nb * s) if transposed else (nb * s, w), dt))
        o_specs.append(pl.BlockSpec((w, r), col) if transposed else pl.BlockSpec((r, w), row))
    scratch = []
    if has_state:
        out_shape.append(jax.ShapeDtypeStruct((nb, nt) + tuple(state_shape), F32))
        o_specs.append(pl.BlockSpec((1, 1) + tuple(state_shape), lambda b, n: (b, n, 0, 0)))
        scratch.append(pltpu.VMEM(tuple(state_shape), F32))
    args = list(params) + list(ins) + (list(ins) if use_halo else [])
    if side is not None:
        body, in_specs, o_specs, out_shape, scratch, args = _host_exchange(
            side, body, in_specs, o_specs, out_shape, scratch, args, (nb, nt))
    return pl.pallas_call(body, grid=(nb, nt), in_specs=in_specs, out_specs=o_specs, out_shape=out_shape,
                          scratch_shapes=scratch, compiler_params=_cparams(("arbitrary", "arbitrary")),
                          name=name)(*args)


def seq_bwd(name, tile_fn, params, ins, use_halo, states, douts, din_dtypes, state_shape, nb, s, r, side=None):
    nt = s // r
    n_p, n_i, n_o = len(params), len(ins), len(douts)
    has_state = state_shape is not None

    def body(*refs):
        p_refs, i_refs = refs[:n_p], refs[n_p:n_p + n_i]
        h_refs = refs[n_p + n_i:n_p + 2 * n_i] if use_halo else ()
        k = n_p + n_i + len(h_refs)
        sv_ref = None
        if has_state:
            sv_ref = refs[k]
            k += 1
        do_refs = refs[k:k + n_o]
        k += n_o
        di_refs, dp_refs = refs[k:k + n_i], refs[k + n_i:k + n_i + n_p]
        k += n_i + n_p
        dst_ref = None
        if has_state:
            dst_ref = refs[k]
            k += 1
        dh_refs = refs[k:k + len(h_refs)]
        b, nn = pl.program_id(0), pl.program_id(1)
        n = nt - 1 - nn

        @pl.when((b == 0) & (nn == 0))
        def _():
            for dp in dp_refs:
                dp[...] = jnp.zeros(dp.shape, F32)

        @pl.when(nn == 0)
        def _():
            if has_state:
                dst_ref[...] = jnp.zeros(state_shape, F32)
            for dh in dh_refs:
                dh[...] = jnp.zeros(dh.shape, F32)

        pv = [p[...] for p in p_refs]
        iv = [i[...].astype(F32) for i in i_refs]
        hv = [jnp.where(n > 0, h[...].astype(F32), 0.0) for h in h_refs]
        if has_state:
            f = lambda pv_, st_, iv_, hv_: tile_fn(pv_, st_, iv_, hv_)
            _, vjp = jax.vjp(f, pv, sv_ref[0, 0], iv, hv)
            dpv, dst, div, dhv = vjp((dst_ref[...], [d[...].astype(F32) for d in do_refs]))
            dst_ref[...] = dst
        else:
            f = lambda pv_, iv_, hv_: tile_fn(pv_, None, iv_, hv_)[1]
            _, vjp = jax.vjp(f, pv, iv, hv)
            dpv, div, dhv = vjp([d[...].astype(F32) for d in do_refs])
        for j, (di_ref, d) in enumerate(zip(di_refs, div)):
            if use_halo:
                d = jnp.concatenate([d[:r - 8], d[r - 8:] + dh_refs[j][...]], axis=0)
            di_ref[...] = d.astype(di_ref.dtype)
        for dh_ref, d in zip(dh_refs, dhv):
            dh_ref[...] = d
        for dp_ref, d in zip(dp_refs, dpv):
            dp_ref[...] += d

    row = lambda b, nn: (b * nt + nt - 1 - nn, 0)
    hmap = _halo_map(nt, r)
    in_specs = [pl.BlockSpec(p.shape, lambda b, nn: (0, 0)) for p in params]
    in_specs += [pl.BlockSpec((r, a.shape[1]), row) for a in ins]
    if use_halo:
        in_specs += [pl.BlockSpec((8, a.shape[1]), lambda b, nn: hmap(b, nt - 1 - nn)) for a in ins]
    args = list(params) + list(ins) + (list(ins) if use_halo else [])
    scratch = []
    if has_state:
        in_specs.append(pl.BlockSpec((1, 1) + tuple(state_shape), lambda b, nn: (b, nt - 1 - nn, 0, 0)))
        args.append(states)
        scratch.append(pltpu.VMEM(tuple(state_shape), F32))
    in_specs += [pl.BlockSpec((r, d.shape[1]), row) for d in douts]
    args += list(douts)
    if use_halo:
        scratch += [pltpu.VMEM((8, a.shape[1]), F32) for a in ins]
    out_shape = [jax.ShapeDtypeStruct(a.shape, dt) for a, dt in zip(ins, din_dtypes)]
    out_shape += [jax.ShapeDtypeStruct(p.shape, F32) for p in params]
    o_specs = [pl.BlockSpec((r, a.shape[1]), row) for a in ins]
    o_specs += [pl.BlockSpec(p.shape, lambda b, nn: (0, 0)) for p in params]
    if side is not None:
        body, in_specs, o_specs, out_shape, scratch, args = _host_exchange(
            side, body, in_specs, o_specs, out_shape, scratch, args, (nb, nt))
    res = pl.pallas_call(body, grid=(nb, nt), in_specs=in_specs, out_specs=o_specs, out_shape=out_shape,
                         scratch_shapes=scratch, compiler_params=_cparams(("arbitrary", "arbitrary")),
                         name=name)(*args)
    if side is not None:
        return res[:n_i], res[n_i:n_i + n_p], res[n_i + n_p:]
    return res[:n_i], res[n_i:]


def matmul(name, a, b, mode, out_dtype=F32, addend=None, resid=None, tm=512, tn=None, tk=None):
    if mode == "nn":
        (m, kd), (_, n) = a.shape, b.shape
    elif mode == "nt":
        (m, kd), (n, _) = a.shape, b.shape
    else:
        (kd, m), (_, n) = a.shape, b.shape
    tm, tn, tk = min(tm, m if resid is None else resid[2]), min(tn or n, n), min(tk or kd, kd)
    nk = kd // tk
    assert m % tm == 0 and n % tn == 0 and kd % tk == 0
    dims = {"nn": ((1,), (0,)), "nt": ((1,), (1,)), "tn": ((0,), (0,))}[mode]
    extra = [] if addend is None else [addend]
    if resid is not None:
        extra = [resid[0], resid[1]]
    n_in, n_out = 2 + len(extra), 1 if resid is None else 2

    def body(*refs):
        a_ref, b_ref = refs[0], refs[1]
        part = lax.dot_general(a_ref[...].astype(BF16), b_ref[...].astype(BF16), (dims, ((), ())),
                               preferred_element_type=F32)

        def finish(acc):
            if resid is not None:
                refs[n_in][...] = refs[2][...] + refs[3][0] * acc
                refs[n_in + 1][...] = acc.astype(BF16)
            else:
                if addend is not None:
                    acc = acc + refs[2][...]
                refs[n_in][...] = acc.astype(refs[n_in].dtype)

        if nk == 1:
            finish(part)
        else:
            acc_ref = refs[n_in + n_out]
            k = pl.program_id(2)

            @pl.when(k == 0)
            def _():
                acc_ref[...] = part

            @pl.when(k > 0)
            def _():
                acc_ref[...] += part

            @pl.when(k == nk - 1)
            def _():
                finish(acc_ref[...])

    if mode == "tn":
        a_spec = pl.BlockSpec((tk, tm), lambda j, i, k: (k, i))
    else:
        a_spec = pl.BlockSpec((tm, tk), lambda j, i, k: (i, k))
    if mode == "nt":
        b_spec = pl.BlockSpec((tn, tk), lambda j, i, k: (j, k))
    else:
        b_spec = pl.BlockSpec((tk, tn), lambda j, i, k: (k, j))
    o_spec = pl.BlockSpec((tm, tn), lambda j, i, k: (i, j))
    in_specs = [a_spec, b_spec] + [o_spec] * (len(extra) > 0)
    out_specs, out_shape = o_spec, jax.ShapeDtypeStruct((m, n), out_dtype)
    if resid is not None:
        rows = resid[2]
        assert rows % tm == 0
        in_specs.append(pl.BlockSpec((1, 1, tn), lambda j, i, k: (lax.div(i * tm, rows), 0, j)))
        out_specs, out_shape = [o_spec, o_spec], [out_shape, jax.ShapeDtypeStruct((m, n), BF16)]
    scratch = [pltpu.VMEM((tm, tn), F32)] if nk > 1 else []
    return pl.pallas_call(body, grid=(n // tn, m // tm, nk), in_specs=in_specs, out_specs=out_specs,
                          out_shape=out_shape, scratch_shapes=scratch,
                          compiler_params=_cparams(("parallel", "parallel", "arbitrary")), name=name)(a, b, *extra)


def _normmod(x, nw, shift, scale):
    return _rms(x, nw) * (1.0 + scale) + shift


def _row_specs(nb, s, tr, d):
    nt = s // tr
    row = pl.BlockSpec((tr, d), lambda b, i: (b * nt + i, 0))
    per_seq = pl.BlockSpec((1, 1, d), lambda b, i: (b, 0, 0))
    full = pl.BlockSpec((1, d), lambda b, i: (0, 0))
    return nt, row, per_seq, full


def normmod_fwd(name, x, nw, shift, scale, nb, s, tr=512):
    d, tr = x.shape[1], min(tr, s)
    nt, row, per_seq, full = _row_specs(nb, s, tr, d)

    def body(x_ref, nw_ref, sh_ref, sc_ref, h_ref, ht_ref):
        h = _normmod(x_ref[...], nw_ref[...], sh_ref[0], sc_ref[0])
        h_ref[...] = h.astype(h_ref.dtype)
        ht_ref[...] = h.T.astype(ht_ref.dtype)

    return pl.pallas_call(body, grid=(nb, nt), in_specs=[row, full, per_seq, per_seq],
                          out_specs=[row, pl.BlockSpec((d, tr), lambda b, i: (0, b * nt + i))],
                          out_shape=[jax.ShapeDtypeStruct(x.shape, BF16), jax.ShapeDtypeStruct(x.shape[::-1], BF16)],
                          compiler_params=_cparams(("parallel", "parallel")), name=name)(x, nw, shift, scale)


def normmod_bwd(name, x, nw, shift, scale, dh, dres, nb, s, tr=512, through=None):
    d, tr = x.shape[1], min(tr, s)
    nt, row, per_seq, full = _row_specs(nb, s, tr, d)

    def body(x_ref, nw_ref, sh_ref, sc_ref, dh_ref, dres_ref, *rest):
        dx_ref, dnw_ref, dsh_ref, dsc_ref = rest[-4:]
        b, i = pl.program_id(0), pl.program_id(1)

        @pl.when((b == 0) & (i == 0))
        def _():
            dnw_ref[...] = jnp.zeros(dnw_ref.shape, F32)

        @pl.when(i == 0)
        def _():
            dsh_ref[...] = jnp.zeros(dsh_ref.shape, F32)
            dsc_ref[...] = jnp.zeros(dsc_ref.shape, F32)

        dh_t = dh_ref[...] if through is None else _dg(dh_ref[...], rest[0][...], 1, 1)
        _, vjp = jax.vjp(_normmod, x_ref[...], nw_ref[...], sh_ref[0], sc_ref[0])
        dx, dnw, dsh, dsc = vjp(dh_t)
        dx_ref[...] = dres_ref[...] + dx
        dnw_ref[...] += dnw
        dsh_ref[0] += dsh
        dsc_ref[0] += dsc

    out_shape = [jax.ShapeDtypeStruct(x.shape, F32), jax.ShapeDtypeStruct((1, d), F32),
                 jax.ShapeDtypeStruct((nb, 1, d), F32), jax.ShapeDtypeStruct((nb, 1, d), F32)]
    in_specs, args = [row, full, per_seq, per_seq, row, row], [x, nw, shift, scale, dh, dres]
    if through is not None:
        in_specs[4] = pl.BlockSpec((tr, dh.shape[1]), lambda b, i: (b * nt + i, 0))
        in_specs.append(pl.BlockSpec(through.shape, lambda b, i: (0, 0)))
        args.append(through)
    return pl.pallas_call(body, grid=(nb, nt), in_specs=in_specs, out_specs=[row, full, per_seq, per_seq],
                          out_shape=out_shape, compiler_params=_cparams(("arbitrary", "arbitrary")),
                          name=name)(*args)


def _merge_specs(nb, s, tr, d, wbr):
    nt, row, per_seq, _ = _row_specs(nb, s, tr, d)
    o_spec = pl.BlockSpec((tr, wbr), lambda b, i: (b * nt + i, 0))
    g_spec = pl.BlockSpec((tr, 3 * d), lambda b, i: (b * nt + i, 0))
    wbr_spec = pl.BlockSpec((wbr, d), lambda b, i: (0, 0))
    wo_spec = pl.BlockSpec((d, d), lambda b, i: (0, 0))
    return nt, row, per_seq, o_spec, g_spec, wbr_spec, wo_spec


def merge_fwd(name, x, oa, ob, oc, pg, gate1, wa, wb, wc, wo, nb, s, tr=512):
    d, tr = x.shape[1], min(tr, s)
    nt, row, per_seq, o_spec, g_spec, wbr_spec, wo_spec = _merge_specs(nb, s, tr, d, oa.shape[1])

    def body(x_ref, oa_ref, ob_ref, oc_ref, pg_ref, g1_ref, wa_ref, wb_ref, wc_ref, wo_ref,
             x1_ref, ya_ref, yb_ref, yc_ref, mg_ref, mx_ref):
        g = _sigmoid(pg_ref[...])
        ys = [_dg(o[...], w_[...], 1, 0) for o, w_ in ((oa_ref, wa_ref), (ob_ref, wb_ref), (oc_ref, wc_ref))]
        merged = g[:, :d] * ys[0] + g[:, d:2 * d] * ys[1] + g[:, 2 * d:] * ys[2]
        mix = _dg(merged, wo_ref[...], 1, 0)
        x1_ref[...] = x_ref[...] + g1_ref[0] * mix
        for r, v in zip((ya_ref, yb_ref, yc_ref, mg_ref, mx_ref), ys + [merged, mix]):
            r[...] = v.astype(r.dtype)

    return pl.pallas_call(body, grid=(nb, nt),
                          in_specs=[row, o_spec, o_spec, o_spec, g_spec, per_seq, wbr_spec, wbr_spec, wbr_spec, wo_spec],
                          out_specs=[row] * 6,
                          out_shape=[jax.ShapeDtypeStruct(x.shape, F32)] + [jax.ShapeDtypeStruct(x.shape, BF16)] * 5,
                          compiler_params=_cparams(("parallel", "parallel")),
                          name=name)(x, oa, ob, oc, pg, gate1, wa, wb, wc, wo)


def merge_bwd(name, oa, ob, oc, pg, gate1, wa, wb, wc, wo, ys, merged, mix, dx1, nb, s, tr=256):
    d, tr = dx1.shape[1], min(tr, s)
    wbr = oa.shape[1]
    nt, row, per_seq, o_spec, g_spec, wbr_spec, wo_spec = _merge_specs(nb, s, tr, d, wbr)

    def body(oa_ref, ob_ref, oc_ref, pg_ref, g1_ref, wa_ref, wb_ref, wc_ref, wo_ref, ya_ref, yb_ref, yc_ref, mg_ref,
             mx_ref, dx_ref, doa_ref, dob_ref, doc_ref, dpg_ref, dg1_ref, dwa_ref, dwb_ref, dwc_ref, dwo_ref):
        b, i = pl.program_id(0), pl.program_id(1)

        @pl.when((b == 0) & (i == 0))
        def _():
            for r in (dwa_ref, dwb_ref, dwc_ref, dwo_ref):
                r[...] = jnp.zeros(r.shape, F32)

        @pl.when(i == 0)
        def _():
            dg1_ref[...] = jnp.zeros(dg1_ref.shape, F32)

        dx = dx_ref[...]
        dg1_ref[0] += jnp.sum(dx * mx_ref[...].astype(F32), axis=0, keepdims=True)
        dmix = g1_ref[0] * dx
        dmerged = _dg(dmix, wo_ref[...], 1, 1)
        dwo_ref[...] += _dg(mg_ref[...], dmix, 0, 0)
        g = _sigmoid(pg_ref[...])
        branches = ((oa_ref, wa_ref, ya_ref, doa_ref, dwa_ref), (ob_ref, wb_ref, yb_ref, dob_ref, dwb_ref),
                    (oc_ref, wc_ref, yc_ref, doc_ref, dwc_ref))
        dgs = []
        for k, (o_ref, w_ref, y_ref, do_ref, dw_ref) in enumerate(branches):
            gk = g[:, k * d:(k + 1) * d]
            dy = dmerged * gk
            dgs.append(dmerged * y_ref[...].astype(F32) * gk * (1.0 - gk))
            do_ref[...] = _dg(dy, w_ref[...], 1, 1)
            dw_ref[...] += _dg(o_ref[...], dy, 0, 0)
        dpg_ref[...] = jnp.concatenate(dgs, axis=1).astype(dpg_ref.dtype)

    t = nb * s
    out_shape = ([jax.ShapeDtypeStruct((t, wbr), F32)] * 3
                 + [jax.ShapeDtypeStruct((t, 3 * d), BF16), jax.ShapeDtypeStruct((nb, 1, d), F32)]
                 + [jax.ShapeDtypeStruct((wbr, d), F32)] * 3 + [jax.ShapeDtypeStruct((d, d), F32)])
    return pl.pallas_call(body, grid=(nb, nt),
                          in_specs=[o_spec, o_spec, o_spec, g_spec, per_seq, wbr_spec, wbr_spec, wbr_spec, wo_spec]
                          + [row] * 6,
                          out_specs=[o_spec, o_spec, o_spec, g_spec, per_seq, wbr_spec, wbr_spec, wbr_spec, wo_spec],
                          out_shape=out_shape, compiler_params=_cparams(("arbitrary", "arbitrary")),
                          name=name)(oa, ob, oc, pg, gate1, wa, wb, wc, wo, *ys, merged, mix, dx1)


def resid_down_bwd(name, dx, f, gate, w_down, nb, s, tr=512):
    d, tr = dx.shape[1], min(tr, s)
    hidden = w_down.shape[0]
    nt, row, per_seq, _ = _row_specs(nb, s, tr, d)

    def body(dx_ref, f_ref, g_ref, w_ref, df_ref, dg_ref, da_ref):
        @pl.when(pl.program_id(1) == 0)
        def _():
            dg_ref[...] = jnp.zeros(dg_ref.shape, F32)

        dx_t = dx_ref[...]
        df = (g_ref[0] * dx_t).astype(BF16)
        df_ref[...] = df
        dg_ref[0] += jnp.sum(dx_t * f_ref[...], axis=0, keepdims=True)
        da_ref[...] = _dgb(df, w_ref[...], 1, 1)

    return pl.pallas_call(body, grid=(nb, nt),
                          in_specs=[row, row, per_seq, pl.BlockSpec((hidden, d), lambda b, i: (0, 0))],
                          out_specs=[row, per_seq, pl.BlockSpec((tr, hidden), lambda b, i: (b * nt + i, 0))],
                          out_shape=[jax.ShapeDtypeStruct(dx.shape, BF16), jax.ShapeDtypeStruct((nb, 1, d), F32),
                                     jax.ShapeDtypeStruct((dx.shape[0], hidden), F32)],
                          compiler_params=_cparams(("arbitrary", "arbitrary")), name=name)(dx, f, gate, w_down)


def loss_head(name, x, fw, target, tr=512):
    t, d = x.shape
    row = pl.BlockSpec((tr, d), lambda i: (i, 0))
    full = pl.BlockSpec((1, d), lambda i: (0, 0))

    def loss_fn(xv, fwv, tv):
        err = _rms(xv, fwv) - tv
        return 0.5 * jnp.sum(jnp.mean(err * err, axis=-1))

    def body(x_ref, fw_ref, t_ref, dx_ref, l_ref, dfw_ref):
        @pl.when(pl.program_id(0) == 0)
        def _():
            l_ref[...] = jnp.zeros(l_ref.shape, F32)
            dfw_ref[...] = jnp.zeros(dfw_ref.shape, F32)

        val, (dx, dfw) = jax.value_and_grad(loss_fn, argnums=(0, 1))(x_ref[...], fw_ref[...], t_ref[...])
        dx_ref[...] = dx
        l_ref[...] += val
        dfw_ref[...] += dfw

    return pl.pallas_call(body, grid=(t // tr,), in_specs=[row, full, row],
                          out_specs=[row, pl.BlockSpec((1, 128), lambda i: (0, 0)), full],
                          out_shape=[jax.ShapeDtypeStruct((t, d), F32), jax.ShapeDtypeStruct((1, 128), F32),
                                     jax.ShapeDtypeStruct((1, d), F32)],
                          compiler_params=_cparams(("arbitrary",)), name=name)(x, fw, target)


def _row_tile(rows, cols, n_arrays):
    budget = 24 * 1024 * 1024 // (8 * cols * max(n_arrays, 1))
    tr = rows
    while tr > max(budget, 16) and tr % 2 == 0 and (tr // 2) % 16 == 0:
        tr //= 2
    return tr


def elementwise(name, fn, ins, out_dtypes):
    rows, cols = ins[0].shape
    tr = _row_tile(rows, cols, len(ins) + len(out_dtypes))
    spec = pl.BlockSpec((tr, cols), lambda i: (i, 0))
    n_in = len(ins)

    def body(*refs):
        outs = fn(*[r[...] for r in refs[:n_in]])
        for o_ref, o in zip(refs[n_in:], outs):
            o_ref[...] = o.astype(o_ref.dtype)

    return pl.pallas_call(body, grid=(rows // tr,), in_specs=[spec] * n_in, out_specs=[spec] * len(out_dtypes),
                          out_shape=[jax.ShapeDtypeStruct((rows, cols), dt) for dt in out_dtypes],
                          compiler_params=_cparams(("parallel",)), name=name)(*ins)


def _adamw(w, g, m, v):
    m = ADAM_B1 * m + (1.0 - ADAM_B1) * g
    v = ADAM_B2 * v + (1.0 - ADAM_B2) * (g * g)
    m_hat = m / (1.0 - ADAM_B1 ** ADAM_STEP)
    v_hat = v / (1.0 - ADAM_B2 ** ADAM_STEP)
    delta = -ADAM_LR * (m_hat / (jnp.sqrt(v_hat) + ADAM_EPS) + ADAM_WD * w)
    return delta, m, v


def adamw(name, w, g, m, v):
    return elementwise(name, _adamw, [w, g, m, v], [F32, F32, F32])


_ANY = pl.BlockSpec(memory_space=pl.ANY)


def _coords():
    return lax.axis_index("x"), lax.axis_index("y"), lax.axis_index("c")


def allgather8(name, arrays, halves):
    n = len(arrays)

    def body(*refs):
        in_refs, out_refs = refs[:n], refs[n:2 * n]
        send_sems, recv_sems, local_sems = refs[2 * n:]
        x, y, c = _coords()
        me, sibling = (x, y, c), (x, y, 1 - c)
        chips = [(1 - x, y), (x, 1 - y), (1 - x, 1 - y)]

        def blk(i, px, py, pc):
            return out_refs[i].at[4 * px + 2 * py + pc]

        def piece(i):
            return in_refs[i].at[c] if halves[i] else in_refs[i]

        def copy(i, k, block, to, src=None):
            return pltpu.make_async_remote_copy(
                src_ref=blk(i, *block) if src is None else src, dst_ref=blk(i, *block),
                send_sem=send_sems.at[7 * i + k], recv_sem=recv_sems.at[7 * i + k],
                device_id=to, device_id_type=MESH)

        mine = [pltpu.make_async_copy(piece(i), blk(i, *me), local_sems.at[i]) for i in range(n)]
        for cp in mine:
            cp.start()
        first = []
        for i in range(n):
            first.append(copy(i, 0, me, sibling, src=piece(i)))
            first += [copy(i, 1 + j, me, (*chip, c), src=piece(i)) for j, chip in enumerate(chips)]
        for cp in first:
            cp.start()
        passed = []
        for j, chip in enumerate(chips):
            for i in range(n):
                copy(i, 1 + j, (*chip, c), me).wait_recv()
                fwd = copy(i, 4 + j, (*chip, c), sibling)
                fwd.start()
                passed.append(fwd)
        for i in range(n):
            copy(i, 0, sibling, me).wait_recv()
            for j, chip in enumerate(chips):
                copy(i, 4 + j, (*chip, 1 - c), me).wait_recv()
        for cp in first + passed:
            cp.wait_send()
        for cp in mine:
            cp.wait()

    out_shape = []
    for a, hv in zip(arrays, halves):
        out_shape.append(jax.ShapeDtypeStruct((N_DEV,) + tuple(a.shape[1:] if hv else a.shape), a.dtype))
    return pl.pallas_call(
        body, in_specs=[_ANY] * n, out_specs=[_ANY] * n, out_shape=out_shape,
        scratch_shapes=[pltpu.SemaphoreType.DMA((7 * n,)), pltpu.SemaphoreType.DMA((7 * n,)),
                        pltpu.SemaphoreType.DMA((n,))],
        name=name)(*arrays)


def exchange(name, ins, out_shapes, plan, local_plan=()):
    n_in, n_out = len(ins), len(out_shapes)

    def body(*refs):
        copies = _exchange_copies(plan, local_plan, refs[:n_in], refs[n_in:n_in + n_out], *refs[n_in + n_out:])
        for cp in copies:
            cp.start()
        for cp in copies:
            cp.wait()

    return pl.pallas_call(
        body, in_specs=[_ANY] * n_in, out_specs=[_ANY] * n_out, out_shape=out_shapes,
        scratch_shapes=_exchange_sems(plan, local_plan), name=name)(*ins)


def sum_halves(name, gs, recv, c_arr, chip_arr):
    _, _, hr, cs = gs.shape
    tr = _row_tile(hr, cs, 4)

    def body(c_ref, chip_ref, g_ref, r_ref, qf_ref, qb_ref):
        q = g_ref[0, 0] + r_ref[0, 0]
        qb_ref[0] = q.astype(BF16)

        @pl.when(pl.program_id(1) == chip_ref[0])
        def _():
            qf_ref[...] = q

    grid_spec = pltpu.PrefetchScalarGridSpec(
        num_scalar_prefetch=2, grid=(hr // tr, N_CHIPS),
        in_specs=[pl.BlockSpec((1, 1, tr, cs), lambda i, j, c_ref, chip_ref: (j, c_ref[0], i, 0)),
                  pl.BlockSpec((1, 1, tr, cs), lambda i, j, c_ref, chip_ref: (j, 0, i, 0))],
        out_specs=[pl.BlockSpec((tr, cs), lambda i, j, c_ref, chip_ref: (i, 0)),
                   pl.BlockSpec((1, tr, cs), lambda i, j, c_ref, chip_ref: (j, i, 0))])
    return pl.pallas_call(body, grid_spec=grid_spec,
                          out_shape=[jax.ShapeDtypeStruct((hr, cs), F32),
                                     jax.ShapeDtypeStruct((N_CHIPS, hr, cs), BF16)],
                          compiler_params=_cparams(("parallel", "arbitrary")), name=name)(c_arr, chip_arr, gs, recv)


def sum_chips(name, qf, recv):
    (total,) = elementwise(name, lambda q, a, b, c: (q + a.astype(F32) + b.astype(F32) + c.astype(F32),),
                           [qf] + list(recv), [F32])
    return total


def adamw_halves(name, w, m, v, layer, g_mine, g_other, c_arr, prev=None):
    _, _, hr, cs = w.shape
    tr = _row_tile(hr, cs, 9)

    def body(c_ref, w_ref, m_ref, v_ref, gm_ref, go_ref, *rest):
        g_ref, d_ref, nm_ref, nv_ref = rest[-4:]
        g = jnp.where(pl.program_id(0) == c_ref[0], gm_ref[...], go_ref[...])
        delta, nm, nv = _adamw(w_ref[0, 0], g, m_ref[0, 0], v_ref[0, 0])
        g_ref[0, 0], d_ref[0, 0], nm_ref[0, 0], nv_ref[0, 0] = g, delta, nm, nv

    half = pl.BlockSpec((1, 1, tr, cs), lambda h, i, c_ref: (layer, h, i, 0))
    row = pl.BlockSpec((tr, cs), lambda h, i, c_ref: (i, 0))
    in_specs, args, aliases = [half, half, half, row, row], [c_arr, w, m, v, g_mine, g_other], {}
    if prev is not None:
        in_specs += [pl.BlockSpec(memory_space=pl.ANY)] * 4
        args += list(prev)
        aliases = {6 + k: k for k in range(4)}
    grid_spec = pltpu.PrefetchScalarGridSpec(num_scalar_prefetch=1, grid=(2, hr // tr),
                                             in_specs=in_specs, out_specs=[half] * 4)
    return pl.pallas_call(body, grid_spec=grid_spec, out_shape=[jax.ShapeDtypeStruct(w.shape, F32)] * 4,
                          input_output_aliases=aliases, compiler_params=_cparams(("parallel", "parallel")),
                          name=name)(*args)


def sum8(name, g):
    _, rows, cols = g.shape
    tr = _row_tile(rows, cols, 9)

    def body(*refs):
        acc = refs[0][0]
        for r in refs[1:N_DEV]:
            acc = acc + r[0]
        refs[N_DEV][...] = acc

    in_specs = [pl.BlockSpec((1, tr, cols), functools.partial(lambda k, i: (k, i, 0), k)) for k in range(N_DEV)]
    return pl.pallas_call(body, grid=(rows // tr,), in_specs=in_specs,
                          out_specs=pl.BlockSpec((tr, cols), lambda i: (i, 0)),
                          out_shape=jax.ShapeDtypeStruct((rows, cols), F32),
                          compiler_params=_cparams(("parallel",)), name=name)(*([g] * N_DEV))


_QKV, _AB, _GZ = (0, 1536), (1536, 1544), (1544, 2056)
_HG = (2056, 4104)
_SZ, _XBC, _DT = (4104, 4616), (4616, 5640), (5640, 5648)
_GATES = (5648, 8720)


def _split_w_in(w8):
    _, hr, cs = w8.shape
    w4 = w8.reshape(N_CHIPS, 2 * hr, cs)

    def cols(rng):
        lo, hi = rng
        return [w4[j][:, max(lo, j * cs) - j * cs:min(hi, (j + 1) * cs) - j * cs]
                for j in range(N_CHIPS) if max(lo, j * cs) < min(hi, (j + 1) * cs)]

    pad = [jnp.zeros((2 * hr, 120), w8.dtype)]
    return (jnp.concatenate(cols(_GATES), axis=1),
            jnp.concatenate(cols(_QKV) + cols(_GZ) + cols(_AB) + pad, axis=1),
            jnp.concatenate(cols(_HG), axis=1),
            jnp.concatenate(cols(_SZ) + cols(_XBC) + cols(_DT) + pad, axis=1))


def _stack_w_in(g, a, b, c):
    segments = [(a, 0, 1536), (a, 2048, 2056), (a, 1536, 2048), (b, 0, 2048), (c, 0, 512), (c, 512, 1536),
                (c, 1536, 1544), (g, 0, 3072)]
    cs = sum(s1 - s0 for _, s0, s1 in segments) // N_CHIPS
    chips = []
    for j in range(N_CHIPS):
        parts, off = [], 0
        for arr, s0, s1 in segments:
            u0, u1 = max(j * cs, off), min((j + 1) * cs, off + s1 - s0)
            if u0 < u1:
                parts.append(arr[:, s0 + u0 - off:s0 + u1 - off])
            off += s1 - s0
        chips.append(jnp.concatenate(parts, axis=1))
    rows = g.shape[0]
    return jnp.stack(chips).reshape(N_CHIPS, 2, rows // 2, cs)


def _rows8(rows, width):
    out = [jnp.pad(r.astype(F32), (0, width - r.shape[0])) for r in rows]
    out += [jnp.zeros((width,), F32)] * (8 - len(out))
    return jnp.stack(out)


class _Packer:
    def __init__(self):
        self.items, self.size = [], 0

    def add(self, name, shape):
        n = 1
        for d in shape:
            n *= d
        self.items.append((name, tuple(shape), self.size, n))
        self.size += n

    def rows(self):
        return -(-self.size // 8192) * 8

    def pack(self, values):
        flat = [values[name].astype(F32).reshape(-1) for name, _, _, _ in self.items]
        flat.append(jnp.zeros((self.rows() * 1024 - self.size,), F32))
        return jnp.concatenate(flat).reshape(self.rows(), 1024)

    def unpack(self, buf):
        flat = buf.reshape(-1)
        return {name: flat[off:off + n].reshape(shape) for name, shape, off, n in self.items}


def _stack_by_chip(g, axis):
    l, r, c = g.shape
    if axis == 2:
        cs = c // N_CHIPS
        g = g.reshape(l, r, N_CHIPS, cs).transpose(2, 0, 1, 3).reshape(N_CHIPS, 2, l * r // 2, cs)
    else:
        rs = r // N_CHIPS
        g = g.reshape(l, N_CHIPS, rs, c).transpose(1, 0, 2, 3).reshape(N_CHIPS, 2, l * rs // 2, c)
    return g


def _unstack_gathered(w8, l, axis):
    _, hr, cs = w8.shape
    w = w8.reshape(N_CHIPS, l, 2 * hr // l, cs)
    if axis == 2:
        return w.transpose(1, 2, 0, 3).reshape(l, 2 * hr // l, N_CHIPS * cs)
    return w.transpose(1, 0, 2, 3).reshape(l, N_CHIPS * 2 * hr // l, cs)


_BIG = (("w_in", 2), ("w_br_a", 2), ("w_br_b", 2), ("w_br_c", 2), ("w_out", 1), ("ffn_w_up", 2), ("ffn_w_down", 1))
_SMALL = ("b_ada", "norm1_w", "gdn_conv_w", "gdn_a_log", "gdn_dt_bias", "gdn_norm_w", "hgrn_lb_param",
          "hgrn_norm_w", "ssd_conv_w", "ssd_conv_b", "ssd_a_log", "ssd_dt_bias", "ssd_d", "ssd_norm_w",
          "norm2_w", "ffn_conv_w", "ffn_conv_b", "final_norm_w")
_WEIGHTS = ("w_ada", "b_ada", "norm1_w", "w_in", "gdn_conv_w", "gdn_a_log", "gdn_dt_bias", "gdn_norm_w",
            "hgrn_lb_param", "hgrn_norm_w", "ssd_conv_w", "ssd_conv_b", "ssd_a_log", "ssd_dt_bias", "ssd_d",
            "ssd_norm_w", "w_br_a", "w_br_b", "w_br_c", "w_out", "norm2_w", "ffn_w_up", "ffn_conv_w",
            "ffn_conv_b", "ffn_w_down", "final_norm_w")
_R_GDN, _R_HGRN, _R_SSD, _R_FFN = 256, 128, 256, 256


_MASKS = ((1, 0), (0, 1), (1, 1))


def _flip(k, x, y):
    return (1 - x if _MASKS[k][0] else x), (1 - y if _MASKS[k][1] else y)


def _rs_d2d(grads):
    plan = [functools.partial(lambda i, ins, outs, x, y, c: (ins[i].at[:, pl.ds(1 - c, 1)], outs[i], (x, y, 1 - c)), i)
            for i in range(len(grads))]
    return grads, [jax.ShapeDtypeStruct((N_CHIPS, 1) + g.shape[2:], F32) for g in grads], plan, (), ()


def _rs_ici(grads, recv, tag):
    n = len(grads)
    c_arr = lax.axis_index("c").astype(jnp.int32).reshape(1)
    chip_arr = (2 * lax.axis_index("x") + lax.axis_index("y")).astype(jnp.int32).reshape(1)
    q = [sum_halves("rs_sum_d2d%s_%d" % (tag, i), g, r, c_arr, chip_arr) for i, (g, r) in enumerate(zip(grads, recv))]
    qf, qb = [a for a, _ in q], [b for _, b in q]

    def ici(i, k, ins, outs, x, y, c):
        px, py = _flip(k, x, y)
        return ins[i].at[2 * px + py], outs[3 * i + k], (px, py, c)

    plan = [functools.partial(ici, i, k) for i in range(n) for k in range(3)]
    shapes = [jax.ShapeDtypeStruct(g.shape[2:], BF16) for g in grads for _ in range(3)]
    return qf, (qb, shapes, plan, (), ())


def _rs_finish(qf, res, tag):
    n = len(qf)
    red = [sum_chips("rs_sum_ici%s_%d" % (tag, i), qf[i], res[3 * i:3 * i + 3]) for i in range(n)]
    plan = [functools.partial(lambda i, ins, outs, x, y, c: (ins[i], outs[i], (x, y, 1 - c)), i) for i in range(n)]
    other = exchange("rs_swap" + tag, red, [jax.ShapeDtypeStruct(r.shape, F32) for r in red], plan)
    return red, other


def _join_sides(sides):
    ins, shapes, plan = [], [], []
    for s_ins, s_shapes, s_plan, _, _ in sides:
        def shifted(fn, i0, i1, o0, o1, in_refs, out_refs, x, y, c):
            return fn(in_refs[i0:i1], out_refs[o0:o1], x, y, c)

        i0, o0 = len(ins), len(shapes)
        plan += [functools.partial(shifted, fn, i0, i0 + len(s_ins), o0, o0 + len(s_shapes)) for fn in s_plan]
        ins += list(s_ins)
        shapes += list(s_shapes)
    return ins, shapes, plan, (), ()


def _gather_side(pieces):
    n = len(pieces)

    def send(i, k, ins, outs, x, y, c):
        px, py = _flip(k, x, y)
        return ins[i].at[c], outs[i].at[2 * (2 * x + y) + c], (px, py, c)

    def to_sibling(i, h, ins, outs, x, y, c):
        return ins[i].at[h], outs[i].at[2 * (2 * x + y) + h], (x, y, 1 - c)

    def pass_on(i, k, ins, outs, x, y, c):
        px, py = _flip(k, x, y)
        blk = 2 * (2 * px + py) + c
        return outs[i].at[blk], outs[i].at[blk], (x, y, 1 - c)

    plan = [functools.partial(send, i, k) for i in range(n) for k in range(3)]
    plan += [functools.partial(to_sibling, i, h) for i in range(n) for h in range(2)]
    then = [functools.partial(pass_on, i, k) for i in range(n) for k in range(3)]
    shapes = [jax.ShapeDtypeStruct((N_DEV,) + p.shape[1:], p.dtype) for p in pieces]
    return pieces, shapes, plan, (), then


def kernel(x, c, w_ada, b_ada, norm1_w, w_in, gdn_conv_w, gdn_a_log, gdn_dt_bias, gdn_norm_w, hgrn_lb_param, hgrn_norm_w, ssd_conv_w, ssd_conv_b, ssd_a_log, ssd_dt_bias, ssd_d, ssd_norm_w, w_br_a, w_br_b, w_br_c, w_out, norm2_w, ffn_w_up, ffn_conv_w, ffn_conv_b, ffn_w_down, final_norm_w, loss_target, m_w_ada, m_b_ada, m_norm1_w, m_w_in, m_gdn_conv_w, m_gdn_a_log, m_gdn_dt_bias, m_gdn_norm_w, m_hgrn_lb_param, m_hgrn_norm_w, m_ssd_conv_w, m_ssd_conv_b, m_ssd_a_log, m_ssd_dt_bias, m_ssd_d, m_ssd_norm_w, m_w_br_a, m_w_br_b, m_w_br_c, m_w_out, m_norm2_w, m_ffn_w_up, m_ffn_conv_w, m_ffn_conv_b, m_ffn_w_down, m_final_norm_w, v_w_ada, v_b_ada, v_norm1_w, v_w_in, v_gdn_conv_w, v_gdn_a_log, v_gdn_dt_bias, v_gdn_norm_w, v_hgrn_lb_param, v_hgrn_norm_w, v_ssd_conv_w, v_ssd_conv_b, v_ssd_a_log, v_ssd_dt_bias, v_ssd_d, v_ssd_norm_w, v_w_br_a, v_w_br_b, v_w_br_c, v_w_out, v_norm2_w, v_ffn_w_up, v_ffn_conv_w, v_ffn_conv_b, v_ffn_w_down, v_final_norm_w):
    loc = dict(locals())
    w = {k: loc[k] for k in _WEIGHTS}
    mom = {k: loc["m_" + k] for k in _WEIGHTS}
    var = {k: loc["v_" + k] for k in _WEIGHTS}
    nb, s, d = x.shape
    t = nb * s
    depth = w_ada.shape[0]
    chip = 2 * lax.axis_index("x") + lax.axis_index("y")
    dev = 2 * chip + lax.axis_index("c")
    x0 = x.reshape(t, d)
    target = loss_target.reshape(t, d)

    small_in = [c, gdn_conv_w.reshape(depth * 4, -1), ssd_conv_w.reshape(depth * 4, -1),
                ffn_conv_w.reshape(depth * 3, -1)]
    c_all, gcw, scw, fcw = allgather8("ag_small", small_in, [False] * 4)
    c_all = c_all.reshape(N_DEV * nb, d)

    def conv_full(g, taps):
        g = g[::2].reshape(N_CHIPS, depth, taps, -1)
        return g.transpose(1, 2, 0, 3).reshape(depth, taps, -1)

    gdn_cw, ssd_cw, ffn_cw = conv_full(gcw, 4), conv_full(scw, 4), conv_full(fcw, 3)

    axis_of = dict(_BIG)
    first_needed, later = ("w_in",), tuple(n for n, _ in _BIG if n != "w_in")
    wls = [dict() for _ in range(depth)]

    def pieces(keys):
        out = []
        for l, name in keys:
            a = w[name][l].astype(BF16)
            out.append(a.reshape(2, a.shape[0] // 2, a.shape[1]))
        return out

    def arrived(keys, bufs):
        for (l, name), g in zip(keys, bufs):
            if name == "w_in":
                wls[l]["w_g"], wls[l]["w_a"], wls[l]["w_b"], wls[l]["w_c"] = _split_w_in(g)
            else:
                wls[l][name] = _unstack_gathered(g, 1, axis_of[name])[0]

    keys0 = [(0, n) for n in first_needed]
    arrived(keys0, allgather8("ag_weights0", pieces(keys0), [True] * len(keys0)))

    (c_act,) = elementwise("silu_c", lambda v: (_silu(v),), [c_all], [F32])
    mod_cols = jnp.concatenate([matmul("ada_fwd%d" % l, c_act, w_ada[l], "nn") for l in range(depth)], axis=0)
    (mod8,) = allgather8("ag_mod", [mod_cols], [False])
    mod = mod8[::2].reshape(N_CHIPS, depth, N_DEV * nb, -1).transpose(1, 2, 0, 3).reshape(depth, N_DEV * nb, 6 * d)
    mod = lax.dynamic_slice_in_dim(mod, dev * nb, nb, axis=1) + b_ada[:, None, :]

    def mod_part(l, k):
        return mod[l, :, k * d:(k + 1) * d].reshape(nb, 1, d)

    saved = []
    xl = x0
    for l in range(depth):
        sfx = str(l)
        wl = wls[l]
        sv = {"x0": xl}
        shift1, scale1, gate1, shift2, scale2, gate2 = [mod_part(l, k) for k in range(6)]
        sv["mods"] = (shift1, scale1, gate1, shift2, scale2, gate2)
        h, h_t = normmod_fwd("norm1_fwd" + sfx, xl, norm1_w[l][None], shift1, scale1, nb, s)
        pg = matmul("proj_g" + sfx, h, wl["w_g"], "nn")
        pa = matmul("proj_a" + sfx, h, wl["w_a"], "nn")
        pb = matmul("proj_b" + sfx, h, wl["w_b"], "nn")
        pc = matmul("proj_c" + sfx, h, wl["w_c"], "nn")
        gdn_p = [_rows8(list(gdn_cw[l]), 1536), _rows8([gdn_a_log[l], gdn_dt_bias[l], gdn_norm_w[l]], 128)]
        hgrn_p = [_rows8(list(hgrn_lb_param), 512), _rows8([hgrn_norm_w[l]], 128)]
        ssd_p = [_rows8(list(ssd_cw[l]), 1024), _rows8([ssd_conv_b[l], ssd_norm_w[l]], 1024),
                 _rows8([ssd_a_log[l], ssd_dt_bias[l], ssd_d[l]], 128)]
        ffn_p = [_rows8(list(ffn_cw[l]) + [ffn_conv_b[l]], 2 * FFN_HIDDEN)]
        hgrn_fn = make_hgrn_tile(l, depth)
        keys = [(l, n) for n in later] + ([(l + 1, n) for n in first_needed] if l + 1 < depth else [])
        oa, st_a, *bufs = seq_fwd("gdn_fwd" + sfx, gdn_tile, gdn_p, [pa], True, [(512, BF16)], (512, 128), nb, s,
                                  _R_GDN, side=_gather_side(pieces(keys)))
        arrived(keys, bufs)
        ob, st_b = seq_fwd("hgrn_fwd" + sfx, hgrn_fn, hgrn_p, [pb], False, [(512, BF16)], (512, 128), nb, s, _R_HGRN)
        oc, st_c = seq_fwd("ssd_fwd" + sfx, ssd_tile, ssd_p, [pc], True, [(512, BF16)], (256, 256), nb, s, _R_SSD)
        x1, *merge_saved = merge_fwd("merge_fwd" + sfx, xl, oa, ob, oc, pg, gate1, wl["w_br_a"], wl["w_br_b"],
                                     wl["w_br_c"], wl["w_out"], nb, s)
        h2, h2_t = normmod_fwd("norm2_fwd" + sfx, x1, norm2_w[l][None], shift2, scale2, nb, s)
        u = matmul("ffn_up" + sfx, h2, wl["ffn_w_up"], "nn", tn=FFN_HIDDEN)
        act, act_t = seq_fwd("convglu_fwd" + sfx, convglu_tile_t, ffn_p, [u], True,
                             [(FFN_HIDDEN, BF16), (FFN_HIDDEN, BF16, "T")], None, nb, s, _R_FFN)
        xl, f = matmul("ffn_down" + sfx, act, wl["ffn_w_down"], "nn", resid=(x1, gate2, s))
        sv.update(merge_saved=merge_saved, h_t=h_t, h2_t=h2_t, act_t=act_t, pg=pg, pa=pa, pb=pb, pc=pc, oa=oa, ob=ob, oc=oc, st_a=st_a, st_b=st_b, st_c=st_c, x1=x1,
                  u=u, f=f, gdn_p=gdn_p, hgrn_p=hgrn_p, ssd_p=ssd_p, ffn_p=ffn_p, hgrn_fn=hgrn_fn)
        saved.append(sv)

    dx, loss_part, d_final = loss_head("loss_head", xl, final_norm_w[None], target)

    sg = {}
    dmod = [None] * depth
    d_lb = None
    reduced = {}
    early = [n for n, _ in _BIG if n != "w_in"]
    to_d2d = to_ici = None

    def finish(keys, qf, res, tag):
        for key, mine, other in zip(keys, *_rs_finish(qf, res, tag)):
            reduced[key] = (mine, other)
    for l in reversed(range(depth)):
        sfx = str(l)
        sv, wl = saved[l], wls[l]
        gfull = {}
        shift1, scale1, gate1, shift2, scale2, gate2 = sv["mods"]
        df, dgate2, dact = resid_down_bwd("ffn_down_dx" + sfx, dx, sv["f"], gate2, wl["ffn_w_down"], nb, s)
        gfull["ffn_w_down"] = matmul("ffn_down_dw" + sfx, sv["act_t"], df, "nn", tm=1408, tn=512, tk=4096)
        cg_args = ("convglu_bwd" + sfx, convglu_tile, sv["ffn_p"], [sv["u"]], True, None, [dact], [BF16], None, nb, s,
                   _R_FFN)
        if to_d2d is None:
            (du,), (dcw,) = seq_bwd(*cg_args)
        else:
            lp, stacked = to_d2d
            (du,), (dcw,), recv = seq_bwd(*cg_args, side=_rs_d2d(stacked))
            to_ici = ([(lp, "w_in")],) + _rs_ici(stacked, recv, "i%d" % lp) + ("i%d" % lp,)
        gfull["ffn_w_up"] = matmul("ffn_up_dw" + sfx, sv["h2_t"], du, "nn", tm=1024, tn=512, tk=4096)
        dx1, dnw2, dshift2, dscale2 = normmod_bwd("norm2_bwd" + sfx, sv["x1"], norm2_w[l][None], shift2, scale2, du, dx,
                                                  nb, s, tr=256, through=wl["ffn_w_up"])
        doa, dob, doc, dpg, dgate1, dwa, dwb, dwc, dwo = merge_bwd(
            "merge_bwd" + sfx, sv["oa"], sv["ob"], sv["oc"], sv["pg"], gate1, wl["w_br_a"], wl["w_br_b"],
            wl["w_br_c"], wl["w_out"], sv["merge_saved"][:3], *sv["merge_saved"][3:], dx1, nb, s)
        gfull["w_br_a"], gfull["w_br_b"], gfull["w_br_c"], gfull["w_out"] = dwa, dwb, dwc, dwo
        gdn_args = ("gdn_bwd" + sfx, gdn_tile, sv["gdn_p"], [sv["pa"]], True, sv["st_a"], [doa], [BF16], (512, 128),
                    nb, s, _R_GDN)
        stacked = [_stack_by_chip(gfull[n][None], axis_of[n]) for n in early]
        (dpc,), (dscw, dspv, dsps), recv = seq_bwd("ssd_bwd" + sfx, ssd_tile, sv["ssd_p"], [sv["pc"]], True, sv["st_c"],
                                                   [doc], [BF16], (256, 256), nb, s, _R_SSD, side=_rs_d2d(stacked))
        hosted = [([(l, n) for n in early],) + _rs_ici(stacked, recv, "e" + sfx) + ("e" + sfx,)]
        if to_ici is not None:
            hosted.append(to_ici)
        (dpa,), (dgcw, dgpk), res = seq_bwd(*gdn_args, side=_join_sides([h[2] for h in hosted]))
        for keys, qf, side, tag in hosted:
            finish(keys, qf, res[:len(side[1])], tag)
            res = res[len(side[1]):]
        (dpb,), (dlbp, dhnw) = seq_bwd("hgrn_bwd" + sfx, sv["hgrn_fn"], sv["hgrn_p"], [sv["pb"]], False, sv["st_b"],
                                       [dob], [BF16], (512, 128), nb, s, _R_HGRN)
        dh = matmul("proj_g_dx" + sfx, dpg, wl["w_g"], "nt")
        dh = matmul("proj_a_dx" + sfx, dpa, wl["w_a"], "nt", addend=dh)
        dh = matmul("proj_b_dx" + sfx, dpb, wl["w_b"], "nt", addend=dh)
        dh = matmul("proj_c_dx" + sfx, dpc, wl["w_c"], "nt", addend=dh)
        stacked_w_in = _stack_w_in(
            matmul("proj_g_dw" + sfx, sv["h_t"], dpg, "nn", tm=1024, tn=512, tk=4096),
            matmul("proj_a_dw" + sfx, sv["h_t"], dpa, "nn", tm=1024, tk=1024),
            matmul("proj_b_dw" + sfx, sv["h_t"], dpb, "nn", tm=1024, tn=512, tk=4096),
            matmul("proj_c_dw" + sfx, sv["h_t"], dpc, "nn", tm=1024, tk=1024))
        dx, dnw1, dshift1, dscale1 = normmod_bwd("norm1_bwd" + sfx, sv["x0"], norm1_w[l][None], shift1, scale1, dh, dx1,
                                                 nb, s)
        dmod[l] = jnp.concatenate([dshift1, dscale1, dgate1, dshift2, dscale2, dgate2], axis=-1).reshape(nb, 6 * d)
        d_lb = dlbp[:depth] if d_lb is None else d_lb + dlbp[:depth]
        sg[l] = dict(norm1_w=dnw1[0], norm2_w=dnw2[0], gdn_conv_w=dgcw[:4], gdn_a_log=dgpk[0, :4],
                     gdn_dt_bias=dgpk[1, :4], gdn_norm_w=dgpk[2], hgrn_norm_w=dhnw[0], ssd_conv_w=dscw[:4],
                     ssd_conv_b=dspv[0], ssd_norm_w=dspv[1, :512], ssd_a_log=dsps[0, :8], ssd_dt_bias=dsps[1, :8],
                     ssd_d=dsps[2, :8], ffn_conv_w=dcw[:3], ffn_conv_b=dcw[3])
        to_d2d = (l, [stacked_w_in])
    lp, stacked = to_d2d
    recv = exchange("rs_d2d_i%d" % lp, *_rs_d2d(stacked)[:4])
    qf, side = _rs_ici(stacked, recv, "i%d" % lp)
    finish([(lp, "w_in")], qf, exchange("rs_ici_i%d" % lp, *side[:4]), "i%d" % lp)
    grad_x = dx.reshape(nb, s, d)

    dmod = jnp.stack(dmod)
    (b_sum,) = elementwise("bias_rows", lambda *r: (functools.reduce(lambda p, q: p + q, r),),
                           [dmod[:, b].reshape(depth * 6, d) for b in range(nb)], [F32])
    per_layer = ("norm1_w", "norm2_w", "gdn_conv_w", "gdn_a_log", "gdn_dt_bias", "gdn_norm_w", "hgrn_norm_w",
                 "ssd_conv_w", "ssd_conv_b", "ssd_norm_w", "ssd_a_log", "ssd_dt_bias", "ssd_d", "ffn_conv_w", "ffn_conv_b")
    vals = {k: jnp.stack([sg[l][k] for l in range(depth)]) for k in per_layer}
    vals.update(loss=loss_part[0, :1], b_ada=b_sum.reshape(depth, 6 * d), hgrn_lb_param=d_lb, final_norm_w=d_final[0])
    gp = _Packer()
    for k, v in vals.items():
        gp.add(k, v.shape)
    packed8, dmod8 = allgather8("ag_grads", [gp.pack(vals), dmod.reshape(depth * nb, 6 * d)], [False, False])
    gs = gp.unpack(sum8("sum_small", packed8))
    loss = gs["loss"].reshape(())

    def my_cols(g):
        cs = g.shape[-1] // N_CHIPS
        return lax.dynamic_slice_in_dim(g, chip * cs, cs, axis=g.ndim - 1)

    for k in ("gdn_conv_w", "ssd_conv_w", "ffn_conv_w"):
        gs[k] = my_cols(gs[k])

    dmod_all = dmod8.reshape(N_DEV, depth, nb, 6 * d).transpose(1, 0, 2, 3).reshape(depth, N_DEV * nb, 6 * d)
    dmod_mine = lax.dynamic_slice_in_dim(dmod_all, chip * (6 * d // N_CHIPS), 6 * d // N_CHIPS, axis=2)
    g_w_ada = jnp.stack([matmul("ada_dw%d" % l, c_act, dmod_mine[l], "tn", tm=1024) for l in range(depth)])

    c_arr = lax.axis_index("c").astype(jnp.int32).reshape(1)
    grads, delta, new_m, new_v = {}, {}, {}, {}
    for i, (name, _) in enumerate(_BIG):
        shp = w[name].shape
        halves = lambda a: a.reshape((depth, 2) + reduced[(0, name)][0].shape)
        res = None
        for l in reversed(range(depth)):
            res = adamw_halves("adamw_%s%d" % (name, l), halves(w[name]), halves(mom[name]), halves(var[name]), l,
                               *reduced[(l, name)], c_arr, prev=res)
        grads[name], delta[name], new_m[name], new_v[name] = [r.reshape(shp) for r in res]
    grads["w_ada"] = g_w_ada
    for k in _SMALL:
        grads[k] = gs[k].reshape(w[k].shape)
    shp = w_ada.shape
    flat = lambda a: a.reshape(shp[0] * shp[1], shp[2])
    dl, nm, nv = adamw("adamw_w_ada", flat(w_ada), flat(g_w_ada), flat(m_w_ada), flat(v_w_ada))
    delta["w_ada"], new_m["w_ada"], new_v["w_ada"] = dl.reshape(shp), nm.reshape(shp), nv.reshape(shp)
    sp = _Packer()
    for k in _SMALL:
        sp.add(k, w[k].shape)
    dl, nm, nv = adamw("adamw_small", sp.pack(w), sp.pack(grads), sp.pack(mom), sp.pack(var))
    delta.update(sp.unpack(dl))
    new_m.update(sp.unpack(nm))
    new_v.update(sp.unpack(nv))

    return (loss, grad_x, *[grads[k] for k in _WEIGHTS], *[delta[k] for k in _WEIGHTS],
            *[new_m[k] for k in _WEIGHTS], *[new_v[k] for k in _WEIGHTS])
```
